```python
import math
import jax, jax.numpy as jnp
from jax import lax
import numpy as np

D_MODEL = 1024
BATCH = 8
SEQ = 8192
DEPTH = 1

N_MEM = 256
MAX_POS_OFFSET = 1024
BLOCK = 128
RMS_EPS = 1e-6
NEG_INF = -1e30

MLA_HEADS = 8
MLA_NOPE = 64
MLA_ROPE = 32
MLA_V = 64
MLA_QK_DIM = MLA_NOPE + MLA_ROPE
MLA_Q_RANK = 384
MLA_KV_RANK = 256
ROPE_THETA = 10000.0
MLA_WIDTH = MLA_HEADS * MLA_V

DIL_PAIRS = ((128, 1), (512, 4), (2048, 16))
DIL_GROUPS = 3
DIL_HEADS_PER_GROUP = 4
DIL_HEADS = DIL_GROUPS * DIL_HEADS_PER_GROUP
DIL_HEAD_DIM = 128
DIL_WIDTH = DIL_HEADS_PER_GROUP * DIL_HEAD_DIM

MEM_HEADS = 4
MEM_HEAD_DIM = 128
MEM_WIDTH = MEM_HEADS * MEM_HEAD_DIM

N_BRANCH = 3

D_FF = 2816
CONV_WIDTH = 3

OFF_Q = MLA_Q_RANK
OFF_KV = OFF_Q + MLA_KV_RANK
OFF_KR = OFF_KV + MLA_ROPE
OFF_DIL = OFF_KR + 3 * DIL_HEADS * DIL_HEAD_DIM
OFF_MEMQ = OFF_DIL + MEM_WIDTH
D_IN = OFF_MEMQ + N_BRANCH * D_MODEL

kernel_name = "hybrid_mla_dilated_memory_convffn"


def _rms_norm(x, g):
    xf = x.astype(jnp.float32)
    y = xf * lax.rsqrt(jnp.mean(xf * xf, axis=-1, keepdims=True) + RMS_EPS)
    return (y * g.astype(jnp.float32)).astype(x.dtype)


def _rope(t, positions):
    half = t.shape[-1] // 2
    inv_freq = ROPE_THETA ** (-jnp.arange(half, dtype=jnp.float32) / half)
    ang = positions.astype(jnp.float32)[:, :, None, None] * inv_freq
    cos, sin = jnp.cos(ang), jnp.sin(ang)
    t1 = t[..., :half].astype(jnp.float32)
    t2 = t[..., half:].astype(jnp.float32)
    return jnp.concatenate([t1 * cos - t2 * sin, t2 * cos + t1 * sin], axis=-1).astype(t.dtype)


def _alibi_slopes(n):
    return jnp.exp2(-8.0 * jnp.arange(1, n + 1, dtype=jnp.float32) / n)


def _mla(c_q, c_kv, k_rope, positions, q_norm, w_uq, kv_norm, w_ukv):
    B, S, _ = c_q.shape
    q = (_rms_norm(c_q, q_norm) @ w_uq).reshape(B, S, MLA_HEADS, MLA_QK_DIM)
    q = jnp.concatenate([q[..., :MLA_NOPE], _rope(q[..., MLA_NOPE:], positions)], axis=-1)
    kv = (_rms_norm(c_kv, kv_norm) @ w_ukv).reshape(B, S, MLA_HEADS, MLA_NOPE + MLA_V)
    k_pe = jnp.broadcast_to(_rope(k_rope[:, :, None, :], positions), (B, S, MLA_HEADS, MLA_ROPE))
    k = jnp.concatenate([kv[..., :MLA_NOPE], k_pe], axis=-1)
    v = kv[..., MLA_NOPE:]
    n_blk = S // BLOCK
    q_blocks = (q * MLA_QK_DIM ** -0.5).reshape(B, n_blk, BLOCK, MLA_HEADS, MLA_QK_DIM).transpose(1, 0, 2, 3, 4)
    key_idx = jnp.arange(S)

    def attend(args):
        q_blk, blk = args
        s = jnp.einsum('bqhd,bkhd->bhqk', q_blk, k).astype(jnp.float32)
        q_idx = blk * BLOCK + jnp.arange(BLOCK)
        s = jnp.where(key_idx[None, :] <= q_idx[:, None], s, NEG_INF)
        p = jax.nn.softmax(s, axis=-1).astype(v.dtype)
        return jnp.einsum('bhqk,bkhd->bqhd', p, v)

    o = lax.map(attend, (q_blocks, jnp.arange(n_blk)))
    return o.transpose(1, 0, 2, 3, 4).reshape(B, S, MLA_WIDTH)


def _dilated_group(q, k, v, window, dilation, slopes):
    B, S, H, dh = q.shape
    span = window // dilation
    L = S // dilation
    n_blk = -(-L // BLOCK)
    Lp = n_blk * BLOCK

    def to_blocks(t):
        t = t.reshape(B, L, dilation, H, dh).transpose(0, 2, 3, 1, 4)
        t = jnp.pad(t, ((0, 0), (0, 0), (0, 0), (0, Lp - L), (0, 0)))
        return t.reshape(B, dilation, H, n_blk, BLOCK, dh)

    def band(t):
        prev = jnp.pad(t, ((0, 0), (0, 0), (0, 0), (1, 0), (0, 0), (0, 0)))[:, :, :, :-1]
        return jnp.concatenate([prev, t], axis=4)

    qb = to_blocks(q) * dh ** -0.5
    kb = band(to_blocks(k))
    vb = band(to_blocks(v))
    s = jnp.einsum('bdhnqe,bdhnke->bdhnqk', qb, kb).astype(jnp.float32)
    dist = jnp.arange(BLOCK)[:, None] + BLOCK - jnp.arange(2 * BLOCK)[None, :]
    key_sub = jnp.arange(n_blk)[:, None, None] * BLOCK - BLOCK + jnp.arange(2 * BLOCK)[None, None, :]
    valid = (dist >= 0) & (dist <= span) & (key_sub >= 0)
    alibi = -slopes.astype(jnp.float32)[:, None, None, None] * (dist * dilation).astype(jnp.float32)
    s = jnp.where(valid, s + alibi, NEG_INF)
    m = jnp.max(s, axis=-1, keepdims=True)
    e = jnp.exp(s - m)
    den = jnp.sum(e, axis=-1, keepdims=True)
    o = jnp.einsum('bdhnqk,bdhnke->bdhnqe', (e / den).astype(v.dtype), vb)
    lse = (m + jnp.log(den))[..., 0]
    o = o.reshape(B, dilation, H, Lp, dh)[:, :, :, :L].transpose(0, 3, 1, 2, 4).reshape(B, S, H, dh)
    lse = lse.reshape(B, dilation, H, Lp)[..., :L].transpose(0, 3, 1, 2).reshape(B, S, H)
    return o, lse


def _dilated_mixture(dil_qkv):
    B, S, _ = dil_qkv.shape
    qkv = dil_qkv.reshape(B, S, 3, DIL_GROUPS, DIL_HEADS_PER_GROUP, DIL_HEAD_DIM)
    slopes = _alibi_slopes(DIL_HEADS).reshape(DIL_HEADS_PER_GROUP, DIL_GROUPS).T
    outs, lses = [], []
    for g, (window, dilation) in enumerate(DIL_PAIRS):
        o, lse = _dilated_group(qkv[:, :, 0, g], qkv[:, :, 1, g], qkv[:, :, 2, g], window, dilation, slopes[g])
        outs.append(o)
        lses.append(lse)
    w = jax.nn.softmax(jnp.stack(lses, axis=0), axis=0)
    o_stack = jnp.stack(outs, axis=0)
    o = jnp.sum(w[..., None].astype(o_stack.dtype) * o_stack, axis=0)
    return o.reshape(B, S, DIL_WIDTH)


def _mem_attention(q, mem, g_mem, w_mem_kv):
    B, S, _ = q.shape
    M = mem.shape[1]
    kv = (_rms_norm(mem, g_mem) @ w_mem_kv).reshape(B, M, 2, MEM_HEADS, MEM_HEAD_DIM)
    qh = q.reshape(B, S, MEM_HEADS, MEM_HEAD_DIM) * MEM_HEAD_DIM ** -0.5
    s = jnp.einsum('bshd,bmhd->bhsm', qh, kv[:, :, 0]).astype(jnp.float32)
    p = jax.nn.softmax(s, axis=-1).astype(q.dtype)
    return jnp.einsum('bhsm,bmhd->bshd', p, kv[:, :, 1]).reshape(B, S, MEM_WIDTH)


def _conv_ffn(h, w_up, conv_w, conv_b, w_down):
    S = h.shape[1]
    u = h @ w_up
    u_pad = jnp.pad(u, ((0, 0), (CONV_WIDTH - 1, 0), (0, 0)))
    z = conv_b + conv_w[0] * u_pad[:, 0:S]
    for j in range(1, CONV_WIDTH):
        z = z + conv_w[j] * u_pad[:, j:j + S]
    gate, val = z[..., :D_FF], z[..., D_FF:]
    return (jax.nn.silu(gate) * val) @ w_down


def _layer(x, mem, positions, g_pre_mix, w_in, b_gate, mla_q_norm, w_uq, mla_kv_norm, w_ukv, g_mem, w_mem_kv,
           w_br_mla, w_br_dil, w_br_mem, w_o, g_post_mix, g_pre_ffn, w_ffn_up, conv_w, conv_b, w_ffn_down, g_post_ffn):
    B, S, _ = x.shape
    h = _rms_norm(x, g_pre_mix)
    proj = h @ w_in
    y_mla = _mla(proj[..., :OFF_Q], proj[..., OFF_Q:OFF_KV], proj[..., OFF_KV:OFF_KR], positions,
                 mla_q_norm, w_uq, mla_kv_norm, w_ukv)
    y_dil = _dilated_mixture(proj[..., OFF_KR:OFF_DIL])
    y_mem = _mem_attention(proj[..., OFF_DIL:OFF_MEMQ], mem, g_mem, w_mem_kv)
    gates = jax.nn.sigmoid((proj[..., OFF_MEMQ:] + b_gate).astype(jnp.float32)).astype(x.dtype)
    gates = gates.reshape(B, S, N_BRANCH, D_MODEL)
    merged = (gates[:, :, 0] * (y_mla @ w_br_mla)
              + gates[:, :, 1] * (y_dil @ w_br_dil)
              + gates[:, :, 2] * (y_mem @ w_br_mem))
    x = x + _rms_norm(merged @ w_o, g_post_mix)
    h2 = _rms_norm(x, g_pre_ffn)
    x = x + _rms_norm(_conv_ffn(h2, w_ffn_up, conv_w, conv_b, w_ffn_down), g_post_ffn)
    return x


def _fwd_setup_inputs(seed: int = 0) -> dict:
    key = jax.random.key(seed)
    ks = jax.random.split(key, 24)
    f32 = jnp.float32

    def dense(k, fan_in, fan_out):
        return jax.random.normal(k, (DEPTH, fan_in, fan_out), f32) * fan_in ** -0.5

    def gain(k, n):
        return 1.0 + 0.05 * jax.random.normal(k, (DEPTH, n), f32)

    x = jax.random.normal(ks[0], (BATCH, SEQ, D_MODEL), f32)
    mem = jax.random.normal(ks[1], (BATCH, N_MEM, D_MODEL), f32)
    offset = jax.random.randint(ks[2], (BATCH, 1), 0, MAX_POS_OFFSET, dtype=jnp.int32)
    positions = (offset + jnp.arange(SEQ, dtype=jnp.int32)[None, :]).astype(jnp.int32)
    return {
        "x": x,
        "mem": mem,
        "positions": positions,
        "g_pre_mix": gain(ks[3], D_MODEL),
        "w_in": dense(ks[4], D_MODEL, D_IN),
        "b_gate": 0.1 * jax.random.normal(ks[5], (DEPTH, N_BRANCH * D_MODEL), f32),
        "mla_q_norm": gain(ks[6], MLA_Q_RANK),
        "w_uq": dense(ks[7], MLA_Q_RANK, MLA_HEADS * MLA_QK_DIM),
        "mla_kv_norm": gain(ks[8], MLA_KV_RANK),
        "w_ukv": dense(ks[9], MLA_KV_RANK, MLA_HEADS * (MLA_NOPE + MLA_V)),
        "g_mem": gain(ks[10], D_MODEL),
        "w_mem_kv": dense(ks[11], D_MODEL, 2 * MEM_WIDTH),
        "w_br_mla": dense(ks[12], MLA_WIDTH, D_MODEL),
        "w_br_dil": dense(ks[13], DIL_WIDTH, D_MODEL),
        "w_br_mem": dense(ks[14], MEM_WIDTH, D_MODEL),
        "w_o": dense(ks[15], D_MODEL, D_MODEL),
        "g_post_mix": gain(ks[16], D_MODEL),
        "g_pre_ffn": gain(ks[17], D_MODEL),
        "w_ffn_up": dense(ks[18], D_MODEL, 2 * D_FF),
        "conv_w": jax.random.normal(ks[19], (DEPTH, CONV_WIDTH, 2 * D_FF), f32) * CONV_WIDTH ** -0.5,
        "conv_b": 0.01 * jax.random.normal(ks[20], (DEPTH, 2 * D_FF), f32),
        "w_ffn_down": dense(ks[21], D_FF, D_MODEL),
        "g_post_ffn": gain(ks[22], D_MODEL),
    }


def _fwd_reference(x, mem, positions, g_pre_mix, w_in, b_gate, mla_q_norm, w_uq, mla_kv_norm, w_ukv, g_mem, w_mem_kv,
              w_br_mla, w_br_dil, w_br_mem, w_o, g_post_mix, g_pre_ffn, w_ffn_up, conv_w, conv_b, w_ffn_down,
              g_post_ffn):
    for l in range(DEPTH):
        x = _layer(x, mem, positions, g_pre_mix[l], w_in[l], b_gate[l], mla_q_norm[l], w_uq[l], mla_kv_norm[l],
                   w_ukv[l], g_mem[l], w_mem_kv[l], w_br_mla[l], w_br_dil[l], w_br_mem[l], w_o[l], g_post_mix[l],
                   g_pre_ffn[l], w_ffn_up[l], conv_w[l], conv_b[l], w_ffn_down[l], g_post_ffn[l])
    return x


import jax as _jax
import jax.numpy as _jnp

TWIN_FORMAT = 'train_step'
FWD_PARAMS = ['x', 'mem', 'positions', 'g_pre_mix', 'w_in', 'b_gate', 'mla_q_norm', 'w_uq', 'mla_kv_norm', 'w_ukv', 'g_mem', 'w_mem_kv', 'w_br_mla', 'w_br_dil', 'w_br_mem', 'w_o', 'g_post_mix', 'g_pre_ffn', 'w_ffn_up', 'conv_w', 'conv_b', 'w_ffn_down', 'g_post_ffn']
TWIN_WEIGHTS = ['g_pre_mix', 'w_in', 'b_gate', 'mla_q_norm', 'w_uq', 'mla_kv_norm', 'w_ukv', 'g_mem', 'w_mem_kv', 'w_br_mla', 'w_br_dil', 'w_br_mem', 'w_o', 'g_post_mix', 'g_pre_ffn', 'w_ffn_up', 'conv_w', 'conv_b', 'w_ffn_down', 'g_post_ffn']
TWIN_DIFF_INPUT = 'x'
TWIN_INPUTS = ['x', 'mem', 'positions', 'g_pre_mix', 'w_in', 'b_gate', 'mla_q_norm', 'w_uq', 'mla_kv_norm', 'w_ukv', 'g_mem', 'w_mem_kv', 'w_br_mla', 'w_br_dil', 'w_br_mem', 'w_o', 'g_post_mix', 'g_pre_ffn', 'w_ffn_up', 'conv_w', 'conv_b', 'w_ffn_down', 'g_post_ffn', 'loss_target', 'm_g_pre_mix', 'm_w_in', 'm_b_gate', 'm_mla_q_norm', 'm_w_uq', 'm_mla_kv_norm', 'm_w_ukv', 'm_g_mem', 'm_w_mem_kv', 'm_w_br_mla', 'm_w_br_dil', 'm_w_br_mem', 'm_w_o', 'm_g_post_mix', 'm_g_pre_ffn', 'm_w_ffn_up', 'm_conv_w', 'm_conv_b', 'm_w_ffn_down', 'm_g_post_ffn', 'v_g_pre_mix', 'v_w_in', 'v_b_gate', 'v_mla_q_norm', 'v_w_uq', 'v_mla_kv_norm', 'v_w_ukv', 'v_g_mem', 'v_w_mem_kv', 'v_w_br_mla', 'v_w_br_dil', 'v_w_br_mem', 'v_w_o', 'v_g_post_mix', 'v_g_pre_ffn', 'v_w_ffn_up', 'v_conv_w', 'v_conv_b', 'v_w_ffn_down', 'v_g_post_ffn']
TWIN_OUTPUTS = ['loss', 'grad_x', 'grad_g_pre_mix', 'grad_w_in', 'grad_b_gate', 'grad_mla_q_norm', 'grad_w_uq', 'grad_mla_kv_norm', 'grad_w_ukv', 'grad_g_mem', 'grad_w_mem_kv', 'grad_w_br_mla', 'grad_w_br_dil', 'grad_w_br_mem', 'grad_w_o', 'grad_g_post_mix', 'grad_g_pre_ffn', 'grad_w_ffn_up', 'grad_conv_w', 'grad_conv_b', 'grad_w_ffn_down', 'grad_g_post_ffn', 'delta_g_pre_mix', 'delta_w_in', 'delta_b_gate', 'delta_mla_q_norm', 'delta_w_uq', 'delta_mla_kv_norm', 'delta_w_ukv', 'delta_g_mem', 'delta_w_mem_kv', 'delta_w_br_mla', 'delta_w_br_dil', 'delta_w_br_mem', 'delta_w_o', 'delta_g_post_mix', 'delta_g_pre_ffn', 'delta_w_ffn_up', 'delta_conv_w', 'delta_conv_b', 'delta_w_ffn_down', 'delta_g_post_ffn', 'new_m_g_pre_mix', 'new_m_w_in', 'new_m_b_gate', 'new_m_mla_q_norm', 'new_m_w_uq', 'new_m_mla_kv_norm', 'new_m_w_ukv', 'new_m_g_mem', 'new_m_w_mem_kv', 'new_m_w_br_mla', 'new_m_w_br_dil', 'new_m_w_br_mem', 'new_m_w_o', 'new_m_g_post_mix', 'new_m_g_pre_ffn', 'new_m_w_ffn_up', 'new_m_conv_w', 'new_m_conv_b', 'new_m_w_ffn_down', 'new_m_g_post_ffn', 'new_v_g_pre_mix', 'new_v_w_in', 'new_v_b_gate', 'new_v_mla_q_norm', 'new_v_w_uq', 'new_v_mla_kv_norm', 'new_v_w_ukv', 'new_v_g_mem', 'new_v_w_mem_kv', 'new_v_w_br_mla', 'new_v_w_br_dil', 'new_v_w_br_mem', 'new_v_w_o', 'new_v_g_post_mix', 'new_v_g_pre_ffn', 'new_v_w_ffn_up', 'new_v_conv_w', 'new_v_conv_b', 'new_v_w_ffn_down', 'new_v_g_post_ffn']
TWIN_LEAF_KINDS = {'loss': 'loss', 'grad_x': 'grad_x', 'grad_g_pre_mix': 'grad_w', 'grad_w_in': 'grad_w', 'grad_b_gate': 'grad_w', 'grad_mla_q_norm': 'grad_w', 'grad_w_uq': 'grad_w', 'grad_mla_kv_norm': 'grad_w', 'grad_w_ukv': 'grad_w', 'grad_g_mem': 'grad_w', 'grad_w_mem_kv': 'grad_w', 'grad_w_br_mla': 'grad_w', 'grad_w_br_dil': 'grad_w', 'grad_w_br_mem': 'grad_w', 'grad_w_o': 'grad_w', 'grad_g_post_mix': 'grad_w', 'grad_g_pre_ffn': 'grad_w', 'grad_w_ffn_up': 'grad_w', 'grad_conv_w': 'grad_w', 'grad_conv_b': 'grad_w', 'grad_w_ffn_down': 'grad_w', 'grad_g_post_ffn': 'grad_w', 'delta_g_pre_mix': 'delta_w', 'delta_w_in': 'delta_w', 'delta_b_gate': 'delta_w', 'delta_mla_q_norm': 'delta_w', 'delta_w_uq': 'delta_w', 'delta_mla_kv_norm': 'delta_w', 'delta_w_ukv': 'delta_w', 'delta_g_mem': 'delta_w', 'delta_w_mem_kv': 'delta_w', 'delta_w_br_mla': 'delta_w', 'delta_w_br_dil': 'delta_w', 'delta_w_br_mem': 'delta_w', 'delta_w_o': 'delta_w', 'delta_g_post_mix': 'delta_w', 'delta_g_pre_ffn': 'delta_w', 'delta_w_ffn_up': 'delta_w', 'delta_conv_w': 'delta_w', 'delta_conv_b': 'delta_w', 'delta_w_ffn_down': 'delta_w', 'delta_g_post_ffn': 'delta_w', 'new_m_g_pre_mix': 'new_m', 'new_m_w_in': 'new_m', 'new_m_b_gate': 'new_m', 'new_m_mla_q_norm': 'new_m', 'new_m_w_uq': 'new_m', 'new_m_mla_kv_norm': 'new_m', 'new_m_w_ukv': 'new_m', 'new_m_g_mem': 'new_m', 'new_m_w_mem_kv': 'new_m', 'new_m_w_br_mla': 'new_m', 'new_m_w_br_dil': 'new_m', 'new_m_w_br_mem': 'new_m', 'new_m_w_o': 'new_m', 'new_m_g_post_mix': 'new_m', 'new_m_g_pre_ffn': 'new_m', 'new_m_w_ffn_up': 'new_m', 'new_m_conv_w': 'new_m', 'new_m_conv_b': 'new_m', 'new_m_w_ffn_down': 'new_m', 'new_m_g_post_ffn': 'new_m', 'new_v_g_pre_mix': 'new_v', 'new_v_w_in': 'new_v', 'new_v_b_gate': 'new_v', 'new_v_mla_q_norm': 'new_v', 'new_v_w_uq': 'new_v', 'new_v_mla_kv_norm': 'new_v', 'new_v_w_ukv': 'new_v', 'new_v_g_mem': 'new_v', 'new_v_w_mem_kv': 'new_v', 'new_v_w_br_mla': 'new_v', 'new_v_w_br_dil': 'new_v', 'new_v_w_br_mem': 'new_v', 'new_v_w_o': 'new_v', 'new_v_g_post_mix': 'new_v', 'new_v_g_pre_ffn': 'new_v', 'new_v_w_ffn_up': 'new_v', 'new_v_conv_w': 'new_v', 'new_v_conv_b': 'new_v', 'new_v_w_ffn_down': 'new_v', 'new_v_g_post_ffn': 'new_v'}


def _forward(args):
    return _fwd_reference(*[args[k] for k in FWD_PARAMS])


def _output_shape():
    def fwd():
        inp = _fwd_setup_inputs(0)
        return _fwd_reference(*[inp[k] for k in FWD_PARAMS])
    out = _jax.eval_shape(fwd)
    return out.shape, out.dtype

N_MICROBATCH = 1
ADAM_LR = 0.001
ADAM_B1 = 0.9
ADAM_B2 = 0.999
ADAM_EPS = 1e-08
ADAM_WD = 0.01
ADAM_STEP = 10
PER_EXAMPLE_BATCH_AXIS = {'x': 0, 'mem': 0, 'positions': 0, 'loss_target': 0}
SHARED_INPUTS = []
_WEIGHT_DTYPES = {'g_pre_mix': _jnp.float32, 'w_in': _jnp.float32, 'b_gate': _jnp.float32, 'mla_q_norm': _jnp.float32, 'w_uq': _jnp.float32, 'mla_kv_norm': _jnp.float32, 'w_ukv': _jnp.float32, 'g_mem': _jnp.float32, 'w_mem_kv': _jnp.float32, 'w_br_mla': _jnp.float32, 'w_br_dil': _jnp.float32, 'w_br_mem': _jnp.float32, 'w_o': _jnp.float32, 'g_post_mix': _jnp.float32, 'g_pre_ffn': _jnp.float32, 'w_ffn_up': _jnp.float32, 'conv_w': _jnp.float32, 'conv_b': _jnp.float32, 'w_ffn_down': _jnp.float32, 'g_post_ffn': _jnp.float32}
MOMENT_SCALE = {'g_pre_mix': 1.196375e+00, 'w_in': 3.799088e-01, 'b_gate': 2.613168e-01, 'mla_q_norm': 4.306314e-01, 'w_uq': 3.106307e-01, 'mla_kv_norm': 8.406295e-01, 'w_ukv': 3.898505e-01, 'g_mem': 4.073663e-01, 'w_mem_kv': 4.151835e-01, 'w_br_mla': 3.241253e-01, 'w_br_dil': 1.033860e+00, 'w_br_mem': 3.425361e-01, 'w_o': 1.074297e+00, 'g_post_mix': 6.400611e+01, 'g_pre_ffn': 7.796652e-01, 'w_ffn_up': 3.472905e-01, 'conv_w': 3.834207e-01, 'conv_b': 9.993223e-01, 'w_ffn_down': 6.958392e-01, 'g_post_ffn': 6.398434e+01}


def _to_microbatches(a, axis):
    t = _jnp.moveaxis(a, axis, 0)
    t = t.reshape((N_MICROBATCH, t.shape[0] // N_MICROBATCH) + t.shape[1:])
    return _jnp.moveaxis(t, 1, axis + 1)


def setup_inputs(seed: int = 0) -> dict:
    inp = _fwd_setup_inputs(seed)
    key = _jax.random.fold_in(_jax.random.key(seed), 7919)
    shape, _ = _output_shape()
    out = dict(inp)
    out["loss_target"] = _jax.random.normal(_jax.random.fold_in(key, 0), shape, _jnp.float32)
    for i, name in enumerate(TWIN_WEIGHTS):
        w = inp[name].astype(_jnp.float32)
        if MOMENT_SCALE is None:
            s = _jnp.sqrt(_jnp.mean(_jnp.square(w)) + 1e-30)
        else:
            s = MOMENT_SCALE[name]
        km, kv = _jax.random.split(_jax.random.fold_in(key, i + 1))
        out[name] = w
        out["m_" + name] = s * _jax.random.normal(km, w.shape, _jnp.float32)
        out["v_" + name] = (s * s) * _jax.random.uniform(kv, w.shape, _jnp.float32, 0.5, 1.5)
    if N_MICROBATCH > 1:
        for name, axis in PER_EXAMPLE_BATCH_AXIS.items():
            out[name] = _to_microbatches(out[name], axis)
    return {'x': out['x'], 'mem': out['mem'], 'positions': out['positions'], 'g_pre_mix': out['g_pre_mix'], 'w_in': out['w_in'], 'b_gate': out['b_gate'], 'mla_q_norm': out['mla_q_norm'], 'w_uq': out['w_uq'], 'mla_kv_norm': out['mla_kv_norm'], 'w_ukv': out['w_ukv'], 'g_mem': out['g_mem'], 'w_mem_kv': out['w_mem_kv'], 'w_br_mla': out['w_br_mla'], 'w_br_dil': out['w_br_dil'], 'w_br_mem': out['w_br_mem'], 'w_o': out['w_o'], 'g_post_mix': out['g_post_mix'], 'g_pre_ffn': out['g_pre_ffn'], 'w_ffn_up': out['w_ffn_up'], 'conv_w': out['conv_w'], 'conv_b': out['conv_b'], 'w_ffn_down': out['w_ffn_down'], 'g_post_ffn': out['g_post_ffn'], 'loss_target': out['loss_target'], 'm_g_pre_mix': out['m_g_pre_mix'], 'm_w_in': out['m_w_in'], 'm_b_gate': out['m_b_gate'], 'm_mla_q_norm': out['m_mla_q_norm'], 'm_w_uq': out['m_w_uq'], 'm_mla_kv_norm': out['m_mla_kv_norm'], 'm_w_ukv': out['m_w_ukv'], 'm_g_mem': out['m_g_mem'], 'm_w_mem_kv': out['m_w_mem_kv'], 'm_w_br_mla': out['m_w_br_mla'], 'm_w_br_dil': out['m_w_br_dil'], 'm_w_br_mem': out['m_w_br_mem'], 'm_w_o': out['m_w_o'], 'm_g_post_mix': out['m_g_post_mix'], 'm_g_pre_ffn': out['m_g_pre_ffn'], 'm_w_ffn_up': out['m_w_ffn_up'], 'm_conv_w': out['m_conv_w'], 'm_conv_b': out['m_conv_b'], 'm_w_ffn_down': out['m_w_ffn_down'], 'm_g_post_ffn': out['m_g_post_ffn'], 'v_g_pre_mix': out['v_g_pre_mix'], 'v_w_in': out['v_w_in'], 'v_b_gate': out['v_b_gate'], 'v_mla_q_norm': out['v_mla_q_norm'], 'v_w_uq': out['v_w_uq'], 'v_mla_kv_norm': out['v_mla_kv_norm'], 'v_w_ukv': out['v_w_ukv'], 'v_g_mem': out['v_g_mem'], 'v_w_mem_kv': out['v_w_mem_kv'], 'v_w_br_mla': out['v_w_br_mla'], 'v_w_br_dil': out['v_w_br_dil'], 'v_w_br_mem': out['v_w_br_mem'], 'v_w_o': out['v_w_o'], 'v_g_post_mix': out['v_g_post_mix'], 'v_g_pre_ffn': out['v_g_pre_ffn'], 'v_w_ffn_up': out['v_w_ffn_up'], 'v_conv_w': out['v_conv_w'], 'v_conv_b': out['v_conv_b'], 'v_w_ffn_down': out['v_w_ffn_down'], 'v_g_post_ffn': out['v_g_post_ffn']}


def _loss(weights, diff, rest, loss_target):
    with _jax.named_scope("forward"):
        args = {**rest, TWIN_DIFF_INPUT: diff, **{k: w.astype(_WEIGHT_DTYPES[k]) for k, w in weights.items()}}
        y = _forward(args)
    with _jax.named_scope("loss_head"):
        err = _jnp.square(y.astype(_jnp.float32) - loss_target)
        return 0.5 * _jnp.sum(_jnp.mean(err, axis=-1)) if err.ndim else 0.5 * err


def _adamw(w, g, m, v):
    m = ADAM_B1 * m + (1.0 - ADAM_B1) * g
    v = ADAM_B2 * v + (1.0 - ADAM_B2) * _jnp.square(g)
    m_hat = m / (1.0 - ADAM_B1 ** ADAM_STEP)
    v_hat = v / (1.0 - ADAM_B2 ** ADAM_STEP)
    delta = -ADAM_LR * (m_hat / (_jnp.sqrt(v_hat) + ADAM_EPS) + ADAM_WD * w)
    return delta, m, v


def reference(x, mem, positions, g_pre_mix, w_in, b_gate, mla_q_norm, w_uq, mla_kv_norm, w_ukv, g_mem, w_mem_kv, w_br_mla, w_br_dil, w_br_mem, w_o, g_post_mix, g_pre_ffn, w_ffn_up, conv_w, conv_b, w_ffn_down, g_post_ffn, loss_target, m_g_pre_mix, m_w_in, m_b_gate, m_mla_q_norm, m_w_uq, m_mla_kv_norm, m_w_ukv, m_g_mem, m_w_mem_kv, m_w_br_mla, m_w_br_dil, m_w_br_mem, m_w_o, m_g_post_mix, m_g_pre_ffn, m_w_ffn_up, m_conv_w, m_conv_b, m_w_ffn_down, m_g_post_ffn, v_g_pre_mix, v_w_in, v_b_gate, v_mla_q_norm, v_w_uq, v_mla_kv_norm, v_w_ukv, v_g_mem, v_w_mem_kv, v_w_br_mla, v_w_br_dil, v_w_br_mem, v_w_o, v_g_post_mix, v_g_pre_ffn, v_w_ffn_up, v_conv_w, v_conv_b, v_w_ffn_down, v_g_post_ffn):
    given = dict(x=x, mem=mem, positions=positions, g_pre_mix=g_pre_mix, w_in=w_in, b_gate=b_gate, mla_q_norm=mla_q_norm, w_uq=w_uq, mla_kv_norm=mla_kv_norm, w_ukv=w_ukv, g_mem=g_mem, w_mem_kv=w_mem_kv, w_br_mla=w_br_mla, w_br_dil=w_br_dil, w_br_mem=w_br_mem, w_o=w_o, g_post_mix=g_post_mix, g_pre_ffn=g_pre_ffn, w_ffn_up=w_ffn_up, conv_w=conv_w, conv_b=conv_b, w_ffn_down=w_ffn_down, g_post_ffn=g_post_ffn, loss_target=loss_target, m_g_pre_mix=m_g_pre_mix, m_w_in=m_w_in, m_b_gate=m_b_gate, m_mla_q_norm=m_mla_q_norm, m_w_uq=m_w_uq, m_mla_kv_norm=m_mla_kv_norm, m_w_ukv=m_w_ukv, m_g_mem=m_g_mem, m_w_mem_kv=m_w_mem_kv, m_w_br_mla=m_w_br_mla, m_w_br_dil=m_w_br_dil, m_w_br_mem=m_w_br_mem, m_w_o=m_w_o, m_g_post_mix=m_g_post_mix, m_g_pre_ffn=m_g_pre_ffn, m_w_ffn_up=m_w_ffn_up, m_conv_w=m_conv_w, m_conv_b=m_conv_b, m_w_ffn_down=m_w_ffn_down, m_g_post_ffn=m_g_post_ffn, v_g_pre_mix=v_g_pre_mix, v_w_in=v_w_in, v_b_gate=v_b_gate, v_mla_q_norm=v_mla_q_norm, v_w_uq=v_w_uq, v_mla_kv_norm=v_mla_kv_norm, v_w_ukv=v_w_ukv, v_g_mem=v_g_mem, v_w_mem_kv=v_w_mem_kv, v_w_br_mla=v_w_br_mla, v_w_br_dil=v_w_br_dil, v_w_br_mem=v_w_br_mem, v_w_o=v_w_o, v_g_post_mix=v_g_post_mix, v_g_pre_ffn=v_g_pre_ffn, v_w_ffn_up=v_w_ffn_up, v_conv_w=v_conv_w, v_conv_b=v_conv_b, v_w_ffn_down=v_w_ffn_down, v_g_post_ffn=v_g_post_ffn)
    weights = {n: given[n] for n in TWIN_WEIGHTS}
    shared = {n: given[n] for n in SHARED_INPUTS}
    per_example = {n: given[n] for n in ['x', 'mem', 'positions']}
    grad_fn = _jax.value_and_grad(_loss, argnums=(0, 1))

    def one_microbatch(ex, loss_target):
        ex = dict(ex)
        diff = ex.pop(TWIN_DIFF_INPUT)
        return grad_fn(weights, diff, {**shared, **ex}, loss_target)

    if N_MICROBATCH == 1:
        loss, (grad_w, grad_x) = one_microbatch(per_example, given["loss_target"])
    else:
        def body(carry, xs):
            loss_sum, grad_sum = carry
            l_k, (gw_k, gx_k) = one_microbatch(xs[0], xs[1])
            with _jax.named_scope("update"):
                return (loss_sum + l_k, _jax.tree.map(_jnp.add, grad_sum, gw_k)), gx_k

        init = (_jnp.zeros((), _jnp.float32), _jax.tree.map(_jnp.zeros_like, weights))
        (loss, grad_w), grad_x = _jax.lax.scan(body, init, (per_example, given["loss_target"]))
    with _jax.named_scope("update"):
        delta_w, new_m, new_v = {}, {}, {}
        for n in TWIN_WEIGHTS:
            delta_w[n], new_m[n], new_v[n] = _adamw(weights[n], grad_w[n], given["m_" + n], given["v_" + n])
    return (loss, grad_x, *[grad_w[n] for n in TWIN_WEIGHTS], *[delta_w[n] for n in TWIN_WEIGHTS],
            *[new_m[n] for n in TWIN_WEIGHTS], *[new_v[n] for n in TWIN_WEIGHTS])
```

```python
import functools
import math

import jax
import jax.numpy as jnp
from jax import lax
from jax.experimental import pallas as pl
from jax.experimental.pallas import tpu as pltpu

F32 = jnp.float32
BF16 = jnp.bfloat16

D_MODEL = 1024
N_MEM = 256
RMS_EPS = 1e-6
NEG_INF = -1e30
MLA_HEADS = 8
MLA_NOPE = 64
MLA_ROPE = 32
MLA_QK = 96
Q_RANK = 384
KV_RANK = 256
ROPE_THETA = 10000.0
DIL_PAIRS = ((128, 1), (512, 4), (2048, 16))
DIL_GROUPS = 3
DIL_HPG = 4
DIL_HEADS = 12
DIL_W = 512
MEM_HEADS = 4
MEM_W = 512
D_FF = 2816
OFF_Q = 384
OFF_KV = 640
OFF_KR = 672
OFF_DIL = 5280
OFF_MEMQ = 5792
D_IN = 8864
ADAM_LR = 0.001
ADAM_B1 = 0.9
ADAM_B2 = 0.999
ADAM_EPS = 1e-08
ADAM_WD = 0.01
ADAM_STEP = 10

LANES = 128
VMEM_LIMIT = 56 * 1024 * 1024

N_CHIPS = 4
ROW_TILE = 256

NN = (((1,), (0,)), ((), ()))
NT = (((1,), (1,)), ((), ()))
TN = (((0,), (0,)), ((), ()))


def _params(*sem):
    return pltpu.CompilerParams(dimension_semantics=sem, vmem_limit_bytes=VMEM_LIMIT)


def _full(shape):
    return pl.BlockSpec(shape, lambda *_: (0,) * len(shape))


def _matmul(a, b, *, mode="nn", out_dtype=F32, tm=512, tn=512, tk=None, add=None, name):
    if mode == "nn":
        (M, K), N = a.shape, b.shape[1]
    elif mode == "nt":
        (M, K), N = a.shape, b.shape[0]
    else:
        (K, M), N = a.shape, b.shape[1]
    tm, tn = min(tm, M), min(tn, N)
    tk = K if tk is None else min(tk, K)
    assert M % tm == 0 and N % tn == 0 and K % tk == 0, (name, M, N, K, tm, tn, tk)
    nk = K // tk
    dims = {"nn": NN, "nt": NT, "tn": TN}[mode]
    a_spec = pl.BlockSpec((tk, tm), lambda i, j, k: (k, i)) if mode == "tn" else pl.BlockSpec((tm, tk), lambda i, j, k: (i, k))
    b_spec = pl.BlockSpec((tn, tk), lambda i, j, k: (j, k)) if mode == "nt" else pl.BlockSpec((tk, tn), lambda i, j, k: (k, j))
    o_spec = pl.BlockSpec((tm, tn), lambda i, j, k: (i, j))
    has_add = add is not None

    def body(*refs):
        a_ref, b_ref = refs[0], refs[1]
        c_ref = refs[2] if has_add else None
        o_ref = refs[3] if has_add else refs[2]
        part = lax.dot_general(a_ref[...].astype(BF16), b_ref[...].astype(BF16), dims, preferred_element_type=F32)
        if nk == 1:
            if has_add:
                part = part + c_ref[...]
            o_ref[...] = part.astype(out_dtype)
        else:
            acc = refs[-1]
            k = pl.program_id(2)

            @pl.when(k == 0)
            def _():
                acc[...] = part

            @pl.when(k > 0)
            def _():
                acc[...] += part

            @pl.when(k == nk - 1)
            def _():
                r = acc[...]
                if has_add:
                    r = r + c_ref[...]
                o_ref[...] = r.astype(out_dtype)

    in_specs = [a_spec, b_spec] + ([o_spec] if has_add else [])
    args = (a, b) + ((add,) if has_add else ())
    return pl.pallas_call(
        body, name=name, grid=(M // tm, N // tn, nk), in_specs=in_specs, out_specs=o_spec,
        out_shape=jax.ShapeDtypeStruct((M, N), out_dtype),
        scratch_shapes=[pltpu.VMEM((tm, tn), F32)] if nk > 1 else [],
        compiler_params=_params("parallel", "parallel", "arbitrary"),
    )(*args)


def _rms_fwd_val(x, g):
    r = lax.rsqrt(jnp.mean(x * x, axis=-1, keepdims=True) + RMS_EPS)
    return (x * r) * g


def _rms_bwd_val(dy, x, g):
    r = lax.rsqrt(jnp.mean(x * x, axis=-1, keepdims=True) + RMS_EPS)
    xn = x * r
    gdy = g * dy
    dx = r * (gdy - xn * jnp.mean(gdy * xn, axis=-1, keepdims=True))
    return dx, dy * xn


def _rope_tables(pos, invf, inverse):
    ang = pos * invf
    cos, sin = jnp.cos(ang), jnp.sin(ang)
    lane = lax.broadcasted_iota(jnp.int32, ang.shape, 1)
    first = (lane >= MLA_NOPE) & (lane < MLA_NOPE + MLA_ROPE // 2)
    second = (lane >= MLA_NOPE + MLA_ROPE // 2) & (lane < MLA_QK)
    sgn = -1.0 if inverse else 1.0
    sa = jnp.where(first, -sgn * sin, 0.0)
    sb = jnp.where(second, sgn * sin, 0.0)
    return cos, sa, sb


def _rope_val(x, cos, sa, sb):
    half = MLA_ROPE // 2
    return x * cos + pltpu.roll(x, LANES - half, 1) * sa + pltpu.roll(x, half, 1) * sb


def _head_sum_bcast(v, n_heads):
    parts = []
    for h in range(n_heads):
        s = jnp.sum(v[:, h * LANES:(h + 1) * LANES], axis=1, keepdims=True)
        parts.append(jnp.broadcast_to(s, (v.shape[0], LANES)))
    return parts


def _row_spec(t, w):
    return pl.BlockSpec((t, w), lambda i: (i, 0))


def _acc_spec(w, rows=1):
    return pl.BlockSpec((rows, w), lambda i: (0, 0))


def _rmsnorm(x, g, out_dtype, name):
    S, W = x.shape
    t = min(ROW_TILE, S)

    def body(x_ref, g_ref, o_ref):
        o_ref[...] = _rms_fwd_val(x_ref[...], g_ref[...]).astype(out_dtype)

    return pl.pallas_call(body, name=name, grid=(S // t,), in_specs=[_row_spec(t, W), _acc_spec(W)],
                          out_specs=_row_spec(t, W), out_shape=jax.ShapeDtypeStruct((S, W), out_dtype),
                          compiler_params=_params("parallel"))(x, g)


def _mla_prep(proj_a, pos, invf, q_norm, kv_norm):
    S = proj_a.shape[0]
    t = ROW_TILE

    def body(a_ref, pos_ref, invf_ref, qn_ref, kvn_ref, cq_ref, ckv_ref, kpe_ref):
        a = a_ref[...]
        cq_ref[...] = _rms_fwd_val(a[:, 0:Q_RANK], qn_ref[...]).astype(BF16)
        ckv_ref[...] = _rms_fwd_val(a[:, Q_RANK + LANES:], kvn_ref[...]).astype(BF16)
        cos, sa, sb = _rope_tables(pos_ref[...], invf_ref[...], False)
        kpe_ref[...] = _rope_val(a[:, Q_RANK:Q_RANK + LANES], cos, sa, sb)

    return pl.pallas_call(
        body, name="mla_prep", grid=(S // t,),
        in_specs=[_row_spec(t, 768), _row_spec(t, 1), _acc_spec(LANES), _acc_spec(Q_RANK), _acc_spec(KV_RANK)],
        out_specs=[_row_spec(t, Q_RANK), _row_spec(t, KV_RANK), _row_spec(t, LANES)],
        out_shape=[jax.ShapeDtypeStruct((S, Q_RANK), BF16), jax.ShapeDtypeStruct((S, KV_RANK), BF16),
                   jax.ShapeDtypeStruct((S, LANES), F32)],
        compiler_params=_params("parallel"))(proj_a, pos, invf, q_norm, kv_norm)


def _qk_final(q_pre, k_pre, kpe, pos, invf):
    S, W = q_pre.shape
    t = ROW_TILE
    scale = MLA_QK ** -0.5

    def body(q_ref, k_ref, kpe_ref, pos_ref, invf_ref, qo_ref, ko_ref):
        cos, sa, sb = _rope_tables(pos_ref[...], invf_ref[...], False)
        kpe_v = kpe_ref[...]
        for h in range(MLA_HEADS):
            sl = slice(h * LANES, (h + 1) * LANES)
            qo_ref[:, sl] = (_rope_val(q_ref[:, sl], cos, sa, sb) * scale).astype(BF16)
            ko_ref[:, sl] = (k_ref[:, sl] + kpe_v).astype(BF16)

    return pl.pallas_call(
        body, name="qk_final", grid=(S // t,),
        in_specs=[_row_spec(t, W), _row_spec(t, W), _row_spec(t, LANES), _row_spec(t, 1), _acc_spec(LANES)],
        out_specs=[_row_spec(t, W), _row_spec(t, W)],
        out_shape=[jax.ShapeDtypeStruct((S, W), BF16)] * 2,
        compiler_params=_params("parallel"))(q_pre, k_pre, kpe, pos, invf)


def _mla_bwd_prep(dq, dk, pos, invf):
    S, W = dq.shape
    t = ROW_TILE
    scale = MLA_QK ** -0.5

    def body(dq_ref, dk_ref, pos_ref, invf_ref, dqp_ref, dkpe_ref):
        cos, sa, sb = _rope_tables(pos_ref[...], invf_ref[...], True)
        tot = jnp.zeros((t, LANES), F32)
        for h in range(MLA_HEADS):
            sl = slice(h * LANES, (h + 1) * LANES)
            dqp_ref[:, sl] = _rope_val(dq_ref[:, sl] * scale, cos, sa, sb).astype(BF16)
            tot = tot + dk_ref[:, sl]
        lane = lax.broadcasted_iota(jnp.int32, tot.shape, 1)
        tot = jnp.where((lane >= MLA_NOPE) & (lane < MLA_QK), tot, 0.0)
        dkpe_ref[...] = _rope_val(tot, cos, sa, sb)

    return pl.pallas_call(
        body, name="mla_bwd_prep", grid=(S // t,),
        in_specs=[_row_spec(t, W), _row_spec(t, W), _row_spec(t, 1), _acc_spec(LANES)],
        out_specs=[_row_spec(t, W), _row_spec(t, LANES)],
        out_shape=[jax.ShapeDtypeStruct((S, W), BF16), jax.ShapeDtypeStruct((S, LANES), F32)],
        compiler_params=_params("parallel"))(dq, dk, pos, invf)


def _mla_norm_bwd(dcq, dckv_a, dckv_b, dkpe, proj_a, q_norm, kv_norm):
    S = proj_a.shape[0]
    t = ROW_TILE

    def body(dcq_ref, da_ref, db_ref, dkpe_ref, a_ref, qn_ref, kvn_ref, o_ref, dqn_ref, dkvn_ref):
        i = pl.program_id(0)
        a = a_ref[...]
        dxq, gq = _rms_bwd_val(dcq_ref[...], a[:, 0:Q_RANK], qn_ref[...])
        dxkv, gkv = _rms_bwd_val(da_ref[...] + db_ref[...], a[:, Q_RANK + LANES:], kvn_ref[...])
        o_ref[:, 0:Q_RANK] = dxq.astype(BF16)
        o_ref[:, Q_RANK:Q_RANK + LANES] = dkpe_ref[...].astype(BF16)
        o_ref[:, Q_RANK + LANES:] = dxkv.astype(BF16)

        @pl.when(i == 0)
        def _():
            dqn_ref[...] = jnp.zeros_like(dqn_ref)
            dkvn_ref[...] = jnp.zeros_like(dkvn_ref)

        dqn_ref[...] += jnp.sum(gq, axis=0, keepdims=True)
        dkvn_ref[...] += jnp.sum(gkv, axis=0, keepdims=True)

    return pl.pallas_call(
        body, name="mla_norm_bwd", grid=(S // t,),
        in_specs=[_row_spec(t, Q_RANK), _row_spec(t, KV_RANK), _row_spec(t, KV_RANK), _row_spec(t, LANES),
                  _row_spec(t, 768), _acc_spec(Q_RANK), _acc_spec(KV_RANK)],
        out_specs=[_row_spec(t, 768), _acc_spec(Q_RANK), _acc_spec(KV_RANK)],
        out_shape=[jax.ShapeDtypeStruct((S, 768), BF16), jax.ShapeDtypeStruct((1, Q_RANK), F32),
                   jax.ShapeDtypeStruct((1, KV_RANK), F32)],
        compiler_params=_params("arbitrary"))(dcq, dckv_a, dckv_b, dkpe, proj_a, q_norm, kv_norm)


def _dil_mix(o_list, lse_list):
    S = o_list[0].shape[0]
    t = ROW_TILE

    def body(o0, o1, o2, l0, l1, l2, y_ref):
        ls = [l0[...], l1[...], l2[...]]
        m = jnp.maximum(jnp.maximum(ls[0], ls[1]), ls[2])
        es = [jnp.exp(l - m) for l in ls]
        den = es[0] + es[1] + es[2]
        y = (es[0] / den) * o0[...] + (es[1] / den) * o1[...] + (es[2] / den) * o2[...]
        y_ref[...] = y.astype(BF16)

    return pl.pallas_call(
        body, name="dil_mix", grid=(S // t,), in_specs=[_row_spec(t, DIL_W)] * 6, out_specs=_row_spec(t, DIL_W),
        out_shape=jax.ShapeDtypeStruct((S, DIL_W), BF16), compiler_params=_params("parallel"))(*o_list, *lse_list)


def _dil_mix_bwd(dy, o_list, lse_list):
    S = dy.shape[0]
    t = ROW_TILE

    def body(dy_ref, o0, o1, o2, l0, l1, l2, d0, d1, d2, e0, e1, e2):
        ls = [l0[...], l1[...], l2[...]]
        os_ = [o0[...], o1[...], o2[...]]
        m = jnp.maximum(jnp.maximum(ls[0], ls[1]), ls[2])
        es = [jnp.exp(l - m) for l in ls]
        den = es[0] + es[1] + es[2]
        ws = [e / den for e in es]
        dyv = dy_ref[...]
        y = ws[0] * os_[0] + ws[1] * os_[1] + ws[2] * os_[2]
        b = jnp.concatenate(_head_sum_bcast(dyv * y, DIL_HPG), axis=1)
        for w, d_ref, e_ref in zip(ws, (d0, d1, d2), (e0, e1, e2)):
            d_ref[...] = (w * dyv).astype(BF16)
            e_ref[...] = w * b

    return pl.pallas_call(
        body, name="dil_mix_bwd", grid=(S // t,), in_specs=[_row_spec(t, DIL_W)] * 7, out_specs=[_row_spec(t, DIL_W)] * 6,
        out_shape=[jax.ShapeDtypeStruct((S, DIL_W), BF16)] * 3 + [jax.ShapeDtypeStruct((S, DIL_W), F32)] * 3,
        compiler_params=_params("parallel"))(dy, *o_list, *lse_list)


def _delta(do, o, n_heads, name):
    S, W = do.shape
    t = ROW_TILE

    def body(do_ref, o_ref, d_ref):
        prod = do_ref[...].astype(F32) * o_ref[...].astype(F32)
        d_ref[...] = jnp.concatenate(_head_sum_bcast(prod, n_heads), axis=1)

    return pl.pallas_call(body, name=name, grid=(S // t,), in_specs=[_row_spec(t, W)] * 2, out_specs=_row_spec(t, W),
                          out_shape=jax.ShapeDtypeStruct((S, W), F32), compiler_params=_params("parallel"))(do, o)


def _merge(proj_g, b_gate, b_list):
    S = proj_g.shape[0]
    t = ROW_TILE

    def body(g_ref, b_ref, y0, y1, y2, o_ref):
        acc = jnp.zeros((t, D_MODEL), F32)
        for i, y in enumerate((y0, y1, y2)):
            sl = slice(i * D_MODEL, (i + 1) * D_MODEL)
            acc = acc + jax.nn.sigmoid(g_ref[:, sl] + b_ref[:, sl]) * y[...]
        o_ref[...] = acc.astype(BF16)

    return pl.pallas_call(
        body, name="merge", grid=(S // t,),
        in_specs=[_row_spec(t, 3 * D_MODEL), _acc_spec(3 * D_MODEL)] + [_row_spec(t, D_MODEL)] * 3,
        out_specs=_row_spec(t, D_MODEL), out_shape=jax.ShapeDtypeStruct((S, D_MODEL), BF16),
        compiler_params=_params("parallel"))(proj_g, b_gate, *b_list)


def _merge_bwd(dmerged, proj_g, b_gate, b_list):
    S = proj_g.shape[0]
    t = ROW_TILE

    def body(dm_ref, g_ref, b_ref, y0, y1, y2, d0, d1, d2, dz_ref, db_ref):
        i = pl.program_id(0)

        @pl.when(i == 0)
        def _():
            db_ref[...] = jnp.zeros_like(db_ref)

        dm = dm_ref[...]
        for k, (y, d_ref) in enumerate(zip((y0, y1, y2), (d0, d1, d2))):
            sl = slice(k * D_MODEL, (k + 1) * D_MODEL)
            s = jax.nn.sigmoid(g_ref[:, sl] + b_ref[:, sl])
            d_ref[...] = (s * dm).astype(BF16)
            dz = dm * y[...] * (s * (1.0 - s))
            dz_ref[:, sl] = dz.astype(BF16)
            db_ref[:, sl] += jnp.sum(dz, axis=0, keepdims=True)

    return pl.pallas_call(
        body, name="merge_bwd", grid=(S // t,),
        in_specs=[_row_spec(t, D_MODEL), _row_spec(t, 3 * D_MODEL), _acc_spec(3 * D_MODEL)] + [_row_spec(t, D_MODEL)] * 3,
        out_specs=[_row_spec(t, D_MODEL)] * 3 + [_row_spec(t, 3 * D_MODEL), _acc_spec(3 * D_MODEL)],
        out_shape=[jax.ShapeDtypeStruct((S, D_MODEL), BF16)] * 3
        + [jax.ShapeDtypeStruct((S, 3 * D_MODEL), BF16), jax.ShapeDtypeStruct((1, 3 * D_MODEL), F32)],
        compiler_params=_params("arbitrary"))(dmerged, proj_g, b_gate, *b_list)


def _norm2(o, x, g_post, g_pre):
    S = x.shape[0]
    t = ROW_TILE

    def body(o_ref, x_ref, gp_ref, gf_ref, x1_ref, h2_ref):
        x1 = x_ref[...] + _rms_fwd_val(o_ref[...], gp_ref[...])
        x1_ref[...] = x1
        h2_ref[...] = _rms_fwd_val(x1, gf_ref[...]).astype(BF16)

    return pl.pallas_call(
        body, name="norm2", grid=(S // t,),
        in_specs=[_row_spec(t, D_MODEL)] * 2 + [_acc_spec(D_MODEL)] * 2, out_specs=[_row_spec(t, D_MODEL)] * 2,
        out_shape=[jax.ShapeDtypeStruct((S, D_MODEL), F32), jax.ShapeDtypeStruct((S, D_MODEL), BF16)],
        compiler_params=_params("parallel"))(o, x, g_post, g_pre)


def _norm2_bwd(dx2, dh2, x1, o, g_pre, g_post):
    S = x1.shape[0]
    t = ROW_TILE

    def body(dx2_ref, dh2_ref, x1_ref, o_ref, gf_ref, gp_ref, dx1_ref, do_ref, dgf_ref, dgp_ref):
        i = pl.program_id(0)

        @pl.when(i == 0)
        def _():
            dgf_ref[...] = jnp.zeros_like(dgf_ref)
            dgp_ref[...] = jnp.zeros_like(dgp_ref)

        d1, gf = _rms_bwd_val(dh2_ref[...], x1_ref[...], gf_ref[...])
        dx1 = dx2_ref[...] + d1
        dx1_ref[...] = dx1
        do, gp = _rms_bwd_val(dx1, o_ref[...], gp_ref[...])
        do_ref[...] = do.astype(BF16)
        dgf_ref[...] += jnp.sum(gf, axis=0, keepdims=True)
        dgp_ref[...] += jnp.sum(gp, axis=0, keepdims=True)

    return pl.pallas_call(
        body, name="norm2_bwd", grid=(S // t,),
        in_specs=[_row_spec(t, D_MODEL)] * 4 + [_acc_spec(D_MODEL)] * 2,
        out_specs=[_row_spec(t, D_MODEL)] * 2 + [_acc_spec(D_MODEL)] * 2,
        out_shape=[jax.ShapeDtypeStruct((S, D_MODEL), F32), jax.ShapeDtypeStruct((S, D_MODEL), BF16),
                   jax.ShapeDtypeStruct((1, D_MODEL), F32), jax.ShapeDtypeStruct((1, D_MODEL), F32)],
        compiler_params=_params("arbitrary"))(dx2, dh2, x1, o, g_pre, g_post)


def _norm1_bwd(dx1, dh, x, g):
    S = x.shape[0]
    t = ROW_TILE

    def body(dx1_ref, dh_ref, x_ref, g_ref, dx_ref, dg_ref):
        i = pl.program_id(0)

        @pl.when(i == 0)
        def _():
            dg_ref[...] = jnp.zeros_like(dg_ref)

        d, gg = _rms_bwd_val(dh_ref[...], x_ref[...], g_ref[...])
        dx_ref[...] = dx1_ref[...] + d
        dg_ref[...] += jnp.sum(gg, axis=0, keepdims=True)

    return pl.pallas_call(
        body, name="norm1_bwd", grid=(S // t,),
        in_specs=[_row_spec(t, D_MODEL)] * 3 + [_acc_spec(D_MODEL)], out_specs=[_row_spec(t, D_MODEL), _acc_spec(D_MODEL)],
        out_shape=[jax.ShapeDtypeStruct((S, D_MODEL), F32), jax.ShapeDtypeStruct((1, D_MODEL), F32)],
        compiler_params=_params("arbitrary"))(dx1, dh, x, g)


def _gain_grad(dy, x, name):
    R, W = x.shape

    def body(dy_ref, x_ref, dg_ref):
        xv = x_ref[...]
        r = lax.rsqrt(jnp.mean(xv * xv, axis=-1, keepdims=True) + RMS_EPS)
        dg_ref[...] = jnp.sum(dy_ref[...] * (xv * r), axis=0, keepdims=True)

    return pl.pallas_call(body, name=name, grid=(1,), in_specs=[_full((R, W))] * 2, out_specs=_full((1, W)),
                          out_shape=jax.ShapeDtypeStruct((1, W), F32), compiler_params=_params("arbitrary"))(dy, x)


def _loss_head(f, x1, tgt, g):
    S = f.shape[0]
    t = ROW_TILE

    def body(f_ref, x1_ref, t_ref, g_ref, loss_ref, dx2_ref, df_ref, dg_ref):
        i = pl.program_id(0)

        @pl.when(i == 0)
        def _():
            loss_ref[...] = jnp.zeros_like(loss_ref)
            dg_ref[...] = jnp.zeros_like(dg_ref)

        fv, gv = f_ref[...], g_ref[...]
        err = x1_ref[...] + _rms_fwd_val(fv, gv) - t_ref[...]
        part = jnp.sum(jnp.mean(err * err, axis=-1, keepdims=True), axis=0, keepdims=True)
        loss_ref[...] += jnp.broadcast_to(0.5 * part, loss_ref.shape)
        dx2 = err * (1.0 / D_MODEL)
        dx2_ref[...] = dx2
        df, gg = _rms_bwd_val(dx2, fv, gv)
        df_ref[...] = df.astype(BF16)
        dg_ref[...] += jnp.sum(gg, axis=0, keepdims=True)

    return pl.pallas_call(
        body, name="loss_head", grid=(S // t,),
        in_specs=[_row_spec(t, D_MODEL)] * 3 + [_acc_spec(D_MODEL)],
        out_specs=[_acc_spec(LANES, 8), _row_spec(t, D_MODEL), _row_spec(t, D_MODEL), _acc_spec(D_MODEL)],
        out_shape=[jax.ShapeDtypeStruct((8, LANES), F32), jax.ShapeDtypeStruct((S, D_MODEL), F32),
                   jax.ShapeDtypeStruct((S, D_MODEL), BF16), jax.ShapeDtypeStruct((1, D_MODEL), F32)],
        compiler_params=_params("arbitrary"))(f, x1, tgt, g)


CONV_TC = 256
CONV_TT = 512
HALO = 8


def _shift_down(u, halo, first):
    row = lax.broadcasted_iota(jnp.int32, u.shape, 0)
    h6 = jnp.where(first, 0.0, halo[HALO - 2:HALO - 1, :])
    h7 = jnp.where(first, 0.0, halo[HALO - 1:HALO, :])
    s1 = jnp.where(row == 0, h7, pltpu.roll(u, 1, 0))
    s2 = jnp.where(row == 0, h6, jnp.where(row == 1, h7, pltpu.roll(u, 2, 0)))
    return s1, s2


def _conv_specs(tt, n_c, n_t, lead):
    def halo_row(i):
        return jnp.maximum(i * (tt // HALO) - 1, 0) if lead else jnp.minimum((i + 1) * (tt // HALO), n_t * (tt // HALO) - 1)
    return [
        pl.BlockSpec((tt, CONV_TC), lambda j, i: (i, j)),
        pl.BlockSpec((tt, CONV_TC), lambda j, i: (i, j + n_c)),
        pl.BlockSpec((HALO, CONV_TC), lambda j, i: (halo_row(i), j)),
        pl.BlockSpec((HALO, CONV_TC), lambda j, i: (halo_row(i), j + n_c)),
    ]


def _conv_z(ug, uv, hg, hv, w_g, w_v, b_g, b_v, first):
    g1, g2 = _shift_down(ug, hg, first)
    v1, v2 = _shift_down(uv, hv, first)
    zg = b_g + w_g[0:1, :] * g2
    zg = zg + w_g[1:2, :] * g1
    zg = zg + w_g[2:3, :] * ug
    zv = b_v + w_v[0:1, :] * v2
    zv = zv + w_v[1:2, :] * v1
    zv = zv + w_v[2:3, :] * uv
    return zg, zv, (g2, g1, ug), (v2, v1, uv)


def _conv_fwd(u, conv_w, conv_b):
    S = u.shape[0]
    tt = min(CONV_TT, S)
    n_c, n_t = D_FF // CONV_TC, S // tt
    wspec = [pl.BlockSpec((3, CONV_TC), lambda j, i: (0, j)), pl.BlockSpec((3, CONV_TC), lambda j, i: (0, j + n_c)),
             pl.BlockSpec((1, CONV_TC), lambda j, i: (0, j)), pl.BlockSpec((1, CONV_TC), lambda j, i: (0, j + n_c))]

    def body(ug_ref, uv_ref, hg_ref, hv_ref, wg_ref, wv_ref, bg_ref, bv_ref, a_ref):
        first = pl.program_id(1) == 0
        zg, zv, _, _ = _conv_z(ug_ref[...], uv_ref[...], hg_ref, hv_ref, wg_ref, wv_ref, bg_ref[...], bv_ref[...], first)
        a_ref[...] = (zg * jax.nn.sigmoid(zg) * zv).astype(BF16)

    return pl.pallas_call(
        body, name="conv_fwd", grid=(n_c, n_t), in_specs=_conv_specs(tt, n_c, n_t, True) + wspec,
        out_specs=pl.BlockSpec((tt, CONV_TC), lambda j, i: (i, j)), out_shape=jax.ShapeDtypeStruct((S, D_FF), BF16),
        compiler_params=_params("parallel", "parallel"))(u, u, u, u, conv_w, conv_w, conv_b, conv_b)


def _conv_bwd_dz(da, u, conv_w, conv_b):
    S = u.shape[0]
    tt = min(CONV_TT, S)
    n_c, n_t = D_FF // CONV_TC, S // tt
    wspec = [pl.BlockSpec((3, CONV_TC), lambda j, i: (0, j)), pl.BlockSpec((3, CONV_TC), lambda j, i: (0, j + n_c)),
             pl.BlockSpec((1, CONV_TC), lambda j, i: (0, j)), pl.BlockSpec((1, CONV_TC), lambda j, i: (0, j + n_c))]
    tile = pl.BlockSpec((tt, CONV_TC), lambda j, i: (i, j))
    tile_v = pl.BlockSpec((tt, CONV_TC), lambda j, i: (i, j + n_c))

    def body(da_ref, ug_ref, uv_ref, hg_ref, hv_ref, wg_ref, wv_ref, bg_ref, bv_ref,
             dzg_ref, dzv_ref, dwg_ref, dwv_ref, dbg_ref, dbv_ref):
        i = pl.program_id(1)
        zg, zv, gs, vs = _conv_z(ug_ref[...], uv_ref[...], hg_ref, hv_ref, wg_ref, wv_ref, bg_ref[...], bv_ref[...], i == 0)
        dav = da_ref[...]
        sg = jax.nn.sigmoid(zg)
        dzv = dav * (zg * sg)
        dzg = dav * zv * (sg * (1.0 + zg * (1.0 - sg)))
        dzg_ref[...] = dzg
        dzv_ref[...] = dzv

        @pl.when(i == 0)
        def _():
            for r in (dwg_ref, dwv_ref, dbg_ref, dbv_ref):
                r[...] = jnp.zeros_like(r)

        for k in range(3):
            dwg_ref[k:k + 1, :] += jnp.sum(dzg * gs[k], axis=0, keepdims=True)
            dwv_ref[k:k + 1, :] += jnp.sum(dzv * vs[k], axis=0, keepdims=True)
        dbg_ref[...] += jnp.sum(dzg, axis=0, keepdims=True)
        dbv_ref[...] += jnp.sum(dzv, axis=0, keepdims=True)

    outs = pl.pallas_call(
        body, name="conv_bwd_dz", grid=(n_c, n_t), in_specs=[tile] + _conv_specs(tt, n_c, n_t, True) + wspec,
        out_specs=[tile, tile] + [pl.BlockSpec((3, CONV_TC), lambda j, i: (0, j))] * 2 + [pl.BlockSpec((1, CONV_TC), lambda j, i: (0, j))] * 2,
        out_shape=[jax.ShapeDtypeStruct((S, D_FF), F32)] * 2 + [jax.ShapeDtypeStruct((3, D_FF), F32)] * 2
        + [jax.ShapeDtypeStruct((1, D_FF), F32)] * 2,
        compiler_params=_params("parallel", "arbitrary"))(da, u, u, u, u, conv_w, conv_w, conv_b, conv_b)
    dzg, dzv, dwg, dwv, dbg, dbv = outs
    return dzg, dzv, jnp.concatenate([dwg, dwv], axis=1), jnp.concatenate([dbg, dbv], axis=1)


def _conv_bwd_du(dzg, dzv, conv_w):
    S = dzg.shape[0]
    tt = min(CONV_TT, S)
    n_c, n_t = D_FF // CONV_TC, S // tt
    steps = tt // HALO

    def nxt(i):
        return jnp.minimum((i + 1) * steps, n_t * steps - 1)

    tile = pl.BlockSpec((tt, CONV_TC), lambda h, j, i: (i, j))
    halo = pl.BlockSpec((HALO, CONV_TC), lambda h, j, i: (nxt(i), j))
    wsp = pl.BlockSpec((3, CONV_TC), lambda h, j, i: (0, j + h * n_c))

    def body(zg_ref, zv_ref, hg_ref, hv_ref, w_ref, du_ref):
        h = pl.program_id(0)
        last = pl.program_id(2) == n_t - 1
        z = jnp.where(h == 0, zg_ref[...], zv_ref[...])
        hal = jnp.where(h == 0, hg_ref[...], hv_ref[...])
        row = lax.broadcasted_iota(jnp.int32, z.shape, 0)
        h0 = jnp.where(last, 0.0, hal[0:1, :])
        h1 = jnp.where(last, 0.0, hal[1:2, :])
        u1 = jnp.where(row == tt - 1, h0, pltpu.roll(z, tt - 1, 0))
        u2 = jnp.where(row == tt - 1, h1, jnp.where(row == tt - 2, h0, pltpu.roll(z, tt - 2, 0)))
        du = w_ref[2:3, :] * z + w_ref[1:2, :] * u1 + w_ref[0:1, :] * u2
        du_ref[...] = du.astype(BF16)

    return pl.pallas_call(
        body, name="conv_bwd_du", grid=(2, n_c, n_t), in_specs=[tile, tile, halo, halo, wsp],
        out_specs=pl.BlockSpec((tt, CONV_TC), lambda h, j, i: (i, j + h * n_c)),
        out_shape=jax.ShapeDtypeStruct((S, 2 * D_FF), BF16),
        compiler_params=_params("parallel", "parallel", "parallel"))(dzg, dzv, dzg, dzv, conv_w)


BAND = 128


def _scores(q, kj, mode, q0, k0, slope):
    s = lax.dot_general(q, kj, NT, preferred_element_type=F32)
    if mode == "full":
        return s
    dist = (q0 + lax.broadcasted_iota(jnp.int32, s.shape, 0)) - (k0 + lax.broadcasted_iota(jnp.int32, s.shape, 1))
    if mode == "band":
        s = s - slope * dist.astype(F32)
        return jnp.where((dist >= 0) & (dist <= BAND), s, NEG_INF)
    return jnp.where(dist >= 0, s, NEG_INF)


def _attn_fwd(q, k, v, *, n_heads, qcol, kcol, vcol, ocol, Lq, Lk, T, mode, scale, o_shape, o_dtype, slopes=None,
              slope_mul=1.0, name):
    tq = T
    nq = Lq // tq
    tk = Lk if mode == "full" else T

    def body(*refs):
        if mode == "band":
            sl_ref, q_ref, k_ref, v_ref, o_ref, l_ref = refs
        else:
            q_ref, k_ref, v_ref, o_ref, l_ref = refs
        hh, i = pl.program_id(0), pl.program_id(1)
        qv = q_ref[...]
        if scale is not None:
            qv = qv.astype(F32) * scale
        qv = qv.astype(BF16)
        slope = sl_ref[hh % DIL_HPG] * slope_mul if mode == "band" else None

        def step(j, carry, masked):
            m, l, acc = carry
            rows = pl.ds(pl.multiple_of(j * tk, tk), tk)
            kj = k_ref[rows, :].astype(BF16)
            vj = v_ref[rows, :].astype(BF16)
            s = _scores(qv, kj, mode if masked else "full", i * tq, j * tk, slope)
            m_new = jnp.maximum(m, jnp.max(s, axis=1, keepdims=True))
            alpha = jnp.exp(m - m_new)
            p = jnp.exp(s - m_new)
            l = alpha * l + jnp.sum(p, axis=1, keepdims=True)
            acc = alpha * acc + lax.dot_general(p.astype(BF16), vj, NN, preferred_element_type=F32)
            return m_new, l, acc

        carry = (jnp.full((tq, 1), NEG_INF, F32), jnp.zeros((tq, 1), F32), jnp.zeros((tq, LANES), F32))
        if mode == "causal":
            carry = lax.fori_loop(0, i, lambda j, c: step(j, c, False), carry)
            carry = step(i, carry, True)
        elif mode == "band":
            carry = lax.fori_loop(jnp.maximum(i - 1, 0), i + 1, lambda j, c: step(j, c, True), carry)
        else:
            carry = step(0, carry, False)
        m, l, acc = carry
        o_ref[...] = (acc / l).astype(o_dtype)
        l_ref[...] = jnp.broadcast_to(m + jnp.log(l), (tq, LANES))

    in_specs = [
        pl.BlockSpec((tq, LANES), lambda hh, i: (i, qcol(hh))),
        pl.BlockSpec((Lk, LANES), lambda hh, i: (0, kcol(hh))),
        pl.BlockSpec((Lk, LANES), lambda hh, i: (0, vcol(hh))),
    ]
    args = (q, k, v)
    if mode == "band":
        in_specs = [pl.BlockSpec(memory_space=pltpu.SMEM)] + in_specs
        args = (slopes,) + args
    o_spec = pl.BlockSpec((tq, LANES), lambda hh, i: (i, ocol(hh)))
    return pl.pallas_call(
        body, name=name, grid=(n_heads, nq), in_specs=in_specs, out_specs=[o_spec, o_spec],
        out_shape=[jax.ShapeDtypeStruct(o_shape, o_dtype), jax.ShapeDtypeStruct(o_shape, F32)],
        compiler_params=_params("parallel", "arbitrary"))(*args)


def _attn_bwd(q, k, v, do, lse, delta, *, n_heads, qcol, kcol, vcol, ocol, dkcol, Lq, Lk, T, mode, scale, dq_shape,
              dkv_shape, slopes=None, slope_mul=1.0, name):
    tk = Lk if mode == "full" else T
    tq = T
    nk, nq = Lk // tk, Lq // tq

    def body(*refs):
        if mode == "band":
            sl_ref, q_ref, k_ref, v_ref, do_ref, l_ref, d_ref, dq_ref, dk_ref, dv_ref = refs
        else:
            q_ref, k_ref, v_ref, do_ref, l_ref, d_ref, dq_ref, dk_ref, dv_ref = refs
        hh, j = pl.program_id(0), pl.program_id(1)
        slope = sl_ref[hh % DIL_HPG] * slope_mul if mode == "band" else None

        @pl.when(j == 0)
        def _():
            dq_ref[...] = jnp.zeros_like(dq_ref)

        kj = k_ref[...].astype(BF16)
        vj = v_ref[...].astype(BF16)

        def step(i, carry, masked):
            dk, dv = carry
            rows = pl.ds(pl.multiple_of(i * tq, tq), tq)
            qi = q_ref[rows, :]
            if scale is not None:
                qi = qi.astype(F32) * scale
            qi = qi.astype(BF16)
            doi = do_ref[rows, :].astype(BF16)
            s = _scores(qi, kj, mode if masked else "full", i * tq, j * tk, slope)
            p = jnp.exp(s - l_ref[rows, :][:, 0:1])
            dp = lax.dot_general(doi, vj, NT, preferred_element_type=F32)
            ds = (p * (dp - d_ref[rows, :][:, 0:1])).astype(BF16)
            dv = dv + lax.dot_general(p.astype(BF16), doi, TN, preferred_element_type=F32)
            dk = dk + lax.dot_general(ds, qi, TN, preferred_element_type=F32)
            dq_ref[rows, :] += lax.dot_general(ds, kj, NN, preferred_element_type=F32)
            return dk, dv

        carry = (jnp.zeros((tk, LANES), F32), jnp.zeros((tk, LANES), F32))
        if mode == "causal":
            carry = step(j, carry, True)
            carry = lax.fori_loop(j + 1, nq, lambda i, c: step(i, c, False), carry)
        elif mode == "band":
            carry = lax.fori_loop(j, jnp.minimum(j + 2, nq), lambda i, c: step(i, c, True), carry)
        else:
            carry = lax.fori_loop(0, nq, lambda i, c: step(i, c, False), carry)
        dk_ref[...], dv_ref[...] = carry

        if scale is not None:
            @pl.when(j == nk - 1)
            def _():
                dq_ref[...] = dq_ref[...] * scale

    def whole(col):
        return pl.BlockSpec((Lq, LANES), lambda hh, j: (0, col(hh)))

    in_specs = [whole(qcol), pl.BlockSpec((tk, LANES), lambda hh, j: (j, kcol(hh))),
                pl.BlockSpec((tk, LANES), lambda hh, j: (j, vcol(hh))), whole(ocol), whole(ocol), whole(ocol)]
    args = (q, k, v, do, lse, delta)
    if mode == "band":
        in_specs = [pl.BlockSpec(memory_space=pltpu.SMEM)] + in_specs
        args = (slopes,) + args
    dkv_spec = pl.BlockSpec((tk, LANES), lambda hh, j: (j, dkcol(hh)))
    return pl.pallas_call(
        body, name=name, grid=(n_heads, nk), in_specs=in_specs, out_specs=[whole(ocol), dkv_spec, dkv_spec],
        out_shape=[jax.ShapeDtypeStruct(dq_shape, F32), jax.ShapeDtypeStruct(dkv_shape, F32),
                   jax.ShapeDtypeStruct(dkv_shape, F32)],
        compiler_params=_params("parallel", "arbitrary"))(*args)


def _pad_heads(w, n_heads, width, axis):
    shp = w.shape
    new = shp[:axis] + (n_heads, width) + shp[axis + 1:]
    pad = [(0, 0)] * len(new)
    pad[axis + 1] = (0, LANES - width)
    out = jnp.pad(w.reshape(new), pad)
    return out.reshape(shp[:axis] + (n_heads * LANES,) + shp[axis + 1:])


def _unpad_heads(w, n_heads, width, axis):
    shp = w.shape
    new = shp[:axis] + (n_heads, LANES) + shp[axis + 1:]
    out = lax.slice_in_dim(w.reshape(new), 0, width, axis=axis + 1)
    return out.reshape(shp[:axis] + (n_heads * width,) + shp[axis + 1:])


def _alibi_slopes():
    s = jnp.exp2(-8.0 * jnp.arange(1, DIL_HEADS + 1, dtype=F32) / DIL_HEADS)
    return s.reshape(DIL_HPG, DIL_GROUPS).T


def _local_step(x, mem, positions, tgt, W):
    S = x.shape[0]
    pos = positions.reshape(S, 1).astype(F32)
    half = MLA_ROPE // 2
    inv_freq = ROPE_THETA ** (-jnp.arange(half, dtype=F32) / half)
    invf = jnp.zeros((1, LANES), F32).at[0, MLA_NOPE:MLA_NOPE + half].set(inv_freq).at[0, MLA_NOPE + half:MLA_QK].set(inv_freq)
    slopes = _alibi_slopes()

    w_in = W["w_in"]
    zc = lambda n: jnp.zeros((D_MODEL, n), BF16)
    w_a = jnp.concatenate([w_in[:, :OFF_Q], zc(MLA_NOPE), w_in[:, OFF_KV:OFF_KR], zc(LANES - MLA_QK), w_in[:, OFF_Q:OFF_KV]], axis=1)
    w_d, w_m, w_g = w_in[:, OFF_KR:OFF_DIL], w_in[:, OFF_DIL:OFF_MEMQ], w_in[:, OFF_MEMQ:]
    w_uq_p = _pad_heads(W["w_uq"], MLA_HEADS, MLA_QK, 1)
    ukv = W["w_ukv"].reshape(KV_RANK, MLA_HEADS, 2 * MLA_NOPE)
    w_uk_p = _pad_heads(ukv[:, :, :MLA_NOPE].reshape(KV_RANK, -1), MLA_HEADS, MLA_NOPE, 1)
    w_uv_p = _pad_heads(ukv[:, :, MLA_NOPE:].reshape(KV_RANK, -1), MLA_HEADS, MLA_NOPE, 1)
    w_br_mla_p = _pad_heads(W["w_br_mla"], MLA_HEADS, MLA_NOPE, 0)

    h = _rmsnorm(x, W["g_pre_mix"], BF16, "norm1")
    proj_a = _matmul(h, w_a, tn=768, name="proj_a")
    proj_d = _matmul(h, w_d, tn=768, name="proj_d")
    proj_m = _matmul(h, w_m, tn=512, name="proj_m")
    proj_g = _matmul(h, w_g, tn=768, name="proj_g")

    cq_n, ckv_n, kpe = _mla_prep(proj_a, pos, invf, W["mla_q_norm"], W["mla_kv_norm"])
    q_pre = _matmul(cq_n, w_uq_p, tn=1024, name="mla_q")
    k_pre = _matmul(ckv_n, w_uk_p, tn=1024, name="mla_k")
    v_mla = _matmul(ckv_n, w_uv_p, tn=1024, out_dtype=BF16, name="mla_v")
    q_mla, k_mla = _qk_final(q_pre, k_pre, kpe, pos, invf)
    ident = lambda hh: hh
    T_MLA = min(256, S)
    o_mla, lse_mla = _attn_fwd(q_mla, k_mla, v_mla, n_heads=MLA_HEADS, qcol=ident, kcol=ident, vcol=ident, ocol=ident,
                               Lq=S, Lk=S, T=T_MLA, mode="causal", scale=None, o_shape=(S, MLA_HEADS * LANES),
                               o_dtype=BF16, name="attn_mla_fwd")

    n_dil_cols = 3 * DIL_HEADS
    o_dil, lse_dil = [], []
    for g, (window, dil) in enumerate(DIL_PAIRS):
        L = S // dil
        view = proj_d.reshape(L, dil * 3 * DIL_HEADS * LANES)
        col = lambda part, g=g: (lambda hh: (hh // DIL_HPG) * n_dil_cols + (part * DIL_GROUPS + g) * DIL_HPG + hh % DIL_HPG)
        o, lse = _attn_fwd(view, view, view, n_heads=dil * DIL_HPG, qcol=col(0), kcol=col(1), vcol=col(2), ocol=ident,
                           Lq=L, Lk=L, T=BAND, mode="band", scale=LANES ** -0.5, o_shape=(L, dil * DIL_W), o_dtype=F32,
                           slopes=slopes[g], slope_mul=float(dil), name=f"attn_dil{g}_fwd")
        o_dil.append(o.reshape(S, DIL_W))
        lse_dil.append(lse.reshape(S, DIL_W))
    y_dil = _dil_mix(o_dil, lse_dil)

    mem_n = _rmsnorm(mem, W["g_mem"], BF16, "mem_norm")
    kv_mem = _matmul(mem_n, W["w_mem_kv"], tm=256, tn=512, name="mem_kv")
    T_MEM = min(512, S)
    o_mem, lse_mem = _attn_fwd(proj_m, kv_mem, kv_mem, n_heads=MEM_HEADS, qcol=ident, kcol=ident, vcol=lambda hh: hh + MEM_HEADS,
                               ocol=ident, Lq=S, Lk=N_MEM, T=T_MEM, mode="full", scale=LANES ** -0.5, o_shape=(S, MEM_W),
                               o_dtype=BF16, name="attn_mem_fwd")

    b_mla = _matmul(o_mla, w_br_mla_p, name="br_mla")
    b_dil = _matmul(y_dil, W["w_br_dil"], name="br_dil")
    b_mem = _matmul(o_mem, W["w_br_mem"], name="br_mem")
    merged = _merge(proj_g, W["b_gate"], [b_mla, b_dil, b_mem])
    o_proj = _matmul(merged, W["w_o"], name="o_proj")
    x1, h2 = _norm2(o_proj, x, W["g_post_mix"], W["g_pre_ffn"])

    u = _matmul(h2, W["w_ffn_up"], tn=512, name="ffn_up")
    act = _conv_fwd(u, W["conv_w"], W["conv_b"])
    f = _matmul(act, W["w_ffn_down"], tk=1408, name="ffn_down")
    loss8, dx2, df, dg_post_ffn = _loss_head(f, x1, tgt, W["g_post_ffn"])
    loss = loss8[0, 0]

    G = {"g_post_ffn": dg_post_ffn}
    d_act = _matmul(df, W["w_ffn_down"], mode="nt", tn=1408, name="d_act")
    G["w_ffn_down"] = _matmul(act, df, mode="tn", tm=1408, tk=512, name="dw_ffn_down")
    dzg, dzv, G["conv_w"], G["conv_b"] = _conv_bwd_dz(d_act, u, W["conv_w"], W["conv_b"])
    du = _conv_bwd_du(dzg, dzv, W["conv_w"])
    dh2 = _matmul(du, W["w_ffn_up"], mode="nt", tk=1408, name="d_h2")
    G["w_ffn_up"] = _matmul(h2, du, mode="tn", tk=512, name="dw_ffn_up")
    dx1, do_proj, G["g_pre_ffn"], G["g_post_mix"] = _norm2_bwd(dx2, dh2, x1, o_proj, W["g_pre_ffn"], W["g_post_mix"])
    dmerged = _matmul(do_proj, W["w_o"], mode="nt", name="d_merged")
    G["w_o"] = _matmul(merged, do_proj, mode="tn", tk=512, name="dw_o")
    db_mla, db_dil, db_mem, dproj_g, G["b_gate"] = _merge_bwd(dmerged, proj_g, W["b_gate"], [b_mla, b_dil, b_mem])

    dy_mem = _matmul(db_mem, W["w_br_mem"], mode="nt", name="d_y_mem")
    G["w_br_mem"] = _matmul(o_mem, db_mem, mode="tn", tk=512, name="dw_br_mem")
    delta_mem = _delta(dy_mem, o_mem, MEM_HEADS, "delta_mem")
    dq_mem, dk_mem, dv_mem = _attn_bwd(
        proj_m, kv_mem, kv_mem, dy_mem, lse_mem, delta_mem, n_heads=MEM_HEADS, qcol=ident, kcol=ident,
        vcol=lambda hh: hh + MEM_HEADS, ocol=ident, dkcol=ident, Lq=S, Lk=N_MEM, T=T_MEM, mode="full", scale=LANES ** -0.5,
        dq_shape=(S, MEM_W), dkv_shape=(N_MEM, MEM_W), name="attn_mem_bwd")
    dkv_mem = jnp.concatenate([dk_mem, dv_mem], axis=1)
    G["w_mem_kv"] = _matmul(mem_n, dkv_mem, mode="tn", name="dw_mem_kv")
    dmem_n = _matmul(dkv_mem, W["w_mem_kv"], mode="nt", tm=256, name="d_mem_n")
    G["g_mem"] = _gain_grad(dmem_n, mem, "dg_mem")

    dy_dil = _matmul(db_dil, W["w_br_dil"], mode="nt", name="d_y_dil")
    G["w_br_dil"] = _matmul(y_dil, db_dil, mode="tn", tk=512, name="dw_br_dil")
    mix = _dil_mix_bwd(dy_dil, o_dil, lse_dil)
    do_dil, dl_dil = mix[:3], mix[3:]
    d_parts = [[None] * DIL_GROUPS for _ in range(3)]
    for g, (window, dil) in enumerate(DIL_PAIRS):
        L = S // dil
        view = proj_d.reshape(L, dil * 3 * DIL_HEADS * LANES)
        col = lambda part, g=g: (lambda hh: (hh // DIL_HPG) * n_dil_cols + (part * DIL_GROUPS + g) * DIL_HPG + hh % DIL_HPG)
        shp = (L, dil * DIL_W)
        dq, dk, dv = _attn_bwd(
            view, view, view, do_dil[g].reshape(shp), lse_dil[g].reshape(shp), dl_dil[g].reshape(shp),
            n_heads=dil * DIL_HPG, qcol=col(0), kcol=col(1), vcol=col(2), ocol=ident, dkcol=ident, Lq=L, Lk=L, T=BAND,
            mode="band", scale=LANES ** -0.5, dq_shape=shp, dkv_shape=shp, slopes=slopes[g], slope_mul=float(dil),
            name=f"attn_dil{g}_bwd")
        for part, d in enumerate((dq, dk, dv)):
            d_parts[part][g] = d.reshape(S, DIL_W).astype(BF16)
    dproj_d = jnp.concatenate([d for part in d_parts for d in part], axis=1)

    dy_mla = _matmul(db_mla, w_br_mla_p, mode="nt", name="d_y_mla")
    dw_br_mla_p = _matmul(o_mla, db_mla, mode="tn", tk=512, name="dw_br_mla")
    G["w_br_mla"] = _unpad_heads(dw_br_mla_p, MLA_HEADS, MLA_NOPE, 0)
    delta_mla = _delta(dy_mla, o_mla, MLA_HEADS, "delta_mla")
    wide = (S, MLA_HEADS * LANES)
    dq_mla, dk_mla, dv_mla = _attn_bwd(
        q_mla, k_mla, v_mla, dy_mla, lse_mla, delta_mla, n_heads=MLA_HEADS, qcol=ident, kcol=ident, vcol=ident, ocol=ident,
        dkcol=ident, Lq=S, Lk=S, T=T_MLA, mode="causal", scale=None, dq_shape=wide, dkv_shape=wide, name="attn_mla_bwd")
    dq_pre, dkpe = _mla_bwd_prep(dq_mla, dk_mla, pos, invf)
    dcq_n = _matmul(dq_pre, w_uq_p, mode="nt", tn=Q_RANK, name="d_cq")
    G["w_uq"] = _unpad_heads(_matmul(cq_n, dq_pre, mode="tn", tm=Q_RANK, tk=512, name="dw_uq"), MLA_HEADS, MLA_QK, 1)
    dckv_a = _matmul(dk_mla, w_uk_p, mode="nt", tn=KV_RANK, name="d_ckv_k")
    dckv_b = _matmul(dv_mla, w_uv_p, mode="nt", tn=KV_RANK, name="d_ckv_v")
    dw_uk = _unpad_heads(_matmul(ckv_n, dk_mla, mode="tn", tm=KV_RANK, tk=512, name="dw_uk"), MLA_HEADS, MLA_NOPE, 1)
    dw_uv = _unpad_heads(_matmul(ckv_n, dv_mla, mode="tn", tm=KV_RANK, tk=512, name="dw_uv"), MLA_HEADS, MLA_NOPE, 1)
    G["w_ukv"] = jnp.concatenate([dw_uk.reshape(KV_RANK, MLA_HEADS, MLA_NOPE), dw_uv.reshape(KV_RANK, MLA_HEADS, MLA_NOPE)],
                                 axis=2).reshape(KV_RANK, -1)
    dproj_a, G["mla_q_norm"], G["mla_kv_norm"] = _mla_norm_bwd(dcq_n, dckv_a, dckv_b, dkpe, proj_a, W["mla_q_norm"],
                                                              W["mla_kv_norm"])

    dh = _matmul(dproj_a, w_a, mode="nt", name="d_h_a")
    dh = _matmul(dproj_d, w_d, mode="nt", tk=1536, add=dh, name="d_h_d")
    dh = _matmul(dq_mem, w_m, mode="nt", add=dh, name="d_h_m")
    dh = _matmul(dproj_g, w_g, mode="nt", tk=1536, add=dh, name="d_h_g")
    dw_a = _matmul(h, dproj_a, mode="tn", tn=768, tk=512, name="dw_in_a")
    dw_d = _matmul(h, dproj_d, mode="tn", tn=768, tk=512, name="dw_in_d")
    dw_m = _matmul(h, dq_mem, mode="tn", tn=512, tk=512, name="dw_in_m")
    dw_g = _matmul(h, dproj_g, mode="tn", tn=768, tk=512, name="dw_in_g")
    kr0 = Q_RANK + MLA_NOPE
    G["w_in"] = jnp.concatenate([dw_a[:, :Q_RANK], dw_a[:, Q_RANK + LANES:], dw_a[:, kr0:kr0 + MLA_ROPE], dw_d, dw_m, dw_g], axis=1)
    grad_x, G["g_pre_mix"] = _norm1_bwd(dx1, dh, x, W["g_pre_mix"])
    return loss, grad_x, G


WEIGHTS = ["g_pre_mix", "w_in", "b_gate", "mla_q_norm", "w_uq", "mla_kv_norm", "w_ukv", "g_mem", "w_mem_kv", "w_br_mla",
           "w_br_dil", "w_br_mem", "w_o", "g_post_mix", "g_pre_ffn", "w_ffn_up", "conv_w", "conv_b", "w_ffn_down", "g_post_ffn"]
SHARDED = [("w_in", (D_MODEL, D_IN), 1), ("w_uq", (Q_RANK, MLA_HEADS * MLA_QK), 1), ("w_ukv", (KV_RANK, MLA_HEADS * 2 * MLA_NOPE), 1),
           ("w_mem_kv", (D_MODEL, 2 * MEM_W), 0), ("w_br_mla", (MLA_HEADS * MLA_NOPE, D_MODEL), 1), ("w_br_dil", (DIL_W, D_MODEL), 1),
           ("w_br_mem", (MEM_W, D_MODEL), 1), ("w_o", (D_MODEL, D_MODEL), 0), ("w_ffn_up", (D_MODEL, 2 * D_FF), 1),
           ("w_ffn_down", (D_FF, D_MODEL), 0), ("conv_w", (3, 2 * D_FF), 1)]
REPLICATED = [("g_pre_mix", D_MODEL), ("b_gate", 3 * D_MODEL), ("mla_q_norm", Q_RANK), ("mla_kv_norm", KV_RANK), ("g_mem", D_MODEL),
              ("g_post_mix", D_MODEL), ("g_pre_ffn", D_MODEL), ("conv_b", 2 * D_FF), ("g_post_ffn", D_MODEL)]
SMALL_ROWS = 128
LOSS_AT = sum(n for _, n in REPLICATED)


def _shard_shape(shape, axis):
    return tuple(d // N_CHIPS if a == axis else d for a, d in enumerate(shape))


def _shard_elems(shape):
    return shape[0] * shape[1] // N_CHIPS


def _round_up(n, m):
    return -(-n // m) * m


GRAD_ELEMS = sum(_shard_elems(s) for _, s, _ in SHARDED)
ADD_ROWS = 512
GRAD_ROWS_HALF = _round_up(-(-GRAD_ELEMS // LANES), 2 * ADD_ROWS) // 2
WPACK_ELEMS = GRAD_ELEMS + _shard_elems(SHARDED[-1][1])
WPACK_ROWS_HALF = _round_up(-(-WPACK_ELEMS // LANES), 64) // 2


def _split_shards(a, axis):
    r, c = a.shape
    if axis == 0:
        return a.reshape(N_CHIPS, -1)
    return a.reshape(r, N_CHIPS, c // N_CHIPS).transpose(1, 0, 2).reshape(N_CHIPS, -1)


def _join_shards(flat, shape, axis):
    r, c = shape
    if axis == 0:
        return flat.reshape(r, c)
    return flat.reshape(N_CHIPS, r, c // N_CHIPS).transpose(1, 0, 2).reshape(r, c)


def _pack_my_weights(w):
    parts = []
    for name, _, _ in SHARDED:
        a = w[name]
        if name == "conv_w":
            a = lax.bitcast_convert_type(a, BF16)
        else:
            a = a.astype(BF16)
        parts.append(a.reshape(-1))
    flat = jnp.concatenate(parts)
    flat = jnp.pad(flat, (0, 2 * WPACK_ROWS_HALF * LANES - flat.shape[0]))
    return flat.reshape(2, WPACK_ROWS_HALF, LANES)


def _unpack_weights(gathered):
    flat = gathered.reshape(N_CHIPS, -1)
    out, off = {}, 0
    for name, shape, axis in SHARDED:
        n = _shard_elems(shape) * (2 if name == "conv_w" else 1)
        seg = flat[:, off:off + n]
        off += n
        if name == "conv_w":
            seg = lax.bitcast_convert_type(seg.reshape(N_CHIPS, -1, 2), F32)
        out[name] = _join_shards(seg, shape, axis)
    return out


def _pack_grads(G):
    flat = jnp.concatenate([_split_shards(G[name], axis) for name, _, axis in SHARDED], axis=1)
    flat = jnp.pad(flat, ((0, 0), (0, 2 * GRAD_ROWS_HALF * LANES - flat.shape[1])))
    return flat.reshape(N_CHIPS, 2, GRAD_ROWS_HALF, LANES).transpose(1, 0, 2, 3)


def _unpack_my_grads(red):
    flat = red.reshape(-1)
    out, off = {}, 0
    for name, shape, axis in SHARDED:
        n = _shard_elems(shape)
        out[name] = flat[off:off + n].reshape(_shard_shape(shape, axis))
        off += n
    return out


def _pack_small(vals, loss=None):
    parts = [vals[name].reshape(-1) for name, _ in REPLICATED]
    if loss is not None:
        parts.append(loss.reshape(1))
    flat = jnp.concatenate(parts)
    return jnp.pad(flat, (0, SMALL_ROWS * LANES - flat.shape[0])).reshape(SMALL_ROWS, LANES)


def _unpack_small(packed):
    flat = packed.reshape(-1)
    out, off = {}, 0
    for name, n in REPLICATED:
        out[name] = flat[off:off + n].reshape(1, n)
        off += n
    return out


MESH = pl.DeviceIdType.MESH
HBM_SPEC = pl.BlockSpec(memory_space=pltpu.HBM)


def _place():
    x, y, c = lax.axis_index("x"), lax.axis_index("y"), lax.axis_index("c")
    chips = [(1 - x, y), (x, 1 - y), (1 - x, 1 - y)]
    return x, y, c, chips


def _remote(src, dst, send_sems, recv_sems, k, to):
    return pltpu.make_async_remote_copy(src_ref=src, dst_ref=dst, send_sem=send_sems.at[k], recv_sem=recv_sems.at[k],
                                        device_id=to, device_id_type=MESH)


def _gather_weights(pack):
    _, R, _ = pack.shape

    def body(src, out, send_sems, recv_sems, local_sem):
        x, y, c, chips = _place()
        me = 2 * x + y
        sibling = (x, y, 1 - c)
        mine = pltpu.make_async_copy(src, out.at[me], local_sem)
        mine.start()
        first = [_remote(src.at[c], out.at[me, c], send_sems, recv_sems, k, (px, py, c)) for k, (px, py) in enumerate(chips)]
        for cp in first:
            cp.start()
        passed = []
        for k, (px, py) in enumerate(chips):
            slot = out.at[2 * px + py, c]
            _remote(slot, slot, send_sems, recv_sems, k, (px, py, c)).wait_recv()
            cp = _remote(slot, slot, send_sems, recv_sems, 3 + k, sibling)
            cp.start()
            passed.append(cp)
        for k, (px, py) in enumerate(chips):
            slot = out.at[2 * px + py, 1 - c]
            _remote(slot, slot, send_sems, recv_sems, 3 + k, sibling).wait_recv()
        for cp in first + passed:
            cp.wait_send()
        mine.wait()

    return pl.pallas_call(
        body, name="comm_gather_weights", in_specs=[HBM_SPEC], out_specs=HBM_SPEC,
        out_shape=jax.ShapeDtypeStruct((N_CHIPS, 2, R, LANES), pack.dtype),
        scratch_shapes=[pltpu.SemaphoreType.DMA((6,)), pltpu.SemaphoreType.DMA((6,)), pltpu.SemaphoreType.DMA],
    )(pack)


def _swap_halves(g):
    _, n, R, _ = g.shape

    def body(src, out, send_sems, recv_sems):
        x, y, c, _ = _place()
        cp = _remote(src.at[1 - c], out, send_sems, recv_sems, 0, (x, y, 1 - c))
        cp.start()
        cp.wait()

    return pl.pallas_call(
        body, name="comm_swap_halves", in_specs=[HBM_SPEC], out_specs=HBM_SPEC,
        out_shape=jax.ShapeDtypeStruct((n, R, LANES), g.dtype),
        scratch_shapes=[pltpu.SemaphoreType.DMA((1,)), pltpu.SemaphoreType.DMA((1,))],
    )(g)


def _scatter_partials(p):
    n, R, _ = p.shape

    def body(src, out, send_sems, recv_sems, local_sem):
        x, y, c, chips = _place()
        me = 2 * x + y
        mine = pltpu.make_async_copy(src.at[me], out.at[me], local_sem)
        mine.start()
        sends = [_remote(src.at[2 * px + py], out.at[me], send_sems, recv_sems, k, (px, py, c)) for k, (px, py) in enumerate(chips)]
        for cp in sends:
            cp.start()
        for k, (px, py) in enumerate(chips):
            slot = out.at[2 * px + py]
            _remote(slot, slot, send_sems, recv_sems, k, (px, py, c)).wait_recv()
        for cp in sends:
            cp.wait_send()
        mine.wait()

    return pl.pallas_call(
        body, name="comm_scatter_partials", in_specs=[HBM_SPEC], out_specs=HBM_SPEC,
        out_shape=jax.ShapeDtypeStruct((n, R, LANES), p.dtype),
        scratch_shapes=[pltpu.SemaphoreType.DMA((3,)), pltpu.SemaphoreType.DMA((3,)), pltpu.SemaphoreType.DMA],
    )(p)


def _share_reduced(r):
    R, _ = r.shape

    def body(src, out, send_sems, recv_sems, local_sem):
        x, y, c, _ = _place()
        mine = pltpu.make_async_copy(src, out.at[c], local_sem)
        mine.start()
        cp = _remote(src, out.at[c], send_sems, recv_sems, 0, (x, y, 1 - c))
        cp.start()
        _remote(src, out.at[1 - c], send_sems, recv_sems, 0, (x, y, 1 - c)).wait_recv()
        cp.wait_send()
        mine.wait()

    return pl.pallas_call(
        body, name="comm_share_reduced", in_specs=[HBM_SPEC], out_specs=HBM_SPEC,
        out_shape=jax.ShapeDtypeStruct((2, R, LANES), r.dtype),
        scratch_shapes=[pltpu.SemaphoreType.DMA((1,)), pltpu.SemaphoreType.DMA((1,)), pltpu.SemaphoreType.DMA],
    )(r)


def _allreduce_small(v):
    n_dev = 2 * N_CHIPS

    def body(src, out, buf, send_sems, recv_sems):
        x, y, c, _ = _place()
        me = 4 * x + 2 * y + c
        buf[me] = src[...]
        flips = [(k >> 2 & 1, k >> 1 & 1, k & 1) for k in range(1, n_dev)]
        sends = []
        for k, (fx, fy, fc) in enumerate(flips):
            to = ((1 - x) if fx else x, (1 - y) if fy else y, (1 - c) if fc else c)
            cp = _remote(src, buf.at[me], send_sems, recv_sems, k, to)
            cp.start()
            sends.append((cp, to))
        for k, (cp, to) in enumerate(sends):
            slot = buf.at[4 * to[0] + 2 * to[1] + to[2]]
            _remote(slot, slot, send_sems, recv_sems, k, to).wait_recv()
        for cp, _ in sends:
            cp.wait_send()
        acc = buf[0]
        for d in range(1, n_dev):
            acc = acc + buf[d]
        out[...] = acc

    vm = pl.BlockSpec(memory_space=pltpu.VMEM)
    return pl.pallas_call(
        body, name="comm_allreduce_small", in_specs=[vm], out_specs=vm, out_shape=jax.ShapeDtypeStruct(v.shape, F32),
        scratch_shapes=[pltpu.VMEM((n_dev,) + v.shape, F32), pltpu.SemaphoreType.DMA((n_dev - 1,)),
                        pltpu.SemaphoreType.DMA((n_dev - 1,))],
    )(v)


def _add_slabs(a, b=None, name="add"):
    n, R, _ = a.shape
    t = ADD_ROWS
    assert R % t == 0, (R, t)
    if b is None:
        def body(a_ref, o_ref):
            acc = a_ref[0]
            for k in range(1, n):
                acc = acc + a_ref[k]
            o_ref[...] = acc
        return pl.pallas_call(body, name=name, grid=(R // t,), in_specs=[pl.BlockSpec((n, t, LANES), lambda i: (0, i, 0))],
                              out_specs=pl.BlockSpec((t, LANES), lambda i: (i, 0)),
                              out_shape=jax.ShapeDtypeStruct((R, LANES), F32), compiler_params=_params("parallel"))(a)

    def body2(a_ref, b_ref, o_ref):
        o_ref[...] = a_ref[...] + b_ref[...]
    sp = pl.BlockSpec((1, t, LANES), lambda k, i: (k, i, 0))
    return pl.pallas_call(body2, name=name, grid=(n, R // t), in_specs=[sp, sp], out_specs=sp,
                          out_shape=jax.ShapeDtypeStruct(a.shape, F32), compiler_params=_params("parallel", "parallel"))(a, b)


def _adamw(w, g, m, v, name):
    R, C = w.shape
    t = 128 if R % 128 == 0 else R
    c1 = 1.0 - ADAM_B1 ** ADAM_STEP
    c2 = 1.0 - ADAM_B2 ** ADAM_STEP

    def body(w_ref, g_ref, m_ref, v_ref, d_ref, nm_ref, nv_ref):
        gv = g_ref[...]
        nm = ADAM_B1 * m_ref[...] + (1.0 - ADAM_B1) * gv
        nv = ADAM_B2 * v_ref[...] + (1.0 - ADAM_B2) * (gv * gv)
        d_ref[...] = -ADAM_LR * ((nm / c1) / (jnp.sqrt(nv / c2) + ADAM_EPS) + ADAM_WD * w_ref[...])
        nm_ref[...] = nm
        nv_ref[...] = nv

    sp = pl.BlockSpec((t, C), lambda i: (i, 0))
    return pl.pallas_call(body, name=name, grid=(R // t,), in_specs=[sp] * 4, out_specs=[sp] * 3,
                          out_shape=[jax.ShapeDtypeStruct((R, C), F32)] * 3, compiler_params=_params("parallel"))(w, g, m, v)


def kernel(x, mem, positions, g_pre_mix, w_in, b_gate, mla_q_norm, w_uq, mla_kv_norm, w_ukv, g_mem, w_mem_kv, w_br_mla, w_br_dil, w_br_mem, w_o, g_post_mix, g_pre_ffn, w_ffn_up, conv_w, conv_b, w_ffn_down, g_post_ffn, loss_target, m_g_pre_mix, m_w_in, m_b_gate, m_mla_q_norm, m_w_uq, m_mla_kv_norm, m_w_ukv, m_g_mem, m_w_mem_kv, m_w_br_mla, m_w_br_dil, m_w_br_mem, m_w_o, m_g_post_mix, m_g_pre_ffn, m_w_ffn_up, m_conv_w, m_conv_b, m_w_ffn_down, m_g_post_ffn, v_g_pre_mix, v_w_in, v_b_gate, v_mla_q_norm, v_w_uq, v_mla_kv_norm, v_w_ukv, v_g_mem, v_w_mem_kv, v_w_br_mla, v_w_br_dil, v_w_br_mem, v_w_o, v_g_post_mix, v_g_pre_ffn, v_w_ffn_up, v_conv_w, v_conv_b, v_w_ffn_down, v_g_post_ffn):
    w_args = (g_pre_mix, w_in, b_gate, mla_q_norm, w_uq, mla_kv_norm, w_ukv, g_mem, w_mem_kv, w_br_mla, w_br_dil, w_br_mem, w_o,
              g_post_mix, g_pre_ffn, w_ffn_up, conv_w, conv_b, w_ffn_down, g_post_ffn)
    m_args = (m_g_pre_mix, m_w_in, m_b_gate, m_mla_q_norm, m_w_uq, m_mla_kv_norm, m_w_ukv, m_g_mem, m_w_mem_kv, m_w_br_mla,
              m_w_br_dil, m_w_br_mem, m_w_o, m_g_post_mix, m_g_pre_ffn, m_w_ffn_up, m_conv_w, m_conv_b, m_w_ffn_down, m_g_post_ffn)
    v_args = (v_g_pre_mix, v_w_in, v_b_gate, v_mla_q_norm, v_w_uq, v_mla_kv_norm, v_w_ukv, v_g_mem, v_w_mem_kv, v_w_br_mla,
              v_w_br_dil, v_w_br_mem, v_w_o, v_g_post_mix, v_g_pre_ffn, v_w_ffn_up, v_conv_w, v_conv_b, v_w_ffn_down, v_g_post_ffn)
    sharded = {name for name, _, _ in SHARDED}

    def local(a, name):
        return a[0] if name in sharded else a

    w = {n: local(a, n) for n, a in zip(WEIGHTS, w_args)}
    m = {n: local(a, n) for n, a in zip(WEIGHTS, m_args)}
    v = {n: local(a, n) for n, a in zip(WEIGHTS, v_args)}

    full = _unpack_weights(_gather_weights(_pack_my_weights(w)))
    full.update({name: w[name] for name, _ in REPLICATED})

    loss_local, grad_x, G = _local_step(x[0], mem[0], positions, loss_target[0], full)

    packed = _pack_grads(G)
    from_sibling = _swap_halves(packed)
    c = lax.axis_index("c")
    my_half = lax.dynamic_index_in_dim(packed, c, axis=0, keepdims=False)
    chip_partial = _add_slabs(my_half, from_sibling, name="add_sibling")
    reduced_half = _add_slabs(_scatter_partials(chip_partial), name="add_chips")
    g_shard = _unpack_my_grads(_share_reduced(reduced_half))
    small = _allreduce_small(_pack_small(G, loss_local))
    g_small = _unpack_small(small)
    loss = small.reshape(-1)[LOSS_AT]

    grads, deltas, new_m, new_v = {}, {}, {}, {}
    for name, _, _ in SHARDED:
        grads[name] = g_shard[name]
        deltas[name], new_m[name], new_v[name] = _adamw(w[name], g_shard[name], m[name], v[name], "adamw_" + name)
    sd, sm, sv = _adamw(_pack_small(w), small, _pack_small(m), _pack_small(v), "adamw_small")
    for dst, packed_small in ((deltas, sd), (new_m, sm), (new_v, sv)):
        dst.update(_unpack_small(packed_small))
    grads.update(g_small)

    def out(d, name):
        return d[name][None] if name in sharded else d[name]

    return (loss, grad_x[None], *[out(grads, n) for n in WEIGHTS], *[out(deltas, n) for n in WEIGHTS],
            *[out(new_m, n) for n in WEIGHTS], *[out(new_v, n) for n in WEIGHTS])
```

```python
import functools
import math

import jax
import jax.numpy as jnp
from jax import lax
from jax.experimental import pallas as pl
from jax.experimental.pallas import tpu as pltpu

F32 = jnp.float32
BF16 = jnp.bfloat16

D_MODEL = 1024
N_MEM = 256
RMS_EPS = 1e-6
NEG_INF = -1e30
MLA_HEADS = 8
MLA_NOPE = 64
MLA_ROPE = 32
MLA_QK = 96
Q_RANK = 384
KV_RANK = 256
ROPE_THETA = 10000.0
DIL_PAIRS = ((128, 1), (512, 4), (2048, 16))
DIL_GROUPS = 3
DIL_HPG = 4
DIL_HEADS = 12
DIL_W = 512
MEM_HEADS = 4
MEM_W = 512
D_FF = 2816
OFF_Q = 384
OFF_KV = 640
OFF_KR = 672
OFF_DIL = 5280
OFF_MEMQ = 5792
D_IN = 8864
ADAM_LR = 0.001
ADAM_B1 = 0.9
ADAM_B2 = 0.999
ADAM_EPS = 1e-08
ADAM_WD = 0.01
ADAM_STEP = 10

LANES = 128
VMEM_LIMIT = 56 * 1024 * 1024

N_CHIPS = 4
ROW_TILE = 256

NN = (((1,), (0,)), ((), ()))
NT = (((1,), (1,)), ((), ()))
TN = (((0,), (0,)), ((), ()))


def _params(*sem):
    return pltpu.CompilerParams(dimension_semantics=sem, vmem_limit_bytes=VMEM_LIMIT)


def _full(shape):
    return pl.BlockSpec(shape, lambda *_: (0,) * len(shape))


def _matmul(a, b, *, mode="nn", out_dtype=F32, tm=512, tn=512, tk=None, add=None, name):
    if mode == "nn":
        (M, K), N = a.shape, b.shape[1]
    elif mode == "nt":
        (M, K), N = a.shape, b.shape[0]
    else:
        (K, M), N = a.shape, b.shape[1]
    tm, tn = min(tm, M), min(tn, N)
    tk = K if tk is None else min(tk, K)
    assert M % tm == 0 and N % tn == 0 and K % tk == 0, (name, M, N, K, tm, tn, tk)
    nk = K // tk
    dims = {"nn": NN, "nt": NT, "tn": TN}[mode]
    a_spec = pl.BlockSpec((tk, tm), lambda i, j, k: (k, i)) if mode == "tn" else pl.BlockSpec((tm, tk), lambda i, j, k: (i, k))
    b_spec = pl.BlockSpec((tn, tk), lambda i, j, k: (j, k)) if mode == "nt" else pl.BlockSpec((tk, tn), lambda i, j, k: (k, j))
    o_spec = pl.BlockSpec((tm, tn), lambda i, j, k: (i, j))
    has_add = add is not None

    def body(*refs):
        a_ref, b_ref = refs[0], refs[1]
        c_ref = refs[2] if has_add else None
        o_ref = refs[3] if has_add else refs[2]
        part = lax.dot_general(a_ref[...].astype(BF16), b_ref[...].astype(BF16), dims, preferred_element_type=F32)
        if nk == 1:
            if has_add:
                part = part + c_ref[...]
            o_ref[...] = part.astype(out_dtype)
        else:
            acc = refs[-1]
            k = pl.program_id(2)

            @pl.when(k == 0)
            def _():
                acc[...] = part

            @pl.when(k > 0)
            def _():
                acc[...] += part

            @pl.when(k == nk - 1)
            def _():
                r = acc[...]
                if has_add:
                    r = r + c_ref[...]
                o_ref[...] = r.astype(out_dtype)

    in_specs = [a_spec, b_spec] + ([o_spec] if has_add else [])
    args = (a, b) + ((add,) if has_add else ())
    return pl.pallas_call(
        body, name=name, grid=(M // tm, N // tn, nk), in_specs=in_specs, out_specs=o_spec,
        out_shape=jax.ShapeDtypeStruct((M, N), out_dtype),
        scratch_shapes=[pltpu.VMEM((tm, tn), F32)] if nk > 1 else [],
        compiler_params=_params("parallel", "parallel", "arbitrary"),
    )(*args)


def _rms_fwd_val(x, g):
    r = lax.rsqrt(jnp.mean(x * x, axis=-1, keepdims=True) + RMS_EPS)
    return (x * r) * g


def _rms_bwd_val(dy, x, g):
    r = lax.rsqrt(jnp.mean(x * x, axis=-1, keepdims=True) + RMS_EPS)
    xn = x * r
    gdy = g * dy
    dx = r * (gdy - xn * jnp.mean(gdy * xn, axis=-1, keepdims=True))
    return dx, dy * xn


def _rope_tables(pos, invf, inverse):
    ang = pos * invf
    cos, sin = jnp.cos(ang), jnp.sin(ang)
    lane = lax.broadcasted_iota(jnp.int32, ang.shape, 1)
    first = (lane >= MLA_NOPE) & (lane < MLA_NOPE + MLA_ROPE // 2)
    second = (lane >= MLA_NOPE + MLA_ROPE // 2) & (lane < MLA_QK)
    sgn = -1.0 if inverse else 1.0
    sa = jnp.where(first, -sgn * sin, 0.0)
    sb = jnp.where(second, sgn * sin, 0.0)
    return cos, sa, sb


def _rope_val(x, cos, sa, sb):
    half = MLA_ROPE // 2
    return x * cos + pltpu.roll(x, LANES - half, 1) * sa + pltpu.roll(x, half, 1) * sb


def _head_sum_bcast(v, n_heads):
    parts = []
    for h in range(n_heads):
        s = jnp.sum(v[:, h * LANES:(h + 1) * LANES], axis=1, keepdims=True)
        parts.append(jnp.broadcast_to(s, (v.shape[0], LANES)))
    return parts


def _row_spec(t, w):
    return pl.BlockSpec((t, w), lambda i: (i, 0))


def _acc_spec(w, rows=1):
    return pl.BlockSpec((rows, w), lambda i: (0, 0))


def _rmsnorm(x, g, out_dtype, name):
    S, W = x.shape
    t = min(ROW_TILE, S)

    def body(x_ref, g_ref, o_ref):
        o_ref[...] = _rms_fwd_val(x_ref[...], g_ref[...]).astype(out_dtype)

    return pl.pallas_call(body, name=name, grid=(S // t,), in_specs=[_row_spec(t, W), _acc_spec(W)],
                          out_specs=_row_spec(t, W), out_shape=jax.ShapeDtypeStruct((S, W), out_dtype),
                          compiler_params=_params("parallel"))(x, g)


def _mla_prep(proj_a, pos, invf, q_norm, kv_norm):
    S = proj_a.shape[0]
    t = ROW_TILE

    def body(a_ref, pos_ref, invf_ref, qn_ref, kvn_ref, cq_ref, ckv_ref, kpe_ref):
        a = a_ref[...]
        cq_ref[...] = _rms_fwd_val(a[:, 0:Q_RANK], qn_ref[...]).astype(BF16)
        ckv_ref[...] = _rms_fwd_val(a[:, Q_RANK + LANES:], kvn_ref[...]).astype(BF16)
        cos, sa, sb = _rope_tables(pos_ref[...], invf_ref[...], False)
        kpe_ref[...] = _rope_val(a[:, Q_RANK:Q_RANK + LANES], cos, sa, sb)

    return pl.pallas_call(
        body, name="mla_prep", grid=(S // t,),
        in_specs=[_row_spec(t, 768), _row_spec(t, 1), _acc_spec(LANES), _acc_spec(Q_RANK), _acc_spec(KV_RANK)],
        out_specs=[_row_spec(t, Q_RANK), _row_spec(t, KV_RANK), _row_spec(t, LANES)],
        out_shape=[jax.ShapeDtypeStruct((S, Q_RANK), BF16), jax.ShapeDtypeStruct((S, KV_RANK), BF16),
                   jax.ShapeDtypeStruct((S, LANES), F32)],
        compiler_params=_params("parallel"))(proj_a, pos, invf, q_norm, kv_norm)


def _qk_final(q_pre, k_pre, kpe, pos, invf):
    S, W = q_pre.shape
    t = ROW_TILE
    scale = MLA_QK ** -0.5

    def body(q_ref, k_ref, kpe_ref, pos_ref, invf_ref, qo_ref, ko_ref):
        cos, sa, sb = _rope_tables(pos_ref[...], invf_ref[...], False)
        kpe_v = kpe_ref[...]
        for h in range(MLA_HEADS):
            sl = slice(h * LANES, (h + 1) * LANES)
            qo_ref[:, sl] = (_rope_val(q_ref[:, sl], cos, sa, sb) * scale).astype(BF16)
            ko_ref[:, sl] = (k_ref[:, sl] + kpe_v).astype(BF16)

    return pl.pallas_call(
        body, name="qk_final", grid=(S // t,),
        in_specs=[_row_spec(t, W), _row_spec(t, W), _row_spec(t, LANES), _row_spec(t, 1), _acc_spec(LANES)],
        out_specs=[_row_spec(t, W), _row_spec(t, W)],
        out_shape=[jax.ShapeDtypeStruct((S, W), BF16)] * 2,
        compiler_params=_params("parallel"))(q_pre, k_pre, kpe, pos, invf)


def _mla_bwd_prep(dq, dk, pos, invf):
    S, W = dq.shape
    t = ROW_TILE
    scale = MLA_QK ** -0.5

    def body(dq_ref, dk_ref, pos_ref, invf_ref, dqp_ref, dkpe_ref):
        cos, sa, sb = _rope_tables(pos_ref[...], invf_ref[...], True)
        tot = jnp.zeros((t, LANES), F32)
        for h in range(MLA_HEADS):
            sl = slice(h * LANES, (h + 1) * LANES)
            dqp_ref[:, sl] = _rope_val(dq_ref[:, sl] * scale, cos, sa, sb).astype(BF16)
            tot = tot + dk_ref[:, sl]
        lane = lax.broadcasted_iota(jnp.int32, tot.shape, 1)
        tot = jnp.where((lane >= MLA_NOPE) & (lane < MLA_QK), tot, 0.0)
        dkpe_ref[...] = _rope_val(tot, cos, sa, sb)

    return pl.pallas_call(
        body, name="mla_bwd_prep", grid=(S // t,),
        in_specs=[_row_spec(t, W), _row_spec(t, W), _row_spec(t, 1), _acc_spec(LANES)],
        out_specs=[_row_spec(t, W), _row_spec(t, LANES)],
        out_shape=[jax.ShapeDtypeStruct((S, W), BF16), jax.ShapeDtypeStruct((S, LANES), F32)],
        compiler_params=_params("parallel"))(dq, dk, pos, invf)


def _mla_norm_bwd(dcq, dckv_a, dckv_b, dkpe, proj_a, q_norm, kv_norm):
    S = proj_a.shape[0]
    t = ROW_TILE

    def body(dcq_ref, da_ref, db_ref, dkpe_ref, a_ref, qn_ref, kvn_ref, o_ref, dqn_ref, dkvn_ref):
        i = pl.program_id(0)
        a = a_ref[...]
        dxq, gq = _rms_bwd_val(dcq_ref[...], a[:, 0:Q_RANK], qn_ref[...])
        dxkv, gkv = _rms_bwd_val(da_ref[...] + db_ref[...], a[:, Q_RANK + LANES:], kvn_ref[...])
        o_ref[:, 0:Q_RANK] = dxq.astype(BF16)
        o_ref[:, Q_RANK:Q_RANK + LANES] = dkpe_ref[...].astype(BF16)
        o_ref[:, Q_RANK + LANES:] = dxkv.astype(BF16)

        @pl.when(i == 0)
        def _():
            dqn_ref[...] = jnp.zeros_like(dqn_ref)
            dkvn_ref[...] = jnp.zeros_like(dkvn_ref)

        dqn_ref[...] += jnp.sum(gq, axis=0, keepdims=True)
        dkvn_ref[...] += jnp.sum(gkv, axis=0, keepdims=True)

    return pl.pallas_call(
        body, name="mla_norm_bwd", grid=(S // t,),
        in_specs=[_row_spec(t, Q_RANK), _row_spec(t, KV_RANK), _row_spec(t, KV_RANK), _row_spec(t, LANES),
                  _row_spec(t, 768), _acc_spec(Q_RANK), _acc_spec(KV_RANK)],
        out_specs=[_row_spec(t, 768), _acc_spec(Q_RANK), _acc_spec(KV_RANK)],
        out_shape=[jax.ShapeDtypeStruct((S, 768), BF16), jax.ShapeDtypeStruct((1, Q_RANK), F32),
                   jax.ShapeDtypeStruct((1, KV_RANK), F32)],
        compiler_params=_params("arbitrary"))(dcq, dckv_a, dckv_b, dkpe, proj_a, q_norm, kv_norm)


def _dil_mix(o_list, lse_list):
    S = o_list[0].shape[0]
    t = ROW_TILE

    def body(o0, o1, o2, l0, l1, l2, y_ref):
        ls = [l0[...], l1[...], l2[...]]
        m = jnp.maximum(jnp.maximum(ls[0], ls[1]), ls[2])
        es = [jnp.exp(l - m) for l in ls]
        den = es[0] + es[1] + es[2]
        y = (es[0] / den) * o0[...] + (es[1] / den) * o1[...] + (es[2] / den) * o2[...]
        y_ref[...] = y.astype(BF16)

    return pl.pallas_call(
        body, name="dil_mix", grid=(S // t,), in_specs=[_row_spec(t, DIL_W)] * 6, out_specs=_row_spec(t, DIL_W),
        out_shape=jax.ShapeDtypeStruct((S, DIL_W), BF16), compiler_params=_params("parallel"))(*o_list, *lse_list)


def _dil_mix_bwd(dy, o_list, lse_list):
    S = dy.shape[0]
    t = ROW_TILE

    def body(dy_ref, o0, o1, o2, l0, l1, l2, d0, d1, d2, e0, e1, e2):
        ls = [l0[...], l1[...], l2[...]]
        os_ = [o0[...], o1[...], o2[...]]
        m = jnp.maximum(jnp.maximum(ls[0], ls[1]), ls[2])
        es = [jnp.exp(l - m) for l in ls]
        den = es[0] + es[1] + es[2]
        ws = [e / den for e in es]
        dyv = dy_ref[...]
        y = ws[0] * os_[0] + ws[1] * os_[1] + ws[2] * os_[2]
        b = jnp.concatenate(_head_sum_bcast(dyv * y, DIL_HPG), axis=1)
        for w, d_ref, e_ref in zip(ws, (d0, d1, d2), (e0, e1, e2)):
            d_ref[...] = (w * dyv).astype(BF16)
            e_ref[...] = w * b

    return pl.pallas_call(
        body, name="dil_mix_bwd", grid=(S // t,), in_specs=[_row_spec(t, DIL_W)] * 7, out_specs=[_row_spec(t, DIL_W)] * 6,
        out_shape=[jax.ShapeDtypeStruct((S, DIL_W), BF16)] * 3 + [jax.ShapeDtypeStruct((S, DIL_W), F32)] * 3,
        compiler_params=_params("parallel"))(dy, *o_list, *lse_list)


def _delta(do, o, n_heads, name):
    S, W = do.shape
    t = ROW_TILE

    def body(do_ref, o_ref, d_ref):
        prod = do_ref[...].astype(F32) * o_ref[...].astype(F32)
        d_ref[...] = jnp.concatenate(_head_sum_bcast(prod, n_heads), axis=1)

    return pl.pallas_call(body, name=name, grid=(S // t,), in_specs=[_row_spec(t, W)] * 2, out_specs=_row_spec(t, W),
                          out_shape=jax.ShapeDtypeStruct((S, W), F32), compiler_params=_params("parallel"))(do, o)


def _merge(proj_g, b_gate, b_list):
    S = proj_g.shape[0]
    t = ROW_TILE

    def body(g_ref, b_ref, y0, y1, y2, o_ref):
        acc = jnp.zeros((t, D_MODEL), F32)
        for i, y in enumerate((y0, y1, y2)):
            sl = slice(i * D_MODEL, (i + 1) * D_MODEL)
            acc = acc + jax.nn.sigmoid(g_ref[:, sl] + b_ref[:, sl]) * y[...]
        o_ref[...] = acc.astype(BF16)

    return pl.pallas_call(
        body, name="merge", grid=(S // t,),
        in_specs=[_row_spec(t, 3 * D_MODEL), _acc_spec(3 * D_MODEL)] + [_row_spec(t, D_MODEL)] * 3,
        out_specs=_row_spec(t, D_MODEL), out_shape=jax.ShapeDtypeStruct((S, D_MODEL), BF16),
        compiler_params=_params("parallel"))(proj_g, b_gate, *b_list)


def _merge_bwd(dmerged, proj_g, b_gate, b_list):
    S = proj_g.shape[0]
    t = ROW_TILE

    def body(dm_ref, g_ref, b_ref, y0, y1, y2, d0, d1, d2, dz_ref, db_ref):
        i = pl.program_id(0)

        @pl.when(i == 0)
        def _():
            db_ref[...] = jnp.zeros_like(db_ref)

        dm = dm_ref[...]
        for k, (y, d_ref) in enumerate(zip((y0, y1, y2), (d0, d1, d2))):
            sl = slice(k * D_MODEL, (k + 1) * D_MODEL)
            s = jax.nn.sigmoid(g_ref[:, sl] + b_ref[:, sl])
            d_ref[...] = (s * dm).astype(BF16)
            dz = dm * y[...] * (s * (1.0 - s))
            dz_ref[:, sl] = dz.astype(BF16)
            db_ref[:, sl] += jnp.sum(dz, axis=0, keepdims=True)

    return pl.pallas_call(
        body, name="merge_bwd", grid=(S // t,),
        in_specs=[_row_spec(t, D_MODEL), _row_spec(t, 3 * D_MODEL), _acc_spec(3 * D_MODEL)] + [_row_spec(t, D_MODEL)] * 3,
        out_specs=[_row_spec(t, D_MODEL)] * 3 + [_row_spec(t, 3 * D_MODEL), _acc_spec(3 * D_MODEL)],
        out_shape=[jax.ShapeDtypeStruct((S, D_MODEL), BF16)] * 3
        + [jax.ShapeDtypeStruct((S, 3 * D_MODEL), BF16), jax.ShapeDtypeStruct((1, 3 * D_MODEL), F32)],
        compiler_params=_params("arbitrary"))(dmerged, proj_g, b_gate, *b_list)


def _norm2(o, x, g_post, g_pre):
    S = x.shape[0]
    t = ROW_TILE

    def body(o_ref, x_ref, gp_ref, gf_ref, x1_ref, h2_ref):
        x1 = x_ref[...] + _rms_fwd_val(o_ref[...], gp_ref[...])
        x1_ref[...] = x1
        h2_ref[...] = _rms_fwd_val(x1, gf_ref[...]).astype(BF16)

    return pl.pallas_call(
        body, name="norm2", grid=(S // t,),
        in_specs=[_row_spec(t, D_MODEL)] * 2 + [_acc_spec(D_MODEL)] * 2, out_specs=[_row_spec(t, D_MODEL)] * 2,
        out_shape=[jax.ShapeDtypeStruct((S, D_MODEL), F32), jax.ShapeDtypeStruct((S, D_MODEL), BF16)],
        compiler_params=_params("parallel"))(o, x, g_post, g_pre)


def _norm2_bwd(dx2, dh2, x1, o, g_pre, g_post):
    S = x1.shape[0]
    t = ROW_TILE

    def body(dx2_ref, dh2_ref, x1_ref, o_ref, gf_ref, gp_ref, dx1_ref, do_ref, dgf_ref, dgp_ref):
        i = pl.program_id(0)

        @pl.when(i == 0)
        def _():
            dgf_ref[...] = jnp.zeros_like(dgf_ref)
            dgp_ref[...] = jnp.zeros_like(dgp_ref)

        d1, gf = _rms_bwd_val(dh2_ref[...], x1_ref[...], gf_ref[...])
        dx1 = dx2_ref[...] + d1
        dx1_ref[...] = dx1
        do, gp = _rms_bwd_val(dx1, o_ref[...], gp_ref[...])
        do_ref[...] = do.astype(BF16)
        dgf_ref[...] += jnp.sum(gf, axis=0, keepdims=True)
        dgp_ref[...] += jnp.sum(gp, axis=0, keepdims=True)

    return pl.pallas_call(
        body, name="norm2_bwd", grid=(S // t,),
        in_specs=[_row_spec(t, D_MODEL)] * 4 + [_acc_spec(D_MODEL)] * 2,
        out_specs=[_row_spec(t, D_MODEL)] * 2 + [_acc_spec(D_MODEL)] * 2,
        out_shape=[jax.ShapeDtypeStruct((S, D_MODEL), F32), jax.ShapeDtypeStruct((S, D_MODEL), BF16),
                   jax.ShapeDtypeStruct((1, D_MODEL), F32), jax.ShapeDtypeStruct((1, D_MODEL), F32)],
        compiler_params=_params("arbitrary"))(dx2, dh2, x1, o, g_pre, g_post)


def _norm1_bwd(dx1, dh, x, g):
    S = x.shape[0]
    t = ROW_TILE

    def body(dx1_ref, dh_ref, x_ref, g_ref, dx_ref, dg_ref):
        i = pl.program_id(0)

        @pl.when(i == 0)
        def _():
            dg_ref[...] = jnp.zeros_like(dg_ref)

        d, gg = _rms_bwd_val(dh_ref[...], x_ref[...], g_ref[...])
        dx_ref[...] = dx1_ref[...] + d
        dg_ref[...] += jnp.sum(gg, axis=0, keepdims=True)

    return pl.pallas_call(
        body, name="norm1_bwd", grid=(S // t,),
        in_specs=[_row_spec(t, D_MODEL)] * 3 + [_acc_spec(D_MODEL)], out_specs=[_row_spec(t, D_MODEL), _acc_spec(D_MODEL)],
        out_shape=[jax.ShapeDtypeStruct((S, D_MODEL), F32), jax.ShapeDtypeStruct((1, D_MODEL), F32)],
        compiler_params=_params("arbitrary"))(dx1, dh, x, g)


def _gain_grad(dy, x, name):
    R, W = x.shape

    def body(dy_ref, x_ref, dg_ref):
        xv = x_ref[...]
        r = lax.rsqrt(jnp.mean(xv * xv, axis=-1, keepdims=True) + RMS_EPS)
        dg_ref[...] = jnp.sum(dy_ref[...] * (xv * r), axis=0, keepdims=True)

    return pl.pallas_call(body, name=name, grid=(1,), in_specs=[_full((R, W))] * 2, out_specs=_full((1, W)),
                          out_shape=jax.ShapeDtypeStruct((1, W), F32), compiler_params=_params("arbitrary"))(dy, x)


def _loss_head(f, x1, tgt, g):
    S = f.shape[0]
    t = ROW_TILE

    def body(f_ref, x1_ref, t_ref, g_ref, loss_ref, dx2_ref, df_ref, dg_ref):
        i = pl.program_id(0)

        @pl.when(i == 0)
        def _():
            loss_ref[...] = jnp.zeros_like(loss_ref)
            dg_ref[...] = jnp.zeros_like(dg_ref)

        fv, gv = f_ref[...], g_ref[...]
        err = x1_ref[...] + _rms_fwd_val(fv, gv) - t_ref[...]
        part = jnp.sum(jnp.mean(err * err, axis=-1, keepdims=True), axis=0, keepdims=True)
        loss_ref[...] += jnp.broadcast_to(0.5 * part, loss_ref.shape)
        dx2 = err * (1.0 / D_MODEL)
        dx2_ref[...] = dx2
        df, gg = _rms_bwd_val(dx2, fv, gv)
        df_ref[...] = df.astype(BF16)
        dg_ref[...] += jnp.sum(gg, axis=0, keepdims=True)

    return pl.pallas_call(
        body, name="loss_head", grid=(S // t,),
        in_specs=[_row_spec(t, D_MODEL)] * 3 + [_acc_spec(D_MODEL)],
        out_specs=[_acc_spec(LANES, 8), _row_spec(t, D_MODEL), _row_spec(t, D_MODEL), _acc_spec(D_MODEL)],
        out_shape=[jax.ShapeDtypeStruct((8, LANES), F32), jax.ShapeDtypeStruct((S, D_MODEL), F32),
                   jax.ShapeDtypeStruct((S, D_MODEL), BF16), jax.ShapeDtypeStruct((1, D_MODEL), F32)],
        compiler_params=_params("arbitrary"))(f, x1, tgt, g)


CONV_TC = 256
CONV_TT = 512
HALO = 8


def _shift_down(u, halo, first):
    row = lax.broadcasted_iota(jnp.int32, u.shape, 0)
    h6 = jnp.where(first, 0.0, halo[HALO - 2:HALO - 1, :])
    h7 = jnp.where(first, 0.0, halo[HALO - 1:HALO, :])
    s1 = jnp.where(row == 0, h7, pltpu.roll(u, 1, 0))
    s2 = jnp.where(row == 0, h6, jnp.where(row == 1, h7, pltpu.roll(u, 2, 0)))
    return s1, s2


def _conv_specs(tt, n_c, n_t, lead):
    def halo_row(i):
        return jnp.maximum(i * (tt // HALO) - 1, 0) if lead else jnp.minimum((i + 1) * (tt // HALO), n_t * (tt // HALO) - 1)
    return [
        pl.BlockSpec((tt, CONV_TC), lambda j, i: (i, j)),
        pl.BlockSpec((tt, CONV_TC), lambda j, i: (i, j + n_c)),
        pl.BlockSpec((HALO, CONV_TC), lambda j, i: (halo_row(i), j)),
        pl.BlockSpec((HALO, CONV_TC), lambda j, i: (halo_row(i), j + n_c)),
    ]


def _conv_z(ug, uv, hg, hv, w_g, w_v, b_g, b_v, first):
    g1, g2 = _shift_down(ug, hg, first)
    v1, v2 = _shift_down(uv, hv, first)
    zg = b_g + w_g[0:1, :] * g2
    zg = zg + w_g[1:2, :] * g1
    zg = zg + w_g[2:3, :] * ug
    zv = b_v + w_v[0:1, :] * v2
    zv = zv + w_v[1:2, :] * v1
    zv = zv + w_v[2:3, :] * uv
    return zg, zv, (g2, g1, ug), (v2, v1, uv)


def _conv_fwd(u, conv_w, conv_b):
    S = u.shape[0]
    tt = min(CONV_TT, S)
    n_c, n_t = D_FF // CONV_TC, S // tt
    wspec = [pl.BlockSpec((3, CONV_TC), lambda j, i: (0, j)), pl.BlockSpec((3, CONV_TC), lambda j, i: (0, j + n_c)),
             pl.BlockSpec((1, CONV_TC), lambda j, i: (0, j)), pl.BlockSpec((1, CONV_TC), lambda j, i: (0, j + n_c))]

    def body(ug_ref, uv_ref, hg_ref, hv_ref, wg_ref, wv_ref, bg_ref, bv_ref, a_ref):
        first = pl.program_id(1) == 0
        zg, zv, _, _ = _conv_z(ug_ref[...], uv_ref[...], hg_ref, hv_ref, wg_ref, wv_ref, bg_ref[...], bv_ref[...], first)
        a_ref[...] = (zg * jax.nn.sigmoid(zg) * zv).astype(BF16)

    return pl.pallas_call(
        body, name="conv_fwd", grid=(n_c, n_t), in_specs=_conv_specs(tt, n_c, n_t, True) + wspec,
        out_specs=pl.BlockSpec((tt, CONV_TC), lambda j, i: (i, j)), out_shape=jax.ShapeDtypeStruct((S, D_FF), BF16),
        compiler_params=_params("parallel", "parallel"))(u, u, u, u, conv_w, conv_w, conv_b, conv_b)


def _conv_bwd_dz(da, u, conv_w, conv_b):
    S = u.shape[0]
    tt = min(CONV_TT, S)
    n_c, n_t = D_FF // CONV_TC, S // tt
    wspec = [pl.BlockSpec((3, CONV_TC), lambda j, i: (0, j)), pl.BlockSpec((3, CONV_TC), lambda j, i: (0, j + n_c)),
             pl.BlockSpec((1, CONV_TC), lambda j, i: (0, j)), pl.BlockSpec((1, CONV_TC), lambda j, i: (0, j + n_c))]
    tile = pl.BlockSpec((tt, CONV_TC), lambda j, i: (i, j))
    tile_v = pl.BlockSpec((tt, CONV_TC), lambda j, i: (i, j + n_c))

    def body(da_ref, ug_ref, uv_ref, hg_ref, hv_ref, wg_ref, wv_ref, bg_ref, bv_ref,
             dzg_ref, dzv_ref, dwg_ref, dwv_ref, dbg_ref, dbv_ref):
        i = pl.program_id(1)
        zg, zv, gs, vs = _conv_z(ug_ref[...], uv_ref[...], hg_ref, hv_ref, wg_ref, wv_ref, bg_ref[...], bv_ref[...], i == 0)
        dav = da_ref[...]
        sg = jax.nn.sigmoid(zg)
        dzv = dav * (zg * sg)
        dzg = dav * zv * (sg * (1.0 + zg * (1.0 - sg)))
        dzg_ref[...] = dzg
        dzv_ref[...] = dzv

        @pl.when(i == 0)
        def _():
            for r in (dwg_ref, dwv_ref, dbg_ref, dbv_ref):
                r[...] = jnp.zeros_like(r)

        for k in range(3):
            dwg_ref[k:k + 1, :] += jnp.sum(dzg * gs[k], axis=0, keepdims=True)
            dwv_ref[k:k + 1, :] += jnp.sum(dzv * vs[k], axis=0, keepdims=True)
        dbg_ref[...] += jnp.sum(dzg, axis=0, keepdims=True)
        dbv_ref[...] += jnp.sum(dzv, axis=0, keepdims=True)

    outs = pl.pallas_call(
        body, name="conv_bwd_dz", grid=(n_c, n_t), in_specs=[tile] + _conv_specs(tt, n_c, n_t, True) + wspec,
        out_specs=[tile, tile] + [pl.BlockSpec((3, CONV_TC), lambda j, i: (0, j))] * 2 + [pl.BlockSpec((1, CONV_TC), lambda j, i: (0, j))] * 2,
        out_shape=[jax.ShapeDtypeStruct((S, D_FF), F32)] * 2 + [jax.ShapeDtypeStruct((3, D_FF), F32)] * 2
        + [jax.ShapeDtypeStruct((1, D_FF), F32)] * 2,
        compiler_params=_params("parallel", "arbitrary"))(da, u, u, u, u, conv_w, conv_w, conv_b, conv_b)
    dzg, dzv, dwg, dwv, dbg, dbv = outs
    return dzg, dzv, jnp.concatenate([dwg, dwv], axis=1), jnp.concatenate([dbg, dbv], axis=1)


def _conv_bwd_du(dzg, dzv, conv_w):
    S = dzg.shape[0]
    tt = min(CONV_TT, S)
    n_c, n_t = D_FF // CONV_TC, S // tt
    steps = tt // HALO

    def nxt(i):
        return jnp.minimum((i + 1) * steps, n_t * steps - 1)

    tile_g = pl.BlockSpec((tt, CONV_TC), lambda h, j, i: (i * (1 - h), j * (1 - h)))
    tile_v = pl.BlockSpec((tt, CONV_TC), lambda h, j, i: (i * h, j * h))
    halo_g = pl.BlockSpec((HALO, CONV_TC), lambda h, j, i: (nxt(i) * (1 - h), j * (1 - h)))
    halo_v = pl.BlockSpec((HALO, CONV_TC), lambda h, j, i: (nxt(i) * h, j * h))
    wsp = pl.BlockSpec((3, CONV_TC), lambda h, j, i: (0, j + h * n_c))

    def body(zg_ref, zv_ref, hg_ref, hv_ref, w_ref, du_ref):
        h = pl.program_id(0)
        last = pl.program_id(2) == n_t - 1
        z = jnp.where(h == 0, zg_ref[...], zv_ref[...])
        hal = jnp.where(h == 0, hg_ref[...], hv_ref[...])
        row = lax.broadcasted_iota(jnp.int32, z.shape, 0)
        h0 = jnp.where(last, 0.0, hal[0:1, :])
        h1 = jnp.where(last, 0.0, hal[1:2, :])
        u1 = jnp.where(row == tt - 1, h0, pltpu.roll(z, tt - 1, 0))
        u2 = jnp.where(row == tt - 1, h1, jnp.where(row == tt - 2, h0, pltpu.roll(z, tt - 2, 0)))
        du = w_ref[2:3, :] * z + w_ref[1:2, :] * u1 + w_ref[0:1, :] * u2
        du_ref[...] = du.astype(BF16)

    return pl.pallas_call(
        body, name="conv_bwd_du", grid=(2, n_c, n_t), in_specs=[tile_g, tile_v, halo_g, halo_v, wsp],
        out_specs=pl.BlockSpec((tt, CONV_TC), lambda h, j, i: (i, j + h * n_c)),
        out_shape=jax.ShapeDtypeStruct((S, 2 * D_FF), BF16),
        compiler_params=_params("parallel", "parallel", "parallel"))(dzg, dzv, dzg, dzv, conv_w)


BAND = 128


MEM_TQ = 512


def _mem_fwd(q, kv, *, scale, name):
    S, W = q.shape
    M = kv.shape[0]
    nh = W // LANES
    tq = min(MEM_TQ, S)

    def body(q_ref, k_ref, v_ref, o_ref, l_ref):
        qv = (q_ref[...] * scale).astype(BF16)
        s = lax.dot_general(qv, k_ref[...].astype(BF16), NT, preferred_element_type=F32)
        m = jnp.max(s, axis=1, keepdims=True)
        p = jnp.exp(s - m)
        l = jnp.sum(p, axis=1, keepdims=True)
        o_ref[...] = (lax.dot_general(p.astype(BF16), v_ref[...].astype(BF16), NN, preferred_element_type=F32) / l).astype(BF16)
        l_ref[...] = jnp.broadcast_to(m + jnp.log(l), (tq, LANES))

    blk = pl.BlockSpec((tq, LANES), lambda hh, i: (i, hh))
    return pl.pallas_call(
        body, name=name, grid=(nh, S // tq),
        in_specs=[blk, pl.BlockSpec((M, LANES), lambda hh, i: (0, hh)), pl.BlockSpec((M, LANES), lambda hh, i: (0, hh + nh))],
        out_specs=[blk, blk], out_shape=[jax.ShapeDtypeStruct((S, W), BF16), jax.ShapeDtypeStruct((S, W), F32)],
        compiler_params=_params("parallel", "parallel"))(q, kv, kv)


def _mem_bwd(q, kv, do, lse, delta, *, scale, name):
    S, W = q.shape
    M = kv.shape[0]
    nh = W // LANES
    tq = min(MEM_TQ, S)

    def body(q_ref, k_ref, v_ref, do_ref, l_ref, d_ref, dq_ref, dk_ref, dv_ref):
        i = pl.program_id(1)

        @pl.when(i == 0)
        def _():
            dk_ref[...] = jnp.zeros_like(dk_ref)
            dv_ref[...] = jnp.zeros_like(dv_ref)

        qv = (q_ref[...] * scale).astype(BF16)
        kv_, vv = k_ref[...].astype(BF16), v_ref[...].astype(BF16)
        dov = do_ref[...].astype(BF16)
        s = lax.dot_general(qv, kv_, NT, preferred_element_type=F32)
        p = jnp.exp(s - l_ref[...][:, 0:1])
        dp = lax.dot_general(dov, vv, NT, preferred_element_type=F32)
        ds = (p * (dp - d_ref[...][:, 0:1])).astype(BF16)
        dq_ref[...] = lax.dot_general(ds, kv_, NN, preferred_element_type=F32) * scale
        dk_ref[...] += lax.dot_general(ds, qv, TN, preferred_element_type=F32)
        dv_ref[...] += lax.dot_general(p.astype(BF16), dov, TN, preferred_element_type=F32)

    blk = pl.BlockSpec((tq, LANES), lambda hh, i: (i, hh))
    kblk = pl.BlockSpec((M, LANES), lambda hh, i: (0, hh))
    vblk = pl.BlockSpec((M, LANES), lambda hh, i: (0, hh + nh))
    dq, dk, dv = pl.pallas_call(
        body, name=name, grid=(nh, S // tq), in_specs=[blk, kblk, vblk, blk, blk, blk], out_specs=[blk, kblk, kblk],
        out_shape=[jax.ShapeDtypeStruct((S, W), F32), jax.ShapeDtypeStruct((M, W), F32), jax.ShapeDtypeStruct((M, W), F32)],
        compiler_params=_params("parallel", "arbitrary"))(q, kv, kv, do, lse, delta)
    return dq, jnp.concatenate([dk, dv], axis=1)


CAUSAL_T = 256


def _causal_fwd(q, k, v, *, name):
    S, W = q.shape
    T = min(CAUSAL_T, S // 2)
    tq = 2 * T

    def body(q_ref, k_ref, v_ref, o_ref, l_ref):
        i = pl.program_id(1)
        qa, qb = q_ref[0:T, :], q_ref[T:tq, :]

        def kv(j):
            rows = pl.ds(pl.multiple_of(j * T, T), T)
            return k_ref[rows, :], v_ref[rows, :]

        def upd(qv, kj, vj, carry, diag):
            m, l, acc = carry
            s = lax.dot_general(qv, kj, NT, preferred_element_type=F32)
            if diag:
                s = jnp.where(lax.broadcasted_iota(jnp.int32, s.shape, 0) >= lax.broadcasted_iota(jnp.int32, s.shape, 1), s, NEG_INF)
            m_new = jnp.maximum(m, jnp.max(s, axis=1, keepdims=True))
            alpha = jnp.exp(m - m_new)
            p = jnp.exp(s - m_new)
            l = alpha * l + jnp.sum(p, axis=1, keepdims=True)
            acc = alpha * acc + lax.dot_general(p.astype(BF16), vj, NN, preferred_element_type=F32)
            return m_new, l, acc

        init = (jnp.full((T, 1), NEG_INF, F32), jnp.zeros((T, 1), F32), jnp.zeros((T, LANES), F32))

        def both(j, c):
            kj, vj = kv(j)
            return upd(qa, kj, vj, c[0], False), upd(qb, kj, vj, c[1], False)

        ca, cb = lax.fori_loop(0, 2 * i, both, (init, init))
        kj, vj = kv(2 * i)
        ca = upd(qa, kj, vj, ca, True)
        cb = upd(qb, kj, vj, cb, False)
        kj, vj = kv(2 * i + 1)
        cb = upd(qb, kj, vj, cb, True)
        for (m, l, acc), sl in ((ca, slice(0, T)), (cb, slice(T, tq))):
            o_ref[sl, :] = (acc / l).astype(BF16)
            l_ref[sl, :] = jnp.broadcast_to(m + jnp.log(l), (T, LANES))

    blk = pl.BlockSpec((tq, LANES), lambda hh, i: (i, hh))
    whole = pl.BlockSpec((S, LANES), lambda hh, i: (0, hh))
    return pl.pallas_call(
        body, name=name, grid=(W // LANES, S // tq), in_specs=[blk, whole, whole], out_specs=[blk, blk],
        out_shape=[jax.ShapeDtypeStruct((S, W), BF16), jax.ShapeDtypeStruct((S, W), F32)],
        compiler_params=_params("parallel", "arbitrary"))(q, k, v)


def _causal_bwd(q, k, v, do, lse, delta, *, name):
    S, W = q.shape
    T = min(CAUSAL_T, S // 2)
    tk = 2 * T
    nq = S // T

    def body(q_ref, k_ref, v_ref, do_ref, l_ref, d_ref, dq_ref, dk_ref, dv_ref):
        j = pl.program_id(1)

        @pl.when(j == 0)
        def _():
            dq_ref[...] = jnp.zeros_like(dq_ref)

        ka, va, kb, vb = k_ref[0:T, :], v_ref[0:T, :], k_ref[T:tk, :], v_ref[T:tk, :]

        def chunk(i):
            rows = pl.ds(pl.multiple_of(i * T, T), T)
            return rows, q_ref[rows, :], do_ref[rows, :].astype(BF16), l_ref[rows, :][:, 0:1], d_ref[rows, :][:, 0:1]

        def half(qi, doi, li, di, kh, vh, diag):
            s = lax.dot_general(qi, kh, NT, preferred_element_type=F32)
            if diag:
                s = jnp.where(lax.broadcasted_iota(jnp.int32, s.shape, 0) >= lax.broadcasted_iota(jnp.int32, s.shape, 1), s, NEG_INF)
            p = jnp.exp(s - li)
            dp = lax.dot_general(doi, vh, NT, preferred_element_type=F32)
            ds = (p * (dp - di)).astype(BF16)
            return (lax.dot_general(ds, qi, TN, preferred_element_type=F32),
                    lax.dot_general(p.astype(BF16), doi, TN, preferred_element_type=F32),
                    lax.dot_general(ds, kh, NN, preferred_element_type=F32))

        rows, qi, doi, li, di = chunk(2 * j)
        dka, dva, dq = half(qi, doi, li, di, ka, va, True)
        dq_ref[rows, :] += dq
        rows, qi, doi, li, di = chunk(2 * j + 1)
        a = half(qi, doi, li, di, ka, va, False)
        dkb, dvb, dq = half(qi, doi, li, di, kb, vb, True)
        dq_ref[rows, :] += a[2] + dq
        dka, dva = dka + a[0], dva + a[1]

        def both(i, c):
            rows, qi, doi, li, di = chunk(i)
            a = half(qi, doi, li, di, ka, va, False)
            b = half(qi, doi, li, di, kb, vb, False)
            dq_ref[rows, :] += a[2] + b[2]
            return c[0] + a[0], c[1] + a[1], c[2] + b[0], c[3] + b[1]

        dka, dva, dkb, dvb = lax.fori_loop(2 * j + 2, nq, both, (dka, dva, dkb, dvb))
        dk_ref[0:T, :], dv_ref[0:T, :], dk_ref[T:tk, :], dv_ref[T:tk, :] = dka, dva, dkb, dvb

    blk = pl.BlockSpec((tk, LANES), lambda hh, j: (j, hh))
    whole = pl.BlockSpec((S, LANES), lambda hh, j: (0, hh))
    return pl.pallas_call(
        body, name=name, grid=(W // LANES, S // tk), in_specs=[whole, blk, blk, whole, whole, whole],
        out_specs=[whole, blk, blk], out_shape=[jax.ShapeDtypeStruct((S, W), F32)] * 3,
        compiler_params=_params("parallel", "arbitrary"))(q, k, v, do, lse, delta)


BAND_TQ = 512


def _band_window(i, sub, nsub, L, q_ref, k_ref, v_ref, slope, scale):
    kw = min(2 * BAND, L)
    n = i * nsub + sub
    k0 = 0 if kw == L else pl.multiple_of(jnp.maximum(n - 1, 0) * BAND, BAND)
    win = pl.ds(k0, kw)
    qs = (q_ref[sub * BAND:(sub + 1) * BAND, :].astype(F32) * scale).astype(BF16)
    kwv, vwv = k_ref[win, :].astype(BF16), v_ref[win, :].astype(BF16)
    s = lax.dot_general(qs, kwv, NT, preferred_element_type=F32)
    dist = (n * BAND + lax.broadcasted_iota(jnp.int32, s.shape, 0)) - (k0 + lax.broadcasted_iota(jnp.int32, s.shape, 1))
    s = jnp.where((dist >= 0) & (dist <= BAND), s - slope * dist.astype(F32), NEG_INF)
    return win, qs, kwv, vwv, s


def _band_fwd(q, k, v, slopes, *, n_heads, qcol, kcol, vcol, L, scale, slope_mul, o_shape, name):
    tq = min(BAND_TQ, L)
    nsub = tq // BAND

    def body(sl_ref, q_ref, k_ref, v_ref, o_ref, l_ref):
        hh, i = pl.program_id(0), pl.program_id(1)
        slope = sl_ref[hh % DIL_HPG] * slope_mul
        for sub in range(nsub):
            _, _, _, vwv, s = _band_window(i, sub, nsub, L, q_ref, k_ref, v_ref, slope, scale)
            m = jnp.max(s, axis=1, keepdims=True)
            p = jnp.exp(s - m)
            l = jnp.sum(p, axis=1, keepdims=True)
            rows = slice(sub * BAND, (sub + 1) * BAND)
            o_ref[rows, :] = lax.dot_general(p.astype(BF16), vwv, NN, preferred_element_type=F32) / l
            l_ref[rows, :] = jnp.broadcast_to(m + jnp.log(l), (BAND, LANES))

    whole = lambda col: pl.BlockSpec((L, LANES), lambda hh, i: (0, col(hh)))
    o_spec = pl.BlockSpec((tq, LANES), lambda hh, i: (i, hh))
    return pl.pallas_call(
        body, name=name, grid=(n_heads, L // tq),
        in_specs=[pl.BlockSpec(memory_space=pltpu.SMEM), pl.BlockSpec((tq, LANES), lambda hh, i: (i, qcol(hh))), whole(kcol), whole(vcol)],
        out_specs=[o_spec, o_spec], out_shape=[jax.ShapeDtypeStruct(o_shape, F32)] * 2,
        compiler_params=_params("parallel", "arbitrary"))(slopes, q, k, v)


def _band_bwd(q, k, v, do, lse, delta, slopes, *, n_heads, qcol, kcol, vcol, L, scale, slope_mul, d_shape, name):
    tq = min(BAND_TQ, L)
    nsub = tq // BAND

    def body(sl_ref, q_ref, k_ref, v_ref, do_ref, l_ref, d_ref, dq_ref, dk_ref, dv_ref):
        hh, i = pl.program_id(0), pl.program_id(1)
        slope = sl_ref[hh % DIL_HPG] * slope_mul

        @pl.when(i == 0)
        def _():
            dk_ref[...] = jnp.zeros_like(dk_ref)
            dv_ref[...] = jnp.zeros_like(dv_ref)

        for sub in range(nsub):
            win, qs, kwv, vwv, s = _band_window(i, sub, nsub, L, q_ref, k_ref, v_ref, slope, scale)
            rows = slice(sub * BAND, (sub + 1) * BAND)
            dos = do_ref[rows, :].astype(BF16)
            p = jnp.exp(s - l_ref[rows, :][:, 0:1])
            dp = lax.dot_general(dos, vwv, NT, preferred_element_type=F32)
            ds = (p * (dp - d_ref[rows, :][:, 0:1])).astype(BF16)
            dq_ref[rows, :] = lax.dot_general(ds, kwv, NN, preferred_element_type=F32) * scale
            dk_ref[win, :] += lax.dot_general(ds, qs, TN, preferred_element_type=F32)
            dv_ref[win, :] += lax.dot_general(p.astype(BF16), dos, TN, preferred_element_type=F32)

    whole = lambda col: pl.BlockSpec((L, LANES), lambda hh, i: (0, col(hh)))
    blk = pl.BlockSpec((tq, LANES), lambda hh, i: (i, hh))
    ident = lambda hh: hh
    return pl.pallas_call(
        body, name=name, grid=(n_heads, L // tq),
        in_specs=[pl.BlockSpec(memory_space=pltpu.SMEM), pl.BlockSpec((tq, LANES), lambda hh, i: (i, qcol(hh))), whole(kcol), whole(vcol),
                  blk, blk, blk],
        out_specs=[blk, whole(ident), whole(ident)], out_shape=[jax.ShapeDtypeStruct(d_shape, F32)] * 3,
        compiler_params=_params("parallel", "arbitrary"))(slopes, q, k, v, do, lse, delta)


def _pad_heads(w, n_heads, width, axis):
    shp = w.shape
    new = shp[:axis] + (n_heads, width) + shp[axis + 1:]
    pad = [(0, 0)] * len(new)
    pad[axis + 1] = (0, LANES - width)
    out = jnp.pad(w.reshape(new), pad)
    return out.reshape(shp[:axis] + (n_heads * LANES,) + shp[axis + 1:])


def _unpad_heads(w, n_heads, width, axis):
    shp = w.shape
    new = shp[:axis] + (n_heads, LANES) + shp[axis + 1:]
    out = lax.slice_in_dim(w.reshape(new), 0, width, axis=axis + 1)
    return out.reshape(shp[:axis] + (n_heads * width,) + shp[axis + 1:])


def _alibi_slopes():
    s = jnp.exp2(-8.0 * jnp.arange(1, DIL_HEADS + 1, dtype=F32) / DIL_HEADS)
    return s.reshape(DIL_HPG, DIL_GROUPS).T


def _local_step(x, mem, positions, tgt, W):
    S = x.shape[0]
    pos = positions.reshape(S, 1).astype(F32)
    half = MLA_ROPE // 2
    inv_freq = ROPE_THETA ** (-jnp.arange(half, dtype=F32) / half)
    invf = jnp.zeros((1, LANES), F32).at[0, MLA_NOPE:MLA_NOPE + half].set(inv_freq).at[0, MLA_NOPE + half:MLA_QK].set(inv_freq)
    slopes = _alibi_slopes()

    w_in = W["w_in"]
    zc = lambda n: jnp.zeros((D_MODEL, n), BF16)
    w_a = jnp.concatenate([w_in[:, :OFF_Q], zc(MLA_NOPE), w_in[:, OFF_KV:OFF_KR], zc(LANES - MLA_QK), w_in[:, OFF_Q:OFF_KV]], axis=1)
    w_d, w_m, w_g = w_in[:, OFF_KR:OFF_DIL], w_in[:, OFF_DIL:OFF_MEMQ], w_in[:, OFF_MEMQ:]
    w_uq_p = _pad_heads(W["w_uq"], MLA_HEADS, MLA_QK, 1)
    ukv = W["w_ukv"].reshape(KV_RANK, MLA_HEADS, 2 * MLA_NOPE)
    w_uk_p = _pad_heads(ukv[:, :, :MLA_NOPE].reshape(KV_RANK, -1), MLA_HEADS, MLA_NOPE, 1)
    w_uv_p = _pad_heads(ukv[:, :, MLA_NOPE:].reshape(KV_RANK, -1), MLA_HEADS, MLA_NOPE, 1)
    w_br_mla_p = _pad_heads(W["w_br_mla"], MLA_HEADS, MLA_NOPE, 0)

    h = _rmsnorm(x, W["g_pre_mix"], BF16, "norm1")
    proj_a = _matmul(h, w_a, tn=768, name="proj_a")
    proj_d = _matmul(h, w_d, tn=768, name="proj_d")
    proj_m = _matmul(h, w_m, tn=512, name="proj_m")
    proj_g = _matmul(h, w_g, tn=768, name="proj_g")

    cq_n, ckv_n, kpe = _mla_prep(proj_a, pos, invf, W["mla_q_norm"], W["mla_kv_norm"])
    q_pre = _matmul(cq_n, w_uq_p, tn=1024, name="mla_q")
    k_pre = _matmul(ckv_n, w_uk_p, tn=1024, name="mla_k")
    v_mla = _matmul(ckv_n, w_uv_p, tn=1024, out_dtype=BF16, name="mla_v")
    q_mla, k_mla = _qk_final(q_pre, k_pre, kpe, pos, invf)
    ident = lambda hh: hh
    o_mla, lse_mla = _causal_fwd(q_mla, k_mla, v_mla, name="attn_mla_fwd")

    n_dil_cols = 3 * DIL_HEADS
    o_dil, lse_dil = [], []
    for g, (window, dil) in enumerate(DIL_PAIRS):
        L = S // dil
        view = proj_d.reshape(L, dil * 3 * DIL_HEADS * LANES)
        col = lambda part, g=g: (lambda hh: (hh // DIL_HPG) * n_dil_cols + (part * DIL_GROUPS + g) * DIL_HPG + hh % DIL_HPG)
        o, lse = _band_fwd(view, view, view, slopes[g], n_heads=dil * DIL_HPG, qcol=col(0), kcol=col(1), vcol=col(2), L=L,
                           scale=LANES ** -0.5, slope_mul=float(dil), o_shape=(L, dil * DIL_W), name=f"attn_dil{g}_fwd")
        o_dil.append(o.reshape(S, DIL_W))
        lse_dil.append(lse.reshape(S, DIL_W))
    y_dil = _dil_mix(o_dil, lse_dil)

    mem_n = _rmsnorm(mem, W["g_mem"], BF16, "mem_norm")
    kv_mem = _matmul(mem_n, W["w_mem_kv"], tm=256, tn=512, name="mem_kv")
    o_mem, lse_mem = _mem_fwd(proj_m, kv_mem, scale=LANES ** -0.5, name="attn_mem_fwd")

    b_mla = _matmul(o_mla, w_br_mla_p, name="br_mla")
    b_dil = _matmul(y_dil, W["w_br_dil"], name="br_dil")
    b_mem = _matmul(o_mem, W["w_br_mem"], name="br_mem")
    merged = _merge(proj_g, W["b_gate"], [b_mla, b_dil, b_mem])
    o_proj = _matmul(merged, W["w_o"], name="o_proj")
    x1, h2 = _norm2(o_proj, x, W["g_post_mix"], W["g_pre_ffn"])

    u = _matmul(h2, W["w_ffn_up"], tn=512, name="ffn_up")
    act = _conv_fwd(u, W["conv_w"], W["conv_b"])
    f = _matmul(act, W["w_ffn_down"], tk=1408, name="ffn_down")
    loss8, dx2, df, dg_post_ffn = _loss_head(f, x1, tgt, W["g_post_ffn"])
    loss = loss8[0, 0]

    G = {"g_post_ffn": dg_post_ffn}
    d_act = _matmul(df, W["w_ffn_down"], mode="nt", tn=1408, name="d_act")
    G["w_ffn_down"] = _matmul(act, df, mode="tn", tm=1408, tk=512, name="dw_ffn_down")
    dzg, dzv, G["conv_w"], G["conv_b"] = _conv_bwd_dz(d_act, u, W["conv_w"], W["conv_b"])
    du = _conv_bwd_du(dzg, dzv, W["conv_w"])
    dh2 = _matmul(du, W["w_ffn_up"], mode="nt", tk=1408, name="d_h2")
    G["w_ffn_up"] = _matmul(h2, du, mode="tn", tk=512, name="dw_ffn_up")
    dx1, do_proj, G["g_pre_ffn"], G["g_post_mix"] = _norm2_bwd(dx2, dh2, x1, o_proj, W["g_pre_ffn"], W["g_post_mix"])
    dmerged = _matmul(do_proj, W["w_o"], mode="nt", name="d_merged")
    G["w_o"] = _matmul(merged, do_proj, mode="tn", tk=512, name="dw_o")
    db_mla, db_dil, db_mem, dproj_g, G["b_gate"] = _merge_bwd(dmerged, proj_g, W["b_gate"], [b_mla, b_dil, b_mem])

    dy_mem = _matmul(db_mem, W["w_br_mem"], mode="nt", name="d_y_mem")
    G["w_br_mem"] = _matmul(o_mem, db_mem, mode="tn", tk=512, name="dw_br_mem")
    delta_mem = _delta(dy_mem, o_mem, MEM_HEADS, "delta_mem")
    dq_mem, dkv_mem = _mem_bwd(proj_m, kv_mem, dy_mem, lse_mem, delta_mem, scale=LANES ** -0.5, name="attn_mem_bwd")
    G["w_mem_kv"] = _matmul(mem_n, dkv_mem, mode="tn", name="dw_mem_kv")
    dmem_n = _matmul(dkv_mem, W["w_mem_kv"], mode="nt", tm=256, name="d_mem_n")
    G["g_mem"] = _gain_grad(dmem_n, mem, "dg_mem")

    dy_dil = _matmul(db_dil, W["w_br_dil"], mode="nt", name="d_y_dil")
    G["w_br_dil"] = _matmul(y_dil, db_dil, mode="tn", tk=512, name="dw_br_dil")
    mix = _dil_mix_bwd(dy_dil, o_dil, lse_dil)
    do_dil, dl_dil = mix[:3], mix[3:]
    d_parts = [[None] * DIL_GROUPS for _ in range(3)]
    for g, (window, dil) in enumerate(DIL_PAIRS):
        L = S // dil
        view = proj_d.reshape(L, dil * 3 * DIL_HEADS * LANES)
        col = lambda part, g=g: (lambda hh: (hh // DIL_HPG) * n_dil_cols + (part * DIL_GROUPS + g) * DIL_HPG + hh % DIL_HPG)
        shp = (L, dil * DIL_W)
        dq, dk, dv = _band_bwd(
            view, view, view, do_dil[g].reshape(shp), lse_dil[g].reshape(shp), dl_dil[g].reshape(shp), slopes[g],
            n_heads=dil * DIL_HPG, qcol=col(0), kcol=col(1), vcol=col(2), L=L, scale=LANES ** -0.5, slope_mul=float(dil),
            d_shape=shp, name=f"attn_dil{g}_bwd")
        for part, d in enumerate((dq, dk, dv)):
            d_parts[part][g] = d.reshape(S, DIL_W).astype(BF16)
    dproj_d = jnp.concatenate([d for part in d_parts for d in part], axis=1)

    dy_mla = _matmul(db_mla, w_br_mla_p, mode="nt", name="d_y_mla")
    dw_br_mla_p = _matmul(o_mla, db_mla, mode="tn", tk=512, name="dw_br_mla")
    G["w_br_mla"] = _unpad_heads(dw_br_mla_p, MLA_HEADS, MLA_NOPE, 0)
    delta_mla = _delta(dy_mla, o_mla, MLA_HEADS, "delta_mla")
    dq_mla, dk_mla, dv_mla = _causal_bwd(q_mla, k_mla, v_mla, dy_mla, lse_mla, delta_mla, name="attn_mla_bwd")
    dq_pre, dkpe = _mla_bwd_prep(dq_mla, dk_mla, pos, invf)
    dcq_n = _matmul(dq_pre, w_uq_p, mode="nt", tn=Q_RANK, name="d_cq")
    G["w_uq"] = _unpad_heads(_matmul(cq_n, dq_pre, mode="tn", tm=Q_RANK, tk=512, name="dw_uq"), MLA_HEADS, MLA_QK, 1)
    dckv_a = _matmul(dk_mla, w_uk_p, mode="nt", tn=KV_RANK, name="d_ckv_k")
    dckv_b = _matmul(dv_mla, w_uv_p, mode="nt", tn=KV_RANK, name="d_ckv_v")
    dw_uk = _unpad_heads(_matmul(ckv_n, dk_mla, mode="tn", tm=KV_RANK, tk=512, name="dw_uk"), MLA_HEADS, MLA_NOPE, 1)
    dw_uv = _unpad_heads(_matmul(ckv_n, dv_mla, mode="tn", tm=KV_RANK, tk=512, name="dw_uv"), MLA_HEADS, MLA_NOPE, 1)
    G["w_ukv"] = jnp.concatenate([dw_uk.reshape(KV_RANK, MLA_HEADS, MLA_NOPE), dw_uv.reshape(KV_RANK, MLA_HEADS, MLA_NOPE)],
                                 axis=2).reshape(KV_RANK, -1)
    dproj_a, G["mla_q_norm"], G["mla_kv_norm"] = _mla_norm_bwd(dcq_n, dckv_a, dckv_b, dkpe, proj_a, W["mla_q_norm"],
                                                              W["mla_kv_norm"])

    dh = _matmul(dproj_a, w_a, mode="nt", name="d_h_a")
    dh = _matmul(dproj_d, w_d, mode="nt", tk=1536, add=dh, name="d_h_d")
    dh = _matmul(dq_mem, w_m, mode="nt", add=dh, name="d_h_m")
    dh = _matmul(dproj_g, w_g, mode="nt", tk=1536, add=dh, name="d_h_g")
    dw_a = _matmul(h, dproj_a, mode="tn", tn=768, tk=512, name="dw_in_a")
    dw_d = _matmul(h, dproj_d, mode="tn", tn=768, tk=512, name="dw_in_d")
    dw_m = _matmul(h, dq_mem, mode="tn", tn=512, tk=512, name="dw_in_m")
    dw_g = _matmul(h, dproj_g, mode="tn", tn=768, tk=512, name="dw_in_g")
    kr0 = Q_RANK + MLA_NOPE
    G["w_in"] = jnp.concatenate([dw_a[:, :Q_RANK], dw_a[:, Q_RANK + LANES:], dw_a[:, kr0:kr0 + MLA_ROPE], dw_d, dw_m, dw_g], axis=1)
    grad_x, G["g_pre_mix"] = _norm1_bwd(dx1, dh, x, W["g_pre_mix"])
    return loss, grad_x, G


WEIGHTS = ["g_pre_mix", "w_in", "b_gate", "mla_q_norm", "w_uq", "mla_kv_norm", "w_ukv", "g_mem", "w_mem_kv", "w_br_mla",
           "w_br_dil", "w_br_mem", "w_o", "g_post_mix", "g_pre_ffn", "w_ffn_up", "conv_w", "conv_b", "w_ffn_down", "g_post_ffn"]
GROUPS = [
    [("w_in", (D_MODEL, D_IN), 1)],
    [("w_uq", (Q_RANK, MLA_HEADS * MLA_QK), 1)],
    [("w_ukv", (KV_RANK, MLA_HEADS * 2 * MLA_NOPE), 1), ("w_br_mla", (MLA_HEADS * MLA_NOPE, D_MODEL), 1),
     ("w_br_dil", (DIL_W, D_MODEL), 1), ("w_br_mem", (MEM_W, D_MODEL), 1)],
    [("w_mem_kv", (D_MODEL, 2 * MEM_W), 0), ("w_o", (D_MODEL, D_MODEL), 0), ("w_ffn_down", (D_FF, D_MODEL), 0)],
    [("w_ffn_up", (D_MODEL, 2 * D_FF), 1)],
]
CONV_W = ("conv_w", (3, 2 * D_FF), 1)
REPLICATED = [("g_pre_mix", D_MODEL), ("b_gate", 3 * D_MODEL), ("mla_q_norm", Q_RANK), ("mla_kv_norm", KV_RANK), ("g_mem", D_MODEL),
              ("g_post_mix", D_MODEL), ("g_pre_ffn", D_MODEL), ("conv_b", 2 * D_FF), ("g_post_ffn", D_MODEL)]
SMALL_ROWS = 256
CONV_AT = sum(n for _, n in REPLICATED)
LOSS_AT = CONV_AT + 3 * 2 * D_FF


def _shard_shape(shape, axis):
    return tuple(d // N_CHIPS if a == axis else d for a, d in enumerate(shape))


def _group_shape(grp):
    shapes = [_shard_shape(shape, axis) for _, shape, axis in grp]
    assert len({s[1] for s in shapes}) == 1
    return sum(s[0] for s in shapes), shapes[0][1]


def _member_shards(a, axis):
    r, c = a.shape
    if axis == 0:
        return a.reshape(N_CHIPS, r // N_CHIPS, c)
    return a.reshape(r, N_CHIPS, c // N_CHIPS).transpose(1, 0, 2)


def _member_full(s, axis):
    n, r, c = s.shape
    if axis == 0:
        return s.reshape(n * r, c)
    return s.transpose(1, 0, 2).reshape(r, n * c)


def _my_weight_groups(w):
    out = []
    for grp in GROUPS:
        rows, width = _group_shape(grp)
        out.append(jnp.concatenate([w[name].astype(BF16) for name, _, _ in grp], axis=0).reshape(2, rows // 2, width))
    return out


def _full_weights(gathered, conv_all):
    out = {}
    for grp, ga in zip(GROUPS, gathered):
        ga = ga.reshape(N_CHIPS, -1, ga.shape[-1])
        off = 0
        for name, shape, axis in grp:
            rows = _shard_shape(shape, axis)[0]
            out[name] = _member_full(ga[:, off:off + rows], axis)
            off += rows
    out[CONV_W[0]] = _member_full(conv_all, CONV_W[2])
    return out


def _grad_groups(G):
    out = []
    for grp in GROUPS:
        rows, width = _group_shape(grp)
        a = jnp.concatenate([_member_shards(G[name], axis) for name, _, axis in grp], axis=1)
        out.append(a.reshape(N_CHIPS, 2, rows // 2, width).transpose(1, 0, 2, 3))
    return out


def _pack_small(vals, conv_g=None, loss=None):
    parts = [vals[name].reshape(-1) for name, _ in REPLICATED]
    if conv_g is not None:
        parts += [conv_g.reshape(-1), loss.reshape(1)]
    flat = jnp.concatenate(parts)
    return jnp.pad(flat, (0, SMALL_ROWS * LANES - flat.shape[0])).reshape(SMALL_ROWS, LANES)


def _unpack_small(packed):
    flat = packed.reshape(-1)
    out, off = {}, 0
    for name, n in REPLICATED:
        out[name] = flat[off:off + n].reshape(1, n)
        off += n
    return out


MESH = pl.DeviceIdType.MESH
HBM_SPEC = pl.BlockSpec(memory_space=pltpu.HBM)


def _place():
    x, y, c = lax.axis_index("x"), lax.axis_index("y"), lax.axis_index("c")
    chips = [(1 - x, y), (x, 1 - y), (1 - x, 1 - y)]
    return x, y, c, chips


def _remote(src, dst, send_sems, recv_sems, k, to):
    return pltpu.make_async_remote_copy(src_ref=src, dst_ref=dst, send_sem=send_sems.at[k], recv_sem=recv_sems.at[k],
                                        device_id=to, device_id_type=MESH)


def _gather_weights(groups, conv_w):
    n = len(groups)

    def body(*refs):
        srcs, conv_src, outs, conv_out = refs[:n], refs[n], refs[n + 1:2 * n + 1], refs[2 * n + 1]
        send_sems, recv_sems, local_sems = refs[2 * n + 2:]
        x, y, c, chips = _place()
        me = 2 * x + y
        sibling = (x, y, 1 - c)
        local = [pltpu.make_async_copy(srcs[g], outs[g].at[me], local_sems.at[g]) for g in range(n)]
        local.append(pltpu.make_async_copy(conv_src, conv_out.at[me], local_sems.at[n]))
        for cp in local:
            cp.start()
        first = [_remote(srcs[g].at[c], outs[g].at[me, c], send_sems, recv_sems, g * 3 + k, (px, py, c))
                 for k, (px, py) in enumerate(chips) for g in range(n)]
        first += [_remote(conv_src, conv_out.at[me], send_sems, recv_sems, 6 * n + k, (px, py, c)) for k, (px, py) in enumerate(chips)]
        for cp in first:
            cp.start()
        passed = []
        for k, (px, py) in enumerate(chips):
            for g in range(n):
                slot = outs[g].at[2 * px + py, c]
                _remote(slot, slot, send_sems, recv_sems, g * 3 + k, (px, py, c)).wait_recv()
                cp = _remote(slot, slot, send_sems, recv_sems, 3 * n + g * 3 + k, sibling)
                cp.start()
                passed.append(cp)
        for k, (px, py) in enumerate(chips):
            slot = conv_out.at[2 * px + py]
            _remote(slot, slot, send_sems, recv_sems, 6 * n + k, (px, py, c)).wait_recv()
            for g in range(n):
                slot = outs[g].at[2 * px + py, 1 - c]
                _remote(slot, slot, send_sems, recv_sems, 3 * n + g * 3 + k, sibling).wait_recv()
        for cp in first + passed:
            cp.wait_send()
        for cp in local:
            cp.wait()

    n_sem = 6 * n + 3
    outs = pl.pallas_call(
        body, name="comm_gather_weights", in_specs=[HBM_SPEC] * (n + 1), out_specs=[HBM_SPEC] * (n + 1),
        out_shape=[jax.ShapeDtypeStruct((N_CHIPS,) + g.shape, g.dtype) for g in groups]
        + [jax.ShapeDtypeStruct((N_CHIPS,) + conv_w.shape, conv_w.dtype)],
        scratch_shapes=[pltpu.SemaphoreType.DMA((n_sem,)), pltpu.SemaphoreType.DMA((n_sem,)), pltpu.SemaphoreType.DMA((n + 1,))],
    )(*groups, conv_w)
    return outs[:n], outs[n]


def _swap_halves(groups):
    n = len(groups)

    def body(*refs):
        srcs, outs, send_sems, recv_sems = refs[:n], refs[n:2 * n], refs[2 * n], refs[2 * n + 1]
        x, y, c, _ = _place()
        cps = [_remote(srcs[g].at[1 - c], outs[g], send_sems, recv_sems, g, (x, y, 1 - c)) for g in range(n)]
        for cp in cps:
            cp.start()
        for cp in cps:
            cp.wait()

    return pl.pallas_call(
        body, name="comm_swap_halves", in_specs=[HBM_SPEC] * n, out_specs=[HBM_SPEC] * n,
        out_shape=[jax.ShapeDtypeStruct(g.shape[1:], g.dtype) for g in groups],
        scratch_shapes=[pltpu.SemaphoreType.DMA((n,)), pltpu.SemaphoreType.DMA((n,))],
    )(*groups)


def _scatter_partials(groups):
    n = len(groups)

    def body(*refs):
        srcs, outs = refs[:n], refs[n:2 * n]
        send_sems, recv_sems, local_sems = refs[2 * n:]
        x, y, c, chips = _place()
        me = 2 * x + y
        local = [pltpu.make_async_copy(srcs[g].at[me], outs[g].at[me], local_sems.at[g]) for g in range(n)]
        for cp in local:
            cp.start()
        sends = [_remote(srcs[g].at[2 * px + py], outs[g].at[me], send_sems, recv_sems, g * 3 + k, (px, py, c))
                 for k, (px, py) in enumerate(chips) for g in range(n)]
        for cp in sends:
            cp.start()
        for k, (px, py) in enumerate(chips):
            for g in range(n):
                slot = outs[g].at[2 * px + py]
                _remote(slot, slot, send_sems, recv_sems, g * 3 + k, (px, py, c)).wait_recv()
        for cp in sends:
            cp.wait_send()
        for cp in local:
            cp.wait()

    return pl.pallas_call(
        body, name="comm_scatter_partials", in_specs=[HBM_SPEC] * n, out_specs=[HBM_SPEC] * n,
        out_shape=[jax.ShapeDtypeStruct(g.shape, g.dtype) for g in groups],
        scratch_shapes=[pltpu.SemaphoreType.DMA((3 * n,)), pltpu.SemaphoreType.DMA((3 * n,)), pltpu.SemaphoreType.DMA((n,))],
    )(*groups)


def _share_reduced(groups):
    n = len(groups)

    def body(*refs):
        srcs, outs = refs[:n], refs[n:2 * n]
        send_sems, recv_sems, local_sems = refs[2 * n:]
        x, y, c, _ = _place()
        local = [pltpu.make_async_copy(srcs[g], outs[g].at[c], local_sems.at[g]) for g in range(n)]
        for cp in local:
            cp.start()
        sends = [_remote(srcs[g], outs[g].at[c], send_sems, recv_sems, g, (x, y, 1 - c)) for g in range(n)]
        for cp in sends:
            cp.start()
        for g in range(n):
            _remote(srcs[g], outs[g].at[1 - c], send_sems, recv_sems, g, (x, y, 1 - c)).wait_recv()
        for cp in sends:
            cp.wait_send()
        for cp in local:
            cp.wait()

    return pl.pallas_call(
        body, name="comm_share_reduced", in_specs=[HBM_SPEC] * n, out_specs=[HBM_SPEC] * n,
        out_shape=[jax.ShapeDtypeStruct((2,) + g.shape, g.dtype) for g in groups],
        scratch_shapes=[pltpu.SemaphoreType.DMA((n,)), pltpu.SemaphoreType.DMA((n,)), pltpu.SemaphoreType.DMA((n,))],
    )(*groups)


def _allreduce_small(v):
    n_dev = 2 * N_CHIPS

    def body(src, out, buf, send_sems, recv_sems):
        x, y, c, _ = _place()
        me = 4 * x + 2 * y + c
        buf[me] = src[...]
        flips = [(k >> 2 & 1, k >> 1 & 1, k & 1) for k in range(1, n_dev)]
        sends = []
        for k, (fx, fy, fc) in enumerate(flips):
            to = ((1 - x) if fx else x, (1 - y) if fy else y, (1 - c) if fc else c)
            cp = _remote(src, buf.at[me], send_sems, recv_sems, k, to)
            cp.start()
            sends.append((cp, to))
        for k, (cp, to) in enumerate(sends):
            slot = buf.at[4 * to[0] + 2 * to[1] + to[2]]
            _remote(slot, slot, send_sems, recv_sems, k, to).wait_recv()
        for cp, _ in sends:
            cp.wait_send()
        acc = buf[0]
        for d in range(1, n_dev):
            acc = acc + buf[d]
        out[...] = acc

    vm = pl.BlockSpec(memory_space=pltpu.VMEM)
    return pl.pallas_call(
        body, name="comm_allreduce_small", in_specs=[vm], out_specs=vm, out_shape=jax.ShapeDtypeStruct(v.shape, F32),
        scratch_shapes=[pltpu.VMEM((n_dev,) + v.shape, F32), pltpu.SemaphoreType.DMA((n_dev - 1,)),
                        pltpu.SemaphoreType.DMA((n_dev - 1,))],
    )(v)


def _row_tile(rows, cap=256):
    return max(t for t in range(8, cap + 1, 8) if rows % t == 0)


def _add_sibling(mine, theirs, core, name):
    _, n, R, C = mine.shape
    t = _row_tile(R)

    def body(core_ref, a_ref, b_ref, o_ref):
        o_ref[...] = a_ref[...] + b_ref[...]

    sp = pl.BlockSpec((None, t, C), lambda k, i, core_ref: (k, i, 0))
    grid_spec = pltpu.PrefetchScalarGridSpec(
        num_scalar_prefetch=1, grid=(n, R // t),
        in_specs=[pl.BlockSpec((None, None, t, C), lambda k, i, core_ref: (core_ref[0], k, i, 0)), sp], out_specs=sp)
    return pl.pallas_call(body, name=name, grid_spec=grid_spec, out_shape=jax.ShapeDtypeStruct((n, R, C), F32),
                          compiler_params=_params("parallel", "parallel"))(core, mine, theirs)


def _add_chips(a, name):
    n, R, C = a.shape
    t = _row_tile(R)

    def body(a_ref, o_ref):
        acc = a_ref[0]
        for k in range(1, n):
            acc = acc + a_ref[k]
        o_ref[...] = acc

    return pl.pallas_call(body, name=name, grid=(R // t,), in_specs=[pl.BlockSpec((n, t, C), lambda i: (0, i, 0))],
                          out_specs=pl.BlockSpec((t, C), lambda i: (i, 0)),
                          out_shape=jax.ShapeDtypeStruct((R, C), F32), compiler_params=_params("parallel"))(a)


def _adamw(w, g, m, v, name, g_row0=0):
    R, C = w.shape
    t = math.gcd(math.gcd(R, g_row0), 128) if R % 8 == 0 else R
    assert g_row0 % t == 0
    c1 = 1.0 - ADAM_B1 ** ADAM_STEP
    c2 = 1.0 - ADAM_B2 ** ADAM_STEP

    def body(w_ref, g_ref, m_ref, v_ref, go_ref, d_ref, nm_ref, nv_ref):
        gv = g_ref[...]
        nm = ADAM_B1 * m_ref[...] + (1.0 - ADAM_B1) * gv
        nv = ADAM_B2 * v_ref[...] + (1.0 - ADAM_B2) * (gv * gv)
        go_ref[...] = gv
        d_ref[...] = -ADAM_LR * ((nm / c1) / (jnp.sqrt(nv / c2) + ADAM_EPS) + ADAM_WD * w_ref[...])
        nm_ref[...] = nm
        nv_ref[...] = nv

    sp = pl.BlockSpec((t, C), lambda i: (i, 0))
    g_sp = pl.BlockSpec((t, C), lambda i: (i + g_row0 // t, 0))
    return pl.pallas_call(body, name=name, grid=(R // t,), in_specs=[sp, g_sp, sp, sp], out_specs=[sp] * 4,
                          out_shape=[jax.ShapeDtypeStruct((R, C), F32)] * 4, compiler_params=_params("parallel"))(w, g, m, v)


def kernel(x, mem, positions, g_pre_mix, w_in, b_gate, mla_q_norm, w_uq, mla_kv_norm, w_ukv, g_mem, w_mem_kv, w_br_mla, w_br_dil, w_br_mem, w_o, g_post_mix, g_pre_ffn, w_ffn_up, conv_w, conv_b, w_ffn_down, g_post_ffn, loss_target, m_g_pre_mix, m_w_in, m_b_gate, m_mla_q_norm, m_w_uq, m_mla_kv_norm, m_w_ukv, m_g_mem, m_w_mem_kv, m_w_br_mla, m_w_br_dil, m_w_br_mem, m_w_o, m_g_post_mix, m_g_pre_ffn, m_w_ffn_up, m_conv_w, m_conv_b, m_w_ffn_down, m_g_post_ffn, v_g_pre_mix, v_w_in, v_b_gate, v_mla_q_norm, v_w_uq, v_mla_kv_norm, v_w_ukv, v_g_mem, v_w_mem_kv, v_w_br_mla, v_w_br_dil, v_w_br_mem, v_w_o, v_g_post_mix, v_g_pre_ffn, v_w_ffn_up, v_conv_w, v_conv_b, v_w_ffn_down, v_g_post_ffn):
    w_args = (g_pre_mix, w_in, b_gate, mla_q_norm, w_uq, mla_kv_norm, w_ukv, g_mem, w_mem_kv, w_br_mla, w_br_dil, w_br_mem, w_o,
              g_post_mix, g_pre_ffn, w_ffn_up, conv_w, conv_b, w_ffn_down, g_post_ffn)
    m_args = (m_g_pre_mix, m_w_in, m_b_gate, m_mla_q_norm, m_w_uq, m_mla_kv_norm, m_w_ukv, m_g_mem, m_w_mem_kv, m_w_br_mla,
              m_w_br_dil, m_w_br_mem, m_w_o, m_g_post_mix, m_g_pre_ffn, m_w_ffn_up, m_conv_w, m_conv_b, m_w_ffn_down, m_g_post_ffn)
    v_args = (v_g_pre_mix, v_w_in, v_b_gate, v_mla_q_norm, v_w_uq, v_mla_kv_norm, v_w_ukv, v_g_mem, v_w_mem_kv, v_w_br_mla,
              v_w_br_dil, v_w_br_mem, v_w_o, v_g_post_mix, v_g_pre_ffn, v_w_ffn_up, v_conv_w, v_conv_b, v_w_ffn_down, v_g_post_ffn)
    sharded = {name for grp in GROUPS for name, _, _ in grp} | {CONV_W[0]}

    def local(a, name):
        return a[0] if name in sharded else a

    w = {n: local(a, n) for n, a in zip(WEIGHTS, w_args)}
    m = {n: local(a, n) for n, a in zip(WEIGHTS, m_args)}
    v = {n: local(a, n) for n, a in zip(WEIGHTS, v_args)}

    full = _full_weights(*_gather_weights(_my_weight_groups(w), w[CONV_W[0]]))
    full.update({name: w[name] for name, _ in REPLICATED})

    loss_local, grad_x, G = _local_step(x[0], mem[0], positions, loss_target[0], full)

    core = lax.axis_index("c").astype(jnp.int32).reshape(1)
    mine = _grad_groups(G)
    theirs = _swap_halves(mine)
    partial = [_add_sibling(a, b, core, f"add_sibling_{i}") for i, (a, b) in enumerate(zip(mine, theirs))]
    reduced = [_add_chips(a, f"add_chips_{i}") for i, a in enumerate(_scatter_partials(partial))]
    shard_groups = [a.reshape(-1, a.shape[-1]) for a in _share_reduced(reduced)]
    small = _allreduce_small(_pack_small(G, G[CONV_W[0]], loss_local))
    flat = small.reshape(-1)
    loss = flat[LOSS_AT]
    conv_g = flat[CONV_AT:LOSS_AT].reshape(CONV_W[1])
    conv_cols = CONV_W[1][1] // N_CHIPS
    conv_g = lax.dynamic_slice_in_dim(conv_g, (2 * lax.axis_index("x") + lax.axis_index("y")) * conv_cols, conv_cols, axis=1)

    grads, deltas, new_m, new_v = {}, {}, {}, {}
    for grp, g_all in zip(GROUPS, shard_groups):
        off = 0
        for name, _, _ in grp:
            grads[name], deltas[name], new_m[name], new_v[name] = _adamw(w[name], g_all, m[name], v[name], "adamw_" + name, off)
            off += w[name].shape[0]
    name = CONV_W[0]
    grads[name], deltas[name], new_m[name], new_v[name] = _adamw(w[name], conv_g, m[name], v[name], "adamw_" + name)
    packed = _adamw(_pack_small(w), small, _pack_small(m), _pack_small(v), "adamw_small")
    for dst, packed_small in zip((grads, deltas, new_m, new_v), packed):
        dst.update(_unpack_small(packed_small))

    def out(d, name):
        return d[name][None] if name in sharded else d[name]

    return (loss, grad_x[None], *[out(grads, n) for n in WEIGHTS], *[out(deltas, n) for n in WEIGHTS],
            *[out(new_m, n) for n in WEIGHTS], *[out(new_v, n) for n in WEIGHTS])
```

```python
import functools
import math

import jax
import jax.numpy as jnp
from jax import lax
from jax.experimental import pallas as pl
from jax.experimental.pallas import tpu as pltpu

F32 = jnp.float32
BF16 = jnp.bfloat16

D_MODEL = 1024
N_MEM = 256
RMS_EPS = 1e-6
NEG_INF = -1e30
MLA_HEADS = 8
MLA_NOPE = 64
MLA_ROPE = 32
MLA_QK = 96
Q_RANK = 384
KV_RANK = 256
ROPE_THETA = 10000.0
DIL_PAIRS = ((128, 1), (512, 4), (2048, 16))
DIL_GROUPS = 3
DIL_HPG = 4
DIL_HEADS = 12
DIL_W = 512
MEM_HEADS = 4
MEM_W = 512
D_FF = 2816
OFF_Q = 384
OFF_KV = 640
OFF_KR = 672
OFF_DIL = 5280
OFF_MEMQ = 5792
D_IN = 8864
ADAM_LR = 0.001
ADAM_B1 = 0.9
ADAM_B2 = 0.999
ADAM_EPS = 1e-08
ADAM_WD = 0.01
ADAM_STEP = 10

LANES = 128
VMEM_LIMIT = 56 * 1024 * 1024

N_CHIPS = 4
ROW_TILE = 256

NN = (((1,), (0,)), ((), ()))
NT = (((1,), (1,)), ((), ()))
TN = (((0,), (0,)), ((), ()))


def _params(*sem):
    return pltpu.CompilerParams(dimension_semantics=sem, vmem_limit_bytes=VMEM_LIMIT)


def _full(shape):
    return pl.BlockSpec(shape, lambda *_: (0,) * len(shape))


def _matmul(a, b, *, mode="nn", out_dtype=F32, tm=1024, tn=1024, tk=None, add=None, scale=None, name):
    if mode == "nn":
        (M, K), N = a.shape, b.shape[1]
    elif mode == "nt":
        (M, K), N = a.shape, b.shape[0]
    else:
        (K, M), N = a.shape, b.shape[1]
    tm, tn = min(tm, M), min(tn, N)
    tk = K if tk is None else min(tk, K)
    assert M % tm == 0 and N % tn == 0 and K % tk == 0, (name, M, N, K, tm, tn, tk)
    nk = K // tk
    dims = {"nn": NN, "nt": NT, "tn": TN}[mode]
    a_spec = pl.BlockSpec((tk, tm), lambda i, j, k: (k, i)) if mode == "tn" else pl.BlockSpec((tm, tk), lambda i, j, k: (i, k))
    b_spec = pl.BlockSpec((tn, tk), lambda i, j, k: (j, k)) if mode == "nt" else pl.BlockSpec((tk, tn), lambda i, j, k: (k, j))
    o_spec = pl.BlockSpec((tm, tn), lambda i, j, k: (i, j))
    has_add = add is not None

    def body(*refs):
        a_ref, b_ref = refs[0], refs[1]
        c_ref = refs[2] if has_add else None
        o_ref = refs[3] if has_add else refs[2]
        part = lax.dot_general(a_ref[...].astype(BF16), b_ref[...].astype(BF16), dims, preferred_element_type=F32)
        if scale is not None:
            assert nk == 1 and not has_add
            part = part * jnp.where(pl.program_id(1) < scale[0], scale[1], 1.0)
        if nk == 1:
            if has_add:
                part = part + c_ref[...]
            o_ref[...] = part.astype(out_dtype)
        else:
            acc = refs[-1]
            k = pl.program_id(2)

            @pl.when(k == 0)
            def _():
                acc[...] = part

            @pl.when(k > 0)
            def _():
                acc[...] += part

            @pl.when(k == nk - 1)
            def _():
                r = acc[...]
                if has_add:
                    r = r + c_ref[...]
                o_ref[...] = r.astype(out_dtype)

    in_specs = [a_spec, b_spec] + ([o_spec] if has_add else [])
    args = (a, b) + ((add,) if has_add else ())
    return pl.pallas_call(
        body, name=name, grid=(M // tm, N // tn, nk), in_specs=in_specs, out_specs=o_spec,
        out_shape=jax.ShapeDtypeStruct((M, N), out_dtype),
        scratch_shapes=[pltpu.VMEM((tm, tn), F32)] if nk > 1 else [],
        compiler_params=_params("parallel", "parallel", "arbitrary"),
    )(*args)


def _rms_fwd_val(x, g):
    r = lax.rsqrt(jnp.mean(x * x, axis=-1, keepdims=True) + RMS_EPS)
    return (x * r) * g


def _rms_bwd_val(dy, x, g):
    r = lax.rsqrt(jnp.mean(x * x, axis=-1, keepdims=True) + RMS_EPS)
    xn = x * r
    gdy = g * dy
    dx = r * (gdy - xn * jnp.mean(gdy * xn, axis=-1, keepdims=True))
    return dx, dy * xn


def _rope_tables(pos, invf, inverse):
    ang = pos * invf
    cos, sin = jnp.cos(ang), jnp.sin(ang)
    lane = lax.broadcasted_iota(jnp.int32, ang.shape, 1)
    first = (lane >= MLA_NOPE) & (lane < MLA_NOPE + MLA_ROPE // 2)
    second = (lane >= MLA_NOPE + MLA_ROPE // 2) & (lane < MLA_QK)
    sgn = -1.0 if inverse else 1.0
    sa = jnp.where(first, -sgn * sin, 0.0)
    sb = jnp.where(second, sgn * sin, 0.0)
    return cos, sa, sb


def _rope_val(x, cos, sa, sb):
    half = MLA_ROPE // 2
    return x * cos + pltpu.roll(x, LANES - half, 1) * sa + pltpu.roll(x, half, 1) * sb


def _head_sum_bcast(v, n_heads):
    parts = []
    for h in range(n_heads):
        s = jnp.sum(v[:, h * LANES:(h + 1) * LANES], axis=1, keepdims=True)
        parts.append(jnp.broadcast_to(s, (v.shape[0], LANES)))
    return parts


def _row_spec(t, w):
    return pl.BlockSpec((t, w), lambda i: (i, 0))


def _acc_spec(w, rows=1):
    return pl.BlockSpec((rows, w), lambda i: (0, 0))


def _rmsnorm(x, g, out_dtype, name):
    S, W = x.shape
    t = min(ROW_TILE, S)

    def body(x_ref, g_ref, o_ref):
        o_ref[...] = _rms_fwd_val(x_ref[...], g_ref[...]).astype(out_dtype)

    return pl.pallas_call(body, name=name, grid=(S // t,), in_specs=[_row_spec(t, W), _acc_spec(W)],
                          out_specs=_row_spec(t, W), out_shape=jax.ShapeDtypeStruct((S, W), out_dtype),
                          compiler_params=_params("parallel"))(x, g)


def _mla_prep(proj_a, pos, invf, q_norm, kv_norm):
    S = proj_a.shape[0]
    t = ROW_TILE

    def body(a_ref, pos_ref, invf_ref, qn_ref, kvn_ref, cq_ref, ckv_ref, kpe_ref):
        a = a_ref[...]
        cq_ref[...] = _rms_fwd_val(a[:, 0:Q_RANK], qn_ref[...]).astype(BF16)
        ckv_ref[...] = _rms_fwd_val(a[:, Q_RANK + LANES:], kvn_ref[...]).astype(BF16)
        cos, sa, sb = _rope_tables(pos_ref[...], invf_ref[...], False)
        kpe_ref[...] = _rope_val(a[:, Q_RANK:Q_RANK + LANES], cos, sa, sb)

    return pl.pallas_call(
        body, name="mla_prep", grid=(S // t,),
        in_specs=[_row_spec(t, 768), _row_spec(t, 1), _acc_spec(LANES), _acc_spec(Q_RANK), _acc_spec(KV_RANK)],
        out_specs=[_row_spec(t, Q_RANK), _row_spec(t, KV_RANK), _row_spec(t, LANES)],
        out_shape=[jax.ShapeDtypeStruct((S, Q_RANK), BF16), jax.ShapeDtypeStruct((S, KV_RANK), BF16),
                   jax.ShapeDtypeStruct((S, LANES), F32)],
        compiler_params=_params("parallel"))(proj_a, pos, invf, q_norm, kv_norm)


def _qk_final(q_pre, k_pre, kpe, pos, invf):
    S, W = q_pre.shape
    t = ROW_TILE
    scale = MLA_QK ** -0.5

    def body(q_ref, k_ref, kpe_ref, pos_ref, invf_ref, qo_ref, ko_ref):
        cos, sa, sb = _rope_tables(pos_ref[...], invf_ref[...], False)
        kpe_v = kpe_ref[...]
        for h in range(MLA_HEADS):
            sl = slice(h * LANES, (h + 1) * LANES)
            qo_ref[:, sl] = (_rope_val(q_ref[:, sl], cos, sa, sb) * scale).astype(BF16)
            ko_ref[:, sl] = (k_ref[:, sl] + kpe_v).astype(BF16)

    return pl.pallas_call(
        body, name="qk_final", grid=(S // t,),
        in_specs=[_row_spec(t, W), _row_spec(t, W), _row_spec(t, LANES), _row_spec(t, 1), _acc_spec(LANES)],
        out_specs=[_row_spec(t, W), _row_spec(t, W)],
        out_shape=[jax.ShapeDtypeStruct((S, W), BF16)] * 2,
        compiler_params=_params("parallel"))(q_pre, k_pre, kpe, pos, invf)


def _mla_bwd_prep(dq, dk, pos, invf):
    S, W = dq.shape
    t = ROW_TILE
    scale = MLA_QK ** -0.5

    def body(dq_ref, dk_ref, pos_ref, invf_ref, dqp_ref, dkpe_ref):
        cos, sa, sb = _rope_tables(pos_ref[...], invf_ref[...], True)
        tot = jnp.zeros((t, LANES), F32)
        for h in range(MLA_HEADS):
            sl = slice(h * LANES, (h + 1) * LANES)
            dqp_ref[:, sl] = _rope_val(dq_ref[:, sl] * scale, cos, sa, sb).astype(BF16)
            tot = tot + dk_ref[:, sl]
        lane = lax.broadcasted_iota(jnp.int32, tot.shape, 1)
        tot = jnp.where((lane >= MLA_NOPE) & (lane < MLA_QK), tot, 0.0)
        dkpe_ref[...] = _rope_val(tot, cos, sa, sb)

    return pl.pallas_call(
        body, name="mla_bwd_prep", grid=(S // t,),
        in_specs=[_row_spec(t, W), _row_spec(t, W), _row_spec(t, 1), _acc_spec(LANES)],
        out_specs=[_row_spec(t, W), _row_spec(t, LANES)],
        out_shape=[jax.ShapeDtypeStruct((S, W), BF16), jax.ShapeDtypeStruct((S, LANES), F32)],
        compiler_params=_params("parallel"))(dq, dk, pos, invf)


def _mla_norm_bwd(dcq, dckv_a, dckv_b, dkpe, proj_a, q_norm, kv_norm):
    S = proj_a.shape[0]
    t = ROW_TILE

    def body(dcq_ref, da_ref, db_ref, dkpe_ref, a_ref, qn_ref, kvn_ref, o_ref, dqn_ref, dkvn_ref):
        i = pl.program_id(0)
        a = a_ref[...]
        dxq, gq = _rms_bwd_val(dcq_ref[...], a[:, 0:Q_RANK], qn_ref[...])
        dxkv, gkv = _rms_bwd_val(da_ref[...] + db_ref[...], a[:, Q_RANK + LANES:], kvn_ref[...])
        o_ref[:, 0:Q_RANK] = dxq.astype(BF16)
        o_ref[:, Q_RANK:Q_RANK + LANES] = dkpe_ref[...].astype(BF16)
        o_ref[:, Q_RANK + LANES:] = dxkv.astype(BF16)

        @pl.when(i == 0)
        def _():
            dqn_ref[...] = jnp.zeros_like(dqn_ref)
            dkvn_ref[...] = jnp.zeros_like(dkvn_ref)

        dqn_ref[...] += jnp.sum(gq, axis=0, keepdims=True)
        dkvn_ref[...] += jnp.sum(gkv, axis=0, keepdims=True)

    return pl.pallas_call(
        body, name="mla_norm_bwd", grid=(S // t,),
        in_specs=[_row_spec(t, Q_RANK), _row_spec(t, KV_RANK), _row_spec(t, KV_RANK), _row_spec(t, LANES),
                  _row_spec(t, 768), _acc_spec(Q_RANK), _acc_spec(KV_RANK)],
        out_specs=[_row_spec(t, 768), _acc_spec(Q_RANK), _acc_spec(KV_RANK)],
        out_shape=[jax.ShapeDtypeStruct((S, 768), BF16), jax.ShapeDtypeStruct((1, Q_RANK), F32),
                   jax.ShapeDtypeStruct((1, KV_RANK), F32)],
        compiler_params=_params("arbitrary"))(dcq, dckv_a, dckv_b, dkpe, proj_a, q_norm, kv_norm)


def _dil_mix(o_list, lse_list):
    S = o_list[0].shape[0]
    t = ROW_TILE

    def body(o0, o1, o2, l0, l1, l2, y_ref):
        ls = [l0[...], l1[...], l2[...]]
        m = jnp.maximum(jnp.maximum(ls[0], ls[1]), ls[2])
        es = [jnp.exp(l - m) for l in ls]
        den = es[0] + es[1] + es[2]
        y = (es[0] / den) * o0[...] + (es[1] / den) * o1[...] + (es[2] / den) * o2[...]
        y_ref[...] = y.astype(BF16)

    return pl.pallas_call(
        body, name="dil_mix", grid=(S // t,), in_specs=[_row_spec(t, DIL_W)] * 6, out_specs=_row_spec(t, DIL_W),
        out_shape=jax.ShapeDtypeStruct((S, DIL_W), BF16), compiler_params=_params("parallel"))(*o_list, *lse_list)


def _dil_mix_bwd(dy, o_list, lse_list):
    S = dy.shape[0]
    t = ROW_TILE

    def body(dy_ref, o0, o1, o2, l0, l1, l2, d0, d1, d2, e0, e1, e2):
        ls = [l0[...], l1[...], l2[...]]
        os_ = [o0[...], o1[...], o2[...]]
        m = jnp.maximum(jnp.maximum(ls[0], ls[1]), ls[2])
        es = [jnp.exp(l - m) for l in ls]
        den = es[0] + es[1] + es[2]
        ws = [e / den for e in es]
        dyv = dy_ref[...]
        y = ws[0] * os_[0] + ws[1] * os_[1] + ws[2] * os_[2]
        b = jnp.concatenate(_head_sum_bcast(dyv * y, DIL_HPG), axis=1)
        for w, d_ref, e_ref in zip(ws, (d0, d1, d2), (e0, e1, e2)):
            d_ref[...] = (w * dyv).astype(BF16)
            e_ref[...] = w * b

    return pl.pallas_call(
        body, name="dil_mix_bwd", grid=(S // t,), in_specs=[_row_spec(t, DIL_W)] * 7, out_specs=[_row_spec(t, DIL_W)] * 6,
        out_shape=[jax.ShapeDtypeStruct((S, DIL_W), BF16)] * 3 + [jax.ShapeDtypeStruct((S, DIL_W), F32)] * 3,
        compiler_params=_params("parallel"))(dy, *o_list, *lse_list)


def _delta(do, o, n_heads, name):
    S, W = do.shape
    t = ROW_TILE

    def body(do_ref, o_ref, d_ref):
        prod = do_ref[...].astype(F32) * o_ref[...].astype(F32)
        d_ref[...] = jnp.concatenate(_head_sum_bcast(prod, n_heads), axis=1)

    return pl.pallas_call(body, name=name, grid=(S // t,), in_specs=[_row_spec(t, W)] * 2, out_specs=_row_spec(t, W),
                          out_shape=jax.ShapeDtypeStruct((S, W), F32), compiler_params=_params("parallel"))(do, o)


def _merge(proj_g, b_gate, b_list):
    S = proj_g.shape[0]
    t = ROW_TILE

    def body(g_ref, b_ref, y0, y1, y2, o_ref):
        acc = jnp.zeros((t, D_MODEL), F32)
        for i, y in enumerate((y0, y1, y2)):
            sl = slice(i * D_MODEL, (i + 1) * D_MODEL)
            acc = acc + jax.nn.sigmoid(g_ref[:, sl] + b_ref[:, sl]) * y[...]
        o_ref[...] = acc.astype(BF16)

    return pl.pallas_call(
        body, name="merge", grid=(S // t,),
        in_specs=[_row_spec(t, 3 * D_MODEL), _acc_spec(3 * D_MODEL)] + [_row_spec(t, D_MODEL)] * 3,
        out_specs=_row_spec(t, D_MODEL), out_shape=jax.ShapeDtypeStruct((S, D_MODEL), BF16),
        compiler_params=_params("parallel"))(proj_g, b_gate, *b_list)


def _merge_bwd(dmerged, proj_g, b_gate, b_list):
    S = proj_g.shape[0]
    t = ROW_TILE

    def body(dm_ref, g_ref, b_ref, y0, y1, y2, d0, d1, d2, dz_ref, db_ref):
        i = pl.program_id(0)

        @pl.when(i == 0)
        def _():
            db_ref[...] = jnp.zeros_like(db_ref)

        dm = dm_ref[...]
        for k, (y, d_ref) in enumerate(zip((y0, y1, y2), (d0, d1, d2))):
            sl = slice(k * D_MODEL, (k + 1) * D_MODEL)
            s = jax.nn.sigmoid(g_ref[:, sl] + b_ref[:, sl])
            d_ref[...] = (s * dm).astype(BF16)
            dz = dm * y[...] * (s * (1.0 - s))
            dz_ref[:, sl] = dz.astype(BF16)
            db_ref[:, sl] += jnp.sum(dz, axis=0, keepdims=True)

    return pl.pallas_call(
        body, name="merge_bwd", grid=(S // t,),
        in_specs=[_row_spec(t, D_MODEL), _row_spec(t, 3 * D_MODEL), _acc_spec(3 * D_MODEL)] + [_row_spec(t, D_MODEL)] * 3,
        out_specs=[_row_spec(t, D_MODEL)] * 3 + [_row_spec(t, 3 * D_MODEL), _acc_spec(3 * D_MODEL)],
        out_shape=[jax.ShapeDtypeStruct((S, D_MODEL), BF16)] * 3
        + [jax.ShapeDtypeStruct((S, 3 * D_MODEL), BF16), jax.ShapeDtypeStruct((1, 3 * D_MODEL), F32)],
        compiler_params=_params("arbitrary"))(dmerged, proj_g, b_gate, *b_list)


def _norm2(o, x, g_post, g_pre):
    S = x.shape[0]
    t = ROW_TILE

    def body(o_ref, x_ref, gp_ref, gf_ref, x1_ref, h2_ref):
        x1 = x_ref[...] + _rms_fwd_val(o_ref[...], gp_ref[...])
        x1_ref[...] = x1
        h2_ref[...] = _rms_fwd_val(x1, gf_ref[...]).astype(BF16)

    return pl.pallas_call(
        body, name="norm2", grid=(S // t,),
        in_specs=[_row_spec(t, D_MODEL)] * 2 + [_acc_spec(D_MODEL)] * 2, out_specs=[_row_spec(t, D_MODEL)] * 2,
        out_shape=[jax.ShapeDtypeStruct((S, D_MODEL), F32), jax.ShapeDtypeStruct((S, D_MODEL), BF16)],
        compiler_params=_params("parallel"))(o, x, g_post, g_pre)


def _norm2_bwd(dx2, dh2, x1, o, g_pre, g_post):
    S = x1.shape[0]
    t = ROW_TILE

    def body(dx2_ref, dh2_ref, x1_ref, o_ref, gf_ref, gp_ref, dx1_ref, do_ref, dgf_ref, dgp_ref):
        i = pl.program_id(0)

        @pl.when(i == 0)
        def _():
            dgf_ref[...] = jnp.zeros_like(dgf_ref)
            dgp_ref[...] = jnp.zeros_like(dgp_ref)

        d1, gf = _rms_bwd_val(dh2_ref[...], x1_ref[...], gf_ref[...])
        dx1 = dx2_ref[...] + d1
        dx1_ref[...] = dx1
        do, gp = _rms_bwd_val(dx1, o_ref[...], gp_ref[...])
        do_ref[...] = do.astype(BF16)
        dgf_ref[...] += jnp.sum(gf, axis=0, keepdims=True)
        dgp_ref[...] += jnp.sum(gp, axis=0, keepdims=True)

    return pl.pallas_call(
        body, name="norm2_bwd", grid=(S // t,),
        in_specs=[_row_spec(t, D_MODEL)] * 4 + [_acc_spec(D_MODEL)] * 2,
        out_specs=[_row_spec(t, D_MODEL)] * 2 + [_acc_spec(D_MODEL)] * 2,
        out_shape=[jax.ShapeDtypeStruct((S, D_MODEL), F32), jax.ShapeDtypeStruct((S, D_MODEL), BF16),
                   jax.ShapeDtypeStruct((1, D_MODEL), F32), jax.ShapeDtypeStruct((1, D_MODEL), F32)],
        compiler_params=_params("arbitrary"))(dx2, dh2, x1, o, g_pre, g_post)


def _norm1_bwd(dx1, dh, x, g):
    S = x.shape[0]
    t = ROW_TILE

    def body(dx1_ref, dh_ref, x_ref, g_ref, dx_ref, dg_ref):
        i = pl.program_id(0)

        @pl.when(i == 0)
        def _():
            dg_ref[...] = jnp.zeros_like(dg_ref)

        d, gg = _rms_bwd_val(dh_ref[...], x_ref[...], g_ref[...])
        dx_ref[...] = dx1_ref[...] + d
        dg_ref[...] += jnp.sum(gg, axis=0, keepdims=True)

    return pl.pallas_call(
        body, name="norm1_bwd", grid=(S // t,),
        in_specs=[_row_spec(t, D_MODEL)] * 3 + [_acc_spec(D_MODEL)], out_specs=[_row_spec(t, D_MODEL), _acc_spec(D_MODEL)],
        out_shape=[jax.ShapeDtypeStruct((S, D_MODEL), F32), jax.ShapeDtypeStruct((1, D_MODEL), F32)],
        compiler_params=_params("arbitrary"))(dx1, dh, x, g)


def _gain_grad(dy, x, name):
    R, W = x.shape

    def body(dy_ref, x_ref, dg_ref):
        xv = x_ref[...]
        r = lax.rsqrt(jnp.mean(xv * xv, axis=-1, keepdims=True) + RMS_EPS)
        dg_ref[...] = jnp.sum(dy_ref[...] * (xv * r), axis=0, keepdims=True)

    return pl.pallas_call(body, name=name, grid=(1,), in_specs=[_full((R, W))] * 2, out_specs=_full((1, W)),
                          out_shape=jax.ShapeDtypeStruct((1, W), F32), compiler_params=_params("arbitrary"))(dy, x)


def _loss_head(f, x1, tgt, g):
    S = f.shape[0]
    t = ROW_TILE

    def body(f_ref, x1_ref, t_ref, g_ref, loss_ref, dx2_ref, df_ref, dg_ref):
        i = pl.program_id(0)

        @pl.when(i == 0)
        def _():
            loss_ref[...] = jnp.zeros_like(loss_ref)
            dg_ref[...] = jnp.zeros_like(dg_ref)

        fv, gv = f_ref[...], g_ref[...]
        err = x1_ref[...] + _rms_fwd_val(fv, gv) - t_ref[...]
        part = jnp.sum(jnp.mean(err * err, axis=-1, keepdims=True), axis=0, keepdims=True)
        loss_ref[...] += jnp.broadcast_to(0.5 * part, loss_ref.shape)
        dx2 = err * (1.0 / D_MODEL)
        dx2_ref[...] = dx2
        df, gg = _rms_bwd_val(dx2, fv, gv)
        df_ref[...] = df.astype(BF16)
        dg_ref[...] += jnp.sum(gg, axis=0, keepdims=True)

    return pl.pallas_call(
        body, name="loss_head", grid=(S // t,),
        in_specs=[_row_spec(t, D_MODEL)] * 3 + [_acc_spec(D_MODEL)],
        out_specs=[_acc_spec(LANES, 8), _row_spec(t, D_MODEL), _row_spec(t, D_MODEL), _acc_spec(D_MODEL)],
        out_shape=[jax.ShapeDtypeStruct((8, LANES), F32), jax.ShapeDtypeStruct((S, D_MODEL), F32),
                   jax.ShapeDtypeStruct((S, D_MODEL), BF16), jax.ShapeDtypeStruct((1, D_MODEL), F32)],
        compiler_params=_params("arbitrary"))(f, x1, tgt, g)


CONV_TC = 256
CONV_TT = 512
HALO = 8


def _shift_down(u, halo, first):
    row = lax.broadcasted_iota(jnp.int32, u.shape, 0)
    h6 = jnp.where(first, 0.0, halo[HALO - 2:HALO - 1, :])
    h7 = jnp.where(first, 0.0, halo[HALO - 1:HALO, :])
    s1 = jnp.where(row == 0, h7, pltpu.roll(u, 1, 0))
    s2 = jnp.where(row == 0, h6, jnp.where(row == 1, h7, pltpu.roll(u, 2, 0)))
    return s1, s2


def _conv_specs(tt, n_c, n_t, lead):
    def halo_row(i):
        return jnp.maximum(i * (tt // HALO) - 1, 0) if lead else jnp.minimum((i + 1) * (tt // HALO), n_t * (tt // HALO) - 1)
    return [
        pl.BlockSpec((tt, CONV_TC), lambda j, i: (i, j)),
        pl.BlockSpec((tt, CONV_TC), lambda j, i: (i, j + n_c)),
        pl.BlockSpec((HALO, CONV_TC), lambda j, i: (halo_row(i), j)),
        pl.BlockSpec((HALO, CONV_TC), lambda j, i: (halo_row(i), j + n_c)),
    ]


def _conv_z(ug, uv, hg, hv, w_g, w_v, b_g, b_v, first):
    g1, g2 = _shift_down(ug, hg, first)
    v1, v2 = _shift_down(uv, hv, first)
    zg = b_g + w_g[0:1, :] * g2
    zg = zg + w_g[1:2, :] * g1
    zg = zg + w_g[2:3, :] * ug
    zv = b_v + w_v[0:1, :] * v2
    zv = zv + w_v[1:2, :] * v1
    zv = zv + w_v[2:3, :] * uv
    return zg, zv, (g2, g1, ug), (v2, v1, uv)


def _conv_fwd(u, conv_w, conv_b):
    S = u.shape[0]
    tt = min(CONV_TT, S)
    n_c, n_t = D_FF // CONV_TC, S // tt
    wspec = [pl.BlockSpec((3, CONV_TC), lambda j, i: (0, j)), pl.BlockSpec((3, CONV_TC), lambda j, i: (0, j + n_c)),
             pl.BlockSpec((1, CONV_TC), lambda j, i: (0, j)), pl.BlockSpec((1, CONV_TC), lambda j, i: (0, j + n_c))]

    def body(ug_ref, uv_ref, hg_ref, hv_ref, wg_ref, wv_ref, bg_ref, bv_ref, a_ref):
        first = pl.program_id(1) == 0
        zg, zv, _, _ = _conv_z(ug_ref[...], uv_ref[...], hg_ref, hv_ref, wg_ref, wv_ref, bg_ref[...], bv_ref[...], first)
        a_ref[...] = (zg * jax.nn.sigmoid(zg) * zv).astype(BF16)

    return pl.pallas_call(
        body, name="conv_fwd", grid=(n_c, n_t), in_specs=_conv_specs(tt, n_c, n_t, True) + wspec,
        out_specs=pl.BlockSpec((tt, CONV_TC), lambda j, i: (i, j)), out_shape=jax.ShapeDtypeStruct((S, D_FF), BF16),
        compiler_params=_params("parallel", "parallel"))(u, u, u, u, conv_w, conv_w, conv_b, conv_b)


def _conv_bwd_dz(da, u, conv_w, conv_b):
    S = u.shape[0]
    tt = min(CONV_TT, S)
    n_c, n_t = D_FF // CONV_TC, S // tt
    wspec = [pl.BlockSpec((3, CONV_TC), lambda j, i: (0, j)), pl.BlockSpec((3, CONV_TC), lambda j, i: (0, j + n_c)),
             pl.BlockSpec((1, CONV_TC), lambda j, i: (0, j)), pl.BlockSpec((1, CONV_TC), lambda j, i: (0, j + n_c))]
    tile = pl.BlockSpec((tt, CONV_TC), lambda j, i: (i, j))
    tile_v = pl.BlockSpec((tt, CONV_TC), lambda j, i: (i, j + n_c))

    def body(da_ref, ug_ref, uv_ref, hg_ref, hv_ref, wg_ref, wv_ref, bg_ref, bv_ref,
             dzg_ref, dzv_ref, dwg_ref, dwv_ref, dbg_ref, dbv_ref):
        i = pl.program_id(1)
        zg, zv, gs, vs = _conv_z(ug_ref[...], uv_ref[...], hg_ref, hv_ref, wg_ref, wv_ref, bg_ref[...], bv_ref[...], i == 0)
        dav = da_ref[...]
        sg = jax.nn.sigmoid(zg)
        dzv = dav * (zg * sg)
        dzg = dav * zv * (sg * (1.0 + zg * (1.0 - sg)))
        dzg_ref[...] = dzg
        dzv_ref[...] = dzv

        @pl.when(i == 0)
        def _():
            for r in (dwg_ref, dwv_ref, dbg_ref, dbv_ref):
                r[...] = jnp.zeros_like(r)

        for k in range(3):
            dwg_ref[k:k + 1, :] += jnp.sum(dzg * gs[k], axis=0, keepdims=True)
            dwv_ref[k:k + 1, :] += jnp.sum(dzv * vs[k], axis=0, keepdims=True)
        dbg_ref[...] += jnp.sum(dzg, axis=0, keepdims=True)
        dbv_ref[...] += jnp.sum(dzv, axis=0, keepdims=True)

    outs = pl.pallas_call(
        body, name="conv_bwd_dz", grid=(n_c, n_t), in_specs=[tile] + _conv_specs(tt, n_c, n_t, True) + wspec,
        out_specs=[tile, tile] + [pl.BlockSpec((3, CONV_TC), lambda j, i: (0, j))] * 2 + [pl.BlockSpec((1, CONV_TC), lambda j, i: (0, j))] * 2,
        out_shape=[jax.ShapeDtypeStruct((S, D_FF), F32)] * 2 + [jax.ShapeDtypeStruct((3, D_FF), F32)] * 2
        + [jax.ShapeDtypeStruct((1, D_FF), F32)] * 2,
        compiler_params=_params("parallel", "arbitrary"))(da, u, u, u, u, conv_w, conv_w, conv_b, conv_b)
    dzg, dzv, dwg, dwv, dbg, dbv = outs
    return dzg, dzv, jnp.concatenate([dwg, dwv], axis=1), jnp.concatenate([dbg, dbv], axis=1)


def _conv_bwd_du(dzg, dzv, conv_w):
    S = dzg.shape[0]
    tt = min(CONV_TT, S)
    n_c, n_t = D_FF // CONV_TC, S // tt
    steps = tt // HALO

    def nxt(i):
        return jnp.minimum((i + 1) * steps, n_t * steps - 1)

    tile_g = pl.BlockSpec((tt, CONV_TC), lambda h, j, i: (i * (1 - h), j * (1 - h)))
    tile_v = pl.BlockSpec((tt, CONV_TC), lambda h, j, i: (i * h, j * h))
    halo_g = pl.BlockSpec((HALO, CONV_TC), lambda h, j, i: (nxt(i) * (1 - h), j * (1 - h)))
    halo_v = pl.BlockSpec((HALO, CONV_TC), lambda h, j, i: (nxt(i) * h, j * h))
    wsp = pl.BlockSpec((3, CONV_TC), lambda h, j, i: (0, j + h * n_c))

    def body(zg_ref, zv_ref, hg_ref, hv_ref, w_ref, du_ref):
        h = pl.program_id(0)
        last = pl.program_id(2) == n_t - 1
        z = jnp.where(h == 0, zg_ref[...], zv_ref[...])
        hal = jnp.where(h == 0, hg_ref[...], hv_ref[...])
        row = lax.broadcasted_iota(jnp.int32, z.shape, 0)
        h0 = jnp.where(last, 0.0, hal[0:1, :])
        h1 = jnp.where(last, 0.0, hal[1:2, :])
        u1 = jnp.where(row == tt - 1, h0, pltpu.roll(z, tt - 1, 0))
        u2 = jnp.where(row == tt - 1, h1, jnp.where(row == tt - 2, h0, pltpu.roll(z, tt - 2, 0)))
        du = w_ref[2:3, :] * z + w_ref[1:2, :] * u1 + w_ref[0:1, :] * u2
        du_ref[...] = du.astype(BF16)

    return pl.pallas_call(
        body, name="conv_bwd_du", grid=(2, n_c, n_t), in_specs=[tile_g, tile_v, halo_g, halo_v, wsp],
        out_specs=pl.BlockSpec((tt, CONV_TC), lambda h, j, i: (i, j + h * n_c)),
        out_shape=jax.ShapeDtypeStruct((S, 2 * D_FF), BF16),
        compiler_params=_params("parallel", "parallel", "parallel"))(dzg, dzv, dzg, dzv, conv_w)


BAND = 128


MEM_TQ = 512


def _mem_fwd(q, kv, *, name):
    S, W = q.shape
    M = kv.shape[0]
    nh = W // LANES
    tq = min(MEM_TQ, S)

    def body(q_ref, k_ref, v_ref, o_ref, l_ref):
        s = lax.dot_general(q_ref[...], k_ref[...].astype(BF16), NT, preferred_element_type=F32)
        m = jnp.max(s, axis=1, keepdims=True)
        p = jnp.exp(s - m)
        l = jnp.sum(p, axis=1, keepdims=True)
        o_ref[...] = (lax.dot_general(p.astype(BF16), v_ref[...].astype(BF16), NN, preferred_element_type=F32) / l).astype(BF16)
        l_ref[...] = jnp.broadcast_to(m + jnp.log(l), (tq, LANES))

    blk = pl.BlockSpec((tq, LANES), lambda hh, i: (i, hh))
    return pl.pallas_call(
        body, name=name, grid=(nh, S // tq),
        in_specs=[blk, pl.BlockSpec((M, LANES), lambda hh, i: (0, hh)), pl.BlockSpec((M, LANES), lambda hh, i: (0, hh + nh))],
        out_specs=[blk, blk], out_shape=[jax.ShapeDtypeStruct((S, W), BF16), jax.ShapeDtypeStruct((S, W), F32)],
        compiler_params=_params("parallel", "parallel"))(q, kv, kv)


def _mem_bwd(q, kv, do, lse, delta, *, scale, name):
    S, W = q.shape
    M = kv.shape[0]
    nh = W // LANES
    tq = min(MEM_TQ, S)

    def body(q_ref, k_ref, v_ref, do_ref, l_ref, d_ref, dq_ref, dk_ref, dv_ref):
        i = pl.program_id(1)

        @pl.when(i == 0)
        def _():
            dk_ref[...] = jnp.zeros_like(dk_ref)
            dv_ref[...] = jnp.zeros_like(dv_ref)

        qv = q_ref[...]
        kv_, vv = k_ref[...].astype(BF16), v_ref[...].astype(BF16)
        dov = do_ref[...].astype(BF16)
        s = lax.dot_general(qv, kv_, NT, preferred_element_type=F32)
        p = jnp.exp(s - l_ref[...][:, 0:1])
        dp = lax.dot_general(dov, vv, NT, preferred_element_type=F32)
        ds = (p * (dp - d_ref[...][:, 0:1])).astype(BF16)
        dq_ref[...] = lax.dot_general(ds, kv_, NN, preferred_element_type=F32) * scale
        dk_ref[...] += lax.dot_general(ds, qv, TN, preferred_element_type=F32)
        dv_ref[...] += lax.dot_general(p.astype(BF16), dov, TN, preferred_element_type=F32)

    blk = pl.BlockSpec((tq, LANES), lambda hh, i: (i, hh))
    kblk = pl.BlockSpec((M, LANES), lambda hh, i: (0, hh))
    vblk = pl.BlockSpec((M, LANES), lambda hh, i: (0, hh + nh))
    dq, dk, dv = pl.pallas_call(
        body, name=name, grid=(nh, S // tq), in_specs=[blk, kblk, vblk, blk, blk, blk], out_specs=[blk, kblk, kblk],
        out_shape=[jax.ShapeDtypeStruct((S, W), F32), jax.ShapeDtypeStruct((M, W), F32), jax.ShapeDtypeStruct((M, W), F32)],
        compiler_params=_params("parallel", "arbitrary"))(q, kv, kv, do, lse, delta)
    return dq, jnp.concatenate([dk, dv], axis=1)


CAUSAL_T = 256


def _causal_fwd(q, k, v, *, name):
    S, W = q.shape
    T = min(CAUSAL_T, S // 2)
    tq = 2 * T

    def body(q_ref, k_ref, v_ref, o_ref, l_ref):
        i = pl.program_id(1)
        qa, qb = q_ref[0:T, :], q_ref[T:tq, :]

        def rows(j):
            return pl.ds(pl.multiple_of(j * T, T), T)

        def scores(qv, j):
            return lax.dot_general(qv, k_ref[rows(j), :], NT, preferred_element_type=F32)

        def diag(s):
            return jnp.where(lax.broadcasted_iota(jnp.int32, s.shape, 0) >= lax.broadcasted_iota(jnp.int32, s.shape, 1), s, NEG_INF)

        def upd(s, j, carry):
            m, l, acc = carry
            m_new = jnp.maximum(m, jnp.max(s, axis=1, keepdims=True))
            alpha = jnp.exp(m - m_new)
            p = jnp.exp(s - m_new)
            l = alpha * l + jnp.sum(p, axis=1, keepdims=True)
            acc = alpha * acc + lax.dot_general(p.astype(BF16), v_ref[rows(j), :], NN, preferred_element_type=F32)
            return m_new, l, acc

        init = (jnp.full((T, 1), NEG_INF, F32), jnp.zeros((T, 1), F32), jnp.zeros((T, LANES), F32))

        def both(j, c):
            ca, cb, sa, sb = c
            na, nb = scores(qa, j + 1), scores(qb, j + 1)
            return upd(sa, j, ca), upd(sb, j, cb), na, nb

        ca, cb, sa, sb = lax.fori_loop(0, 2 * i, both, (init, init, scores(qa, 0), scores(qb, 0)))
        ca = upd(diag(sa), 2 * i, ca)
        cb = upd(sb, 2 * i, cb)
        cb = upd(diag(scores(qb, 2 * i + 1)), 2 * i + 1, cb)
        for (m, l, acc), sl in ((ca, slice(0, T)), (cb, slice(T, tq))):
            o_ref[sl, :] = (acc / l).astype(BF16)
            l_ref[sl, :] = jnp.broadcast_to(m + jnp.log(l), (T, LANES))

    blk = pl.BlockSpec((tq, LANES), lambda hh, i: (i, hh))
    whole = pl.BlockSpec((S, LANES), lambda hh, i: (0, hh))
    return pl.pallas_call(
        body, name=name, grid=(W // LANES, S // tq), in_specs=[blk, whole, whole], out_specs=[blk, blk],
        out_shape=[jax.ShapeDtypeStruct((S, W), BF16), jax.ShapeDtypeStruct((S, W), F32)],
        compiler_params=_params("parallel", "arbitrary"))(q, k, v)


def _causal_bwd(q, k, v, do, lse, delta, *, name):
    S, W = q.shape
    T = min(CAUSAL_T, S // 2)
    tk = 2 * T
    nq = S // T

    def body(q_ref, k_ref, v_ref, do_ref, l_ref, d_ref, dq_ref, dk_ref, dv_ref):
        j = pl.program_id(1)

        @pl.when(j == 0)
        def _():
            dq_ref[...] = jnp.zeros_like(dq_ref)

        ka, va, kb, vb = k_ref[0:T, :], v_ref[0:T, :], k_ref[T:tk, :], v_ref[T:tk, :]

        def chunk(i):
            rows = pl.ds(pl.multiple_of(i * T, T), T)
            return rows, q_ref[rows, :], do_ref[rows, :].astype(BF16), l_ref[rows, :][:, 0:1], d_ref[rows, :][:, 0:1]

        def products(qi, doi, kh, vh):
            return (lax.dot_general(qi, kh, NT, preferred_element_type=F32), lax.dot_general(doi, vh, NT, preferred_element_type=F32))

        def half(s, dp, qi, doi, li, di, kh, diag):
            if diag:
                s = jnp.where(lax.broadcasted_iota(jnp.int32, s.shape, 0) >= lax.broadcasted_iota(jnp.int32, s.shape, 1), s, NEG_INF)
            p = jnp.exp(s - li)
            ds = (p * (dp - di)).astype(BF16)
            return (lax.dot_general(ds, qi, TN, preferred_element_type=F32),
                    lax.dot_general(p.astype(BF16), doi, TN, preferred_element_type=F32),
                    lax.dot_general(ds, kh, NN, preferred_element_type=F32))

        rows, qi, doi, li, di = chunk(2 * j)
        dka, dva, dq = half(*products(qi, doi, ka, va), qi, doi, li, di, ka, True)
        dq_ref[rows, :] += dq
        rows, qi, doi, li, di = chunk(2 * j + 1)
        a = half(*products(qi, doi, ka, va), qi, doi, li, di, ka, False)
        dkb, dvb, dq = half(*products(qi, doi, kb, vb), qi, doi, li, di, kb, True)
        dq_ref[rows, :] += a[2] + dq
        dka, dva = dka + a[0], dva + a[1]

        def ahead(i):
            _, qi, doi, _, _ = chunk(jnp.minimum(i, nq - 1))
            return products(qi, doi, ka, va) + products(qi, doi, kb, vb)

        def both(i, c):
            dka, dva, dkb, dvb, sa, dpa, sb, dpb = c
            nxt = ahead(i + 1)
            rows, qi, doi, li, di = chunk(i)
            a = half(sa, dpa, qi, doi, li, di, ka, False)
            b = half(sb, dpb, qi, doi, li, di, kb, False)
            dq_ref[rows, :] += a[2] + b[2]
            return (dka + a[0], dva + a[1], dkb + b[0], dvb + b[1]) + nxt

        dka, dva, dkb, dvb = lax.fori_loop(2 * j + 2, nq, both, (dka, dva, dkb, dvb) + ahead(2 * j + 2))[:4]
        dk_ref[0:T, :], dv_ref[0:T, :], dk_ref[T:tk, :], dv_ref[T:tk, :] = dka, dva, dkb, dvb

    blk = pl.BlockSpec((tk, LANES), lambda hh, j: (j, hh))
    whole = pl.BlockSpec((S, LANES), lambda hh, j: (0, hh))
    return pl.pallas_call(
        body, name=name, grid=(W // LANES, S // tk), in_specs=[whole, blk, blk, whole, whole, whole],
        out_specs=[whole, blk, blk], out_shape=[jax.ShapeDtypeStruct((S, W), F32)] * 3,
        compiler_params=_params("parallel", "arbitrary"))(q, k, v, do, lse, delta)


BAND_TQ = 512


def _band_window(i, sub, nsub, L, q_ref, k_ref, v_ref, slope):
    kw = min(2 * BAND, L)
    n = i * nsub + sub
    k0 = 0 if kw == L else pl.multiple_of(jnp.maximum(n - 1, 0) * BAND, BAND)
    win = pl.ds(k0, kw)
    qs = q_ref[sub * BAND:(sub + 1) * BAND, :]
    kwv, vwv = k_ref[win, :], v_ref[win, :]
    s = lax.dot_general(qs, kwv, NT, preferred_element_type=F32)
    dist = (n * BAND + lax.broadcasted_iota(jnp.int32, s.shape, 0)) - (k0 + lax.broadcasted_iota(jnp.int32, s.shape, 1))
    s = jnp.where((dist >= 0) & (dist <= BAND), s - slope * dist.astype(F32), NEG_INF)
    return win, qs, kwv, vwv, s


def _band_fwd(q, k, v, slopes, *, n_heads, qcol, kcol, vcol, L, slope_mul, o_shape, name):
    tq = min(BAND_TQ, L)
    nsub = tq // BAND

    def body(sl_ref, q_ref, k_ref, v_ref, o_ref, l_ref):
        hh, i = pl.program_id(0), pl.program_id(1)
        slope = sl_ref[hh % DIL_HPG] * slope_mul
        for sub in range(nsub):
            _, _, _, vwv, s = _band_window(i, sub, nsub, L, q_ref, k_ref, v_ref, slope)
            m = jnp.max(s, axis=1, keepdims=True)
            p = jnp.exp(s - m)
            l = jnp.sum(p, axis=1, keepdims=True)
            rows = slice(sub * BAND, (sub + 1) * BAND)
            o_ref[rows, :] = lax.dot_general(p.astype(BF16), vwv, NN, preferred_element_type=F32) / l
            l_ref[rows, :] = jnp.broadcast_to(m + jnp.log(l), (BAND, LANES))

    whole = lambda col: pl.BlockSpec((L, LANES), lambda hh, i: (0, col(hh)))
    o_spec = pl.BlockSpec((tq, LANES), lambda hh, i: (i, hh))
    return pl.pallas_call(
        body, name=name, grid=(n_heads, L // tq),
        in_specs=[pl.BlockSpec(memory_space=pltpu.SMEM), pl.BlockSpec((tq, LANES), lambda hh, i: (i, qcol(hh))), whole(kcol), whole(vcol)],
        out_specs=[o_spec, o_spec], out_shape=[jax.ShapeDtypeStruct(o_shape, F32)] * 2,
        compiler_params=_params("parallel", "arbitrary"))(slopes, q, k, v)


def _band_bwd(q, k, v, do, lse, delta, slopes, *, n_heads, qcol, kcol, vcol, L, scale, slope_mul, d_shape, name):
    tq = min(BAND_TQ, L)
    nsub = tq // BAND
    n_steps = L // tq

    def body(sl_ref, q_ref, k_ref, v_ref, do_ref, l_ref, d_ref, dq_ref, dk_ref, dv_ref, dk_acc, dv_acc):
        hh, i = pl.program_id(0), pl.program_id(1)
        slope = sl_ref[hh % DIL_HPG] * slope_mul

        @pl.when(i == 0)
        def _():
            dk_acc[...] = jnp.zeros_like(dk_acc)
            dv_acc[...] = jnp.zeros_like(dv_acc)

        for sub in range(nsub):
            win, qs, kwv, vwv, s = _band_window(i, sub, nsub, L, q_ref, k_ref, v_ref, slope)
            rows = slice(sub * BAND, (sub + 1) * BAND)
            dos = do_ref[rows, :]
            p = jnp.exp(s - l_ref[rows, :][:, 0:1])
            dp = lax.dot_general(dos, vwv, NT, preferred_element_type=F32)
            ds = (p * (dp - d_ref[rows, :][:, 0:1])).astype(BF16)
            dq_ref[rows, :] = (lax.dot_general(ds, kwv, NN, preferred_element_type=F32) * scale).astype(BF16)
            dk_acc[win, :] += lax.dot_general(ds, qs, TN, preferred_element_type=F32)
            dv_acc[win, :] += lax.dot_general(p.astype(BF16), dos, TN, preferred_element_type=F32)

        @pl.when(i == n_steps - 1)
        def _():
            dk_ref[...] = dk_acc[...].astype(BF16)
            dv_ref[...] = dv_acc[...].astype(BF16)

    whole = lambda col: pl.BlockSpec((L, LANES), lambda hh, i: (0, col(hh)))
    blk = pl.BlockSpec((tq, LANES), lambda hh, i: (i, hh))
    ident = lambda hh: hh
    return pl.pallas_call(
        body, name=name, grid=(n_heads, n_steps),
        in_specs=[pl.BlockSpec(memory_space=pltpu.SMEM), pl.BlockSpec((tq, LANES), lambda hh, i: (i, qcol(hh))), whole(kcol), whole(vcol),
                  blk, blk, blk],
        out_specs=[blk, whole(ident), whole(ident)], out_shape=[jax.ShapeDtypeStruct(d_shape, BF16)] * 3,
        scratch_shapes=[pltpu.VMEM((L, LANES), F32)] * 2,
        compiler_params=_params("parallel", "arbitrary"))(slopes, q, k, v, do, lse, delta)


def _pad_heads(w, n_heads, width, axis):
    shp = w.shape
    new = shp[:axis] + (n_heads, width) + shp[axis + 1:]
    pad = [(0, 0)] * len(new)
    pad[axis + 1] = (0, LANES - width)
    out = jnp.pad(w.reshape(new), pad)
    return out.reshape(shp[:axis] + (n_heads * LANES,) + shp[axis + 1:])


def _unpad_heads(w, n_heads, width, axis):
    shp = w.shape
    new = shp[:axis] + (n_heads, LANES) + shp[axis + 1:]
    out = lax.slice_in_dim(w.reshape(new), 0, width, axis=axis + 1)
    return out.reshape(shp[:axis] + (n_heads * width,) + shp[axis + 1:])


def _alibi_slopes():
    s = jnp.exp2(-8.0 * jnp.arange(1, DIL_HEADS + 1, dtype=F32) / DIL_HEADS)
    return s.reshape(DIL_HPG, DIL_GROUPS).T


def _local_step(x, mem, positions, tgt, W):
    S = x.shape[0]
    pos = positions.reshape(S, 1).astype(F32)
    half = MLA_ROPE // 2
    inv_freq = ROPE_THETA ** (-jnp.arange(half, dtype=F32) / half)
    invf = jnp.zeros((1, LANES), F32).at[0, MLA_NOPE:MLA_NOPE + half].set(inv_freq).at[0, MLA_NOPE + half:MLA_QK].set(inv_freq)
    slopes = _alibi_slopes()

    w_in = W["w_in"]
    zc = lambda n: jnp.zeros((D_MODEL, n), BF16)
    w_a = jnp.concatenate([w_in[:, :OFF_Q], zc(MLA_NOPE), w_in[:, OFF_KV:OFF_KR], zc(LANES - MLA_QK), w_in[:, OFF_Q:OFF_KV]], axis=1)
    w_d, w_m, w_g = w_in[:, OFF_KR:OFF_DIL], w_in[:, OFF_DIL:OFF_MEMQ], w_in[:, OFF_MEMQ:]
    w_uq_p = _pad_heads(W["w_uq"], MLA_HEADS, MLA_QK, 1)
    ukv = W["w_ukv"].reshape(KV_RANK, MLA_HEADS, 2 * MLA_NOPE)
    w_uk_p = _pad_heads(ukv[:, :, :MLA_NOPE].reshape(KV_RANK, -1), MLA_HEADS, MLA_NOPE, 1)
    w_uv_p = _pad_heads(ukv[:, :, MLA_NOPE:].reshape(KV_RANK, -1), MLA_HEADS, MLA_NOPE, 1)
    w_br_mla_p = _pad_heads(W["w_br_mla"], MLA_HEADS, MLA_NOPE, 0)

    h = _rmsnorm(x, W["g_pre_mix"], BF16, "norm1")
    proj_a = _matmul(h, w_a, tn=768, name="proj_a")
    att_scale = LANES ** -0.5
    proj_d = _matmul(h, w_d, tn=DIL_GROUPS * DIL_W, out_dtype=BF16, scale=(1, att_scale), name="proj_d")
    proj_m = _matmul(h, w_m, tn=MEM_W, out_dtype=BF16, scale=(1, att_scale), name="proj_m")
    proj_g = _matmul(h, w_g, tn=1536, name="proj_g")

    cq_n, ckv_n, kpe = _mla_prep(proj_a, pos, invf, W["mla_q_norm"], W["mla_kv_norm"])
    q_pre = _matmul(cq_n, w_uq_p, tn=1024, name="mla_q")
    k_pre = _matmul(ckv_n, w_uk_p, tn=1024, name="mla_k")
    v_mla = _matmul(ckv_n, w_uv_p, tn=1024, out_dtype=BF16, name="mla_v")
    q_mla, k_mla = _qk_final(q_pre, k_pre, kpe, pos, invf)
    ident = lambda hh: hh
    o_mla, lse_mla = _causal_fwd(q_mla, k_mla, v_mla, name="attn_mla_fwd")

    n_dil_cols = 3 * DIL_HEADS
    o_dil, lse_dil = [], []
    for g, (window, dil) in enumerate(DIL_PAIRS):
        L = S // dil
        view = proj_d.reshape(L, dil * 3 * DIL_HEADS * LANES)
        col = lambda part, g=g: (lambda hh: (hh // DIL_HPG) * n_dil_cols + (part * DIL_GROUPS + g) * DIL_HPG + hh % DIL_HPG)
        o, lse = _band_fwd(view, view, view, slopes[g], n_heads=dil * DIL_HPG, qcol=col(0), kcol=col(1), vcol=col(2), L=L,
                           slope_mul=float(dil), o_shape=(L, dil * DIL_W), name=f"attn_dil{g}_fwd")
        o_dil.append(o.reshape(S, DIL_W))
        lse_dil.append(lse.reshape(S, DIL_W))
    y_dil = _dil_mix(o_dil, lse_dil)

    mem_n = _rmsnorm(mem, W["g_mem"], BF16, "mem_norm")
    kv_mem = _matmul(mem_n, W["w_mem_kv"], name="mem_kv")
    o_mem, lse_mem = _mem_fwd(proj_m, kv_mem, name="attn_mem_fwd")

    b_mla = _matmul(o_mla, w_br_mla_p, name="br_mla")
    b_dil = _matmul(y_dil, W["w_br_dil"], name="br_dil")
    b_mem = _matmul(o_mem, W["w_br_mem"], name="br_mem")
    merged = _merge(proj_g, W["b_gate"], [b_mla, b_dil, b_mem])
    o_proj = _matmul(merged, W["w_o"], name="o_proj")
    x1, h2 = _norm2(o_proj, x, W["g_post_mix"], W["g_pre_ffn"])

    u = _matmul(h2, W["w_ffn_up"], tn=1408, name="ffn_up")
    act = _conv_fwd(u, W["conv_w"], W["conv_b"])
    f = _matmul(act, W["w_ffn_down"], name="ffn_down")
    loss8, dx2, df, dg_post_ffn = _loss_head(f, x1, tgt, W["g_post_ffn"])
    loss = loss8[0, 0]

    G = {"g_post_ffn": dg_post_ffn}
    d_act = _matmul(df, W["w_ffn_down"], mode="nt", tn=1408, name="d_act")
    G["w_ffn_down"] = _matmul(act, df, mode="tn", tm=1408, tk=1024, name="dw_ffn_down")
    dzg, dzv, G["conv_w"], G["conv_b"] = _conv_bwd_dz(d_act, u, W["conv_w"], W["conv_b"])
    du = _conv_bwd_du(dzg, dzv, W["conv_w"])
    dh2 = _matmul(du, W["w_ffn_up"], mode="nt", tk=2816, name="d_h2")
    G["w_ffn_up"] = _matmul(h2, du, mode="tn", tn=1408, tk=1024, name="dw_ffn_up")
    dx1, do_proj, G["g_pre_ffn"], G["g_post_mix"] = _norm2_bwd(dx2, dh2, x1, o_proj, W["g_pre_ffn"], W["g_post_mix"])
    dmerged = _matmul(do_proj, W["w_o"], mode="nt", name="d_merged")
    G["w_o"] = _matmul(merged, do_proj, mode="tn", tk=1024, name="dw_o")
    db_mla, db_dil, db_mem, dproj_g, G["b_gate"] = _merge_bwd(dmerged, proj_g, W["b_gate"], [b_mla, b_dil, b_mem])

    dy_mem = _matmul(db_mem, W["w_br_mem"], mode="nt", name="d_y_mem")
    G["w_br_mem"] = _matmul(o_mem, db_mem, mode="tn", tk=1024, name="dw_br_mem")
    delta_mem = _delta(dy_mem, o_mem, MEM_HEADS, "delta_mem")
    dq_mem, dkv_mem = _mem_bwd(proj_m, kv_mem, dy_mem, lse_mem, delta_mem, scale=att_scale, name="attn_mem_bwd")
    G["w_mem_kv"] = _matmul(mem_n, dkv_mem, mode="tn", name="dw_mem_kv")
    dmem_n = _matmul(dkv_mem, W["w_mem_kv"], mode="nt", name="d_mem_n")
    G["g_mem"] = _gain_grad(dmem_n, mem, "dg_mem")

    dy_dil = _matmul(db_dil, W["w_br_dil"], mode="nt", name="d_y_dil")
    G["w_br_dil"] = _matmul(y_dil, db_dil, mode="tn", tk=1024, name="dw_br_dil")
    mix = _dil_mix_bwd(dy_dil, o_dil, lse_dil)
    do_dil, dl_dil = mix[:3], mix[3:]
    d_parts = [[None] * DIL_GROUPS for _ in range(3)]
    for g, (window, dil) in enumerate(DIL_PAIRS):
        L = S // dil
        view = proj_d.reshape(L, dil * 3 * DIL_HEADS * LANES)
        col = lambda part, g=g: (lambda hh: (hh // DIL_HPG) * n_dil_cols + (part * DIL_GROUPS + g) * DIL_HPG + hh % DIL_HPG)
        shp = (L, dil * DIL_W)
        dq, dk, dv = _band_bwd(
            view, view, view, do_dil[g].reshape(shp), lse_dil[g].reshape(shp), dl_dil[g].reshape(shp), slopes[g],
            n_heads=dil * DIL_HPG, qcol=col(0), kcol=col(1), vcol=col(2), L=L, scale=att_scale, slope_mul=float(dil),
            d_shape=shp, name=f"attn_dil{g}_bwd")
        for part, d in enumerate((dq, dk, dv)):
            d_parts[part][g] = d.reshape(S, DIL_W)
    dproj_d = jnp.concatenate([d for part in d_parts for d in part], axis=1)

    dy_mla = _matmul(db_mla, w_br_mla_p, mode="nt", name="d_y_mla")
    dw_br_mla_p = _matmul(o_mla, db_mla, mode="tn", tk=1024, name="dw_br_mla")
    G["w_br_mla"] = _unpad_heads(dw_br_mla_p, MLA_HEADS, MLA_NOPE, 0)
    delta_mla = _delta(dy_mla, o_mla, MLA_HEADS, "delta_mla")
    dq_mla, dk_mla, dv_mla = _causal_bwd(q_mla, k_mla, v_mla, dy_mla, lse_mla, delta_mla, name="attn_mla_bwd")
    dq_pre, dkpe = _mla_bwd_prep(dq_mla, dk_mla, pos, invf)
    dcq_n = _matmul(dq_pre, w_uq_p, mode="nt", tn=Q_RANK, name="d_cq")
    G["w_uq"] = _unpad_heads(_matmul(cq_n, dq_pre, mode="tn", tm=Q_RANK, tk=1024, name="dw_uq"), MLA_HEADS, MLA_QK, 1)
    dckv_a = _matmul(dk_mla, w_uk_p, mode="nt", tn=KV_RANK, name="d_ckv_k")
    dckv_b = _matmul(dv_mla, w_uv_p, mode="nt", tn=KV_RANK, name="d_ckv_v")
    dw_uk = _unpad_heads(_matmul(ckv_n, dk_mla, mode="tn", tm=KV_RANK, tk=1024, name="dw_uk"), MLA_HEADS, MLA_NOPE, 1)
    dw_uv = _unpad_heads(_matmul(ckv_n, dv_mla, mode="tn", tm=KV_RANK, tk=1024, name="dw_uv"), MLA_HEADS, MLA_NOPE, 1)
    G["w_ukv"] = jnp.concatenate([dw_uk.reshape(KV_RANK, MLA_HEADS, MLA_NOPE), dw_uv.reshape(KV_RANK, MLA_HEADS, MLA_NOPE)],
                                 axis=2).reshape(KV_RANK, -1)
    dproj_a, G["mla_q_norm"], G["mla_kv_norm"] = _mla_norm_bwd(dcq_n, dckv_a, dckv_b, dkpe, proj_a, W["mla_q_norm"],
                                                              W["mla_kv_norm"])

    dh = _matmul(dproj_a, w_a, mode="nt", name="d_h_a")
    dh = _matmul(dproj_d, w_d, mode="nt", tk=2304, add=dh, name="d_h_d")
    dh = _matmul(dq_mem, w_m, mode="nt", add=dh, name="d_h_m")
    dh = _matmul(dproj_g, w_g, mode="nt", add=dh, name="d_h_g")
    dw_a = _matmul(h, dproj_a, mode="tn", tn=768, tk=1024, name="dw_in_a")
    dw_d = _matmul(h, dproj_d, mode="tn", tn=1536, tk=1024, name="dw_in_d")
    dw_m = _matmul(h, dq_mem, mode="tn", tn=512, tk=1024, name="dw_in_m")
    dw_g = _matmul(h, dproj_g, mode="tn", tn=1536, tk=1024, name="dw_in_g")
    kr0 = Q_RANK + MLA_NOPE
    G["w_in"] = jnp.concatenate([dw_a[:, :Q_RANK], dw_a[:, Q_RANK + LANES:], dw_a[:, kr0:kr0 + MLA_ROPE], dw_d, dw_m, dw_g], axis=1)
    grad_x, G["g_pre_mix"] = _norm1_bwd(dx1, dh, x, W["g_pre_mix"])
    return loss, grad_x, G


WEIGHTS = ["g_pre_mix", "w_in", "b_gate", "mla_q_norm", "w_uq", "mla_kv_norm", "w_ukv", "g_mem", "w_mem_kv", "w_br_mla",
           "w_br_dil", "w_br_mem", "w_o", "g_post_mix", "g_pre_ffn", "w_ffn_up", "conv_w", "conv_b", "w_ffn_down", "g_post_ffn"]
GROUPS = [
    [("w_in", (D_MODEL, D_IN), 1)],
    [("w_uq", (Q_RANK, MLA_HEADS * MLA_QK), 1)],
    [("w_ukv", (KV_RANK, MLA_HEADS * 2 * MLA_NOPE), 1), ("w_br_mla", (MLA_HEADS * MLA_NOPE, D_MODEL), 1),
     ("w_br_dil", (DIL_W, D_MODEL), 1), ("w_br_mem", (MEM_W, D_MODEL), 1)],
    [("w_mem_kv", (D_MODEL, 2 * MEM_W), 0), ("w_o", (D_MODEL, D_MODEL), 0), ("w_ffn_down", (D_FF, D_MODEL), 0)],
    [("w_ffn_up", (D_MODEL, 2 * D_FF), 1)],
]
CONV_W = ("conv_w", (3, 2 * D_FF), 1)
REPLICATED = [("g_pre_mix", D_MODEL), ("b_gate", 3 * D_MODEL), ("mla_q_norm", Q_RANK), ("mla_kv_norm", KV_RANK), ("g_mem", D_MODEL),
              ("g_post_mix", D_MODEL), ("g_pre_ffn", D_MODEL), ("conv_b", 2 * D_FF), ("g_post_ffn", D_MODEL)]
SMALL_ROWS = 256
CONV_AT = sum(n for _, n in REPLICATED)
LOSS_AT = CONV_AT + 3 * 2 * D_FF


def _shard_shape(shape, axis):
    return tuple(d // N_CHIPS if a == axis else d for a, d in enumerate(shape))


def _group_shape(grp):
    shapes = [_shard_shape(shape, axis) for _, shape, axis in grp]
    assert len({s[1] for s in shapes}) == 1
    return sum(s[0] for s in shapes), shapes[0][1]


def _member_shards(a, axis):
    r, c = a.shape
    if axis == 0:
        return a.reshape(N_CHIPS, r // N_CHIPS, c)
    return a.reshape(r, N_CHIPS, c // N_CHIPS).transpose(1, 0, 2)


def _member_full(s, axis):
    n, r, c = s.shape
    if axis == 0:
        return s.reshape(n * r, c)
    return s.transpose(1, 0, 2).reshape(r, n * c)


def _my_weight_groups(w):
    out = []
    for grp in GROUPS:
        rows, width = _group_shape(grp)
        out.append(jnp.concatenate([w[name].astype(BF16) for name, _, _ in grp], axis=0).reshape(2, rows // 2, width))
    return out


def _full_weights(gathered, conv_all):
    out = {}
    for grp, ga in zip(GROUPS, gathered):
        ga = ga.reshape(N_CHIPS, -1, ga.shape[-1])
        off = 0
        for name, shape, axis in grp:
            rows = _shard_shape(shape, axis)[0]
            out[name] = _member_full(ga[:, off:off + rows], axis)
            off += rows
    out[CONV_W[0]] = _member_full(conv_all, CONV_W[2])
    return out


def _grad_groups(G):
    out = []
    for grp in GROUPS:
        rows, width = _group_shape(grp)
        a = jnp.concatenate([_member_shards(G[name], axis) for name, _, axis in grp], axis=1)
        out.append(a.reshape(N_CHIPS, 2, rows // 2, width).transpose(1, 0, 2, 3))
    return out


def _pack_small(vals, conv_g=None, loss=None):
    parts = [vals[name].reshape(-1) for name, _ in REPLICATED]
    if conv_g is not None:
        parts += [conv_g.reshape(-1), loss.reshape(1)]
    flat = jnp.concatenate(parts)
    return jnp.pad(flat, (0, SMALL_ROWS * LANES - flat.shape[0])).reshape(SMALL_ROWS, LANES)


def _unpack_small(packed):
    flat = packed.reshape(-1)
    out, off = {}, 0
    for name, n in REPLICATED:
        out[name] = flat[off:off + n].reshape(1, n)
        off += n
    return out


MESH = pl.DeviceIdType.MESH
HBM_SPEC = pl.BlockSpec(memory_space=pltpu.HBM)


def _place():
    x, y, c = lax.axis_index("x"), lax.axis_index("y"), lax.axis_index("c")
    chips = [(1 - x, y), (x, 1 - y), (1 - x, 1 - y)]
    return x, y, c, chips


def _remote(src, dst, send_sems, recv_sems, k, to):
    return pltpu.make_async_remote_copy(src_ref=src, dst_ref=dst, send_sem=send_sems.at[k], recv_sem=recv_sems.at[k],
                                        device_id=to, device_id_type=MESH)


def _gather_weights(groups, conv_w):
    n = len(groups)

    def body(*refs):
        srcs, conv_src, outs, conv_out = refs[:n], refs[n], refs[n + 1:2 * n + 1], refs[2 * n + 1]
        send_sems, recv_sems, local_sems = refs[2 * n + 2:]
        x, y, c, chips = _place()
        me = 2 * x + y
        sibling = (x, y, 1 - c)
        local = [pltpu.make_async_copy(srcs[g], outs[g].at[me], local_sems.at[g]) for g in range(n)]
        local.append(pltpu.make_async_copy(conv_src, conv_out.at[me], local_sems.at[n]))
        for cp in local:
            cp.start()
        first = [_remote(srcs[g].at[c], outs[g].at[me, c], send_sems, recv_sems, g * 3 + k, (px, py, c))
                 for k, (px, py) in enumerate(chips) for g in range(n)]
        first += [_remote(conv_src, conv_out.at[me], send_sems, recv_sems, 6 * n + k, (px, py, c)) for k, (px, py) in enumerate(chips)]
        for cp in first:
            cp.start()
        passed = []
        for k, (px, py) in enumerate(chips):
            for g in range(n):
                slot = outs[g].at[2 * px + py, c]
                _remote(slot, slot, send_sems, recv_sems, g * 3 + k, (px, py, c)).wait_recv()
                cp = _remote(slot, slot, send_sems, recv_sems, 3 * n + g * 3 + k, sibling)
                cp.start()
                passed.append(cp)
        for k, (px, py) in enumerate(chips):
            slot = conv_out.at[2 * px + py]
            _remote(slot, slot, send_sems, recv_sems, 6 * n + k, (px, py, c)).wait_recv()
            for g in range(n):
                slot = outs[g].at[2 * px + py, 1 - c]
                _remote(slot, slot, send_sems, recv_sems, 3 * n + g * 3 + k, sibling).wait_recv()
        for cp in first + passed:
            cp.wait_send()
        for cp in local:
            cp.wait()

    n_sem = 6 * n + 3
    outs = pl.pallas_call(
        body, name="comm_gather_weights", in_specs=[HBM_SPEC] * (n + 1), out_specs=[HBM_SPEC] * (n + 1),
        out_shape=[jax.ShapeDtypeStruct((N_CHIPS,) + g.shape, g.dtype) for g in groups]
        + [jax.ShapeDtypeStruct((N_CHIPS,) + conv_w.shape, conv_w.dtype)],
        scratch_shapes=[pltpu.SemaphoreType.DMA((n_sem,)), pltpu.SemaphoreType.DMA((n_sem,)), pltpu.SemaphoreType.DMA((n + 1,))],
    )(*groups, conv_w)
    return outs[:n], outs[n]


def _swap_halves(groups):
    n = len(groups)

    def body(*refs):
        srcs, outs, send_sems, recv_sems = refs[:n], refs[n:2 * n], refs[2 * n], refs[2 * n + 1]
        x, y, c, _ = _place()
        cps = [_remote(srcs[g].at[1 - c], outs[g], send_sems, recv_sems, g, (x, y, 1 - c)) for g in range(n)]
        for cp in cps:
            cp.start()
        for cp in cps:
            cp.wait()

    return pl.pallas_call(
        body, name="comm_swap_halves", in_specs=[HBM_SPEC] * n, out_specs=[HBM_SPEC] * n,
        out_shape=[jax.ShapeDtypeStruct(g.shape[1:], g.dtype) for g in groups],
        scratch_shapes=[pltpu.SemaphoreType.DMA((n,)), pltpu.SemaphoreType.DMA((n,))],
    )(*groups)


def _scatter_partials(groups):
    n = len(groups)

    def body(*refs):
        srcs, outs = refs[:n], refs[n:2 * n]
        send_sems, recv_sems, local_sems = refs[2 * n:]
        x, y, c, chips = _place()
        me = 2 * x + y
        local = [pltpu.make_async_copy(srcs[g].at[me], outs[g].at[me], local_sems.at[g]) for g in range(n)]
        for cp in local:
            cp.start()
        sends = [_remote(srcs[g].at[2 * px + py], outs[g].at[me], send_sems, recv_sems, g * 3 + k, (px, py, c))
                 for k, (px, py) in enumerate(chips) for g in range(n)]
        for cp in sends:
            cp.start()
        for k, (px, py) in enumerate(chips):
            for g in range(n):
                slot = outs[g].at[2 * px + py]
                _remote(slot, slot, send_sems, recv_sems, g * 3 + k, (px, py, c)).wait_recv()
        for cp in sends:
            cp.wait_send()
        for cp in local:
            cp.wait()

    return pl.pallas_call(
        body, name="comm_scatter_partials", in_specs=[HBM_SPEC] * n, out_specs=[HBM_SPEC] * n,
        out_shape=[jax.ShapeDtypeStruct(g.shape, g.dtype) for g in groups],
        scratch_shapes=[pltpu.SemaphoreType.DMA((3 * n,)), pltpu.SemaphoreType.DMA((3 * n,)), pltpu.SemaphoreType.DMA((n,))],
    )(*groups)


def _share_reduced(groups):
    n = len(groups)

    def body(*refs):
        srcs, outs = refs[:n], refs[n:2 * n]
        send_sems, recv_sems, local_sems = refs[2 * n:]
        x, y, c, _ = _place()
        local = [pltpu.make_async_copy(srcs[g], outs[g].at[c], local_sems.at[g]) for g in range(n)]
        for cp in local:
            cp.start()
        sends = [_remote(srcs[g], outs[g].at[c], send_sems, recv_sems, g, (x, y, 1 - c)) for g in range(n)]
        for cp in sends:
            cp.start()
        for g in range(n):
            _remote(srcs[g], outs[g].at[1 - c], send_sems, recv_sems, g, (x, y, 1 - c)).wait_recv()
        for cp in sends:
            cp.wait_send()
        for cp in local:
            cp.wait()

    return pl.pallas_call(
        body, name="comm_share_reduced", in_specs=[HBM_SPEC] * n, out_specs=[HBM_SPEC] * n,
        out_shape=[jax.ShapeDtypeStruct((2,) + g.shape, g.dtype) for g in groups],
        scratch_shapes=[pltpu.SemaphoreType.DMA((n,)), pltpu.SemaphoreType.DMA((n,)), pltpu.SemaphoreType.DMA((n,))],
    )(*groups)


def _allreduce_small(v):
    n_dev = 2 * N_CHIPS

    def body(src, out, buf, send_sems, recv_sems):
        x, y, c, _ = _place()
        me = 4 * x + 2 * y + c
        buf[me] = src[...]
        flips = [(k >> 2 & 1, k >> 1 & 1, k & 1) for k in range(1, n_dev)]
        sends = []
        for k, (fx, fy, fc) in enumerate(flips):
            to = ((1 - x) if fx else x, (1 - y) if fy else y, (1 - c) if fc else c)
            cp = _remote(src, buf.at[me], send_sems, recv_sems, k, to)
            cp.start()
            sends.append((cp, to))
        for k, (cp, to) in enumerate(sends):
            slot = buf.at[4 * to[0] + 2 * to[1] + to[2]]
            _remote(slot, slot, send_sems, recv_sems, k, to).wait_recv()
        for cp, _ in sends:
            cp.wait_send()
        acc = buf[0]
        for d in range(1, n_dev):
            acc = acc + buf[d]
        out[...] = acc

    vm = pl.BlockSpec(memory_space=pltpu.VMEM)
    return pl.pallas_call(
        body, name="comm_allreduce_small", in_specs=[vm], out_specs=vm, out_shape=jax.ShapeDtypeStruct(v.shape, F32),
        scratch_shapes=[pltpu.VMEM((n_dev,) + v.shape, F32), pltpu.SemaphoreType.DMA((n_dev - 1,)),
                        pltpu.SemaphoreType.DMA((n_dev - 1,))],
    )(v)


def _row_tile(rows, cap=256):
    return max(t for t in range(8, cap + 1, 8) if rows % t == 0)


def _add_sibling(mine, theirs, core, name):
    _, n, R, C = mine.shape
    t = _row_tile(R)

    def body(core_ref, a_ref, b_ref, o_ref):
        o_ref[...] = a_ref[...] + b_ref[...]

    sp = pl.BlockSpec((None, t, C), lambda k, i, core_ref: (k, i, 0))
    grid_spec = pltpu.PrefetchScalarGridSpec(
        num_scalar_prefetch=1, grid=(n, R // t),
        in_specs=[pl.BlockSpec((None, None, t, C), lambda k, i, core_ref: (core_ref[0], k, i, 0)), sp], out_specs=sp)
    return pl.pallas_call(body, name=name, grid_spec=grid_spec, out_shape=jax.ShapeDtypeStruct((n, R, C), F32),
                          compiler_params=_params("parallel", "parallel"))(core, mine, theirs)


def _add_chips(a, name):
    n, R, C = a.shape
    t = _row_tile(R)

    def body(a_ref, o_ref):
        acc = a_ref[0]
        for k in range(1, n):
            acc = acc + a_ref[k]
        o_ref[...] = acc

    return pl.pallas_call(body, name=name, grid=(R // t,), in_specs=[pl.BlockSpec((n, t, C), lambda i: (0, i, 0))],
                          out_specs=pl.BlockSpec((t, C), lambda i: (i, 0)),
                          out_shape=jax.ShapeDtypeStruct((R, C), F32), compiler_params=_params("parallel"))(a)


def _adamw(w, g, m, v, name, g_row0=0):
    R, C = w.shape
    t = math.gcd(math.gcd(R, g_row0), 128) if R % 8 == 0 else R
    assert g_row0 % t == 0
    c1 = 1.0 - ADAM_B1 ** ADAM_STEP
    c2 = 1.0 - ADAM_B2 ** ADAM_STEP

    def body(w_ref, g_ref, m_ref, v_ref, go_ref, d_ref, nm_ref, nv_ref):
        gv = g_ref[...]
        nm = ADAM_B1 * m_ref[...] + (1.0 - ADAM_B1) * gv
        nv = ADAM_B2 * v_ref[...] + (1.0 - ADAM_B2) * (gv * gv)
        go_ref[...] = gv
        d_ref[...] = -ADAM_LR * ((nm / c1) / (jnp.sqrt(nv / c2) + ADAM_EPS) + ADAM_WD * w_ref[...])
        nm_ref[...] = nm
        nv_ref[...] = nv

    sp = pl.BlockSpec((t, C), lambda i: (i, 0))
    g_sp = pl.BlockSpec((t, C), lambda i: (i + g_row0 // t, 0))
    return pl.pallas_call(body, name=name, grid=(R // t,), in_specs=[sp, g_sp, sp, sp], out_specs=[sp] * 4,
                          out_shape=[jax.ShapeDtypeStruct((R, C), F32)] * 4, compiler_params=_params("parallel"))(w, g, m, v)


def kernel(x, mem, positions, g_pre_mix, w_in, b_gate, mla_q_norm, w_uq, mla_kv_norm, w_ukv, g_mem, w_mem_kv, w_br_mla, w_br_dil, w_br_mem, w_o, g_post_mix, g_pre_ffn, w_ffn_up, conv_w, conv_b, w_ffn_down, g_post_ffn, loss_target, m_g_pre_mix, m_w_in, m_b_gate, m_mla_q_norm, m_w_uq, m_mla_kv_norm, m_w_ukv, m_g_mem, m_w_mem_kv, m_w_br_mla, m_w_br_dil, m_w_br_mem, m_w_o, m_g_post_mix, m_g_pre_ffn, m_w_ffn_up, m_conv_w, m_conv_b, m_w_ffn_down, m_g_post_ffn, v_g_pre_mix, v_w_in, v_b_gate, v_mla_q_norm, v_w_uq, v_mla_kv_norm, v_w_ukv, v_g_mem, v_w_mem_kv, v_w_br_mla, v_w_br_dil, v_w_br_mem, v_w_o, v_g_post_mix, v_g_pre_ffn, v_w_ffn_up, v_conv_w, v_conv_b, v_w_ffn_down, v_g_post_ffn):
    w_args = (g_pre_mix, w_in, b_gate, mla_q_norm, w_uq, mla_kv_norm, w_ukv, g_mem, w_mem_kv, w_br_mla, w_br_dil, w_br_mem, w_o,
              g_post_mix, g_pre_ffn, w_ffn_up, conv_w, conv_b, w_ffn_down, g_post_ffn)
    m_args = (m_g_pre_mix, m_w_in, m_b_gate, m_mla_q_norm, m_w_uq, m_mla_kv_norm, m_w_ukv, m_g_mem, m_w_mem_kv, m_w_br_mla,
              m_w_br_dil, m_w_br_mem, m_w_o, m_g_post_mix, m_g_pre_ffn, m_w_ffn_up, m_conv_w, m_conv_b, m_w_ffn_down, m_g_post_ffn)
    v_args = (v_g_pre_mix, v_w_in, v_b_gate, v_mla_q_norm, v_w_uq, v_mla_kv_norm, v_w_ukv, v_g_mem, v_w_mem_kv, v_w_br_mla,
              v_w_br_dil, v_w_br_mem, v_w_o, v_g_post_mix, v_g_pre_ffn, v_w_ffn_up, v_conv_w, v_conv_b, v_w_ffn_down, v_g_post_ffn)
    sharded = {name for grp in GROUPS for name, _, _ in grp} | {CONV_W[0]}

    def local(a, name):
        return a[0] if name in sharded else a

    w = {n: local(a, n) for n, a in zip(WEIGHTS, w_args)}
    m = {n: local(a, n) for n, a in zip(WEIGHTS, m_args)}
    v = {n: local(a, n) for n, a in zip(WEIGHTS, v_args)}

    full = _full_weights(*_gather_weights(_my_weight_groups(w), w[CONV_W[0]]))
    full.update({name: w[name] for name, _ in REPLICATED})

    loss_local, grad_x, G = _local_step(x[0], mem[0], positions, loss_target[0], full)

    core = lax.axis_index("c").astype(jnp.int32).reshape(1)
    mine = _grad_groups(G)
    theirs = _swap_halves(mine)
    partial = [_add_sibling(a, b, core, f"add_sibling_{i}") for i, (a, b) in enumerate(zip(mine, theirs))]
    reduced = [_add_chips(a, f"add_chips_{i}") for i, a in enumerate(_scatter_partials(partial))]
    shard_groups = [a.reshape(-1, a.shape[-1]) for a in _share_reduced(reduced)]
    small = _allreduce_small(_pack_small(G, G[CONV_W[0]], loss_local))
    flat = small.reshape(-1)
    loss = flat[LOSS_AT]
    conv_g = flat[CONV_AT:LOSS_AT].reshape(CONV_W[1])
    conv_cols = CONV_W[1][1] // N_CHIPS
    conv_g = lax.dynamic_slice_in_dim(conv_g, (2 * lax.axis_index("x") + lax.axis_index("y")) * conv_cols, conv_cols, axis=1)

    grads, deltas, new_m, new_v = {}, {}, {}, {}
    for grp, g_all in zip(GROUPS, shard_groups):
        off = 0
        for name, _, _ in grp:
            grads[name], deltas[name], new_m[name], new_v[name] = _adamw(w[name], g_all, m[name], v[name], "adamw_" + name, off)
            off += w[name].shape[0]
    name = CONV_W[0]
    grads[name], deltas[name], new_m[name], new_v[name] = _adamw(w[name], conv_g, m[name], v[name], "adamw_" + name)
    packed = _adamw(_pack_small(w), small, _pack_small(m), _pack_small(v), "adamw_small")
    for dst, packed_small in zip((grads, deltas, new_m, new_v), packed):
        dst.update(_unpack_small(packed_small))

    def out(d, name):
        return d[name][None] if name in sharded else d[name]

    return (loss, grad_x[None], *[out(grads, n) for n in WEIGHTS], *[out(deltas, n) for n in WEIGHTS],
            *[out(new_m, n) for n in WEIGHTS], *[out(new_v, n) for n in WEIGHTS])
```

```python
import functools
import math

import jax
import jax.numpy as jnp
from jax import lax
from jax.experimental import pallas as pl
from jax.experimental.pallas import tpu as pltpu

F32 = jnp.float32
BF16 = jnp.bfloat16

D_MODEL = 1024
N_MEM = 256
RMS_EPS = 1e-6
NEG_INF = -1e30
MLA_HEADS = 8
MLA_NOPE = 64
MLA_ROPE = 32
MLA_QK = 96
Q_RANK = 384
KV_RANK = 256
ROPE_THETA = 10000.0
DIL_PAIRS = ((128, 1), (512, 4), (2048, 16))
DIL_GROUPS = 3
DIL_HPG = 4
DIL_HEADS = 12
DIL_W = 512
MEM_HEADS = 4
MEM_W = 512
D_FF = 2816
OFF_Q = 384
OFF_KV = 640
OFF_KR = 672
OFF_DIL = 5280
OFF_MEMQ = 5792
D_IN = 8864
ADAM_LR = 0.001
ADAM_B1 = 0.9
ADAM_B2 = 0.999
ADAM_EPS = 1e-08
ADAM_WD = 0.01
ADAM_STEP = 10

LANES = 128
VMEM_LIMIT = 56 * 1024 * 1024

N_CHIPS = 4
ROW_TILE = 256

NN = (((1,), (0,)), ((), ()))
NT = (((1,), (1,)), ((), ()))
TN = (((0,), (0,)), ((), ()))


def _params(*sem):
    return pltpu.CompilerParams(dimension_semantics=sem, vmem_limit_bytes=VMEM_LIMIT)


def _full(shape):
    return pl.BlockSpec(shape, lambda *_: (0,) * len(shape))


def _matmul(a, b, *, mode="nn", out_dtype=F32, tm=1024, tn=1024, tk=None, add=None, scale=None, name):
    if mode == "nn":
        (M, K), N = a.shape, b.shape[1]
    elif mode == "nt":
        (M, K), N = a.shape, b.shape[0]
    else:
        (K, M), N = a.shape, b.shape[1]
    tm, tn = min(tm, M), min(tn, N)
    tk = K if tk is None else min(tk, K)
    assert M % tm == 0 and N % tn == 0 and K % tk == 0, (name, M, N, K, tm, tn, tk)
    nk = K // tk
    dims = {"nn": NN, "nt": NT, "tn": TN}[mode]
    a_spec = pl.BlockSpec((tk, tm), lambda i, j, k: (k, i)) if mode == "tn" else pl.BlockSpec((tm, tk), lambda i, j, k: (i, k))
    b_spec = pl.BlockSpec((tn, tk), lambda i, j, k: (j, k)) if mode == "nt" else pl.BlockSpec((tk, tn), lambda i, j, k: (k, j))
    o_spec = pl.BlockSpec((tm, tn), lambda i, j, k: (i, j))
    has_add = add is not None

    def body(*refs):
        a_ref, b_ref = refs[0], refs[1]
        c_ref = refs[2] if has_add else None
        o_ref = refs[3] if has_add else refs[2]
        part = lax.dot_general(a_ref[...].astype(BF16), b_ref[...].astype(BF16), dims, preferred_element_type=F32)
        if scale is not None:
            assert nk == 1 and not has_add
            part = part * jnp.where(pl.program_id(1) < scale[0], scale[1], 1.0)
        if nk == 1:
            if has_add:
                part = part + c_ref[...]
            o_ref[...] = part.astype(out_dtype)
        else:
            acc = refs[-1]
            k = pl.program_id(2)

            @pl.when(k == 0)
            def _():
                acc[...] = part

            @pl.when(k > 0)
            def _():
                acc[...] += part

            @pl.when(k == nk - 1)
            def _():
                r = acc[...]
                if has_add:
                    r = r + c_ref[...]
                o_ref[...] = r.astype(out_dtype)

    in_specs = [a_spec, b_spec] + ([o_spec] if has_add else [])
    args = (a, b) + ((add,) if has_add else ())
    return pl.pallas_call(
        body, name=name, grid=(M // tm, N // tn, nk), in_specs=in_specs, out_specs=o_spec,
        out_shape=jax.ShapeDtypeStruct((M, N), out_dtype),
        scratch_shapes=[pltpu.VMEM((tm, tn), F32)] if nk > 1 else [],
        compiler_params=_params("parallel", "parallel", "arbitrary"),
    )(*args)


def _rms_fwd_val(x, g):
    r = lax.rsqrt(jnp.mean(x * x, axis=-1, keepdims=True) + RMS_EPS)
    return (x * r) * g


def _rms_bwd_val(dy, x, g):
    r = lax.rsqrt(jnp.mean(x * x, axis=-1, keepdims=True) + RMS_EPS)
    xn = x * r
    gdy = g * dy
    dx = r * (gdy - xn * jnp.mean(gdy * xn, axis=-1, keepdims=True))
    return dx, dy * xn


def _rope_tables(pos, invf, inverse):
    ang = pos * invf
    cos, sin = jnp.cos(ang), jnp.sin(ang)
    lane = lax.broadcasted_iota(jnp.int32, ang.shape, 1)
    first = (lane >= MLA_NOPE) & (lane < MLA_NOPE + MLA_ROPE // 2)
    second = (lane >= MLA_NOPE + MLA_ROPE // 2) & (lane < MLA_QK)
    sgn = -1.0 if inverse else 1.0
    sa = jnp.where(first, -sgn * sin, 0.0)
    sb = jnp.where(second, sgn * sin, 0.0)
    return cos, sa, sb


def _rope_val(x, cos, sa, sb):
    half = MLA_ROPE // 2
    return x * cos + pltpu.roll(x, LANES - half, 1) * sa + pltpu.roll(x, half, 1) * sb


def _head_sum_bcast(v, n_heads):
    parts = []
    for h in range(n_heads):
        s = jnp.sum(v[:, h * LANES:(h + 1) * LANES], axis=1, keepdims=True)
        parts.append(jnp.broadcast_to(s, (v.shape[0], LANES)))
    return parts


def _row_spec(t, w):
    return pl.BlockSpec((t, w), lambda i: (i, 0))


def _acc_spec(w, rows=1):
    return pl.BlockSpec((rows, w), lambda i: (0, 0))


def _rmsnorm(x, g, out_dtype, name):
    S, W = x.shape
    t = min(ROW_TILE, S)

    def body(x_ref, g_ref, o_ref):
        o_ref[...] = _rms_fwd_val(x_ref[...], g_ref[...]).astype(out_dtype)

    return pl.pallas_call(body, name=name, grid=(S // t,), in_specs=[_row_spec(t, W), _acc_spec(W)],
                          out_specs=_row_spec(t, W), out_shape=jax.ShapeDtypeStruct((S, W), out_dtype),
                          compiler_params=_params("parallel"))(x, g)


def _mla_prep(proj_a, pos, invf, q_norm, kv_norm):
    S = proj_a.shape[0]
    t = ROW_TILE

    def body(a_ref, pos_ref, invf_ref, qn_ref, kvn_ref, cq_ref, ckv_ref, kpe_ref):
        a = a_ref[...]
        cq_ref[...] = _rms_fwd_val(a[:, 0:Q_RANK], qn_ref[...]).astype(BF16)
        ckv_ref[...] = _rms_fwd_val(a[:, Q_RANK + LANES:], kvn_ref[...]).astype(BF16)
        cos, sa, sb = _rope_tables(pos_ref[...], invf_ref[...], False)
        kpe_ref[...] = _rope_val(a[:, Q_RANK:Q_RANK + LANES], cos, sa, sb)

    return pl.pallas_call(
        body, name="mla_prep", grid=(S // t,),
        in_specs=[_row_spec(t, 768), _row_spec(t, 1), _acc_spec(LANES), _acc_spec(Q_RANK), _acc_spec(KV_RANK)],
        out_specs=[_row_spec(t, Q_RANK), _row_spec(t, KV_RANK), _row_spec(t, LANES)],
        out_shape=[jax.ShapeDtypeStruct((S, Q_RANK), BF16), jax.ShapeDtypeStruct((S, KV_RANK), BF16),
                   jax.ShapeDtypeStruct((S, LANES), F32)],
        compiler_params=_params("parallel"))(proj_a, pos, invf, q_norm, kv_norm)


def _qk_final(q_pre, k_pre, kpe, pos, invf):
    S, W = q_pre.shape
    t = ROW_TILE
    scale = MLA_QK ** -0.5

    def body(q_ref, k_ref, kpe_ref, pos_ref, invf_ref, qo_ref, ko_ref):
        cos, sa, sb = _rope_tables(pos_ref[...], invf_ref[...], False)
        kpe_v = kpe_ref[...]
        for h in range(MLA_HEADS):
            sl = slice(h * LANES, (h + 1) * LANES)
            qo_ref[:, sl] = (_rope_val(q_ref[:, sl], cos, sa, sb) * scale).astype(BF16)
            ko_ref[:, sl] = (k_ref[:, sl] + kpe_v).astype(BF16)

    return pl.pallas_call(
        body, name="qk_final", grid=(S // t,),
        in_specs=[_row_spec(t, W), _row_spec(t, W), _row_spec(t, LANES), _row_spec(t, 1), _acc_spec(LANES)],
        out_specs=[_row_spec(t, W), _row_spec(t, W)],
        out_shape=[jax.ShapeDtypeStruct((S, W), BF16)] * 2,
        compiler_params=_params("parallel"))(q_pre, k_pre, kpe, pos, invf)


def _mla_bwd_prep(dq, dk, pos, invf):
    S, W = dq.shape
    t = ROW_TILE
    scale = MLA_QK ** -0.5

    def body(dq_ref, dk_ref, pos_ref, invf_ref, dqp_ref, dkpe_ref):
        cos, sa, sb = _rope_tables(pos_ref[...], invf_ref[...], True)
        tot = jnp.zeros((t, LANES), F32)
        for h in range(MLA_HEADS):
            sl = slice(h * LANES, (h + 1) * LANES)
            dqp_ref[:, sl] = _rope_val(dq_ref[:, sl] * scale, cos, sa, sb).astype(BF16)
            tot = tot + dk_ref[:, sl]
        lane = lax.broadcasted_iota(jnp.int32, tot.shape, 1)
        tot = jnp.where((lane >= MLA_NOPE) & (lane < MLA_QK), tot, 0.0)
        dkpe_ref[...] = _rope_val(tot, cos, sa, sb)

    return pl.pallas_call(
        body, name="mla_bwd_prep", grid=(S // t,),
        in_specs=[_row_spec(t, W), _row_spec(t, W), _row_spec(t, 1), _acc_spec(LANES)],
        out_specs=[_row_spec(t, W), _row_spec(t, LANES)],
        out_shape=[jax.ShapeDtypeStruct((S, W), BF16), jax.ShapeDtypeStruct((S, LANES), F32)],
        compiler_params=_params("parallel"))(dq, dk, pos, invf)


def _mla_norm_bwd(dcq, dckv_a, dckv_b, dkpe, proj_a, q_norm, kv_norm):
    S = proj_a.shape[0]
    t = ROW_TILE

    def body(dcq_ref, da_ref, db_ref, dkpe_ref, a_ref, qn_ref, kvn_ref, o_ref, dqn_ref, dkvn_ref):
        i = pl.program_id(0)
        a = a_ref[...]
        dxq, gq = _rms_bwd_val(dcq_ref[...], a[:, 0:Q_RANK], qn_ref[...])
        dxkv, gkv = _rms_bwd_val(da_ref[...] + db_ref[...], a[:, Q_RANK + LANES:], kvn_ref[...])
        o_ref[:, 0:Q_RANK] = dxq.astype(BF16)
        o_ref[:, Q_RANK:Q_RANK + LANES] = dkpe_ref[...].astype(BF16)
        o_ref[:, Q_RANK + LANES:] = dxkv.astype(BF16)

        @pl.when(i == 0)
        def _():
            dqn_ref[...] = jnp.zeros_like(dqn_ref)
            dkvn_ref[...] = jnp.zeros_like(dkvn_ref)

        dqn_ref[...] += jnp.sum(gq, axis=0, keepdims=True)
        dkvn_ref[...] += jnp.sum(gkv, axis=0, keepdims=True)

    return pl.pallas_call(
        body, name="mla_norm_bwd", grid=(S // t,),
        in_specs=[_row_spec(t, Q_RANK), _row_spec(t, KV_RANK), _row_spec(t, KV_RANK), _row_spec(t, LANES),
                  _row_spec(t, 768), _acc_spec(Q_RANK), _acc_spec(KV_RANK)],
        out_specs=[_row_spec(t, 768), _acc_spec(Q_RANK), _acc_spec(KV_RANK)],
        out_shape=[jax.ShapeDtypeStruct((S, 768), BF16), jax.ShapeDtypeStruct((1, Q_RANK), F32),
                   jax.ShapeDtypeStruct((1, KV_RANK), F32)],
        compiler_params=_params("arbitrary"))(dcq, dckv_a, dckv_b, dkpe, proj_a, q_norm, kv_norm)


def _dil_mix(o_list, lse_list):
    S = o_list[0].shape[0]
    t = ROW_TILE

    def body(o0, o1, o2, l0, l1, l2, y_ref):
        ls = [l0[...], l1[...], l2[...]]
        m = jnp.maximum(jnp.maximum(ls[0], ls[1]), ls[2])
        es = [jnp.exp(l - m) for l in ls]
        den = es[0] + es[1] + es[2]
        y = (es[0] / den) * o0[...] + (es[1] / den) * o1[...] + (es[2] / den) * o2[...]
        y_ref[...] = y.astype(BF16)

    return pl.pallas_call(
        body, name="dil_mix", grid=(S // t,), in_specs=[_row_spec(t, DIL_W)] * 6, out_specs=_row_spec(t, DIL_W),
        out_shape=jax.ShapeDtypeStruct((S, DIL_W), BF16), compiler_params=_params("parallel"))(*o_list, *lse_list)


def _dil_mix_bwd(dy, o_list, lse_list):
    S = dy.shape[0]
    t = ROW_TILE

    def body(dy_ref, o0, o1, o2, l0, l1, l2, d0, d1, d2, e0, e1, e2):
        ls = [l0[...], l1[...], l2[...]]
        os_ = [o0[...], o1[...], o2[...]]
        m = jnp.maximum(jnp.maximum(ls[0], ls[1]), ls[2])
        es = [jnp.exp(l - m) for l in ls]
        den = es[0] + es[1] + es[2]
        ws = [e / den for e in es]
        dyv = dy_ref[...]
        y = ws[0] * os_[0] + ws[1] * os_[1] + ws[2] * os_[2]
        b = jnp.concatenate(_head_sum_bcast(dyv * y, DIL_HPG), axis=1)
        for w, d_ref, e_ref in zip(ws, (d0, d1, d2), (e0, e1, e2)):
            d_ref[...] = (w * dyv).astype(BF16)
            e_ref[...] = w * b

    return pl.pallas_call(
        body, name="dil_mix_bwd", grid=(S // t,), in_specs=[_row_spec(t, DIL_W)] * 7, out_specs=[_row_spec(t, DIL_W)] * 6,
        out_shape=[jax.ShapeDtypeStruct((S, DIL_W), BF16)] * 3 + [jax.ShapeDtypeStruct((S, DIL_W), F32)] * 3,
        compiler_params=_params("parallel"))(dy, *o_list, *lse_list)


def _delta(do, o, n_heads, name):
    S, W = do.shape
    t = ROW_TILE

    def body(do_ref, o_ref, d_ref):
        prod = do_ref[...].astype(F32) * o_ref[...].astype(F32)
        d_ref[...] = jnp.concatenate(_head_sum_bcast(prod, n_heads), axis=1)

    return pl.pallas_call(body, name=name, grid=(S // t,), in_specs=[_row_spec(t, W)] * 2, out_specs=_row_spec(t, W),
                          out_shape=jax.ShapeDtypeStruct((S, W), F32), compiler_params=_params("parallel"))(do, o)


def _merge(proj_g, b_gate, b_list):
    S = proj_g.shape[0]
    t = ROW_TILE

    def body(g_ref, b_ref, y0, y1, y2, o_ref):
        acc = jnp.zeros((t, D_MODEL), F32)
        for i, y in enumerate((y0, y1, y2)):
            sl = slice(i * D_MODEL, (i + 1) * D_MODEL)
            acc = acc + jax.nn.sigmoid(g_ref[:, sl] + b_ref[:, sl]) * y[...]
        o_ref[...] = acc.astype(BF16)

    return pl.pallas_call(
        body, name="merge", grid=(S // t,),
        in_specs=[_row_spec(t, 3 * D_MODEL), _acc_spec(3 * D_MODEL)] + [_row_spec(t, D_MODEL)] * 3,
        out_specs=_row_spec(t, D_MODEL), out_shape=jax.ShapeDtypeStruct((S, D_MODEL), BF16),
        compiler_params=_params("parallel"))(proj_g, b_gate, *b_list)


def _merge_bwd(dmerged, proj_g, b_gate, b_list):
    S = proj_g.shape[0]
    t = ROW_TILE

    def body(dm_ref, g_ref, b_ref, y0, y1, y2, d0, d1, d2, dz_ref, db_ref):
        i = pl.program_id(0)

        @pl.when(i == 0)
        def _():
            db_ref[...] = jnp.zeros_like(db_ref)

        dm = dm_ref[...]
        for k, (y, d_ref) in enumerate(zip((y0, y1, y2), (d0, d1, d2))):
            sl = slice(k * D_MODEL, (k + 1) * D_MODEL)
            s = jax.nn.sigmoid(g_ref[:, sl] + b_ref[:, sl])
            d_ref[...] = (s * dm).astype(BF16)
            dz = dm * y[...] * (s * (1.0 - s))
            dz_ref[:, sl] = dz.astype(BF16)
            db_ref[:, sl] += jnp.sum(dz, axis=0, keepdims=True)

    return pl.pallas_call(
        body, name="merge_bwd", grid=(S // t,),
        in_specs=[_row_spec(t, D_MODEL), _row_spec(t, 3 * D_MODEL), _acc_spec(3 * D_MODEL)] + [_row_spec(t, D_MODEL)] * 3,
        out_specs=[_row_spec(t, D_MODEL)] * 3 + [_row_spec(t, 3 * D_MODEL), _acc_spec(3 * D_MODEL)],
        out_shape=[jax.ShapeDtypeStruct((S, D_MODEL), BF16)] * 3
        + [jax.ShapeDtypeStruct((S, 3 * D_MODEL), BF16), jax.ShapeDtypeStruct((1, 3 * D_MODEL), F32)],
        compiler_params=_params("arbitrary"))(dmerged, proj_g, b_gate, *b_list)


def _norm2(o, x, g_post, g_pre):
    S = x.shape[0]
    t = ROW_TILE

    def body(o_ref, x_ref, gp_ref, gf_ref, x1_ref, h2_ref):
        x1 = x_ref[...] + _rms_fwd_val(o_ref[...], gp_ref[...])
        x1_ref[...] = x1
        h2_ref[...] = _rms_fwd_val(x1, gf_ref[...]).astype(BF16)

    return pl.pallas_call(
        body, name="norm2", grid=(S // t,),
        in_specs=[_row_spec(t, D_MODEL)] * 2 + [_acc_spec(D_MODEL)] * 2, out_specs=[_row_spec(t, D_MODEL)] * 2,
        out_shape=[jax.ShapeDtypeStruct((S, D_MODEL), F32), jax.ShapeDtypeStruct((S, D_MODEL), BF16)],
        compiler_params=_params("parallel"))(o, x, g_post, g_pre)


def _norm2_bwd(dx2, dh2, x1, o, g_pre, g_post):
    S = x1.shape[0]
    t = ROW_TILE

    def body(dx2_ref, dh2_ref, x1_ref, o_ref, gf_ref, gp_ref, dx1_ref, do_ref, dgf_ref, dgp_ref):
        i = pl.program_id(0)

        @pl.when(i == 0)
        def _():
            dgf_ref[...] = jnp.zeros_like(dgf_ref)
            dgp_ref[...] = jnp.zeros_like(dgp_ref)

        d1, gf = _rms_bwd_val(dh2_ref[...], x1_ref[...], gf_ref[...])
        dx1 = dx2_ref[...] + d1
        dx1_ref[...] = dx1
        do, gp = _rms_bwd_val(dx1, o_ref[...], gp_ref[...])
        do_ref[...] = do.astype(BF16)
        dgf_ref[...] += jnp.sum(gf, axis=0, keepdims=True)
        dgp_ref[...] += jnp.sum(gp, axis=0, keepdims=True)

    return pl.pallas_call(
        body, name="norm2_bwd", grid=(S // t,),
        in_specs=[_row_spec(t, D_MODEL)] * 4 + [_acc_spec(D_MODEL)] * 2,
        out_specs=[_row_spec(t, D_MODEL)] * 2 + [_acc_spec(D_MODEL)] * 2,
        out_shape=[jax.ShapeDtypeStruct((S, D_MODEL), F32), jax.ShapeDtypeStruct((S, D_MODEL), BF16),
                   jax.ShapeDtypeStruct((1, D_MODEL), F32), jax.ShapeDtypeStruct((1, D_MODEL), F32)],
        compiler_params=_params("arbitrary"))(dx2, dh2, x1, o, g_pre, g_post)


def _norm1_bwd(dx1, dh, x, g):
    S = x.shape[0]
    t = ROW_TILE

    def body(dx1_ref, dh_ref, x_ref, g_ref, dx_ref, dg_ref):
        i = pl.program_id(0)

        @pl.when(i == 0)
        def _():
            dg_ref[...] = jnp.zeros_like(dg_ref)

        d, gg = _rms_bwd_val(dh_ref[...], x_ref[...], g_ref[...])
        dx_ref[...] = dx1_ref[...] + d
        dg_ref[...] += jnp.sum(gg, axis=0, keepdims=True)

    return pl.pallas_call(
        body, name="norm1_bwd", grid=(S // t,),
        in_specs=[_row_spec(t, D_MODEL)] * 3 + [_acc_spec(D_MODEL)], out_specs=[_row_spec(t, D_MODEL), _acc_spec(D_MODEL)],
        out_shape=[jax.ShapeDtypeStruct((S, D_MODEL), F32), jax.ShapeDtypeStruct((1, D_MODEL), F32)],
        compiler_params=_params("arbitrary"))(dx1, dh, x, g)


def _gain_grad(dy, x, name):
    R, W = x.shape

    def body(dy_ref, x_ref, dg_ref):
        xv = x_ref[...]
        r = lax.rsqrt(jnp.mean(xv * xv, axis=-1, keepdims=True) + RMS_EPS)
        dg_ref[...] = jnp.sum(dy_ref[...] * (xv * r), axis=0, keepdims=True)

    return pl.pallas_call(body, name=name, grid=(1,), in_specs=[_full((R, W))] * 2, out_specs=_full((1, W)),
                          out_shape=jax.ShapeDtypeStruct((1, W), F32), compiler_params=_params("arbitrary"))(dy, x)


def _loss_head(f, x1, tgt, g):
    S = f.shape[0]
    t = ROW_TILE

    def body(f_ref, x1_ref, t_ref, g_ref, loss_ref, dx2_ref, df_ref, dg_ref):
        i = pl.program_id(0)

        @pl.when(i == 0)
        def _():
            loss_ref[...] = jnp.zeros_like(loss_ref)
            dg_ref[...] = jnp.zeros_like(dg_ref)

        fv, gv = f_ref[...], g_ref[...]
        err = x1_ref[...] + _rms_fwd_val(fv, gv) - t_ref[...]
        part = jnp.sum(jnp.mean(err * err, axis=-1, keepdims=True), axis=0, keepdims=True)
        loss_ref[...] += jnp.broadcast_to(0.5 * part, loss_ref.shape)
        dx2 = err * (1.0 / D_MODEL)
        dx2_ref[...] = dx2
        df, gg = _rms_bwd_val(dx2, fv, gv)
        df_ref[...] = df.astype(BF16)
        dg_ref[...] += jnp.sum(gg, axis=0, keepdims=True)

    return pl.pallas_call(
        body, name="loss_head", grid=(S // t,),
        in_specs=[_row_spec(t, D_MODEL)] * 3 + [_acc_spec(D_MODEL)],
        out_specs=[_acc_spec(LANES, 8), _row_spec(t, D_MODEL), _row_spec(t, D_MODEL), _acc_spec(D_MODEL)],
        out_shape=[jax.ShapeDtypeStruct((8, LANES), F32), jax.ShapeDtypeStruct((S, D_MODEL), F32),
                   jax.ShapeDtypeStruct((S, D_MODEL), BF16), jax.ShapeDtypeStruct((1, D_MODEL), F32)],
        compiler_params=_params("arbitrary"))(f, x1, tgt, g)


CONV_TC = 256
CONV_TT = 512
HALO = 8


def _shift_down(u, halo, first):
    row = lax.broadcasted_iota(jnp.int32, u.shape, 0)
    h6 = jnp.where(first, 0.0, halo[HALO - 2:HALO - 1, :])
    h7 = jnp.where(first, 0.0, halo[HALO - 1:HALO, :])
    s1 = jnp.where(row == 0, h7, pltpu.roll(u, 1, 0))
    s2 = jnp.where(row == 0, h6, jnp.where(row == 1, h7, pltpu.roll(u, 2, 0)))
    return s1, s2


def _conv_specs(tt, n_c, n_t, lead):
    def halo_row(i):
        return jnp.maximum(i * (tt // HALO) - 1, 0) if lead else jnp.minimum((i + 1) * (tt // HALO), n_t * (tt // HALO) - 1)
    return [
        pl.BlockSpec((tt, CONV_TC), lambda j, i: (i, j)),
        pl.BlockSpec((tt, CONV_TC), lambda j, i: (i, j + n_c)),
        pl.BlockSpec((HALO, CONV_TC), lambda j, i: (halo_row(i), j)),
        pl.BlockSpec((HALO, CONV_TC), lambda j, i: (halo_row(i), j + n_c)),
    ]


def _conv_z(ug, uv, hg, hv, w_g, w_v, b_g, b_v, first):
    g1, g2 = _shift_down(ug, hg, first)
    v1, v2 = _shift_down(uv, hv, first)
    zg = b_g + w_g[0:1, :] * g2
    zg = zg + w_g[1:2, :] * g1
    zg = zg + w_g[2:3, :] * ug
    zv = b_v + w_v[0:1, :] * v2
    zv = zv + w_v[1:2, :] * v1
    zv = zv + w_v[2:3, :] * uv
    return zg, zv, (g2, g1, ug), (v2, v1, uv)


def _conv_fwd(u, conv_w, conv_b):
    S = u.shape[0]
    tt = min(CONV_TT, S)
    n_c, n_t = D_FF // CONV_TC, S // tt
    wspec = [pl.BlockSpec((3, CONV_TC), lambda j, i: (0, j)), pl.BlockSpec((3, CONV_TC), lambda j, i: (0, j + n_c)),
             pl.BlockSpec((1, CONV_TC), lambda j, i: (0, j)), pl.BlockSpec((1, CONV_TC), lambda j, i: (0, j + n_c))]

    def body(ug_ref, uv_ref, hg_ref, hv_ref, wg_ref, wv_ref, bg_ref, bv_ref, a_ref):
        first = pl.program_id(1) == 0
        zg, zv, _, _ = _conv_z(ug_ref[...], uv_ref[...], hg_ref, hv_ref, wg_ref, wv_ref, bg_ref[...], bv_ref[...], first)
        a_ref[...] = (zg * jax.nn.sigmoid(zg) * zv).astype(BF16)

    return pl.pallas_call(
        body, name="conv_fwd", grid=(n_c, n_t), in_specs=_conv_specs(tt, n_c, n_t, True) + wspec,
        out_specs=pl.BlockSpec((tt, CONV_TC), lambda j, i: (i, j)), out_shape=jax.ShapeDtypeStruct((S, D_FF), BF16),
        compiler_params=_params("parallel", "parallel"))(u, u, u, u, conv_w, conv_w, conv_b, conv_b)


def _conv_bwd_dz(da, u, conv_w, conv_b):
    S = u.shape[0]
    tt = min(CONV_TT, S)
    n_c, n_t = D_FF // CONV_TC, S // tt
    wspec = [pl.BlockSpec((3, CONV_TC), lambda j, i: (0, j)), pl.BlockSpec((3, CONV_TC), lambda j, i: (0, j + n_c)),
             pl.BlockSpec((1, CONV_TC), lambda j, i: (0, j)), pl.BlockSpec((1, CONV_TC), lambda j, i: (0, j + n_c))]
    tile = pl.BlockSpec((tt, CONV_TC), lambda j, i: (i, j))
    tile_v = pl.BlockSpec((tt, CONV_TC), lambda j, i: (i, j + n_c))

    def body(da_ref, ug_ref, uv_ref, hg_ref, hv_ref, wg_ref, wv_ref, bg_ref, bv_ref,
             dzg_ref, dzv_ref, dwg_ref, dwv_ref, dbg_ref, dbv_ref):
        i = pl.program_id(1)
        zg, zv, gs, vs = _conv_z(ug_ref[...], uv_ref[...], hg_ref, hv_ref, wg_ref, wv_ref, bg_ref[...], bv_ref[...], i == 0)
        dav = da_ref[...]
        sg = jax.nn.sigmoid(zg)
        dzv = dav * (zg * sg)
        dzg = dav * zv * (sg * (1.0 + zg * (1.0 - sg)))
        dzg_ref[...] = dzg
        dzv_ref[...] = dzv

        @pl.when(i == 0)
        def _():
            for r in (dwg_ref, dwv_ref, dbg_ref, dbv_ref):
                r[...] = jnp.zeros_like(r)

        for k in range(3):
            dwg_ref[k:k + 1, :] += jnp.sum(dzg * gs[k], axis=0, keepdims=True)
            dwv_ref[k:k + 1, :] += jnp.sum(dzv * vs[k], axis=0, keepdims=True)
        dbg_ref[...] += jnp.sum(dzg, axis=0, keepdims=True)
        dbv_ref[...] += jnp.sum(dzv, axis=0, keepdims=True)

    outs = pl.pallas_call(
        body, name="conv_bwd_dz", grid=(n_c, n_t), in_specs=[tile] + _conv_specs(tt, n_c, n_t, True) + wspec,
        out_specs=[tile, tile] + [pl.BlockSpec((3, CONV_TC), lambda j, i: (0, j))] * 2 + [pl.BlockSpec((1, CONV_TC), lambda j, i: (0, j))] * 2,
        out_shape=[jax.ShapeDtypeStruct((S, D_FF), F32)] * 2 + [jax.ShapeDtypeStruct((3, D_FF), F32)] * 2
        + [jax.ShapeDtypeStruct((1, D_FF), F32)] * 2,
        compiler_params=_params("parallel", "arbitrary"))(da, u, u, u, u, conv_w, conv_w, conv_b, conv_b)
    dzg, dzv, dwg, dwv, dbg, dbv = outs
    return dzg, dzv, jnp.concatenate([dwg, dwv], axis=1), jnp.concatenate([dbg, dbv], axis=1)


def _conv_bwd_du(dzg, dzv, conv_w):
    S = dzg.shape[0]
    tt = min(CONV_TT, S)
    n_c, n_t = D_FF // CONV_TC, S // tt
    steps = tt // HALO

    def nxt(i):
        return jnp.minimum((i + 1) * steps, n_t * steps - 1)

    tile_g = pl.BlockSpec((tt, CONV_TC), lambda h, j, i: (i * (1 - h), j * (1 - h)))
    tile_v = pl.BlockSpec((tt, CONV_TC), lambda h, j, i: (i * h, j * h))
    halo_g = pl.BlockSpec((HALO, CONV_TC), lambda h, j, i: (nxt(i) * (1 - h), j * (1 - h)))
    halo_v = pl.BlockSpec((HALO, CONV_TC), lambda h, j, i: (nxt(i) * h, j * h))
    wsp = pl.BlockSpec((3, CONV_TC), lambda h, j, i: (0, j + h * n_c))

    def body(zg_ref, zv_ref, hg_ref, hv_ref, w_ref, du_ref):
        h = pl.program_id(0)
        last = pl.program_id(2) == n_t - 1
        z = jnp.where(h == 0, zg_ref[...], zv_ref[...])
        hal = jnp.where(h == 0, hg_ref[...], hv_ref[...])
        row = lax.broadcasted_iota(jnp.int32, z.shape, 0)
        h0 = jnp.where(last, 0.0, hal[0:1, :])
        h1 = jnp.where(last, 0.0, hal[1:2, :])
        u1 = jnp.where(row == tt - 1, h0, pltpu.roll(z, tt - 1, 0))
        u2 = jnp.where(row == tt - 1, h1, jnp.where(row == tt - 2, h0, pltpu.roll(z, tt - 2, 0)))
        du = w_ref[2:3, :] * z + w_ref[1:2, :] * u1 + w_ref[0:1, :] * u2
        du_ref[...] = du.astype(BF16)

    return pl.pallas_call(
        body, name="conv_bwd_du", grid=(2, n_c, n_t), in_specs=[tile_g, tile_v, halo_g, halo_v, wsp],
        out_specs=pl.BlockSpec((tt, CONV_TC), lambda h, j, i: (i, j + h * n_c)),
        out_shape=jax.ShapeDtypeStruct((S, 2 * D_FF), BF16),
        compiler_params=_params("parallel", "parallel", "parallel"))(dzg, dzv, dzg, dzv, conv_w)


BAND = 128


MEM_TQ = 512


def _mem_fwd(q, kv, *, name):
    S, W = q.shape
    M = kv.shape[0]
    nh = W // LANES
    tq = min(MEM_TQ, S)

    def body(q_ref, k_ref, v_ref, o_ref, l_ref):
        s = lax.dot_general(q_ref[...], k_ref[...].astype(BF16), NT, preferred_element_type=F32)
        m = jnp.max(s, axis=1, keepdims=True)
        p = jnp.exp(s - m)
        l = jnp.sum(p, axis=1, keepdims=True)
        o_ref[...] = (lax.dot_general(p.astype(BF16), v_ref[...].astype(BF16), NN, preferred_element_type=F32) / l).astype(BF16)
        l_ref[...] = jnp.broadcast_to(m + jnp.log(l), (tq, LANES))

    blk = pl.BlockSpec((tq, LANES), lambda hh, i: (i, hh))
    return pl.pallas_call(
        body, name=name, grid=(nh, S // tq),
        in_specs=[blk, pl.BlockSpec((M, LANES), lambda hh, i: (0, hh)), pl.BlockSpec((M, LANES), lambda hh, i: (0, hh + nh))],
        out_specs=[blk, blk], out_shape=[jax.ShapeDtypeStruct((S, W), BF16), jax.ShapeDtypeStruct((S, W), F32)],
        compiler_params=_params("parallel", "parallel"))(q, kv, kv)


def _mem_bwd(q, kv, do, lse, delta, *, scale, name):
    S, W = q.shape
    M = kv.shape[0]
    nh = W // LANES
    tq = min(MEM_TQ, S)

    def body(q_ref, k_ref, v_ref, do_ref, l_ref, d_ref, dq_ref, dk_ref, dv_ref):
        i = pl.program_id(1)

        @pl.when(i == 0)
        def _():
            dk_ref[...] = jnp.zeros_like(dk_ref)
            dv_ref[...] = jnp.zeros_like(dv_ref)

        qv = q_ref[...]
        kv_, vv = k_ref[...].astype(BF16), v_ref[...].astype(BF16)
        dov = do_ref[...].astype(BF16)
        s = lax.dot_general(qv, kv_, NT, preferred_element_type=F32)
        p = jnp.exp(s - l_ref[...][:, 0:1])
        dp = lax.dot_general(dov, vv, NT, preferred_element_type=F32)
        ds = (p * (dp - d_ref[...][:, 0:1])).astype(BF16)
        dq_ref[...] = lax.dot_general(ds, kv_, NN, preferred_element_type=F32) * scale
        dk_ref[...] += lax.dot_general(ds, qv, TN, preferred_element_type=F32)
        dv_ref[...] += lax.dot_general(p.astype(BF16), dov, TN, preferred_element_type=F32)

    blk = pl.BlockSpec((tq, LANES), lambda hh, i: (i, hh))
    kblk = pl.BlockSpec((M, LANES), lambda hh, i: (0, hh))
    vblk = pl.BlockSpec((M, LANES), lambda hh, i: (0, hh + nh))
    dq, dk, dv = pl.pallas_call(
        body, name=name, grid=(nh, S // tq), in_specs=[blk, kblk, vblk, blk, blk, blk], out_specs=[blk, kblk, kblk],
        out_shape=[jax.ShapeDtypeStruct((S, W), F32), jax.ShapeDtypeStruct((M, W), F32), jax.ShapeDtypeStruct((M, W), F32)],
        compiler_params=_params("parallel", "arbitrary"))(q, kv, kv, do, lse, delta)
    return dq, jnp.concatenate([dk, dv], axis=1)


CAUSAL_BLOCK = 512
STRIP = 32


def _causal_fwd(q, k, v, *, name):
    S, W = q.shape
    T = min(CAUSAL_BLOCK, S // 2)
    n_strips = T // STRIP

    def body(q_ref, k_ref, v_ref, o_ref, lse_ref, s0, s1, p0, p1, a0, a1, acc_scr):
        i = pl.program_id(1)
        s_scr, p_scr, a_scr = (s0, s1), (p0, p1), (a0, a1)

        def rows(j):
            return pl.ds(pl.multiple_of(j * T, T), T)

        def scores(j, slot):
            s_scr[slot][...] = lax.dot_general(q_ref[...], k_ref[rows(j), :], NT, preferred_element_type=F32)

        def softmax(slot, stats, diag):
            def strip(r):
                s = s_scr[slot][r * STRIP:(r + 1) * STRIP, :]
                if diag:
                    row = r * STRIP + lax.broadcasted_iota(jnp.int32, s.shape, 0)
                    s = jnp.where(row >= lax.broadcasted_iota(jnp.int32, s.shape, 1), s, NEG_INF)
                return s

            m_new = [jnp.maximum(m_old, jnp.max(strip(r), axis=1, keepdims=True)) for r, (m_old, _) in enumerate(stats)]
            new = []
            for r, (m_old, l_old) in enumerate(stats):
                rs = slice(r * STRIP, (r + 1) * STRIP)
                p = jnp.exp(strip(r) - m_new[r])
                alpha = jnp.exp(m_old - m_new[r])
                new.append((m_new[r], alpha * l_old + jnp.sum(p, axis=1, keepdims=True)))
                a_scr[slot][rs, :] = alpha
                p_scr[slot][rs, :] = p.astype(BF16)
            return tuple(new)

        def values(j, slot):
            acc_scr[...] = a_scr[slot][...] * acc_scr[...] + lax.dot_general(p_scr[slot][...], v_ref[rows(j), :], NN,
                                                                            preferred_element_type=F32)

        def trip(j, stats, mine, other):
            scores(j + 1, other)
            stats = softmax(mine, stats, False)
            values(jnp.maximum(j - 1, 0), other)
            return stats

        def pair(jj, stats):
            return trip(2 * jj + 1, trip(2 * jj, stats, 0, 1), 1, 0)

        def last(stats, mine, other):
            stats = softmax(mine, stats, True)
            values(jnp.maximum(i - 1, 0), other)
            values(i, mine)
            for r, (m, l) in enumerate(stats):
                rs = slice(r * STRIP, (r + 1) * STRIP)
                o_ref[rs, :] = (acc_scr[rs, :] / l).astype(BF16)
                lse_ref[rs, :] = jnp.broadcast_to(m + jnp.log(l), (STRIP, LANES))

        acc_scr[...] = jnp.zeros_like(acc_scr)
        p1[...] = jnp.zeros_like(p1)
        a1[...] = jnp.ones_like(a1)
        scores(0, 0)
        init = tuple((jnp.full((STRIP, 1), NEG_INF, F32), jnp.zeros((STRIP, 1), F32)) for _ in range(n_strips))
        stats = lax.fori_loop(0, i // 2, pair, init)

        @pl.when(i % 2 == 1)
        def _():
            last(trip(i - 1, stats, 0, 1), 1, 0)

        @pl.when(i % 2 == 0)
        def _():
            last(stats, 0, 1)

    blk = pl.BlockSpec((T, LANES), lambda hh, i: (i, hh))
    whole = pl.BlockSpec((S, LANES), lambda hh, i: (0, hh))
    return pl.pallas_call(
        body, name=name, grid=(W // LANES, S // T), in_specs=[blk, whole, whole], out_specs=[blk, blk],
        out_shape=[jax.ShapeDtypeStruct((S, W), BF16), jax.ShapeDtypeStruct((S, W), F32)],
        scratch_shapes=[pltpu.VMEM((T, T), F32)] * 2 + [pltpu.VMEM((T, T), BF16)] * 2 + [pltpu.VMEM((T, 1), F32)] * 2
        + [pltpu.VMEM((T, LANES), F32)],
        compiler_params=_params("parallel", "arbitrary"))(q, k, v)


def _causal_bwd(q, k, v, do, lse, delta, *, name):
    S, W = q.shape
    T = min(CAUSAL_BLOCK, S // 2)
    nq = S // T
    n_strips, n_col = T // STRIP, T // LANES

    def body(q_ref, k_ref, v_ref, do_ref, l_ref, d_ref, dq_ref, dk_ref, dv_ref, s0, s1, e0, e1, p0, p1, g0, g1):
        j = pl.program_id(1)
        s_scr, e_scr, p_scr, g_scr = (s0, s1), (e0, e1), (p0, p1), (g0, g1)

        @pl.when(j == 0)
        def _():
            dq_ref[...] = jnp.zeros_like(dq_ref)

        dk_ref[...] = jnp.zeros_like(dk_ref)
        dv_ref[...] = jnp.zeros_like(dv_ref)

        def rows(i):
            return pl.ds(pl.multiple_of(jnp.minimum(i, nq - 1) * T, T), T)

        def products(i, slot):
            r = rows(i)
            s_scr[slot][...] = lax.dot_general(q_ref[r, :], k_ref[...], NT, preferred_element_type=F32)
            e_scr[slot][...] = lax.dot_general(do_ref[r, :].astype(BF16), v_ref[...], NT, preferred_element_type=F32)

        def pointwise(i, slot, diag):
            base = pl.multiple_of(i * T, T)
            for r in range(n_strips):
                rs = slice(r * STRIP, (r + 1) * STRIP)
                lse_r = l_ref[pl.ds(base + r * STRIP, STRIP), :]
                del_r = d_ref[pl.ds(base + r * STRIP, STRIP), :]
                for c in range(n_col):
                    cs = slice(c * LANES, (c + 1) * LANES)
                    if diag and c * LANES > (r + 1) * STRIP - 1:
                        p_scr[slot][rs, cs] = jnp.zeros((STRIP, LANES), BF16)
                        g_scr[slot][rs, cs] = jnp.zeros((STRIP, LANES), BF16)
                        continue
                    sv = s_scr[slot][rs, cs]
                    if diag and (c + 1) * LANES - 1 > r * STRIP:
                        row = r * STRIP + lax.broadcasted_iota(jnp.int32, sv.shape, 0)
                        col = c * LANES + lax.broadcasted_iota(jnp.int32, sv.shape, 1)
                        sv = jnp.where(row >= col, sv, NEG_INF)
                    p = jnp.exp(sv - lse_r)
                    p_scr[slot][rs, cs] = p.astype(BF16)
                    g_scr[slot][rs, cs] = (p * (e_scr[slot][rs, cs] - del_r)).astype(BF16)

        def gradients(i, slot):
            r = rows(i)
            qi, doi = q_ref[r, :], do_ref[r, :].astype(BF16)
            g = g_scr[slot][...]
            dv_ref[...] += lax.dot_general(p_scr[slot][...], doi, TN, preferred_element_type=F32)
            dk_ref[...] += lax.dot_general(g, qi, TN, preferred_element_type=F32)
            dq_ref[r, :] += lax.dot_general(g, k_ref[...], NN, preferred_element_type=F32)

        p1[...] = jnp.zeros_like(p1)
        g1[...] = jnp.zeros_like(g1)
        products(j, 0)
        products(j + 1, 1)
        pointwise(j, 0, True)
        gradients(j, 1)

        def trip(i, mine, other):
            products(i + 1, other)
            pointwise(i, mine, False)
            gradients(i - 1, other)

        def pair(t, _):
            trip(j + 1 + 2 * t, 1, 0)
            trip(j + 2 + 2 * t, 0, 1)
            return 0

        n_rest = nq - 1 - j
        lax.fori_loop(0, n_rest // 2, pair, 0)

        @pl.when(n_rest % 2 == 1)
        def _():
            trip(nq - 1, 1, 0)
            gradients(nq - 1, 1)

        @pl.when(n_rest % 2 == 0)
        def _():
            gradients(nq - 1, 0)

    blk = pl.BlockSpec((T, LANES), lambda hh, j: (j, hh))
    whole = pl.BlockSpec((S, LANES), lambda hh, j: (0, hh))
    return pl.pallas_call(
        body, name=name, grid=(W // LANES, S // T), in_specs=[whole, blk, blk, whole, whole, whole],
        out_specs=[whole, blk, blk], out_shape=[jax.ShapeDtypeStruct((S, W), F32)] * 3,
        scratch_shapes=[pltpu.VMEM((T, T), F32)] * 4 + [pltpu.VMEM((T, T), BF16)] * 4,
        compiler_params=_params("parallel", "arbitrary"))(q, k, v, do, lse, delta)


BAND_TQ = 512


def _band_window(i, sub, nsub, L, q_ref, k_ref, v_ref, slope):
    kw = min(2 * BAND, L)
    n = i * nsub + sub
    k0 = 0 if kw == L else pl.multiple_of(jnp.maximum(n - 1, 0) * BAND, BAND)
    win = pl.ds(k0, kw)
    qs = q_ref[sub * BAND:(sub + 1) * BAND, :]
    kwv, vwv = k_ref[win, :], v_ref[win, :]
    s = lax.dot_general(qs, kwv, NT, preferred_element_type=F32)
    dist = (n * BAND + lax.broadcasted_iota(jnp.int32, s.shape, 0)) - (k0 + lax.broadcasted_iota(jnp.int32, s.shape, 1))
    s = jnp.where((dist >= 0) & (dist <= BAND), s - slope * dist.astype(F32), NEG_INF)
    return win, qs, kwv, vwv, s


def _band_fwd(q, k, v, slopes, *, n_heads, qcol, kcol, vcol, L, slope_mul, o_shape, name):
    tq = min(BAND_TQ, L)
    nsub = tq // BAND

    def body(sl_ref, q_ref, k_ref, v_ref, o_ref, l_ref):
        hh, i = pl.program_id(0), pl.program_id(1)
        slope = sl_ref[hh % DIL_HPG] * slope_mul
        for sub in range(nsub):
            _, _, _, vwv, s = _band_window(i, sub, nsub, L, q_ref, k_ref, v_ref, slope)
            m = jnp.max(s, axis=1, keepdims=True)
            p = jnp.exp(s - m)
            l = jnp.sum(p, axis=1, keepdims=True)
            rows = slice(sub * BAND, (sub + 1) * BAND)
            o_ref[rows, :] = lax.dot_general(p.astype(BF16), vwv, NN, preferred_element_type=F32) / l
            l_ref[rows, :] = jnp.broadcast_to(m + jnp.log(l), (BAND, LANES))

    whole = lambda col: pl.BlockSpec((L, LANES), lambda hh, i: (0, col(hh)))
    o_spec = pl.BlockSpec((tq, LANES), lambda hh, i: (i, hh))
    return pl.pallas_call(
        body, name=name, grid=(n_heads, L // tq),
        in_specs=[pl.BlockSpec(memory_space=pltpu.SMEM), pl.BlockSpec((tq, LANES), lambda hh, i: (i, qcol(hh))), whole(kcol), whole(vcol)],
        out_specs=[o_spec, o_spec], out_shape=[jax.ShapeDtypeStruct(o_shape, F32)] * 2,
        compiler_params=_params("parallel", "arbitrary"))(slopes, q, k, v)


def _band_bwd(q, k, v, do, lse, delta, slopes, *, n_heads, qcol, kcol, vcol, L, scale, slope_mul, d_shape, name):
    tq = min(BAND_TQ, L)
    nsub = tq // BAND
    n_steps = L // tq

    def body(sl_ref, q_ref, k_ref, v_ref, do_ref, l_ref, d_ref, dq_ref, dk_ref, dv_ref, dk_acc, dv_acc):
        hh, i = pl.program_id(0), pl.program_id(1)
        slope = sl_ref[hh % DIL_HPG] * slope_mul

        @pl.when(i == 0)
        def _():
            dk_acc[...] = jnp.zeros_like(dk_acc)
            dv_acc[...] = jnp.zeros_like(dv_acc)

        for sub in range(nsub):
            win, qs, kwv, vwv, s = _band_window(i, sub, nsub, L, q_ref, k_ref, v_ref, slope)
            rows = slice(sub * BAND, (sub + 1) * BAND)
            dos = do_ref[rows, :]
            p = jnp.exp(s - l_ref[rows, :][:, 0:1])
            dp = lax.dot_general(dos, vwv, NT, preferred_element_type=F32)
            ds = (p * (dp - d_ref[rows, :][:, 0:1])).astype(BF16)
            dq_ref[rows, :] = (lax.dot_general(ds, kwv, NN, preferred_element_type=F32) * scale).astype(BF16)
            dk_acc[win, :] += lax.dot_general(ds, qs, TN, preferred_element_type=F32)
            dv_acc[win, :] += lax.dot_general(p.astype(BF16), dos, TN, preferred_element_type=F32)

        @pl.when(i == n_steps - 1)
        def _():
            dk_ref[...] = dk_acc[...].astype(BF16)
            dv_ref[...] = dv_acc[...].astype(BF16)

    whole = lambda col: pl.BlockSpec((L, LANES), lambda hh, i: (0, col(hh)))
    blk = pl.BlockSpec((tq, LANES), lambda hh, i: (i, hh))
    ident = lambda hh: hh
    return pl.pallas_call(
        body, name=name, grid=(n_heads, n_steps),
        in_specs=[pl.BlockSpec(memory_space=pltpu.SMEM), pl.BlockSpec((tq, LANES), lambda hh, i: (i, qcol(hh))), whole(kcol), whole(vcol),
                  blk, blk, blk],
        out_specs=[blk, whole(ident), whole(ident)], out_shape=[jax.ShapeDtypeStruct(d_shape, BF16)] * 3,
        scratch_shapes=[pltpu.VMEM((L, LANES), F32)] * 2,
        compiler_params=_params("parallel", "arbitrary"))(slopes, q, k, v, do, lse, delta)


def _pad_heads(w, n_heads, width, axis):
    shp = w.shape
    new = shp[:axis] + (n_heads, width) + shp[axis + 1:]
    pad = [(0, 0)] * len(new)
    pad[axis + 1] = (0, LANES - width)
    out = jnp.pad(w.reshape(new), pad)
    return out.reshape(shp[:axis] + (n_heads * LANES,) + shp[axis + 1:])


def _unpad_heads(w, n_heads, width, axis):
    shp = w.shape
    new = shp[:axis] + (n_heads, LANES) + shp[axis + 1:]
    out = lax.slice_in_dim(w.reshape(new), 0, width, axis=axis + 1)
    return out.reshape(shp[:axis] + (n_heads * width,) + shp[axis + 1:])


def _alibi_slopes():
    s = jnp.exp2(-8.0 * jnp.arange(1, DIL_HEADS + 1, dtype=F32) / DIL_HEADS)
    return s.reshape(DIL_HPG, DIL_GROUPS).T


def _local_step(x, mem, positions, tgt, W):
    S = x.shape[0]
    pos = positions.reshape(S, 1).astype(F32)
    half = MLA_ROPE // 2
    inv_freq = ROPE_THETA ** (-jnp.arange(half, dtype=F32) / half)
    invf = jnp.zeros((1, LANES), F32).at[0, MLA_NOPE:MLA_NOPE + half].set(inv_freq).at[0, MLA_NOPE + half:MLA_QK].set(inv_freq)
    slopes = _alibi_slopes()

    w_in = W["w_in"]
    zc = lambda n: jnp.zeros((D_MODEL, n), BF16)
    w_a = jnp.concatenate([w_in[:, :OFF_Q], zc(MLA_NOPE), w_in[:, OFF_KV:OFF_KR], zc(LANES - MLA_QK), w_in[:, OFF_Q:OFF_KV]], axis=1)
    w_d, w_m, w_g = w_in[:, OFF_KR:OFF_DIL], w_in[:, OFF_DIL:OFF_MEMQ], w_in[:, OFF_MEMQ:]
    w_uq_p = _pad_heads(W["w_uq"], MLA_HEADS, MLA_QK, 1)
    ukv = W["w_ukv"].reshape(KV_RANK, MLA_HEADS, 2 * MLA_NOPE)
    w_uk_p = _pad_heads(ukv[:, :, :MLA_NOPE].reshape(KV_RANK, -1), MLA_HEADS, MLA_NOPE, 1)
    w_uv_p = _pad_heads(ukv[:, :, MLA_NOPE:].reshape(KV_RANK, -1), MLA_HEADS, MLA_NOPE, 1)
    w_br_mla_p = _pad_heads(W["w_br_mla"], MLA_HEADS, MLA_NOPE, 0)

    h = _rmsnorm(x, W["g_pre_mix"], BF16, "norm1")
    proj_a = _matmul(h, w_a, tn=768, name="proj_a")
    att_scale = LANES ** -0.5
    proj_d = _matmul(h, w_d, tn=DIL_GROUPS * DIL_W, out_dtype=BF16, scale=(1, att_scale), name="proj_d")
    proj_m = _matmul(h, w_m, tn=MEM_W, out_dtype=BF16, scale=(1, att_scale), name="proj_m")
    proj_g = _matmul(h, w_g, tn=1536, name="proj_g")

    cq_n, ckv_n, kpe = _mla_prep(proj_a, pos, invf, W["mla_q_norm"], W["mla_kv_norm"])
    q_pre = _matmul(cq_n, w_uq_p, tn=1024, name="mla_q")
    k_pre = _matmul(ckv_n, w_uk_p, tn=1024, name="mla_k")
    v_mla = _matmul(ckv_n, w_uv_p, tn=1024, out_dtype=BF16, name="mla_v")
    q_mla, k_mla = _qk_final(q_pre, k_pre, kpe, pos, invf)
    ident = lambda hh: hh
    o_mla, lse_mla = _causal_fwd(q_mla, k_mla, v_mla, name="attn_mla_fwd")

    n_dil_cols = 3 * DIL_HEADS
    o_dil, lse_dil = [], []
    for g, (window, dil) in enumerate(DIL_PAIRS):
        L = S // dil
        view = proj_d.reshape(L, dil * 3 * DIL_HEADS * LANES)
        col = lambda part, g=g: (lambda hh: (hh // DIL_HPG) * n_dil_cols + (part * DIL_GROUPS + g) * DIL_HPG + hh % DIL_HPG)
        o, lse = _band_fwd(view, view, view, slopes[g], n_heads=dil * DIL_HPG, qcol=col(0), kcol=col(1), vcol=col(2), L=L,
                           slope_mul=float(dil), o_shape=(L, dil * DIL_W), name=f"attn_dil{g}_fwd")
        o_dil.append(o.reshape(S, DIL_W))
        lse_dil.append(lse.reshape(S, DIL_W))
    y_dil = _dil_mix(o_dil, lse_dil)

    mem_n = _rmsnorm(mem, W["g_mem"], BF16, "mem_norm")
    kv_mem = _matmul(mem_n, W["w_mem_kv"], name="mem_kv")
    o_mem, lse_mem = _mem_fwd(proj_m, kv_mem, name="attn_mem_fwd")

    b_mla = _matmul(o_mla, w_br_mla_p, name="br_mla")
    b_dil = _matmul(y_dil, W["w_br_dil"], name="br_dil")
    b_mem = _matmul(o_mem, W["w_br_mem"], name="br_mem")
    merged = _merge(proj_g, W["b_gate"], [b_mla, b_dil, b_mem])
    o_proj = _matmul(merged, W["w_o"], name="o_proj")
    x1, h2 = _norm2(o_proj, x, W["g_post_mix"], W["g_pre_ffn"])

    u = _matmul(h2, W["w_ffn_up"], tn=1408, name="ffn_up")
    act = _conv_fwd(u, W["conv_w"], W["conv_b"])
    f = _matmul(act, W["w_ffn_down"], name="ffn_down")
    loss8, dx2, df, dg_post_ffn = _loss_head(f, x1, tgt, W["g_post_ffn"])
    loss = loss8[0, 0]

    G = {"g_post_ffn": dg_post_ffn}
    d_act = _matmul(df, W["w_ffn_down"], mode="nt", tn=1408, name="d_act")
    G["w_ffn_down"] = _matmul(act, df, mode="tn", tm=1408, tk=1024, name="dw_ffn_down")
    dzg, dzv, G["conv_w"], G["conv_b"] = _conv_bwd_dz(d_act, u, W["conv_w"], W["conv_b"])
    du = _conv_bwd_du(dzg, dzv, W["conv_w"])
    dh2 = _matmul(du, W["w_ffn_up"], mode="nt", tk=2816, name="d_h2")
    G["w_ffn_up"] = _matmul(h2, du, mode="tn", tn=1408, tk=1024, name="dw_ffn_up")
    dx1, do_proj, G["g_pre_ffn"], G["g_post_mix"] = _norm2_bwd(dx2, dh2, x1, o_proj, W["g_pre_ffn"], W["g_post_mix"])
    dmerged = _matmul(do_proj, W["w_o"], mode="nt", name="d_merged")
    G["w_o"] = _matmul(merged, do_proj, mode="tn", tk=1024, name="dw_o")
    db_mla, db_dil, db_mem, dproj_g, G["b_gate"] = _merge_bwd(dmerged, proj_g, W["b_gate"], [b_mla, b_dil, b_mem])

    dy_mem = _matmul(db_mem, W["w_br_mem"], mode="nt", name="d_y_mem")
    G["w_br_mem"] = _matmul(o_mem, db_mem, mode="tn", tk=1024, name="dw_br_mem")
    delta_mem = _delta(dy_mem, o_mem, MEM_HEADS, "delta_mem")
    dq_mem, dkv_mem = _mem_bwd(proj_m, kv_mem, dy_mem, lse_mem, delta_mem, scale=att_scale, name="attn_mem_bwd")
    G["w_mem_kv"] = _matmul(mem_n, dkv_mem, mode="tn", name="dw_mem_kv")
    dmem_n = _matmul(dkv_mem, W["w_mem_kv"], mode="nt", name="d_mem_n")
    G["g_mem"] = _gain_grad(dmem_n, mem, "dg_mem")

    dy_dil = _matmul(db_dil, W["w_br_dil"], mode="nt", name="d_y_dil")
    G["w_br_dil"] = _matmul(y_dil, db_dil, mode="tn", tk=1024, name="dw_br_dil")
    mix = _dil_mix_bwd(dy_dil, o_dil, lse_dil)
    do_dil, dl_dil = mix[:3], mix[3:]
    d_parts = [[None] * DIL_GROUPS for _ in range(3)]
    for g, (window, dil) in enumerate(DIL_PAIRS):
        L = S // dil
        view = proj_d.reshape(L, dil * 3 * DIL_HEADS * LANES)
        col = lambda part, g=g: (lambda hh: (hh // DIL_HPG) * n_dil_cols + (part * DIL_GROUPS + g) * DIL_HPG + hh % DIL_HPG)
        shp = (L, dil * DIL_W)
        dq, dk, dv = _band_bwd(
            view, view, view, do_dil[g].reshape(shp), lse_dil[g].reshape(shp), dl_dil[g].reshape(shp), slopes[g],
            n_heads=dil * DIL_HPG, qcol=col(0), kcol=col(1), vcol=col(2), L=L, scale=att_scale, slope_mul=float(dil),
            d_shape=shp, name=f"attn_dil{g}_bwd")
        for part, d in enumerate((dq, dk, dv)):
            d_parts[part][g] = d.reshape(S, DIL_W)
    dproj_d = jnp.concatenate([d for part in d_parts for d in part], axis=1)

    dy_mla = _matmul(db_mla, w_br_mla_p, mode="nt", name="d_y_mla")
    dw_br_mla_p = _matmul(o_mla, db_mla, mode="tn", tk=1024, name="dw_br_mla")
    G["w_br_mla"] = _unpad_heads(dw_br_mla_p, MLA_HEADS, MLA_NOPE, 0)
    delta_mla = _delta(dy_mla, o_mla, MLA_HEADS, "delta_mla")
    dq_mla, dk_mla, dv_mla = _causal_bwd(q_mla, k_mla, v_mla, dy_mla, lse_mla, delta_mla, name="attn_mla_bwd")
    dq_pre, dkpe = _mla_bwd_prep(dq_mla, dk_mla, pos, invf)
    dcq_n = _matmul(dq_pre, w_uq_p, mode="nt", tn=Q_RANK, name="d_cq")
    G["w_uq"] = _unpad_heads(_matmul(cq_n, dq_pre, mode="tn", tm=Q_RANK, tk=1024, name="dw_uq"), MLA_HEADS, MLA_QK, 1)
    dckv_a = _matmul(dk_mla, w_uk_p, mode="nt", tn=KV_RANK, name="d_ckv_k")
    dckv_b = _matmul(dv_mla, w_uv_p, mode="nt", tn=KV_RANK, name="d_ckv_v")
    dw_uk = _unpad_heads(_matmul(ckv_n, dk_mla, mode="tn", tm=KV_RANK, tk=1024, name="dw_uk"), MLA_HEADS, MLA_NOPE, 1)
    dw_uv = _unpad_heads(_matmul(ckv_n, dv_mla, mode="tn", tm=KV_RANK, tk=1024, name="dw_uv"), MLA_HEADS, MLA_NOPE, 1)
    G["w_ukv"] = jnp.concatenate([dw_uk.reshape(KV_RANK, MLA_HEADS, MLA_NOPE), dw_uv.reshape(KV_RANK, MLA_HEADS, MLA_NOPE)],
                                 axis=2).reshape(KV_RANK, -1)
    dproj_a, G["mla_q_norm"], G["mla_kv_norm"] = _mla_norm_bwd(dcq_n, dckv_a, dckv_b, dkpe, proj_a, W["mla_q_norm"],
                                                              W["mla_kv_norm"])

    dh = _matmul(dproj_a, w_a, mode="nt", name="d_h_a")
    dh = _matmul(dproj_d, w_d, mode="nt", tk=2304, add=dh, name="d_h_d")
    dh = _matmul(dq_mem, w_m, mode="nt", add=dh, name="d_h_m")
    dh = _matmul(dproj_g, w_g, mode="nt", add=dh, name="d_h_g")
    dw_a = _matmul(h, dproj_a, mode="tn", tn=768, tk=1024, name="dw_in_a")
    dw_d = _matmul(h, dproj_d, mode="tn", tn=1536, tk=1024, name="dw_in_d")
    dw_m = _matmul(h, dq_mem, mode="tn", tn=512, tk=1024, name="dw_in_m")
    dw_g = _matmul(h, dproj_g, mode="tn", tn=1536, tk=1024, name="dw_in_g")
    kr0 = Q_RANK + MLA_NOPE
    G["w_in"] = jnp.concatenate([dw_a[:, :Q_RANK], dw_a[:, Q_RANK + LANES:], dw_a[:, kr0:kr0 + MLA_ROPE], dw_d, dw_m, dw_g], axis=1)
    grad_x, G["g_pre_mix"] = _norm1_bwd(dx1, dh, x, W["g_pre_mix"])
    return loss, grad_x, G


WEIGHTS = ["g_pre_mix", "w_in", "b_gate", "mla_q_norm", "w_uq", "mla_kv_norm", "w_ukv", "g_mem", "w_mem_kv", "w_br_mla",
           "w_br_dil", "w_br_mem", "w_o", "g_post_mix", "g_pre_ffn", "w_ffn_up", "conv_w", "conv_b", "w_ffn_down", "g_post_ffn"]
GROUPS = [
    [("w_in", (D_MODEL, D_IN), 1)],
    [("w_uq", (Q_RANK, MLA_HEADS * MLA_QK), 1)],
    [("w_ukv", (KV_RANK, MLA_HEADS * 2 * MLA_NOPE), 1), ("w_br_mla", (MLA_HEADS * MLA_NOPE, D_MODEL), 1),
     ("w_br_dil", (DIL_W, D_MODEL), 1), ("w_br_mem", (MEM_W, D_MODEL), 1)],
    [("w_mem_kv", (D_MODEL, 2 * MEM_W), 0), ("w_o", (D_MODEL, D_MODEL), 0), ("w_ffn_down", (D_FF, D_MODEL), 0)],
    [("w_ffn_up", (D_MODEL, 2 * D_FF), 1)],
]
CONV_W = ("conv_w", (3, 2 * D_FF), 1)
REPLICATED = [("g_pre_mix", D_MODEL), ("b_gate", 3 * D_MODEL), ("mla_q_norm", Q_RANK), ("mla_kv_norm", KV_RANK), ("g_mem", D_MODEL),
              ("g_post_mix", D_MODEL), ("g_pre_ffn", D_MODEL), ("conv_b", 2 * D_FF), ("g_post_ffn", D_MODEL)]
SMALL_ROWS = 256
CONV_AT = sum(n for _, n in REPLICATED)
LOSS_AT = CONV_AT + 3 * 2 * D_FF


def _shard_shape(shape, axis):
    return tuple(d // N_CHIPS if a == axis else d for a, d in enumerate(shape))


def _group_shape(grp):
    shapes = [_shard_shape(shape, axis) for _, shape, axis in grp]
    assert len({s[1] for s in shapes}) == 1
    return sum(s[0] for s in shapes), shapes[0][1]


def _member_shards(a, axis):
    r, c = a.shape
    if axis == 0:
        return a.reshape(N_CHIPS, r // N_CHIPS, c)
    return a.reshape(r, N_CHIPS, c // N_CHIPS).transpose(1, 0, 2)


def _member_full(s, axis):
    n, r, c = s.shape
    if axis == 0:
        return s.reshape(n * r, c)
    return s.transpose(1, 0, 2).reshape(r, n * c)


def _my_weight_groups(w):
    out = []
    for grp in GROUPS:
        rows, width = _group_shape(grp)
        out.append(jnp.concatenate([w[name].astype(BF16) for name, _, _ in grp], axis=0).reshape(2, rows // 2, width))
    return out


def _full_weights(gathered, conv_all):
    out = {}
    for grp, ga in zip(GROUPS, gathered):
        ga = ga.reshape(N_CHIPS, -1, ga.shape[-1])
        off = 0
        for name, shape, axis in grp:
            rows = _shard_shape(shape, axis)[0]
            out[name] = _member_full(ga[:, off:off + rows], axis)
            off += rows
    out[CONV_W[0]] = _member_full(conv_all, CONV_W[2])
    return out


def _grad_groups(G):
    out = []
    for grp in GROUPS:
        rows, width = _group_shape(grp)
        a = jnp.concatenate([_member_shards(G[name], axis) for name, _, axis in grp], axis=1)
        out.append(a.reshape(N_CHIPS, 2, rows // 2, width).transpose(1, 0, 2, 3))
    return out


def _pack_small(vals, conv_g=None, loss=None):
    parts = [vals[name].reshape(-1) for name, _ in REPLICATED]
    if conv_g is not None:
        parts += [conv_g.reshape(-1), loss.reshape(1)]
    flat = jnp.concatenate(parts)
    return jnp.pad(flat, (0, SMALL_ROWS * LANES - flat.shape[0])).reshape(SMALL_ROWS, LANES)


def _unpack_small(packed):
    flat = packed.reshape(-1)
    out, off = {}, 0
    for name, n in REPLICATED:
        out[name] = flat[off:off + n].reshape(1, n)
        off += n
    return out


MESH = pl.DeviceIdType.MESH
HBM_SPEC = pl.BlockSpec(memory_space=pltpu.HBM)


def _place():
    x, y, c = lax.axis_index("x"), lax.axis_index("y"), lax.axis_index("c")
    chips = [(1 - x, y), (x, 1 - y), (1 - x, 1 - y)]
    return x, y, c, chips


def _remote(src, dst, send_sems, recv_sems, k, to):
    return pltpu.make_async_remote_copy(src_ref=src, dst_ref=dst, send_sem=send_sems.at[k], recv_sem=recv_sems.at[k],
                                        device_id=to, device_id_type=MESH)


def _gather_weights(groups, conv_w):
    n = len(groups)

    def body(*refs):
        srcs, conv_src, outs, conv_out = refs[:n], refs[n], refs[n + 1:2 * n + 1], refs[2 * n + 1]
        send_sems, recv_sems, local_sems = refs[2 * n + 2:]
        x, y, c, chips = _place()
        me = 2 * x + y
        sibling = (x, y, 1 - c)
        local = [pltpu.make_async_copy(srcs[g], outs[g].at[me], local_sems.at[g]) for g in range(n)]
        local.append(pltpu.make_async_copy(conv_src, conv_out.at[me], local_sems.at[n]))
        for cp in local:
            cp.start()
        first = [_remote(srcs[g].at[c], outs[g].at[me, c], send_sems, recv_sems, g * 3 + k, (px, py, c))
                 for k, (px, py) in enumerate(chips) for g in range(n)]
        first += [_remote(conv_src, conv_out.at[me], send_sems, recv_sems, 6 * n + k, (px, py, c)) for k, (px, py) in enumerate(chips)]
        for cp in first:
            cp.start()
        passed = []
        for k, (px, py) in enumerate(chips):
            for g in range(n):
                slot = outs[g].at[2 * px + py, c]
                _remote(slot, slot, send_sems, recv_sems, g * 3 + k, (px, py, c)).wait_recv()
                cp = _remote(slot, slot, send_sems, recv_sems, 3 * n + g * 3 + k, sibling)
                cp.start()
                passed.append(cp)
        for k, (px, py) in enumerate(chips):
            slot = conv_out.at[2 * px + py]
            _remote(slot, slot, send_sems, recv_sems, 6 * n + k, (px, py, c)).wait_recv()
            for g in range(n):
                slot = outs[g].at[2 * px + py, 1 - c]
                _remote(slot, slot, send_sems, recv_sems, 3 * n + g * 3 + k, sibling).wait_recv()
        for cp in first + passed:
            cp.wait_send()
        for cp in local:
            cp.wait()

    n_sem = 6 * n + 3
    outs = pl.pallas_call(
        body, name="comm_gather_weights", in_specs=[HBM_SPEC] * (n + 1), out_specs=[HBM_SPEC] * (n + 1),
        out_shape=[jax.ShapeDtypeStruct((N_CHIPS,) + g.shape, g.dtype) for g in groups]
        + [jax.ShapeDtypeStruct((N_CHIPS,) + conv_w.shape, conv_w.dtype)],
        scratch_shapes=[pltpu.SemaphoreType.DMA((n_sem,)), pltpu.SemaphoreType.DMA((n_sem,)), pltpu.SemaphoreType.DMA((n + 1,))],
    )(*groups, conv_w)
    return outs[:n], outs[n]


def _swap_halves(groups):
    n = len(groups)

    def body(*refs):
        srcs, outs, send_sems, recv_sems = refs[:n], refs[n:2 * n], refs[2 * n], refs[2 * n + 1]
        x, y, c, _ = _place()
        cps = [_remote(srcs[g].at[1 - c], outs[g], send_sems, recv_sems, g, (x, y, 1 - c)) for g in range(n)]
        for cp in cps:
            cp.start()
        for cp in cps:
            cp.wait()

    return pl.pallas_call(
        body, name="comm_swap_halves", in_specs=[HBM_SPEC] * n, out_specs=[HBM_SPEC] * n,
        out_shape=[jax.ShapeDtypeStruct(g.shape[1:], g.dtype) for g in groups],
        scratch_shapes=[pltpu.SemaphoreType.DMA((n,)), pltpu.SemaphoreType.DMA((n,))],
    )(*groups)


def _scatter_partials(groups):
    n = len(groups)

    def body(*refs):
        srcs, outs = refs[:n], refs[n:2 * n]
        send_sems, recv_sems, local_sems = refs[2 * n:]
        x, y, c, chips = _place()
        me = 2 * x + y
        local = [pltpu.make_async_copy(srcs[g].at[me], outs[g].at[me], local_sems.at[g]) for g in range(n)]
        for cp in local:
            cp.start()
        sends = [_remote(srcs[g].at[2 * px + py], outs[g].at[me], send_sems, recv_sems, g * 3 + k, (px, py, c))
                 for k, (px, py) in enumerate(chips) for g in range(n)]
        for cp in sends:
            cp.start()
        for k, (px, py) in enumerate(chips):
            for g in range(n):
                slot = outs[g].at[2 * px + py]
                _remote(slot, slot, send_sems, recv_sems, g * 3 + k, (px, py, c)).wait_recv()
        for cp in sends:
            cp.wait_send()
        for cp in local:
            cp.wait()

    return pl.pallas_call(
        body, name="comm_scatter_partials", in_specs=[HBM_SPEC] * n, out_specs=[HBM_SPEC] * n,
        out_shape=[jax.ShapeDtypeStruct(g.shape, g.dtype) for g in groups],
        scratch_shapes=[pltpu.SemaphoreType.DMA((3 * n,)), pltpu.SemaphoreType.DMA((3 * n,)), pltpu.SemaphoreType.DMA((n,))],
    )(*groups)


def _share_reduced(groups):
    n = len(groups)

    def body(*refs):
        srcs, outs = refs[:n], refs[n:2 * n]
        send_sems, recv_sems, local_sems = refs[2 * n:]
        x, y, c, _ = _place()
        local = [pltpu.make_async_copy(srcs[g], outs[g].at[c], local_sems.at[g]) for g in range(n)]
        for cp in local:
            cp.start()
        sends = [_remote(srcs[g], outs[g].at[c], send_sems, recv_sems, g, (x, y, 1 - c)) for g in range(n)]
        for cp in sends:
            cp.start()
        for g in range(n):
            _remote(srcs[g], outs[g].at[1 - c], send_sems, recv_sems, g, (x, y, 1 - c)).wait_recv()
        for cp in sends:
            cp.wait_send()
        for cp in local:
            cp.wait()

    return pl.pallas_call(
        body, name="comm_share_reduced", in_specs=[HBM_SPEC] * n, out_specs=[HBM_SPEC] * n,
        out_shape=[jax.ShapeDtypeStruct((2,) + g.shape, g.dtype) for g in groups],
        scratch_shapes=[pltpu.SemaphoreType.DMA((n,)), pltpu.SemaphoreType.DMA((n,)), pltpu.SemaphoreType.DMA((n,))],
    )(*groups)


def _allreduce_small(v):
    n_dev = 2 * N_CHIPS

    def body(src, out, buf, send_sems, recv_sems):
        x, y, c, _ = _place()
        me = 4 * x + 2 * y + c
        buf[me] = src[...]
        flips = [(k >> 2 & 1, k >> 1 & 1, k & 1) for k in range(1, n_dev)]
        sends = []
        for k, (fx, fy, fc) in enumerate(flips):
            to = ((1 - x) if fx else x, (1 - y) if fy else y, (1 - c) if fc else c)
            cp = _remote(src, buf.at[me], send_sems, recv_sems, k, to)
            cp.start()
            sends.append((cp, to))
        for k, (cp, to) in enumerate(sends):
            slot = buf.at[4 * to[0] + 2 * to[1] + to[2]]
            _remote(slot, slot, send_sems, recv_sems, k, to).wait_recv()
        for cp, _ in sends:
            cp.wait_send()
        acc = buf[0]
        for d in range(1, n_dev):
            acc = acc + buf[d]
        out[...] = acc

    vm = pl.BlockSpec(memory_space=pltpu.VMEM)
    return pl.pallas_call(
        body, name="comm_allreduce_small", in_specs=[vm], out_specs=vm, out_shape=jax.ShapeDtypeStruct(v.shape, F32),
        scratch_shapes=[pltpu.VMEM((n_dev,) + v.shape, F32), pltpu.SemaphoreType.DMA((n_dev - 1,)),
                        pltpu.SemaphoreType.DMA((n_dev - 1,))],
    )(v)


def _row_tile(rows, cap=256):
    return max(t for t in range(8, cap + 1, 8) if rows % t == 0)


def _add_sibling(mine, theirs, core, name):
    _, n, R, C = mine.shape
    t = _row_tile(R)

    def body(core_ref, a_ref, b_ref, o_ref):
        o_ref[...] = a_ref[...] + b_ref[...]

    sp = pl.BlockSpec((None, t, C), lambda k, i, core_ref: (k, i, 0))
    grid_spec = pltpu.PrefetchScalarGridSpec(
        num_scalar_prefetch=1, grid=(n, R // t),
        in_specs=[pl.BlockSpec((None, None, t, C), lambda k, i, core_ref: (core_ref[0], k, i, 0)), sp], out_specs=sp)
    return pl.pallas_call(body, name=name, grid_spec=grid_spec, out_shape=jax.ShapeDtypeStruct((n, R, C), F32),
                          compiler_params=_params("parallel", "parallel"))(core, mine, theirs)


def _add_chips(a, name):
    n, R, C = a.shape
    t = _row_tile(R)

    def body(a_ref, o_ref):
        acc = a_ref[0]
        for k in range(1, n):
            acc = acc + a_ref[k]
        o_ref[...] = acc

    return pl.pallas_call(body, name=name, grid=(R // t,), in_specs=[pl.BlockSpec((n, t, C), lambda i: (0, i, 0))],
                          out_specs=pl.BlockSpec((t, C), lambda i: (i, 0)),
                          out_shape=jax.ShapeDtypeStruct((R, C), F32), compiler_params=_params("parallel"))(a)


def _adamw(w, g, m, v, name, g_row0=0):
    R, C = w.shape
    t = math.gcd(math.gcd(R, g_row0), 128) if R % 8 == 0 else R
    assert g_row0 % t == 0
    c1 = 1.0 - ADAM_B1 ** ADAM_STEP
    c2 = 1.0 - ADAM_B2 ** ADAM_STEP

    def body(w_ref, g_ref, m_ref, v_ref, go_ref, d_ref, nm_ref, nv_ref):
        gv = g_ref[...]
        nm = ADAM_B1 * m_ref[...] + (1.0 - ADAM_B1) * gv
        nv = ADAM_B2 * v_ref[...] + (1.0 - ADAM_B2) * (gv * gv)
        go_ref[...] = gv
        d_ref[...] = -ADAM_LR * ((nm / c1) / (jnp.sqrt(nv / c2) + ADAM_EPS) + ADAM_WD * w_ref[...])
        nm_ref[...] = nm
        nv_ref[...] = nv

    sp = pl.BlockSpec((t, C), lambda i: (i, 0))
    g_sp = pl.BlockSpec((t, C), lambda i: (i + g_row0 // t, 0))
    return pl.pallas_call(body, name=name, grid=(R // t,), in_specs=[sp, g_sp, sp, sp], out_specs=[sp] * 4,
                          out_shape=[jax.ShapeDtypeStruct((R, C), F32)] * 4, compiler_params=_params("parallel"))(w, g, m, v)


def kernel(x, mem, positions, g_pre_mix, w_in, b_gate, mla_q_norm, w_uq, mla_kv_norm, w_ukv, g_mem, w_mem_kv, w_br_mla, w_br_dil, w_br_mem, w_o, g_post_mix, g_pre_ffn, w_ffn_up, conv_w, conv_b, w_ffn_down, g_post_ffn, loss_target, m_g_pre_mix, m_w_in, m_b_gate, m_mla_q_norm, m_w_uq, m_mla_kv_norm, m_w_ukv, m_g_mem, m_w_mem_kv, m_w_br_mla, m_w_br_dil, m_w_br_mem, m_w_o, m_g_post_mix, m_g_pre_ffn, m_w_ffn_up, m_conv_w, m_conv_b, m_w_ffn_down, m_g_post_ffn, v_g_pre_mix, v_w_in, v_b_gate, v_mla_q_norm, v_w_uq, v_mla_kv_norm, v_w_ukv, v_g_mem, v_w_mem_kv, v_w_br_mla, v_w_br_dil, v_w_br_mem, v_w_o, v_g_post_mix, v_g_pre_ffn, v_w_ffn_up, v_conv_w, v_conv_b, v_w_ffn_down, v_g_post_ffn):
    w_args = (g_pre_mix, w_in, b_gate, mla_q_norm, w_uq, mla_kv_norm, w_ukv, g_mem, w_mem_kv, w_br_mla, w_br_dil, w_br_mem, w_o,
              g_post_mix, g_pre_ffn, w_ffn_up, conv_w, conv_b, w_ffn_down, g_post_ffn)
    m_args = (m_g_pre_mix, m_w_in, m_b_gate, m_mla_q_norm, m_w_uq, m_mla_kv_norm, m_w_ukv, m_g_mem, m_w_mem_kv, m_w_br_mla,
              m_w_br_dil, m_w_br_mem, m_w_o, m_g_post_mix, m_g_pre_ffn, m_w_ffn_up, m_conv_w, m_conv_b, m_w_ffn_down, m_g_post_ffn)
    v_args = (v_g_pre_mix, v_w_in, v_b_gate, v_mla_q_norm, v_w_uq, v_mla_kv_norm, v_w_ukv, v_g_mem, v_w_mem_kv, v_w_br_mla,
              v_w_br_dil, v_w_br_mem, v_w_o, v_g_post_mix, v_g_pre_ffn, v_w_ffn_up, v_conv_w, v_conv_b, v_w_ffn_down, v_g_post_ffn)
    sharded = {name for grp in GROUPS for name, _, _ in grp} | {CONV_W[0]}

    def local(a, name):
        return a[0] if name in sharded else a

    w = {n: local(a, n) for n, a in zip(WEIGHTS, w_args)}
    m = {n: local(a, n) for n, a in zip(WEIGHTS, m_args)}
    v = {n: local(a, n) for n, a in zip(WEIGHTS, v_args)}

    full = _full_weights(*_gather_weights(_my_weight_groups(w), w[CONV_W[0]]))
    full.update({name: w[name] for name, _ in REPLICATED})

    loss_local, grad_x, G = _local_step(x[0], mem[0], positions, loss_target[0], full)

    core = lax.axis_index("c").astype(jnp.int32).reshape(1)
    mine = _grad_groups(G)
    theirs = _swap_halves(mine)
    partial = [_add_sibling(a, b, core, f"add_sibling_{i}") for i, (a, b) in enumerate(zip(mine, theirs))]
    reduced = [_add_chips(a, f"add_chips_{i}") for i, a in enumerate(_scatter_partials(partial))]
    shard_groups = [a.reshape(-1, a.shape[-1]) for a in _share_reduced(reduced)]
    small = _allreduce_small(_pack_small(G, G[CONV_W[0]], loss_local))
    flat = small.reshape(-1)
    loss = flat[LOSS_AT]
    conv_g = flat[CONV_AT:LOSS_AT].reshape(CONV_W[1])
    conv_cols = CONV_W[1][1] // N_CHIPS
    conv_g = lax.dynamic_slice_in_dim(conv_g, (2 * lax.axis_index("x") + lax.axis_index("y")) * conv_cols, conv_cols, axis=1)

    grads, deltas, new_m, new_v = {}, {}, {}, {}
    for grp, g_all in zip(GROUPS, shard_groups):
        off = 0
        for name, _, _ in grp:
            grads[name], deltas[name], new_m[name], new_v[name] = _adamw(w[name], g_all, m[name], v[name], "adamw_" + name, off)
            off += w[name].shape[0]
    name = CONV_W[0]
    grads[name], deltas[name], new_m[name], new_v[name] = _adamw(w[name], conv_g, m[name], v[name], "adamw_" + name)
    packed = _adamw(_pack_small(w), small, _pack_small(m), _pack_small(v), "adamw_small")
    for dst, packed_small in zip((grads, deltas, new_m, new_v), packed):
        dst.update(_unpack_small(packed_small))

    def out(d, name):
        return d[name][None] if name in sharded else d[name]

    return (loss, grad_x[None], *[out(grads, n) for n in WEIGHTS], *[out(deltas, n) for n in WEIGHTS],
            *[out(new_m, n) for n in WEIGHTS], *[out(new_v, n) for n in WEIGHTS])
```

```python
import functools
import math

import jax
import jax.numpy as jnp
from jax import lax
from jax.experimental import pallas as pl
from jax.experimental.pallas import tpu as pltpu

F32 = jnp.float32
BF16 = jnp.bfloat16

D_MODEL = 1024
N_MEM = 256
RMS_EPS = 1e-6
NEG_INF = -1e30
MLA_HEADS = 8
MLA_NOPE = 64
MLA_ROPE = 32
MLA_QK = 96
Q_RANK = 384
KV_RANK = 256
ROPE_THETA = 10000.0
DIL_PAIRS = ((128, 1), (512, 4), (2048, 16))
DIL_GROUPS = 3
DIL_HPG = 4
DIL_HEADS = 12
DIL_W = 512
MEM_HEADS = 4
MEM_W = 512
D_FF = 2816
OFF_Q = 384
OFF_KV = 640
OFF_KR = 672
OFF_DIL = 5280
OFF_MEMQ = 5792
D_IN = 8864
ADAM_LR = 0.001
ADAM_B1 = 0.9
ADAM_B2 = 0.999
ADAM_EPS = 1e-08
ADAM_WD = 0.01
ADAM_STEP = 10

LANES = 128
VMEM_LIMIT = 56 * 1024 * 1024

N_CHIPS = 4
ROW_TILE = 256

NN = (((1,), (0,)), ((), ()))
NT = (((1,), (1,)), ((), ()))
TN = (((0,), (0,)), ((), ()))


def _params(*sem):
    return pltpu.CompilerParams(dimension_semantics=sem, vmem_limit_bytes=VMEM_LIMIT)


def _full(shape):
    return pl.BlockSpec(shape, lambda *_: (0,) * len(shape))


def _matmul(a, b, *, mode="nn", out_dtype=F32, tm=1024, tn=1024, tk=None, add=None, scale=None, name):
    if mode == "nn":
        (M, K), N = a.shape, b.shape[1]
    elif mode == "nt":
        (M, K), N = a.shape, b.shape[0]
    else:
        (K, M), N = a.shape, b.shape[1]
    tm, tn = min(tm, M), min(tn, N)
    tk = K if tk is None else min(tk, K)
    assert M % tm == 0 and N % tn == 0 and K % tk == 0, (name, M, N, K, tm, tn, tk)
    nk = K // tk
    dims = {"nn": NN, "nt": NT, "tn": TN}[mode]
    a_spec = pl.BlockSpec((tk, tm), lambda i, j, k: (k, i)) if mode == "tn" else pl.BlockSpec((tm, tk), lambda i, j, k: (i, k))
    b_spec = pl.BlockSpec((tn, tk), lambda i, j, k: (j, k)) if mode == "nt" else pl.BlockSpec((tk, tn), lambda i, j, k: (k, j))
    o_spec = pl.BlockSpec((tm, tn), lambda i, j, k: (i, j))
    has_add = add is not None

    def body(*refs):
        a_ref, b_ref = refs[0], refs[1]
        c_ref = refs[2] if has_add else None
        o_ref = refs[3] if has_add else refs[2]
        part = lax.dot_general(a_ref[...].astype(BF16), b_ref[...].astype(BF16), dims, preferred_element_type=F32)
        if scale is not None:
            assert nk == 1 and not has_add
            part = part * jnp.where(pl.program_id(1) < scale[0], scale[1], 1.0)
        if nk == 1:
            if has_add:
                part = part + c_ref[...]
            o_ref[...] = part.astype(out_dtype)
        else:
            acc = refs[-1]
            k = pl.program_id(2)

            @pl.when(k == 0)
            def _():
                acc[...] = part

            @pl.when(k > 0)
            def _():
                acc[...] += part

            @pl.when(k == nk - 1)
            def _():
                r = acc[...]
                if has_add:
                    r = r + c_ref[...]
                o_ref[...] = r.astype(out_dtype)

    in_specs = [a_spec, b_spec] + ([o_spec] if has_add else [])
    args = (a, b) + ((add,) if has_add else ())
    return pl.pallas_call(
        body, name=name, grid=(M // tm, N // tn, nk), in_specs=in_specs, out_specs=o_spec,
        out_shape=jax.ShapeDtypeStruct((M, N), out_dtype),
        scratch_shapes=[pltpu.VMEM((tm, tn), F32)] if nk > 1 else [],
        compiler_params=_params("parallel", "parallel", "arbitrary"),
    )(*args)


def _rms_fwd_val(x, g):
    r = lax.rsqrt(jnp.mean(x * x, axis=-1, keepdims=True) + RMS_EPS)
    return (x * r) * g


def _rms_bwd_val(dy, x, g):
    r = lax.rsqrt(jnp.mean(x * x, axis=-1, keepdims=True) + RMS_EPS)
    xn = x * r
    gdy = g * dy
    dx = r * (gdy - xn * jnp.mean(gdy * xn, axis=-1, keepdims=True))
    return dx, dy * xn


def _rope_tables(pos, invf, inverse):
    ang = pos * invf
    cos, sin = jnp.cos(ang), jnp.sin(ang)
    lane = lax.broadcasted_iota(jnp.int32, ang.shape, 1)
    first = (lane >= MLA_NOPE) & (lane < MLA_NOPE + MLA_ROPE // 2)
    second = (lane >= MLA_NOPE + MLA_ROPE // 2) & (lane < MLA_QK)
    sgn = -1.0 if inverse else 1.0
    sa = jnp.where(first, -sgn * sin, 0.0)
    sb = jnp.where(second, sgn * sin, 0.0)
    return cos, sa, sb


def _rope_val(x, cos, sa, sb):
    half = MLA_ROPE // 2
    return x * cos + pltpu.roll(x, LANES - half, 1) * sa + pltpu.roll(x, half, 1) * sb


def _head_sum_bcast(v, n_heads):
    parts = []
    for h in range(n_heads):
        s = jnp.sum(v[:, h * LANES:(h + 1) * LANES], axis=1, keepdims=True)
        parts.append(jnp.broadcast_to(s, (v.shape[0], LANES)))
    return parts


def _row_spec(t, w):
    return pl.BlockSpec((t, w), lambda i: (i, 0))


def _acc_spec(w, rows=1):
    return pl.BlockSpec((rows, w), lambda i: (0, 0))


def _rmsnorm(x, g, out_dtype, name):
    S, W = x.shape
    t = min(ROW_TILE, S)

    def body(x_ref, g_ref, o_ref):
        o_ref[...] = _rms_fwd_val(x_ref[...], g_ref[...]).astype(out_dtype)

    return pl.pallas_call(body, name=name, grid=(S // t,), in_specs=[_row_spec(t, W), _acc_spec(W)],
                          out_specs=_row_spec(t, W), out_shape=jax.ShapeDtypeStruct((S, W), out_dtype),
                          compiler_params=_params("parallel"))(x, g)


def _mla_prep(proj_a, pos, invf, q_norm, kv_norm):
    S = proj_a.shape[0]
    t = ROW_TILE

    def body(a_ref, pos_ref, invf_ref, qn_ref, kvn_ref, cq_ref, ckv_ref, kpe_ref):
        a = a_ref[...]
        cq_ref[...] = _rms_fwd_val(a[:, 0:Q_RANK], qn_ref[...]).astype(BF16)
        ckv_ref[...] = _rms_fwd_val(a[:, Q_RANK + LANES:], kvn_ref[...]).astype(BF16)
        cos, sa, sb = _rope_tables(pos_ref[...], invf_ref[...], False)
        kpe_ref[...] = _rope_val(a[:, Q_RANK:Q_RANK + LANES], cos, sa, sb)

    return pl.pallas_call(
        body, name="mla_prep", grid=(S // t,),
        in_specs=[_row_spec(t, 768), _row_spec(t, 1), _acc_spec(LANES), _acc_spec(Q_RANK), _acc_spec(KV_RANK)],
        out_specs=[_row_spec(t, Q_RANK), _row_spec(t, KV_RANK), _row_spec(t, LANES)],
        out_shape=[jax.ShapeDtypeStruct((S, Q_RANK), BF16), jax.ShapeDtypeStruct((S, KV_RANK), BF16),
                   jax.ShapeDtypeStruct((S, LANES), F32)],
        compiler_params=_params("parallel"))(proj_a, pos, invf, q_norm, kv_norm)


def _qk_final(q_pre, k_pre, kpe, pos, invf):
    S, W = q_pre.shape
    t = ROW_TILE
    scale = MLA_QK ** -0.5

    def body(q_ref, k_ref, kpe_ref, pos_ref, invf_ref, qo_ref, ko_ref):
        cos, sa, sb = _rope_tables(pos_ref[...], invf_ref[...], False)
        kpe_v = kpe_ref[...]
        for h in range(MLA_HEADS):
            sl = slice(h * LANES, (h + 1) * LANES)
            qo_ref[:, sl] = (_rope_val(q_ref[:, sl], cos, sa, sb) * scale).astype(BF16)
            ko_ref[:, sl] = (k_ref[:, sl] + kpe_v).astype(BF16)

    return pl.pallas_call(
        body, name="qk_final", grid=(S // t,),
        in_specs=[_row_spec(t, W), _row_spec(t, W), _row_spec(t, LANES), _row_spec(t, 1), _acc_spec(LANES)],
        out_specs=[_row_spec(t, W), _row_spec(t, W)],
        out_shape=[jax.ShapeDtypeStruct((S, W), BF16)] * 2,
        compiler_params=_params("parallel"))(q_pre, k_pre, kpe, pos, invf)


def _mla_bwd_prep(dq, dk, pos, invf):
    S, W = dq.shape
    t = ROW_TILE
    scale = MLA_QK ** -0.5

    def body(dq_ref, dk_ref, pos_ref, invf_ref, dqp_ref, dkpe_ref):
        cos, sa, sb = _rope_tables(pos_ref[...], invf_ref[...], True)
        tot = jnp.zeros((t, LANES), F32)
        for h in range(MLA_HEADS):
            sl = slice(h * LANES, (h + 1) * LANES)
            dqp_ref[:, sl] = _rope_val(dq_ref[:, sl] * scale, cos, sa, sb).astype(BF16)
            tot = tot + dk_ref[:, sl]
        lane = lax.broadcasted_iota(jnp.int32, tot.shape, 1)
        tot = jnp.where((lane >= MLA_NOPE) & (lane < MLA_QK), tot, 0.0)
        dkpe_ref[...] = _rope_val(tot, cos, sa, sb)

    return pl.pallas_call(
        body, name="mla_bwd_prep", grid=(S // t,),
        in_specs=[_row_spec(t, W), _row_spec(t, W), _row_spec(t, 1), _acc_spec(LANES)],
        out_specs=[_row_spec(t, W), _row_spec(t, LANES)],
        out_shape=[jax.ShapeDtypeStruct((S, W), BF16), jax.ShapeDtypeStruct((S, LANES), F32)],
        compiler_params=_params("parallel"))(dq, dk, pos, invf)


def _mla_norm_bwd(dcq, dckv_a, dckv_b, dkpe, proj_a, q_norm, kv_norm):
    S = proj_a.shape[0]
    t = ROW_TILE

    def body(dcq_ref, da_ref, db_ref, dkpe_ref, a_ref, qn_ref, kvn_ref, o_ref, dqn_ref, dkvn_ref):
        i = pl.program_id(0)
        a = a_ref[...]
        dxq, gq = _rms_bwd_val(dcq_ref[...], a[:, 0:Q_RANK], qn_ref[...])
        dxkv, gkv = _rms_bwd_val(da_ref[...] + db_ref[...], a[:, Q_RANK + LANES:], kvn_ref[...])
        o_ref[:, 0:Q_RANK] = dxq.astype(BF16)
        o_ref[:, Q_RANK:Q_RANK + LANES] = dkpe_ref[...].astype(BF16)
        o_ref[:, Q_RANK + LANES:] = dxkv.astype(BF16)

        @pl.when(i == 0)
        def _():
            dqn_ref[...] = jnp.zeros_like(dqn_ref)
            dkvn_ref[...] = jnp.zeros_like(dkvn_ref)

        dqn_ref[...] += jnp.sum(gq, axis=0, keepdims=True)
        dkvn_ref[...] += jnp.sum(gkv, axis=0, keepdims=True)

    return pl.pallas_call(
        body, name="mla_norm_bwd", grid=(S // t,),
        in_specs=[_row_spec(t, Q_RANK), _row_spec(t, KV_RANK), _row_spec(t, KV_RANK), _row_spec(t, LANES),
                  _row_spec(t, 768), _acc_spec(Q_RANK), _acc_spec(KV_RANK)],
        out_specs=[_row_spec(t, 768), _acc_spec(Q_RANK), _acc_spec(KV_RANK)],
        out_shape=[jax.ShapeDtypeStruct((S, 768), BF16), jax.ShapeDtypeStruct((1, Q_RANK), F32),
                   jax.ShapeDtypeStruct((1, KV_RANK), F32)],
        compiler_params=_params("arbitrary"))(dcq, dckv_a, dckv_b, dkpe, proj_a, q_norm, kv_norm)


def _dil_mix(o_list, lse_list):
    S = o_list[0].shape[0]
    t = ROW_TILE

    def body(o0, o1, o2, l0, l1, l2, y_ref):
        ls = [l0[...], l1[...], l2[...]]
        m = jnp.maximum(jnp.maximum(ls[0], ls[1]), ls[2])
        es = [jnp.exp(l - m) for l in ls]
        den = es[0] + es[1] + es[2]
        y = (es[0] / den) * o0[...] + (es[1] / den) * o1[...] + (es[2] / den) * o2[...]
        y_ref[...] = y.astype(BF16)

    return pl.pallas_call(
        body, name="dil_mix", grid=(S // t,), in_specs=[_row_spec(t, DIL_W)] * 6, out_specs=_row_spec(t, DIL_W),
        out_shape=jax.ShapeDtypeStruct((S, DIL_W), BF16), compiler_params=_params("parallel"))(*o_list, *lse_list)


def _dil_mix_bwd(dy, o_list, lse_list):
    S = dy.shape[0]
    t = ROW_TILE

    def body(dy_ref, o0, o1, o2, l0, l1, l2, d0, d1, d2, e0, e1, e2):
        ls = [l0[...], l1[...], l2[...]]
        os_ = [o0[...], o1[...], o2[...]]
        m = jnp.maximum(jnp.maximum(ls[0], ls[1]), ls[2])
        es = [jnp.exp(l - m) for l in ls]
        den = es[0] + es[1] + es[2]
        ws = [e / den for e in es]
        dyv = dy_ref[...]
        y = ws[0] * os_[0] + ws[1] * os_[1] + ws[2] * os_[2]
        b = jnp.concatenate(_head_sum_bcast(dyv * y, DIL_HPG), axis=1)
        for w, d_ref, e_ref in zip(ws, (d0, d1, d2), (e0, e1, e2)):
            d_ref[...] = (w * dyv).astype(BF16)
            e_ref[...] = w * b

    return pl.pallas_call(
        body, name="dil_mix_bwd", grid=(S // t,), in_specs=[_row_spec(t, DIL_W)] * 7, out_specs=[_row_spec(t, DIL_W)] * 6,
        out_shape=[jax.ShapeDtypeStruct((S, DIL_W), BF16)] * 3 + [jax.ShapeDtypeStruct((S, DIL_W), F32)] * 3,
        compiler_params=_params("parallel"))(dy, *o_list, *lse_list)


def _delta(do, o, n_heads, name):
    S, W = do.shape
    t = ROW_TILE

    def body(do_ref, o_ref, d_ref):
        prod = do_ref[...].astype(F32) * o_ref[...].astype(F32)
        d_ref[...] = jnp.concatenate(_head_sum_bcast(prod, n_heads), axis=1)

    return pl.pallas_call(body, name=name, grid=(S // t,), in_specs=[_row_spec(t, W)] * 2, out_specs=_row_spec(t, W),
                          out_shape=jax.ShapeDtypeStruct((S, W), F32), compiler_params=_params("parallel"))(do, o)


def _merge(proj_g, b_gate, b_list):
    S = proj_g.shape[0]
    t = ROW_TILE

    def body(g_ref, b_ref, y0, y1, y2, o_ref):
        acc = jnp.zeros((t, D_MODEL), F32)
        for i, y in enumerate((y0, y1, y2)):
            sl = slice(i * D_MODEL, (i + 1) * D_MODEL)
            acc = acc + jax.nn.sigmoid(g_ref[:, sl] + b_ref[:, sl]) * y[...]
        o_ref[...] = acc.astype(BF16)

    return pl.pallas_call(
        body, name="merge", grid=(S // t,),
        in_specs=[_row_spec(t, 3 * D_MODEL), _acc_spec(3 * D_MODEL)] + [_row_spec(t, D_MODEL)] * 3,
        out_specs=_row_spec(t, D_MODEL), out_shape=jax.ShapeDtypeStruct((S, D_MODEL), BF16),
        compiler_params=_params("parallel"))(proj_g, b_gate, *b_list)


def _merge_bwd(dmerged, proj_g, b_gate, b_list):
    S = proj_g.shape[0]
    t = ROW_TILE

    def body(dm_ref, g_ref, b_ref, y0, y1, y2, d0, d1, d2, dz_ref, db_ref):
        i = pl.program_id(0)

        @pl.when(i == 0)
        def _():
            db_ref[...] = jnp.zeros_like(db_ref)

        dm = dm_ref[...]
        for k, (y, d_ref) in enumerate(zip((y0, y1, y2), (d0, d1, d2))):
            sl = slice(k * D_MODEL, (k + 1) * D_MODEL)
            s = jax.nn.sigmoid(g_ref[:, sl] + b_ref[:, sl])
            d_ref[...] = (s * dm).astype(BF16)
            dz = dm * y[...] * (s * (1.0 - s))
            dz_ref[:, sl] = dz.astype(BF16)
            db_ref[:, sl] += jnp.sum(dz, axis=0, keepdims=True)

    return pl.pallas_call(
        body, name="merge_bwd", grid=(S // t,),
        in_specs=[_row_spec(t, D_MODEL), _row_spec(t, 3 * D_MODEL), _acc_spec(3 * D_MODEL)] + [_row_spec(t, D_MODEL)] * 3,
        out_specs=[_row_spec(t, D_MODEL)] * 3 + [_row_spec(t, 3 * D_MODEL), _acc_spec(3 * D_MODEL)],
        out_shape=[jax.ShapeDtypeStruct((S, D_MODEL), BF16)] * 3
        + [jax.ShapeDtypeStruct((S, 3 * D_MODEL), BF16), jax.ShapeDtypeStruct((1, 3 * D_MODEL), F32)],
        compiler_params=_params("arbitrary"))(dmerged, proj_g, b_gate, *b_list)


def _norm2(o, x, g_post, g_pre):
    S = x.shape[0]
    t = ROW_TILE

    def body(o_ref, x_ref, gp_ref, gf_ref, x1_ref, h2_ref):
        x1 = x_ref[...] + _rms_fwd_val(o_ref[...], gp_ref[...])
        x1_ref[...] = x1
        h2_ref[...] = _rms_fwd_val(x1, gf_ref[...]).astype(BF16)

    return pl.pallas_call(
        body, name="norm2", grid=(S // t,),
        in_specs=[_row_spec(t, D_MODEL)] * 2 + [_acc_spec(D_MODEL)] * 2, out_specs=[_row_spec(t, D_MODEL)] * 2,
        out_shape=[jax.ShapeDtypeStruct((S, D_MODEL), F32), jax.ShapeDtypeStruct((S, D_MODEL), BF16)],
        compiler_params=_params("parallel"))(o, x, g_post, g_pre)


def _norm2_bwd(dx2, dh2, x1, o, g_pre, g_post):
    S = x1.shape[0]
    t = ROW_TILE

    def body(dx2_ref, dh2_ref, x1_ref, o_ref, gf_ref, gp_ref, dx1_ref, do_ref, dgf_ref, dgp_ref):
        i = pl.program_id(0)

        @pl.when(i == 0)
        def _():
            dgf_ref[...] = jnp.zeros_like(dgf_ref)
            dgp_ref[...] = jnp.zeros_like(dgp_ref)

        d1, gf = _rms_bwd_val(dh2_ref[...], x1_ref[...], gf_ref[...])
        dx1 = dx2_ref[...] + d1
        dx1_ref[...] = dx1
        do, gp = _rms_bwd_val(dx1, o_ref[...], gp_ref[...])
        do_ref[...] = do.astype(BF16)
        dgf_ref[...] += jnp.sum(gf, axis=0, keepdims=True)
        dgp_ref[...] += jnp.sum(gp, axis=0, keepdims=True)

    return pl.pallas_call(
        body, name="norm2_bwd", grid=(S // t,),
        in_specs=[_row_spec(t, D_MODEL)] * 4 + [_acc_spec(D_MODEL)] * 2,
        out_specs=[_row_spec(t, D_MODEL)] * 2 + [_acc_spec(D_MODEL)] * 2,
        out_shape=[jax.ShapeDtypeStruct((S, D_MODEL), F32), jax.ShapeDtypeStruct((S, D_MODEL), BF16),
                   jax.ShapeDtypeStruct((1, D_MODEL), F32), jax.ShapeDtypeStruct((1, D_MODEL), F32)],
        compiler_params=_params("arbitrary"))(dx2, dh2, x1, o, g_pre, g_post)


def _norm1_bwd(dx1, dh, x, g):
    S = x.shape[0]
    t = ROW_TILE

    def body(dx1_ref, dh_ref, x_ref, g_ref, dx_ref, dg_ref):
        i = pl.program_id(0)

        @pl.when(i == 0)
        def _():
            dg_ref[...] = jnp.zeros_like(dg_ref)

        d, gg = _rms_bwd_val(dh_ref[...], x_ref[...], g_ref[...])
        dx_ref[...] = dx1_ref[...] + d
        dg_ref[...] += jnp.sum(gg, axis=0, keepdims=True)

    return pl.pallas_call(
        body, name="norm1_bwd", grid=(S // t,),
        in_specs=[_row_spec(t, D_MODEL)] * 3 + [_acc_spec(D_MODEL)], out_specs=[_row_spec(t, D_MODEL), _acc_spec(D_MODEL)],
        out_shape=[jax.ShapeDtypeStruct((S, D_MODEL), F32), jax.ShapeDtypeStruct((1, D_MODEL), F32)],
        compiler_params=_params("arbitrary"))(dx1, dh, x, g)


def _gain_grad(dy, x, name):
    R, W = x.shape

    def body(dy_ref, x_ref, dg_ref):
        xv = x_ref[...]
        r = lax.rsqrt(jnp.mean(xv * xv, axis=-1, keepdims=True) + RMS_EPS)
        dg_ref[...] = jnp.sum(dy_ref[...] * (xv * r), axis=0, keepdims=True)

    return pl.pallas_call(body, name=name, grid=(1,), in_specs=[_full((R, W))] * 2, out_specs=_full((1, W)),
                          out_shape=jax.ShapeDtypeStruct((1, W), F32), compiler_params=_params("arbitrary"))(dy, x)


def _loss_head(f, x1, tgt, g):
    S = f.shape[0]
    t = ROW_TILE

    def body(f_ref, x1_ref, t_ref, g_ref, loss_ref, dx2_ref, df_ref, dg_ref):
        i = pl.program_id(0)

        @pl.when(i == 0)
        def _():
            loss_ref[...] = jnp.zeros_like(loss_ref)
            dg_ref[...] = jnp.zeros_like(dg_ref)

        fv, gv = f_ref[...], g_ref[...]
        err = x1_ref[...] + _rms_fwd_val(fv, gv) - t_ref[...]
        part = jnp.sum(jnp.mean(err * err, axis=-1, keepdims=True), axis=0, keepdims=True)
        loss_ref[...] += jnp.broadcast_to(0.5 * part, loss_ref.shape)
        dx2 = err * (1.0 / D_MODEL)
        dx2_ref[...] = dx2
        df, gg = _rms_bwd_val(dx2, fv, gv)
        df_ref[...] = df.astype(BF16)
        dg_ref[...] += jnp.sum(gg, axis=0, keepdims=True)

    return pl.pallas_call(
        body, name="loss_head", grid=(S // t,),
        in_specs=[_row_spec(t, D_MODEL)] * 3 + [_acc_spec(D_MODEL)],
        out_specs=[_acc_spec(LANES, 8), _row_spec(t, D_MODEL), _row_spec(t, D_MODEL), _acc_spec(D_MODEL)],
        out_shape=[jax.ShapeDtypeStruct((8, LANES), F32), jax.ShapeDtypeStruct((S, D_MODEL), F32),
                   jax.ShapeDtypeStruct((S, D_MODEL), BF16), jax.ShapeDtypeStruct((1, D_MODEL), F32)],
        compiler_params=_params("arbitrary"))(f, x1, tgt, g)


CONV_TC = 1408
CONV_TT = 256
HALO = 8


def _shift_down(u, halo, first):
    row = lax.broadcasted_iota(jnp.int32, u.shape, 0)
    h6 = jnp.where(first, 0.0, halo[HALO - 2:HALO - 1, :])
    h7 = jnp.where(first, 0.0, halo[HALO - 1:HALO, :])
    s1 = jnp.where(row == 0, h7, pltpu.roll(u, 1, 0))
    s2 = jnp.where(row == 0, h6, jnp.where(row == 1, h7, pltpu.roll(u, 2, 0)))
    return s1, s2


def _conv_specs(tt, n_c, n_t, lead):
    def halo_row(i):
        return jnp.maximum(i * (tt // HALO) - 1, 0) if lead else jnp.minimum((i + 1) * (tt // HALO), n_t * (tt // HALO) - 1)
    return [
        pl.BlockSpec((tt, CONV_TC), lambda j, i: (i, j)),
        pl.BlockSpec((tt, CONV_TC), lambda j, i: (i, j + n_c)),
        pl.BlockSpec((HALO, CONV_TC), lambda j, i: (halo_row(i), j)),
        pl.BlockSpec((HALO, CONV_TC), lambda j, i: (halo_row(i), j + n_c)),
    ]


def _conv_z(ug, uv, hg, hv, w_g, w_v, b_g, b_v, first):
    g1, g2 = _shift_down(ug, hg, first)
    v1, v2 = _shift_down(uv, hv, first)
    zg = b_g + w_g[0:1, :] * g2
    zg = zg + w_g[1:2, :] * g1
    zg = zg + w_g[2:3, :] * ug
    zv = b_v + w_v[0:1, :] * v2
    zv = zv + w_v[1:2, :] * v1
    zv = zv + w_v[2:3, :] * uv
    return zg, zv, (g2, g1, ug), (v2, v1, uv)


def _conv_fwd(u, conv_w, conv_b):
    S = u.shape[0]
    tt = min(CONV_TT, S)
    n_c, n_t = D_FF // CONV_TC, S // tt
    wspec = [pl.BlockSpec((3, CONV_TC), lambda j, i: (0, j)), pl.BlockSpec((3, CONV_TC), lambda j, i: (0, j + n_c)),
             pl.BlockSpec((1, CONV_TC), lambda j, i: (0, j)), pl.BlockSpec((1, CONV_TC), lambda j, i: (0, j + n_c))]

    def body(ug_ref, uv_ref, hg_ref, hv_ref, wg_ref, wv_ref, bg_ref, bv_ref, a_ref):
        first = pl.program_id(1) == 0
        zg, zv, _, _ = _conv_z(ug_ref[...], uv_ref[...], hg_ref, hv_ref, wg_ref, wv_ref, bg_ref[...], bv_ref[...], first)
        a_ref[...] = (zg * jax.nn.sigmoid(zg) * zv).astype(BF16)

    return pl.pallas_call(
        body, name="conv_fwd", grid=(n_c, n_t), in_specs=_conv_specs(tt, n_c, n_t, True) + wspec,
        out_specs=pl.BlockSpec((tt, CONV_TC), lambda j, i: (i, j)), out_shape=jax.ShapeDtypeStruct((S, D_FF), BF16),
        compiler_params=_params("parallel", "parallel"))(u, u, u, u, conv_w, conv_w, conv_b, conv_b)


def _conv_bwd_dz(da, u, conv_w, conv_b):
    S = u.shape[0]
    tt = min(CONV_TT, S)
    n_c, n_t = D_FF // CONV_TC, S // tt
    wspec = [pl.BlockSpec((3, CONV_TC), lambda j, i: (0, j)), pl.BlockSpec((3, CONV_TC), lambda j, i: (0, j + n_c)),
             pl.BlockSpec((1, CONV_TC), lambda j, i: (0, j)), pl.BlockSpec((1, CONV_TC), lambda j, i: (0, j + n_c))]
    tile = pl.BlockSpec((tt, CONV_TC), lambda j, i: (i, j))
    tile_v = pl.BlockSpec((tt, CONV_TC), lambda j, i: (i, j + n_c))

    def body(da_ref, ug_ref, uv_ref, hg_ref, hv_ref, wg_ref, wv_ref, bg_ref, bv_ref,
             dzg_ref, dzv_ref, dwg_ref, dwv_ref, dbg_ref, dbv_ref):
        i = pl.program_id(1)
        zg, zv, gs, vs = _conv_z(ug_ref[...], uv_ref[...], hg_ref, hv_ref, wg_ref, wv_ref, bg_ref[...], bv_ref[...], i == 0)
        dav = da_ref[...]
        sg = jax.nn.sigmoid(zg)
        dzv = dav * (zg * sg)
        dzg = dav * zv * (sg * (1.0 + zg * (1.0 - sg)))
        dzg_ref[...] = dzg
        dzv_ref[...] = dzv

        @pl.when(i == 0)
        def _():
            for r in (dwg_ref, dwv_ref, dbg_ref, dbv_ref):
                r[...] = jnp.zeros_like(r)

        for k in range(3):
            dwg_ref[k:k + 1, :] += jnp.sum(dzg * gs[k], axis=0, keepdims=True)
            dwv_ref[k:k + 1, :] += jnp.sum(dzv * vs[k], axis=0, keepdims=True)
        dbg_ref[...] += jnp.sum(dzg, axis=0, keepdims=True)
        dbv_ref[...] += jnp.sum(dzv, axis=0, keepdims=True)

    outs = pl.pallas_call(
        body, name="conv_bwd_dz", grid=(n_c, n_t), in_specs=[tile] + _conv_specs(tt, n_c, n_t, True) + wspec,
        out_specs=[tile, tile] + [pl.BlockSpec((3, CONV_TC), lambda j, i: (0, j))] * 2 + [pl.BlockSpec((1, CONV_TC), lambda j, i: (0, j))] * 2,
        out_shape=[jax.ShapeDtypeStruct((S, D_FF), F32)] * 2 + [jax.ShapeDtypeStruct((3, D_FF), F32)] * 2
        + [jax.ShapeDtypeStruct((1, D_FF), F32)] * 2,
        compiler_params=_params("parallel", "arbitrary"))(da, u, u, u, u, conv_w, conv_w, conv_b, conv_b)
    dzg, dzv, dwg, dwv, dbg, dbv = outs
    return dzg, dzv, jnp.concatenate([dwg, dwv], axis=1), jnp.concatenate([dbg, dbv], axis=1)


def _conv_bwd_du(dzg, dzv, conv_w):
    S = dzg.shape[0]
    tt = min(CONV_TT, S)
    n_c, n_t = D_FF // CONV_TC, S // tt
    steps = tt // HALO

    def nxt(i):
        return jnp.minimum((i + 1) * steps, n_t * steps - 1)

    tile_g = pl.BlockSpec((tt, CONV_TC), lambda h, j, i: (i * (1 - h), j * (1 - h)))
    tile_v = pl.BlockSpec((tt, CONV_TC), lambda h, j, i: (i * h, j * h))
    halo_g = pl.BlockSpec((HALO, CONV_TC), lambda h, j, i: (nxt(i) * (1 - h), j * (1 - h)))
    halo_v = pl.BlockSpec((HALO, CONV_TC), lambda h, j, i: (nxt(i) * h, j * h))
    wsp = pl.BlockSpec((3, CONV_TC), lambda h, j, i: (0, j + h * n_c))

    def body(zg_ref, zv_ref, hg_ref, hv_ref, w_ref, du_ref):
        h = pl.program_id(0)
        last = pl.program_id(2) == n_t - 1
        z = jnp.where(h == 0, zg_ref[...], zv_ref[...])
        hal = jnp.where(h == 0, hg_ref[...], hv_ref[...])
        row = lax.broadcasted_iota(jnp.int32, z.shape, 0)
        h0 = jnp.where(last, 0.0, hal[0:1, :])
        h1 = jnp.where(last, 0.0, hal[1:2, :])
        u1 = jnp.where(row == tt - 1, h0, pltpu.roll(z, tt - 1, 0))
        u2 = jnp.where(row == tt - 1, h1, jnp.where(row == tt - 2, h0, pltpu.roll(z, tt - 2, 0)))
        du = w_ref[2:3, :] * z + w_ref[1:2, :] * u1 + w_ref[0:1, :] * u2
        du_ref[...] = du.astype(BF16)

    return pl.pallas_call(
        body, name="conv_bwd_du", grid=(2, n_c, n_t), in_specs=[tile_g, tile_v, halo_g, halo_v, wsp],
        out_specs=pl.BlockSpec((tt, CONV_TC), lambda h, j, i: (i, j + h * n_c)),
        out_shape=jax.ShapeDtypeStruct((S, 2 * D_FF), BF16),
        compiler_params=_params("parallel", "parallel", "parallel"))(dzg, dzv, dzg, dzv, conv_w)


BAND = 128


MEM_TQ = 512


def _mem_fwd(q, kv, *, name):
    S, W = q.shape
    M = kv.shape[0]
    nh = W // LANES
    tq = min(MEM_TQ, S)

    def body(q_ref, k_ref, v_ref, o_ref, l_ref):
        s = lax.dot_general(q_ref[...], k_ref[...].astype(BF16), NT, preferred_element_type=F32)
        m = jnp.max(s, axis=1, keepdims=True)
        p = jnp.exp(s - m)
        l = jnp.sum(p, axis=1, keepdims=True)
        o_ref[...] = (lax.dot_general(p.astype(BF16), v_ref[...].astype(BF16), NN, preferred_element_type=F32) / l).astype(BF16)
        l_ref[...] = jnp.broadcast_to(m + jnp.log(l), (tq, LANES))

    blk = pl.BlockSpec((tq, LANES), lambda hh, i: (i, hh))
    return pl.pallas_call(
        body, name=name, grid=(nh, S // tq),
        in_specs=[blk, pl.BlockSpec((M, LANES), lambda hh, i: (0, hh)), pl.BlockSpec((M, LANES), lambda hh, i: (0, hh + nh))],
        out_specs=[blk, blk], out_shape=[jax.ShapeDtypeStruct((S, W), BF16), jax.ShapeDtypeStruct((S, W), F32)],
        compiler_params=_params("parallel", "parallel"))(q, kv, kv)


def _mem_bwd(q, kv, do, lse, delta, *, scale, name):
    S, W = q.shape
    M = kv.shape[0]
    nh = W // LANES
    tq = min(MEM_TQ, S)

    def body(q_ref, k_ref, v_ref, do_ref, l_ref, d_ref, dq_ref, dk_ref, dv_ref):
        i = pl.program_id(1)

        @pl.when(i == 0)
        def _():
            dk_ref[...] = jnp.zeros_like(dk_ref)
            dv_ref[...] = jnp.zeros_like(dv_ref)

        qv = q_ref[...]
        kv_, vv = k_ref[...].astype(BF16), v_ref[...].astype(BF16)
        dov = do_ref[...].astype(BF16)
        s = lax.dot_general(qv, kv_, NT, preferred_element_type=F32)
        p = jnp.exp(s - l_ref[...][:, 0:1])
        dp = lax.dot_general(dov, vv, NT, preferred_element_type=F32)
        ds = (p * (dp - d_ref[...][:, 0:1])).astype(BF16)
        dq_ref[...] = lax.dot_general(ds, kv_, NN, preferred_element_type=F32) * scale
        dk_ref[...] += lax.dot_general(ds, qv, TN, preferred_element_type=F32)
        dv_ref[...] += lax.dot_general(p.astype(BF16), dov, TN, preferred_element_type=F32)

    blk = pl.BlockSpec((tq, LANES), lambda hh, i: (i, hh))
    kblk = pl.BlockSpec((M, LANES), lambda hh, i: (0, hh))
    vblk = pl.BlockSpec((M, LANES), lambda hh, i: (0, hh + nh))
    dq, dk, dv = pl.pallas_call(
        body, name=name, grid=(nh, S // tq), in_specs=[blk, kblk, vblk, blk, blk, blk], out_specs=[blk, kblk, kblk],
        out_shape=[jax.ShapeDtypeStruct((S, W), F32), jax.ShapeDtypeStruct((M, W), F32), jax.ShapeDtypeStruct((M, W), F32)],
        compiler_params=_params("parallel", "arbitrary"))(q, kv, kv, do, lse, delta)
    return dq, jnp.concatenate([dk, dv], axis=1)


CAUSAL_BLOCK = 512
STRIP = 32


def _causal_fwd(q, k, v, *, name):
    S, W = q.shape
    T = min(CAUSAL_BLOCK, S // 2)
    n_strips = T // STRIP

    def body(q_ref, k_ref, v_ref, o_ref, lse_ref, s0, s1, p0, p1, a0, a1, acc_scr):
        i = pl.program_id(1)
        s_scr, p_scr, a_scr = (s0, s1), (p0, p1), (a0, a1)

        def rows(j):
            return pl.ds(pl.multiple_of(j * T, T), T)

        def scores(j, slot):
            s_scr[slot][...] = lax.dot_general(q_ref[...], k_ref[rows(j), :], NT, preferred_element_type=F32)

        def softmax(slot, stats, diag):
            def strip(r):
                s = s_scr[slot][r * STRIP:(r + 1) * STRIP, :]
                if diag:
                    row = r * STRIP + lax.broadcasted_iota(jnp.int32, s.shape, 0)
                    s = jnp.where(row >= lax.broadcasted_iota(jnp.int32, s.shape, 1), s, NEG_INF)
                return s

            m_new = [jnp.maximum(m_old, jnp.max(strip(r), axis=1, keepdims=True)) for r, (m_old, _) in enumerate(stats)]
            new = []
            for r, (m_old, l_old) in enumerate(stats):
                rs = slice(r * STRIP, (r + 1) * STRIP)
                p = jnp.exp(strip(r) - m_new[r])
                alpha = jnp.exp(m_old - m_new[r])
                new.append((m_new[r], alpha * l_old + jnp.sum(p, axis=1, keepdims=True)))
                a_scr[slot][rs, :] = alpha
                p_scr[slot][rs, :] = p.astype(BF16)
            return tuple(new)

        def values(j, slot):
            acc_scr[...] = a_scr[slot][...] * acc_scr[...] + lax.dot_general(p_scr[slot][...], v_ref[rows(j), :], NN,
                                                                            preferred_element_type=F32)

        def trip(j, stats, mine, other):
            scores(j + 1, other)
            stats = softmax(mine, stats, False)
            values(jnp.maximum(j - 1, 0), other)
            return stats

        def pair(jj, stats):
            return trip(2 * jj + 1, trip(2 * jj, stats, 0, 1), 1, 0)

        def last(stats, mine, other):
            stats = softmax(mine, stats, True)
            values(jnp.maximum(i - 1, 0), other)
            values(i, mine)
            for r, (m, l) in enumerate(stats):
                rs = slice(r * STRIP, (r + 1) * STRIP)
                o_ref[rs, :] = (acc_scr[rs, :] / l).astype(BF16)
                lse_ref[rs, :] = jnp.broadcast_to(m + jnp.log(l), (STRIP, LANES))

        acc_scr[...] = jnp.zeros_like(acc_scr)
        p1[...] = jnp.zeros_like(p1)
        a1[...] = jnp.ones_like(a1)
        scores(0, 0)
        init = tuple((jnp.full((STRIP, 1), NEG_INF, F32), jnp.zeros((STRIP, 1), F32)) for _ in range(n_strips))
        stats = lax.fori_loop(0, i // 2, pair, init)

        @pl.when(i % 2 == 1)
        def _():
            last(trip(i - 1, stats, 0, 1), 1, 0)

        @pl.when(i % 2 == 0)
        def _():
            last(stats, 0, 1)

    blk = pl.BlockSpec((T, LANES), lambda hh, i: (i, hh))
    whole = pl.BlockSpec((S, LANES), lambda hh, i: (0, hh))
    return pl.pallas_call(
        body, name=name, grid=(W // LANES, S // T), in_specs=[blk, whole, whole], out_specs=[blk, blk],
        out_shape=[jax.ShapeDtypeStruct((S, W), BF16), jax.ShapeDtypeStruct((S, W), F32)],
        scratch_shapes=[pltpu.VMEM((T, T), F32)] * 2 + [pltpu.VMEM((T, T), BF16)] * 2 + [pltpu.VMEM((T, 1), F32)] * 2
        + [pltpu.VMEM((T, LANES), F32)],
        compiler_params=_params("parallel", "arbitrary"))(q, k, v)


def _causal_bwd(q, k, v, do, lse, delta, *, name):
    S, W = q.shape
    T = min(CAUSAL_BLOCK, S // 2)
    nq = S // T
    n_strips, n_col = T // STRIP, T // LANES

    def body(q_ref, k_ref, v_ref, do_ref, l_ref, d_ref, dq_ref, dk_ref, dv_ref, s0, s1, e0, e1, p0, p1, g0, g1):
        j = pl.program_id(1)
        s_scr, e_scr, p_scr, g_scr = (s0, s1), (e0, e1), (p0, p1), (g0, g1)

        @pl.when(j == 0)
        def _():
            dq_ref[...] = jnp.zeros_like(dq_ref)

        dk_ref[...] = jnp.zeros_like(dk_ref)
        dv_ref[...] = jnp.zeros_like(dv_ref)

        def rows(i):
            return pl.ds(pl.multiple_of(jnp.minimum(i, nq - 1) * T, T), T)

        def products(i, slot):
            r = rows(i)
            s_scr[slot][...] = lax.dot_general(q_ref[r, :], k_ref[...], NT, preferred_element_type=F32)
            e_scr[slot][...] = lax.dot_general(do_ref[r, :].astype(BF16), v_ref[...], NT, preferred_element_type=F32)

        def pointwise(i, slot, diag):
            base = pl.multiple_of(i * T, T)
            for r in range(n_strips):
                rs = slice(r * STRIP, (r + 1) * STRIP)
                lse_r = l_ref[pl.ds(base + r * STRIP, STRIP), :]
                del_r = d_ref[pl.ds(base + r * STRIP, STRIP), :]
                for c in range(n_col):
                    cs = slice(c * LANES, (c + 1) * LANES)
                    if diag and c * LANES > (r + 1) * STRIP - 1:
                        p_scr[slot][rs, cs] = jnp.zeros((STRIP, LANES), BF16)
                        g_scr[slot][rs, cs] = jnp.zeros((STRIP, LANES), BF16)
                        continue
                    sv = s_scr[slot][rs, cs]
                    if diag and (c + 1) * LANES - 1 > r * STRIP:
                        row = r * STRIP + lax.broadcasted_iota(jnp.int32, sv.shape, 0)
                        col = c * LANES + lax.broadcasted_iota(jnp.int32, sv.shape, 1)
                        sv = jnp.where(row >= col, sv, NEG_INF)
                    p = jnp.exp(sv - lse_r)
                    p_scr[slot][rs, cs] = p.astype(BF16)
                    g_scr[slot][rs, cs] = (p * (e_scr[slot][rs, cs] - del_r)).astype(BF16)

        def gradients(i, slot):
            r = rows(i)
            qi, doi = q_ref[r, :], do_ref[r, :].astype(BF16)
            g = g_scr[slot][...]
            dv_ref[...] += lax.dot_general(p_scr[slot][...], doi, TN, preferred_element_type=F32)
            dk_ref[...] += lax.dot_general(g, qi, TN, preferred_element_type=F32)
            dq_ref[r, :] += lax.dot_general(g, k_ref[...], NN, preferred_element_type=F32)

        p1[...] = jnp.zeros_like(p1)
        g1[...] = jnp.zeros_like(g1)
        products(j, 0)
        products(j + 1, 1)
        pointwise(j, 0, True)
        gradients(j, 1)

        def trip(i, mine, other):
            products(i + 1, other)
            pointwise(i, mine, False)
            gradients(i - 1, other)

        def pair(t, _):
            trip(j + 1 + 2 * t, 1, 0)
            trip(j + 2 + 2 * t, 0, 1)
            return 0

        n_rest = nq - 1 - j
        lax.fori_loop(0, n_rest // 2, pair, 0)

        @pl.when(n_rest % 2 == 1)
        def _():
            trip(nq - 1, 1, 0)
            gradients(nq - 1, 1)

        @pl.when(n_rest % 2 == 0)
        def _():
            gradients(nq - 1, 0)

    blk = pl.BlockSpec((T, LANES), lambda hh, j: (j, hh))
    whole = pl.BlockSpec((S, LANES), lambda hh, j: (0, hh))
    return pl.pallas_call(
        body, name=name, grid=(W // LANES, S // T), in_specs=[whole, blk, blk, whole, whole, whole],
        out_specs=[whole, blk, blk], out_shape=[jax.ShapeDtypeStruct((S, W), F32)] * 3,
        scratch_shapes=[pltpu.VMEM((T, T), F32)] * 4 + [pltpu.VMEM((T, T), BF16)] * 4,
        compiler_params=_params("parallel", "arbitrary"))(q, k, v, do, lse, delta)


BAND_TQ = 512


def _band_window(i, sub, nsub, L, q_ref, k_ref, v_ref, slope):
    kw = min(2 * BAND, L)
    n = i * nsub + sub
    k0 = 0 if kw == L else pl.multiple_of(jnp.maximum(n - 1, 0) * BAND, BAND)
    win = pl.ds(k0, kw)
    qs = q_ref[sub * BAND:(sub + 1) * BAND, :]
    kwv, vwv = k_ref[win, :], v_ref[win, :]
    s = lax.dot_general(qs, kwv, NT, preferred_element_type=F32)
    dist = (n * BAND + lax.broadcasted_iota(jnp.int32, s.shape, 0)) - (k0 + lax.broadcasted_iota(jnp.int32, s.shape, 1))
    s = jnp.where((dist >= 0) & (dist <= BAND), s - slope * dist.astype(F32), NEG_INF)
    return win, qs, kwv, vwv, s


def _band_fwd(q, k, v, slopes, *, n_heads, qcol, kcol, vcol, L, slope_mul, o_shape, name):
    tq = min(BAND_TQ, L)
    nsub = tq // BAND

    def body(sl_ref, q_ref, k_ref, v_ref, o_ref, l_ref):
        hh, i = pl.program_id(0), pl.program_id(1)
        slope = sl_ref[hh % DIL_HPG] * slope_mul
        for sub in range(nsub):
            _, _, _, vwv, s = _band_window(i, sub, nsub, L, q_ref, k_ref, v_ref, slope)
            m = jnp.max(s, axis=1, keepdims=True)
            p = jnp.exp(s - m)
            l = jnp.sum(p, axis=1, keepdims=True)
            rows = slice(sub * BAND, (sub + 1) * BAND)
            o_ref[rows, :] = lax.dot_general(p.astype(BF16), vwv, NN, preferred_element_type=F32) / l
            l_ref[rows, :] = jnp.broadcast_to(m + jnp.log(l), (BAND, LANES))

    whole = lambda col: pl.BlockSpec((L, LANES), lambda hh, i: (0, col(hh)))
    o_spec = pl.BlockSpec((tq, LANES), lambda hh, i: (i, hh))
    return pl.pallas_call(
        body, name=name, grid=(n_heads, L // tq),
        in_specs=[pl.BlockSpec(memory_space=pltpu.SMEM), pl.BlockSpec((tq, LANES), lambda hh, i: (i, qcol(hh))), whole(kcol), whole(vcol)],
        out_specs=[o_spec, o_spec], out_shape=[jax.ShapeDtypeStruct(o_shape, F32)] * 2,
        compiler_params=_params("parallel", "arbitrary"))(slopes, q, k, v)


def _band_bwd(q, k, v, do, lse, delta, slopes, *, n_heads, qcol, kcol, vcol, L, scale, slope_mul, d_shape, name):
    tq = min(BAND_TQ, L)
    nsub = tq // BAND
    n_steps = L // tq

    def body(sl_ref, q_ref, k_ref, v_ref, do_ref, l_ref, d_ref, dq_ref, dk_ref, dv_ref, dk_acc, dv_acc):
        hh, i = pl.program_id(0), pl.program_id(1)
        slope = sl_ref[hh % DIL_HPG] * slope_mul

        @pl.when(i == 0)
        def _():
            dk_acc[...] = jnp.zeros_like(dk_acc)
            dv_acc[...] = jnp.zeros_like(dv_acc)

        for sub in range(nsub):
            win, qs, kwv, vwv, s = _band_window(i, sub, nsub, L, q_ref, k_ref, v_ref, slope)
            rows = slice(sub * BAND, (sub + 1) * BAND)
            dos = do_ref[rows, :]
            p = jnp.exp(s - l_ref[rows, :][:, 0:1])
            dp = lax.dot_general(dos, vwv, NT, preferred_element_type=F32)
            ds = (p * (dp - d_ref[rows, :][:, 0:1])).astype(BF16)
            dq_ref[rows, :] = (lax.dot_general(ds, kwv, NN, preferred_element_type=F32) * scale).astype(BF16)
            dk_acc[win, :] += lax.dot_general(ds, qs, TN, preferred_element_type=F32)
            dv_acc[win, :] += lax.dot_general(p.astype(BF16), dos, TN, preferred_element_type=F32)

        @pl.when(i == n_steps - 1)
        def _():
            dk_ref[...] = dk_acc[...].astype(BF16)
            dv_ref[...] = dv_acc[...].astype(BF16)

    whole = lambda col: pl.BlockSpec((L, LANES), lambda hh, i: (0, col(hh)))
    blk = pl.BlockSpec((tq, LANES), lambda hh, i: (i, hh))
    ident = lambda hh: hh
    return pl.pallas_call(
        body, name=name, grid=(n_heads, n_steps),
        in_specs=[pl.BlockSpec(memory_space=pltpu.SMEM), pl.BlockSpec((tq, LANES), lambda hh, i: (i, qcol(hh))), whole(kcol), whole(vcol),
                  blk, blk, blk],
        out_specs=[blk, whole(ident), whole(ident)], out_shape=[jax.ShapeDtypeStruct(d_shape, BF16)] * 3,
        scratch_shapes=[pltpu.VMEM((L, LANES), F32)] * 2,
        compiler_params=_params("parallel", "arbitrary"))(slopes, q, k, v, do, lse, delta)


def _pad_heads(w, n_heads, width, axis):
    shp = w.shape
    new = shp[:axis] + (n_heads, width) + shp[axis + 1:]
    pad = [(0, 0)] * len(new)
    pad[axis + 1] = (0, LANES - width)
    out = jnp.pad(w.reshape(new), pad)
    return out.reshape(shp[:axis] + (n_heads * LANES,) + shp[axis + 1:])


def _unpad_heads(w, n_heads, width, axis):
    shp = w.shape
    new = shp[:axis] + (n_heads, LANES) + shp[axis + 1:]
    out = lax.slice_in_dim(w.reshape(new), 0, width, axis=axis + 1)
    return out.reshape(shp[:axis] + (n_heads * width,) + shp[axis + 1:])


def _alibi_slopes():
    s = jnp.exp2(-8.0 * jnp.arange(1, DIL_HEADS + 1, dtype=F32) / DIL_HEADS)
    return s.reshape(DIL_HPG, DIL_GROUPS).T


def _local_step(x, mem, positions, tgt, W):
    S = x.shape[0]
    pos = positions.reshape(S, 1).astype(F32)
    half = MLA_ROPE // 2
    inv_freq = ROPE_THETA ** (-jnp.arange(half, dtype=F32) / half)
    invf = jnp.zeros((1, LANES), F32).at[0, MLA_NOPE:MLA_NOPE + half].set(inv_freq).at[0, MLA_NOPE + half:MLA_QK].set(inv_freq)
    slopes = _alibi_slopes()

    w_in = W["w_in"]
    zc = lambda n: jnp.zeros((D_MODEL, n), BF16)
    w_a = jnp.concatenate([w_in[:, :OFF_Q], zc(MLA_NOPE), w_in[:, OFF_KV:OFF_KR], zc(LANES - MLA_QK), w_in[:, OFF_Q:OFF_KV]], axis=1)
    w_d, w_m, w_g = w_in[:, OFF_KR:OFF_DIL], w_in[:, OFF_DIL:OFF_MEMQ], w_in[:, OFF_MEMQ:]
    w_uq_p = _pad_heads(W["w_uq"], MLA_HEADS, MLA_QK, 1)
    ukv = W["w_ukv"].reshape(KV_RANK, MLA_HEADS, 2 * MLA_NOPE)
    w_uk_p = _pad_heads(ukv[:, :, :MLA_NOPE].reshape(KV_RANK, -1), MLA_HEADS, MLA_NOPE, 1)
    w_uv_p = _pad_heads(ukv[:, :, MLA_NOPE:].reshape(KV_RANK, -1), MLA_HEADS, MLA_NOPE, 1)
    w_br_mla_p = _pad_heads(W["w_br_mla"], MLA_HEADS, MLA_NOPE, 0)

    h = _rmsnorm(x, W["g_pre_mix"], BF16, "norm1")
    proj_a = _matmul(h, w_a, tn=768, name="proj_a")
    att_scale = LANES ** -0.5
    proj_d = _matmul(h, w_d, tn=DIL_GROUPS * DIL_W, out_dtype=BF16, scale=(1, att_scale), name="proj_d")
    proj_m = _matmul(h, w_m, tn=MEM_W, out_dtype=BF16, scale=(1, att_scale), name="proj_m")
    proj_g = _matmul(h, w_g, tn=1536, name="proj_g")

    cq_n, ckv_n, kpe = _mla_prep(proj_a, pos, invf, W["mla_q_norm"], W["mla_kv_norm"])
    q_pre = _matmul(cq_n, w_uq_p, tn=1024, name="mla_q")
    k_pre = _matmul(ckv_n, w_uk_p, tn=1024, name="mla_k")
    v_mla = _matmul(ckv_n, w_uv_p, tn=1024, out_dtype=BF16, name="mla_v")
    q_mla, k_mla = _qk_final(q_pre, k_pre, kpe, pos, invf)
    ident = lambda hh: hh
    o_mla, lse_mla = _causal_fwd(q_mla, k_mla, v_mla, name="attn_mla_fwd")

    n_dil_cols = 3 * DIL_HEADS
    o_dil, lse_dil = [], []
    for g, (window, dil) in enumerate(DIL_PAIRS):
        L = S // dil
        view = proj_d.reshape(L, dil * 3 * DIL_HEADS * LANES)
        col = lambda part, g=g: (lambda hh: (hh // DIL_HPG) * n_dil_cols + (part * DIL_GROUPS + g) * DIL_HPG + hh % DIL_HPG)
        o, lse = _band_fwd(view, view, view, slopes[g], n_heads=dil * DIL_HPG, qcol=col(0), kcol=col(1), vcol=col(2), L=L,
                           slope_mul=float(dil), o_shape=(L, dil * DIL_W), name=f"attn_dil{g}_fwd")
        o_dil.append(o.reshape(S, DIL_W))
        lse_dil.append(lse.reshape(S, DIL_W))
    y_dil = _dil_mix(o_dil, lse_dil)

    mem_n = _rmsnorm(mem, W["g_mem"], BF16, "mem_norm")
    kv_mem = _matmul(mem_n, W["w_mem_kv"], name="mem_kv")
    o_mem, lse_mem = _mem_fwd(proj_m, kv_mem, name="attn_mem_fwd")

    b_mla = _matmul(o_mla, w_br_mla_p, name="br_mla")
    b_dil = _matmul(y_dil, W["w_br_dil"], name="br_dil")
    b_mem = _matmul(o_mem, W["w_br_mem"], name="br_mem")
    merged = _merge(proj_g, W["b_gate"], [b_mla, b_dil, b_mem])
    o_proj = _matmul(merged, W["w_o"], name="o_proj")
    x1, h2 = _norm2(o_proj, x, W["g_post_mix"], W["g_pre_ffn"])

    u = _matmul(h2, W["w_ffn_up"], tn=1408, name="ffn_up")
    act = _conv_fwd(u, W["conv_w"], W["conv_b"])
    f = _matmul(act, W["w_ffn_down"], name="ffn_down")
    loss8, dx2, df, dg_post_ffn = _loss_head(f, x1, tgt, W["g_post_ffn"])
    loss = loss8[0, 0]

    G = {"g_post_ffn": dg_post_ffn}
    d_act = _matmul(df, W["w_ffn_down"], mode="nt", tn=1408, name="d_act")
    G["w_ffn_down"] = _matmul(act, df, mode="tn", tm=1408, tk=1024, name="dw_ffn_down")
    dzg, dzv, G["conv_w"], G["conv_b"] = _conv_bwd_dz(d_act, u, W["conv_w"], W["conv_b"])
    du = _conv_bwd_du(dzg, dzv, W["conv_w"])
    dh2 = _matmul(du, W["w_ffn_up"], mode="nt", tk=2816, name="d_h2")
    G["w_ffn_up"] = _matmul(h2, du, mode="tn", tn=1408, tk=1024, name="dw_ffn_up")
    dx1, do_proj, G["g_pre_ffn"], G["g_post_mix"] = _norm2_bwd(dx2, dh2, x1, o_proj, W["g_pre_ffn"], W["g_post_mix"])
    dmerged = _matmul(do_proj, W["w_o"], mode="nt", name="d_merged")
    G["w_o"] = _matmul(merged, do_proj, mode="tn", tk=1024, name="dw_o")
    db_mla, db_dil, db_mem, dproj_g, G["b_gate"] = _merge_bwd(dmerged, proj_g, W["b_gate"], [b_mla, b_dil, b_mem])

    dy_mem = _matmul(db_mem, W["w_br_mem"], mode="nt", name="d_y_mem")
    G["w_br_mem"] = _matmul(o_mem, db_mem, mode="tn", tk=1024, name="dw_br_mem")
    delta_mem = _delta(dy_mem, o_mem, MEM_HEADS, "delta_mem")
    dq_mem, dkv_mem = _mem_bwd(proj_m, kv_mem, dy_mem, lse_mem, delta_mem, scale=att_scale, name="attn_mem_bwd")
    G["w_mem_kv"] = _matmul(mem_n, dkv_mem, mode="tn", name="dw_mem_kv")
    dmem_n = _matmul(dkv_mem, W["w_mem_kv"], mode="nt", name="d_mem_n")
    G["g_mem"] = _gain_grad(dmem_n, mem, "dg_mem")

    dy_dil = _matmul(db_dil, W["w_br_dil"], mode="nt", name="d_y_dil")
    G["w_br_dil"] = _matmul(y_dil, db_dil, mode="tn", tk=1024, name="dw_br_dil")
    mix = _dil_mix_bwd(dy_dil, o_dil, lse_dil)
    do_dil, dl_dil = mix[:3], mix[3:]
    d_parts = [[None] * DIL_GROUPS for _ in range(3)]
    for g, (window, dil) in enumerate(DIL_PAIRS):
        L = S // dil
        view = proj_d.reshape(L, dil * 3 * DIL_HEADS * LANES)
        col = lambda part, g=g: (lambda hh: (hh // DIL_HPG) * n_dil_cols + (part * DIL_GROUPS + g) * DIL_HPG + hh % DIL_HPG)
        shp = (L, dil * DIL_W)
        dq, dk, dv = _band_bwd(
            view, view, view, do_dil[g].reshape(shp), lse_dil[g].reshape(shp), dl_dil[g].reshape(shp), slopes[g],
            n_heads=dil * DIL_HPG, qcol=col(0), kcol=col(1), vcol=col(2), L=L, scale=att_scale, slope_mul=float(dil),
            d_shape=shp, name=f"attn_dil{g}_bwd")
        for part, d in enumerate((dq, dk, dv)):
            d_parts[part][g] = d.reshape(S, DIL_W)
    dproj_d = jnp.concatenate([d for part in d_parts for d in part], axis=1)

    dy_mla = _matmul(db_mla, w_br_mla_p, mode="nt", name="d_y_mla")
    dw_br_mla_p = _matmul(o_mla, db_mla, mode="tn", tk=1024, name="dw_br_mla")
    G["w_br_mla"] = _unpad_heads(dw_br_mla_p, MLA_HEADS, MLA_NOPE, 0)
    delta_mla = _delta(dy_mla, o_mla, MLA_HEADS, "delta_mla")
    dq_mla, dk_mla, dv_mla = _causal_bwd(q_mla, k_mla, v_mla, dy_mla, lse_mla, delta_mla, name="attn_mla_bwd")
    dq_pre, dkpe = _mla_bwd_prep(dq_mla, dk_mla, pos, invf)
    dcq_n = _matmul(dq_pre, w_uq_p, mode="nt", tn=Q_RANK, name="d_cq")
    G["w_uq"] = _unpad_heads(_matmul(cq_n, dq_pre, mode="tn", tm=Q_RANK, tk=1024, name="dw_uq"), MLA_HEADS, MLA_QK, 1)
    dckv_a = _matmul(dk_mla, w_uk_p, mode="nt", tn=KV_RANK, name="d_ckv_k")
    dckv_b = _matmul(dv_mla, w_uv_p, mode="nt", tn=KV_RANK, name="d_ckv_v")
    dw_uk = _unpad_heads(_matmul(ckv_n, dk_mla, mode="tn", tm=KV_RANK, tk=1024, name="dw_uk"), MLA_HEADS, MLA_NOPE, 1)
    dw_uv = _unpad_heads(_matmul(ckv_n, dv_mla, mode="tn", tm=KV_RANK, tk=1024, name="dw_uv"), MLA_HEADS, MLA_NOPE, 1)
    G["w_ukv"] = jnp.concatenate([dw_uk.reshape(KV_RANK, MLA_HEADS, MLA_NOPE), dw_uv.reshape(KV_RANK, MLA_HEADS, MLA_NOPE)],
                                 axis=2).reshape(KV_RANK, -1)
    dproj_a, G["mla_q_norm"], G["mla_kv_norm"] = _mla_norm_bwd(dcq_n, dckv_a, dckv_b, dkpe, proj_a, W["mla_q_norm"],
                                                              W["mla_kv_norm"])

    dh = _matmul(dproj_a, w_a, mode="nt", name="d_h_a")
    dh = _matmul(dproj_d, w_d, mode="nt", tk=2304, add=dh, name="d_h_d")
    dh = _matmul(dq_mem, w_m, mode="nt", add=dh, name="d_h_m")
    dh = _matmul(dproj_g, w_g, mode="nt", add=dh, name="d_h_g")
    dw_a = _matmul(h, dproj_a, mode="tn", tn=768, tk=1024, name="dw_in_a")
    dw_d = _matmul(h, dproj_d, mode="tn", tn=1536, tk=1024, name="dw_in_d")
    dw_m = _matmul(h, dq_mem, mode="tn", tn=512, tk=1024, name="dw_in_m")
    dw_g = _matmul(h, dproj_g, mode="tn", tn=1536, tk=1024, name="dw_in_g")
    kr0 = Q_RANK + MLA_NOPE
    G["w_in"] = jnp.concatenate([dw_a[:, :Q_RANK], dw_a[:, Q_RANK + LANES:], dw_a[:, kr0:kr0 + MLA_ROPE], dw_d, dw_m, dw_g], axis=1)
    grad_x, G["g_pre_mix"] = _norm1_bwd(dx1, dh, x, W["g_pre_mix"])
    return loss, grad_x, G


WEIGHTS = ["g_pre_mix", "w_in", "b_gate", "mla_q_norm", "w_uq", "mla_kv_norm", "w_ukv", "g_mem", "w_mem_kv", "w_br_mla",
           "w_br_dil", "w_br_mem", "w_o", "g_post_mix", "g_pre_ffn", "w_ffn_up", "conv_w", "conv_b", "w_ffn_down", "g_post_ffn"]
GROUPS = [
    [("w_in", (D_MODEL, D_IN), 1)],
    [("w_uq", (Q_RANK, MLA_HEADS * MLA_QK), 1)],
    [("w_ukv", (KV_RANK, MLA_HEADS * 2 * MLA_NOPE), 1), ("w_br_mla", (MLA_HEADS * MLA_NOPE, D_MODEL), 1),
     ("w_br_dil", (DIL_W, D_MODEL), 1), ("w_br_mem", (MEM_W, D_MODEL), 1)],
    [("w_mem_kv", (D_MODEL, 2 * MEM_W), 0), ("w_o", (D_MODEL, D_MODEL), 0), ("w_ffn_down", (D_FF, D_MODEL), 0)],
    [("w_ffn_up", (D_MODEL, 2 * D_FF), 1)],
]
CONV_W = ("conv_w", (3, 2 * D_FF), 1)
REPLICATED = [("g_pre_mix", D_MODEL), ("b_gate", 3 * D_MODEL), ("mla_q_norm", Q_RANK), ("mla_kv_norm", KV_RANK), ("g_mem", D_MODEL),
              ("g_post_mix", D_MODEL), ("g_pre_ffn", D_MODEL), ("conv_b", 2 * D_FF), ("g_post_ffn", D_MODEL)]
SMALL_ROWS = 256
CONV_AT = sum(n for _, n in REPLICATED)
LOSS_AT = CONV_AT + 3 * 2 * D_FF


def _shard_shape(shape, axis):
    return tuple(d // N_CHIPS if a == axis else d for a, d in enumerate(shape))


def _group_shape(grp):
    shapes = [_shard_shape(shape, axis) for _, shape, axis in grp]
    assert len({s[1] for s in shapes}) == 1
    return sum(s[0] for s in shapes), shapes[0][1]


def _member_shards(a, axis):
    r, c = a.shape
    if axis == 0:
        return a.reshape(N_CHIPS, r // N_CHIPS, c)
    return a.reshape(r, N_CHIPS, c // N_CHIPS).transpose(1, 0, 2)


def _member_full(s, axis):
    n, r, c = s.shape
    if axis == 0:
        return s.reshape(n * r, c)
    return s.transpose(1, 0, 2).reshape(r, n * c)


def _my_weight_groups(w):
    out = []
    for grp in GROUPS:
        rows, width = _group_shape(grp)
        out.append(jnp.concatenate([w[name].astype(BF16) for name, _, _ in grp], axis=0).reshape(2, rows // 2, width))
    return out


def _full_weights(gathered, conv_all):
    out = {}
    for grp, ga in zip(GROUPS, gathered):
        ga = ga.reshape(N_CHIPS, -1, ga.shape[-1])
        off = 0
        for name, shape, axis in grp:
            rows = _shard_shape(shape, axis)[0]
            out[name] = _member_full(ga[:, off:off + rows], axis)
            off += rows
    out[CONV_W[0]] = _member_full(conv_all, CONV_W[2])
    return out


def _grad_groups(G):
    out = []
    for grp in GROUPS:
        rows, width = _group_shape(grp)
        a = jnp.concatenate([_member_shards(G[name], axis) for name, _, axis in grp], axis=1)
        out.append(a.reshape(N_CHIPS, 2, rows // 2, width).transpose(1, 0, 2, 3))
    return out


def _pack_small(vals, conv_g=None, loss=None):
    parts = [vals[name].reshape(-1) for name, _ in REPLICATED]
    if conv_g is not None:
        parts += [conv_g.reshape(-1), loss.reshape(1)]
    flat = jnp.concatenate(parts)
    return jnp.pad(flat, (0, SMALL_ROWS * LANES - flat.shape[0])).reshape(SMALL_ROWS, LANES)


def _unpack_small(packed):
    flat = packed.reshape(-1)
    out, off = {}, 0
    for name, n in REPLICATED:
        out[name] = flat[off:off + n].reshape(1, n)
        off += n
    return out


MESH = pl.DeviceIdType.MESH
HBM_SPEC = pl.BlockSpec(memory_space=pltpu.HBM)


def _place():
    x, y, c = lax.axis_index("x"), lax.axis_index("y"), lax.axis_index("c")
    chips = [(1 - x, y), (x, 1 - y), (1 - x, 1 - y)]
    return x, y, c, chips


def _remote(src, dst, send_sems, recv_sems, k, to):
    return pltpu.make_async_remote_copy(src_ref=src, dst_ref=dst, send_sem=send_sems.at[k], recv_sem=recv_sems.at[k],
                                        device_id=to, device_id_type=MESH)


def _gather_weights(groups, conv_w):
    n = len(groups)

    def body(*refs):
        srcs, conv_src, outs, conv_out = refs[:n], refs[n], refs[n + 1:2 * n + 1], refs[2 * n + 1]
        send_sems, recv_sems, local_sems = refs[2 * n + 2:]
        x, y, c, chips = _place()
        me = 2 * x + y
        sibling = (x, y, 1 - c)
        local = [pltpu.make_async_copy(srcs[g], outs[g].at[me], local_sems.at[g]) for g in range(n)]
        local.append(pltpu.make_async_copy(conv_src, conv_out.at[me], local_sems.at[n]))
        for cp in local:
            cp.start()
        first = [_remote(srcs[g].at[c], outs[g].at[me, c], send_sems, recv_sems, g * 3 + k, (px, py, c))
                 for k, (px, py) in enumerate(chips) for g in range(n)]
        first += [_remote(conv_src, conv_out.at[me], send_sems, recv_sems, 6 * n + k, (px, py, c)) for k, (px, py) in enumerate(chips)]
        for cp in first:
            cp.start()
        passed = []
        for k, (px, py) in enumerate(chips):
            for g in range(n):
                slot = outs[g].at[2 * px + py, c]
                _remote(slot, slot, send_sems, recv_sems, g * 3 + k, (px, py, c)).wait_recv()
                cp = _remote(slot, slot, send_sems, recv_sems, 3 * n + g * 3 + k, sibling)
                cp.start()
                passed.append(cp)
        for k, (px, py) in enumerate(chips):
            slot = conv_out.at[2 * px + py]
            _remote(slot, slot, send_sems, recv_sems, 6 * n + k, (px, py, c)).wait_recv()
            for g in range(n):
                slot = outs[g].at[2 * px + py, 1 - c]
                _remote(slot, slot, send_sems, recv_sems, 3 * n + g * 3 + k, sibling).wait_recv()
        for cp in first + passed:
            cp.wait_send()
        for cp in local:
            cp.wait()

    n_sem = 6 * n + 3
    outs = pl.pallas_call(
        body, name="comm_gather_weights", in_specs=[HBM_SPEC] * (n + 1), out_specs=[HBM_SPEC] * (n + 1),
        out_shape=[jax.ShapeDtypeStruct((N_CHIPS,) + g.shape, g.dtype) for g in groups]
        + [jax.ShapeDtypeStruct((N_CHIPS,) + conv_w.shape, conv_w.dtype)],
        scratch_shapes=[pltpu.SemaphoreType.DMA((n_sem,)), pltpu.SemaphoreType.DMA((n_sem,)), pltpu.SemaphoreType.DMA((n + 1,))],
    )(*groups, conv_w)
    return outs[:n], outs[n]


def _swap_halves(groups):
    n = len(groups)

    def body(*refs):
        srcs, outs, send_sems, recv_sems = refs[:n], refs[n:2 * n], refs[2 * n], refs[2 * n + 1]
        x, y, c, _ = _place()
        cps = [_remote(srcs[g].at[1 - c], outs[g], send_sems, recv_sems, g, (x, y, 1 - c)) for g in range(n)]
        for cp in cps:
            cp.start()
        for cp in cps:
            cp.wait()

    return pl.pallas_call(
        body, name="comm_swap_halves", in_specs=[HBM_SPEC] * n, out_specs=[HBM_SPEC] * n,
        out_shape=[jax.ShapeDtypeStruct(g.shape[1:], g.dtype) for g in groups],
        scratch_shapes=[pltpu.SemaphoreType.DMA((n,)), pltpu.SemaphoreType.DMA((n,))],
    )(*groups)


def _scatter_partials(groups):
    n = len(groups)

    def body(*refs):
        srcs, outs = refs[:n], refs[n:2 * n]
        send_sems, recv_sems, local_sems = refs[2 * n:]
        x, y, c, chips = _place()
        me = 2 * x + y
        local = [pltpu.make_async_copy(srcs[g].at[me], outs[g].at[me], local_sems.at[g]) for g in range(n)]
        for cp in local:
            cp.start()
        sends = [_remote(srcs[g].at[2 * px + py], outs[g].at[me], send_sems, recv_sems, g * 3 + k, (px, py, c))
                 for k, (px, py) in enumerate(chips) for g in range(n)]
        for cp in sends:
            cp.start()
        for k, (px, py) in enumerate(chips):
            for g in range(n):
                slot = outs[g].at[2 * px + py]
                _remote(slot, slot, send_sems, recv_sems, g * 3 + k, (px, py, c)).wait_recv()
        for cp in sends:
            cp.wait_send()
        for cp in local:
            cp.wait()

    return pl.pallas_call(
        body, name="comm_scatter_partials", in_specs=[HBM_SPEC] * n, out_specs=[HBM_SPEC] * n,
        out_shape=[jax.ShapeDtypeStruct(g.shape, g.dtype) for g in groups],
        scratch_shapes=[pltpu.SemaphoreType.DMA((3 * n,)), pltpu.SemaphoreType.DMA((3 * n,)), pltpu.SemaphoreType.DMA((n,))],
    )(*groups)


def _share_reduced(groups):
    n = len(groups)

    def body(*refs):
        srcs, outs = refs[:n], refs[n:2 * n]
        send_sems, recv_sems, local_sems = refs[2 * n:]
        x, y, c, _ = _place()
        local = [pltpu.make_async_copy(srcs[g], outs[g].at[c], local_sems.at[g]) for g in range(n)]
        for cp in local:
            cp.start()
        sends = [_remote(srcs[g], outs[g].at[c], send_sems, recv_sems, g, (x, y, 1 - c)) for g in range(n)]
        for cp in sends:
            cp.start()
        for g in range(n):
            _remote(srcs[g], outs[g].at[1 - c], send_sems, recv_sems, g, (x, y, 1 - c)).wait_recv()
        for cp in sends:
            cp.wait_send()
        for cp in local:
            cp.wait()

    return pl.pallas_call(
        body, name="comm_share_reduced", in_specs=[HBM_SPEC] * n, out_specs=[HBM_SPEC] * n,
        out_shape=[jax.ShapeDtypeStruct((2,) + g.shape, g.dtype) for g in groups],
        scratch_shapes=[pltpu.SemaphoreType.DMA((n,)), pltpu.SemaphoreType.DMA((n,)), pltpu.SemaphoreType.DMA((n,))],
    )(*groups)


def _allreduce_small(v):
    n_dev = 2 * N_CHIPS

    def body(src, out, buf, send_sems, recv_sems):
        x, y, c, _ = _place()
        me = 4 * x + 2 * y + c
        buf[me] = src[...]
        flips = [(k >> 2 & 1, k >> 1 & 1, k & 1) for k in range(1, n_dev)]
        sends = []
        for k, (fx, fy, fc) in enumerate(flips):
            to = ((1 - x) if fx else x, (1 - y) if fy else y, (1 - c) if fc else c)
            cp = _remote(src, buf.at[me], send_sems, recv_sems, k, to)
            cp.start()
            sends.append((cp, to))
        for k, (cp, to) in enumerate(sends):
            slot = buf.at[4 * to[0] + 2 * to[1] + to[2]]
            _remote(slot, slot, send_sems, recv_sems, k, to).wait_recv()
        for cp, _ in sends:
            cp.wait_send()
        acc = buf[0]
        for d in range(1, n_dev):
            acc = acc + buf[d]
        out[...] = acc

    vm = pl.BlockSpec(memory_space=pltpu.VMEM)
    return pl.pallas_call(
        body, name="comm_allreduce_small", in_specs=[vm], out_specs=vm, out_shape=jax.ShapeDtypeStruct(v.shape, F32),
        scratch_shapes=[pltpu.VMEM((n_dev,) + v.shape, F32), pltpu.SemaphoreType.DMA((n_dev - 1,)),
                        pltpu.SemaphoreType.DMA((n_dev - 1,))],
    )(v)


def _row_tile(rows, cap=320):
    return max(t for t in range(16, cap + 1, 16) if rows % t == 0)


def _add_sibling(mine, theirs, core, name):
    _, n, R, C = mine.shape
    t = _row_tile(R)

    def body(core_ref, a_ref, b_ref, o_ref, ob_ref):
        tot = a_ref[...] + b_ref[...]
        o_ref[...] = tot
        ob_ref[...] = tot.astype(BF16)

    sp = pl.BlockSpec((None, t, C), lambda k, i, core_ref: (k, i, 0))
    grid_spec = pltpu.PrefetchScalarGridSpec(
        num_scalar_prefetch=1, grid=(n, R // t),
        in_specs=[pl.BlockSpec((None, None, t, C), lambda k, i, core_ref: (core_ref[0], k, i, 0)), sp], out_specs=[sp, sp])
    return pl.pallas_call(body, name=name, grid_spec=grid_spec,
                          out_shape=[jax.ShapeDtypeStruct((n, R, C), F32), jax.ShapeDtypeStruct((n, R, C), BF16)],
                          compiler_params=_params("parallel", "parallel"))(core, mine, theirs)


def _add_chips(received, own, chip, name):
    n, R, C = received.shape
    t = _row_tile(R)

    def body(chip_ref, r_ref, o_ref, out_ref):
        acc = None
        for k in range(n):
            term = jnp.where(chip_ref[0] == k, o_ref[...], r_ref[k].astype(F32))
            acc = term if acc is None else acc + term
        out_ref[...] = acc

    grid_spec = pltpu.PrefetchScalarGridSpec(
        num_scalar_prefetch=1, grid=(R // t,),
        in_specs=[pl.BlockSpec((n, t, C), lambda i, chip_ref: (0, i, 0)), pl.BlockSpec((None, t, C), lambda i, chip_ref: (chip_ref[0], i, 0))],
        out_specs=pl.BlockSpec((t, C), lambda i, chip_ref: (i, 0)))
    return pl.pallas_call(body, name=name, grid_spec=grid_spec, out_shape=jax.ShapeDtypeStruct((R, C), F32),
                          compiler_params=_params("parallel"))(chip, received, own)


def _adamw(w, g, m, v, name, g_row0=0):
    R, C = w.shape
    t = math.gcd(math.gcd(R, g_row0), 128) if R % 8 == 0 else R
    assert g_row0 % t == 0
    c1 = 1.0 - ADAM_B1 ** ADAM_STEP
    c2 = 1.0 - ADAM_B2 ** ADAM_STEP

    def body(w_ref, g_ref, m_ref, v_ref, go_ref, d_ref, nm_ref, nv_ref):
        gv = g_ref[...]
        nm = ADAM_B1 * m_ref[...] + (1.0 - ADAM_B1) * gv
        nv = ADAM_B2 * v_ref[...] + (1.0 - ADAM_B2) * (gv * gv)
        go_ref[...] = gv
        d_ref[...] = -ADAM_LR * ((nm / c1) / (jnp.sqrt(nv / c2) + ADAM_EPS) + ADAM_WD * w_ref[...])
        nm_ref[...] = nm
        nv_ref[...] = nv

    sp = pl.BlockSpec((t, C), lambda i: (i, 0))
    g_sp = pl.BlockSpec((t, C), lambda i: (i + g_row0 // t, 0))
    return pl.pallas_call(body, name=name, grid=(R // t,), in_specs=[sp, g_sp, sp, sp], out_specs=[sp] * 4,
                          out_shape=[jax.ShapeDtypeStruct((R, C), F32)] * 4, compiler_params=_params("parallel"))(w, g, m, v)


def kernel(x, mem, positions, g_pre_mix, w_in, b_gate, mla_q_norm, w_uq, mla_kv_norm, w_ukv, g_mem, w_mem_kv, w_br_mla, w_br_dil, w_br_mem, w_o, g_post_mix, g_pre_ffn, w_ffn_up, conv_w, conv_b, w_ffn_down, g_post_ffn, loss_target, m_g_pre_mix, m_w_in, m_b_gate, m_mla_q_norm, m_w_uq, m_mla_kv_norm, m_w_ukv, m_g_mem, m_w_mem_kv, m_w_br_mla, m_w_br_dil, m_w_br_mem, m_w_o, m_g_post_mix, m_g_pre_ffn, m_w_ffn_up, m_conv_w, m_conv_b, m_w_ffn_down, m_g_post_ffn, v_g_pre_mix, v_w_in, v_b_gate, v_mla_q_norm, v_w_uq, v_mla_kv_norm, v_w_ukv, v_g_mem, v_w_mem_kv, v_w_br_mla, v_w_br_dil, v_w_br_mem, v_w_o, v_g_post_mix, v_g_pre_ffn, v_w_ffn_up, v_conv_w, v_conv_b, v_w_ffn_down, v_g_post_ffn):
    w_args = (g_pre_mix, w_in, b_gate, mla_q_norm, w_uq, mla_kv_norm, w_ukv, g_mem, w_mem_kv, w_br_mla, w_br_dil, w_br_mem, w_o,
              g_post_mix, g_pre_ffn, w_ffn_up, conv_w, conv_b, w_ffn_down, g_post_ffn)
    m_args = (m_g_pre_mix, m_w_in, m_b_gate, m_mla_q_norm, m_w_uq, m_mla_kv_norm, m_w_ukv, m_g_mem, m_w_mem_kv, m_w_br_mla,
              m_w_br_dil, m_w_br_mem, m_w_o, m_g_post_mix, m_g_pre_ffn, m_w_ffn_up, m_conv_w, m_conv_b, m_w_ffn_down, m_g_post_ffn)
    v_args = (v_g_pre_mix, v_w_in, v_b_gate, v_mla_q_norm, v_w_uq, v_mla_kv_norm, v_w_ukv, v_g_mem, v_w_mem_kv, v_w_br_mla,
              v_w_br_dil, v_w_br_mem, v_w_o, v_g_post_mix, v_g_pre_ffn, v_w_ffn_up, v_conv_w, v_conv_b, v_w_ffn_down, v_g_post_ffn)
    sharded = {name for grp in GROUPS for name, _, _ in grp} | {CONV_W[0]}

    def local(a, name):
        return a[0] if name in sharded else a

    w = {n: local(a, n) for n, a in zip(WEIGHTS, w_args)}
    m = {n: local(a, n) for n, a in zip(WEIGHTS, m_args)}
    v = {n: local(a, n) for n, a in zip(WEIGHTS, v_args)}

    full = _full_weights(*_gather_weights(_my_weight_groups(w), w[CONV_W[0]]))
    full.update({name: w[name] for name, _ in REPLICATED})

    loss_local, grad_x, G = _local_step(x[0], mem[0], positions, loss_target[0], full)

    core = lax.axis_index("c").astype(jnp.int32).reshape(1)
    mine = _grad_groups(G)
    theirs = _swap_halves(mine)
    chip = (2 * lax.axis_index("x") + lax.axis_index("y")).astype(jnp.int32).reshape(1)
    partial = [_add_sibling(a, b, core, f"add_sibling_{i}") for i, (a, b) in enumerate(zip(mine, theirs))]
    received = _scatter_partials([p16 for _, p16 in partial])
    reduced = [_add_chips(r, p32, chip, f"add_chips_{i}") for i, (r, (p32, _)) in enumerate(zip(received, partial))]
    shard_groups = [a.reshape(-1, a.shape[-1]) for a in _share_reduced(reduced)]
    small = _allreduce_small(_pack_small(G, G[CONV_W[0]], loss_local))
    flat = small.reshape(-1)
    loss = flat[LOSS_AT]
    conv_g = flat[CONV_AT:LOSS_AT].reshape(CONV_W[1])
    conv_cols = CONV_W[1][1] // N_CHIPS
    conv_g = lax.dynamic_slice_in_dim(conv_g, (2 * lax.axis_index("x") + lax.axis_index("y")) * conv_cols, conv_cols, axis=1)

    grads, deltas, new_m, new_v = {}, {}, {}, {}
    for grp, g_all in zip(GROUPS, shard_groups):
        off = 0
        for name, _, _ in grp:
            grads[name], deltas[name], new_m[name], new_v[name] = _adamw(w[name], g_all, m[name], v[name], "adamw_" + name, off)
            off += w[name].shape[0]
    name = CONV_W[0]
    grads[name], deltas[name], new_m[name], new_v[name] = _adamw(w[name], conv_g, m[name], v[name], "adamw_" + name)
    packed = _adamw(_pack_small(w), small, _pack_small(m), _pack_small(v), "adamw_small")
    for dst, packed_small in zip((grads, deltas, new_m, new_v), packed):
        dst.update(_unpack_small(packed_small))

    def out(d, name):
        return d[name][None] if name in sharded else d[name]

    return (loss, grad_x[None], *[out(grads, n) for n in WEIGHTS], *[out(deltas, n) for n in WEIGHTS],
            *[out(new_m, n) for n in WEIGHTS], *[out(new_v, n) for n in WEIGHTS])
```

```python
import functools
import math

import jax
import jax.numpy as jnp
from jax import lax
from jax.experimental import pallas as pl
from jax.experimental.pallas import tpu as pltpu

F32 = jnp.float32
BF16 = jnp.bfloat16

D_MODEL = 1024
N_MEM = 256
RMS_EPS = 1e-6
NEG_INF = -1e30
MLA_HEADS = 8
MLA_NOPE = 64
MLA_ROPE = 32
MLA_QK = 96
Q_RANK = 384
KV_RANK = 256
ROPE_THETA = 10000.0
DIL_PAIRS = ((128, 1), (512, 4), (2048, 16))
DIL_GROUPS = 3
DIL_HPG = 4
DIL_HEADS = 12
DIL_W = 512
MEM_HEADS = 4
MEM_W = 512
D_FF = 2816
OFF_Q = 384
OFF_KV = 640
OFF_KR = 672
OFF_DIL = 5280
OFF_MEMQ = 5792
D_IN = 8864
ADAM_LR = 0.001
ADAM_B1 = 0.9
ADAM_B2 = 0.999
ADAM_EPS = 1e-08
ADAM_WD = 0.01
ADAM_STEP = 10

LANES = 128
VMEM_LIMIT = 56 * 1024 * 1024

N_CHIPS = 4
ROW_TILE = 256

NN = (((1,), (0,)), ((), ()))
NT = (((1,), (1,)), ((), ()))
TN = (((0,), (0,)), ((), ()))


def _params(*sem):
    return pltpu.CompilerParams(dimension_semantics=sem, vmem_limit_bytes=VMEM_LIMIT)


def _full(shape):
    return pl.BlockSpec(shape, lambda *_: (0,) * len(shape))


def _matmul(a, b, *, mode="nn", out_dtype=F32, tm=1024, tn=1024, tk=None, add=None, scale=None, name):
    if mode == "nn":
        (M, K), N = a.shape, b.shape[1]
    elif mode == "nt":
        (M, K), N = a.shape, b.shape[0]
    else:
        (K, M), N = a.shape, b.shape[1]
    tm, tn = min(tm, M), min(tn, N)
    tk = K if tk is None else min(tk, K)
    assert M % tm == 0 and N % tn == 0 and K % tk == 0, (name, M, N, K, tm, tn, tk)
    nk = K // tk
    dims = {"nn": NN, "nt": NT, "tn": TN}[mode]
    a_spec = pl.BlockSpec((tk, tm), lambda i, j, k: (k, i)) if mode == "tn" else pl.BlockSpec((tm, tk), lambda i, j, k: (i, k))
    b_spec = pl.BlockSpec((tn, tk), lambda i, j, k: (j, k)) if mode == "nt" else pl.BlockSpec((tk, tn), lambda i, j, k: (k, j))
    o_spec = pl.BlockSpec((tm, tn), lambda i, j, k: (i, j))
    has_add = add is not None

    def body(*refs):
        a_ref, b_ref = refs[0], refs[1]
        c_ref = refs[2] if has_add else None
        o_ref = refs[3] if has_add else refs[2]
        part = lax.dot_general(a_ref[...].astype(BF16), b_ref[...].astype(BF16), dims, preferred_element_type=F32)
        if scale is not None:
            assert nk == 1 and not has_add
            part = part * jnp.where(pl.program_id(1) < scale[0], scale[1], 1.0)
        if nk == 1:
            if has_add:
                part = part + c_ref[...]
            o_ref[...] = part.astype(out_dtype)
        else:
            acc = refs[-1]
            k = pl.program_id(2)

            @pl.when(k == 0)
            def _():
                acc[...] = part

            @pl.when(k > 0)
            def _():
                acc[...] += part

            @pl.when(k == nk - 1)
            def _():
                r = acc[...]
                if has_add:
                    r = r + c_ref[...]
                o_ref[...] = r.astype(out_dtype)

    in_specs = [a_spec, b_spec] + ([o_spec] if has_add else [])
    args = (a, b) + ((add,) if has_add else ())
    return pl.pallas_call(
        body, name=name, grid=(M // tm, N // tn, nk), in_specs=in_specs, out_specs=o_spec,
        out_shape=jax.ShapeDtypeStruct((M, N), out_dtype),
        scratch_shapes=[pltpu.VMEM((tm, tn), F32)] if nk > 1 else [],
        compiler_params=_params("parallel", "parallel", "arbitrary"),
    )(*args)


def _rms_fwd_val(x, g):
    r = lax.rsqrt(jnp.mean(x * x, axis=-1, keepdims=True) + RMS_EPS)
    return (x * r) * g


def _rms_bwd_val(dy, x, g):
    r = lax.rsqrt(jnp.mean(x * x, axis=-1, keepdims=True) + RMS_EPS)
    xn = x * r
    gdy = g * dy
    dx = r * (gdy - xn * jnp.mean(gdy * xn, axis=-1, keepdims=True))
    return dx, dy * xn


def _rope_tables(pos, invf, inverse):
    ang = pos * invf
    cos, sin = jnp.cos(ang), jnp.sin(ang)
    lane = lax.broadcasted_iota(jnp.int32, ang.shape, 1)
    first = (lane >= MLA_NOPE) & (lane < MLA_NOPE + MLA_ROPE // 2)
    second = (lane >= MLA_NOPE + MLA_ROPE // 2) & (lane < MLA_QK)
    sgn = -1.0 if inverse else 1.0
    sa = jnp.where(first, -sgn * sin, 0.0)
    sb = jnp.where(second, sgn * sin, 0.0)
    return cos, sa, sb


def _rope_val(x, cos, sa, sb):
    half = MLA_ROPE // 2
    return x * cos + pltpu.roll(x, LANES - half, 1) * sa + pltpu.roll(x, half, 1) * sb


def _head_sum_bcast(v, n_heads):
    parts = []
    for h in range(n_heads):
        s = jnp.sum(v[:, h * LANES:(h + 1) * LANES], axis=1, keepdims=True)
        parts.append(jnp.broadcast_to(s, (v.shape[0], LANES)))
    return parts


def _row_spec(t, w):
    return pl.BlockSpec((t, w), lambda i: (i, 0))


def _acc_spec(w, rows=1):
    return pl.BlockSpec((rows, w), lambda i: (0, 0))


def _rmsnorm(x, g, out_dtype, name):
    S, W = x.shape
    t = min(ROW_TILE, S)

    def body(x_ref, g_ref, o_ref):
        o_ref[...] = _rms_fwd_val(x_ref[...], g_ref[...]).astype(out_dtype)

    return pl.pallas_call(body, name=name, grid=(S // t,), in_specs=[_row_spec(t, W), _acc_spec(W)],
                          out_specs=_row_spec(t, W), out_shape=jax.ShapeDtypeStruct((S, W), out_dtype),
                          compiler_params=_params("parallel"))(x, g)


def _mla_prep(proj_a, pos, invf, q_norm, kv_norm):
    S = proj_a.shape[0]
    t = ROW_TILE

    def body(a_ref, pos_ref, invf_ref, qn_ref, kvn_ref, cq_ref, ckv_ref, kpe_ref):
        a = a_ref[...]
        cq_ref[...] = _rms_fwd_val(a[:, 0:Q_RANK], qn_ref[...]).astype(BF16)
        ckv_ref[...] = _rms_fwd_val(a[:, Q_RANK + LANES:], kvn_ref[...]).astype(BF16)
        cos, sa, sb = _rope_tables(pos_ref[...], invf_ref[...], False)
        kpe_ref[...] = _rope_val(a[:, Q_RANK:Q_RANK + LANES], cos, sa, sb)

    return pl.pallas_call(
        body, name="mla_prep", grid=(S // t,),
        in_specs=[_row_spec(t, 768), _row_spec(t, 1), _acc_spec(LANES), _acc_spec(Q_RANK), _acc_spec(KV_RANK)],
        out_specs=[_row_spec(t, Q_RANK), _row_spec(t, KV_RANK), _row_spec(t, LANES)],
        out_shape=[jax.ShapeDtypeStruct((S, Q_RANK), BF16), jax.ShapeDtypeStruct((S, KV_RANK), BF16),
                   jax.ShapeDtypeStruct((S, LANES), F32)],
        compiler_params=_params("parallel"))(proj_a, pos, invf, q_norm, kv_norm)


def _qk_final(q_pre, k_pre, kpe, pos, invf):
    S, W = q_pre.shape
    t = ROW_TILE
    scale = MLA_QK ** -0.5

    def body(q_ref, k_ref, kpe_ref, pos_ref, invf_ref, qo_ref, ko_ref):
        cos, sa, sb = _rope_tables(pos_ref[...], invf_ref[...], False)
        kpe_v = kpe_ref[...]
        for h in range(MLA_HEADS):
            sl = slice(h * LANES, (h + 1) * LANES)
            qo_ref[:, sl] = (_rope_val(q_ref[:, sl], cos, sa, sb) * scale).astype(BF16)
            ko_ref[:, sl] = (k_ref[:, sl] + kpe_v).astype(BF16)

    return pl.pallas_call(
        body, name="qk_final", grid=(S // t,),
        in_specs=[_row_spec(t, W), _row_spec(t, W), _row_spec(t, LANES), _row_spec(t, 1), _acc_spec(LANES)],
        out_specs=[_row_spec(t, W), _row_spec(t, W)],
        out_shape=[jax.ShapeDtypeStruct((S, W), BF16)] * 2,
        compiler_params=_params("parallel"))(q_pre, k_pre, kpe, pos, invf)


def _mla_bwd_prep(dq, dk, pos, invf):
    S, W = dq.shape
    t = ROW_TILE
    scale = MLA_QK ** -0.5

    def body(dq_ref, dk_ref, pos_ref, invf_ref, dqp_ref, dkpe_ref):
        cos, sa, sb = _rope_tables(pos_ref[...], invf_ref[...], True)
        tot = jnp.zeros((t, LANES), F32)
        for h in range(MLA_HEADS):
            sl = slice(h * LANES, (h + 1) * LANES)
            dqp_ref[:, sl] = _rope_val(dq_ref[:, sl] * scale, cos, sa, sb).astype(BF16)
            tot = tot + dk_ref[:, sl]
        lane = lax.broadcasted_iota(jnp.int32, tot.shape, 1)
        tot = jnp.where((lane >= MLA_NOPE) & (lane < MLA_QK), tot, 0.0)
        dkpe_ref[...] = _rope_val(tot, cos, sa, sb)

    return pl.pallas_call(
        body, name="mla_bwd_prep", grid=(S // t,),
        in_specs=[_row_spec(t, W), _row_spec(t, W), _row_spec(t, 1), _acc_spec(LANES)],
        out_specs=[_row_spec(t, W), _row_spec(t, LANES)],
        out_shape=[jax.ShapeDtypeStruct((S, W), BF16), jax.ShapeDtypeStruct((S, LANES), F32)],
        compiler_params=_params("parallel"))(dq, dk, pos, invf)


def _mla_norm_bwd(dcq, dckv_a, dckv_b, dkpe, proj_a, q_norm, kv_norm):
    S = proj_a.shape[0]
    t = ROW_TILE

    def body(dcq_ref, da_ref, db_ref, dkpe_ref, a_ref, qn_ref, kvn_ref, o_ref, dqn_ref, dkvn_ref):
        i = pl.program_id(0)
        a = a_ref[...]
        dxq, gq = _rms_bwd_val(dcq_ref[...], a[:, 0:Q_RANK], qn_ref[...])
        dxkv, gkv = _rms_bwd_val(da_ref[...] + db_ref[...], a[:, Q_RANK + LANES:], kvn_ref[...])
        o_ref[:, 0:Q_RANK] = dxq.astype(BF16)
        o_ref[:, Q_RANK:Q_RANK + LANES] = dkpe_ref[...].astype(BF16)
        o_ref[:, Q_RANK + LANES:] = dxkv.astype(BF16)

        @pl.when(i == 0)
        def _():
            dqn_ref[...] = jnp.zeros_like(dqn_ref)
            dkvn_ref[...] = jnp.zeros_like(dkvn_ref)

        dqn_ref[...] += jnp.sum(gq, axis=0, keepdims=True)
        dkvn_ref[...] += jnp.sum(gkv, axis=0, keepdims=True)

    return pl.pallas_call(
        body, name="mla_norm_bwd", grid=(S // t,),
        in_specs=[_row_spec(t, Q_RANK), _row_spec(t, KV_RANK), _row_spec(t, KV_RANK), _row_spec(t, LANES),
                  _row_spec(t, 768), _acc_spec(Q_RANK), _acc_spec(KV_RANK)],
        out_specs=[_row_spec(t, 768), _acc_spec(Q_RANK), _acc_spec(KV_RANK)],
        out_shape=[jax.ShapeDtypeStruct((S, 768), BF16), jax.ShapeDtypeStruct((1, Q_RANK), F32),
                   jax.ShapeDtypeStruct((1, KV_RANK), F32)],
        compiler_params=_params("arbitrary"))(dcq, dckv_a, dckv_b, dkpe, proj_a, q_norm, kv_norm)


def _dil_mix(o_list, lse_list):
    S = o_list[0].shape[0]
    t = ROW_TILE

    def body(o0, o1, o2, l0, l1, l2, y_ref):
        ls = [l0[...], l1[...], l2[...]]
        m = jnp.maximum(jnp.maximum(ls[0], ls[1]), ls[2])
        es = [jnp.exp(l - m) for l in ls]
        den = es[0] + es[1] + es[2]
        y = (es[0] / den) * o0[...] + (es[1] / den) * o1[...] + (es[2] / den) * o2[...]
        y_ref[...] = y.astype(BF16)

    return pl.pallas_call(
        body, name="dil_mix", grid=(S // t,), in_specs=[_row_spec(t, DIL_W)] * 6, out_specs=_row_spec(t, DIL_W),
        out_shape=jax.ShapeDtypeStruct((S, DIL_W), BF16), compiler_params=_params("parallel"))(*o_list, *lse_list)


def _dil_mix_bwd(dy, o_list, lse_list):
    S = dy.shape[0]
    t = ROW_TILE

    def body(dy_ref, o0, o1, o2, l0, l1, l2, d0, d1, d2, e0, e1, e2):
        ls = [l0[...], l1[...], l2[...]]
        os_ = [o0[...], o1[...], o2[...]]
        m = jnp.maximum(jnp.maximum(ls[0], ls[1]), ls[2])
        es = [jnp.exp(l - m) for l in ls]
        den = es[0] + es[1] + es[2]
        ws = [e / den for e in es]
        dyv = dy_ref[...]
        y = ws[0] * os_[0] + ws[1] * os_[1] + ws[2] * os_[2]
        b = jnp.concatenate(_head_sum_bcast(dyv * y, DIL_HPG), axis=1)
        for w, d_ref, e_ref in zip(ws, (d0, d1, d2), (e0, e1, e2)):
            d_ref[...] = (w * dyv).astype(BF16)
            e_ref[...] = w * b

    return pl.pallas_call(
        body, name="dil_mix_bwd", grid=(S // t,), in_specs=[_row_spec(t, DIL_W)] * 7, out_specs=[_row_spec(t, DIL_W)] * 6,
        out_shape=[jax.ShapeDtypeStruct((S, DIL_W), BF16)] * 3 + [jax.ShapeDtypeStruct((S, DIL_W), F32)] * 3,
        compiler_params=_params("parallel"))(dy, *o_list, *lse_list)


def _delta(do, o, n_heads, name):
    S, W = do.shape
    t = ROW_TILE

    def body(do_ref, o_ref, d_ref):
        prod = do_ref[...].astype(F32) * o_ref[...].astype(F32)
        d_ref[...] = jnp.concatenate(_head_sum_bcast(prod, n_heads), axis=1)

    return pl.pallas_call(body, name=name, grid=(S // t,), in_specs=[_row_spec(t, W)] * 2, out_specs=_row_spec(t, W),
                          out_shape=jax.ShapeDtypeStruct((S, W), F32), compiler_params=_params("parallel"))(do, o)


def _merge(proj_g, b_gate, b_list):
    S = proj_g.shape[0]
    t = ROW_TILE

    def body(g_ref, b_ref, y0, y1, y2, o_ref):
        acc = jnp.zeros((t, D_MODEL), F32)
        for i, y in enumerate((y0, y1, y2)):
            sl = slice(i * D_MODEL, (i + 1) * D_MODEL)
            acc = acc + jax.nn.sigmoid(g_ref[:, sl] + b_ref[:, sl]) * y[...]
        o_ref[...] = acc.astype(BF16)

    return pl.pallas_call(
        body, name="merge", grid=(S // t,),
        in_specs=[_row_spec(t, 3 * D_MODEL), _acc_spec(3 * D_MODEL)] + [_row_spec(t, D_MODEL)] * 3,
        out_specs=_row_spec(t, D_MODEL), out_shape=jax.ShapeDtypeStruct((S, D_MODEL), BF16),
        compiler_params=_params("parallel"))(proj_g, b_gate, *b_list)


def _merge_bwd(dmerged, proj_g, b_gate, b_list):
    S = proj_g.shape[0]
    t = ROW_TILE

    def body(dm_ref, g_ref, b_ref, y0, y1, y2, d0, d1, d2, dz_ref, db_ref):
        i = pl.program_id(0)

        @pl.when(i == 0)
        def _():
            db_ref[...] = jnp.zeros_like(db_ref)

        dm = dm_ref[...]
        for k, (y, d_ref) in enumerate(zip((y0, y1, y2), (d0, d1, d2))):
            sl = slice(k * D_MODEL, (k + 1) * D_MODEL)
            s = jax.nn.sigmoid(g_ref[:, sl] + b_ref[:, sl])
            d_ref[...] = (s * dm).astype(BF16)
            dz = dm * y[...] * (s * (1.0 - s))
            dz_ref[:, sl] = dz.astype(BF16)
            db_ref[:, sl] += jnp.sum(dz, axis=0, keepdims=True)

    return pl.pallas_call(
        body, name="merge_bwd", grid=(S // t,),
        in_specs=[_row_spec(t, D_MODEL), _row_spec(t, 3 * D_MODEL), _acc_spec(3 * D_MODEL)] + [_row_spec(t, D_MODEL)] * 3,
        out_specs=[_row_spec(t, D_MODEL)] * 3 + [_row_spec(t, 3 * D_MODEL), _acc_spec(3 * D_MODEL)],
        out_shape=[jax.ShapeDtypeStruct((S, D_MODEL), BF16)] * 3
        + [jax.ShapeDtypeStruct((S, 3 * D_MODEL), BF16), jax.ShapeDtypeStruct((1, 3 * D_MODEL), F32)],
        compiler_params=_params("arbitrary"))(dmerged, proj_g, b_gate, *b_list)


def _norm2(o, x, g_post, g_pre):
    S = x.shape[0]
    t = ROW_TILE

    def body(o_ref, x_ref, gp_ref, gf_ref, x1_ref, h2_ref):
        x1 = x_ref[...] + _rms_fwd_val(o_ref[...], gp_ref[...])
        x1_ref[...] = x1
        h2_ref[...] = _rms_fwd_val(x1, gf_ref[...]).astype(BF16)

    return pl.pallas_call(
        body, name="norm2", grid=(S // t,),
        in_specs=[_row_spec(t, D_MODEL)] * 2 + [_acc_spec(D_MODEL)] * 2, out_specs=[_row_spec(t, D_MODEL)] * 2,
        out_shape=[jax.ShapeDtypeStruct((S, D_MODEL), F32), jax.ShapeDtypeStruct((S, D_MODEL), BF16)],
        compiler_params=_params("parallel"))(o, x, g_post, g_pre)


def _norm2_bwd(dx2, dh2, x1, o, g_pre, g_post):
    S = x1.shape[0]
    t = ROW_TILE

    def body(dx2_ref, dh2_ref, x1_ref, o_ref, gf_ref, gp_ref, dx1_ref, do_ref, dgf_ref, dgp_ref):
        i = pl.program_id(0)

        @pl.when(i == 0)
        def _():
            dgf_ref[...] = jnp.zeros_like(dgf_ref)
            dgp_ref[...] = jnp.zeros_like(dgp_ref)

        d1, gf = _rms_bwd_val(dh2_ref[...], x1_ref[...], gf_ref[...])
        dx1 = dx2_ref[...] + d1
        dx1_ref[...] = dx1
        do, gp = _rms_bwd_val(dx1, o_ref[...], gp_ref[...])
        do_ref[...] = do.astype(BF16)
        dgf_ref[...] += jnp.sum(gf, axis=0, keepdims=True)
        dgp_ref[...] += jnp.sum(gp, axis=0, keepdims=True)

    return pl.pallas_call(
        body, name="norm2_bwd", grid=(S // t,),
        in_specs=[_row_spec(t, D_MODEL)] * 4 + [_acc_spec(D_MODEL)] * 2,
        out_specs=[_row_spec(t, D_MODEL)] * 2 + [_acc_spec(D_MODEL)] * 2,
        out_shape=[jax.ShapeDtypeStruct((S, D_MODEL), F32), jax.ShapeDtypeStruct((S, D_MODEL), BF16),
                   jax.ShapeDtypeStruct((1, D_MODEL), F32), jax.ShapeDtypeStruct((1, D_MODEL), F32)],
        compiler_params=_params("arbitrary"))(dx2, dh2, x1, o, g_pre, g_post)


def _norm1_bwd(dx1, dh, x, g):
    S = x.shape[0]
    t = ROW_TILE

    def body(dx1_ref, dh_ref, x_ref, g_ref, dx_ref, dg_ref):
        i = pl.program_id(0)

        @pl.when(i == 0)
        def _():
            dg_ref[...] = jnp.zeros_like(dg_ref)

        d, gg = _rms_bwd_val(dh_ref[...], x_ref[...], g_ref[...])
        dx_ref[...] = dx1_ref[...] + d
        dg_ref[...] += jnp.sum(gg, axis=0, keepdims=True)

    return pl.pallas_call(
        body, name="norm1_bwd", grid=(S // t,),
        in_specs=[_row_spec(t, D_MODEL)] * 3 + [_acc_spec(D_MODEL)], out_specs=[_row_spec(t, D_MODEL), _acc_spec(D_MODEL)],
        out_shape=[jax.ShapeDtypeStruct((S, D_MODEL), F32), jax.ShapeDtypeStruct((1, D_MODEL), F32)],
        compiler_params=_params("arbitrary"))(dx1, dh, x, g)


def _gain_grad(dy, x, name):
    R, W = x.shape

    def body(dy_ref, x_ref, dg_ref):
        xv = x_ref[...]
        r = lax.rsqrt(jnp.mean(xv * xv, axis=-1, keepdims=True) + RMS_EPS)
        dg_ref[...] = jnp.sum(dy_ref[...] * (xv * r), axis=0, keepdims=True)

    return pl.pallas_call(body, name=name, grid=(1,), in_specs=[_full((R, W))] * 2, out_specs=_full((1, W)),
                          out_shape=jax.ShapeDtypeStruct((1, W), F32), compiler_params=_params("arbitrary"))(dy, x)


def _loss_head(f, x1, tgt, g):
    S = f.shape[0]
    t = ROW_TILE

    def body(f_ref, x1_ref, t_ref, g_ref, loss_ref, dx2_ref, df_ref, dg_ref):
        i = pl.program_id(0)

        @pl.when(i == 0)
        def _():
            loss_ref[...] = jnp.zeros_like(loss_ref)
            dg_ref[...] = jnp.zeros_like(dg_ref)

        fv, gv = f_ref[...], g_ref[...]
        err = x1_ref[...] + _rms_fwd_val(fv, gv) - t_ref[...]
        part = jnp.sum(jnp.mean(err * err, axis=-1, keepdims=True), axis=0, keepdims=True)
        loss_ref[...] += jnp.broadcast_to(0.5 * part, loss_ref.shape)
        dx2 = err * (1.0 / D_MODEL)
        dx2_ref[...] = dx2
        df, gg = _rms_bwd_val(dx2, fv, gv)
        df_ref[...] = df.astype(BF16)
        dg_ref[...] += jnp.sum(gg, axis=0, keepdims=True)

    return pl.pallas_call(
        body, name="loss_head", grid=(S // t,),
        in_specs=[_row_spec(t, D_MODEL)] * 3 + [_acc_spec(D_MODEL)],
        out_specs=[_acc_spec(LANES, 8), _row_spec(t, D_MODEL), _row_spec(t, D_MODEL), _acc_spec(D_MODEL)],
        out_shape=[jax.ShapeDtypeStruct((8, LANES), F32), jax.ShapeDtypeStruct((S, D_MODEL), F32),
                   jax.ShapeDtypeStruct((S, D_MODEL), BF16), jax.ShapeDtypeStruct((1, D_MODEL), F32)],
        compiler_params=_params("arbitrary"))(f, x1, tgt, g)


CONV_TC = 1408
CONV_TT = 256
HALO = 8


def _shift_down(u, halo, first):
    row = lax.broadcasted_iota(jnp.int32, u.shape, 0)
    h6 = jnp.where(first, 0.0, halo[HALO - 2:HALO - 1, :])
    h7 = jnp.where(first, 0.0, halo[HALO - 1:HALO, :])
    s1 = jnp.where(row == 0, h7, pltpu.roll(u, 1, 0))
    s2 = jnp.where(row == 0, h6, jnp.where(row == 1, h7, pltpu.roll(u, 2, 0)))
    return s1, s2


def _conv_specs(tt, n_c, n_t, lead):
    def halo_row(i):
        return jnp.maximum(i * (tt // HALO) - 1, 0) if lead else jnp.minimum((i + 1) * (tt // HALO), n_t * (tt // HALO) - 1)
    return [
        pl.BlockSpec((tt, CONV_TC), lambda j, i: (i, j)),
        pl.BlockSpec((tt, CONV_TC), lambda j, i: (i, j + n_c)),
        pl.BlockSpec((HALO, CONV_TC), lambda j, i: (halo_row(i), j)),
        pl.BlockSpec((HALO, CONV_TC), lambda j, i: (halo_row(i), j + n_c)),
    ]


def _conv_z(ug, uv, hg, hv, w_g, w_v, b_g, b_v, first):
    g1, g2 = _shift_down(ug, hg, first)
    v1, v2 = _shift_down(uv, hv, first)
    zg = b_g + w_g[0:1, :] * g2
    zg = zg + w_g[1:2, :] * g1
    zg = zg + w_g[2:3, :] * ug
    zv = b_v + w_v[0:1, :] * v2
    zv = zv + w_v[1:2, :] * v1
    zv = zv + w_v[2:3, :] * uv
    return zg, zv, (g2, g1, ug), (v2, v1, uv)


def _conv_fwd(u, conv_w, conv_b):
    S = u.shape[0]
    tt = min(CONV_TT, S)
    n_c, n_t = D_FF // CONV_TC, S // tt
    wspec = [pl.BlockSpec((3, CONV_TC), lambda j, i: (0, j)), pl.BlockSpec((3, CONV_TC), lambda j, i: (0, j + n_c)),
             pl.BlockSpec((1, CONV_TC), lambda j, i: (0, j)), pl.BlockSpec((1, CONV_TC), lambda j, i: (0, j + n_c))]

    def body(ug_ref, uv_ref, hg_ref, hv_ref, wg_ref, wv_ref, bg_ref, bv_ref, a_ref):
        first = pl.program_id(1) == 0
        zg, zv, _, _ = _conv_z(ug_ref[...], uv_ref[...], hg_ref, hv_ref, wg_ref, wv_ref, bg_ref[...], bv_ref[...], first)
        a_ref[...] = (zg * jax.nn.sigmoid(zg) * zv).astype(BF16)

    return pl.pallas_call(
        body, name="conv_fwd", grid=(n_c, n_t), in_specs=_conv_specs(tt, n_c, n_t, True) + wspec,
        out_specs=pl.BlockSpec((tt, CONV_TC), lambda j, i: (i, j)), out_shape=jax.ShapeDtypeStruct((S, D_FF), BF16),
        compiler_params=_params("parallel", "parallel"))(u, u, u, u, conv_w, conv_w, conv_b, conv_b)


def _conv_bwd_dz(da, u, conv_w, conv_b):
    S = u.shape[0]
    tt = min(CONV_TT, S)
    n_c, n_t = D_FF // CONV_TC, S // tt
    wspec = [pl.BlockSpec((3, CONV_TC), lambda j, i: (0, j)), pl.BlockSpec((3, CONV_TC), lambda j, i: (0, j + n_c)),
             pl.BlockSpec((1, CONV_TC), lambda j, i: (0, j)), pl.BlockSpec((1, CONV_TC), lambda j, i: (0, j + n_c))]
    tile = pl.BlockSpec((tt, CONV_TC), lambda j, i: (i, j))
    tile_v = pl.BlockSpec((tt, CONV_TC), lambda j, i: (i, j + n_c))

    def body(da_ref, ug_ref, uv_ref, hg_ref, hv_ref, wg_ref, wv_ref, bg_ref, bv_ref,
             dzg_ref, dzv_ref, dwg_ref, dwv_ref, dbg_ref, dbv_ref):
        i = pl.program_id(1)
        zg, zv, gs, vs = _conv_z(ug_ref[...], uv_ref[...], hg_ref, hv_ref, wg_ref, wv_ref, bg_ref[...], bv_ref[...], i == 0)
        dav = da_ref[...]
        sg = jax.nn.sigmoid(zg)
        dzv = dav * (zg * sg)
        dzg = dav * zv * (sg * (1.0 + zg * (1.0 - sg)))
        dzg_ref[...] = dzg
        dzv_ref[...] = dzv

        @pl.when(i == 0)
        def _():
            for r in (dwg_ref, dwv_ref, dbg_ref, dbv_ref):
                r[...] = jnp.zeros_like(r)

        for k in range(3):
            dwg_ref[k:k + 1, :] += jnp.sum(dzg * gs[k], axis=0, keepdims=True)
            dwv_ref[k:k + 1, :] += jnp.sum(dzv * vs[k], axis=0, keepdims=True)
        dbg_ref[...] += jnp.sum(dzg, axis=0, keepdims=True)
        dbv_ref[...] += jnp.sum(dzv, axis=0, keepdims=True)

    outs = pl.pallas_call(
        body, name="conv_bwd_dz", grid=(n_c, n_t), in_specs=[tile] + _conv_specs(tt, n_c, n_t, True) + wspec,
        out_specs=[tile, tile] + [pl.BlockSpec((3, CONV_TC), lambda j, i: (0, j))] * 2 + [pl.BlockSpec((1, CONV_TC), lambda j, i: (0, j))] * 2,
        out_shape=[jax.ShapeDtypeStruct((S, D_FF), F32)] * 2 + [jax.ShapeDtypeStruct((3, D_FF), F32)] * 2
        + [jax.ShapeDtypeStruct((1, D_FF), F32)] * 2,
        compiler_params=_params("parallel", "arbitrary"))(da, u, u, u, u, conv_w, conv_w, conv_b, conv_b)
    dzg, dzv, dwg, dwv, dbg, dbv = outs
    return dzg, dzv, jnp.concatenate([dwg, dwv], axis=1), jnp.concatenate([dbg, dbv], axis=1)


def _conv_bwd_du(dzg, dzv, conv_w):
    S = dzg.shape[0]
    tt = min(CONV_TT, S)
    n_c, n_t = D_FF // CONV_TC, S // tt
    steps = tt // HALO

    def nxt(i):
        return jnp.minimum((i + 1) * steps, n_t * steps - 1)

    tile_g = pl.BlockSpec((tt, CONV_TC), lambda h, j, i: (i * (1 - h), j * (1 - h)))
    tile_v = pl.BlockSpec((tt, CONV_TC), lambda h, j, i: (i * h, j * h))
    halo_g = pl.BlockSpec((HALO, CONV_TC), lambda h, j, i: (nxt(i) * (1 - h), j * (1 - h)))
    halo_v = pl.BlockSpec((HALO, CONV_TC), lambda h, j, i: (nxt(i) * h, j * h))
    wsp = pl.BlockSpec((3, CONV_TC), lambda h, j, i: (0, j + h * n_c))

    def body(zg_ref, zv_ref, hg_ref, hv_ref, w_ref, du_ref):
        h = pl.program_id(0)
        last = pl.program_id(2) == n_t - 1
        z = jnp.where(h == 0, zg_ref[...], zv_ref[...])
        hal = jnp.where(h == 0, hg_ref[...], hv_ref[...])
        row = lax.broadcasted_iota(jnp.int32, z.shape, 0)
        h0 = jnp.where(last, 0.0, hal[0:1, :])
        h1 = jnp.where(last, 0.0, hal[1:2, :])
        u1 = jnp.where(row == tt - 1, h0, pltpu.roll(z, tt - 1, 0))
        u2 = jnp.where(row == tt - 1, h1, jnp.where(row == tt - 2, h0, pltpu.roll(z, tt - 2, 0)))
        du = w_ref[2:3, :] * z + w_ref[1:2, :] * u1 + w_ref[0:1, :] * u2
        du_ref[...] = du.astype(BF16)

    return pl.pallas_call(
        body, name="conv_bwd_du", grid=(2, n_c, n_t), in_specs=[tile_g, tile_v, halo_g, halo_v, wsp],
        out_specs=pl.BlockSpec((tt, CONV_TC), lambda h, j, i: (i, j + h * n_c)),
        out_shape=jax.ShapeDtypeStruct((S, 2 * D_FF), BF16),
        compiler_params=_params("parallel", "parallel", "parallel"))(dzg, dzv, dzg, dzv, conv_w)


BAND = 128


MEM_TQ = 512


def _mem_fwd(q, kv, *, name):
    S, W = q.shape
    M = kv.shape[0]
    nh = W // LANES
    tq = min(MEM_TQ, S)

    def body(q_ref, k_ref, v_ref, o_ref, l_ref):
        s = lax.dot_general(q_ref[...], k_ref[...].astype(BF16), NT, preferred_element_type=F32)
        m = jnp.max(s, axis=1, keepdims=True)
        p = jnp.exp(s - m)
        l = jnp.sum(p, axis=1, keepdims=True)
        o_ref[...] = (lax.dot_general(p.astype(BF16), v_ref[...].astype(BF16), NN, preferred_element_type=F32) / l).astype(BF16)
        l_ref[...] = jnp.broadcast_to(m + jnp.log(l), (tq, LANES))

    blk = pl.BlockSpec((tq, LANES), lambda hh, i: (i, hh))
    return pl.pallas_call(
        body, name=name, grid=(nh, S // tq),
        in_specs=[blk, pl.BlockSpec((M, LANES), lambda hh, i: (0, hh)), pl.BlockSpec((M, LANES), lambda hh, i: (0, hh + nh))],
        out_specs=[blk, blk], out_shape=[jax.ShapeDtypeStruct((S, W), BF16), jax.ShapeDtypeStruct((S, W), F32)],
        compiler_params=_params("parallel", "parallel"))(q, kv, kv)


def _mem_bwd(q, kv, do, lse, delta, *, scale, name):
    S, W = q.shape
    M = kv.shape[0]
    nh = W // LANES
    tq = min(MEM_TQ, S)

    def body(q_ref, k_ref, v_ref, do_ref, l_ref, d_ref, dq_ref, dk_ref, dv_ref):
        i = pl.program_id(1)

        @pl.when(i == 0)
        def _():
            dk_ref[...] = jnp.zeros_like(dk_ref)
            dv_ref[...] = jnp.zeros_like(dv_ref)

        qv = q_ref[...]
        kv_, vv = k_ref[...].astype(BF16), v_ref[...].astype(BF16)
        dov = do_ref[...].astype(BF16)
        s = lax.dot_general(qv, kv_, NT, preferred_element_type=F32)
        p = jnp.exp(s - l_ref[...][:, 0:1])
        dp = lax.dot_general(dov, vv, NT, preferred_element_type=F32)
        ds = (p * (dp - d_ref[...][:, 0:1])).astype(BF16)
        dq_ref[...] = lax.dot_general(ds, kv_, NN, preferred_element_type=F32) * scale
        dk_ref[...] += lax.dot_general(ds, qv, TN, preferred_element_type=F32)
        dv_ref[...] += lax.dot_general(p.astype(BF16), dov, TN, preferred_element_type=F32)

    blk = pl.BlockSpec((tq, LANES), lambda hh, i: (i, hh))
    kblk = pl.BlockSpec((M, LANES), lambda hh, i: (0, hh))
    vblk = pl.BlockSpec((M, LANES), lambda hh, i: (0, hh + nh))
    dq, dk, dv = pl.pallas_call(
        body, name=name, grid=(nh, S // tq), in_specs=[blk, kblk, vblk, blk, blk, blk], out_specs=[blk, kblk, kblk],
        out_shape=[jax.ShapeDtypeStruct((S, W), F32), jax.ShapeDtypeStruct((M, W), F32), jax.ShapeDtypeStruct((M, W), F32)],
        compiler_params=_params("parallel", "arbitrary"))(q, kv, kv, do, lse, delta)
    return dq, jnp.concatenate([dk, dv], axis=1)


CAUSAL_BLOCK = 512
STRIP = 32


def _causal_fwd(q, k, v, *, name):
    S, W = q.shape
    T = min(CAUSAL_BLOCK, S // 2)
    n_strips = T // STRIP

    def body(q_ref, k_ref, v_ref, o_ref, lse_ref, s0, s1, p0, p1, a0, a1, acc_scr):
        i = pl.program_id(1)
        s_scr, p_scr, a_scr = (s0, s1), (p0, p1), (a0, a1)

        def rows(j):
            return pl.ds(pl.multiple_of(j * T, T), T)

        def scores(j, slot):
            s_scr[slot][...] = lax.dot_general(q_ref[...], k_ref[rows(j), :], NT, preferred_element_type=F32)

        def softmax(slot, stats, diag):
            def strip(r):
                s = s_scr[slot][r * STRIP:(r + 1) * STRIP, :]
                if diag:
                    row = r * STRIP + lax.broadcasted_iota(jnp.int32, s.shape, 0)
                    s = jnp.where(row >= lax.broadcasted_iota(jnp.int32, s.shape, 1), s, NEG_INF)
                return s

            m_new = [jnp.maximum(m_old, jnp.max(strip(r), axis=1, keepdims=True)) for r, (m_old, _) in enumerate(stats)]
            new = []
            for r, (m_old, l_old) in enumerate(stats):
                rs = slice(r * STRIP, (r + 1) * STRIP)
                p = jnp.exp(strip(r) - m_new[r])
                alpha = jnp.exp(m_old - m_new[r])
                new.append((m_new[r], alpha * l_old + jnp.sum(p, axis=1, keepdims=True)))
                a_scr[slot][rs, :] = alpha
                p_scr[slot][rs, :] = p.astype(BF16)
            return tuple(new)

        def values(j, slot):
            acc_scr[...] = a_scr[slot][...] * acc_scr[...] + lax.dot_general(p_scr[slot][...], v_ref[rows(j), :], NN,
                                                                            preferred_element_type=F32)

        def trip(j, stats, mine, other):
            scores(j + 1, other)
            stats = softmax(mine, stats, False)
            values(jnp.maximum(j - 1, 0), other)
            return stats

        def pair(jj, stats):
            return trip(2 * jj + 1, trip(2 * jj, stats, 0, 1), 1, 0)

        def last(stats, mine, other):
            stats = softmax(mine, stats, True)
            values(jnp.maximum(i - 1, 0), other)
            values(i, mine)
            for r, (m, l) in enumerate(stats):
                rs = slice(r * STRIP, (r + 1) * STRIP)
                o_ref[rs, :] = (acc_scr[rs, :] / l).astype(BF16)
                lse_ref[rs, :] = jnp.broadcast_to(m + jnp.log(l), (STRIP, LANES))

        acc_scr[...] = jnp.zeros_like(acc_scr)
        p1[...] = jnp.zeros_like(p1)
        a1[...] = jnp.ones_like(a1)
        scores(0, 0)
        init = tuple((jnp.full((STRIP, 1), NEG_INF, F32), jnp.zeros((STRIP, 1), F32)) for _ in range(n_strips))
        stats = lax.fori_loop(0, i // 2, pair, init)

        @pl.when(i % 2 == 1)
        def _():
            last(trip(i - 1, stats, 0, 1), 1, 0)

        @pl.when(i % 2 == 0)
        def _():
            last(stats, 0, 1)

    blk = pl.BlockSpec((T, LANES), lambda hh, i: (i, hh))
    whole = pl.BlockSpec((S, LANES), lambda hh, i: (0, hh))
    return pl.pallas_call(
        body, name=name, grid=(W // LANES, S // T), in_specs=[blk, whole, whole], out_specs=[blk, blk],
        out_shape=[jax.ShapeDtypeStruct((S, W), BF16), jax.ShapeDtypeStruct((S, W), F32)],
        scratch_shapes=[pltpu.VMEM((T, T), F32)] * 2 + [pltpu.VMEM((T, T), BF16)] * 2 + [pltpu.VMEM((T, 1), F32)] * 2
        + [pltpu.VMEM((T, LANES), F32)],
        compiler_params=_params("parallel", "arbitrary"))(q, k, v)


def _causal_bwd(q, k, v, do, lse, delta, *, name):
    S, W = q.shape
    T = min(CAUSAL_BLOCK, S // 2)
    nq = S // T
    n_strips, n_col = T // STRIP, T // LANES

    def body(q_ref, k_ref, v_ref, do_ref, l_ref, d_ref, dq_ref, dk_ref, dv_ref, s0, s1, e0, e1, p0, p1, g0, g1):
        j = pl.program_id(1)
        s_scr, e_scr, p_scr, g_scr = (s0, s1), (e0, e1), (p0, p1), (g0, g1)

        @pl.when(j == 0)
        def _():
            dq_ref[...] = jnp.zeros_like(dq_ref)

        dk_ref[...] = jnp.zeros_like(dk_ref)
        dv_ref[...] = jnp.zeros_like(dv_ref)

        def rows(i):
            return pl.ds(pl.multiple_of(jnp.minimum(i, nq - 1) * T, T), T)

        def products(i, slot):
            r = rows(i)
            s_scr[slot][...] = lax.dot_general(q_ref[r, :], k_ref[...], NT, preferred_element_type=F32)
            e_scr[slot][...] = lax.dot_general(do_ref[r, :].astype(BF16), v_ref[...], NT, preferred_element_type=F32)

        def pointwise(i, slot, diag):
            base = pl.multiple_of(i * T, T)
            for r in range(n_strips):
                rs = slice(r * STRIP, (r + 1) * STRIP)
                lse_r = l_ref[pl.ds(base + r * STRIP, STRIP), :]
                del_r = d_ref[pl.ds(base + r * STRIP, STRIP), :]
                for c in range(n_col):
                    cs = slice(c * LANES, (c + 1) * LANES)
                    if diag and c * LANES > (r + 1) * STRIP - 1:
                        p_scr[slot][rs, cs] = jnp.zeros((STRIP, LANES), BF16)
                        g_scr[slot][rs, cs] = jnp.zeros((STRIP, LANES), BF16)
                        continue
                    sv = s_scr[slot][rs, cs]
                    if diag and (c + 1) * LANES - 1 > r * STRIP:
                        row = r * STRIP + lax.broadcasted_iota(jnp.int32, sv.shape, 0)
                        col = c * LANES + lax.broadcasted_iota(jnp.int32, sv.shape, 1)
                        sv = jnp.where(row >= col, sv, NEG_INF)
                    p = jnp.exp(sv - lse_r)
                    p_scr[slot][rs, cs] = p.astype(BF16)
                    g_scr[slot][rs, cs] = (p * (e_scr[slot][rs, cs] - del_r)).astype(BF16)

        def gradients(i, slot):
            r = rows(i)
            qi, doi = q_ref[r, :], do_ref[r, :].astype(BF16)
            g = g_scr[slot][...]
            dv_ref[...] += lax.dot_general(p_scr[slot][...], doi, TN, preferred_element_type=F32)
            dk_ref[...] += lax.dot_general(g, qi, TN, preferred_element_type=F32)
            dq_ref[r, :] += lax.dot_general(g, k_ref[...], NN, preferred_element_type=F32)

        p1[...] = jnp.zeros_like(p1)
        g1[...] = jnp.zeros_like(g1)
        products(j, 0)
        products(j + 1, 1)
        pointwise(j, 0, True)
        gradients(j, 1)

        def trip(i, mine, other):
            products(i + 1, other)
            pointwise(i, mine, False)
            gradients(i - 1, other)

        def pair(t, _):
            trip(j + 1 + 2 * t, 1, 0)
            trip(j + 2 + 2 * t, 0, 1)
            return 0

        n_rest = nq - 1 - j
        lax.fori_loop(0, n_rest // 2, pair, 0)

        @pl.when(n_rest % 2 == 1)
        def _():
            trip(nq - 1, 1, 0)
            gradients(nq - 1, 1)

        @pl.when(n_rest % 2 == 0)
        def _():
            gradients(nq - 1, 0)

    blk = pl.BlockSpec((T, LANES), lambda hh, j: (j, hh))
    whole = pl.BlockSpec((S, LANES), lambda hh, j: (0, hh))
    return pl.pallas_call(
        body, name=name, grid=(W // LANES, S // T), in_specs=[whole, blk, blk, whole, whole, whole],
        out_specs=[whole, blk, blk], out_shape=[jax.ShapeDtypeStruct((S, W), F32)] * 3,
        scratch_shapes=[pltpu.VMEM((T, T), F32)] * 4 + [pltpu.VMEM((T, T), BF16)] * 4,
        compiler_params=_params("parallel", "arbitrary"))(q, k, v, do, lse, delta)


BAND_TQ = 512


def _band_window(i, sub, nsub, L, q_ref, k_ref, v_ref, slope):
    kw = min(2 * BAND, L)
    n = i * nsub + sub
    k0 = 0 if kw == L else pl.multiple_of(jnp.maximum(n - 1, 0) * BAND, BAND)
    win = pl.ds(k0, kw)
    qs = q_ref[sub * BAND:(sub + 1) * BAND, :]
    kwv, vwv = k_ref[win, :], v_ref[win, :]
    s = lax.dot_general(qs, kwv, NT, preferred_element_type=F32)
    dist = (n * BAND + lax.broadcasted_iota(jnp.int32, s.shape, 0)) - (k0 + lax.broadcasted_iota(jnp.int32, s.shape, 1))
    s = jnp.where((dist >= 0) & (dist <= BAND), s - slope * dist.astype(F32), NEG_INF)
    return win, qs, kwv, vwv, s


def _band_fwd(q, k, v, slopes, *, n_heads, qcol, kcol, vcol, L, slope_mul, o_shape, name):
    tq = min(BAND_TQ, L)
    nsub = tq // BAND

    def body(sl_ref, q_ref, k_ref, v_ref, o_ref, l_ref):
        hh, i = pl.program_id(0), pl.program_id(1)
        slope = sl_ref[hh % DIL_HPG] * slope_mul
        for sub in range(nsub):
            _, _, _, vwv, s = _band_window(i, sub, nsub, L, q_ref, k_ref, v_ref, slope)
            m = jnp.max(s, axis=1, keepdims=True)
            p = jnp.exp(s - m)
            l = jnp.sum(p, axis=1, keepdims=True)
            rows = slice(sub * BAND, (sub + 1) * BAND)
            o_ref[rows, :] = lax.dot_general(p.astype(BF16), vwv, NN, preferred_element_type=F32) / l
            l_ref[rows, :] = jnp.broadcast_to(m + jnp.log(l), (BAND, LANES))

    whole = lambda col: pl.BlockSpec((L, LANES), lambda hh, i: (0, col(hh)))
    o_spec = pl.BlockSpec((tq, LANES), lambda hh, i: (i, hh))
    return pl.pallas_call(
        body, name=name, grid=(n_heads, L // tq),
        in_specs=[pl.BlockSpec(memory_space=pltpu.SMEM), pl.BlockSpec((tq, LANES), lambda hh, i: (i, qcol(hh))), whole(kcol), whole(vcol)],
        out_specs=[o_spec, o_spec], out_shape=[jax.ShapeDtypeStruct(o_shape, F32)] * 2,
        compiler_params=_params("parallel", "arbitrary"))(slopes, q, k, v)


def _band_bwd(q, k, v, do, lse, delta, slopes, *, n_heads, qcol, kcol, vcol, L, scale, slope_mul, d_shape, name):
    tq = min(BAND_TQ, L)
    nsub = tq // BAND
    n_steps = L // tq

    def body(sl_ref, q_ref, k_ref, v_ref, do_ref, l_ref, d_ref, dq_ref, dk_ref, dv_ref, dk_acc, dv_acc):
        hh, i = pl.program_id(0), pl.program_id(1)
        slope = sl_ref[hh % DIL_HPG] * slope_mul

        @pl.when(i == 0)
        def _():
            dk_acc[...] = jnp.zeros_like(dk_acc)
            dv_acc[...] = jnp.zeros_like(dv_acc)

        for sub in range(nsub):
            win, qs, kwv, vwv, s = _band_window(i, sub, nsub, L, q_ref, k_ref, v_ref, slope)
            rows = slice(sub * BAND, (sub + 1) * BAND)
            dos = do_ref[rows, :]
            p = jnp.exp(s - l_ref[rows, :][:, 0:1])
            dp = lax.dot_general(dos, vwv, NT, preferred_element_type=F32)
            ds = (p * (dp - d_ref[rows, :][:, 0:1])).astype(BF16)
            dq_ref[rows, :] = (lax.dot_general(ds, kwv, NN, preferred_element_type=F32) * scale).astype(BF16)
            dk_acc[win, :] += lax.dot_general(ds, qs, TN, preferred_element_type=F32)
            dv_acc[win, :] += lax.dot_general(p.astype(BF16), dos, TN, preferred_element_type=F32)

        @pl.when(i == n_steps - 1)
        def _():
            dk_ref[...] = dk_acc[...].astype(BF16)
            dv_ref[...] = dv_acc[...].astype(BF16)

    whole = lambda col: pl.BlockSpec((L, LANES), lambda hh, i: (0, col(hh)))
    blk = pl.BlockSpec((tq, LANES), lambda hh, i: (i, hh))
    ident = lambda hh: hh
    return pl.pallas_call(
        body, name=name, grid=(n_heads, n_steps),
        in_specs=[pl.BlockSpec(memory_space=pltpu.SMEM), pl.BlockSpec((tq, LANES), lambda hh, i: (i, qcol(hh))), whole(kcol), whole(vcol),
                  blk, blk, blk],
        out_specs=[blk, whole(ident), whole(ident)], out_shape=[jax.ShapeDtypeStruct(d_shape, BF16)] * 3,
        scratch_shapes=[pltpu.VMEM((L, LANES), F32)] * 2,
        compiler_params=_params("parallel", "arbitrary"))(slopes, q, k, v, do, lse, delta)


def _pad_heads(w, n_heads, width, axis):
    shp = w.shape
    new = shp[:axis] + (n_heads, width) + shp[axis + 1:]
    pad = [(0, 0)] * len(new)
    pad[axis + 1] = (0, LANES - width)
    out = jnp.pad(w.reshape(new), pad)
    return out.reshape(shp[:axis] + (n_heads * LANES,) + shp[axis + 1:])


def _unpad_heads(w, n_heads, width, axis):
    shp = w.shape
    new = shp[:axis] + (n_heads, LANES) + shp[axis + 1:]
    out = lax.slice_in_dim(w.reshape(new), 0, width, axis=axis + 1)
    return out.reshape(shp[:axis] + (n_heads * width,) + shp[axis + 1:])


def _alibi_slopes():
    s = jnp.exp2(-8.0 * jnp.arange(1, DIL_HEADS + 1, dtype=F32) / DIL_HEADS)
    return s.reshape(DIL_HPG, DIL_GROUPS).T


def _local_step(x, mem, positions, tgt, W):
    S = x.shape[0]
    pos = positions.reshape(S, 1).astype(F32)
    half = MLA_ROPE // 2
    inv_freq = ROPE_THETA ** (-jnp.arange(half, dtype=F32) / half)
    invf = jnp.zeros((1, LANES), F32).at[0, MLA_NOPE:MLA_NOPE + half].set(inv_freq).at[0, MLA_NOPE + half:MLA_QK].set(inv_freq)
    slopes = _alibi_slopes()

    w_in = W["w_in"]
    zc = lambda n: jnp.zeros((D_MODEL, n), BF16)
    w_a = jnp.concatenate([w_in[:, :OFF_Q], zc(MLA_NOPE), w_in[:, OFF_KV:OFF_KR], zc(LANES - MLA_QK), w_in[:, OFF_Q:OFF_KV]], axis=1)
    w_d, w_m, w_g = w_in[:, OFF_KR:OFF_DIL], w_in[:, OFF_DIL:OFF_MEMQ], w_in[:, OFF_MEMQ:]
    w_uq_p = _pad_heads(W["w_uq"], MLA_HEADS, MLA_QK, 1)
    ukv = W["w_ukv"].reshape(KV_RANK, MLA_HEADS, 2 * MLA_NOPE)
    w_uk_p = _pad_heads(ukv[:, :, :MLA_NOPE].reshape(KV_RANK, -1), MLA_HEADS, MLA_NOPE, 1)
    w_uv_p = _pad_heads(ukv[:, :, MLA_NOPE:].reshape(KV_RANK, -1), MLA_HEADS, MLA_NOPE, 1)
    w_br_mla_p = _pad_heads(W["w_br_mla"], MLA_HEADS, MLA_NOPE, 0)

    h = _rmsnorm(x, W["g_pre_mix"], BF16, "norm1")
    proj_a = _matmul(h, w_a, tn=768, name="proj_a")
    att_scale = LANES ** -0.5
    proj_d = _matmul(h, w_d, tn=DIL_GROUPS * DIL_W, out_dtype=BF16, scale=(1, att_scale), name="proj_d")
    proj_m = _matmul(h, w_m, tn=MEM_W, out_dtype=BF16, scale=(1, att_scale), name="proj_m")
    proj_g = _matmul(h, w_g, tn=1536, name="proj_g")

    cq_n, ckv_n, kpe = _mla_prep(proj_a, pos, invf, W["mla_q_norm"], W["mla_kv_norm"])
    q_pre = _matmul(cq_n, w_uq_p, tn=1024, name="mla_q")
    k_pre = _matmul(ckv_n, w_uk_p, tn=1024, name="mla_k")
    v_mla = _matmul(ckv_n, w_uv_p, tn=1024, out_dtype=BF16, name="mla_v")
    q_mla, k_mla = _qk_final(q_pre, k_pre, kpe, pos, invf)
    ident = lambda hh: hh
    o_mla, lse_mla = _causal_fwd(q_mla, k_mla, v_mla, name="attn_mla_fwd")

    n_dil_cols = 3 * DIL_HEADS
    o_dil, lse_dil = [], []
    for g, (window, dil) in enumerate(DIL_PAIRS):
        L = S // dil
        view = proj_d.reshape(L, dil * 3 * DIL_HEADS * LANES)
        col = lambda part, g=g: (lambda hh: (hh // DIL_HPG) * n_dil_cols + (part * DIL_GROUPS + g) * DIL_HPG + hh % DIL_HPG)
        o, lse = _band_fwd(view, view, view, slopes[g], n_heads=dil * DIL_HPG, qcol=col(0), kcol=col(1), vcol=col(2), L=L,
                           slope_mul=float(dil), o_shape=(L, dil * DIL_W), name=f"attn_dil{g}_fwd")
        o_dil.append(o.reshape(S, DIL_W))
        lse_dil.append(lse.reshape(S, DIL_W))
    y_dil = _dil_mix(o_dil, lse_dil)

    mem_n = _rmsnorm(mem, W["g_mem"], BF16, "mem_norm")
    kv_mem = _matmul(mem_n, W["w_mem_kv"], name="mem_kv")
    o_mem, lse_mem = _mem_fwd(proj_m, kv_mem, name="attn_mem_fwd")

    b_mla = _matmul(o_mla, w_br_mla_p, name="br_mla")
    b_dil = _matmul(y_dil, W["w_br_dil"], name="br_dil")
    b_mem = _matmul(o_mem, W["w_br_mem"], name="br_mem")
    merged = _merge(proj_g, W["b_gate"], [b_mla, b_dil, b_mem])
    o_proj = _matmul(merged, W["w_o"], name="o_proj")
    x1, h2 = _norm2(o_proj, x, W["g_post_mix"], W["g_pre_ffn"])

    u = _matmul(h2, W["w_ffn_up"], tn=1408, name="ffn_up")
    act = _conv_fwd(u, W["conv_w"], W["conv_b"])
    f = _matmul(act, W["w_ffn_down"], name="ffn_down")
    loss8, dx2, df, dg_post_ffn = _loss_head(f, x1, tgt, W["g_post_ffn"])
    loss = loss8[0, 0]

    G = {"g_post_ffn": dg_post_ffn}
    d_act = _matmul(df, W["w_ffn_down"], mode="nt", tn=1408, name="d_act")
    G["w_ffn_down"] = _matmul(act, df, mode="tn", tm=1408, tk=1024, name="dw_ffn_down")
    dzg, dzv, G["conv_w"], G["conv_b"] = _conv_bwd_dz(d_act, u, W["conv_w"], W["conv_b"])
    du = _conv_bwd_du(dzg, dzv, W["conv_w"])
    dh2 = _matmul(du, W["w_ffn_up"], mode="nt", tk=2816, name="d_h2")
    G["w_ffn_up"] = _matmul(h2, du, mode="tn", tn=1408, tk=1024, name="dw_ffn_up")
    dx1, do_proj, G["g_pre_ffn"], G["g_post_mix"] = _norm2_bwd(dx2, dh2, x1, o_proj, W["g_pre_ffn"], W["g_post_mix"])
    dmerged = _matmul(do_proj, W["w_o"], mode="nt", name="d_merged")
    G["w_o"] = _matmul(merged, do_proj, mode="tn", tk=1024, name="dw_o")
    db_mla, db_dil, db_mem, dproj_g, G["b_gate"] = _merge_bwd(dmerged, proj_g, W["b_gate"], [b_mla, b_dil, b_mem])

    dy_mem = _matmul(db_mem, W["w_br_mem"], mode="nt", name="d_y_mem")
    G["w_br_mem"] = _matmul(o_mem, db_mem, mode="tn", tk=1024, name="dw_br_mem")
    delta_mem = _delta(dy_mem, o_mem, MEM_HEADS, "delta_mem")
    dq_mem, dkv_mem = _mem_bwd(proj_m, kv_mem, dy_mem, lse_mem, delta_mem, scale=att_scale, name="attn_mem_bwd")
    G["w_mem_kv"] = _matmul(mem_n, dkv_mem, mode="tn", name="dw_mem_kv")
    dmem_n = _matmul(dkv_mem, W["w_mem_kv"], mode="nt", name="d_mem_n")
    G["g_mem"] = _gain_grad(dmem_n, mem, "dg_mem")

    dy_dil = _matmul(db_dil, W["w_br_dil"], mode="nt", name="d_y_dil")
    G["w_br_dil"] = _matmul(y_dil, db_dil, mode="tn", tk=1024, name="dw_br_dil")
    mix = _dil_mix_bwd(dy_dil, o_dil, lse_dil)
    do_dil, dl_dil = mix[:3], mix[3:]
    d_parts = [[None] * DIL_GROUPS for _ in range(3)]
    for g, (window, dil) in enumerate(DIL_PAIRS):
        L = S // dil
        view = proj_d.reshape(L, dil * 3 * DIL_HEADS * LANES)
        col = lambda part, g=g: (lambda hh: (hh // DIL_HPG) * n_dil_cols + (part * DIL_GROUPS + g) * DIL_HPG + hh % DIL_HPG)
        shp = (L, dil * DIL_W)
        dq, dk, dv = _band_bwd(
            view, view, view, do_dil[g].reshape(shp), lse_dil[g].reshape(shp), dl_dil[g].reshape(shp), slopes[g],
            n_heads=dil * DIL_HPG, qcol=col(0), kcol=col(1), vcol=col(2), L=L, scale=att_scale, slope_mul=float(dil),
            d_shape=shp, name=f"attn_dil{g}_bwd")
        for part, d in enumerate((dq, dk, dv)):
            d_parts[part][g] = d.reshape(S, DIL_W)
    dproj_d = jnp.concatenate([d for part in d_parts for d in part], axis=1)

    dy_mla = _matmul(db_mla, w_br_mla_p, mode="nt", name="d_y_mla")
    dw_br_mla_p = _matmul(o_mla, db_mla, mode="tn", tk=1024, name="dw_br_mla")
    G["w_br_mla"] = _unpad_heads(dw_br_mla_p, MLA_HEADS, MLA_NOPE, 0)
    delta_mla = _delta(dy_mla, o_mla, MLA_HEADS, "delta_mla")
    dq_mla, dk_mla, dv_mla = _causal_bwd(q_mla, k_mla, v_mla, dy_mla, lse_mla, delta_mla, name="attn_mla_bwd")
    dq_pre, dkpe = _mla_bwd_prep(dq_mla, dk_mla, pos, invf)
    dcq_n = _matmul(dq_pre, w_uq_p, mode="nt", tn=Q_RANK, name="d_cq")
    G["w_uq"] = _unpad_heads(_matmul(cq_n, dq_pre, mode="tn", tm=Q_RANK, tk=1024, name="dw_uq"), MLA_HEADS, MLA_QK, 1)
    dckv_a = _matmul(dk_mla, w_uk_p, mode="nt", tn=KV_RANK, name="d_ckv_k")
    dckv_b = _matmul(dv_mla, w_uv_p, mode="nt", tn=KV_RANK, name="d_ckv_v")
    dw_uk = _unpad_heads(_matmul(ckv_n, dk_mla, mode="tn", tm=KV_RANK, tk=1024, name="dw_uk"), MLA_HEADS, MLA_NOPE, 1)
    dw_uv = _unpad_heads(_matmul(ckv_n, dv_mla, mode="tn", tm=KV_RANK, tk=1024, name="dw_uv"), MLA_HEADS, MLA_NOPE, 1)
    G["w_ukv"] = jnp.concatenate([dw_uk.reshape(KV_RANK, MLA_HEADS, MLA_NOPE), dw_uv.reshape(KV_RANK, MLA_HEADS, MLA_NOPE)],
                                 axis=2).reshape(KV_RANK, -1)
    dproj_a, G["mla_q_norm"], G["mla_kv_norm"] = _mla_norm_bwd(dcq_n, dckv_a, dckv_b, dkpe, proj_a, W["mla_q_norm"],
                                                              W["mla_kv_norm"])

    dh = _matmul(dproj_a, w_a, mode="nt", name="d_h_a")
    dh = _matmul(dproj_d, w_d, mode="nt", tk=2304, add=dh, name="d_h_d")
    dh = _matmul(dq_mem, w_m, mode="nt", add=dh, name="d_h_m")
    dh = _matmul(dproj_g, w_g, mode="nt", add=dh, name="d_h_g")
    dw_a = _matmul(h, dproj_a, mode="tn", tn=768, tk=1024, name="dw_in_a")
    dw_d = _matmul(h, dproj_d, mode="tn", tn=1536, tk=1024, name="dw_in_d")
    dw_m = _matmul(h, dq_mem, mode="tn", tn=512, tk=1024, name="dw_in_m")
    dw_g = _matmul(h, dproj_g, mode="tn", tn=1536, tk=1024, name="dw_in_g")
    kr0 = Q_RANK + MLA_NOPE
    G["w_in"] = jnp.concatenate([dw_a[:, :Q_RANK], dw_a[:, Q_RANK + LANES:], dw_a[:, kr0:kr0 + MLA_ROPE], dw_d, dw_m, dw_g], axis=1)
    grad_x, G["g_pre_mix"] = _norm1_bwd(dx1, dh, x, W["g_pre_mix"])
    return loss, grad_x, G


WEIGHTS = ["g_pre_mix", "w_in", "b_gate", "mla_q_norm", "w_uq", "mla_kv_norm", "w_ukv", "g_mem", "w_mem_kv", "w_br_mla",
           "w_br_dil", "w_br_mem", "w_o", "g_post_mix", "g_pre_ffn", "w_ffn_up", "conv_w", "conv_b", "w_ffn_down", "g_post_ffn"]
GROUPS = [
    [("w_in", (D_MODEL, D_IN), 1)],
    [("w_uq", (Q_RANK, MLA_HEADS * MLA_QK), 1)],
    [("w_ukv", (KV_RANK, MLA_HEADS * 2 * MLA_NOPE), 1), ("w_br_mla", (MLA_HEADS * MLA_NOPE, D_MODEL), 1),
     ("w_br_dil", (DIL_W, D_MODEL), 1), ("w_br_mem", (MEM_W, D_MODEL), 1)],
    [("w_mem_kv", (D_MODEL, 2 * MEM_W), 0), ("w_o", (D_MODEL, D_MODEL), 0), ("w_ffn_down", (D_FF, D_MODEL), 0)],
    [("w_ffn_up", (D_MODEL, 2 * D_FF), 1)],
]
CONV_W = ("conv_w", (3, 2 * D_FF), 1)
REPLICATED = [("g_pre_mix", D_MODEL), ("b_gate", 3 * D_MODEL), ("mla_q_norm", Q_RANK), ("mla_kv_norm", KV_RANK), ("g_mem", D_MODEL),
              ("g_post_mix", D_MODEL), ("g_pre_ffn", D_MODEL), ("conv_b", 2 * D_FF), ("g_post_ffn", D_MODEL)]
SMALL_ROWS = 256
CONV_AT = sum(n for _, n in REPLICATED)
LOSS_AT = CONV_AT + 3 * 2 * D_FF


def _shard_shape(shape, axis):
    return tuple(d // N_CHIPS if a == axis else d for a, d in enumerate(shape))


def _group_shape(grp):
    shapes = [_shard_shape(shape, axis) for _, shape, axis in grp]
    assert len({s[1] for s in shapes}) == 1
    return sum(s[0] for s in shapes), shapes[0][1]


def _member_shards(a, axis):
    r, c = a.shape
    if axis == 0:
        return a.reshape(N_CHIPS, r // N_CHIPS, c)
    return a.reshape(r, N_CHIPS, c // N_CHIPS).transpose(1, 0, 2)


def _member_full(s, axis):
    n, r, c = s.shape
    if axis == 0:
        return s.reshape(n * r, c)
    return s.transpose(1, 0, 2).reshape(r, n * c)


def _my_weight_groups(w):
    out = []
    for grp in GROUPS:
        rows, width = _group_shape(grp)
        out.append(jnp.concatenate([w[name].astype(BF16) for name, _, _ in grp], axis=0).reshape(2, rows // 2, width))
    return out


def _full_weights(gathered, conv_all):
    out = {}
    for grp, ga in zip(GROUPS, gathered):
        ga = ga.reshape(N_CHIPS, -1, ga.shape[-1])
        off = 0
        for name, shape, axis in grp:
            rows = _shard_shape(shape, axis)[0]
            out[name] = _member_full(ga[:, off:off + rows], axis)
            off += rows
    out[CONV_W[0]] = _member_full(conv_all, CONV_W[2])
    return out


def _grad_groups(G):
    out = []
    for grp in GROUPS:
        rows, width = _group_shape(grp)
        a = jnp.concatenate([_member_shards(G[name], axis) for name, _, axis in grp], axis=1)
        out.append(a.reshape(N_CHIPS, 2, rows // 2, width).transpose(1, 0, 2, 3))
    return out


def _pack_small(vals, conv_g=None, loss=None):
    parts = [vals[name].reshape(-1) for name, _ in REPLICATED]
    if conv_g is not None:
        parts += [conv_g.reshape(-1), loss.reshape(1)]
    flat = jnp.concatenate(parts)
    return jnp.pad(flat, (0, SMALL_ROWS * LANES - flat.shape[0])).reshape(SMALL_ROWS, LANES)


def _unpack_small(packed):
    flat = packed.reshape(-1)
    out, off = {}, 0
    for name, n in REPLICATED:
        out[name] = flat[off:off + n].reshape(1, n)
        off += n
    return out


MESH = pl.DeviceIdType.MESH
HBM_SPEC = pl.BlockSpec(memory_space=pltpu.HBM)


def _place():
    x, y, c = lax.axis_index("x"), lax.axis_index("y"), lax.axis_index("c")
    chips = [(1 - x, y), (x, 1 - y), (1 - x, 1 - y)]
    return x, y, c, chips


def _remote(src, dst, send_sems, recv_sems, k, to):
    return pltpu.make_async_remote_copy(src_ref=src, dst_ref=dst, send_sem=send_sems.at[k], recv_sem=recv_sems.at[k],
                                        device_id=to, device_id_type=MESH)


def _gather_weights(groups, conv_w):
    n = len(groups)

    def body(*refs):
        srcs, conv_src, outs, conv_out = refs[:n], refs[n], refs[n + 1:2 * n + 1], refs[2 * n + 1]
        send_sems, recv_sems = refs[2 * n + 2:]
        x, y, c, chips = _place()
        me = 2 * x + y
        sibling = (x, y, 1 - c)
        first = [_remote(srcs[g].at[c], outs[g].at[me, c], send_sems, recv_sems, g * 3 + k, (px, py, c))
                 for k, (px, py) in enumerate(chips) for g in range(n)]
        first += [_remote(conv_src, conv_out.at[me], send_sems, recv_sems, 6 * n + k, (px, py, c)) for k, (px, py) in enumerate(chips)]
        for cp in first:
            cp.start()
        passed = []
        for k, (px, py) in enumerate(chips):
            for g in range(n):
                slot = outs[g].at[2 * px + py, c]
                _remote(slot, slot, send_sems, recv_sems, g * 3 + k, (px, py, c)).wait_recv()
                cp = _remote(slot, slot, send_sems, recv_sems, 3 * n + g * 3 + k, sibling)
                cp.start()
                passed.append(cp)
        for k, (px, py) in enumerate(chips):
            slot = conv_out.at[2 * px + py]
            _remote(slot, slot, send_sems, recv_sems, 6 * n + k, (px, py, c)).wait_recv()
            for g in range(n):
                slot = outs[g].at[2 * px + py, 1 - c]
                _remote(slot, slot, send_sems, recv_sems, 3 * n + g * 3 + k, sibling).wait_recv()
        for cp in first + passed:
            cp.wait_send()

    n_sem = 6 * n + 3
    outs = pl.pallas_call(
        body, name="comm_gather_weights", in_specs=[HBM_SPEC] * (n + 1), out_specs=[HBM_SPEC] * (n + 1),
        out_shape=[jax.ShapeDtypeStruct((N_CHIPS,) + g.shape, g.dtype) for g in groups]
        + [jax.ShapeDtypeStruct((N_CHIPS,) + conv_w.shape, conv_w.dtype)],
        scratch_shapes=[pltpu.SemaphoreType.DMA((n_sem,)), pltpu.SemaphoreType.DMA((n_sem,))],
    )(*groups, conv_w)
    me = 2 * lax.axis_index("x") + lax.axis_index("y")
    outs = [lax.dynamic_update_slice_in_dim(o, a[None], me, axis=0) for o, a in zip(outs, list(groups) + [conv_w])]
    return outs[:n], outs[n]


def _swap_halves(groups):
    n = len(groups)

    def body(*refs):
        srcs, outs, send_sems, recv_sems = refs[:n], refs[n:2 * n], refs[2 * n], refs[2 * n + 1]
        x, y, c, _ = _place()
        cps = [_remote(srcs[g].at[1 - c], outs[g], send_sems, recv_sems, g, (x, y, 1 - c)) for g in range(n)]
        for cp in cps:
            cp.start()
        for cp in cps:
            cp.wait()

    return pl.pallas_call(
        body, name="comm_swap_halves", in_specs=[HBM_SPEC] * n, out_specs=[HBM_SPEC] * n,
        out_shape=[jax.ShapeDtypeStruct(g.shape[1:], g.dtype) for g in groups],
        scratch_shapes=[pltpu.SemaphoreType.DMA((n,)), pltpu.SemaphoreType.DMA((n,))],
    )(*groups)


def _scatter_partials(groups):
    n = len(groups)

    def body(*refs):
        srcs, outs = refs[:n], refs[n:2 * n]
        send_sems, recv_sems = refs[2 * n:]
        x, y, c, chips = _place()
        sends = [_remote(srcs[g].at[2 * px + py], outs[g].at[k], send_sems, recv_sems, g * 3 + k, (px, py, c))
                 for k, (px, py) in enumerate(chips) for g in range(n)]
        for cp in sends:
            cp.start()
        for k, (px, py) in enumerate(chips):
            for g in range(n):
                slot = outs[g].at[k]
                _remote(slot, slot, send_sems, recv_sems, g * 3 + k, (px, py, c)).wait_recv()
        for cp in sends:
            cp.wait_send()

    return pl.pallas_call(
        body, name="comm_scatter_partials", in_specs=[HBM_SPEC] * n, out_specs=[HBM_SPEC] * n,
        out_shape=[jax.ShapeDtypeStruct((3,) + g.shape[1:], g.dtype) for g in groups],
        scratch_shapes=[pltpu.SemaphoreType.DMA((3 * n,)), pltpu.SemaphoreType.DMA((3 * n,))],
    )(*groups)


def _share_reduced(groups):
    n = len(groups)

    def body(*refs):
        srcs, outs = refs[:n], refs[n:2 * n]
        send_sems, recv_sems = refs[2 * n:]
        x, y, c, _ = _place()
        cps = [_remote(srcs[g], outs[g], send_sems, recv_sems, g, (x, y, 1 - c)) for g in range(n)]
        for cp in cps:
            cp.start()
        for cp in cps:
            cp.wait()

    return pl.pallas_call(
        body, name="comm_share_reduced", in_specs=[HBM_SPEC] * n, out_specs=[HBM_SPEC] * n,
        out_shape=[jax.ShapeDtypeStruct(g.shape, g.dtype) for g in groups],
        scratch_shapes=[pltpu.SemaphoreType.DMA((n,)), pltpu.SemaphoreType.DMA((n,))],
    )(*groups)


def _allreduce_small(v):
    n_dev = 2 * N_CHIPS

    def body(src, out, buf, send_sems, recv_sems):
        x, y, c, _ = _place()
        me = 4 * x + 2 * y + c
        buf[me] = src[...]
        flips = [(k >> 2 & 1, k >> 1 & 1, k & 1) for k in range(1, n_dev)]
        sends = []
        for k, (fx, fy, fc) in enumerate(flips):
            to = ((1 - x) if fx else x, (1 - y) if fy else y, (1 - c) if fc else c)
            cp = _remote(src, buf.at[me], send_sems, recv_sems, k, to)
            cp.start()
            sends.append((cp, to))
        for k, (cp, to) in enumerate(sends):
            slot = buf.at[4 * to[0] + 2 * to[1] + to[2]]
            _remote(slot, slot, send_sems, recv_sems, k, to).wait_recv()
        for cp, _ in sends:
            cp.wait_send()
        acc = buf[0]
        for d in range(1, n_dev):
            acc = acc + buf[d]
        out[...] = acc

    vm = pl.BlockSpec(memory_space=pltpu.VMEM)
    return pl.pallas_call(
        body, name="comm_allreduce_small", in_specs=[vm], out_specs=vm, out_shape=jax.ShapeDtypeStruct(v.shape, F32),
        scratch_shapes=[pltpu.VMEM((n_dev,) + v.shape, F32), pltpu.SemaphoreType.DMA((n_dev - 1,)),
                        pltpu.SemaphoreType.DMA((n_dev - 1,))],
    )(v)


def _row_tile(rows, cap=320):
    return max(t for t in range(16, cap + 1, 16) if rows % t == 0)


def _add_sibling(mine, theirs, core, name):
    _, n, R, C = mine.shape
    t = _row_tile(R)

    def body(core_ref, a_ref, b_ref, o_ref, ob_ref):
        tot = a_ref[...] + b_ref[...]
        o_ref[...] = tot
        ob_ref[...] = tot.astype(BF16)

    sp = pl.BlockSpec((None, t, C), lambda k, i, core_ref: (k, i, 0))
    grid_spec = pltpu.PrefetchScalarGridSpec(
        num_scalar_prefetch=1, grid=(n, R // t),
        in_specs=[pl.BlockSpec((None, None, t, C), lambda k, i, core_ref: (core_ref[0], k, i, 0)), sp], out_specs=[sp, sp])
    return pl.pallas_call(body, name=name, grid_spec=grid_spec,
                          out_shape=[jax.ShapeDtypeStruct((n, R, C), F32), jax.ShapeDtypeStruct((n, R, C), BF16)],
                          compiler_params=_params("parallel", "parallel"))(core, mine, theirs)


def _add_chips(received, own, chip, name):
    n, R, C = received.shape
    t = _row_tile(R)

    def body(chip_ref, r_ref, o_ref, out_ref):
        acc = o_ref[...]
        for k in range(n):
            acc = acc + r_ref[k].astype(F32)
        out_ref[...] = acc

    grid_spec = pltpu.PrefetchScalarGridSpec(
        num_scalar_prefetch=1, grid=(R // t,),
        in_specs=[pl.BlockSpec((n, t, C), lambda i, chip_ref: (0, i, 0)), pl.BlockSpec((None, t, C), lambda i, chip_ref: (chip_ref[0], i, 0))],
        out_specs=pl.BlockSpec((t, C), lambda i, chip_ref: (i, 0)))
    return pl.pallas_call(body, name=name, grid_spec=grid_spec, out_shape=jax.ShapeDtypeStruct((R, C), F32),
                          compiler_params=_params("parallel"))(chip, received, own)


def _adamw(w, g, m, v, name, g_row0=0):
    R, C = w.shape
    t = math.gcd(math.gcd(R, g_row0), 128) if R % 8 == 0 else R
    assert g_row0 % t == 0
    c1 = 1.0 - ADAM_B1 ** ADAM_STEP
    c2 = 1.0 - ADAM_B2 ** ADAM_STEP

    def body(w_ref, g_ref, m_ref, v_ref, go_ref, d_ref, nm_ref, nv_ref):
        gv = g_ref[...]
        nm = ADAM_B1 * m_ref[...] + (1.0 - ADAM_B1) * gv
        nv = ADAM_B2 * v_ref[...] + (1.0 - ADAM_B2) * (gv * gv)
        go_ref[...] = gv
        d_ref[...] = -ADAM_LR * ((nm / c1) / (jnp.sqrt(nv / c2) + ADAM_EPS) + ADAM_WD * w_ref[...])
        nm_ref[...] = nm
        nv_ref[...] = nv

    sp = pl.BlockSpec((t, C), lambda i: (i, 0))
    g_sp = pl.BlockSpec((t, C), lambda i: (i + g_row0 // t, 0))
    return pl.pallas_call(body, name=name, grid=(R // t,), in_specs=[sp, g_sp, sp, sp], out_specs=[sp] * 4,
                          out_shape=[jax.ShapeDtypeStruct((R, C), F32)] * 4, compiler_params=_params("parallel"))(w, g, m, v)


def kernel(x, mem, positions, g_pre_mix, w_in, b_gate, mla_q_norm, w_uq, mla_kv_norm, w_ukv, g_mem, w_mem_kv, w_br_mla, w_br_dil, w_br_mem, w_o, g_post_mix, g_pre_ffn, w_ffn_up, conv_w, conv_b, w_ffn_down, g_post_ffn, loss_target, m_g_pre_mix, m_w_in, m_b_gate, m_mla_q_norm, m_w_uq, m_mla_kv_norm, m_w_ukv, m_g_mem, m_w_mem_kv, m_w_br_mla, m_w_br_dil, m_w_br_mem, m_w_o, m_g_post_mix, m_g_pre_ffn, m_w_ffn_up, m_conv_w, m_conv_b, m_w_ffn_down, m_g_post_ffn, v_g_pre_mix, v_w_in, v_b_gate, v_mla_q_norm, v_w_uq, v_mla_kv_norm, v_w_ukv, v_g_mem, v_w_mem_kv, v_w_br_mla, v_w_br_dil, v_w_br_mem, v_w_o, v_g_post_mix, v_g_pre_ffn, v_w_ffn_up, v_conv_w, v_conv_b, v_w_ffn_down, v_g_post_ffn):
    w_args = (g_pre_mix, w_in, b_gate, mla_q_norm, w_uq, mla_kv_norm, w_ukv, g_mem, w_mem_kv, w_br_mla, w_br_dil, w_br_mem, w_o,
              g_post_mix, g_pre_ffn, w_ffn_up, conv_w, conv_b, w_ffn_down, g_post_ffn)
    m_args = (m_g_pre_mix, m_w_in, m_b_gate, m_mla_q_norm, m_w_uq, m_mla_kv_norm, m_w_ukv, m_g_mem, m_w_mem_kv, m_w_br_mla,
              m_w_br_dil, m_w_br_mem, m_w_o, m_g_post_mix, m_g_pre_ffn, m_w_ffn_up, m_conv_w, m_conv_b, m_w_ffn_down, m_g_post_ffn)
    v_args = (v_g_pre_mix, v_w_in, v_b_gate, v_mla_q_norm, v_w_uq, v_mla_kv_norm, v_w_ukv, v_g_mem, v_w_mem_kv, v_w_br_mla,
              v_w_br_dil, v_w_br_mem, v_w_o, v_g_post_mix, v_g_pre_ffn, v_w_ffn_up, v_conv_w, v_conv_b, v_w_ffn_down, v_g_post_ffn)
    sharded = {name for grp in GROUPS for name, _, _ in grp} | {CONV_W[0]}

    def local(a, name):
        return a[0] if name in sharded else a

    w = {n: local(a, n) for n, a in zip(WEIGHTS, w_args)}
    m = {n: local(a, n) for n, a in zip(WEIGHTS, m_args)}
    v = {n: local(a, n) for n, a in zip(WEIGHTS, v_args)}

    full = _full_weights(*_gather_weights(_my_weight_groups(w), w[CONV_W[0]]))
    full.update({name: w[name] for name, _ in REPLICATED})

    loss_local, grad_x, G = _local_step(x[0], mem[0], positions, loss_target[0], full)

    core = lax.axis_index("c").astype(jnp.int32).reshape(1)
    mine = _grad_groups(G)
    theirs = _swap_halves(mine)
    chip = (2 * lax.axis_index("x") + lax.axis_index("y")).astype(jnp.int32).reshape(1)
    partial = [_add_sibling(a, b, core, f"add_sibling_{i}") for i, (a, b) in enumerate(zip(mine, theirs))]
    received = _scatter_partials([p16 for _, p16 in partial])
    reduced = [_add_chips(r, p32, chip, f"add_chips_{i}") for i, (r, (p32, _)) in enumerate(zip(received, partial))]
    first = lax.axis_index("c") == 0
    shard_groups = [jnp.concatenate([jnp.where(first, a, b), jnp.where(first, b, a)], axis=0)
                    for a, b in zip(reduced, _share_reduced(reduced))]
    small = _allreduce_small(_pack_small(G, G[CONV_W[0]], loss_local))
    flat = small.reshape(-1)
    loss = flat[LOSS_AT]
    conv_g = flat[CONV_AT:LOSS_AT].reshape(CONV_W[1])
    conv_cols = CONV_W[1][1] // N_CHIPS
    conv_g = lax.dynamic_slice_in_dim(conv_g, (2 * lax.axis_index("x") + lax.axis_index("y")) * conv_cols, conv_cols, axis=1)

    grads, deltas, new_m, new_v = {}, {}, {}, {}
    for grp, g_all in zip(GROUPS, shard_groups):
        off = 0
        for name, _, _ in grp:
            grads[name], deltas[name], new_m[name], new_v[name] = _adamw(w[name], g_all, m[name], v[name], "adamw_" + name, off)
            off += w[name].shape[0]
    name = CONV_W[0]
    grads[name], deltas[name], new_m[name], new_v[name] = _adamw(w[name], conv_g, m[name], v[name], "adamw_" + name)
    packed = _adamw(_pack_small(w), small, _pack_small(m), _pack_small(v), "adamw_small")
    for dst, packed_small in zip((grads, deltas, new_m, new_v), packed):
        dst.update(_unpack_small(packed_small))

    def out(d, name):
        return d[name][None] if name in sharded else d[name]

    return (loss, grad_x[None], *[out(grads, n) for n in WEIGHTS], *[out(deltas, n) for n in WEIGHTS],
            *[out(new_m, n) for n in WEIGHTS], *[out(new_v, n) for n in WEIGHTS])
```

```python
import functools
import math

import jax
import jax.numpy as jnp
from jax import lax
from jax.experimental import pallas as pl
from jax.experimental.pallas import tpu as pltpu
from jax.experimental.pallas import tpu_sc as plsc

F32 = jnp.float32
BF16 = jnp.bfloat16

D_MODEL = 1024
N_MEM = 256
RMS_EPS = 1e-6
NEG_INF = -1e30
MLA_HEADS = 8
MLA_NOPE = 64
MLA_ROPE = 32
MLA_QK = 96
Q_RANK = 384
KV_RANK = 256
ROPE_THETA = 10000.0
DIL_PAIRS = ((128, 1), (512, 4), (2048, 16))
DIL_GROUPS = 3
DIL_HPG = 4
DIL_HEADS = 12
DIL_W = 512
MEM_HEADS = 4
MEM_W = 512
D_FF = 2816
OFF_Q = 384
OFF_KV = 640
OFF_KR = 672
OFF_DIL = 5280
OFF_MEMQ = 5792
D_IN = 8864
ADAM_LR = 0.001
ADAM_B1 = 0.9
ADAM_B2 = 0.999
ADAM_EPS = 1e-08
ADAM_WD = 0.01
ADAM_STEP = 10

LANES = 128
VMEM_LIMIT = 56 * 1024 * 1024

N_CHIPS = 4
ROW_TILE = 256

NN = (((1,), (0,)), ((), ()))
NT = (((1,), (1,)), ((), ()))
TN = (((0,), (0,)), ((), ()))


def _params(*sem):
    return pltpu.CompilerParams(dimension_semantics=sem, vmem_limit_bytes=VMEM_LIMIT)


def _full(shape):
    return pl.BlockSpec(shape, lambda *_: (0,) * len(shape))


def _matmul(a, b, *, mode="nn", out_dtype=F32, tm=1024, tn=1024, tk=None, add=None, scale=None, name):
    if mode == "nn":
        (M, K), N = a.shape, b.shape[1]
    elif mode == "nt":
        (M, K), N = a.shape, b.shape[0]
    else:
        (K, M), N = a.shape, b.shape[1]
    tm, tn = min(tm, M), min(tn, N)
    tk = K if tk is None else min(tk, K)
    assert M % tm == 0 and N % tn == 0 and K % tk == 0, (name, M, N, K, tm, tn, tk)
    nk = K // tk
    dims = {"nn": NN, "nt": NT, "tn": TN}[mode]
    a_spec = pl.BlockSpec((tk, tm), lambda i, j, k: (k, i)) if mode == "tn" else pl.BlockSpec((tm, tk), lambda i, j, k: (i, k))
    b_spec = pl.BlockSpec((tn, tk), lambda i, j, k: (j, k)) if mode == "nt" else pl.BlockSpec((tk, tn), lambda i, j, k: (k, j))
    o_spec = pl.BlockSpec((tm, tn), lambda i, j, k: (i, j))
    has_add = add is not None

    def body(*refs):
        a_ref, b_ref = refs[0], refs[1]
        c_ref = refs[2] if has_add else None
        o_ref = refs[3] if has_add else refs[2]
        part = lax.dot_general(a_ref[...].astype(BF16), b_ref[...].astype(BF16), dims, preferred_element_type=F32)
        if scale is not None:
            assert nk == 1 and not has_add
            part = part * jnp.where(pl.program_id(1) < scale[0], scale[1], 1.0)
        if nk == 1:
            if has_add:
                part = part + c_ref[...]
            o_ref[...] = part.astype(out_dtype)
        else:
            acc = refs[-1]
            k = pl.program_id(2)

            @pl.when(k == 0)
            def _():
                acc[...] = part

            @pl.when(k > 0)
            def _():
                acc[...] += part

            @pl.when(k == nk - 1)
            def _():
                r = acc[...]
                if has_add:
                    r = r + c_ref[...]
                o_ref[...] = r.astype(out_dtype)

    in_specs = [a_spec, b_spec] + ([o_spec] if has_add else [])
    args = (a, b) + ((add,) if has_add else ())
    return pl.pallas_call(
        body, name=name, grid=(M // tm, N // tn, nk), in_specs=in_specs, out_specs=o_spec,
        out_shape=jax.ShapeDtypeStruct((M, N), out_dtype),
        scratch_shapes=[pltpu.VMEM((tm, tn), F32)] if nk > 1 else [],
        compiler_params=_params("parallel", "parallel", "arbitrary"),
    )(*args)


def _rms_fwd_val(x, g):
    r = lax.rsqrt(jnp.mean(x * x, axis=-1, keepdims=True) + RMS_EPS)
    return (x * r) * g


def _rms_bwd_val(dy, x, g):
    r = lax.rsqrt(jnp.mean(x * x, axis=-1, keepdims=True) + RMS_EPS)
    xn = x * r
    gdy = g * dy
    dx = r * (gdy - xn * jnp.mean(gdy * xn, axis=-1, keepdims=True))
    return dx, dy * xn


def _rope_tables(pos, invf, inverse):
    ang = pos * invf
    cos, sin = jnp.cos(ang), jnp.sin(ang)
    lane = lax.broadcasted_iota(jnp.int32, ang.shape, 1)
    first = (lane >= MLA_NOPE) & (lane < MLA_NOPE + MLA_ROPE // 2)
    second = (lane >= MLA_NOPE + MLA_ROPE // 2) & (lane < MLA_QK)
    sgn = -1.0 if inverse else 1.0
    sa = jnp.where(first, -sgn * sin, 0.0)
    sb = jnp.where(second, sgn * sin, 0.0)
    return cos, sa, sb


def _rope_val(x, cos, sa, sb):
    half = MLA_ROPE // 2
    return x * cos + pltpu.roll(x, LANES - half, 1) * sa + pltpu.roll(x, half, 1) * sb


def _head_sum_bcast(v, n_heads):
    parts = []
    for h in range(n_heads):
        s = jnp.sum(v[:, h * LANES:(h + 1) * LANES], axis=1, keepdims=True)
        parts.append(jnp.broadcast_to(s, (v.shape[0], LANES)))
    return parts


def _row_spec(t, w):
    return pl.BlockSpec((t, w), lambda i: (i, 0))


def _acc_spec(w, rows=1):
    return pl.BlockSpec((rows, w), lambda i: (0, 0))


def _rmsnorm(x, g, out_dtype, name):
    S, W = x.shape
    t = min(ROW_TILE, S)

    def body(x_ref, g_ref, o_ref):
        o_ref[...] = _rms_fwd_val(x_ref[...], g_ref[...]).astype(out_dtype)

    return pl.pallas_call(body, name=name, grid=(S // t,), in_specs=[_row_spec(t, W), _acc_spec(W)],
                          out_specs=_row_spec(t, W), out_shape=jax.ShapeDtypeStruct((S, W), out_dtype),
                          compiler_params=_params("parallel"))(x, g)


def _mla_prep(proj_a, pos, invf, q_norm, kv_norm):
    S = proj_a.shape[0]
    t = ROW_TILE

    def body(a_ref, pos_ref, invf_ref, qn_ref, kvn_ref, cq_ref, ckv_ref, kpe_ref):
        a = a_ref[...]
        cq_ref[...] = _rms_fwd_val(a[:, 0:Q_RANK], qn_ref[...]).astype(BF16)
        ckv_ref[...] = _rms_fwd_val(a[:, Q_RANK + LANES:], kvn_ref[...]).astype(BF16)
        cos, sa, sb = _rope_tables(pos_ref[...], invf_ref[...], False)
        kpe_ref[...] = _rope_val(a[:, Q_RANK:Q_RANK + LANES], cos, sa, sb)

    return pl.pallas_call(
        body, name="mla_prep", grid=(S // t,),
        in_specs=[_row_spec(t, 768), _row_spec(t, 1), _acc_spec(LANES), _acc_spec(Q_RANK), _acc_spec(KV_RANK)],
        out_specs=[_row_spec(t, Q_RANK), _row_spec(t, KV_RANK), _row_spec(t, LANES)],
        out_shape=[jax.ShapeDtypeStruct((S, Q_RANK), BF16), jax.ShapeDtypeStruct((S, KV_RANK), BF16),
                   jax.ShapeDtypeStruct((S, LANES), F32)],
        compiler_params=_params("parallel"))(proj_a, pos, invf, q_norm, kv_norm)


def _qk_final(q_pre, k_pre, kpe, pos, invf):
    S, W = q_pre.shape
    t = ROW_TILE
    scale = MLA_QK ** -0.5

    def body(q_ref, k_ref, kpe_ref, pos_ref, invf_ref, qo_ref, ko_ref):
        cos, sa, sb = _rope_tables(pos_ref[...], invf_ref[...], False)
        kpe_v = kpe_ref[...]
        for h in range(MLA_HEADS):
            sl = slice(h * LANES, (h + 1) * LANES)
            qo_ref[:, sl] = (_rope_val(q_ref[:, sl], cos, sa, sb) * scale).astype(BF16)
            ko_ref[:, sl] = (k_ref[:, sl] + kpe_v).astype(BF16)

    return pl.pallas_call(
        body, name="qk_final", grid=(S // t,),
        in_specs=[_row_spec(t, W), _row_spec(t, W), _row_spec(t, LANES), _row_spec(t, 1), _acc_spec(LANES)],
        out_specs=[_row_spec(t, W), _row_spec(t, W)],
        out_shape=[jax.ShapeDtypeStruct((S, W), BF16)] * 2,
        compiler_params=_params("parallel"))(q_pre, k_pre, kpe, pos, invf)


def _mla_bwd_prep(dq, dk, pos, invf):
    S, W = dq.shape
    t = ROW_TILE
    scale = MLA_QK ** -0.5

    def body(dq_ref, dk_ref, pos_ref, invf_ref, dqp_ref, dkpe_ref):
        cos, sa, sb = _rope_tables(pos_ref[...], invf_ref[...], True)
        tot = jnp.zeros((t, LANES), F32)
        for h in range(MLA_HEADS):
            sl = slice(h * LANES, (h + 1) * LANES)
            dqp_ref[:, sl] = _rope_val(dq_ref[:, sl] * scale, cos, sa, sb).astype(BF16)
            tot = tot + dk_ref[:, sl]
        lane = lax.broadcasted_iota(jnp.int32, tot.shape, 1)
        tot = jnp.where((lane >= MLA_NOPE) & (lane < MLA_QK), tot, 0.0)
        dkpe_ref[...] = _rope_val(tot, cos, sa, sb)

    return pl.pallas_call(
        body, name="mla_bwd_prep", grid=(S // t,),
        in_specs=[_row_spec(t, W), _row_spec(t, W), _row_spec(t, 1), _acc_spec(LANES)],
        out_specs=[_row_spec(t, W), _row_spec(t, LANES)],
        out_shape=[jax.ShapeDtypeStruct((S, W), BF16), jax.ShapeDtypeStruct((S, LANES), F32)],
        compiler_params=_params("parallel"))(dq, dk, pos, invf)


def _mla_norm_bwd(dcq, dckv_a, dckv_b, dkpe, proj_a, q_norm, kv_norm):
    S = proj_a.shape[0]
    t = ROW_TILE

    def body(dcq_ref, da_ref, db_ref, dkpe_ref, a_ref, qn_ref, kvn_ref, o_ref, dqn_ref, dkvn_ref):
        i = pl.program_id(0)
        a = a_ref[...]
        dxq, gq = _rms_bwd_val(dcq_ref[...], a[:, 0:Q_RANK], qn_ref[...])
        dxkv, gkv = _rms_bwd_val(da_ref[...] + db_ref[...], a[:, Q_RANK + LANES:], kvn_ref[...])
        o_ref[:, 0:Q_RANK] = dxq.astype(BF16)
        o_ref[:, Q_RANK:Q_RANK + LANES] = dkpe_ref[...].astype(BF16)
        o_ref[:, Q_RANK + LANES:] = dxkv.astype(BF16)

        @pl.when(i == 0)
        def _():
            dqn_ref[...] = jnp.zeros_like(dqn_ref)
            dkvn_ref[...] = jnp.zeros_like(dkvn_ref)

        dqn_ref[...] += jnp.sum(gq, axis=0, keepdims=True)
        dkvn_ref[...] += jnp.sum(gkv, axis=0, keepdims=True)

    return pl.pallas_call(
        body, name="mla_norm_bwd", grid=(S // t,),
        in_specs=[_row_spec(t, Q_RANK), _row_spec(t, KV_RANK), _row_spec(t, KV_RANK), _row_spec(t, LANES),
                  _row_spec(t, 768), _acc_spec(Q_RANK), _acc_spec(KV_RANK)],
        out_specs=[_row_spec(t, 768), _acc_spec(Q_RANK), _acc_spec(KV_RANK)],
        out_shape=[jax.ShapeDtypeStruct((S, 768), BF16), jax.ShapeDtypeStruct((1, Q_RANK), F32),
                   jax.ShapeDtypeStruct((1, KV_RANK), F32)],
        compiler_params=_params("arbitrary"))(dcq, dckv_a, dckv_b, dkpe, proj_a, q_norm, kv_norm)


def _dil_mix(o_list, lse_list):
    S = o_list[0].shape[0]
    t = ROW_TILE

    def body(o0, o1, o2, l0, l1, l2, y_ref):
        ls = [l0[...], l1[...], l2[...]]
        m = jnp.maximum(jnp.maximum(ls[0], ls[1]), ls[2])
        es = [jnp.exp(l - m) for l in ls]
        den = es[0] + es[1] + es[2]
        y = (es[0] / den) * o0[...] + (es[1] / den) * o1[...] + (es[2] / den) * o2[...]
        y_ref[...] = y.astype(BF16)

    return pl.pallas_call(
        body, name="dil_mix", grid=(S // t,), in_specs=[_row_spec(t, DIL_W)] * 6, out_specs=_row_spec(t, DIL_W),
        out_shape=jax.ShapeDtypeStruct((S, DIL_W), BF16), compiler_params=_params("parallel"))(*o_list, *lse_list)


def _dil_mix_bwd(dy, o_list, lse_list):
    S = dy.shape[0]
    t = ROW_TILE

    def body(dy_ref, o0, o1, o2, l0, l1, l2, d0, d1, d2, e0, e1, e2):
        ls = [l0[...], l1[...], l2[...]]
        os_ = [o0[...], o1[...], o2[...]]
        m = jnp.maximum(jnp.maximum(ls[0], ls[1]), ls[2])
        es = [jnp.exp(l - m) for l in ls]
        den = es[0] + es[1] + es[2]
        ws = [e / den for e in es]
        dyv = dy_ref[...]
        y = ws[0] * os_[0] + ws[1] * os_[1] + ws[2] * os_[2]
        b = jnp.concatenate(_head_sum_bcast(dyv * y, DIL_HPG), axis=1)
        for w, d_ref, e_ref in zip(ws, (d0, d1, d2), (e0, e1, e2)):
            d_ref[...] = (w * dyv).astype(BF16)
            e_ref[...] = w * b

    return pl.pallas_call(
        body, name="dil_mix_bwd", grid=(S // t,), in_specs=[_row_spec(t, DIL_W)] * 7, out_specs=[_row_spec(t, DIL_W)] * 6,
        out_shape=[jax.ShapeDtypeStruct((S, DIL_W), BF16)] * 3 + [jax.ShapeDtypeStruct((S, DIL_W), F32)] * 3,
        compiler_params=_params("parallel"))(dy, *o_list, *lse_list)


def _delta(do, o, n_heads, name):
    S, W = do.shape
    t = ROW_TILE

    def body(do_ref, o_ref, d_ref):
        prod = do_ref[...].astype(F32) * o_ref[...].astype(F32)
        d_ref[...] = jnp.concatenate(_head_sum_bcast(prod, n_heads), axis=1)

    return pl.pallas_call(body, name=name, grid=(S // t,), in_specs=[_row_spec(t, W)] * 2, out_specs=_row_spec(t, W),
                          out_shape=jax.ShapeDtypeStruct((S, W), F32), compiler_params=_params("parallel"))(do, o)


def _merge(proj_g, b_gate, b_list):
    S = proj_g.shape[0]
    t = ROW_TILE

    def body(g_ref, b_ref, y0, y1, y2, o_ref):
        acc = jnp.zeros((t, D_MODEL), F32)
        for i, y in enumerate((y0, y1, y2)):
            sl = slice(i * D_MODEL, (i + 1) * D_MODEL)
            acc = acc + jax.nn.sigmoid(g_ref[:, sl] + b_ref[:, sl]) * y[...]
        o_ref[...] = acc.astype(BF16)

    return pl.pallas_call(
        body, name="merge", grid=(S // t,),
        in_specs=[_row_spec(t, 3 * D_MODEL), _acc_spec(3 * D_MODEL)] + [_row_spec(t, D_MODEL)] * 3,
        out_specs=_row_spec(t, D_MODEL), out_shape=jax.ShapeDtypeStruct((S, D_MODEL), BF16),
        compiler_params=_params("parallel"))(proj_g, b_gate, *b_list)


def _merge_bwd(dmerged, proj_g, b_gate, b_list):
    S = proj_g.shape[0]
    t = ROW_TILE

    def body(dm_ref, g_ref, b_ref, y0, y1, y2, d0, d1, d2, dz_ref, db_ref):
        i = pl.program_id(0)

        @pl.when(i == 0)
        def _():
            db_ref[...] = jnp.zeros_like(db_ref)

        dm = dm_ref[...]
        for k, (y, d_ref) in enumerate(zip((y0, y1, y2), (d0, d1, d2))):
            sl = slice(k * D_MODEL, (k + 1) * D_MODEL)
            s = jax.nn.sigmoid(g_ref[:, sl] + b_ref[:, sl])
            d_ref[...] = (s * dm).astype(BF16)
            dz = dm * y[...] * (s * (1.0 - s))
            dz_ref[:, sl] = dz.astype(BF16)
            db_ref[:, sl] += jnp.sum(dz, axis=0, keepdims=True)

    return pl.pallas_call(
        body, name="merge_bwd", grid=(S // t,),
        in_specs=[_row_spec(t, D_MODEL), _row_spec(t, 3 * D_MODEL), _acc_spec(3 * D_MODEL)] + [_row_spec(t, D_MODEL)] * 3,
        out_specs=[_row_spec(t, D_MODEL)] * 3 + [_row_spec(t, 3 * D_MODEL), _acc_spec(3 * D_MODEL)],
        out_shape=[jax.ShapeDtypeStruct((S, D_MODEL), BF16)] * 3
        + [jax.ShapeDtypeStruct((S, 3 * D_MODEL), BF16), jax.ShapeDtypeStruct((1, 3 * D_MODEL), F32)],
        compiler_params=_params("arbitrary"))(dmerged, proj_g, b_gate, *b_list)


def _norm2(o, x, g_post, g_pre):
    S = x.shape[0]
    t = ROW_TILE

    def body(o_ref, x_ref, gp_ref, gf_ref, x1_ref, h2_ref):
        x1 = x_ref[...] + _rms_fwd_val(o_ref[...], gp_ref[...])
        x1_ref[...] = x1
        h2_ref[...] = _rms_fwd_val(x1, gf_ref[...]).astype(BF16)

    return pl.pallas_call(
        body, name="norm2", grid=(S // t,),
        in_specs=[_row_spec(t, D_MODEL)] * 2 + [_acc_spec(D_MODEL)] * 2, out_specs=[_row_spec(t, D_MODEL)] * 2,
        out_shape=[jax.ShapeDtypeStruct((S, D_MODEL), F32), jax.ShapeDtypeStruct((S, D_MODEL), BF16)],
        compiler_params=_params("parallel"))(o, x, g_post, g_pre)


def _norm2_bwd(dx2, dh2, x1, o, g_pre, g_post):
    S = x1.shape[0]
    t = ROW_TILE

    def body(dx2_ref, dh2_ref, x1_ref, o_ref, gf_ref, gp_ref, dx1_ref, do_ref, dgf_ref, dgp_ref):
        i = pl.program_id(0)

        @pl.when(i == 0)
        def _():
            dgf_ref[...] = jnp.zeros_like(dgf_ref)
            dgp_ref[...] = jnp.zeros_like(dgp_ref)

        d1, gf = _rms_bwd_val(dh2_ref[...], x1_ref[...], gf_ref[...])
        dx1 = dx2_ref[...] + d1
        dx1_ref[...] = dx1
        do, gp = _rms_bwd_val(dx1, o_ref[...], gp_ref[...])
        do_ref[...] = do.astype(BF16)
        dgf_ref[...] += jnp.sum(gf, axis=0, keepdims=True)
        dgp_ref[...] += jnp.sum(gp, axis=0, keepdims=True)

    return pl.pallas_call(
        body, name="norm2_bwd", grid=(S // t,),
        in_specs=[_row_spec(t, D_MODEL)] * 4 + [_acc_spec(D_MODEL)] * 2,
        out_specs=[_row_spec(t, D_MODEL)] * 2 + [_acc_spec(D_MODEL)] * 2,
        out_shape=[jax.ShapeDtypeStruct((S, D_MODEL), F32), jax.ShapeDtypeStruct((S, D_MODEL), BF16),
                   jax.ShapeDtypeStruct((1, D_MODEL), F32), jax.ShapeDtypeStruct((1, D_MODEL), F32)],
        compiler_params=_params("arbitrary"))(dx2, dh2, x1, o, g_pre, g_post)


def _norm1_bwd(dx1, dh, x, g):
    S = x.shape[0]
    t = ROW_TILE

    def body(dx1_ref, dh_ref, x_ref, g_ref, dx_ref, dg_ref):
        i = pl.program_id(0)

        @pl.when(i == 0)
        def _():
            dg_ref[...] = jnp.zeros_like(dg_ref)

        d, gg = _rms_bwd_val(dh_ref[...], x_ref[...], g_ref[...])
        dx_ref[...] = dx1_ref[...] + d
        dg_ref[...] += jnp.sum(gg, axis=0, keepdims=True)

    return pl.pallas_call(
        body, name="norm1_bwd", grid=(S // t,),
        in_specs=[_row_spec(t, D_MODEL)] * 3 + [_acc_spec(D_MODEL)], out_specs=[_row_spec(t, D_MODEL), _acc_spec(D_MODEL)],
        out_shape=[jax.ShapeDtypeStruct((S, D_MODEL), F32), jax.ShapeDtypeStruct((1, D_MODEL), F32)],
        compiler_params=_params("arbitrary"))(dx1, dh, x, g)


def _gain_grad(dy, x, name):
    R, W = x.shape

    def body(dy_ref, x_ref, dg_ref):
        xv = x_ref[...]
        r = lax.rsqrt(jnp.mean(xv * xv, axis=-1, keepdims=True) + RMS_EPS)
        dg_ref[...] = jnp.sum(dy_ref[...] * (xv * r), axis=0, keepdims=True)

    return pl.pallas_call(body, name=name, grid=(1,), in_specs=[_full((R, W))] * 2, out_specs=_full((1, W)),
                          out_shape=jax.ShapeDtypeStruct((1, W), F32), compiler_params=_params("arbitrary"))(dy, x)


def _loss_head(f, x1, tgt, g):
    S = f.shape[0]
    t = ROW_TILE

    def body(f_ref, x1_ref, t_ref, g_ref, loss_ref, dx2_ref, df_ref, dg_ref):
        i = pl.program_id(0)

        @pl.when(i == 0)
        def _():
            loss_ref[...] = jnp.zeros_like(loss_ref)
            dg_ref[...] = jnp.zeros_like(dg_ref)

        fv, gv = f_ref[...], g_ref[...]
        err = x1_ref[...] + _rms_fwd_val(fv, gv) - t_ref[...]
        part = jnp.sum(jnp.mean(err * err, axis=-1, keepdims=True), axis=0, keepdims=True)
        loss_ref[...] += jnp.broadcast_to(0.5 * part, loss_ref.shape)
        dx2 = err * (1.0 / D_MODEL)
        dx2_ref[...] = dx2
        df, gg = _rms_bwd_val(dx2, fv, gv)
        df_ref[...] = df.astype(BF16)
        dg_ref[...] += jnp.sum(gg, axis=0, keepdims=True)

    return pl.pallas_call(
        body, name="loss_head", grid=(S // t,),
        in_specs=[_row_spec(t, D_MODEL)] * 3 + [_acc_spec(D_MODEL)],
        out_specs=[_acc_spec(LANES, 8), _row_spec(t, D_MODEL), _row_spec(t, D_MODEL), _acc_spec(D_MODEL)],
        out_shape=[jax.ShapeDtypeStruct((8, LANES), F32), jax.ShapeDtypeStruct((S, D_MODEL), F32),
                   jax.ShapeDtypeStruct((S, D_MODEL), BF16), jax.ShapeDtypeStruct((1, D_MODEL), F32)],
        compiler_params=_params("arbitrary"))(f, x1, tgt, g)


CONV_TC = 1408
CONV_TT = 256
HALO = 8


def _shift_down(u, halo, first):
    row = lax.broadcasted_iota(jnp.int32, u.shape, 0)
    h6 = jnp.where(first, 0.0, halo[HALO - 2:HALO - 1, :])
    h7 = jnp.where(first, 0.0, halo[HALO - 1:HALO, :])
    s1 = jnp.where(row == 0, h7, pltpu.roll(u, 1, 0))
    s2 = jnp.where(row == 0, h6, jnp.where(row == 1, h7, pltpu.roll(u, 2, 0)))
    return s1, s2


def _conv_specs(tt, n_c, n_t, lead):
    def halo_row(i):
        return jnp.maximum(i * (tt // HALO) - 1, 0) if lead else jnp.minimum((i + 1) * (tt // HALO), n_t * (tt // HALO) - 1)
    return [
        pl.BlockSpec((tt, CONV_TC), lambda j, i: (i, j)),
        pl.BlockSpec((tt, CONV_TC), lambda j, i: (i, j + n_c)),
        pl.BlockSpec((HALO, CONV_TC), lambda j, i: (halo_row(i), j)),
        pl.BlockSpec((HALO, CONV_TC), lambda j, i: (halo_row(i), j + n_c)),
    ]


def _conv_z(ug, uv, hg, hv, w_g, w_v, b_g, b_v, first):
    g1, g2 = _shift_down(ug, hg, first)
    v1, v2 = _shift_down(uv, hv, first)
    zg = b_g + w_g[0:1, :] * g2
    zg = zg + w_g[1:2, :] * g1
    zg = zg + w_g[2:3, :] * ug
    zv = b_v + w_v[0:1, :] * v2
    zv = zv + w_v[1:2, :] * v1
    zv = zv + w_v[2:3, :] * uv
    return zg, zv, (g2, g1, ug), (v2, v1, uv)


def _conv_fwd(u, conv_w, conv_b):
    S = u.shape[0]
    tt = min(CONV_TT, S)
    n_c, n_t = D_FF // CONV_TC, S // tt
    wspec = [pl.BlockSpec((3, CONV_TC), lambda j, i: (0, j)), pl.BlockSpec((3, CONV_TC), lambda j, i: (0, j + n_c)),
             pl.BlockSpec((1, CONV_TC), lambda j, i: (0, j)), pl.BlockSpec((1, CONV_TC), lambda j, i: (0, j + n_c))]

    def body(ug_ref, uv_ref, hg_ref, hv_ref, wg_ref, wv_ref, bg_ref, bv_ref, a_ref):
        first = pl.program_id(1) == 0
        zg, zv, _, _ = _conv_z(ug_ref[...], uv_ref[...], hg_ref, hv_ref, wg_ref, wv_ref, bg_ref[...], bv_ref[...], first)
        a_ref[...] = (zg * jax.nn.sigmoid(zg) * zv).astype(BF16)

    return pl.pallas_call(
        body, name="conv_fwd", grid=(n_c, n_t), in_specs=_conv_specs(tt, n_c, n_t, True) + wspec,
        out_specs=pl.BlockSpec((tt, CONV_TC), lambda j, i: (i, j)), out_shape=jax.ShapeDtypeStruct((S, D_FF), BF16),
        compiler_params=_params("parallel", "parallel"))(u, u, u, u, conv_w, conv_w, conv_b, conv_b)


def _conv_bwd_dz(da, u, conv_w, conv_b):
    S = u.shape[0]
    tt = min(CONV_TT, S)
    n_c, n_t = D_FF // CONV_TC, S // tt
    wspec = [pl.BlockSpec((3, CONV_TC), lambda j, i: (0, j)), pl.BlockSpec((3, CONV_TC), lambda j, i: (0, j + n_c)),
             pl.BlockSpec((1, CONV_TC), lambda j, i: (0, j)), pl.BlockSpec((1, CONV_TC), lambda j, i: (0, j + n_c))]
    tile = pl.BlockSpec((tt, CONV_TC), lambda j, i: (i, j))
    tile_v = pl.BlockSpec((tt, CONV_TC), lambda j, i: (i, j + n_c))

    def body(da_ref, ug_ref, uv_ref, hg_ref, hv_ref, wg_ref, wv_ref, bg_ref, bv_ref,
             dzg_ref, dzv_ref, dwg_ref, dwv_ref, dbg_ref, dbv_ref):
        i = pl.program_id(1)
        zg, zv, gs, vs = _conv_z(ug_ref[...], uv_ref[...], hg_ref, hv_ref, wg_ref, wv_ref, bg_ref[...], bv_ref[...], i == 0)
        dav = da_ref[...]
        sg = jax.nn.sigmoid(zg)
        dzv = dav * (zg * sg)
        dzg = dav * zv * (sg * (1.0 + zg * (1.0 - sg)))
        dzg_ref[...] = dzg
        dzv_ref[...] = dzv

        @pl.when(i == 0)
        def _():
            for r in (dwg_ref, dwv_ref, dbg_ref, dbv_ref):
                r[...] = jnp.zeros_like(r)

        for k in range(3):
            dwg_ref[k:k + 1, :] += jnp.sum(dzg * gs[k], axis=0, keepdims=True)
            dwv_ref[k:k + 1, :] += jnp.sum(dzv * vs[k], axis=0, keepdims=True)
        dbg_ref[...] += jnp.sum(dzg, axis=0, keepdims=True)
        dbv_ref[...] += jnp.sum(dzv, axis=0, keepdims=True)

    outs = pl.pallas_call(
        body, name="conv_bwd_dz", grid=(n_c, n_t), in_specs=[tile] + _conv_specs(tt, n_c, n_t, True) + wspec,
        out_specs=[tile, tile] + [pl.BlockSpec((3, CONV_TC), lambda j, i: (0, j))] * 2 + [pl.BlockSpec((1, CONV_TC), lambda j, i: (0, j))] * 2,
        out_shape=[jax.ShapeDtypeStruct((S, D_FF), F32)] * 2 + [jax.ShapeDtypeStruct((3, D_FF), F32)] * 2
        + [jax.ShapeDtypeStruct((1, D_FF), F32)] * 2,
        compiler_params=_params("parallel", "arbitrary"))(da, u, u, u, u, conv_w, conv_w, conv_b, conv_b)
    dzg, dzv, dwg, dwv, dbg, dbv = outs
    return dzg, dzv, jnp.concatenate([dwg, dwv], axis=1), jnp.concatenate([dbg, dbv], axis=1)


def _conv_bwd_du(dzg, dzv, conv_w):
    S = dzg.shape[0]
    tt = min(CONV_TT, S)
    n_c, n_t = D_FF // CONV_TC, S // tt
    steps = tt // HALO

    def nxt(i):
        return jnp.minimum((i + 1) * steps, n_t * steps - 1)

    tile_g = pl.BlockSpec((tt, CONV_TC), lambda h, j, i: (i * (1 - h), j * (1 - h)))
    tile_v = pl.BlockSpec((tt, CONV_TC), lambda h, j, i: (i * h, j * h))
    halo_g = pl.BlockSpec((HALO, CONV_TC), lambda h, j, i: (nxt(i) * (1 - h), j * (1 - h)))
    halo_v = pl.BlockSpec((HALO, CONV_TC), lambda h, j, i: (nxt(i) * h, j * h))
    wsp = pl.BlockSpec((3, CONV_TC), lambda h, j, i: (0, j + h * n_c))

    def body(zg_ref, zv_ref, hg_ref, hv_ref, w_ref, du_ref):
        h = pl.program_id(0)
        last = pl.program_id(2) == n_t - 1
        z = jnp.where(h == 0, zg_ref[...], zv_ref[...])
        hal = jnp.where(h == 0, hg_ref[...], hv_ref[...])
        row = lax.broadcasted_iota(jnp.int32, z.shape, 0)
        h0 = jnp.where(last, 0.0, hal[0:1, :])
        h1 = jnp.where(last, 0.0, hal[1:2, :])
        u1 = jnp.where(row == tt - 1, h0, pltpu.roll(z, tt - 1, 0))
        u2 = jnp.where(row == tt - 1, h1, jnp.where(row == tt - 2, h0, pltpu.roll(z, tt - 2, 0)))
        du = w_ref[2:3, :] * z + w_ref[1:2, :] * u1 + w_ref[0:1, :] * u2
        du_ref[...] = du.astype(BF16)

    return pl.pallas_call(
        body, name="conv_bwd_du", grid=(2, n_c, n_t), in_specs=[tile_g, tile_v, halo_g, halo_v, wsp],
        out_specs=pl.BlockSpec((tt, CONV_TC), lambda h, j, i: (i, j + h * n_c)),
        out_shape=jax.ShapeDtypeStruct((S, 2 * D_FF), BF16),
        compiler_params=_params("parallel", "parallel", "parallel"))(dzg, dzv, dzg, dzv, conv_w)


BAND = 128


MEM_TQ = 512


def _mem_fwd(q, kv, *, name):
    S, W = q.shape
    M = kv.shape[0]
    nh = W // LANES
    tq = min(MEM_TQ, S)

    def body(q_ref, k_ref, v_ref, o_ref, l_ref):
        s = lax.dot_general(q_ref[...], k_ref[...].astype(BF16), NT, preferred_element_type=F32)
        m = jnp.max(s, axis=1, keepdims=True)
        p = jnp.exp(s - m)
        l = jnp.sum(p, axis=1, keepdims=True)
        o_ref[...] = (lax.dot_general(p.astype(BF16), v_ref[...].astype(BF16), NN, preferred_element_type=F32) / l).astype(BF16)
        l_ref[...] = jnp.broadcast_to(m + jnp.log(l), (tq, LANES))

    blk = pl.BlockSpec((tq, LANES), lambda hh, i: (i, hh))
    return pl.pallas_call(
        body, name=name, grid=(nh, S // tq),
        in_specs=[blk, pl.BlockSpec((M, LANES), lambda hh, i: (0, hh)), pl.BlockSpec((M, LANES), lambda hh, i: (0, hh + nh))],
        out_specs=[blk, blk], out_shape=[jax.ShapeDtypeStruct((S, W), BF16), jax.ShapeDtypeStruct((S, W), F32)],
        compiler_params=_params("parallel", "parallel"))(q, kv, kv)


def _mem_bwd(q, kv, do, lse, delta, *, scale, name):
    S, W = q.shape
    M = kv.shape[0]
    nh = W // LANES
    tq = min(MEM_TQ, S)

    def body(q_ref, k_ref, v_ref, do_ref, l_ref, d_ref, dq_ref, dk_ref, dv_ref):
        i = pl.program_id(1)

        @pl.when(i == 0)
        def _():
            dk_ref[...] = jnp.zeros_like(dk_ref)
            dv_ref[...] = jnp.zeros_like(dv_ref)

        qv = q_ref[...]
        kv_, vv = k_ref[...].astype(BF16), v_ref[...].astype(BF16)
        dov = do_ref[...].astype(BF16)
        s = lax.dot_general(qv, kv_, NT, preferred_element_type=F32)
        p = jnp.exp(s - l_ref[...][:, 0:1])
        dp = lax.dot_general(dov, vv, NT, preferred_element_type=F32)
        ds = (p * (dp - d_ref[...][:, 0:1])).astype(BF16)
        dq_ref[...] = lax.dot_general(ds, kv_, NN, preferred_element_type=F32) * scale
        dk_ref[...] += lax.dot_general(ds, qv, TN, preferred_element_type=F32)
        dv_ref[...] += lax.dot_general(p.astype(BF16), dov, TN, preferred_element_type=F32)

    blk = pl.BlockSpec((tq, LANES), lambda hh, i: (i, hh))
    kblk = pl.BlockSpec((M, LANES), lambda hh, i: (0, hh))
    vblk = pl.BlockSpec((M, LANES), lambda hh, i: (0, hh + nh))
    dq, dk, dv = pl.pallas_call(
        body, name=name, grid=(nh, S // tq), in_specs=[blk, kblk, vblk, blk, blk, blk], out_specs=[blk, kblk, kblk],
        out_shape=[jax.ShapeDtypeStruct((S, W), F32), jax.ShapeDtypeStruct((M, W), F32), jax.ShapeDtypeStruct((M, W), F32)],
        compiler_params=_params("parallel", "arbitrary"))(q, kv, kv, do, lse, delta)
    return dq, jnp.concatenate([dk, dv], axis=1)


CAUSAL_BLOCK = 512
STRIP = 32


def _causal_fwd(q, k, v, *, name):
    S, W = q.shape
    T = min(CAUSAL_BLOCK, S // 2)
    n_strips = T // STRIP

    def body(q_ref, k_ref, v_ref, o_ref, lse_ref, s0, s1, p0, p1, a0, a1, acc_scr):
        i = pl.program_id(1)
        s_scr, p_scr, a_scr = (s0, s1), (p0, p1), (a0, a1)

        def rows(j):
            return pl.ds(pl.multiple_of(j * T, T), T)

        def scores(j, slot):
            s_scr[slot][...] = lax.dot_general(q_ref[...], k_ref[rows(j), :], NT, preferred_element_type=F32)

        def softmax(slot, stats, diag):
            def strip(r):
                s = s_scr[slot][r * STRIP:(r + 1) * STRIP, :]
                if diag:
                    row = r * STRIP + lax.broadcasted_iota(jnp.int32, s.shape, 0)
                    s = jnp.where(row >= lax.broadcasted_iota(jnp.int32, s.shape, 1), s, NEG_INF)
                return s

            m_new = [jnp.maximum(m_old, jnp.max(strip(r), axis=1, keepdims=True)) for r, (m_old, _) in enumerate(stats)]
            new = []
            for r, (m_old, l_old) in enumerate(stats):
                rs = slice(r * STRIP, (r + 1) * STRIP)
                p = jnp.exp(strip(r) - m_new[r])
                alpha = jnp.exp(m_old - m_new[r])
                new.append((m_new[r], alpha * l_old + jnp.sum(p, axis=1, keepdims=True)))
                a_scr[slot][rs, :] = alpha
                p_scr[slot][rs, :] = p.astype(BF16)
            return tuple(new)

        def values(j, slot):
            acc_scr[...] = a_scr[slot][...] * acc_scr[...] + lax.dot_general(p_scr[slot][...], v_ref[rows(j), :], NN,
                                                                            preferred_element_type=F32)

        def trip(j, stats, mine, other):
            scores(j + 1, other)
            stats = softmax(mine, stats, False)
            values(jnp.maximum(j - 1, 0), other)
            return stats

        def pair(jj, stats):
            return trip(2 * jj + 1, trip(2 * jj, stats, 0, 1), 1, 0)

        def last(stats, mine, other):
            stats = softmax(mine, stats, True)
            values(jnp.maximum(i - 1, 0), other)
            values(i, mine)
            for r, (m, l) in enumerate(stats):
                rs = slice(r * STRIP, (r + 1) * STRIP)
                o_ref[rs, :] = (acc_scr[rs, :] / l).astype(BF16)
                lse_ref[rs, :] = jnp.broadcast_to(m + jnp.log(l), (STRIP, LANES))

        acc_scr[...] = jnp.zeros_like(acc_scr)
        p1[...] = jnp.zeros_like(p1)
        a1[...] = jnp.ones_like(a1)
        scores(0, 0)
        init = tuple((jnp.full((STRIP, 1), NEG_INF, F32), jnp.zeros((STRIP, 1), F32)) for _ in range(n_strips))
        stats = lax.fori_loop(0, i // 2, pair, init)

        @pl.when(i % 2 == 1)
        def _():
            last(trip(i - 1, stats, 0, 1), 1, 0)

        @pl.when(i % 2 == 0)
        def _():
            last(stats, 0, 1)

    blk = pl.BlockSpec((T, LANES), lambda hh, i: (i, hh))
    whole = pl.BlockSpec((S, LANES), lambda hh, i: (0, hh))
    return pl.pallas_call(
        body, name=name, grid=(W // LANES, S // T), in_specs=[blk, whole, whole], out_specs=[blk, blk],
        out_shape=[jax.ShapeDtypeStruct((S, W), BF16), jax.ShapeDtypeStruct((S, W), F32)],
        scratch_shapes=[pltpu.VMEM((T, T), F32)] * 2 + [pltpu.VMEM((T, T), BF16)] * 2 + [pltpu.VMEM((T, 1), F32)] * 2
        + [pltpu.VMEM((T, LANES), F32)],
        compiler_params=_params("parallel", "arbitrary"))(q, k, v)


def _causal_bwd(q, k, v, do, lse, delta, *, name):
    S, W = q.shape
    T = min(CAUSAL_BLOCK, S // 2)
    nq = S // T
    n_strips, n_col = T // STRIP, T // LANES

    def body(q_ref, k_ref, v_ref, do_ref, l_ref, d_ref, dq_ref, dk_ref, dv_ref, s0, s1, e0, e1, p0, p1, g0, g1):
        j = pl.program_id(1)
        s_scr, e_scr, p_scr, g_scr = (s0, s1), (e0, e1), (p0, p1), (g0, g1)

        @pl.when(j == 0)
        def _():
            dq_ref[...] = jnp.zeros_like(dq_ref)

        dk_ref[...] = jnp.zeros_like(dk_ref)
        dv_ref[...] = jnp.zeros_like(dv_ref)

        def rows(i):
            return pl.ds(pl.multiple_of(jnp.minimum(i, nq - 1) * T, T), T)

        def products(i, slot):
            r = rows(i)
            s_scr[slot][...] = lax.dot_general(q_ref[r, :], k_ref[...], NT, preferred_element_type=F32)
            e_scr[slot][...] = lax.dot_general(do_ref[r, :].astype(BF16), v_ref[...], NT, preferred_element_type=F32)

        def pointwise(i, slot, diag):
            base = pl.multiple_of(i * T, T)
            for r in range(n_strips):
                rs = slice(r * STRIP, (r + 1) * STRIP)
                lse_r = l_ref[pl.ds(base + r * STRIP, STRIP), :]
                del_r = d_ref[pl.ds(base + r * STRIP, STRIP), :]
                for c in range(n_col):
                    cs = slice(c * LANES, (c + 1) * LANES)
                    if diag and c * LANES > (r + 1) * STRIP - 1:
                        p_scr[slot][rs, cs] = jnp.zeros((STRIP, LANES), BF16)
                        g_scr[slot][rs, cs] = jnp.zeros((STRIP, LANES), BF16)
                        continue
                    sv = s_scr[slot][rs, cs]
                    if diag and (c + 1) * LANES - 1 > r * STRIP:
                        row = r * STRIP + lax.broadcasted_iota(jnp.int32, sv.shape, 0)
                        col = c * LANES + lax.broadcasted_iota(jnp.int32, sv.shape, 1)
                        sv = jnp.where(row >= col, sv, NEG_INF)
                    p = jnp.exp(sv - lse_r)
                    p_scr[slot][rs, cs] = p.astype(BF16)
                    g_scr[slot][rs, cs] = (p * (e_scr[slot][rs, cs] - del_r)).astype(BF16)

        def gradients(i, slot):
            r = rows(i)
            qi, doi = q_ref[r, :], do_ref[r, :].astype(BF16)
            g = g_scr[slot][...]
            dv_ref[...] += lax.dot_general(p_scr[slot][...], doi, TN, preferred_element_type=F32)
            dk_ref[...] += lax.dot_general(g, qi, TN, preferred_element_type=F32)
            dq_ref[r, :] += lax.dot_general(g, k_ref[...], NN, preferred_element_type=F32)

        p1[...] = jnp.zeros_like(p1)
        g1[...] = jnp.zeros_like(g1)
        products(j, 0)
        products(j + 1, 1)
        pointwise(j, 0, True)
        gradients(j, 1)

        def trip(i, mine, other):
            products(i + 1, other)
            pointwise(i, mine, False)
            gradients(i - 1, other)

        def pair(t, _):
            trip(j + 1 + 2 * t, 1, 0)
            trip(j + 2 + 2 * t, 0, 1)
            return 0

        n_rest = nq - 1 - j
        lax.fori_loop(0, n_rest // 2, pair, 0)

        @pl.when(n_rest % 2 == 1)
        def _():
            trip(nq - 1, 1, 0)
            gradients(nq - 1, 1)

        @pl.when(n_rest % 2 == 0)
        def _():
            gradients(nq - 1, 0)

    blk = pl.BlockSpec((T, LANES), lambda hh, j: (j, hh))
    whole = pl.BlockSpec((S, LANES), lambda hh, j: (0, hh))
    return pl.pallas_call(
        body, name=name, grid=(W // LANES, S // T), in_specs=[whole, blk, blk, whole, whole, whole],
        out_specs=[whole, blk, blk], out_shape=[jax.ShapeDtypeStruct((S, W), F32)] * 3,
        scratch_shapes=[pltpu.VMEM((T, T), F32)] * 4 + [pltpu.VMEM((T, T), BF16)] * 4,
        compiler_params=_params("parallel", "arbitrary"))(q, k, v, do, lse, delta)


BAND_TQ = 512


def _band_window(i, sub, nsub, L, q_ref, k_ref, v_ref, slope):
    kw = min(2 * BAND, L)
    n = i * nsub + sub
    k0 = 0 if kw == L else pl.multiple_of(jnp.maximum(n - 1, 0) * BAND, BAND)
    win = pl.ds(k0, kw)
    qs = q_ref[sub * BAND:(sub + 1) * BAND, :]
    kwv, vwv = k_ref[win, :], v_ref[win, :]
    s = lax.dot_general(qs, kwv, NT, preferred_element_type=F32)
    dist = (n * BAND + lax.broadcasted_iota(jnp.int32, s.shape, 0)) - (k0 + lax.broadcasted_iota(jnp.int32, s.shape, 1))
    s = jnp.where((dist >= 0) & (dist <= BAND), s - slope * dist.astype(F32), NEG_INF)
    return win, qs, kwv, vwv, s


def _band_fwd(q, k, v, slopes, *, n_heads, qcol, kcol, vcol, L, slope_mul, o_shape, name):
    tq = min(BAND_TQ, L)
    nsub = tq // BAND

    def body(sl_ref, q_ref, k_ref, v_ref, o_ref, l_ref):
        hh, i = pl.program_id(0), pl.program_id(1)
        slope = sl_ref[hh % DIL_HPG] * slope_mul
        for sub in range(nsub):
            _, _, _, vwv, s = _band_window(i, sub, nsub, L, q_ref, k_ref, v_ref, slope)
            m = jnp.max(s, axis=1, keepdims=True)
            p = jnp.exp(s - m)
            l = jnp.sum(p, axis=1, keepdims=True)
            rows = slice(sub * BAND, (sub + 1) * BAND)
            o_ref[rows, :] = lax.dot_general(p.astype(BF16), vwv, NN, preferred_element_type=F32) / l
            l_ref[rows, :] = jnp.broadcast_to(m + jnp.log(l), (BAND, LANES))

    whole = lambda col: pl.BlockSpec((L, LANES), lambda hh, i: (0, col(hh)))
    o_spec = pl.BlockSpec((tq, LANES), lambda hh, i: (i, hh))
    return pl.pallas_call(
        body, name=name, grid=(n_heads, L // tq),
        in_specs=[pl.BlockSpec(memory_space=pltpu.SMEM), pl.BlockSpec((tq, LANES), lambda hh, i: (i, qcol(hh))), whole(kcol), whole(vcol)],
        out_specs=[o_spec, o_spec], out_shape=[jax.ShapeDtypeStruct(o_shape, F32)] * 2,
        compiler_params=_params("parallel", "arbitrary"))(slopes, q, k, v)


def _band_bwd(q, k, v, do, lse, delta, slopes, *, n_heads, qcol, kcol, vcol, L, scale, slope_mul, d_shape, name):
    tq = min(BAND_TQ, L)
    nsub = tq // BAND
    n_steps = L // tq

    def body(sl_ref, q_ref, k_ref, v_ref, do_ref, l_ref, d_ref, dq_ref, dk_ref, dv_ref, dk_acc, dv_acc):
        hh, i = pl.program_id(0), pl.program_id(1)
        slope = sl_ref[hh % DIL_HPG] * slope_mul

        @pl.when(i == 0)
        def _():
            dk_acc[...] = jnp.zeros_like(dk_acc)
            dv_acc[...] = jnp.zeros_like(dv_acc)

        for sub in range(nsub):
            win, qs, kwv, vwv, s = _band_window(i, sub, nsub, L, q_ref, k_ref, v_ref, slope)
            rows = slice(sub * BAND, (sub + 1) * BAND)
            dos = do_ref[rows, :]
            p = jnp.exp(s - l_ref[rows, :][:, 0:1])
            dp = lax.dot_general(dos, vwv, NT, preferred_element_type=F32)
            ds = (p * (dp - d_ref[rows, :][:, 0:1])).astype(BF16)
            dq_ref[rows, :] = (lax.dot_general(ds, kwv, NN, preferred_element_type=F32) * scale).astype(BF16)
            dk_acc[win, :] += lax.dot_general(ds, qs, TN, preferred_element_type=F32)
            dv_acc[win, :] += lax.dot_general(p.astype(BF16), dos, TN, preferred_element_type=F32)

        @pl.when(i == n_steps - 1)
        def _():
            dk_ref[...] = dk_acc[...].astype(BF16)
            dv_ref[...] = dv_acc[...].astype(BF16)

    whole = lambda col: pl.BlockSpec((L, LANES), lambda hh, i: (0, col(hh)))
    blk = pl.BlockSpec((tq, LANES), lambda hh, i: (i, hh))
    ident = lambda hh: hh
    return pl.pallas_call(
        body, name=name, grid=(n_heads, n_steps),
        in_specs=[pl.BlockSpec(memory_space=pltpu.SMEM), pl.BlockSpec((tq, LANES), lambda hh, i: (i, qcol(hh))), whole(kcol), whole(vcol),
                  blk, blk, blk],
        out_specs=[blk, whole(ident), whole(ident)], out_shape=[jax.ShapeDtypeStruct(d_shape, BF16)] * 3,
        scratch_shapes=[pltpu.VMEM((L, LANES), F32)] * 2,
        compiler_params=_params("parallel", "arbitrary"))(slopes, q, k, v, do, lse, delta)


def _pad_heads(w, n_heads, width, axis):
    shp = w.shape
    new = shp[:axis] + (n_heads, width) + shp[axis + 1:]
    pad = [(0, 0)] * len(new)
    pad[axis + 1] = (0, LANES - width)
    out = jnp.pad(w.reshape(new), pad)
    return out.reshape(shp[:axis] + (n_heads * LANES,) + shp[axis + 1:])


def _unpad_heads(w, n_heads, width, axis):
    shp = w.shape
    new = shp[:axis] + (n_heads, LANES) + shp[axis + 1:]
    out = lax.slice_in_dim(w.reshape(new), 0, width, axis=axis + 1)
    return out.reshape(shp[:axis] + (n_heads * width,) + shp[axis + 1:])


def _alibi_slopes():
    s = jnp.exp2(-8.0 * jnp.arange(1, DIL_HEADS + 1, dtype=F32) / DIL_HEADS)
    return s.reshape(DIL_HPG, DIL_GROUPS).T


def _local_step(x, mem, positions, tgt, W):
    S = x.shape[0]
    pos = positions.reshape(S, 1).astype(F32)
    half = MLA_ROPE // 2
    inv_freq = ROPE_THETA ** (-jnp.arange(half, dtype=F32) / half)
    invf = jnp.zeros((1, LANES), F32).at[0, MLA_NOPE:MLA_NOPE + half].set(inv_freq).at[0, MLA_NOPE + half:MLA_QK].set(inv_freq)
    slopes = _alibi_slopes()

    w_in = W["w_in"]
    zc = lambda n: jnp.zeros((D_MODEL, n), BF16)
    w_a = jnp.concatenate([w_in[:, :OFF_Q], zc(MLA_NOPE), w_in[:, OFF_KV:OFF_KR], zc(LANES - MLA_QK), w_in[:, OFF_Q:OFF_KV]], axis=1)
    w_d, w_m, w_g = w_in[:, OFF_KR:OFF_DIL], w_in[:, OFF_DIL:OFF_MEMQ], w_in[:, OFF_MEMQ:]
    w_uq_p = _pad_heads(W["w_uq"], MLA_HEADS, MLA_QK, 1)
    ukv = W["w_ukv"].reshape(KV_RANK, MLA_HEADS, 2 * MLA_NOPE)
    w_uk_p = _pad_heads(ukv[:, :, :MLA_NOPE].reshape(KV_RANK, -1), MLA_HEADS, MLA_NOPE, 1)
    w_uv_p = _pad_heads(ukv[:, :, MLA_NOPE:].reshape(KV_RANK, -1), MLA_HEADS, MLA_NOPE, 1)
    w_br_mla_p = _pad_heads(W["w_br_mla"], MLA_HEADS, MLA_NOPE, 0)

    h = _rmsnorm(x, W["g_pre_mix"], BF16, "norm1")
    proj_a = _matmul(h, w_a, tn=768, name="proj_a")
    att_scale = LANES ** -0.5
    proj_d = _matmul(h, w_d, tn=DIL_GROUPS * DIL_W, out_dtype=BF16, scale=(1, att_scale), name="proj_d")
    proj_m = _matmul(h, w_m, tn=MEM_W, out_dtype=BF16, scale=(1, att_scale), name="proj_m")
    proj_g = _matmul(h, w_g, tn=1536, name="proj_g")

    cq_n, ckv_n, kpe = _mla_prep(proj_a, pos, invf, W["mla_q_norm"], W["mla_kv_norm"])
    q_pre = _matmul(cq_n, w_uq_p, tn=1024, name="mla_q")
    k_pre = _matmul(ckv_n, w_uk_p, tn=1024, name="mla_k")
    v_mla = _matmul(ckv_n, w_uv_p, tn=1024, out_dtype=BF16, name="mla_v")
    q_mla, k_mla = _qk_final(q_pre, k_pre, kpe, pos, invf)
    ident = lambda hh: hh
    o_mla, lse_mla = _causal_fwd(q_mla, k_mla, v_mla, name="attn_mla_fwd")

    n_dil_cols = 3 * DIL_HEADS
    o_dil, lse_dil = [], []
    for g, (window, dil) in enumerate(DIL_PAIRS):
        L = S // dil
        view = proj_d.reshape(L, dil * 3 * DIL_HEADS * LANES)
        col = lambda part, g=g: (lambda hh: (hh // DIL_HPG) * n_dil_cols + (part * DIL_GROUPS + g) * DIL_HPG + hh % DIL_HPG)
        o, lse = _band_fwd(view, view, view, slopes[g], n_heads=dil * DIL_HPG, qcol=col(0), kcol=col(1), vcol=col(2), L=L,
                           slope_mul=float(dil), o_shape=(L, dil * DIL_W), name=f"attn_dil{g}_fwd")
        o_dil.append(o.reshape(S, DIL_W))
        lse_dil.append(lse.reshape(S, DIL_W))
    y_dil = _dil_mix(o_dil, lse_dil)

    mem_n = _rmsnorm(mem, W["g_mem"], BF16, "mem_norm")
    kv_mem = _matmul(mem_n, W["w_mem_kv"], name="mem_kv")
    o_mem, lse_mem = _mem_fwd(proj_m, kv_mem, name="attn_mem_fwd")

    b_mla = _matmul(o_mla, w_br_mla_p, name="br_mla")
    b_dil = _matmul(y_dil, W["w_br_dil"], name="br_dil")
    b_mem = _matmul(o_mem, W["w_br_mem"], name="br_mem")
    merged = _merge(proj_g, W["b_gate"], [b_mla, b_dil, b_mem])
    o_proj = _matmul(merged, W["w_o"], name="o_proj")
    x1, h2 = _norm2(o_proj, x, W["g_post_mix"], W["g_pre_ffn"])

    u = _matmul(h2, W["w_ffn_up"], tn=1408, name="ffn_up")
    act = _conv_fwd(u, W["conv_w"], W["conv_b"])
    f = _matmul(act, W["w_ffn_down"], name="ffn_down")
    loss8, dx2, df, dg_post_ffn = _loss_head(f, x1, tgt, W["g_post_ffn"])
    loss = loss8[0, 0]

    G = {"g_post_ffn": dg_post_ffn}
    d_act = _matmul(df, W["w_ffn_down"], mode="nt", tn=1408, name="d_act")
    G["w_ffn_down"] = _matmul(act, df, mode="tn", tm=1408, tk=1024, name="dw_ffn_down")
    dzg, dzv, G["conv_w"], G["conv_b"] = _conv_bwd_dz(d_act, u, W["conv_w"], W["conv_b"])
    du = _conv_bwd_du(dzg, dzv, W["conv_w"])
    dh2 = _matmul(du, W["w_ffn_up"], mode="nt", tk=2816, name="d_h2")
    G["w_ffn_up"] = _matmul(h2, du, mode="tn", tn=1408, tk=1024, name="dw_ffn_up")
    dx1, do_proj, G["g_pre_ffn"], G["g_post_mix"] = _norm2_bwd(dx2, dh2, x1, o_proj, W["g_pre_ffn"], W["g_post_mix"])
    dmerged = _matmul(do_proj, W["w_o"], mode="nt", name="d_merged")
    G["w_o"] = _matmul(merged, do_proj, mode="tn", tk=1024, name="dw_o")
    db_mla, db_dil, db_mem, dproj_g, G["b_gate"] = _merge_bwd(dmerged, proj_g, W["b_gate"], [b_mla, b_dil, b_mem])

    dy_mem = _matmul(db_mem, W["w_br_mem"], mode="nt", name="d_y_mem")
    G["w_br_mem"] = _matmul(o_mem, db_mem, mode="tn", tk=1024, name="dw_br_mem")
    delta_mem = _delta(dy_mem, o_mem, MEM_HEADS, "delta_mem")
    dq_mem, dkv_mem = _mem_bwd(proj_m, kv_mem, dy_mem, lse_mem, delta_mem, scale=att_scale, name="attn_mem_bwd")
    G["w_mem_kv"] = _matmul(mem_n, dkv_mem, mode="tn", name="dw_mem_kv")
    dmem_n = _matmul(dkv_mem, W["w_mem_kv"], mode="nt", name="d_mem_n")
    G["g_mem"] = _gain_grad(dmem_n, mem, "dg_mem")

    dy_dil = _matmul(db_dil, W["w_br_dil"], mode="nt", name="d_y_dil")
    G["w_br_dil"] = _matmul(y_dil, db_dil, mode="tn", tk=1024, name="dw_br_dil")
    mix = _dil_mix_bwd(dy_dil, o_dil, lse_dil)
    do_dil, dl_dil = mix[:3], mix[3:]
    d_parts = [[None] * DIL_GROUPS for _ in range(3)]
    for g, (window, dil) in enumerate(DIL_PAIRS):
        L = S // dil
        view = proj_d.reshape(L, dil * 3 * DIL_HEADS * LANES)
        col = lambda part, g=g: (lambda hh: (hh // DIL_HPG) * n_dil_cols + (part * DIL_GROUPS + g) * DIL_HPG + hh % DIL_HPG)
        shp = (L, dil * DIL_W)
        dq, dk, dv = _band_bwd(
            view, view, view, do_dil[g].reshape(shp), lse_dil[g].reshape(shp), dl_dil[g].reshape(shp), slopes[g],
            n_heads=dil * DIL_HPG, qcol=col(0), kcol=col(1), vcol=col(2), L=L, scale=att_scale, slope_mul=float(dil),
            d_shape=shp, name=f"attn_dil{g}_bwd")
        for part, d in enumerate((dq, dk, dv)):
            d_parts[part][g] = d.reshape(S, DIL_W)
    dproj_d = jnp.concatenate([d for part in d_parts for d in part], axis=1)

    dy_mla = _matmul(db_mla, w_br_mla_p, mode="nt", name="d_y_mla")
    dw_br_mla_p = _matmul(o_mla, db_mla, mode="tn", tk=1024, name="dw_br_mla")
    G["w_br_mla"] = _unpad_heads(dw_br_mla_p, MLA_HEADS, MLA_NOPE, 0)
    delta_mla = _delta(dy_mla, o_mla, MLA_HEADS, "delta_mla")
    dq_mla, dk_mla, dv_mla = _causal_bwd(q_mla, k_mla, v_mla, dy_mla, lse_mla, delta_mla, name="attn_mla_bwd")
    dq_pre, dkpe = _mla_bwd_prep(dq_mla, dk_mla, pos, invf)
    dcq_n = _matmul(dq_pre, w_uq_p, mode="nt", tn=Q_RANK, name="d_cq")
    G["w_uq"] = _unpad_heads(_matmul(cq_n, dq_pre, mode="tn", tm=Q_RANK, tk=1024, name="dw_uq"), MLA_HEADS, MLA_QK, 1)
    dckv_a = _matmul(dk_mla, w_uk_p, mode="nt", tn=KV_RANK, name="d_ckv_k")
    dckv_b = _matmul(dv_mla, w_uv_p, mode="nt", tn=KV_RANK, name="d_ckv_v")
    dw_uk = _unpad_heads(_matmul(ckv_n, dk_mla, mode="tn", tm=KV_RANK, tk=1024, name="dw_uk"), MLA_HEADS, MLA_NOPE, 1)
    dw_uv = _unpad_heads(_matmul(ckv_n, dv_mla, mode="tn", tm=KV_RANK, tk=1024, name="dw_uv"), MLA_HEADS, MLA_NOPE, 1)
    G["w_ukv"] = jnp.concatenate([dw_uk.reshape(KV_RANK, MLA_HEADS, MLA_NOPE), dw_uv.reshape(KV_RANK, MLA_HEADS, MLA_NOPE)],
                                 axis=2).reshape(KV_RANK, -1)
    dproj_a, G["mla_q_norm"], G["mla_kv_norm"] = _mla_norm_bwd(dcq_n, dckv_a, dckv_b, dkpe, proj_a, W["mla_q_norm"],
                                                              W["mla_kv_norm"])

    dh = _matmul(dproj_a, w_a, mode="nt", name="d_h_a")
    dh = _matmul(dproj_d, w_d, mode="nt", tk=2304, add=dh, name="d_h_d")
    dh = _matmul(dq_mem, w_m, mode="nt", add=dh, name="d_h_m")
    dh = _matmul(dproj_g, w_g, mode="nt", add=dh, name="d_h_g")
    dw_a = _matmul(h, dproj_a, mode="tn", tn=768, tk=1024, name="dw_in_a")
    dw_d = _matmul(h, dproj_d, mode="tn", tn=1536, tk=1024, name="dw_in_d")
    dw_m = _matmul(h, dq_mem, mode="tn", tn=512, tk=1024, name="dw_in_m")
    dw_g = _matmul(h, dproj_g, mode="tn", tn=1536, tk=1024, name="dw_in_g")
    kr0 = Q_RANK + MLA_NOPE
    G["w_in"] = jnp.concatenate([dw_a[:, :Q_RANK], dw_a[:, Q_RANK + LANES:], dw_a[:, kr0:kr0 + MLA_ROPE], dw_d, dw_m, dw_g], axis=1)
    grad_x, G["g_pre_mix"] = _norm1_bwd(dx1, dh, x, W["g_pre_mix"])
    return loss, grad_x, G


WEIGHTS = ["g_pre_mix", "w_in", "b_gate", "mla_q_norm", "w_uq", "mla_kv_norm", "w_ukv", "g_mem", "w_mem_kv", "w_br_mla",
           "w_br_dil", "w_br_mem", "w_o", "g_post_mix", "g_pre_ffn", "w_ffn_up", "conv_w", "conv_b", "w_ffn_down", "g_post_ffn"]
GROUPS = [
    [("w_in", (D_MODEL, D_IN), 1)],
    [("w_uq", (Q_RANK, MLA_HEADS * MLA_QK), 1)],
    [("w_ukv", (KV_RANK, MLA_HEADS * 2 * MLA_NOPE), 1), ("w_br_mla", (MLA_HEADS * MLA_NOPE, D_MODEL), 1),
     ("w_br_dil", (DIL_W, D_MODEL), 1), ("w_br_mem", (MEM_W, D_MODEL), 1)],
    [("w_mem_kv", (D_MODEL, 2 * MEM_W), 0), ("w_o", (D_MODEL, D_MODEL), 0), ("w_ffn_down", (D_FF, D_MODEL), 0)],
    [("w_ffn_up", (D_MODEL, 2 * D_FF), 1)],
]
CONV_W = ("conv_w", (3, 2 * D_FF), 1)
REPLICATED = [("g_pre_mix", D_MODEL), ("b_gate", 3 * D_MODEL), ("mla_q_norm", Q_RANK), ("mla_kv_norm", KV_RANK), ("g_mem", D_MODEL),
              ("g_post_mix", D_MODEL), ("g_pre_ffn", D_MODEL), ("conv_b", 2 * D_FF), ("g_post_ffn", D_MODEL)]
SMALL_ROWS = 256
CONV_AT = sum(n for _, n in REPLICATED)
LOSS_AT = CONV_AT + 3 * 2 * D_FF


def _shard_shape(shape, axis):
    return tuple(d // N_CHIPS if a == axis else d for a, d in enumerate(shape))


def _group_shape(grp):
    shapes = [_shard_shape(shape, axis) for _, shape, axis in grp]
    assert len({s[1] for s in shapes}) == 1
    return sum(s[0] for s in shapes), shapes[0][1]


def _member_shards(a, axis):
    r, c = a.shape
    if axis == 0:
        return a.reshape(N_CHIPS, r // N_CHIPS, c)
    return a.reshape(r, N_CHIPS, c // N_CHIPS).transpose(1, 0, 2)


def _member_full(s, axis):
    n, r, c = s.shape
    if axis == 0:
        return s.reshape(n * r, c)
    return s.transpose(1, 0, 2).reshape(r, n * c)


def _my_weight_groups(w):
    out = []
    for grp in GROUPS:
        rows, width = _group_shape(grp)
        out.append(jnp.concatenate([w[name].astype(BF16) for name, _, _ in grp], axis=0).reshape(2, rows // 2, width))
    return out


def _full_weights(gathered, conv_all):
    out = {}
    for grp, ga in zip(GROUPS, gathered):
        ga = ga.reshape(N_CHIPS, -1, ga.shape[-1])
        off = 0
        for name, shape, axis in grp:
            rows = _shard_shape(shape, axis)[0]
            out[name] = _member_full(ga[:, off:off + rows], axis)
            off += rows
    out[CONV_W[0]] = _member_full(conv_all, CONV_W[2])
    return out


def _grad_groups(G):
    out = []
    for grp in GROUPS:
        rows, width = _group_shape(grp)
        a = jnp.concatenate([_member_shards(G[name], axis) for name, _, axis in grp], axis=1)
        out.append(a.reshape(N_CHIPS, 2, rows // 2, width).transpose(1, 0, 2, 3))
    return out


def _pack_small(vals, conv_g=None, loss=None):
    parts = [vals[name].reshape(-1) for name, _ in REPLICATED]
    if conv_g is not None:
        parts += [conv_g.reshape(-1), loss.reshape(1)]
    flat = jnp.concatenate(parts)
    return jnp.pad(flat, (0, SMALL_ROWS * LANES - flat.shape[0])).reshape(SMALL_ROWS, LANES)


def _unpack_small(packed):
    flat = packed.reshape(-1)
    out, off = {}, 0
    for name, n in REPLICATED:
        out[name] = flat[off:off + n].reshape(1, n)
        off += n
    return out


MESH = pl.DeviceIdType.MESH
HBM_SPEC = pl.BlockSpec(memory_space=pltpu.HBM)


def _place():
    x, y, c = lax.axis_index("x"), lax.axis_index("y"), lax.axis_index("c")
    chips = [(1 - x, y), (x, 1 - y), (1 - x, 1 - y)]
    return x, y, c, chips


def _remote(src, dst, send_sems, recv_sems, k, to):
    return pltpu.make_async_remote_copy(src_ref=src, dst_ref=dst, send_sem=send_sems.at[k], recv_sem=recv_sems.at[k],
                                        device_id=to, device_id_type=MESH)


def _gather_weights(groups, wholes, name, collective_id):
    n, m = len(groups), len(wholes)
    arrays = list(groups) + list(wholes)
    hbm = pltpu.MemorySpace.HBM
    srcs = [jax.new_ref(a, memory_space=hbm) for a in arrays]
    outs = [jax.empty_ref(jax.ShapeDtypeStruct((N_CHIPS,) + a.shape, a.dtype), memory_space=hbm) for a in arrays]
    n_sem = 6 * n + 3 * m

    def launch(send_sems, recv_sems):
        x, y, c, chips = _place()
        me = 2 * x + y
        sibling = (x, y, 1 - c)
        barrier = pltpu.get_barrier_semaphore()
        peers = [(px, py, c) for px, py in chips] + [sibling]
        for peer in peers:
            pl.semaphore_signal(barrier, inc=1, device_id=peer, device_id_type=MESH)
        pl.semaphore_wait(barrier, len(peers))
        first = [_remote(srcs[g].at[c], outs[g].at[me, c], send_sems, recv_sems, g * 3 + k, (px, py, c))
                 for k, (px, py) in enumerate(chips) for g in range(n)]
        first += [_remote(srcs[n + w], outs[n + w].at[me], send_sems, recv_sems, 6 * n + w * 3 + k, (px, py, c))
                  for k, (px, py) in enumerate(chips) for w in range(m)]
        for cp in first:
            cp.start()
        passed = []
        for k, (px, py) in enumerate(chips):
            for g in range(n):
                slot = outs[g].at[2 * px + py, c]
                _remote(slot, slot, send_sems, recv_sems, g * 3 + k, (px, py, c)).wait_recv()
                cp = _remote(slot, slot, send_sems, recv_sems, 3 * n + g * 3 + k, sibling)
                cp.start()
                passed.append(cp)
        for k, (px, py) in enumerate(chips):
            for w in range(m):
                slot = outs[n + w].at[2 * px + py]
                _remote(slot, slot, send_sems, recv_sems, 6 * n + w * 3 + k, (px, py, c)).wait_recv()
            for g in range(n):
                slot = outs[g].at[2 * px + py, 1 - c]
                _remote(slot, slot, send_sems, recv_sems, 3 * n + g * 3 + k, sibling).wait_recv()
        for cp in first + passed:
            cp.wait_send()

    pl.kernel(launch, mesh=plsc.ScalarSubcoreMesh(axis_name="sequencer", num_cores=1), name=name,
              scratch_types=(pltpu.SemaphoreType.DMA((n_sem,)), pltpu.SemaphoreType.DMA((n_sem,))),
              compiler_params=pltpu.CompilerParams(collective_id=collective_id))()
    me = 2 * lax.axis_index("x") + lax.axis_index("y")
    res = [lax.dynamic_update_slice_in_dim(o[...], a[None], me, axis=0) for o, a in zip(outs, arrays)]
    return res[:n], res[n:]


def _swap_halves(groups):
    n = len(groups)

    def body(*refs):
        srcs, outs, send_sems, recv_sems = refs[:n], refs[n:2 * n], refs[2 * n], refs[2 * n + 1]
        x, y, c, _ = _place()
        cps = [_remote(srcs[g].at[1 - c], outs[g], send_sems, recv_sems, g, (x, y, 1 - c)) for g in range(n)]
        for cp in cps:
            cp.start()
        for cp in cps:
            cp.wait()

    return pl.pallas_call(
        body, name="comm_swap_halves", in_specs=[HBM_SPEC] * n, out_specs=[HBM_SPEC] * n,
        out_shape=[jax.ShapeDtypeStruct(g.shape[1:], g.dtype) for g in groups],
        scratch_shapes=[pltpu.SemaphoreType.DMA((n,)), pltpu.SemaphoreType.DMA((n,))],
    )(*groups)


def _scatter_partials(groups):
    n = len(groups)

    def body(*refs):
        srcs, outs = refs[:n], refs[n:2 * n]
        send_sems, recv_sems = refs[2 * n:]
        x, y, c, chips = _place()
        sends = [_remote(srcs[g].at[2 * px + py], outs[g].at[k], send_sems, recv_sems, g * 3 + k, (px, py, c))
                 for k, (px, py) in enumerate(chips) for g in range(n)]
        for cp in sends:
            cp.start()
        for k, (px, py) in enumerate(chips):
            for g in range(n):
                slot = outs[g].at[k]
                _remote(slot, slot, send_sems, recv_sems, g * 3 + k, (px, py, c)).wait_recv()
        for cp in sends:
            cp.wait_send()

    return pl.pallas_call(
        body, name="comm_scatter_partials", in_specs=[HBM_SPEC] * n, out_specs=[HBM_SPEC] * n,
        out_shape=[jax.ShapeDtypeStruct((3,) + g.shape[1:], g.dtype) for g in groups],
        scratch_shapes=[pltpu.SemaphoreType.DMA((3 * n,)), pltpu.SemaphoreType.DMA((3 * n,))],
    )(*groups)


def _share_reduced(groups):
    n = len(groups)

    def body(*refs):
        srcs, outs = refs[:n], refs[n:2 * n]
        send_sems, recv_sems = refs[2 * n:]
        x, y, c, _ = _place()
        cps = [_remote(srcs[g], outs[g], send_sems, recv_sems, g, (x, y, 1 - c)) for g in range(n)]
        for cp in cps:
            cp.start()
        for cp in cps:
            cp.wait()

    return pl.pallas_call(
        body, name="comm_share_reduced", in_specs=[HBM_SPEC] * n, out_specs=[HBM_SPEC] * n,
        out_shape=[jax.ShapeDtypeStruct(g.shape, g.dtype) for g in groups],
        scratch_shapes=[pltpu.SemaphoreType.DMA((n,)), pltpu.SemaphoreType.DMA((n,))],
    )(*groups)


def _allreduce_small(v):
    n_dev = 2 * N_CHIPS

    def body(src, out, buf, send_sems, recv_sems):
        x, y, c, _ = _place()
        me = 4 * x + 2 * y + c
        buf[me] = src[...]
        flips = [(k >> 2 & 1, k >> 1 & 1, k & 1) for k in range(1, n_dev)]
        sends = []
        for k, (fx, fy, fc) in enumerate(flips):
            to = ((1 - x) if fx else x, (1 - y) if fy else y, (1 - c) if fc else c)
            cp = _remote(src, buf.at[me], send_sems, recv_sems, k, to)
            cp.start()
            sends.append((cp, to))
        for k, (cp, to) in enumerate(sends):
            slot = buf.at[4 * to[0] + 2 * to[1] + to[2]]
            _remote(slot, slot, send_sems, recv_sems, k, to).wait_recv()
        for cp, _ in sends:
            cp.wait_send()
        acc = buf[0]
        for d in range(1, n_dev):
            acc = acc + buf[d]
        out[...] = acc

    vm = pl.BlockSpec(memory_space=pltpu.VMEM)
    return pl.pallas_call(
        body, name="comm_allreduce_small", in_specs=[vm], out_specs=vm, out_shape=jax.ShapeDtypeStruct(v.shape, F32),
        scratch_shapes=[pltpu.VMEM((n_dev,) + v.shape, F32), pltpu.SemaphoreType.DMA((n_dev - 1,)),
                        pltpu.SemaphoreType.DMA((n_dev - 1,))],
    )(v)


def _row_tile(rows, cap=320):
    return max(t for t in range(16, cap + 1, 16) if rows % t == 0)


def _add_sibling(mine, theirs, core, name):
    _, n, R, C = mine.shape
    t = _row_tile(R)

    def body(core_ref, a_ref, b_ref, o_ref, ob_ref):
        tot = a_ref[...] + b_ref[...]
        o_ref[...] = tot
        ob_ref[...] = tot.astype(BF16)

    sp = pl.BlockSpec((None, t, C), lambda k, i, core_ref: (k, i, 0))
    grid_spec = pltpu.PrefetchScalarGridSpec(
        num_scalar_prefetch=1, grid=(n, R // t),
        in_specs=[pl.BlockSpec((None, None, t, C), lambda k, i, core_ref: (core_ref[0], k, i, 0)), sp], out_specs=[sp, sp])
    return pl.pallas_call(body, name=name, grid_spec=grid_spec,
                          out_shape=[jax.ShapeDtypeStruct((n, R, C), F32), jax.ShapeDtypeStruct((n, R, C), BF16)],
                          compiler_params=_params("parallel", "parallel"))(core, mine, theirs)


def _add_chips(received, own, chip, name):
    n, R, C = received.shape
    t = _row_tile(R)

    def body(chip_ref, r_ref, o_ref, out_ref):
        acc = o_ref[...]
        for k in range(n):
            acc = acc + r_ref[k].astype(F32)
        out_ref[...] = acc

    grid_spec = pltpu.PrefetchScalarGridSpec(
        num_scalar_prefetch=1, grid=(R // t,),
        in_specs=[pl.BlockSpec((n, t, C), lambda i, chip_ref: (0, i, 0)), pl.BlockSpec((None, t, C), lambda i, chip_ref: (chip_ref[0], i, 0))],
        out_specs=pl.BlockSpec((t, C), lambda i, chip_ref: (i, 0)))
    return pl.pallas_call(body, name=name, grid_spec=grid_spec, out_shape=jax.ShapeDtypeStruct((R, C), F32),
                          compiler_params=_params("parallel"))(chip, received, own)


def _adamw(w, g, m, v, name, g_row0=0):
    R, C = w.shape
    t = math.gcd(math.gcd(R, g_row0), 128) if R % 8 == 0 else R
    assert g_row0 % t == 0
    c1 = 1.0 - ADAM_B1 ** ADAM_STEP
    c2 = 1.0 - ADAM_B2 ** ADAM_STEP

    def body(w_ref, g_ref, m_ref, v_ref, go_ref, d_ref, nm_ref, nv_ref):
        gv = g_ref[...]
        nm = ADAM_B1 * m_ref[...] + (1.0 - ADAM_B1) * gv
        nv = ADAM_B2 * v_ref[...] + (1.0 - ADAM_B2) * (gv * gv)
        go_ref[...] = gv
        d_ref[...] = -ADAM_LR * ((nm / c1) / (jnp.sqrt(nv / c2) + ADAM_EPS) + ADAM_WD * w_ref[...])
        nm_ref[...] = nm
        nv_ref[...] = nv

    sp = pl.BlockSpec((t, C), lambda i: (i, 0))
    g_sp = pl.BlockSpec((t, C), lambda i: (i + g_row0 // t, 0))
    return pl.pallas_call(body, name=name, grid=(R // t,), in_specs=[sp, g_sp, sp, sp], out_specs=[sp] * 4,
                          out_shape=[jax.ShapeDtypeStruct((R, C), F32)] * 4, compiler_params=_params("parallel"))(w, g, m, v)


def kernel(x, mem, positions, g_pre_mix, w_in, b_gate, mla_q_norm, w_uq, mla_kv_norm, w_ukv, g_mem, w_mem_kv, w_br_mla, w_br_dil, w_br_mem, w_o, g_post_mix, g_pre_ffn, w_ffn_up, conv_w, conv_b, w_ffn_down, g_post_ffn, loss_target, m_g_pre_mix, m_w_in, m_b_gate, m_mla_q_norm, m_w_uq, m_mla_kv_norm, m_w_ukv, m_g_mem, m_w_mem_kv, m_w_br_mla, m_w_br_dil, m_w_br_mem, m_w_o, m_g_post_mix, m_g_pre_ffn, m_w_ffn_up, m_conv_w, m_conv_b, m_w_ffn_down, m_g_post_ffn, v_g_pre_mix, v_w_in, v_b_gate, v_mla_q_norm, v_w_uq, v_mla_kv_norm, v_w_ukv, v_g_mem, v_w_mem_kv, v_w_br_mla, v_w_br_dil, v_w_br_mem, v_w_o, v_g_post_mix, v_g_pre_ffn, v_w_ffn_up, v_conv_w, v_conv_b, v_w_ffn_down, v_g_post_ffn):
    w_args = (g_pre_mix, w_in, b_gate, mla_q_norm, w_uq, mla_kv_norm, w_ukv, g_mem, w_mem_kv, w_br_mla, w_br_dil, w_br_mem, w_o,
              g_post_mix, g_pre_ffn, w_ffn_up, conv_w, conv_b, w_ffn_down, g_post_ffn)
    m_args = (m_g_pre_mix, m_w_in, m_b_gate, m_mla_q_norm, m_w_uq, m_mla_kv_norm, m_w_ukv, m_g_mem, m_w_mem_kv, m_w_br_mla,
              m_w_br_dil, m_w_br_mem, m_w_o, m_g_post_mix, m_g_pre_ffn, m_w_ffn_up, m_conv_w, m_conv_b, m_w_ffn_down, m_g_post_ffn)
    v_args = (v_g_pre_mix, v_w_in, v_b_gate, v_mla_q_norm, v_w_uq, v_mla_kv_norm, v_w_ukv, v_g_mem, v_w_mem_kv, v_w_br_mla,
              v_w_br_dil, v_w_br_mem, v_w_o, v_g_post_mix, v_g_pre_ffn, v_w_ffn_up, v_conv_w, v_conv_b, v_w_ffn_down, v_g_post_ffn)
    sharded = {name for grp in GROUPS for name, _, _ in grp} | {CONV_W[0]}

    def local(a, name):
        return a[0] if name in sharded else a

    w = {n: local(a, n) for n, a in zip(WEIGHTS, w_args)}
    m = {n: local(a, n) for n, a in zip(WEIGHTS, m_args)}
    v = {n: local(a, n) for n, a in zip(WEIGHTS, v_args)}

    mine = _my_weight_groups(w)
    first, _ = _gather_weights(mine[:1], [], "comm_gather_w_in", 0)
    rest, (conv_all,) = _gather_weights(mine[1:], [w[CONV_W[0]]], "comm_gather_rest", 1)
    full = _full_weights(first + rest, conv_all)
    full.update({name: w[name] for name, _ in REPLICATED})

    loss_local, grad_x, G = _local_step(x[0], mem[0], positions, loss_target[0], full)

    core = lax.axis_index("c").astype(jnp.int32).reshape(1)
    mine = _grad_groups(G)
    theirs = _swap_halves(mine)
    chip = (2 * lax.axis_index("x") + lax.axis_index("y")).astype(jnp.int32).reshape(1)
    partial = [_add_sibling(a, b, core, f"add_sibling_{i}") for i, (a, b) in enumerate(zip(mine, theirs))]
    received = _scatter_partials([p16 for _, p16 in partial])
    reduced = [_add_chips(r, p32, chip, f"add_chips_{i}") for i, (r, (p32, _)) in enumerate(zip(received, partial))]
    first = lax.axis_index("c") == 0
    shard_groups = [jnp.concatenate([jnp.where(first, a, b), jnp.where(first, b, a)], axis=0)
                    for a, b in zip(reduced, _share_reduced(reduced))]
    small = _allreduce_small(_pack_small(G, G[CONV_W[0]], loss_local))
    flat = small.reshape(-1)
    loss = flat[LOSS_AT]
    conv_g = flat[CONV_AT:LOSS_AT].reshape(CONV_W[1])
    conv_cols = CONV_W[1][1] // N_CHIPS
    conv_g = lax.dynamic_slice_in_dim(conv_g, (2 * lax.axis_index("x") + lax.axis_index("y")) * conv_cols, conv_cols, axis=1)

    grads, deltas, new_m, new_v = {}, {}, {}, {}
    for grp, g_all in zip(GROUPS, shard_groups):
        off = 0
        for name, _, _ in grp:
            grads[name], deltas[name], new_m[name], new_v[name] = _adamw(w[name], g_all, m[name], v[name], "adamw_" + name, off)
            off += w[name].shape[0]
    name = CONV_W[0]
    grads[name], deltas[name], new_m[name], new_v[name] = _adamw(w[name], conv_g, m[name], v[name], "adamw_" + name)
    packed = _adamw(_pack_small(w), small, _pack_small(m), _pack_small(v), "adamw_small")
    for dst, packed_small in zip((grads, deltas, new_m, new_v), packed):
        dst.update(_unpack_small(packed_small))

    def out(d, name):
        return d[name][None] if name in sharded else d[name]

    return (loss, grad_x[None], *[out(grads, n) for n in WEIGHTS], *[out(deltas, n) for n in WEIGHTS],
            *[out(new_m, n) for n in WEIGHTS], *[out(new_v, n) for n in WEIGHTS])
```

```python
import functools
import math

import jax
import jax.numpy as jnp
from jax import lax
from jax.experimental import pallas as pl
from jax.experimental.pallas import tpu as pltpu
from jax.experimental.pallas import tpu_sc as plsc

F32 = jnp.float32
BF16 = jnp.bfloat16

D_MODEL = 1024
N_MEM = 256
RMS_EPS = 1e-6
NEG_INF = -1e30
MLA_HEADS = 8
MLA_NOPE = 64
MLA_ROPE = 32
MLA_QK = 96
Q_RANK = 384
KV_RANK = 256
ROPE_THETA = 10000.0
DIL_PAIRS = ((128, 1), (512, 4), (2048, 16))
DIL_GROUPS = 3
DIL_HPG = 4
DIL_HEADS = 12
DIL_W = 512
MEM_HEADS = 4
MEM_W = 512
D_FF = 2816
OFF_Q = 384
OFF_KV = 640
OFF_KR = 672
OFF_DIL = 5280
OFF_MEMQ = 5792
D_IN = 8864
ADAM_LR = 0.001
ADAM_B1 = 0.9
ADAM_B2 = 0.999
ADAM_EPS = 1e-08
ADAM_WD = 0.01
ADAM_STEP = 10

LANES = 128
VMEM_LIMIT = 56 * 1024 * 1024

N_CHIPS = 4
ROW_TILE = 256

NN = (((1,), (0,)), ((), ()))
NT = (((1,), (1,)), ((), ()))
TN = (((0,), (0,)), ((), ()))


def _params(*sem):
    return pltpu.CompilerParams(dimension_semantics=sem, vmem_limit_bytes=VMEM_LIMIT)


def _full(shape):
    return pl.BlockSpec(shape, lambda *_: (0,) * len(shape))


def _matmul(a, b, *, mode="nn", out_dtype=F32, tm=1024, tn=1024, tk=None, add=None, scale=None, name):
    if mode == "nn":
        (M, K), N = a.shape, b.shape[1]
    elif mode == "nt":
        (M, K), N = a.shape, b.shape[0]
    else:
        (K, M), N = a.shape, b.shape[1]
    tm, tn = min(tm, M), min(tn, N)
    tk = K if tk is None else min(tk, K)
    assert M % tm == 0 and N % tn == 0 and K % tk == 0, (name, M, N, K, tm, tn, tk)
    a_spec = pl.BlockSpec((tk, tm), lambda i, j, k: (k, i)) if mode == "tn" else pl.BlockSpec((tm, tk), lambda i, j, k: (i, k))
    b_spec = pl.BlockSpec((tn, tk), lambda i, j, k: (j, k)) if mode == "nt" else pl.BlockSpec((tk, tn), lambda i, j, k: (k, j))
    o_spec = pl.BlockSpec((tm, tn), lambda i, j, k: (i, j))
    return _matmul_blocks(a, b, mode=mode, grid=(M // tm, N // tn, K // tk), a_spec=a_spec, b_spec=b_spec, o_spec=o_spec,
                          out_shape=(M, N), out_dtype=out_dtype, add=add,
                          scale=None if scale is None else (lambda j: j < scale[0], scale[1]), name=name)


def _matmul_blocks(a, b, *, mode, grid, a_spec, b_spec, o_spec, out_shape, out_dtype=F32, add=None, scale=None, name):
    nk = grid[2]
    dims = {"nn": NN, "nt": NT, "tn": TN}[mode]
    tm, tn = o_spec.block_shape
    has_add = add is not None

    def body(*refs):
        a_ref, b_ref = refs[0], refs[1]
        c_ref = refs[2] if has_add else None
        o_ref = refs[3] if has_add else refs[2]
        part = lax.dot_general(a_ref[...].astype(BF16), b_ref[...].astype(BF16), dims, preferred_element_type=F32)
        if scale is not None:
            assert nk == 1 and not has_add
            part = part * jnp.where(scale[0](pl.program_id(1)), scale[1], 1.0)
        if nk == 1:
            if has_add:
                part = part + c_ref[...]
            o_ref[...] = part.astype(out_dtype)
        else:
            acc = refs[-1]
            k = pl.program_id(2)

            @pl.when(k == 0)
            def _():
                acc[...] = part

            @pl.when(k > 0)
            def _():
                acc[...] += part

            @pl.when(k == nk - 1)
            def _():
                r = acc[...]
                if has_add:
                    r = r + c_ref[...]
                o_ref[...] = r.astype(out_dtype)

    in_specs = [a_spec, b_spec] + ([o_spec] if has_add else [])
    args = (a, b) + ((add,) if has_add else ())
    return pl.pallas_call(
        body, name=name, grid=grid, in_specs=in_specs, out_specs=o_spec,
        out_shape=jax.ShapeDtypeStruct(out_shape, out_dtype),
        scratch_shapes=[pltpu.VMEM((tm, tn), F32)] if nk > 1 else [],
        compiler_params=_params("parallel", "parallel", "arbitrary"),
    )(*args)


def _proj_view(h_view, w, d, q_scale, name):
    L, K = h_view.shape[0], h_view.shape[1] // d
    N = w.shape[1]
    tn = N // 3
    tm = min(1024, L)
    return _matmul_blocks(
        h_view, w, mode="nn", grid=(L // tm, 3 * d, 1), a_spec=pl.BlockSpec((tm, K), lambda i, j, k: (i, j // 3)),
        b_spec=pl.BlockSpec((K, tn), lambda i, j, k: (0, j % 3)), o_spec=pl.BlockSpec((tm, tn), lambda i, j, k: (i, j)),
        out_shape=(L, d * N), out_dtype=BF16, scale=(lambda j: j % 3 == 0, q_scale), name=name)


def _dh_view(dp_view, w, d, add, name):
    L, N = dp_view.shape[0], dp_view.shape[1] // d
    K = w.shape[0]
    tm = min(1024, L)
    return _matmul_blocks(
        dp_view, w, mode="nt", grid=(L // tm, d, 1), a_spec=pl.BlockSpec((tm, N), lambda i, j, k: (i, j)),
        b_spec=pl.BlockSpec((K, N), lambda i, j, k: (0, 0)), o_spec=pl.BlockSpec((tm, K), lambda i, j, k: (i, j)),
        out_shape=(L, d * K), add=add, name=name)


def _dw_view(h_view, dp_view, d, name):
    L, K = h_view.shape[0], h_view.shape[1] // d
    N = dp_view.shape[1] // d
    tk = min(1024, L)
    nl = L // tk
    return _matmul_blocks(
        h_view, dp_view, mode="tn", grid=(1, 1, d * nl), a_spec=pl.BlockSpec((tk, K), lambda i, j, k: (k % nl, k // nl)),
        b_spec=pl.BlockSpec((tk, N), lambda i, j, k: (k % nl, k // nl)), o_spec=pl.BlockSpec((K, N), lambda i, j, k: (0, 0)),
        out_shape=(K, N), name=name)


def _rms_fwd_val(x, g):
    r = lax.rsqrt(jnp.mean(x * x, axis=-1, keepdims=True) + RMS_EPS)
    return (x * r) * g


def _rms_bwd_val(dy, x, g):
    r = lax.rsqrt(jnp.mean(x * x, axis=-1, keepdims=True) + RMS_EPS)
    xn = x * r
    gdy = g * dy
    dx = r * (gdy - xn * jnp.mean(gdy * xn, axis=-1, keepdims=True))
    return dx, dy * xn


def _rope_tables(pos, invf, inverse):
    ang = pos * invf
    cos, sin = jnp.cos(ang), jnp.sin(ang)
    lane = lax.broadcasted_iota(jnp.int32, ang.shape, 1)
    first = (lane >= MLA_NOPE) & (lane < MLA_NOPE + MLA_ROPE // 2)
    second = (lane >= MLA_NOPE + MLA_ROPE // 2) & (lane < MLA_QK)
    sgn = -1.0 if inverse else 1.0
    sa = jnp.where(first, -sgn * sin, 0.0)
    sb = jnp.where(second, sgn * sin, 0.0)
    return cos, sa, sb


def _rope_val(x, cos, sa, sb):
    half = MLA_ROPE // 2
    return x * cos + pltpu.roll(x, LANES - half, 1) * sa + pltpu.roll(x, half, 1) * sb


def _head_sum_bcast(v, n_heads):
    parts = []
    for h in range(n_heads):
        s = jnp.sum(v[:, h * LANES:(h + 1) * LANES], axis=1, keepdims=True)
        parts.append(jnp.broadcast_to(s, (v.shape[0], LANES)))
    return parts


def _row_spec(t, w):
    return pl.BlockSpec((t, w), lambda i: (i, 0))


def _acc_spec(w, rows=1):
    return pl.BlockSpec((rows, w), lambda i: (0, 0))


def _rmsnorm(x, g, out_dtype, name):
    S, W = x.shape
    t = min(ROW_TILE, S)

    def body(x_ref, g_ref, o_ref):
        o_ref[...] = _rms_fwd_val(x_ref[...], g_ref[...]).astype(out_dtype)

    return pl.pallas_call(body, name=name, grid=(S // t,), in_specs=[_row_spec(t, W), _acc_spec(W)],
                          out_specs=_row_spec(t, W), out_shape=jax.ShapeDtypeStruct((S, W), out_dtype),
                          compiler_params=_params("parallel"))(x, g)


def _mla_prep(proj_a, pos, invf, q_norm, kv_norm):
    S = proj_a.shape[0]
    t = ROW_TILE

    def body(a_ref, pos_ref, invf_ref, qn_ref, kvn_ref, cq_ref, ckv_ref, kpe_ref):
        a = a_ref[...]
        cq_ref[...] = _rms_fwd_val(a[:, 0:Q_RANK], qn_ref[...]).astype(BF16)
        ckv_ref[...] = _rms_fwd_val(a[:, Q_RANK + LANES:], kvn_ref[...]).astype(BF16)
        cos, sa, sb = _rope_tables(pos_ref[...], invf_ref[...], False)
        kpe_ref[...] = _rope_val(a[:, Q_RANK:Q_RANK + LANES], cos, sa, sb)

    return pl.pallas_call(
        body, name="mla_prep", grid=(S // t,),
        in_specs=[_row_spec(t, 768), _row_spec(t, 1), _acc_spec(LANES), _acc_spec(Q_RANK), _acc_spec(KV_RANK)],
        out_specs=[_row_spec(t, Q_RANK), _row_spec(t, KV_RANK), _row_spec(t, LANES)],
        out_shape=[jax.ShapeDtypeStruct((S, Q_RANK), BF16), jax.ShapeDtypeStruct((S, KV_RANK), BF16),
                   jax.ShapeDtypeStruct((S, LANES), F32)],
        compiler_params=_params("parallel"))(proj_a, pos, invf, q_norm, kv_norm)


def _qk_final(q_pre, k_pre, kpe, pos, invf):
    S, W = q_pre.shape
    t = ROW_TILE
    scale = MLA_QK ** -0.5

    def body(q_ref, k_ref, kpe_ref, pos_ref, invf_ref, qo_ref, ko_ref):
        cos, sa, sb = _rope_tables(pos_ref[...], invf_ref[...], False)
        kpe_v = kpe_ref[...]
        for h in range(MLA_HEADS):
            sl = slice(h * LANES, (h + 1) * LANES)
            qo_ref[:, sl] = (_rope_val(q_ref[:, sl], cos, sa, sb) * scale).astype(BF16)
            ko_ref[:, sl] = (k_ref[:, sl] + kpe_v).astype(BF16)

    return pl.pallas_call(
        body, name="qk_final", grid=(S // t,),
        in_specs=[_row_spec(t, W), _row_spec(t, W), _row_spec(t, LANES), _row_spec(t, 1), _acc_spec(LANES)],
        out_specs=[_row_spec(t, W), _row_spec(t, W)],
        out_shape=[jax.ShapeDtypeStruct((S, W), BF16)] * 2,
        compiler_params=_params("parallel"))(q_pre, k_pre, kpe, pos, invf)


def _mla_bwd_prep(dq, dk, pos, invf):
    S, W = dq.shape
    t = ROW_TILE
    scale = MLA_QK ** -0.5

    def body(dq_ref, dk_ref, pos_ref, invf_ref, dqp_ref, dkpe_ref):
        cos, sa, sb = _rope_tables(pos_ref[...], invf_ref[...], True)
        tot = jnp.zeros((t, LANES), F32)
        for h in range(MLA_HEADS):
            sl = slice(h * LANES, (h + 1) * LANES)
            dqp_ref[:, sl] = _rope_val(dq_ref[:, sl] * scale, cos, sa, sb).astype(BF16)
            tot = tot + dk_ref[:, sl]
        lane = lax.broadcasted_iota(jnp.int32, tot.shape, 1)
        tot = jnp.where((lane >= MLA_NOPE) & (lane < MLA_QK), tot, 0.0)
        dkpe_ref[...] = _rope_val(tot, cos, sa, sb)

    return pl.pallas_call(
        body, name="mla_bwd_prep", grid=(S // t,),
        in_specs=[_row_spec(t, W), _row_spec(t, W), _row_spec(t, 1), _acc_spec(LANES)],
        out_specs=[_row_spec(t, W), _row_spec(t, LANES)],
        out_shape=[jax.ShapeDtypeStruct((S, W), BF16), jax.ShapeDtypeStruct((S, LANES), F32)],
        compiler_params=_params("parallel"))(dq, dk, pos, invf)


def _mla_norm_bwd(dcq, dckv_a, dckv_b, dkpe, proj_a, q_norm, kv_norm):
    S = proj_a.shape[0]
    t = ROW_TILE

    def body(dcq_ref, da_ref, db_ref, dkpe_ref, a_ref, qn_ref, kvn_ref, o_ref, dqn_ref, dkvn_ref):
        i = pl.program_id(0)
        a = a_ref[...]
        dxq, gq = _rms_bwd_val(dcq_ref[...], a[:, 0:Q_RANK], qn_ref[...])
        dxkv, gkv = _rms_bwd_val(da_ref[...] + db_ref[...], a[:, Q_RANK + LANES:], kvn_ref[...])
        o_ref[:, 0:Q_RANK] = dxq.astype(BF16)
        o_ref[:, Q_RANK:Q_RANK + LANES] = dkpe_ref[...].astype(BF16)
        o_ref[:, Q_RANK + LANES:] = dxkv.astype(BF16)

        @pl.when(i == 0)
        def _():
            dqn_ref[...] = jnp.zeros_like(dqn_ref)
            dkvn_ref[...] = jnp.zeros_like(dkvn_ref)

        dqn_ref[...] += jnp.sum(gq, axis=0, keepdims=True)
        dkvn_ref[...] += jnp.sum(gkv, axis=0, keepdims=True)

    return pl.pallas_call(
        body, name="mla_norm_bwd", grid=(S // t,),
        in_specs=[_row_spec(t, Q_RANK), _row_spec(t, KV_RANK), _row_spec(t, KV_RANK), _row_spec(t, LANES),
                  _row_spec(t, 768), _acc_spec(Q_RANK), _acc_spec(KV_RANK)],
        out_specs=[_row_spec(t, 768), _acc_spec(Q_RANK), _acc_spec(KV_RANK)],
        out_shape=[jax.ShapeDtypeStruct((S, 768), BF16), jax.ShapeDtypeStruct((1, Q_RANK), F32),
                   jax.ShapeDtypeStruct((1, KV_RANK), F32)],
        compiler_params=_params("arbitrary"))(dcq, dckv_a, dckv_b, dkpe, proj_a, q_norm, kv_norm)


DILATIONS = tuple(d for _, d in DIL_PAIRS)


def _to_tokens(view, scr, d):
    if d == 1:
        return view
    n, w = view.shape[0], view.shape[1] // d
    for r in range(d):
        for c in range(w // LANES):
            scr[pl.ds(c, 1), pl.ds(r, n, stride=d), :] = view[:, r * w + c * LANES:r * w + (c + 1) * LANES][None]
    return jnp.concatenate([scr[c] for c in range(w // LANES)], axis=1)


def _from_tokens(tok, scr, d, out_ref):
    if d == 1:
        out_ref[...] = tok.astype(out_ref.dtype)
        return
    n, w = tok.shape[0] // d, tok.shape[1]
    for c in range(w // LANES):
        scr[c] = tok[:, c * LANES:(c + 1) * LANES]
    for r in range(d):
        for c in range(w // LANES):
            out_ref[:, r * w + c * LANES:r * w + (c + 1) * LANES] = scr[pl.ds(c, 1), pl.ds(r, n, stride=d), :][0].astype(out_ref.dtype)


def _token_scratch(t, w):
    return pltpu.VMEM((w // LANES, t, LANES), F32)


def _view_spec(t, d):
    return pl.BlockSpec((t // d, d * DIL_W), lambda i: (i, 0))


def _mix_weights(ls):
    m = jnp.maximum(jnp.maximum(ls[0], ls[1]), ls[2])
    es = [jnp.exp(l - m) for l in ls]
    den = es[0] + es[1] + es[2]
    return [e / den for e in es]


def _dil_mix(o_list, lse_list):
    S = o_list[0].shape[0] * DILATIONS[0]
    t = ROW_TILE
    specs = [_view_spec(t, d) for d in DILATIONS]

    def body(o0, o1, o2, l0, l1, l2, y_ref, *scr):
        os_ = [_to_tokens(r[...], scr[g], d) for g, (r, d) in enumerate(zip((o0, o1, o2), DILATIONS))]
        ws = _mix_weights([_to_tokens(r[...], scr[3 + g], d) for g, (r, d) in enumerate(zip((l0, l1, l2), DILATIONS))])
        y_ref[...] = (ws[0] * os_[0] + ws[1] * os_[1] + ws[2] * os_[2]).astype(BF16)

    return pl.pallas_call(
        body, name="dil_mix", grid=(S // t,), in_specs=specs * 2, out_specs=_row_spec(t, DIL_W),
        out_shape=jax.ShapeDtypeStruct((S, DIL_W), BF16), scratch_shapes=[_token_scratch(t, DIL_W)] * 6,
        compiler_params=_params("parallel"))(*o_list, *lse_list)


def _dil_mix_bwd(dy, o_list, lse_list):
    S = dy.shape[0]
    t = ROW_TILE
    specs = [_view_spec(t, d) for d in DILATIONS]

    def body(dy_ref, o0, o1, o2, l0, l1, l2, d0, d1, d2, e0, e1, e2, *scr):
        os_ = [_to_tokens(r[...], scr[g], d) for g, (r, d) in enumerate(zip((o0, o1, o2), DILATIONS))]
        ws = _mix_weights([_to_tokens(r[...], scr[3 + g], d) for g, (r, d) in enumerate(zip((l0, l1, l2), DILATIONS))])
        dyv = dy_ref[...]
        y = ws[0] * os_[0] + ws[1] * os_[1] + ws[2] * os_[2]
        b = jnp.concatenate(_head_sum_bcast(dyv * y, DIL_HPG), axis=1)
        for g, (w, d_ref, e_ref, d) in enumerate(zip(ws, (d0, d1, d2), (e0, e1, e2), DILATIONS)):
            _from_tokens(w * dyv, scr[6 + g], d, d_ref)
            _from_tokens(w * b, scr[9 + g], d, e_ref)

    shapes = [(S // d, d * DIL_W) for d in DILATIONS]
    return pl.pallas_call(
        body, name="dil_mix_bwd", grid=(S // t,), in_specs=[_row_spec(t, DIL_W)] + specs * 2, out_specs=specs * 2,
        out_shape=[jax.ShapeDtypeStruct(s, BF16) for s in shapes] + [jax.ShapeDtypeStruct(s, F32) for s in shapes],
        scratch_shapes=[_token_scratch(t, DIL_W)] * 12, compiler_params=_params("parallel"))(dy, *o_list, *lse_list)


def _delta(do, o, n_heads, name):
    S, W = do.shape
    t = ROW_TILE

    def body(do_ref, o_ref, d_ref):
        prod = do_ref[...].astype(F32) * o_ref[...].astype(F32)
        d_ref[...] = jnp.concatenate(_head_sum_bcast(prod, n_heads), axis=1)

    return pl.pallas_call(body, name=name, grid=(S // t,), in_specs=[_row_spec(t, W)] * 2, out_specs=_row_spec(t, W),
                          out_shape=jax.ShapeDtypeStruct((S, W), F32), compiler_params=_params("parallel"))(do, o)


def _merge(proj_g, b_gate, b_list):
    S = proj_g.shape[0]
    t = ROW_TILE

    def body(g_ref, b_ref, y0, y1, y2, o_ref):
        acc = jnp.zeros((t, D_MODEL), F32)
        for i, y in enumerate((y0, y1, y2)):
            sl = slice(i * D_MODEL, (i + 1) * D_MODEL)
            acc = acc + jax.nn.sigmoid(g_ref[:, sl] + b_ref[:, sl]) * y[...]
        o_ref[...] = acc.astype(BF16)

    return pl.pallas_call(
        body, name="merge", grid=(S // t,),
        in_specs=[_row_spec(t, 3 * D_MODEL), _acc_spec(3 * D_MODEL)] + [_row_spec(t, D_MODEL)] * 3,
        out_specs=_row_spec(t, D_MODEL), out_shape=jax.ShapeDtypeStruct((S, D_MODEL), BF16),
        compiler_params=_params("parallel"))(proj_g, b_gate, *b_list)


def _merge_bwd(dmerged, proj_g, b_gate, b_list):
    S = proj_g.shape[0]
    t = ROW_TILE

    def body(dm_ref, g_ref, b_ref, y0, y1, y2, d0, d1, d2, dz_ref, db_ref):
        i = pl.program_id(0)

        @pl.when(i == 0)
        def _():
            db_ref[...] = jnp.zeros_like(db_ref)

        dm = dm_ref[...]
        for k, (y, d_ref) in enumerate(zip((y0, y1, y2), (d0, d1, d2))):
            sl = slice(k * D_MODEL, (k + 1) * D_MODEL)
            s = jax.nn.sigmoid(g_ref[:, sl] + b_ref[:, sl])
            d_ref[...] = (s * dm).astype(BF16)
            dz = dm * y[...] * (s * (1.0 - s))
            dz_ref[:, sl] = dz.astype(BF16)
            db_ref[:, sl] += jnp.sum(dz, axis=0, keepdims=True)

    return pl.pallas_call(
        body, name="merge_bwd", grid=(S // t,),
        in_specs=[_row_spec(t, D_MODEL), _row_spec(t, 3 * D_MODEL), _acc_spec(3 * D_MODEL)] + [_row_spec(t, D_MODEL)] * 3,
        out_specs=[_row_spec(t, D_MODEL)] * 3 + [_row_spec(t, 3 * D_MODEL), _acc_spec(3 * D_MODEL)],
        out_shape=[jax.ShapeDtypeStruct((S, D_MODEL), BF16)] * 3
        + [jax.ShapeDtypeStruct((S, 3 * D_MODEL), BF16), jax.ShapeDtypeStruct((1, 3 * D_MODEL), F32)],
        compiler_params=_params("arbitrary"))(dmerged, proj_g, b_gate, *b_list)


def _norm2(o, x, g_post, g_pre):
    S = x.shape[0]
    t = ROW_TILE

    def body(o_ref, x_ref, gp_ref, gf_ref, x1_ref, h2_ref):
        x1 = x_ref[...] + _rms_fwd_val(o_ref[...], gp_ref[...])
        x1_ref[...] = x1
        h2_ref[...] = _rms_fwd_val(x1, gf_ref[...]).astype(BF16)

    return pl.pallas_call(
        body, name="norm2", grid=(S // t,),
        in_specs=[_row_spec(t, D_MODEL)] * 2 + [_acc_spec(D_MODEL)] * 2, out_specs=[_row_spec(t, D_MODEL)] * 2,
        out_shape=[jax.ShapeDtypeStruct((S, D_MODEL), F32), jax.ShapeDtypeStruct((S, D_MODEL), BF16)],
        compiler_params=_params("parallel"))(o, x, g_post, g_pre)


def _norm2_bwd(dx2, dh2, x1, o, g_pre, g_post):
    S = x1.shape[0]
    t = ROW_TILE

    def body(dx2_ref, dh2_ref, x1_ref, o_ref, gf_ref, gp_ref, dx1_ref, do_ref, dgf_ref, dgp_ref):
        i = pl.program_id(0)

        @pl.when(i == 0)
        def _():
            dgf_ref[...] = jnp.zeros_like(dgf_ref)
            dgp_ref[...] = jnp.zeros_like(dgp_ref)

        d1, gf = _rms_bwd_val(dh2_ref[...], x1_ref[...], gf_ref[...])
        dx1 = dx2_ref[...] + d1
        dx1_ref[...] = dx1
        do, gp = _rms_bwd_val(dx1, o_ref[...], gp_ref[...])
        do_ref[...] = do.astype(BF16)
        dgf_ref[...] += jnp.sum(gf, axis=0, keepdims=True)
        dgp_ref[...] += jnp.sum(gp, axis=0, keepdims=True)

    return pl.pallas_call(
        body, name="norm2_bwd", grid=(S // t,),
        in_specs=[_row_spec(t, D_MODEL)] * 4 + [_acc_spec(D_MODEL)] * 2,
        out_specs=[_row_spec(t, D_MODEL)] * 2 + [_acc_spec(D_MODEL)] * 2,
        out_shape=[jax.ShapeDtypeStruct((S, D_MODEL), F32), jax.ShapeDtypeStruct((S, D_MODEL), BF16),
                   jax.ShapeDtypeStruct((1, D_MODEL), F32), jax.ShapeDtypeStruct((1, D_MODEL), F32)],
        compiler_params=_params("arbitrary"))(dx2, dh2, x1, o, g_pre, g_post)


def _norm1_bwd(dx1, dh_list, x, g):
    S = x.shape[0]
    t = ROW_TILE
    dils = (1,) + DILATIONS
    assert len(dh_list) == len(dils)

    def body(dx1_ref, *refs):
        dh_refs, (x_ref, g_ref, dx_ref, dg_ref), scr = refs[:len(dils)], refs[len(dils):len(dils) + 4], refs[len(dils) + 4:]
        i = pl.program_id(0)

        @pl.when(i == 0)
        def _():
            dg_ref[...] = jnp.zeros_like(dg_ref)

        dh = dh_refs[0][...]
        for k in range(1, len(dils)):
            dh = dh + _to_tokens(dh_refs[k][...], scr[k - 1], dils[k])
        d, gg = _rms_bwd_val(dh, x_ref[...], g_ref[...])
        dx_ref[...] = dx1_ref[...] + d
        dg_ref[...] += jnp.sum(gg, axis=0, keepdims=True)

    dh_specs = [pl.BlockSpec((t // d, d * D_MODEL), lambda i: (i, 0)) for d in dils]
    return pl.pallas_call(
        body, name="norm1_bwd", grid=(S // t,),
        in_specs=[_row_spec(t, D_MODEL)] + dh_specs + [_row_spec(t, D_MODEL), _acc_spec(D_MODEL)],
        out_specs=[_row_spec(t, D_MODEL), _acc_spec(D_MODEL)],
        out_shape=[jax.ShapeDtypeStruct((S, D_MODEL), F32), jax.ShapeDtypeStruct((1, D_MODEL), F32)],
        scratch_shapes=[_token_scratch(t, D_MODEL)] * (len(dils) - 1),
        compiler_params=_params("arbitrary"))(dx1, *dh_list, x, g)


def _gain_grad(dy, x, name):
    R, W = x.shape

    def body(dy_ref, x_ref, dg_ref):
        xv = x_ref[...]
        r = lax.rsqrt(jnp.mean(xv * xv, axis=-1, keepdims=True) + RMS_EPS)
        dg_ref[...] = jnp.sum(dy_ref[...] * (xv * r), axis=0, keepdims=True)

    return pl.pallas_call(body, name=name, grid=(1,), in_specs=[_full((R, W))] * 2, out_specs=_full((1, W)),
                          out_shape=jax.ShapeDtypeStruct((1, W), F32), compiler_params=_params("arbitrary"))(dy, x)


def _loss_head(f, x1, tgt, g):
    S = f.shape[0]
    t = ROW_TILE

    def body(f_ref, x1_ref, t_ref, g_ref, loss_ref, dx2_ref, df_ref, dg_ref):
        i = pl.program_id(0)

        @pl.when(i == 0)
        def _():
            loss_ref[...] = jnp.zeros_like(loss_ref)
            dg_ref[...] = jnp.zeros_like(dg_ref)

        fv, gv = f_ref[...], g_ref[...]
        err = x1_ref[...] + _rms_fwd_val(fv, gv) - t_ref[...]
        part = jnp.sum(jnp.mean(err * err, axis=-1, keepdims=True), axis=0, keepdims=True)
        loss_ref[...] += jnp.broadcast_to(0.5 * part, loss_ref.shape)
        dx2 = err * (1.0 / D_MODEL)
        dx2_ref[...] = dx2
        df, gg = _rms_bwd_val(dx2, fv, gv)
        df_ref[...] = df.astype(BF16)
        dg_ref[...] += jnp.sum(gg, axis=0, keepdims=True)

    return pl.pallas_call(
        body, name="loss_head", grid=(S // t,),
        in_specs=[_row_spec(t, D_MODEL)] * 3 + [_acc_spec(D_MODEL)],
        out_specs=[_acc_spec(LANES, 8), _row_spec(t, D_MODEL), _row_spec(t, D_MODEL), _acc_spec(D_MODEL)],
        out_shape=[jax.ShapeDtypeStruct((8, LANES), F32), jax.ShapeDtypeStruct((S, D_MODEL), F32),
                   jax.ShapeDtypeStruct((S, D_MODEL), BF16), jax.ShapeDtypeStruct((1, D_MODEL), F32)],
        compiler_params=_params("arbitrary"))(f, x1, tgt, g)


CONV_TC = 1408
CONV_TT = 256
HALO = 8


def _shift_down(u, halo, first):
    row = lax.broadcasted_iota(jnp.int32, u.shape, 0)
    h6 = jnp.where(first, 0.0, halo[HALO - 2:HALO - 1, :])
    h7 = jnp.where(first, 0.0, halo[HALO - 1:HALO, :])
    s1 = jnp.where(row == 0, h7, pltpu.roll(u, 1, 0))
    s2 = jnp.where(row == 0, h6, jnp.where(row == 1, h7, pltpu.roll(u, 2, 0)))
    return s1, s2


def _conv_specs(tt, n_c, n_t, lead):
    def halo_row(i):
        return jnp.maximum(i * (tt // HALO) - 1, 0) if lead else jnp.minimum((i + 1) * (tt // HALO), n_t * (tt // HALO) - 1)
    return [
        pl.BlockSpec((tt, CONV_TC), lambda j, i: (i, j)),
        pl.BlockSpec((tt, CONV_TC), lambda j, i: (i, j + n_c)),
        pl.BlockSpec((HALO, CONV_TC), lambda j, i: (halo_row(i), j)),
        pl.BlockSpec((HALO, CONV_TC), lambda j, i: (halo_row(i), j + n_c)),
    ]


def _conv_z(ug, uv, hg, hv, w_g, w_v, b_g, b_v, first):
    g1, g2 = _shift_down(ug, hg, first)
    v1, v2 = _shift_down(uv, hv, first)
    zg = b_g + w_g[0:1, :] * g2
    zg = zg + w_g[1:2, :] * g1
    zg = zg + w_g[2:3, :] * ug
    zv = b_v + w_v[0:1, :] * v2
    zv = zv + w_v[1:2, :] * v1
    zv = zv + w_v[2:3, :] * uv
    return zg, zv, (g2, g1, ug), (v2, v1, uv)


def _conv_fwd(u, conv_w, conv_b):
    S = u.shape[0]
    tt = min(CONV_TT, S)
    n_c, n_t = D_FF // CONV_TC, S // tt
    wspec = [pl.BlockSpec((3, CONV_TC), lambda j, i: (0, j)), pl.BlockSpec((3, CONV_TC), lambda j, i: (0, j + n_c)),
             pl.BlockSpec((1, CONV_TC), lambda j, i: (0, j)), pl.BlockSpec((1, CONV_TC), lambda j, i: (0, j + n_c))]

    def body(ug_ref, uv_ref, hg_ref, hv_ref, wg_ref, wv_ref, bg_ref, bv_ref, a_ref):
        first = pl.program_id(1) == 0
        zg, zv, _, _ = _conv_z(ug_ref[...], uv_ref[...], hg_ref, hv_ref, wg_ref, wv_ref, bg_ref[...], bv_ref[...], first)
        a_ref[...] = (zg * jax.nn.sigmoid(zg) * zv).astype(BF16)

    return pl.pallas_call(
        body, name="conv_fwd", grid=(n_c, n_t), in_specs=_conv_specs(tt, n_c, n_t, True) + wspec,
        out_specs=pl.BlockSpec((tt, CONV_TC), lambda j, i: (i, j)), out_shape=jax.ShapeDtypeStruct((S, D_FF), BF16),
        compiler_params=_params("parallel", "parallel"))(u, u, u, u, conv_w, conv_w, conv_b, conv_b)


def _conv_bwd_dz(da, u, conv_w, conv_b):
    S = u.shape[0]
    tt = min(CONV_TT, S)
    n_c, n_t = D_FF // CONV_TC, S // tt
    wspec = [pl.BlockSpec((3, CONV_TC), lambda j, i: (0, j)), pl.BlockSpec((3, CONV_TC), lambda j, i: (0, j + n_c)),
             pl.BlockSpec((1, CONV_TC), lambda j, i: (0, j)), pl.BlockSpec((1, CONV_TC), lambda j, i: (0, j + n_c))]
    tile = pl.BlockSpec((tt, CONV_TC), lambda j, i: (i, j))
    tile_v = pl.BlockSpec((tt, CONV_TC), lambda j, i: (i, j + n_c))

    def body(da_ref, ug_ref, uv_ref, hg_ref, hv_ref, wg_ref, wv_ref, bg_ref, bv_ref,
             dzg_ref, dzv_ref, dwg_ref, dwv_ref, dbg_ref, dbv_ref):
        i = pl.program_id(1)
        zg, zv, gs, vs = _conv_z(ug_ref[...], uv_ref[...], hg_ref, hv_ref, wg_ref, wv_ref, bg_ref[...], bv_ref[...], i == 0)
        dav = da_ref[...]
        sg = jax.nn.sigmoid(zg)
        dzv = dav * (zg * sg)
        dzg = dav * zv * (sg * (1.0 + zg * (1.0 - sg)))
        dzg_ref[...] = dzg
        dzv_ref[...] = dzv

        @pl.when(i == 0)
        def _():
            for r in (dwg_ref, dwv_ref, dbg_ref, dbv_ref):
                r[...] = jnp.zeros_like(r)

        for k in range(3):
            dwg_ref[k:k + 1, :] += jnp.sum(dzg * gs[k], axis=0, keepdims=True)
            dwv_ref[k:k + 1, :] += jnp.sum(dzv * vs[k], axis=0, keepdims=True)
        dbg_ref[...] += jnp.sum(dzg, axis=0, keepdims=True)
        dbv_ref[...] += jnp.sum(dzv, axis=0, keepdims=True)

    outs = pl.pallas_call(
        body, name="conv_bwd_dz", grid=(n_c, n_t), in_specs=[tile] + _conv_specs(tt, n_c, n_t, True) + wspec,
        out_specs=[tile, tile] + [pl.BlockSpec((3, CONV_TC), lambda j, i: (0, j))] * 2 + [pl.BlockSpec((1, CONV_TC), lambda j, i: (0, j))] * 2,
        out_shape=[jax.ShapeDtypeStruct((S, D_FF), F32)] * 2 + [jax.ShapeDtypeStruct((3, D_FF), F32)] * 2
        + [jax.ShapeDtypeStruct((1, D_FF), F32)] * 2,
        compiler_params=_params("parallel", "arbitrary"))(da, u, u, u, u, conv_w, conv_w, conv_b, conv_b)
    dzg, dzv, dwg, dwv, dbg, dbv = outs
    return dzg, dzv, jnp.concatenate([dwg, dwv], axis=1), jnp.concatenate([dbg, dbv], axis=1)


def _conv_bwd_du(dzg, dzv, conv_w):
    S = dzg.shape[0]
    tt = min(CONV_TT, S)
    n_c, n_t = D_FF // CONV_TC, S // tt
    steps = tt // HALO

    def nxt(i):
        return jnp.minimum((i + 1) * steps, n_t * steps - 1)

    tile_g = pl.BlockSpec((tt, CONV_TC), lambda h, j, i: (i * (1 - h), j * (1 - h)))
    tile_v = pl.BlockSpec((tt, CONV_TC), lambda h, j, i: (i * h, j * h))
    halo_g = pl.BlockSpec((HALO, CONV_TC), lambda h, j, i: (nxt(i) * (1 - h), j * (1 - h)))
    halo_v = pl.BlockSpec((HALO, CONV_TC), lambda h, j, i: (nxt(i) * h, j * h))
    wsp = pl.BlockSpec((3, CONV_TC), lambda h, j, i: (0, j + h * n_c))

    def body(zg_ref, zv_ref, hg_ref, hv_ref, w_ref, du_ref):
        h = pl.program_id(0)
        last = pl.program_id(2) == n_t - 1
        z = jnp.where(h == 0, zg_ref[...], zv_ref[...])
        hal = jnp.where(h == 0, hg_ref[...], hv_ref[...])
        row = lax.broadcasted_iota(jnp.int32, z.shape, 0)
        h0 = jnp.where(last, 0.0, hal[0:1, :])
        h1 = jnp.where(last, 0.0, hal[1:2, :])
        u1 = jnp.where(row == tt - 1, h0, pltpu.roll(z, tt - 1, 0))
        u2 = jnp.where(row == tt - 1, h1, jnp.where(row == tt - 2, h0, pltpu.roll(z, tt - 2, 0)))
        du = w_ref[2:3, :] * z + w_ref[1:2, :] * u1 + w_ref[0:1, :] * u2
        du_ref[...] = du.astype(BF16)

    return pl.pallas_call(
        body, name="conv_bwd_du", grid=(2, n_c, n_t), in_specs=[tile_g, tile_v, halo_g, halo_v, wsp],
        out_specs=pl.BlockSpec((tt, CONV_TC), lambda h, j, i: (i, j + h * n_c)),
        out_shape=jax.ShapeDtypeStruct((S, 2 * D_FF), BF16),
        compiler_params=_params("parallel", "parallel", "parallel"))(dzg, dzv, dzg, dzv, conv_w)


BAND = 128


MEM_TQ = 512


def _mem_fwd(q, kv, *, name):
    S, W = q.shape
    M = kv.shape[0]
    nh = W // LANES
    tq = min(MEM_TQ, S)

    def body(q_ref, k_ref, v_ref, o_ref, l_ref):
        s = lax.dot_general(q_ref[...], k_ref[...].astype(BF16), NT, preferred_element_type=F32)
        m = jnp.max(s, axis=1, keepdims=True)
        p = jnp.exp(s - m)
        l = jnp.sum(p, axis=1, keepdims=True)
        o_ref[...] = (lax.dot_general(p.astype(BF16), v_ref[...].astype(BF16), NN, preferred_element_type=F32) / l).astype(BF16)
        l_ref[...] = jnp.broadcast_to(m + jnp.log(l), (tq, LANES))

    blk = pl.BlockSpec((tq, LANES), lambda hh, i: (i, hh))
    return pl.pallas_call(
        body, name=name, grid=(nh, S // tq),
        in_specs=[blk, pl.BlockSpec((M, LANES), lambda hh, i: (0, hh)), pl.BlockSpec((M, LANES), lambda hh, i: (0, hh + nh))],
        out_specs=[blk, blk], out_shape=[jax.ShapeDtypeStruct((S, W), BF16), jax.ShapeDtypeStruct((S, W), F32)],
        compiler_params=_params("parallel", "parallel"))(q, kv, kv)


def _mem_bwd(q, kv, do, lse, delta, *, scale, name):
    S, W = q.shape
    M = kv.shape[0]
    nh = W // LANES
    tq = min(MEM_TQ, S)

    def body(q_ref, k_ref, v_ref, do_ref, l_ref, d_ref, dq_ref, dk_ref, dv_ref):
        i = pl.program_id(1)

        @pl.when(i == 0)
        def _():
            dk_ref[...] = jnp.zeros_like(dk_ref)
            dv_ref[...] = jnp.zeros_like(dv_ref)

        qv = q_ref[...]
        kv_, vv = k_ref[...].astype(BF16), v_ref[...].astype(BF16)
        dov = do_ref[...].astype(BF16)
        s = lax.dot_general(qv, kv_, NT, preferred_element_type=F32)
        p = jnp.exp(s - l_ref[...][:, 0:1])
        dp = lax.dot_general(dov, vv, NT, preferred_element_type=F32)
        ds = (p * (dp - d_ref[...][:, 0:1])).astype(BF16)
        dq_ref[...] = lax.dot_general(ds, kv_, NN, preferred_element_type=F32) * scale
        dk_ref[...] += lax.dot_general(ds, qv, TN, preferred_element_type=F32)
        dv_ref[...] += lax.dot_general(p.astype(BF16), dov, TN, preferred_element_type=F32)

    blk = pl.BlockSpec((tq, LANES), lambda hh, i: (i, hh))
    kblk = pl.BlockSpec((M, LANES), lambda hh, i: (0, hh))
    vblk = pl.BlockSpec((M, LANES), lambda hh, i: (0, hh + nh))
    dq, dk, dv = pl.pallas_call(
        body, name=name, grid=(nh, S // tq), in_specs=[blk, kblk, vblk, blk, blk, blk], out_specs=[blk, kblk, kblk],
        out_shape=[jax.ShapeDtypeStruct((S, W), F32), jax.ShapeDtypeStruct((M, W), F32), jax.ShapeDtypeStruct((M, W), F32)],
        compiler_params=_params("parallel", "arbitrary"))(q, kv, kv, do, lse, delta)
    return dq, jnp.concatenate([dk, dv], axis=1)


CAUSAL_BLOCK = 512
STRIP = 32


def _causal_fwd(q, k, v, *, name):
    S, W = q.shape
    T = min(CAUSAL_BLOCK, S // 2)
    n_strips = T // STRIP

    def body(q_ref, k_ref, v_ref, o_ref, lse_ref, s0, s1, p0, p1, a0, a1, acc_scr):
        i = pl.program_id(1)
        s_scr, p_scr, a_scr = (s0, s1), (p0, p1), (a0, a1)

        def rows(j):
            return pl.ds(pl.multiple_of(j * T, T), T)

        def scores(j, slot):
            s_scr[slot][...] = lax.dot_general(q_ref[...], k_ref[rows(j), :], NT, preferred_element_type=F32)

        def softmax(slot, stats, diag):
            def strip(r):
                s = s_scr[slot][r * STRIP:(r + 1) * STRIP, :]
                if diag:
                    row = r * STRIP + lax.broadcasted_iota(jnp.int32, s.shape, 0)
                    s = jnp.where(row >= lax.broadcasted_iota(jnp.int32, s.shape, 1), s, NEG_INF)
                return s

            m_new = [jnp.maximum(m_old, jnp.max(strip(r), axis=1, keepdims=True)) for r, (m_old, _) in enumerate(stats)]
            new = []
            for r, (m_old, l_old) in enumerate(stats):
                rs = slice(r * STRIP, (r + 1) * STRIP)
                p = jnp.exp(strip(r) - m_new[r])
                alpha = jnp.exp(m_old - m_new[r])
                new.append((m_new[r], alpha * l_old + jnp.sum(p, axis=1, keepdims=True)))
                a_scr[slot][rs, :] = alpha
                p_scr[slot][rs, :] = p.astype(BF16)
            return tuple(new)

        def values(j, slot):
            acc_scr[...] = a_scr[slot][...] * acc_scr[...] + lax.dot_general(p_scr[slot][...], v_ref[rows(j), :], NN,
                                                                            preferred_element_type=F32)

        def trip(j, stats, mine, other):
            scores(j + 1, other)
            stats = softmax(mine, stats, False)
            values(jnp.maximum(j - 1, 0), other)
            return stats

        def pair(jj, stats):
            return trip(2 * jj + 1, trip(2 * jj, stats, 0, 1), 1, 0)

        def last(stats, mine, other):
            stats = softmax(mine, stats, True)
            values(jnp.maximum(i - 1, 0), other)
            values(i, mine)
            for r, (m, l) in enumerate(stats):
                rs = slice(r * STRIP, (r + 1) * STRIP)
                o_ref[rs, :] = (acc_scr[rs, :] / l).astype(BF16)
                lse_ref[rs, :] = jnp.broadcast_to(m + jnp.log(l), (STRIP, LANES))

        acc_scr[...] = jnp.zeros_like(acc_scr)
        p1[...] = jnp.zeros_like(p1)
        a1[...] = jnp.ones_like(a1)
        scores(0, 0)
        init = tuple((jnp.full((STRIP, 1), NEG_INF, F32), jnp.zeros((STRIP, 1), F32)) for _ in range(n_strips))
        stats = lax.fori_loop(0, i // 2, pair, init)

        @pl.when(i % 2 == 1)
        def _():
            last(trip(i - 1, stats, 0, 1), 1, 0)

        @pl.when(i % 2 == 0)
        def _():
            last(stats, 0, 1)

    blk = pl.BlockSpec((T, LANES), lambda hh, i: (i, hh))
    whole = pl.BlockSpec((S, LANES), lambda hh, i: (0, hh))
    return pl.pallas_call(
        body, name=name, grid=(W // LANES, S // T), in_specs=[blk, whole, whole], out_specs=[blk, blk],
        out_shape=[jax.ShapeDtypeStruct((S, W), BF16), jax.ShapeDtypeStruct((S, W), F32)],
        scratch_shapes=[pltpu.VMEM((T, T), F32)] * 2 + [pltpu.VMEM((T, T), BF16)] * 2 + [pltpu.VMEM((T, 1), F32)] * 2
        + [pltpu.VMEM((T, LANES), F32)],
        compiler_params=_params("parallel", "arbitrary"))(q, k, v)


def _causal_bwd(q, k, v, do, lse, delta, *, name):
    S, W = q.shape
    T = min(CAUSAL_BLOCK, S // 2)
    nq = S // T
    n_strips, n_col = T // STRIP, T // LANES

    def body(q_ref, k_ref, v_ref, do_ref, l_ref, d_ref, dq_ref, dk_ref, dv_ref, s0, s1, e0, e1, p0, p1, g0, g1):
        j = pl.program_id(1)
        s_scr, e_scr, p_scr, g_scr = (s0, s1), (e0, e1), (p0, p1), (g0, g1)

        @pl.when(j == 0)
        def _():
            dq_ref[...] = jnp.zeros_like(dq_ref)

        dk_ref[...] = jnp.zeros_like(dk_ref)
        dv_ref[...] = jnp.zeros_like(dv_ref)

        def rows(i):
            return pl.ds(pl.multiple_of(jnp.minimum(i, nq - 1) * T, T), T)

        def products(i, slot):
            r = rows(i)
            s_scr[slot][...] = lax.dot_general(q_ref[r, :], k_ref[...], NT, preferred_element_type=F32)
            e_scr[slot][...] = lax.dot_general(do_ref[r, :].astype(BF16), v_ref[...], NT, preferred_element_type=F32)

        def pointwise(i, slot, diag):
            base = pl.multiple_of(i * T, T)
            for r in range(n_strips):
                rs = slice(r * STRIP, (r + 1) * STRIP)
                lse_r = l_ref[pl.ds(base + r * STRIP, STRIP), :]
                del_r = d_ref[pl.ds(base + r * STRIP, STRIP), :]
                for c in range(n_col):
                    cs = slice(c * LANES, (c + 1) * LANES)
                    if diag and c * LANES > (r + 1) * STRIP - 1:
                        p_scr[slot][rs, cs] = jnp.zeros((STRIP, LANES), BF16)
                        g_scr[slot][rs, cs] = jnp.zeros((STRIP, LANES), BF16)
                        continue
                    sv = s_scr[slot][rs, cs]
                    if diag and (c + 1) * LANES - 1 > r * STRIP:
                        row = r * STRIP + lax.broadcasted_iota(jnp.int32, sv.shape, 0)
                        col = c * LANES + lax.broadcasted_iota(jnp.int32, sv.shape, 1)
                        sv = jnp.where(row >= col, sv, NEG_INF)
                    p = jnp.exp(sv - lse_r)
                    p_scr[slot][rs, cs] = p.astype(BF16)
                    g_scr[slot][rs, cs] = (p * (e_scr[slot][rs, cs] - del_r)).astype(BF16)

        def gradients(i, slot):
            r = rows(i)
            qi, doi = q_ref[r, :], do_ref[r, :].astype(BF16)
            g = g_scr[slot][...]
            dv_ref[...] += lax.dot_general(p_scr[slot][...], doi, TN, preferred_element_type=F32)
            dk_ref[...] += lax.dot_general(g, qi, TN, preferred_element_type=F32)
            dq_ref[r, :] += lax.dot_general(g, k_ref[...], NN, preferred_element_type=F32)

        p1[...] = jnp.zeros_like(p1)
        g1[...] = jnp.zeros_like(g1)
        products(j, 0)
        products(j + 1, 1)
        pointwise(j, 0, True)
        gradients(j, 1)

        def trip(i, mine, other):
            products(i + 1, other)
            pointwise(i, mine, False)
            gradients(i - 1, other)

        def pair(t, _):
            trip(j + 1 + 2 * t, 1, 0)
            trip(j + 2 + 2 * t, 0, 1)
            return 0

        n_rest = nq - 1 - j
        lax.fori_loop(0, n_rest // 2, pair, 0)

        @pl.when(n_rest % 2 == 1)
        def _():
            trip(nq - 1, 1, 0)
            gradients(nq - 1, 1)

        @pl.when(n_rest % 2 == 0)
        def _():
            gradients(nq - 1, 0)

    blk = pl.BlockSpec((T, LANES), lambda hh, j: (j, hh))
    whole = pl.BlockSpec((S, LANES), lambda hh, j: (0, hh))
    return pl.pallas_call(
        body, name=name, grid=(W // LANES, S // T), in_specs=[whole, blk, blk, whole, whole, whole],
        out_specs=[whole, blk, blk], out_shape=[jax.ShapeDtypeStruct((S, W), F32)] * 3,
        scratch_shapes=[pltpu.VMEM((T, T), F32)] * 4 + [pltpu.VMEM((T, T), BF16)] * 4,
        compiler_params=_params("parallel", "arbitrary"))(q, k, v, do, lse, delta)


BAND_TQ = 512


def _band_window(i, sub, nsub, L, q_ref, k_ref, v_ref, slope):
    kw = min(2 * BAND, L)
    n = i * nsub + sub
    k0 = 0 if kw == L else pl.multiple_of(jnp.maximum(n - 1, 0) * BAND, BAND)
    win = pl.ds(k0, kw)
    qs = q_ref[sub * BAND:(sub + 1) * BAND, :]
    kwv, vwv = k_ref[win, :], v_ref[win, :]
    s = lax.dot_general(qs, kwv, NT, preferred_element_type=F32)
    dist = (n * BAND + lax.broadcasted_iota(jnp.int32, s.shape, 0)) - (k0 + lax.broadcasted_iota(jnp.int32, s.shape, 1))
    s = jnp.where((dist >= 0) & (dist <= BAND), s - slope * dist.astype(F32), NEG_INF)
    return win, qs, kwv, vwv, s


def _band_fwd(q, k, v, slopes, *, n_heads, qcol, kcol, vcol, L, slope_mul, o_shape, name):
    tq = min(BAND_TQ, L)
    nsub = tq // BAND

    def body(sl_ref, q_ref, k_ref, v_ref, o_ref, l_ref):
        hh, i = pl.program_id(0), pl.program_id(1)
        slope = sl_ref[hh % DIL_HPG] * slope_mul
        for sub in range(nsub):
            _, _, _, vwv, s = _band_window(i, sub, nsub, L, q_ref, k_ref, v_ref, slope)
            m = jnp.max(s, axis=1, keepdims=True)
            p = jnp.exp(s - m)
            l = jnp.sum(p, axis=1, keepdims=True)
            rows = slice(sub * BAND, (sub + 1) * BAND)
            o_ref[rows, :] = lax.dot_general(p.astype(BF16), vwv, NN, preferred_element_type=F32) / l
            l_ref[rows, :] = jnp.broadcast_to(m + jnp.log(l), (BAND, LANES))

    whole = lambda col: pl.BlockSpec((L, LANES), lambda hh, i: (0, col(hh)))
    o_spec = pl.BlockSpec((tq, LANES), lambda hh, i: (i, hh))
    return pl.pallas_call(
        body, name=name, grid=(n_heads, L // tq),
        in_specs=[pl.BlockSpec(memory_space=pltpu.SMEM), pl.BlockSpec((tq, LANES), lambda hh, i: (i, qcol(hh))), whole(kcol), whole(vcol)],
        out_specs=[o_spec, o_spec], out_shape=[jax.ShapeDtypeStruct(o_shape, F32)] * 2,
        compiler_params=_params("parallel", "arbitrary"))(slopes, q, k, v)


def _band_bwd(q, k, v, do, lse, delta, slopes, *, n_heads, qcol, kcol, vcol, L, scale, slope_mul, d_shape, name):
    tq = min(BAND_TQ, L)
    nsub = tq // BAND
    n_steps = L // tq

    def body(sl_ref, q_ref, k_ref, v_ref, do_ref, l_ref, d_ref, dq_ref, dk_ref, dv_ref, dk_acc, dv_acc):
        hh, i = pl.program_id(0), pl.program_id(1)
        slope = sl_ref[hh % DIL_HPG] * slope_mul

        @pl.when(i == 0)
        def _():
            dk_acc[...] = jnp.zeros_like(dk_acc)
            dv_acc[...] = jnp.zeros_like(dv_acc)

        for sub in range(nsub):
            win, qs, kwv, vwv, s = _band_window(i, sub, nsub, L, q_ref, k_ref, v_ref, slope)
            rows = slice(sub * BAND, (sub + 1) * BAND)
            dos = do_ref[rows, :]
            p = jnp.exp(s - l_ref[rows, :][:, 0:1])
            dp = lax.dot_general(dos, vwv, NT, preferred_element_type=F32)
            ds = (p * (dp - d_ref[rows, :][:, 0:1])).astype(BF16)
            dq_ref[rows, :] = (lax.dot_general(ds, kwv, NN, preferred_element_type=F32) * scale).astype(BF16)
            dk_acc[win, :] += lax.dot_general(ds, qs, TN, preferred_element_type=F32)
            dv_acc[win, :] += lax.dot_general(p.astype(BF16), dos, TN, preferred_element_type=F32)

        @pl.when(i == n_steps - 1)
        def _():
            dk_ref[...] = dk_acc[...].astype(BF16)
            dv_ref[...] = dv_acc[...].astype(BF16)

    whole = lambda col: pl.BlockSpec((L, LANES), lambda hh, i: (0, col(hh)))
    blk = pl.BlockSpec((tq, LANES), lambda hh, i: (i, hh))
    ident = lambda hh: hh
    return pl.pallas_call(
        body, name=name, grid=(n_heads, n_steps),
        in_specs=[pl.BlockSpec(memory_space=pltpu.SMEM), pl.BlockSpec((tq, LANES), lambda hh, i: (i, qcol(hh))), whole(kcol), whole(vcol),
                  blk, blk, blk],
        out_specs=[blk, whole(ident), whole(ident)], out_shape=[jax.ShapeDtypeStruct(d_shape, BF16)] * 3,
        scratch_shapes=[pltpu.VMEM((L, LANES), F32)] * 2,
        compiler_params=_params("parallel", "arbitrary"))(slopes, q, k, v, do, lse, delta)


def _pad_heads(w, n_heads, width, axis):
    shp = w.shape
    new = shp[:axis] + (n_heads, width) + shp[axis + 1:]
    pad = [(0, 0)] * len(new)
    pad[axis + 1] = (0, LANES - width)
    out = jnp.pad(w.reshape(new), pad)
    return out.reshape(shp[:axis] + (n_heads * LANES,) + shp[axis + 1:])


def _unpad_heads(w, n_heads, width, axis):
    shp = w.shape
    new = shp[:axis] + (n_heads, LANES) + shp[axis + 1:]
    out = lax.slice_in_dim(w.reshape(new), 0, width, axis=axis + 1)
    return out.reshape(shp[:axis] + (n_heads * width,) + shp[axis + 1:])


def _alibi_slopes():
    s = jnp.exp2(-8.0 * jnp.arange(1, DIL_HEADS + 1, dtype=F32) / DIL_HEADS)
    return s.reshape(DIL_HPG, DIL_GROUPS).T


def _local_step(x, mem, positions, tgt, W):
    S = x.shape[0]
    pos = positions.reshape(S, 1).astype(F32)
    half = MLA_ROPE // 2
    inv_freq = ROPE_THETA ** (-jnp.arange(half, dtype=F32) / half)
    invf = jnp.zeros((1, LANES), F32).at[0, MLA_NOPE:MLA_NOPE + half].set(inv_freq).at[0, MLA_NOPE + half:MLA_QK].set(inv_freq)
    slopes = _alibi_slopes()

    w_in = W["w_in"]
    zc = lambda n: jnp.zeros((D_MODEL, n), BF16)
    w_a = jnp.concatenate([w_in[:, :OFF_Q], zc(MLA_NOPE), w_in[:, OFF_KV:OFF_KR], zc(LANES - MLA_QK), w_in[:, OFF_Q:OFF_KV]], axis=1)
    w_d, w_m, w_g = w_in[:, OFF_KR:OFF_DIL], w_in[:, OFF_DIL:OFF_MEMQ], w_in[:, OFF_MEMQ:]
    w_uq_p = _pad_heads(W["w_uq"], MLA_HEADS, MLA_QK, 1)
    ukv = W["w_ukv"].reshape(KV_RANK, MLA_HEADS, 2 * MLA_NOPE)
    w_uk_p = _pad_heads(ukv[:, :, :MLA_NOPE].reshape(KV_RANK, -1), MLA_HEADS, MLA_NOPE, 1)
    w_uv_p = _pad_heads(ukv[:, :, MLA_NOPE:].reshape(KV_RANK, -1), MLA_HEADS, MLA_NOPE, 1)
    w_br_mla_p = _pad_heads(W["w_br_mla"], MLA_HEADS, MLA_NOPE, 0)

    h = _rmsnorm(x, W["g_pre_mix"], BF16, "norm1")
    proj_a = _matmul(h, w_a, tn=768, name="proj_a")
    att_scale = LANES ** -0.5
    w_dg = [jnp.concatenate([w_d[:, (part * DIL_GROUPS + g) * DIL_W:(part * DIL_GROUPS + g + 1) * DIL_W] for part in range(3)], axis=1)
            for g in range(DIL_GROUPS)]
    proj_m = _matmul(h, w_m, tn=MEM_W, out_dtype=BF16, scale=(1, att_scale), name="proj_m")
    proj_g = _matmul(h, w_g, tn=1536, name="proj_g")

    cq_n, ckv_n, kpe = _mla_prep(proj_a, pos, invf, W["mla_q_norm"], W["mla_kv_norm"])
    q_pre = _matmul(cq_n, w_uq_p, tn=1024, name="mla_q")
    k_pre = _matmul(ckv_n, w_uk_p, tn=1024, name="mla_k")
    v_mla = _matmul(ckv_n, w_uv_p, tn=1024, out_dtype=BF16, name="mla_v")
    q_mla, k_mla = _qk_final(q_pre, k_pre, kpe, pos, invf)
    ident = lambda hh: hh
    o_mla, lse_mla = _causal_fwd(q_mla, k_mla, v_mla, name="attn_mla_fwd")

    col = lambda part: (lambda hh: (hh // DIL_HPG) * 3 * DIL_HPG + part * DIL_HPG + hh % DIL_HPG)
    h_views = [h.reshape(S // d, d * D_MODEL) for d in DILATIONS]
    proj_dg, o_dil, lse_dil = [], [], []
    for g, d in enumerate(DILATIONS):
        pv = _proj_view(h_views[g], w_dg[g], d, att_scale, f"proj_d{g}")
        o, lse = _band_fwd(pv, pv, pv, slopes[g], n_heads=d * DIL_HPG, qcol=col(0), kcol=col(1), vcol=col(2), L=S // d,
                           slope_mul=float(d), o_shape=(S // d, d * DIL_W), name=f"attn_dil{g}_fwd")
        proj_dg.append(pv)
        o_dil.append(o)
        lse_dil.append(lse)
    y_dil = _dil_mix(o_dil, lse_dil)

    mem_n = _rmsnorm(mem, W["g_mem"], BF16, "mem_norm")
    kv_mem = _matmul(mem_n, W["w_mem_kv"], name="mem_kv")
    o_mem, lse_mem = _mem_fwd(proj_m, kv_mem, name="attn_mem_fwd")

    b_mla = _matmul(o_mla, w_br_mla_p, name="br_mla")
    b_dil = _matmul(y_dil, W["w_br_dil"], name="br_dil")
    b_mem = _matmul(o_mem, W["w_br_mem"], name="br_mem")
    merged = _merge(proj_g, W["b_gate"], [b_mla, b_dil, b_mem])
    o_proj = _matmul(merged, W["w_o"], name="o_proj")
    x1, h2 = _norm2(o_proj, x, W["g_post_mix"], W["g_pre_ffn"])

    u = _matmul(h2, W["w_ffn_up"], tn=1408, name="ffn_up")
    act = _conv_fwd(u, W["conv_w"], W["conv_b"])
    f = _matmul(act, W["w_ffn_down"], name="ffn_down")
    loss8, dx2, df, dg_post_ffn = _loss_head(f, x1, tgt, W["g_post_ffn"])
    loss = loss8[0, 0]

    G = {"g_post_ffn": dg_post_ffn}
    d_act = _matmul(df, W["w_ffn_down"], mode="nt", tn=1408, name="d_act")
    G["w_ffn_down"] = _matmul(act, df, mode="tn", tm=1408, tk=1024, name="dw_ffn_down")
    dzg, dzv, G["conv_w"], G["conv_b"] = _conv_bwd_dz(d_act, u, W["conv_w"], W["conv_b"])
    du = _conv_bwd_du(dzg, dzv, W["conv_w"])
    dh2 = _matmul(du, W["w_ffn_up"], mode="nt", tk=2816, name="d_h2")
    G["w_ffn_up"] = _matmul(h2, du, mode="tn", tn=1408, tk=1024, name="dw_ffn_up")
    dx1, do_proj, G["g_pre_ffn"], G["g_post_mix"] = _norm2_bwd(dx2, dh2, x1, o_proj, W["g_pre_ffn"], W["g_post_mix"])
    dmerged = _matmul(do_proj, W["w_o"], mode="nt", name="d_merged")
    G["w_o"] = _matmul(merged, do_proj, mode="tn", tk=1024, name="dw_o")
    db_mla, db_dil, db_mem, dproj_g, G["b_gate"] = _merge_bwd(dmerged, proj_g, W["b_gate"], [b_mla, b_dil, b_mem])

    dy_mem = _matmul(db_mem, W["w_br_mem"], mode="nt", name="d_y_mem")
    G["w_br_mem"] = _matmul(o_mem, db_mem, mode="tn", tk=1024, name="dw_br_mem")
    delta_mem = _delta(dy_mem, o_mem, MEM_HEADS, "delta_mem")
    dq_mem, dkv_mem = _mem_bwd(proj_m, kv_mem, dy_mem, lse_mem, delta_mem, scale=att_scale, name="attn_mem_bwd")
    G["w_mem_kv"] = _matmul(mem_n, dkv_mem, mode="tn", name="dw_mem_kv")
    dmem_n = _matmul(dkv_mem, W["w_mem_kv"], mode="nt", name="d_mem_n")
    G["g_mem"] = _gain_grad(dmem_n, mem, "dg_mem")

    dy_dil = _matmul(db_dil, W["w_br_dil"], mode="nt", name="d_y_dil")
    G["w_br_dil"] = _matmul(y_dil, db_dil, mode="tn", tk=1024, name="dw_br_dil")
    mix = _dil_mix_bwd(dy_dil, o_dil, lse_dil)
    do_dil, dl_dil = mix[:3], mix[3:]
    dh_views, dw_parts = [], [[None] * DIL_GROUPS for _ in range(3)]
    for g, d in enumerate(DILATIONS):
        pv = proj_dg[g]
        parts = _band_bwd(pv, pv, pv, do_dil[g], lse_dil[g], dl_dil[g], slopes[g], n_heads=d * DIL_HPG, qcol=col(0),
                          kcol=col(1), vcol=col(2), L=S // d, scale=att_scale, slope_mul=float(d), d_shape=(S // d, d * DIL_W),
                          name=f"attn_dil{g}_bwd")
        dhv = None
        for part, dp in enumerate(parts):
            dhv = _dh_view(dp, w_dg[g][:, part * DIL_W:(part + 1) * DIL_W], d, dhv, f"d_h_d{g}_{part}")
            dw_parts[part][g] = _dw_view(h_views[g], dp, d, f"dw_in_d{g}_{part}")
        dh_views.append(dhv)
    dw_d = jnp.concatenate([dw for part in dw_parts for dw in part], axis=1)

    dy_mla = _matmul(db_mla, w_br_mla_p, mode="nt", name="d_y_mla")
    dw_br_mla_p = _matmul(o_mla, db_mla, mode="tn", tk=1024, name="dw_br_mla")
    G["w_br_mla"] = _unpad_heads(dw_br_mla_p, MLA_HEADS, MLA_NOPE, 0)
    delta_mla = _delta(dy_mla, o_mla, MLA_HEADS, "delta_mla")
    dq_mla, dk_mla, dv_mla = _causal_bwd(q_mla, k_mla, v_mla, dy_mla, lse_mla, delta_mla, name="attn_mla_bwd")
    dq_pre, dkpe = _mla_bwd_prep(dq_mla, dk_mla, pos, invf)
    dcq_n = _matmul(dq_pre, w_uq_p, mode="nt", tn=Q_RANK, name="d_cq")
    G["w_uq"] = _unpad_heads(_matmul(cq_n, dq_pre, mode="tn", tm=Q_RANK, tk=1024, name="dw_uq"), MLA_HEADS, MLA_QK, 1)
    dckv_a = _matmul(dk_mla, w_uk_p, mode="nt", tn=KV_RANK, name="d_ckv_k")
    dckv_b = _matmul(dv_mla, w_uv_p, mode="nt", tn=KV_RANK, name="d_ckv_v")
    dw_uk = _unpad_heads(_matmul(ckv_n, dk_mla, mode="tn", tm=KV_RANK, tk=1024, name="dw_uk"), MLA_HEADS, MLA_NOPE, 1)
    dw_uv = _unpad_heads(_matmul(ckv_n, dv_mla, mode="tn", tm=KV_RANK, tk=1024, name="dw_uv"), MLA_HEADS, MLA_NOPE, 1)
    G["w_ukv"] = jnp.concatenate([dw_uk.reshape(KV_RANK, MLA_HEADS, MLA_NOPE), dw_uv.reshape(KV_RANK, MLA_HEADS, MLA_NOPE)],
                                 axis=2).reshape(KV_RANK, -1)
    dproj_a, G["mla_q_norm"], G["mla_kv_norm"] = _mla_norm_bwd(dcq_n, dckv_a, dckv_b, dkpe, proj_a, W["mla_q_norm"],
                                                              W["mla_kv_norm"])

    dh = _matmul(dproj_a, w_a, mode="nt", name="d_h_a")
    dh = _matmul(dq_mem, w_m, mode="nt", add=dh, name="d_h_m")
    dh = _matmul(dproj_g, w_g, mode="nt", add=dh, name="d_h_g")
    dw_a = _matmul(h, dproj_a, mode="tn", tn=768, tk=1024, name="dw_in_a")
    dw_m = _matmul(h, dq_mem, mode="tn", tn=512, tk=1024, name="dw_in_m")
    dw_g = _matmul(h, dproj_g, mode="tn", tn=1536, tk=1024, name="dw_in_g")
    kr0 = Q_RANK + MLA_NOPE
    G["w_in"] = jnp.concatenate([dw_a[:, :Q_RANK], dw_a[:, Q_RANK + LANES:], dw_a[:, kr0:kr0 + MLA_ROPE], dw_d, dw_m, dw_g], axis=1)
    grad_x, G["g_pre_mix"] = _norm1_bwd(dx1, [dh] + dh_views, x, W["g_pre_mix"])
    return loss, grad_x, G


WEIGHTS = ["g_pre_mix", "w_in", "b_gate", "mla_q_norm", "w_uq", "mla_kv_norm", "w_ukv", "g_mem", "w_mem_kv", "w_br_mla",
           "w_br_dil", "w_br_mem", "w_o", "g_post_mix", "g_pre_ffn", "w_ffn_up", "conv_w", "conv_b", "w_ffn_down", "g_post_ffn"]
GROUPS = [
    [("w_in", (D_MODEL, D_IN), 1)],
    [("w_uq", (Q_RANK, MLA_HEADS * MLA_QK), 1)],
    [("w_ukv", (KV_RANK, MLA_HEADS * 2 * MLA_NOPE), 1), ("w_br_mla", (MLA_HEADS * MLA_NOPE, D_MODEL), 1),
     ("w_br_dil", (DIL_W, D_MODEL), 1), ("w_br_mem", (MEM_W, D_MODEL), 1)],
    [("w_mem_kv", (D_MODEL, 2 * MEM_W), 0), ("w_o", (D_MODEL, D_MODEL), 0), ("w_ffn_down", (D_FF, D_MODEL), 0)],
    [("w_ffn_up", (D_MODEL, 2 * D_FF), 1)],
]
CONV_W = ("conv_w", (3, 2 * D_FF), 1)
REPLICATED = [("g_pre_mix", D_MODEL), ("b_gate", 3 * D_MODEL), ("mla_q_norm", Q_RANK), ("mla_kv_norm", KV_RANK), ("g_mem", D_MODEL),
              ("g_post_mix", D_MODEL), ("g_pre_ffn", D_MODEL), ("conv_b", 2 * D_FF), ("g_post_ffn", D_MODEL)]
SMALL_ROWS = 256
CONV_AT = sum(n for _, n in REPLICATED)
LOSS_AT = CONV_AT + 3 * 2 * D_FF


def _shard_shape(shape, axis):
    return tuple(d // N_CHIPS if a == axis else d for a, d in enumerate(shape))


def _group_shape(grp):
    shapes = [_shard_shape(shape, axis) for _, shape, axis in grp]
    assert len({s[1] for s in shapes}) == 1
    return sum(s[0] for s in shapes), shapes[0][1]


def _member_shards(a, axis):
    r, c = a.shape
    if axis == 0:
        return a.reshape(N_CHIPS, r // N_CHIPS, c)
    return a.reshape(r, N_CHIPS, c // N_CHIPS).transpose(1, 0, 2)


def _member_full(s, axis):
    n, r, c = s.shape
    if axis == 0:
        return s.reshape(n * r, c)
    return s.transpose(1, 0, 2).reshape(r, n * c)


def _my_weight_groups(w):
    out = []
    for grp in GROUPS:
        rows, width = _group_shape(grp)
        out.append(jnp.concatenate([w[name].astype(BF16) for name, _, _ in grp], axis=0).reshape(2, rows // 2, width))
    return out


def _full_weights(gathered, conv_all):
    out = {}
    for grp, ga in zip(GROUPS, gathered):
        ga = ga.reshape(N_CHIPS, -1, ga.shape[-1])
        off = 0
        for name, shape, axis in grp:
            rows = _shard_shape(shape, axis)[0]
            out[name] = _member_full(ga[:, off:off + rows], axis)
            off += rows
    out[CONV_W[0]] = _member_full(conv_all, CONV_W[2])
    return out


def _grad_groups(G):
    out = []
    for grp in GROUPS:
        rows, width = _group_shape(grp)
        a = jnp.concatenate([_member_shards(G[name], axis) for name, _, axis in grp], axis=1)
        out.append(a.reshape(N_CHIPS, 2, rows // 2, width).transpose(1, 0, 2, 3))
    return out


def _pack_small(vals, conv_g=None, loss=None):
    parts = [vals[name].reshape(-1) for name, _ in REPLICATED]
    if conv_g is not None:
        parts += [conv_g.reshape(-1), loss.reshape(1)]
    flat = jnp.concatenate(parts)
    return jnp.pad(flat, (0, SMALL_ROWS * LANES - flat.shape[0])).reshape(SMALL_ROWS, LANES)


def _unpack_small(packed):
    flat = packed.reshape(-1)
    out, off = {}, 0
    for name, n in REPLICATED:
        out[name] = flat[off:off + n].reshape(1, n)
        off += n
    return out


MESH = pl.DeviceIdType.MESH
HBM_SPEC = pl.BlockSpec(memory_space=pltpu.HBM)


def _place():
    x, y, c = lax.axis_index("x"), lax.axis_index("y"), lax.axis_index("c")
    chips = [(1 - x, y), (x, 1 - y), (1 - x, 1 - y)]
    return x, y, c, chips


def _remote(src, dst, send_sems, recv_sems, k, to):
    return pltpu.make_async_remote_copy(src_ref=src, dst_ref=dst, send_sem=send_sems.at[k], recv_sem=recv_sems.at[k],
                                        device_id=to, device_id_type=MESH)


def _gather_weights(groups, wholes, name, collective_id):
    n, m = len(groups), len(wholes)
    arrays = list(groups) + list(wholes)
    hbm = pltpu.MemorySpace.HBM
    srcs = [jax.new_ref(a, memory_space=hbm) for a in arrays]
    outs = [jax.empty_ref(jax.ShapeDtypeStruct((N_CHIPS,) + a.shape, a.dtype), memory_space=hbm) for a in arrays]
    n_sem = 6 * n + 3 * m

    def launch(send_sems, recv_sems):
        x, y, c, chips = _place()
        me = 2 * x + y
        sibling = (x, y, 1 - c)
        barrier = pltpu.get_barrier_semaphore()
        peers = [(px, py, c) for px, py in chips] + [sibling]
        for peer in peers:
            pl.semaphore_signal(barrier, inc=1, device_id=peer, device_id_type=MESH)
        pl.semaphore_wait(barrier, len(peers))
        first = [_remote(srcs[g].at[c], outs[g].at[me, c], send_sems, recv_sems, g * 3 + k, (px, py, c))
                 for k, (px, py) in enumerate(chips) for g in range(n)]
        first += [_remote(srcs[n + w], outs[n + w].at[me], send_sems, recv_sems, 6 * n + w * 3 + k, (px, py, c))
                  for k, (px, py) in enumerate(chips) for w in range(m)]
        for cp in first:
            cp.start()
        passed = []
        for k, (px, py) in enumerate(chips):
            for g in range(n):
                slot = outs[g].at[2 * px + py, c]
                _remote(slot, slot, send_sems, recv_sems, g * 3 + k, (px, py, c)).wait_recv()
                cp = _remote(slot, slot, send_sems, recv_sems, 3 * n + g * 3 + k, sibling)
                cp.start()
                passed.append(cp)
        for k, (px, py) in enumerate(chips):
            for w in range(m):
                slot = outs[n + w].at[2 * px + py]
                _remote(slot, slot, send_sems, recv_sems, 6 * n + w * 3 + k, (px, py, c)).wait_recv()
            for g in range(n):
                slot = outs[g].at[2 * px + py, 1 - c]
                _remote(slot, slot, send_sems, recv_sems, 3 * n + g * 3 + k, sibling).wait_recv()
        for cp in first + passed:
            cp.wait_send()

    pl.kernel(launch, mesh=plsc.ScalarSubcoreMesh(axis_name="sequencer", num_cores=1), name=name,
              scratch_types=(pltpu.SemaphoreType.DMA((n_sem,)), pltpu.SemaphoreType.DMA((n_sem,))),
              compiler_params=pltpu.CompilerParams(collective_id=collective_id))()
    me = 2 * lax.axis_index("x") + lax.axis_index("y")
    res = [lax.dynamic_update_slice_in_dim(o[...], a[None], me, axis=0) for o, a in zip(outs, arrays)]
    return res[:n], res[n:]


def _swap_halves(groups):
    n = len(groups)

    def body(*refs):
        srcs, outs, send_sems, recv_sems = refs[:n], refs[n:2 * n], refs[2 * n], refs[2 * n + 1]
        x, y, c, _ = _place()
        cps = [_remote(srcs[g].at[1 - c], outs[g], send_sems, recv_sems, g, (x, y, 1 - c)) for g in range(n)]
        for cp in cps:
            cp.start()
        for cp in cps:
            cp.wait()

    return pl.pallas_call(
        body, name="comm_swap_halves", in_specs=[HBM_SPEC] * n, out_specs=[HBM_SPEC] * n,
        out_shape=[jax.ShapeDtypeStruct(g.shape[1:], g.dtype) for g in groups],
        scratch_shapes=[pltpu.SemaphoreType.DMA((n,)), pltpu.SemaphoreType.DMA((n,))],
    )(*groups)


def _scatter_partials(groups):
    n = len(groups)

    def body(*refs):
        srcs, outs = refs[:n], refs[n:2 * n]
        send_sems, recv_sems = refs[2 * n:]
        x, y, c, chips = _place()
        sends = [_remote(srcs[g].at[2 * px + py], outs[g].at[k], send_sems, recv_sems, g * 3 + k, (px, py, c))
                 for k, (px, py) in enumerate(chips) for g in range(n)]
        for cp in sends:
            cp.start()
        for k, (px, py) in enumerate(chips):
            for g in range(n):
                slot = outs[g].at[k]
                _remote(slot, slot, send_sems, recv_sems, g * 3 + k, (px, py, c)).wait_recv()
        for cp in sends:
            cp.wait_send()

    return pl.pallas_call(
        body, name="comm_scatter_partials", in_specs=[HBM_SPEC] * n, out_specs=[HBM_SPEC] * n,
        out_shape=[jax.ShapeDtypeStruct((3,) + g.shape[1:], g.dtype) for g in groups],
        scratch_shapes=[pltpu.SemaphoreType.DMA((3 * n,)), pltpu.SemaphoreType.DMA((3 * n,))],
    )(*groups)


def _share_reduced(groups):
    n = len(groups)

    def body(*refs):
        srcs, outs = refs[:n], refs[n:2 * n]
        send_sems, recv_sems = refs[2 * n:]
        x, y, c, _ = _place()
        cps = [_remote(srcs[g], outs[g], send_sems, recv_sems, g, (x, y, 1 - c)) for g in range(n)]
        for cp in cps:
            cp.start()
        for cp in cps:
            cp.wait()

    return pl.pallas_call(
        body, name="comm_share_reduced", in_specs=[HBM_SPEC] * n, out_specs=[HBM_SPEC] * n,
        out_shape=[jax.ShapeDtypeStruct(g.shape, g.dtype) for g in groups],
        scratch_shapes=[pltpu.SemaphoreType.DMA((n,)), pltpu.SemaphoreType.DMA((n,))],
    )(*groups)


def _allreduce_small(v):
    n_dev = 2 * N_CHIPS

    def body(src, out, buf, send_sems, recv_sems):
        x, y, c, _ = _place()
        me = 4 * x + 2 * y + c
        buf[me] = src[...]
        flips = [(k >> 2 & 1, k >> 1 & 1, k & 1) for k in range(1, n_dev)]
        sends = []
        for k, (fx, fy, fc) in enumerate(flips):
            to = ((1 - x) if fx else x, (1 - y) if fy else y, (1 - c) if fc else c)
            cp = _remote(src, buf.at[me], send_sems, recv_sems, k, to)
            cp.start()
            sends.append((cp, to))
        for k, (cp, to) in enumerate(sends):
            slot = buf.at[4 * to[0] + 2 * to[1] + to[2]]
            _remote(slot, slot, send_sems, recv_sems, k, to).wait_recv()
        for cp, _ in sends:
            cp.wait_send()
        acc = buf[0]
        for d in range(1, n_dev):
            acc = acc + buf[d]
        out[...] = acc

    vm = pl.BlockSpec(memory_space=pltpu.VMEM)
    return pl.pallas_call(
        body, name="comm_allreduce_small", in_specs=[vm], out_specs=vm, out_shape=jax.ShapeDtypeStruct(v.shape, F32),
        scratch_shapes=[pltpu.VMEM((n_dev,) + v.shape, F32), pltpu.SemaphoreType.DMA((n_dev - 1,)),
                        pltpu.SemaphoreType.DMA((n_dev - 1,))],
    )(v)


def _row_tile(rows, cap=320):
    return max(t for t in range(16, cap + 1, 16) if rows % t == 0)


def _add_sibling(mine, theirs, core, name):
    _, n, R, C = mine.shape
    t = _row_tile(R)

    def body(core_ref, a_ref, b_ref, o_ref, ob_ref):
        tot = a_ref[...] + b_ref[...]
        o_ref[...] = tot
        ob_ref[...] = tot.astype(BF16)

    sp = pl.BlockSpec((None, t, C), lambda k, i, core_ref: (k, i, 0))
    grid_spec = pltpu.PrefetchScalarGridSpec(
        num_scalar_prefetch=1, grid=(n, R // t),
        in_specs=[pl.BlockSpec((None, None, t, C), lambda k, i, core_ref: (core_ref[0], k, i, 0)), sp], out_specs=[sp, sp])
    return pl.pallas_call(body, name=name, grid_spec=grid_spec,
                          out_shape=[jax.ShapeDtypeStruct((n, R, C), F32), jax.ShapeDtypeStruct((n, R, C), BF16)],
                          compiler_params=_params("parallel", "parallel"))(core, mine, theirs)


def _add_chips(received, own, chip, name):
    n, R, C = received.shape
    t = _row_tile(R)

    def body(chip_ref, r_ref, o_ref, out_ref):
        acc = o_ref[...]
        for k in range(n):
            acc = acc + r_ref[k].astype(F32)
        out_ref[...] = acc

    grid_spec = pltpu.PrefetchScalarGridSpec(
        num_scalar_prefetch=1, grid=(R // t,),
        in_specs=[pl.BlockSpec((n, t, C), lambda i, chip_ref: (0, i, 0)), pl.BlockSpec((None, t, C), lambda i, chip_ref: (chip_ref[0], i, 0))],
        out_specs=pl.BlockSpec((t, C), lambda i, chip_ref: (i, 0)))
    return pl.pallas_call(body, name=name, grid_spec=grid_spec, out_shape=jax.ShapeDtypeStruct((R, C), F32),
                          compiler_params=_params("parallel"))(chip, received, own)


def _adamw(w, g, m, v, name, g_row0=0):
    R, C = w.shape
    t = math.gcd(math.gcd(R, g_row0), 128) if R % 8 == 0 else R
    assert g_row0 % t == 0
    c1 = 1.0 - ADAM_B1 ** ADAM_STEP
    c2 = 1.0 - ADAM_B2 ** ADAM_STEP

    def body(w_ref, g_ref, m_ref, v_ref, go_ref, d_ref, nm_ref, nv_ref):
        gv = g_ref[...]
        nm = ADAM_B1 * m_ref[...] + (1.0 - ADAM_B1) * gv
        nv = ADAM_B2 * v_ref[...] + (1.0 - ADAM_B2) * (gv * gv)
        go_ref[...] = gv
        d_ref[...] = -ADAM_LR * ((nm / c1) / (jnp.sqrt(nv / c2) + ADAM_EPS) + ADAM_WD * w_ref[...])
        nm_ref[...] = nm
        nv_ref[...] = nv

    sp = pl.BlockSpec((t, C), lambda i: (i, 0))
    g_sp = pl.BlockSpec((t, C), lambda i: (i + g_row0 // t, 0))
    return pl.pallas_call(body, name=name, grid=(R // t,), in_specs=[sp, g_sp, sp, sp], out_specs=[sp] * 4,
                          out_shape=[jax.ShapeDtypeStruct((R, C), F32)] * 4, compiler_params=_params("parallel"))(w, g, m, v)


def kernel(x, mem, positions, g_pre_mix, w_in, b_gate, mla_q_norm, w_uq, mla_kv_norm, w_ukv, g_mem, w_mem_kv, w_br_mla, w_br_dil, w_br_mem, w_o, g_post_mix, g_pre_ffn, w_ffn_up, conv_w, conv_b, w_ffn_down, g_post_ffn, loss_target, m_g_pre_mix, m_w_in, m_b_gate, m_mla_q_norm, m_w_uq, m_mla_kv_norm, m_w_ukv, m_g_mem, m_w_mem_kv, m_w_br_mla, m_w_br_dil, m_w_br_mem, m_w_o, m_g_post_mix, m_g_pre_ffn, m_w_ffn_up, m_conv_w, m_conv_b, m_w_ffn_down, m_g_post_ffn, v_g_pre_mix, v_w_in, v_b_gate, v_mla_q_norm, v_w_uq, v_mla_kv_norm, v_w_ukv, v_g_mem, v_w_mem_kv, v_w_br_mla, v_w_br_dil, v_w_br_mem, v_w_o, v_g_post_mix, v_g_pre_ffn, v_w_ffn_up, v_conv_w, v_conv_b, v_w_ffn_down, v_g_post_ffn):
    w_args = (g_pre_mix, w_in, b_gate, mla_q_norm, w_uq, mla_kv_norm, w_ukv, g_mem, w_mem_kv, w_br_mla, w_br_dil, w_br_mem, w_o,
              g_post_mix, g_pre_ffn, w_ffn_up, conv_w, conv_b, w_ffn_down, g_post_ffn)
    m_args = (m_g_pre_mix, m_w_in, m_b_gate, m_mla_q_norm, m_w_uq, m_mla_kv_norm, m_w_ukv, m_g_mem, m_w_mem_kv, m_w_br_mla,
              m_w_br_dil, m_w_br_mem, m_w_o, m_g_post_mix, m_g_pre_ffn, m_w_ffn_up, m_conv_w, m_conv_b, m_w_ffn_down, m_g_post_ffn)
    v_args = (v_g_pre_mix, v_w_in, v_b_gate, v_mla_q_norm, v_w_uq, v_mla_kv_norm, v_w_ukv, v_g_mem, v_w_mem_kv, v_w_br_mla,
              v_w_br_dil, v_w_br_mem, v_w_o, v_g_post_mix, v_g_pre_ffn, v_w_ffn_up, v_conv_w, v_conv_b, v_w_ffn_down, v_g_post_ffn)
    sharded = {name for grp in GROUPS for name, _, _ in grp} | {CONV_W[0]}

    def local(a, name):
        return a[0] if name in sharded else a

    w = {n: local(a, n) for n, a in zip(WEIGHTS, w_args)}
    m = {n: local(a, n) for n, a in zip(WEIGHTS, m_args)}
    v = {n: local(a, n) for n, a in zip(WEIGHTS, v_args)}

    mine = _my_weight_groups(w)
    first, _ = _gather_weights(mine[:1], [], "comm_gather_w_in", 0)
    rest, (conv_all,) = _gather_weights(mine[1:], [w[CONV_W[0]]], "comm_gather_rest", 1)
    full = _full_weights(first + rest, conv_all)
    full.update({name: w[name] for name, _ in REPLICATED})

    loss_local, grad_x, G = _local_step(x[0], mem[0], positions, loss_target[0], full)

    core = lax.axis_index("c").astype(jnp.int32).reshape(1)
    mine = _grad_groups(G)
    theirs = _swap_halves(mine)
    chip = (2 * lax.axis_index("x") + lax.axis_index("y")).astype(jnp.int32).reshape(1)
    partial = [_add_sibling(a, b, core, f"add_sibling_{i}") for i, (a, b) in enumerate(zip(mine, theirs))]
    received = _scatter_partials([p16 for _, p16 in partial])
    reduced = [_add_chips(r, p32, chip, f"add_chips_{i}") for i, (r, (p32, _)) in enumerate(zip(received, partial))]
    first = lax.axis_index("c") == 0
    shard_groups = [jnp.concatenate([jnp.where(first, a, b), jnp.where(first, b, a)], axis=0)
                    for a, b in zip(reduced, _share_reduced(reduced))]
    small = _allreduce_small(_pack_small(G, G[CONV_W[0]], loss_local))
    flat = small.reshape(-1)
    loss = flat[LOSS_AT]
    conv_g = flat[CONV_AT:LOSS_AT].reshape(CONV_W[1])
    conv_cols = CONV_W[1][1] // N_CHIPS
    conv_g = lax.dynamic_slice_in_dim(conv_g, (2 * lax.axis_index("x") + lax.axis_index("y")) * conv_cols, conv_cols, axis=1)

    grads, deltas, new_m, new_v = {}, {}, {}, {}
    for grp, g_all in zip(GROUPS, shard_groups):
        off = 0
        for name, _, _ in grp:
            grads[name], deltas[name], new_m[name], new_v[name] = _adamw(w[name], g_all, m[name], v[name], "adamw_" + name, off)
            off += w[name].shape[0]
    name = CONV_W[0]
    grads[name], deltas[name], new_m[name], new_v[name] = _adamw(w[name], conv_g, m[name], v[name], "adamw_" + name)
    packed = _adamw(_pack_small(w), small, _pack_small(m), _pack_small(v), "adamw_small")
    for dst, packed_small in zip((grads, deltas, new_m, new_v), packed):
        dst.update(_unpack_small(packed_small))

    def out(d, name):
        return d[name][None] if name in sharded else d[name]

    return (loss, grad_x[None], *[out(grads, n) for n in WEIGHTS], *[out(deltas, n) for n in WEIGHTS],
            *[out(new_m, n) for n in WEIGHTS], *[out(new_v, n) for n in WEIGHTS])
```

```python
import functools
import math

import jax
import jax.numpy as jnp
from jax import lax
from jax.experimental import pallas as pl
from jax.experimental.pallas import tpu as pltpu
from jax.experimental.pallas import tpu_sc as plsc

F32 = jnp.float32
BF16 = jnp.bfloat16

D_MODEL = 1024
N_MEM = 256
RMS_EPS = 1e-6
NEG_INF = -1e30
MLA_HEADS = 8
MLA_NOPE = 64
MLA_ROPE = 32
MLA_QK = 96
Q_RANK = 384
KV_RANK = 256
ROPE_THETA = 10000.0
DIL_PAIRS = ((128, 1), (512, 4), (2048, 16))
DIL_GROUPS = 3
DIL_HPG = 4
DIL_HEADS = 12
DIL_W = 512
MEM_HEADS = 4
MEM_W = 512
D_FF = 2816
OFF_Q = 384
OFF_KV = 640
OFF_KR = 672
OFF_DIL = 5280
OFF_MEMQ = 5792
D_IN = 8864
ADAM_LR = 0.001
ADAM_B1 = 0.9
ADAM_B2 = 0.999
ADAM_EPS = 1e-08
ADAM_WD = 0.01
ADAM_STEP = 10

LANES = 128
VMEM_LIMIT = 56 * 1024 * 1024

N_CHIPS = 4
ROW_TILE = 256

NN = (((1,), (0,)), ((), ()))
NT = (((1,), (1,)), ((), ()))
TN = (((0,), (0,)), ((), ()))


def _params(*sem):
    return pltpu.CompilerParams(dimension_semantics=sem, vmem_limit_bytes=VMEM_LIMIT)


def _full(shape):
    return pl.BlockSpec(shape, lambda *_: (0,) * len(shape))


def _matmul(a, b, *, mode="nn", out_dtype=F32, tm=1024, tn=1024, tk=None, add=None, scale=None, name):
    if mode == "nn":
        (M, K), N = a.shape, b.shape[1]
    elif mode == "nt":
        (M, K), N = a.shape, b.shape[0]
    else:
        (K, M), N = a.shape, b.shape[1]
    tm, tn = min(tm, M), min(tn, N)
    tk = K if tk is None else min(tk, K)
    assert M % tm == 0 and N % tn == 0 and K % tk == 0, (name, M, N, K, tm, tn, tk)
    a_spec = pl.BlockSpec((tk, tm), lambda i, j, k: (k, i)) if mode == "tn" else pl.BlockSpec((tm, tk), lambda i, j, k: (i, k))
    b_spec = pl.BlockSpec((tn, tk), lambda i, j, k: (j, k)) if mode == "nt" else pl.BlockSpec((tk, tn), lambda i, j, k: (k, j))
    o_spec = pl.BlockSpec((tm, tn), lambda i, j, k: (i, j))
    return _matmul_blocks(a, b, mode=mode, grid=(M // tm, N // tn, K // tk), a_spec=a_spec, b_spec=b_spec, o_spec=o_spec,
                          out_shape=(M, N), out_dtype=out_dtype, add=add,
                          scale=None if scale is None else (lambda j: j < scale[0], scale[1]), name=name)


def _matmul_blocks(a, b, *, mode, grid, a_spec, b_spec, o_spec, out_shape, out_dtype=F32, add=None, scale=None, name):
    nk = grid[2]
    dims = {"nn": NN, "nt": NT, "tn": TN}[mode]
    tm, tn = o_spec.block_shape
    has_add = add is not None

    def body(*refs):
        a_ref, b_ref = refs[0], refs[1]
        c_ref = refs[2] if has_add else None
        o_ref = refs[3] if has_add else refs[2]
        part = lax.dot_general(a_ref[...].astype(BF16), b_ref[...].astype(BF16), dims, preferred_element_type=F32)
        if scale is not None:
            assert nk == 1 and not has_add
            part = part * jnp.where(scale[0](pl.program_id(1)), scale[1], 1.0)
        if nk == 1:
            if has_add:
                part = part + c_ref[...]
            o_ref[...] = part.astype(out_dtype)
        else:
            acc = refs[-1]
            k = pl.program_id(2)

            @pl.when(k == 0)
            def _():
                acc[...] = part

            @pl.when(k > 0)
            def _():
                acc[...] += part

            @pl.when(k == nk - 1)
            def _():
                r = acc[...]
                if has_add:
                    r = r + c_ref[...]
                o_ref[...] = r.astype(out_dtype)

    in_specs = [a_spec, b_spec] + ([o_spec] if has_add else [])
    args = (a, b) + ((add,) if has_add else ())
    return pl.pallas_call(
        body, name=name, grid=grid, in_specs=in_specs, out_specs=o_spec,
        out_shape=jax.ShapeDtypeStruct(out_shape, out_dtype),
        scratch_shapes=[pltpu.VMEM((tm, tn), F32)] if nk > 1 else [],
        compiler_params=_params("parallel", "parallel", "arbitrary"),
    )(*args)


def _proj_view(h_view, w, d, q_scale, name):
    L, K = h_view.shape[0], h_view.shape[1] // d
    N = w.shape[1]
    tn = N // 3
    tm = min(1024, L)
    return _matmul_blocks(
        h_view, w, mode="nn", grid=(L // tm, 3 * d, 1), a_spec=pl.BlockSpec((tm, K), lambda i, j, k: (i, j // 3)),
        b_spec=pl.BlockSpec((K, tn), lambda i, j, k: (0, j % 3)), o_spec=pl.BlockSpec((tm, tn), lambda i, j, k: (i, j)),
        out_shape=(L, d * N), out_dtype=BF16, scale=(lambda j: j % 3 == 0, q_scale), name=name)


def _dh_view(dp_view, w, d, add, name):
    L, N = dp_view.shape[0], dp_view.shape[1] // d
    K = w.shape[0]
    tm = min(1024, L)
    return _matmul_blocks(
        dp_view, w, mode="nt", grid=(L // tm, d, 1), a_spec=pl.BlockSpec((tm, N), lambda i, j, k: (i, j)),
        b_spec=pl.BlockSpec((K, N), lambda i, j, k: (0, 0)), o_spec=pl.BlockSpec((tm, K), lambda i, j, k: (i, j)),
        out_shape=(L, d * K), add=add, name=name)


def _dw_view(h_view, dp_view, d, name):
    L, K = h_view.shape[0], h_view.shape[1] // d
    N = dp_view.shape[1] // d
    tk = min(1024, L)
    nl = L // tk
    return _matmul_blocks(
        h_view, dp_view, mode="tn", grid=(1, 1, d * nl), a_spec=pl.BlockSpec((tk, K), lambda i, j, k: (k % nl, k // nl)),
        b_spec=pl.BlockSpec((tk, N), lambda i, j, k: (k % nl, k // nl)), o_spec=pl.BlockSpec((K, N), lambda i, j, k: (0, 0)),
        out_shape=(K, N), name=name)


def _rms_fwd_val(x, g):
    r = lax.rsqrt(jnp.mean(x * x, axis=-1, keepdims=True) + RMS_EPS)
    return (x * r) * g


def _rms_bwd_val(dy, x, g):
    r = lax.rsqrt(jnp.mean(x * x, axis=-1, keepdims=True) + RMS_EPS)
    xn = x * r
    gdy = g * dy
    dx = r * (gdy - xn * jnp.mean(gdy * xn, axis=-1, keepdims=True))
    return dx, dy * xn


def _rope_tables(pos, invf, inverse):
    ang = pos * invf
    cos, sin = jnp.cos(ang), jnp.sin(ang)
    lane = lax.broadcasted_iota(jnp.int32, ang.shape, 1)
    first = (lane >= MLA_NOPE) & (lane < MLA_NOPE + MLA_ROPE // 2)
    second = (lane >= MLA_NOPE + MLA_ROPE // 2) & (lane < MLA_QK)
    sgn = -1.0 if inverse else 1.0
    sa = jnp.where(first, -sgn * sin, 0.0)
    sb = jnp.where(second, sgn * sin, 0.0)
    return cos, sa, sb


def _rope_val(x, cos, sa, sb):
    half = MLA_ROPE // 2
    return x * cos + pltpu.roll(x, LANES - half, 1) * sa + pltpu.roll(x, half, 1) * sb


def _head_sum_bcast(v, n_heads):
    parts = []
    for h in range(n_heads):
        s = jnp.sum(v[:, h * LANES:(h + 1) * LANES], axis=1, keepdims=True)
        parts.append(jnp.broadcast_to(s, (v.shape[0], LANES)))
    return parts


def _row_spec(t, w):
    return pl.BlockSpec((t, w), lambda i: (i, 0))


def _acc_spec(w, rows=1):
    return pl.BlockSpec((rows, w), lambda i: (0, 0))


def _rmsnorm(x, g, out_dtype, name):
    S, W = x.shape
    t = min(ROW_TILE, S)

    def body(x_ref, g_ref, o_ref):
        o_ref[...] = _rms_fwd_val(x_ref[...], g_ref[...]).astype(out_dtype)

    return pl.pallas_call(body, name=name, grid=(S // t,), in_specs=[_row_spec(t, W), _acc_spec(W)],
                          out_specs=_row_spec(t, W), out_shape=jax.ShapeDtypeStruct((S, W), out_dtype),
                          compiler_params=_params("parallel"))(x, g)


def _mla_prep(proj_a, pos, invf, q_norm, kv_norm):
    S = proj_a.shape[0]
    t = ROW_TILE

    def body(a_ref, pos_ref, invf_ref, qn_ref, kvn_ref, cq_ref, ckv_ref, kpe_ref):
        a = a_ref[...]
        cq_ref[...] = _rms_fwd_val(a[:, 0:Q_RANK], qn_ref[...]).astype(BF16)
        ckv_ref[...] = _rms_fwd_val(a[:, Q_RANK + LANES:], kvn_ref[...]).astype(BF16)
        cos, sa, sb = _rope_tables(pos_ref[...], invf_ref[...], False)
        kpe_ref[...] = _rope_val(a[:, Q_RANK:Q_RANK + LANES], cos, sa, sb)

    return pl.pallas_call(
        body, name="mla_prep", grid=(S // t,),
        in_specs=[_row_spec(t, 768), _row_spec(t, 1), _acc_spec(LANES), _acc_spec(Q_RANK), _acc_spec(KV_RANK)],
        out_specs=[_row_spec(t, Q_RANK), _row_spec(t, KV_RANK), _row_spec(t, LANES)],
        out_shape=[jax.ShapeDtypeStruct((S, Q_RANK), BF16), jax.ShapeDtypeStruct((S, KV_RANK), BF16),
                   jax.ShapeDtypeStruct((S, LANES), F32)],
        compiler_params=_params("parallel"))(proj_a, pos, invf, q_norm, kv_norm)


def _qk_final(q_pre, k_pre, kpe, pos, invf):
    S, W = q_pre.shape
    t = ROW_TILE
    scale = MLA_QK ** -0.5

    def body(q_ref, k_ref, kpe_ref, pos_ref, invf_ref, qo_ref, ko_ref):
        cos, sa, sb = _rope_tables(pos_ref[...], invf_ref[...], False)
        kpe_v = kpe_ref[...]
        for h in range(MLA_HEADS):
            sl = slice(h * LANES, (h + 1) * LANES)
            qo_ref[:, sl] = (_rope_val(q_ref[:, sl], cos, sa, sb) * scale).astype(BF16)
            ko_ref[:, sl] = (k_ref[:, sl] + kpe_v).astype(BF16)

    return pl.pallas_call(
        body, name="qk_final", grid=(S // t,),
        in_specs=[_row_spec(t, W), _row_spec(t, W), _row_spec(t, LANES), _row_spec(t, 1), _acc_spec(LANES)],
        out_specs=[_row_spec(t, W), _row_spec(t, W)],
        out_shape=[jax.ShapeDtypeStruct((S, W), BF16)] * 2,
        compiler_params=_params("parallel"))(q_pre, k_pre, kpe, pos, invf)


def _mla_bwd_prep(dq, dk, pos, invf):
    S, W = dq.shape
    t = ROW_TILE
    scale = MLA_QK ** -0.5

    def body(dq_ref, dk_ref, pos_ref, invf_ref, dqp_ref, dkpe_ref):
        cos, sa, sb = _rope_tables(pos_ref[...], invf_ref[...], True)
        tot = jnp.zeros((t, LANES), F32)
        for h in range(MLA_HEADS):
            sl = slice(h * LANES, (h + 1) * LANES)
            dqp_ref[:, sl] = _rope_val(dq_ref[:, sl] * scale, cos, sa, sb).astype(BF16)
            tot = tot + dk_ref[:, sl]
        lane = lax.broadcasted_iota(jnp.int32, tot.shape, 1)
        tot = jnp.where((lane >= MLA_NOPE) & (lane < MLA_QK), tot, 0.0)
        dkpe_ref[...] = _rope_val(tot, cos, sa, sb)

    return pl.pallas_call(
        body, name="mla_bwd_prep", grid=(S // t,),
        in_specs=[_row_spec(t, W), _row_spec(t, W), _row_spec(t, 1), _acc_spec(LANES)],
        out_specs=[_row_spec(t, W), _row_spec(t, LANES)],
        out_shape=[jax.ShapeDtypeStruct((S, W), BF16), jax.ShapeDtypeStruct((S, LANES), F32)],
        compiler_params=_params("parallel"))(dq, dk, pos, invf)


def _mla_norm_bwd(dcq, dckv_a, dckv_b, dkpe, proj_a, q_norm, kv_norm):
    S = proj_a.shape[0]
    t = ROW_TILE

    def body(dcq_ref, da_ref, db_ref, dkpe_ref, a_ref, qn_ref, kvn_ref, o_ref, dqn_ref, dkvn_ref):
        i = pl.program_id(0)
        a = a_ref[...]
        dxq, gq = _rms_bwd_val(dcq_ref[...], a[:, 0:Q_RANK], qn_ref[...])
        dxkv, gkv = _rms_bwd_val(da_ref[...] + db_ref[...], a[:, Q_RANK + LANES:], kvn_ref[...])
        o_ref[:, 0:Q_RANK] = dxq.astype(BF16)
        o_ref[:, Q_RANK:Q_RANK + LANES] = dkpe_ref[...].astype(BF16)
        o_ref[:, Q_RANK + LANES:] = dxkv.astype(BF16)

        @pl.when(i == 0)
        def _():
            dqn_ref[...] = jnp.zeros_like(dqn_ref)
            dkvn_ref[...] = jnp.zeros_like(dkvn_ref)

        dqn_ref[...] += jnp.sum(gq, axis=0, keepdims=True)
        dkvn_ref[...] += jnp.sum(gkv, axis=0, keepdims=True)

    return pl.pallas_call(
        body, name="mla_norm_bwd", grid=(S // t,),
        in_specs=[_row_spec(t, Q_RANK), _row_spec(t, KV_RANK), _row_spec(t, KV_RANK), _row_spec(t, LANES),
                  _row_spec(t, 768), _acc_spec(Q_RANK), _acc_spec(KV_RANK)],
        out_specs=[_row_spec(t, 768), _acc_spec(Q_RANK), _acc_spec(KV_RANK)],
        out_shape=[jax.ShapeDtypeStruct((S, 768), BF16), jax.ShapeDtypeStruct((1, Q_RANK), F32),
                   jax.ShapeDtypeStruct((1, KV_RANK), F32)],
        compiler_params=_params("arbitrary"))(dcq, dckv_a, dckv_b, dkpe, proj_a, q_norm, kv_norm)


DILATIONS = tuple(d for _, d in DIL_PAIRS)


def _to_tokens(view, scr, d):
    if d == 1:
        return view
    n, w = view.shape[0], view.shape[1] // d
    for r in range(d):
        for c in range(w // LANES):
            scr[pl.ds(c, 1), pl.ds(r, n, stride=d), :] = view[:, r * w + c * LANES:r * w + (c + 1) * LANES][None]
    return jnp.concatenate([scr[c] for c in range(w // LANES)], axis=1)


def _from_tokens(tok, scr, d, out_ref):
    if d == 1:
        out_ref[...] = tok.astype(out_ref.dtype)
        return
    n, w = tok.shape[0] // d, tok.shape[1]
    for c in range(w // LANES):
        scr[c] = tok[:, c * LANES:(c + 1) * LANES]
    for r in range(d):
        for c in range(w // LANES):
            out_ref[:, r * w + c * LANES:r * w + (c + 1) * LANES] = scr[pl.ds(c, 1), pl.ds(r, n, stride=d), :][0].astype(out_ref.dtype)


def _token_scratch(t, w):
    return pltpu.VMEM((w // LANES, t, LANES), F32)


def _view_spec(t, d):
    return pl.BlockSpec((t // d, d * DIL_W), lambda i: (i, 0))


def _mix_weights(ls):
    m = jnp.maximum(jnp.maximum(ls[0], ls[1]), ls[2])
    es = [jnp.exp(l - m) for l in ls]
    den = es[0] + es[1] + es[2]
    return [e / den for e in es]


def _dil_mix(o_list, lse_list):
    S = o_list[0].shape[0] * DILATIONS[0]
    t = ROW_TILE
    specs = [_view_spec(t, d) for d in DILATIONS]

    def body(o0, o1, o2, l0, l1, l2, y_ref, *scr):
        os_ = [_to_tokens(r[...], scr[g], d) for g, (r, d) in enumerate(zip((o0, o1, o2), DILATIONS))]
        ws = _mix_weights([_to_tokens(r[...], scr[3 + g], d) for g, (r, d) in enumerate(zip((l0, l1, l2), DILATIONS))])
        y_ref[...] = (ws[0] * os_[0] + ws[1] * os_[1] + ws[2] * os_[2]).astype(BF16)

    return pl.pallas_call(
        body, name="dil_mix", grid=(S // t,), in_specs=specs * 2, out_specs=_row_spec(t, DIL_W),
        out_shape=jax.ShapeDtypeStruct((S, DIL_W), BF16), scratch_shapes=[_token_scratch(t, DIL_W)] * 6,
        compiler_params=_params("parallel"))(*o_list, *lse_list)


def _dil_mix_bwd(dy, o_list, lse_list):
    S = dy.shape[0]
    t = ROW_TILE
    specs = [_view_spec(t, d) for d in DILATIONS]

    def body(dy_ref, o0, o1, o2, l0, l1, l2, d0, d1, d2, e0, e1, e2, *scr):
        os_ = [_to_tokens(r[...], scr[g], d) for g, (r, d) in enumerate(zip((o0, o1, o2), DILATIONS))]
        ws = _mix_weights([_to_tokens(r[...], scr[3 + g], d) for g, (r, d) in enumerate(zip((l0, l1, l2), DILATIONS))])
        dyv = dy_ref[...]
        y = ws[0] * os_[0] + ws[1] * os_[1] + ws[2] * os_[2]
        b = jnp.concatenate(_head_sum_bcast(dyv * y, DIL_HPG), axis=1)
        for g, (w, d_ref, e_ref, d) in enumerate(zip(ws, (d0, d1, d2), (e0, e1, e2), DILATIONS)):
            _from_tokens(w * dyv, scr[6 + g], d, d_ref)
            _from_tokens(w * b, scr[9 + g], d, e_ref)

    shapes = [(S // d, d * DIL_W) for d in DILATIONS]
    return pl.pallas_call(
        body, name="dil_mix_bwd", grid=(S // t,), in_specs=[_row_spec(t, DIL_W)] + specs * 2, out_specs=specs * 2,
        out_shape=[jax.ShapeDtypeStruct(s, BF16) for s in shapes] + [jax.ShapeDtypeStruct(s, F32) for s in shapes],
        scratch_shapes=[_token_scratch(t, DIL_W)] * 12, compiler_params=_params("parallel"))(dy, *o_list, *lse_list)


def _delta(do, o, n_heads, name):
    S, W = do.shape
    t = ROW_TILE

    def body(do_ref, o_ref, d_ref):
        prod = do_ref[...].astype(F32) * o_ref[...].astype(F32)
        d_ref[...] = jnp.concatenate(_head_sum_bcast(prod, n_heads), axis=1)

    return pl.pallas_call(body, name=name, grid=(S // t,), in_specs=[_row_spec(t, W)] * 2, out_specs=_row_spec(t, W),
                          out_shape=jax.ShapeDtypeStruct((S, W), F32), compiler_params=_params("parallel"))(do, o)


def _merge(proj_g, b_gate, b_list):
    S = proj_g.shape[0]
    t = ROW_TILE

    def body(g_ref, b_ref, y0, y1, y2, o_ref):
        acc = jnp.zeros((t, D_MODEL), F32)
        for i, y in enumerate((y0, y1, y2)):
            sl = slice(i * D_MODEL, (i + 1) * D_MODEL)
            acc = acc + jax.nn.sigmoid(g_ref[:, sl] + b_ref[:, sl]) * y[...]
        o_ref[...] = acc.astype(BF16)

    return pl.pallas_call(
        body, name="merge", grid=(S // t,),
        in_specs=[_row_spec(t, 3 * D_MODEL), _acc_spec(3 * D_MODEL)] + [_row_spec(t, D_MODEL)] * 3,
        out_specs=_row_spec(t, D_MODEL), out_shape=jax.ShapeDtypeStruct((S, D_MODEL), BF16),
        compiler_params=_params("parallel"))(proj_g, b_gate, *b_list)


def _merge_bwd(dmerged, proj_g, b_gate, b_list):
    S = proj_g.shape[0]
    t = ROW_TILE

    def body(dm_ref, g_ref, b_ref, y0, y1, y2, d0, d1, d2, dz_ref, db_ref):
        i = pl.program_id(0)

        @pl.when(i == 0)
        def _():
            db_ref[...] = jnp.zeros_like(db_ref)

        dm = dm_ref[...]
        for k, (y, d_ref) in enumerate(zip((y0, y1, y2), (d0, d1, d2))):
            sl = slice(k * D_MODEL, (k + 1) * D_MODEL)
            s = jax.nn.sigmoid(g_ref[:, sl] + b_ref[:, sl])
            d_ref[...] = (s * dm).astype(BF16)
            dz = dm * y[...] * (s * (1.0 - s))
            dz_ref[:, sl] = dz.astype(BF16)
            db_ref[:, sl] += jnp.sum(dz, axis=0, keepdims=True)

    return pl.pallas_call(
        body, name="merge_bwd", grid=(S // t,),
        in_specs=[_row_spec(t, D_MODEL), _row_spec(t, 3 * D_MODEL), _acc_spec(3 * D_MODEL)] + [_row_spec(t, D_MODEL)] * 3,
        out_specs=[_row_spec(t, D_MODEL)] * 3 + [_row_spec(t, 3 * D_MODEL), _acc_spec(3 * D_MODEL)],
        out_shape=[jax.ShapeDtypeStruct((S, D_MODEL), BF16)] * 3
        + [jax.ShapeDtypeStruct((S, 3 * D_MODEL), BF16), jax.ShapeDtypeStruct((1, 3 * D_MODEL), F32)],
        compiler_params=_params("arbitrary"))(dmerged, proj_g, b_gate, *b_list)


def _norm2(o, x, g_post, g_pre):
    S = x.shape[0]
    t = ROW_TILE

    def body(o_ref, x_ref, gp_ref, gf_ref, x1_ref, h2_ref):
        x1 = x_ref[...] + _rms_fwd_val(o_ref[...], gp_ref[...])
        x1_ref[...] = x1
        h2_ref[...] = _rms_fwd_val(x1, gf_ref[...]).astype(BF16)

    return pl.pallas_call(
        body, name="norm2", grid=(S // t,),
        in_specs=[_row_spec(t, D_MODEL)] * 2 + [_acc_spec(D_MODEL)] * 2, out_specs=[_row_spec(t, D_MODEL)] * 2,
        out_shape=[jax.ShapeDtypeStruct((S, D_MODEL), F32), jax.ShapeDtypeStruct((S, D_MODEL), BF16)],
        compiler_params=_params("parallel"))(o, x, g_post, g_pre)


def _norm2_bwd(dx2, dh2, x1, o, g_pre, g_post):
    S = x1.shape[0]
    t = ROW_TILE

    def body(dx2_ref, dh2_ref, x1_ref, o_ref, gf_ref, gp_ref, dx1_ref, do_ref, dgf_ref, dgp_ref):
        i = pl.program_id(0)

        @pl.when(i == 0)
        def _():
            dgf_ref[...] = jnp.zeros_like(dgf_ref)
            dgp_ref[...] = jnp.zeros_like(dgp_ref)

        d1, gf = _rms_bwd_val(dh2_ref[...], x1_ref[...], gf_ref[...])
        dx1 = dx2_ref[...] + d1
        dx1_ref[...] = dx1
        do, gp = _rms_bwd_val(dx1, o_ref[...], gp_ref[...])
        do_ref[...] = do.astype(BF16)
        dgf_ref[...] += jnp.sum(gf, axis=0, keepdims=True)
        dgp_ref[...] += jnp.sum(gp, axis=0, keepdims=True)

    return pl.pallas_call(
        body, name="norm2_bwd", grid=(S // t,),
        in_specs=[_row_spec(t, D_MODEL)] * 4 + [_acc_spec(D_MODEL)] * 2,
        out_specs=[_row_spec(t, D_MODEL)] * 2 + [_acc_spec(D_MODEL)] * 2,
        out_shape=[jax.ShapeDtypeStruct((S, D_MODEL), F32), jax.ShapeDtypeStruct((S, D_MODEL), BF16),
                   jax.ShapeDtypeStruct((1, D_MODEL), F32), jax.ShapeDtypeStruct((1, D_MODEL), F32)],
        compiler_params=_params("arbitrary"))(dx2, dh2, x1, o, g_pre, g_post)


def _norm1_bwd(dx1, dh_list, x, g):
    S = x.shape[0]
    t = ROW_TILE
    dils = (1,) + DILATIONS
    assert len(dh_list) == len(dils)

    def body(dx1_ref, *refs):
        dh_refs, (x_ref, g_ref, dx_ref, dg_ref), scr = refs[:len(dils)], refs[len(dils):len(dils) + 4], refs[len(dils) + 4:]
        i = pl.program_id(0)

        @pl.when(i == 0)
        def _():
            dg_ref[...] = jnp.zeros_like(dg_ref)

        dh = dh_refs[0][...]
        for k in range(1, len(dils)):
            dh = dh + _to_tokens(dh_refs[k][...], scr[k - 1], dils[k])
        d, gg = _rms_bwd_val(dh, x_ref[...], g_ref[...])
        dx_ref[...] = dx1_ref[...] + d
        dg_ref[...] += jnp.sum(gg, axis=0, keepdims=True)

    dh_specs = [pl.BlockSpec((t // d, d * D_MODEL), lambda i: (i, 0)) for d in dils]
    return pl.pallas_call(
        body, name="norm1_bwd", grid=(S // t,),
        in_specs=[_row_spec(t, D_MODEL)] + dh_specs + [_row_spec(t, D_MODEL), _acc_spec(D_MODEL)],
        out_specs=[_row_spec(t, D_MODEL), _acc_spec(D_MODEL)],
        out_shape=[jax.ShapeDtypeStruct((S, D_MODEL), F32), jax.ShapeDtypeStruct((1, D_MODEL), F32)],
        scratch_shapes=[_token_scratch(t, D_MODEL)] * (len(dils) - 1),
        compiler_params=_params("arbitrary"))(dx1, *dh_list, x, g)


def _gain_grad(dy, x, name):
    R, W = x.shape

    def body(dy_ref, x_ref, dg_ref):
        xv = x_ref[...]
        r = lax.rsqrt(jnp.mean(xv * xv, axis=-1, keepdims=True) + RMS_EPS)
        dg_ref[...] = jnp.sum(dy_ref[...] * (xv * r), axis=0, keepdims=True)

    return pl.pallas_call(body, name=name, grid=(1,), in_specs=[_full((R, W))] * 2, out_specs=_full((1, W)),
                          out_shape=jax.ShapeDtypeStruct((1, W), F32), compiler_params=_params("arbitrary"))(dy, x)


def _loss_head(f, x1, tgt, g):
    S = f.shape[0]
    t = ROW_TILE

    def body(f_ref, x1_ref, t_ref, g_ref, loss_ref, dx2_ref, df_ref, dg_ref):
        i = pl.program_id(0)

        @pl.when(i == 0)
        def _():
            loss_ref[...] = jnp.zeros_like(loss_ref)
            dg_ref[...] = jnp.zeros_like(dg_ref)

        fv, gv = f_ref[...], g_ref[...]
        err = x1_ref[...] + _rms_fwd_val(fv, gv) - t_ref[...]
        part = jnp.sum(jnp.mean(err * err, axis=-1, keepdims=True), axis=0, keepdims=True)
        loss_ref[...] += jnp.broadcast_to(0.5 * part, loss_ref.shape)
        dx2 = err * (1.0 / D_MODEL)
        dx2_ref[...] = dx2
        df, gg = _rms_bwd_val(dx2, fv, gv)
        df_ref[...] = df.astype(BF16)
        dg_ref[...] += jnp.sum(gg, axis=0, keepdims=True)

    return pl.pallas_call(
        body, name="loss_head", grid=(S // t,),
        in_specs=[_row_spec(t, D_MODEL)] * 3 + [_acc_spec(D_MODEL)],
        out_specs=[_acc_spec(LANES, 8), _row_spec(t, D_MODEL), _row_spec(t, D_MODEL), _acc_spec(D_MODEL)],
        out_shape=[jax.ShapeDtypeStruct((8, LANES), F32), jax.ShapeDtypeStruct((S, D_MODEL), F32),
                   jax.ShapeDtypeStruct((S, D_MODEL), BF16), jax.ShapeDtypeStruct((1, D_MODEL), F32)],
        compiler_params=_params("arbitrary"))(f, x1, tgt, g)


CONV_TC = 1408
CONV_TT = 256
HALO = 8


def _shift_down(u, halo, first):
    row = lax.broadcasted_iota(jnp.int32, u.shape, 0)
    h6 = jnp.where(first, 0.0, halo[HALO - 2:HALO - 1, :])
    h7 = jnp.where(first, 0.0, halo[HALO - 1:HALO, :])
    s1 = jnp.where(row == 0, h7, pltpu.roll(u, 1, 0))
    s2 = jnp.where(row == 0, h6, jnp.where(row == 1, h7, pltpu.roll(u, 2, 0)))
    return s1, s2


def _conv_specs(tt, n_c, n_t, lead):
    def halo_row(i):
        return jnp.maximum(i * (tt // HALO) - 1, 0) if lead else jnp.minimum((i + 1) * (tt // HALO), n_t * (tt // HALO) - 1)
    return [
        pl.BlockSpec((tt, CONV_TC), lambda j, i: (i, j)),
        pl.BlockSpec((tt, CONV_TC), lambda j, i: (i, j + n_c)),
        pl.BlockSpec((HALO, CONV_TC), lambda j, i: (halo_row(i), j)),
        pl.BlockSpec((HALO, CONV_TC), lambda j, i: (halo_row(i), j + n_c)),
    ]


def _conv_z(ug, uv, hg, hv, w_g, w_v, b_g, b_v, first):
    g1, g2 = _shift_down(ug, hg, first)
    v1, v2 = _shift_down(uv, hv, first)
    zg = b_g + w_g[0:1, :] * g2
    zg = zg + w_g[1:2, :] * g1
    zg = zg + w_g[2:3, :] * ug
    zv = b_v + w_v[0:1, :] * v2
    zv = zv + w_v[1:2, :] * v1
    zv = zv + w_v[2:3, :] * uv
    return zg, zv, (g2, g1, ug), (v2, v1, uv)


def _conv_fwd(u, conv_w, conv_b):
    S = u.shape[0]
    tt = min(CONV_TT, S)
    n_c, n_t = D_FF // CONV_TC, S // tt
    wspec = [pl.BlockSpec((3, CONV_TC), lambda j, i: (0, j)), pl.BlockSpec((3, CONV_TC), lambda j, i: (0, j + n_c)),
             pl.BlockSpec((1, CONV_TC), lambda j, i: (0, j)), pl.BlockSpec((1, CONV_TC), lambda j, i: (0, j + n_c))]

    def body(ug_ref, uv_ref, hg_ref, hv_ref, wg_ref, wv_ref, bg_ref, bv_ref, a_ref):
        first = pl.program_id(1) == 0
        zg, zv, _, _ = _conv_z(ug_ref[...], uv_ref[...], hg_ref, hv_ref, wg_ref, wv_ref, bg_ref[...], bv_ref[...], first)
        a_ref[...] = (zg * jax.nn.sigmoid(zg) * zv).astype(BF16)

    return pl.pallas_call(
        body, name="conv_fwd", grid=(n_c, n_t), in_specs=_conv_specs(tt, n_c, n_t, True) + wspec,
        out_specs=pl.BlockSpec((tt, CONV_TC), lambda j, i: (i, j)), out_shape=jax.ShapeDtypeStruct((S, D_FF), BF16),
        compiler_params=_params("parallel", "parallel"))(u, u, u, u, conv_w, conv_w, conv_b, conv_b)


def _gate_bwd(da, zg, zv):
    sg = jax.nn.sigmoid(zg)
    return da * zv * (sg * (1.0 + zg * (1.0 - sg))), da * (zg * sg)


def _conv_bwd(da, u, conv_w, conv_b):
    S = u.shape[0]
    tt = min(CONV_TT, S)
    n_c, n_t = D_FF // CONV_TC, S // tt
    wspec = [pl.BlockSpec((3, CONV_TC), lambda j, i: (0, j)), pl.BlockSpec((3, CONV_TC), lambda j, i: (0, j + n_c)),
             pl.BlockSpec((1, CONV_TC), lambda j, i: (0, j)), pl.BlockSpec((1, CONV_TC), lambda j, i: (0, j + n_c))]
    tile = pl.BlockSpec((tt, CONV_TC), lambda j, i: (i, j))
    trail = pl.BlockSpec((HALO, CONV_TC), lambda j, i: (jnp.minimum((i + 1) * (tt // HALO), n_t * (tt // HALO) - 1), j))

    def body(da_ref, ug_ref, uv_ref, hg_ref, hv_ref, tda_ref, tg_ref, tv_ref, wg_ref, wv_ref, bg_ref, bv_ref,
             dug_ref, duv_ref, dwg_ref, dwv_ref, dbg_ref, dbv_ref):
        i = pl.program_id(1)
        last = i == n_t - 1
        bg, bv = bg_ref[...], bv_ref[...]
        zg, zv, gs, vs = _conv_z(ug_ref[...], uv_ref[...], hg_ref, hv_ref, wg_ref, wv_ref, bg, bv, i == 0)
        dzg, dzv = _gate_bwd(da_ref[...], zg, zv)
        tzg, tzv, _, _ = _conv_z(tg_ref[...], tv_ref[...], ug_ref.at[pl.ds(tt - HALO, HALO), :], uv_ref.at[pl.ds(tt - HALO, HALO), :],
                                 wg_ref, wv_ref, bg, bv, False)
        tdzg, tdzv = _gate_bwd(tda_ref[...], tzg, tzv)
        row = lax.broadcasted_iota(jnp.int32, dzg.shape, 0)
        for dz, tdz, w_ref, du_ref in ((dzg, tdzg, wg_ref, dug_ref), (dzv, tdzv, wv_ref, duv_ref)):
            h0 = jnp.where(last, 0.0, tdz[0:1, :])
            h1 = jnp.where(last, 0.0, tdz[1:2, :])
            u1 = jnp.where(row == tt - 1, h0, pltpu.roll(dz, tt - 1, 0))
            u2 = jnp.where(row == tt - 1, h1, jnp.where(row == tt - 2, h0, pltpu.roll(dz, tt - 2, 0)))
            du_ref[...] = (w_ref[2:3, :] * dz + w_ref[1:2, :] * u1 + w_ref[0:1, :] * u2).astype(BF16)

        @pl.when(i == 0)
        def _():
            for r in (dwg_ref, dwv_ref, dbg_ref, dbv_ref):
                r[...] = jnp.zeros_like(r)

        for k in range(3):
            dwg_ref[k:k + 1, :] += jnp.sum(dzg * gs[k], axis=0, keepdims=True)
            dwv_ref[k:k + 1, :] += jnp.sum(dzv * vs[k], axis=0, keepdims=True)
        dbg_ref[...] += jnp.sum(dzg, axis=0, keepdims=True)
        dbv_ref[...] += jnp.sum(dzv, axis=0, keepdims=True)

    lead = _conv_specs(tt, n_c, n_t, True)
    trail_v = pl.BlockSpec((HALO, CONV_TC), lambda j, i: (jnp.minimum((i + 1) * (tt // HALO), n_t * (tt // HALO) - 1), j + n_c))
    outs = pl.pallas_call(
        body, name="conv_bwd", grid=(n_c, n_t), in_specs=[tile] + lead + [trail, trail, trail_v] + wspec,
        out_specs=[tile, tile] + [pl.BlockSpec((3, CONV_TC), lambda j, i: (0, j))] * 2 + [pl.BlockSpec((1, CONV_TC), lambda j, i: (0, j))] * 2,
        out_shape=[jax.ShapeDtypeStruct((S, D_FF), BF16)] * 2 + [jax.ShapeDtypeStruct((3, D_FF), F32)] * 2
        + [jax.ShapeDtypeStruct((1, D_FF), F32)] * 2,
        compiler_params=_params("parallel", "arbitrary"))(da, u, u, u, u, da, u, u, conv_w, conv_w, conv_b, conv_b)
    dug, duv, dwg, dwv, dbg, dbv = outs
    return dug, duv, jnp.concatenate([dwg, dwv], axis=1), jnp.concatenate([dbg, dbv], axis=1)


BAND = 128


MEM_TQ = 512


def _mem_fwd(q, kv, *, name):
    S, W = q.shape
    M = kv.shape[0]
    nh = W // LANES
    tq = min(MEM_TQ, S)

    def body(q_ref, k_ref, v_ref, o_ref, l_ref):
        s = lax.dot_general(q_ref[...], k_ref[...].astype(BF16), NT, preferred_element_type=F32)
        m = jnp.max(s, axis=1, keepdims=True)
        p = jnp.exp(s - m)
        l = jnp.sum(p, axis=1, keepdims=True)
        o_ref[...] = (lax.dot_general(p.astype(BF16), v_ref[...].astype(BF16), NN, preferred_element_type=F32) / l).astype(BF16)
        l_ref[...] = jnp.broadcast_to(m + jnp.log(l), (tq, LANES))

    blk = pl.BlockSpec((tq, LANES), lambda hh, i: (i, hh))
    return pl.pallas_call(
        body, name=name, grid=(nh, S // tq),
        in_specs=[blk, pl.BlockSpec((M, LANES), lambda hh, i: (0, hh)), pl.BlockSpec((M, LANES), lambda hh, i: (0, hh + nh))],
        out_specs=[blk, blk], out_shape=[jax.ShapeDtypeStruct((S, W), BF16), jax.ShapeDtypeStruct((S, W), F32)],
        compiler_params=_params("parallel", "parallel"))(q, kv, kv)


def _mem_bwd(q, kv, do, lse, delta, *, scale, name):
    S, W = q.shape
    M = kv.shape[0]
    nh = W // LANES
    tq = min(MEM_TQ, S)

    def body(q_ref, k_ref, v_ref, do_ref, l_ref, d_ref, dq_ref, dk_ref, dv_ref):
        i = pl.program_id(1)

        @pl.when(i == 0)
        def _():
            dk_ref[...] = jnp.zeros_like(dk_ref)
            dv_ref[...] = jnp.zeros_like(dv_ref)

        qv = q_ref[...]
        kv_, vv = k_ref[...].astype(BF16), v_ref[...].astype(BF16)
        dov = do_ref[...].astype(BF16)
        s = lax.dot_general(qv, kv_, NT, preferred_element_type=F32)
        p = jnp.exp(s - l_ref[...][:, 0:1])
        dp = lax.dot_general(dov, vv, NT, preferred_element_type=F32)
        ds = (p * (dp - d_ref[...][:, 0:1])).astype(BF16)
        dq_ref[...] = lax.dot_general(ds, kv_, NN, preferred_element_type=F32) * scale
        dk_ref[...] += lax.dot_general(ds, qv, TN, preferred_element_type=F32)
        dv_ref[...] += lax.dot_general(p.astype(BF16), dov, TN, preferred_element_type=F32)

    blk = pl.BlockSpec((tq, LANES), lambda hh, i: (i, hh))
    kblk = pl.BlockSpec((M, LANES), lambda hh, i: (0, hh))
    vblk = pl.BlockSpec((M, LANES), lambda hh, i: (0, hh + nh))
    dq, dk, dv = pl.pallas_call(
        body, name=name, grid=(nh, S // tq), in_specs=[blk, kblk, vblk, blk, blk, blk], out_specs=[blk, kblk, kblk],
        out_shape=[jax.ShapeDtypeStruct((S, W), F32), jax.ShapeDtypeStruct((M, W), F32), jax.ShapeDtypeStruct((M, W), F32)],
        compiler_params=_params("parallel", "arbitrary"))(q, kv, kv, do, lse, delta)
    return dq, jnp.concatenate([dk, dv], axis=1)


CAUSAL_BLOCK = 512
STRIP = 32


def _causal_fwd(q, k, v, *, name):
    S, W = q.shape
    T = min(CAUSAL_BLOCK, S // 2)
    n_strips = T // STRIP

    def body(q_ref, k_ref, v_ref, o_ref, lse_ref, s0, s1, p0, p1, a0, a1, acc_scr):
        i = pl.program_id(1)
        s_scr, p_scr, a_scr = (s0, s1), (p0, p1), (a0, a1)

        def rows(j):
            return pl.ds(pl.multiple_of(j * T, T), T)

        def scores(j, slot):
            s_scr[slot][...] = lax.dot_general(q_ref[...], k_ref[rows(j), :], NT, preferred_element_type=F32)

        def softmax(slot, stats, diag):
            def strip(r):
                s = s_scr[slot][r * STRIP:(r + 1) * STRIP, :]
                if diag:
                    row = r * STRIP + lax.broadcasted_iota(jnp.int32, s.shape, 0)
                    s = jnp.where(row >= lax.broadcasted_iota(jnp.int32, s.shape, 1), s, NEG_INF)
                return s

            m_new = [jnp.maximum(m_old, jnp.max(strip(r), axis=1, keepdims=True)) for r, (m_old, _) in enumerate(stats)]
            new = []
            for r, (m_old, l_old) in enumerate(stats):
                rs = slice(r * STRIP, (r + 1) * STRIP)
                p = jnp.exp(strip(r) - m_new[r])
                alpha = jnp.exp(m_old - m_new[r])
                new.append((m_new[r], alpha * l_old + jnp.sum(p, axis=1, keepdims=True)))
                a_scr[slot][rs, :] = alpha
                p_scr[slot][rs, :] = p.astype(BF16)
            return tuple(new)

        def values(j, slot):
            acc_scr[...] = a_scr[slot][...] * acc_scr[...] + lax.dot_general(p_scr[slot][...], v_ref[rows(j), :], NN,
                                                                            preferred_element_type=F32)

        def trip(j, stats, mine, other):
            scores(j + 1, other)
            stats = softmax(mine, stats, False)
            values(jnp.maximum(j - 1, 0), other)
            return stats

        def pair(jj, stats):
            return trip(2 * jj + 1, trip(2 * jj, stats, 0, 1), 1, 0)

        def last(stats, mine, other):
            stats = softmax(mine, stats, True)
            values(jnp.maximum(i - 1, 0), other)
            values(i, mine)
            for r, (m, l) in enumerate(stats):
                rs = slice(r * STRIP, (r + 1) * STRIP)
                o_ref[rs, :] = (acc_scr[rs, :] / l).astype(BF16)
                lse_ref[rs, :] = jnp.broadcast_to(m + jnp.log(l), (STRIP, LANES))

        acc_scr[...] = jnp.zeros_like(acc_scr)
        p1[...] = jnp.zeros_like(p1)
        a1[...] = jnp.ones_like(a1)
        scores(0, 0)
        init = tuple((jnp.full((STRIP, 1), NEG_INF, F32), jnp.zeros((STRIP, 1), F32)) for _ in range(n_strips))
        stats = lax.fori_loop(0, i // 2, pair, init)

        @pl.when(i % 2 == 1)
        def _():
            last(trip(i - 1, stats, 0, 1), 1, 0)

        @pl.when(i % 2 == 0)
        def _():
            last(stats, 0, 1)

    blk = pl.BlockSpec((T, LANES), lambda hh, i: (i, hh))
    whole = pl.BlockSpec((S, LANES), lambda hh, i: (0, hh))
    return pl.pallas_call(
        body, name=name, grid=(W // LANES, S // T), in_specs=[blk, whole, whole], out_specs=[blk, blk],
        out_shape=[jax.ShapeDtypeStruct((S, W), BF16), jax.ShapeDtypeStruct((S, W), F32)],
        scratch_shapes=[pltpu.VMEM((T, T), F32)] * 2 + [pltpu.VMEM((T, T), BF16)] * 2 + [pltpu.VMEM((T, 1), F32)] * 2
        + [pltpu.VMEM((T, LANES), F32)],
        compiler_params=_params("parallel", "arbitrary"))(q, k, v)


def _causal_bwd(q, k, v, do, lse, delta, *, name):
    S, W = q.shape
    T = min(CAUSAL_BLOCK, S // 2)
    nq = S // T
    n_strips, n_col = T // STRIP, T // LANES

    def body(q_ref, k_ref, v_ref, do_ref, l_ref, d_ref, dq_ref, dk_ref, dv_ref, s0, s1, e0, e1, p0, p1, g0, g1):
        j = pl.program_id(1)
        s_scr, e_scr, p_scr, g_scr = (s0, s1), (e0, e1), (p0, p1), (g0, g1)

        @pl.when(j == 0)
        def _():
            dq_ref[...] = jnp.zeros_like(dq_ref)

        dk_ref[...] = jnp.zeros_like(dk_ref)
        dv_ref[...] = jnp.zeros_like(dv_ref)

        def rows(i):
            return pl.ds(pl.multiple_of(jnp.minimum(i, nq - 1) * T, T), T)

        def products(i, slot):
            r = rows(i)
            s_scr[slot][...] = lax.dot_general(q_ref[r, :], k_ref[...], NT, preferred_element_type=F32)
            e_scr[slot][...] = lax.dot_general(do_ref[r, :].astype(BF16), v_ref[...], NT, preferred_element_type=F32)

        def pointwise(i, slot, diag):
            base = pl.multiple_of(i * T, T)
            for r in range(n_strips):
                rs = slice(r * STRIP, (r + 1) * STRIP)
                lse_r = l_ref[pl.ds(base + r * STRIP, STRIP), :]
                del_r = d_ref[pl.ds(base + r * STRIP, STRIP), :]
                for c in range(n_col):
                    cs = slice(c * LANES, (c + 1) * LANES)
                    if diag and c * LANES > (r + 1) * STRIP - 1:
                        p_scr[slot][rs, cs] = jnp.zeros((STRIP, LANES), BF16)
                        g_scr[slot][rs, cs] = jnp.zeros((STRIP, LANES), BF16)
                        continue
                    sv = s_scr[slot][rs, cs]
                    if diag and (c + 1) * LANES - 1 > r * STRIP:
                        row = r * STRIP + lax.broadcasted_iota(jnp.int32, sv.shape, 0)
                        col = c * LANES + lax.broadcasted_iota(jnp.int32, sv.shape, 1)
                        sv = jnp.where(row >= col, sv, NEG_INF)
                    p = jnp.exp(sv - lse_r)
                    p_scr[slot][rs, cs] = p.astype(BF16)
                    g_scr[slot][rs, cs] = (p * (e_scr[slot][rs, cs] - del_r)).astype(BF16)

        def gradients(i, slot):
            r = rows(i)
            qi, doi = q_ref[r, :], do_ref[r, :].astype(BF16)
            g = g_scr[slot][...]
            dv_ref[...] += lax.dot_general(p_scr[slot][...], doi, TN, preferred_element_type=F32)
            dk_ref[...] += lax.dot_general(g, qi, TN, preferred_element_type=F32)
            dq_ref[r, :] += lax.dot_general(g, k_ref[...], NN, preferred_element_type=F32)

        p1[...] = jnp.zeros_like(p1)
        g1[...] = jnp.zeros_like(g1)
        products(j, 0)
        products(j + 1, 1)
        pointwise(j, 0, True)
        gradients(j, 1)

        def trip(i, mine, other):
            products(i + 1, other)
            pointwise(i, mine, False)
            gradients(i - 1, other)

        def pair(t, _):
            trip(j + 1 + 2 * t, 1, 0)
            trip(j + 2 + 2 * t, 0, 1)
            return 0

        n_rest = nq - 1 - j
        lax.fori_loop(0, n_rest // 2, pair, 0)

        @pl.when(n_rest % 2 == 1)
        def _():
            trip(nq - 1, 1, 0)
            gradients(nq - 1, 1)

        @pl.when(n_rest % 2 == 0)
        def _():
            gradients(nq - 1, 0)

    blk = pl.BlockSpec((T, LANES), lambda hh, j: (j, hh))
    whole = pl.BlockSpec((S, LANES), lambda hh, j: (0, hh))
    return pl.pallas_call(
        body, name=name, grid=(W // LANES, S // T), in_specs=[whole, blk, blk, whole, whole, whole],
        out_specs=[whole, blk, blk], out_shape=[jax.ShapeDtypeStruct((S, W), F32)] * 3,
        scratch_shapes=[pltpu.VMEM((T, T), F32)] * 4 + [pltpu.VMEM((T, T), BF16)] * 4,
        compiler_params=_params("parallel", "arbitrary"))(q, k, v, do, lse, delta)


BAND_TQ = 512


def _band_window(i, sub, nsub, L, q_ref, k_ref, v_ref, slope):
    kw = min(2 * BAND, L)
    n = i * nsub + sub
    k0 = 0 if kw == L else pl.multiple_of(jnp.maximum(n - 1, 0) * BAND, BAND)
    win = pl.ds(k0, kw)
    qs = q_ref[sub * BAND:(sub + 1) * BAND, :]
    kwv, vwv = k_ref[win, :], v_ref[win, :]
    s = lax.dot_general(qs, kwv, NT, preferred_element_type=F32)
    dist = (n * BAND + lax.broadcasted_iota(jnp.int32, s.shape, 0)) - (k0 + lax.broadcasted_iota(jnp.int32, s.shape, 1))
    s = jnp.where((dist >= 0) & (dist <= BAND), s - slope * dist.astype(F32), NEG_INF)
    return win, qs, kwv, vwv, s


def _band_fwd(q, k, v, slopes, *, n_heads, qcol, kcol, vcol, L, slope_mul, o_shape, name):
    tq = min(BAND_TQ, L)
    nsub = tq // BAND

    def body(sl_ref, q_ref, k_ref, v_ref, o_ref, l_ref):
        hh, i = pl.program_id(0), pl.program_id(1)
        slope = sl_ref[hh % DIL_HPG] * slope_mul
        for sub in range(nsub):
            _, _, _, vwv, s = _band_window(i, sub, nsub, L, q_ref, k_ref, v_ref, slope)
            m = jnp.max(s, axis=1, keepdims=True)
            p = jnp.exp(s - m)
            l = jnp.sum(p, axis=1, keepdims=True)
            rows = slice(sub * BAND, (sub + 1) * BAND)
            o_ref[rows, :] = lax.dot_general(p.astype(BF16), vwv, NN, preferred_element_type=F32) / l
            l_ref[rows, :] = jnp.broadcast_to(m + jnp.log(l), (BAND, LANES))

    whole = lambda col: pl.BlockSpec((L, LANES), lambda hh, i: (0, col(hh)))
    o_spec = pl.BlockSpec((tq, LANES), lambda hh, i: (i, hh))
    return pl.pallas_call(
        body, name=name, grid=(n_heads, L // tq),
        in_specs=[pl.BlockSpec(memory_space=pltpu.SMEM), pl.BlockSpec((tq, LANES), lambda hh, i: (i, qcol(hh))), whole(kcol), whole(vcol)],
        out_specs=[o_spec, o_spec], out_shape=[jax.ShapeDtypeStruct(o_shape, F32)] * 2,
        compiler_params=_params("parallel", "arbitrary"))(slopes, q, k, v)


def _band_bwd(q, k, v, do, lse, delta, slopes, *, n_heads, qcol, kcol, vcol, L, scale, slope_mul, d_shape, name):
    tq = min(BAND_TQ, L)
    nsub = tq // BAND
    n_steps = L // tq

    def body(sl_ref, q_ref, k_ref, v_ref, do_ref, l_ref, d_ref, dq_ref, dk_ref, dv_ref, dk_acc, dv_acc):
        hh, i = pl.program_id(0), pl.program_id(1)
        slope = sl_ref[hh % DIL_HPG] * slope_mul

        @pl.when(i == 0)
        def _():
            dk_acc[...] = jnp.zeros_like(dk_acc)
            dv_acc[...] = jnp.zeros_like(dv_acc)

        for sub in range(nsub):
            win, qs, kwv, vwv, s = _band_window(i, sub, nsub, L, q_ref, k_ref, v_ref, slope)
            rows = slice(sub * BAND, (sub + 1) * BAND)
            dos = do_ref[rows, :]
            p = jnp.exp(s - l_ref[rows, :][:, 0:1])
            dp = lax.dot_general(dos, vwv, NT, preferred_element_type=F32)
            ds = (p * (dp - d_ref[rows, :][:, 0:1])).astype(BF16)
            dq_ref[rows, :] = (lax.dot_general(ds, kwv, NN, preferred_element_type=F32) * scale).astype(BF16)
            dk_acc[win, :] += lax.dot_general(ds, qs, TN, preferred_element_type=F32)
            dv_acc[win, :] += lax.dot_general(p.astype(BF16), dos, TN, preferred_element_type=F32)

        @pl.when(i == n_steps - 1)
        def _():
            dk_ref[...] = dk_acc[...].astype(BF16)
            dv_ref[...] = dv_acc[...].astype(BF16)

    whole = lambda col: pl.BlockSpec((L, LANES), lambda hh, i: (0, col(hh)))
    blk = pl.BlockSpec((tq, LANES), lambda hh, i: (i, hh))
    ident = lambda hh: hh
    return pl.pallas_call(
        body, name=name, grid=(n_heads, n_steps),
        in_specs=[pl.BlockSpec(memory_space=pltpu.SMEM), pl.BlockSpec((tq, LANES), lambda hh, i: (i, qcol(hh))), whole(kcol), whole(vcol),
                  blk, blk, blk],
        out_specs=[blk, whole(ident), whole(ident)], out_shape=[jax.ShapeDtypeStruct(d_shape, BF16)] * 3,
        scratch_shapes=[pltpu.VMEM((L, LANES), F32)] * 2,
        compiler_params=_params("parallel", "arbitrary"))(slopes, q, k, v, do, lse, delta)


def _pad_heads(w, n_heads, width, axis):
    shp = w.shape
    new = shp[:axis] + (n_heads, width) + shp[axis + 1:]
    pad = [(0, 0)] * len(new)
    pad[axis + 1] = (0, LANES - width)
    out = jnp.pad(w.reshape(new), pad)
    return out.reshape(shp[:axis] + (n_heads * LANES,) + shp[axis + 1:])


def _unpad_heads(w, n_heads, width, axis):
    shp = w.shape
    new = shp[:axis] + (n_heads, LANES) + shp[axis + 1:]
    out = lax.slice_in_dim(w.reshape(new), 0, width, axis=axis + 1)
    return out.reshape(shp[:axis] + (n_heads * width,) + shp[axis + 1:])


def _alibi_slopes():
    s = jnp.exp2(-8.0 * jnp.arange(1, DIL_HEADS + 1, dtype=F32) / DIL_HEADS)
    return s.reshape(DIL_HPG, DIL_GROUPS).T


def _local_step(x, mem, positions, tgt, W):
    S = x.shape[0]
    pos = positions.reshape(S, 1).astype(F32)
    half = MLA_ROPE // 2
    inv_freq = ROPE_THETA ** (-jnp.arange(half, dtype=F32) / half)
    invf = jnp.zeros((1, LANES), F32).at[0, MLA_NOPE:MLA_NOPE + half].set(inv_freq).at[0, MLA_NOPE + half:MLA_QK].set(inv_freq)
    slopes = _alibi_slopes()

    w_in = W["w_in"]
    zc = lambda n: jnp.zeros((D_MODEL, n), BF16)
    w_a = jnp.concatenate([w_in[:, :OFF_Q], zc(MLA_NOPE), w_in[:, OFF_KV:OFF_KR], zc(LANES - MLA_QK), w_in[:, OFF_Q:OFF_KV]], axis=1)
    w_d, w_m, w_g = w_in[:, OFF_KR:OFF_DIL], w_in[:, OFF_DIL:OFF_MEMQ], w_in[:, OFF_MEMQ:]
    w_uq_p = _pad_heads(W["w_uq"], MLA_HEADS, MLA_QK, 1)
    ukv = W["w_ukv"].reshape(KV_RANK, MLA_HEADS, 2 * MLA_NOPE)
    w_uk_p = _pad_heads(ukv[:, :, :MLA_NOPE].reshape(KV_RANK, -1), MLA_HEADS, MLA_NOPE, 1)
    w_uv_p = _pad_heads(ukv[:, :, MLA_NOPE:].reshape(KV_RANK, -1), MLA_HEADS, MLA_NOPE, 1)
    w_br_mla_p = _pad_heads(W["w_br_mla"], MLA_HEADS, MLA_NOPE, 0)

    h = _rmsnorm(x, W["g_pre_mix"], BF16, "norm1")
    proj_a = _matmul(h, w_a, tn=768, name="proj_a")
    att_scale = LANES ** -0.5
    w_dg = [jnp.concatenate([w_d[:, (part * DIL_GROUPS + g) * DIL_W:(part * DIL_GROUPS + g + 1) * DIL_W] for part in range(3)], axis=1)
            for g in range(DIL_GROUPS)]
    proj_m = _matmul(h, w_m, tn=MEM_W, out_dtype=BF16, scale=(1, att_scale), name="proj_m")
    proj_g = _matmul(h, w_g, tn=1536, name="proj_g")

    cq_n, ckv_n, kpe = _mla_prep(proj_a, pos, invf, W["mla_q_norm"], W["mla_kv_norm"])
    q_pre = _matmul(cq_n, w_uq_p, tn=1024, name="mla_q")
    k_pre = _matmul(ckv_n, w_uk_p, tn=1024, name="mla_k")
    v_mla = _matmul(ckv_n, w_uv_p, tn=1024, out_dtype=BF16, name="mla_v")
    q_mla, k_mla = _qk_final(q_pre, k_pre, kpe, pos, invf)
    ident = lambda hh: hh
    o_mla, lse_mla = _causal_fwd(q_mla, k_mla, v_mla, name="attn_mla_fwd")

    col = lambda part: (lambda hh: (hh // DIL_HPG) * 3 * DIL_HPG + part * DIL_HPG + hh % DIL_HPG)
    h_views = [h.reshape(S // d, d * D_MODEL) for d in DILATIONS]
    proj_dg, o_dil, lse_dil = [], [], []
    for g, d in enumerate(DILATIONS):
        pv = _proj_view(h_views[g], w_dg[g], d, att_scale, f"proj_d{g}")
        o, lse = _band_fwd(pv, pv, pv, slopes[g], n_heads=d * DIL_HPG, qcol=col(0), kcol=col(1), vcol=col(2), L=S // d,
                           slope_mul=float(d), o_shape=(S // d, d * DIL_W), name=f"attn_dil{g}_fwd")
        proj_dg.append(pv)
        o_dil.append(o)
        lse_dil.append(lse)
    y_dil = _dil_mix(o_dil, lse_dil)

    mem_n = _rmsnorm(mem, W["g_mem"], BF16, "mem_norm")
    kv_mem = _matmul(mem_n, W["w_mem_kv"], name="mem_kv")
    o_mem, lse_mem = _mem_fwd(proj_m, kv_mem, name="attn_mem_fwd")

    b_mla = _matmul(o_mla, w_br_mla_p, name="br_mla")
    b_dil = _matmul(y_dil, W["w_br_dil"], name="br_dil")
    b_mem = _matmul(o_mem, W["w_br_mem"], name="br_mem")
    merged = _merge(proj_g, W["b_gate"], [b_mla, b_dil, b_mem])
    o_proj = _matmul(merged, W["w_o"], name="o_proj")
    x1, h2 = _norm2(o_proj, x, W["g_post_mix"], W["g_pre_ffn"])

    u = _matmul(h2, W["w_ffn_up"], tn=1408, name="ffn_up")
    act = _conv_fwd(u, W["conv_w"], W["conv_b"])
    f = _matmul(act, W["w_ffn_down"], name="ffn_down")
    loss8, dx2, df, dg_post_ffn = _loss_head(f, x1, tgt, W["g_post_ffn"])
    loss = loss8[0, 0]

    G = {"g_post_ffn": dg_post_ffn}
    d_act = _matmul(df, W["w_ffn_down"], mode="nt", tn=1408, name="d_act")
    G["w_ffn_down"] = _matmul(act, df, mode="tn", tm=1408, tk=1024, name="dw_ffn_down")
    du_g, du_v, G["conv_w"], G["conv_b"] = _conv_bwd(d_act, u, W["conv_w"], W["conv_b"])
    dh2 = _matmul(du_g, W["w_ffn_up"][:, :D_FF], mode="nt", name="d_h2_gate")
    dh2 = _matmul(du_v, W["w_ffn_up"][:, D_FF:], mode="nt", add=dh2, name="d_h2_value")
    G["w_ffn_up"] = jnp.concatenate([_matmul(h2, du_g, mode="tn", tn=1408, tk=1024, name="dw_ffn_up_gate"),
                                     _matmul(h2, du_v, mode="tn", tn=1408, tk=1024, name="dw_ffn_up_value")], axis=1)
    dx1, do_proj, G["g_pre_ffn"], G["g_post_mix"] = _norm2_bwd(dx2, dh2, x1, o_proj, W["g_pre_ffn"], W["g_post_mix"])
    dmerged = _matmul(do_proj, W["w_o"], mode="nt", name="d_merged")
    G["w_o"] = _matmul(merged, do_proj, mode="tn", tk=1024, name="dw_o")
    db_mla, db_dil, db_mem, dproj_g, G["b_gate"] = _merge_bwd(dmerged, proj_g, W["b_gate"], [b_mla, b_dil, b_mem])

    dy_mem = _matmul(db_mem, W["w_br_mem"], mode="nt", name="d_y_mem")
    G["w_br_mem"] = _matmul(o_mem, db_mem, mode="tn", tk=1024, name="dw_br_mem")
    delta_mem = _delta(dy_mem, o_mem, MEM_HEADS, "delta_mem")
    dq_mem, dkv_mem = _mem_bwd(proj_m, kv_mem, dy_mem, lse_mem, delta_mem, scale=att_scale, name="attn_mem_bwd")
    G["w_mem_kv"] = _matmul(mem_n, dkv_mem, mode="tn", name="dw_mem_kv")
    dmem_n = _matmul(dkv_mem, W["w_mem_kv"], mode="nt", name="d_mem_n")
    G["g_mem"] = _gain_grad(dmem_n, mem, "dg_mem")

    dy_dil = _matmul(db_dil, W["w_br_dil"], mode="nt", name="d_y_dil")
    G["w_br_dil"] = _matmul(y_dil, db_dil, mode="tn", tk=1024, name="dw_br_dil")
    mix = _dil_mix_bwd(dy_dil, o_dil, lse_dil)
    do_dil, dl_dil = mix[:3], mix[3:]
    dh_views, dw_groups = [], []
    for g, d in enumerate(DILATIONS):
        pv = proj_dg[g]
        parts = _band_bwd(pv, pv, pv, do_dil[g], lse_dil[g], dl_dil[g], slopes[g], n_heads=d * DIL_HPG, qcol=col(0),
                          kcol=col(1), vcol=col(2), L=S // d, scale=att_scale, slope_mul=float(d), d_shape=(S // d, d * DIL_W),
                          name=f"attn_dil{g}_bwd")
        dpv = jnp.concatenate([p[:, r * DIL_W:(r + 1) * DIL_W] for r in range(d) for p in parts], axis=1)
        dh_views.append(_dh_view(dpv, w_dg[g], d, None, f"d_h_d{g}"))
        dw_groups.append(_dw_view(h_views[g], dpv, d, f"dw_in_d{g}"))
    dw_d = jnp.concatenate([dw[:, part * DIL_W:(part + 1) * DIL_W] for part in range(3) for dw in dw_groups], axis=1)

    dy_mla = _matmul(db_mla, w_br_mla_p, mode="nt", name="d_y_mla")
    dw_br_mla_p = _matmul(o_mla, db_mla, mode="tn", tk=1024, name="dw_br_mla")
    G["w_br_mla"] = _unpad_heads(dw_br_mla_p, MLA_HEADS, MLA_NOPE, 0)
    delta_mla = _delta(dy_mla, o_mla, MLA_HEADS, "delta_mla")
    dq_mla, dk_mla, dv_mla = _causal_bwd(q_mla, k_mla, v_mla, dy_mla, lse_mla, delta_mla, name="attn_mla_bwd")
    dq_pre, dkpe = _mla_bwd_prep(dq_mla, dk_mla, pos, invf)
    dcq_n = _matmul(dq_pre, w_uq_p, mode="nt", tn=Q_RANK, name="d_cq")
    G["w_uq"] = _unpad_heads(_matmul(cq_n, dq_pre, mode="tn", tm=Q_RANK, tk=1024, name="dw_uq"), MLA_HEADS, MLA_QK, 1)
    dckv_a = _matmul(dk_mla, w_uk_p, mode="nt", tn=KV_RANK, name="d_ckv_k")
    dckv_b = _matmul(dv_mla, w_uv_p, mode="nt", tn=KV_RANK, name="d_ckv_v")
    dw_uk = _unpad_heads(_matmul(ckv_n, dk_mla, mode="tn", tm=KV_RANK, tk=1024, name="dw_uk"), MLA_HEADS, MLA_NOPE, 1)
    dw_uv = _unpad_heads(_matmul(ckv_n, dv_mla, mode="tn", tm=KV_RANK, tk=1024, name="dw_uv"), MLA_HEADS, MLA_NOPE, 1)
    G["w_ukv"] = jnp.concatenate([dw_uk.reshape(KV_RANK, MLA_HEADS, MLA_NOPE), dw_uv.reshape(KV_RANK, MLA_HEADS, MLA_NOPE)],
                                 axis=2).reshape(KV_RANK, -1)
    dproj_a, G["mla_q_norm"], G["mla_kv_norm"] = _mla_norm_bwd(dcq_n, dckv_a, dckv_b, dkpe, proj_a, W["mla_q_norm"],
                                                              W["mla_kv_norm"])

    dh = _matmul(dproj_a, w_a, mode="nt", name="d_h_a")
    dh = _matmul(dq_mem, w_m, mode="nt", add=dh, name="d_h_m")
    dh = _matmul(dproj_g, w_g, mode="nt", add=dh, name="d_h_g")
    dw_a = _matmul(h, dproj_a, mode="tn", tn=768, tk=1024, name="dw_in_a")
    dw_m = _matmul(h, dq_mem, mode="tn", tn=512, tk=1024, name="dw_in_m")
    dw_g = _matmul(h, dproj_g, mode="tn", tn=1536, tk=1024, name="dw_in_g")
    kr0 = Q_RANK + MLA_NOPE
    G["w_in"] = jnp.concatenate([dw_a[:, :Q_RANK], dw_a[:, Q_RANK + LANES:], dw_a[:, kr0:kr0 + MLA_ROPE], dw_d, dw_m, dw_g], axis=1)
    grad_x, G["g_pre_mix"] = _norm1_bwd(dx1, [dh] + dh_views, x, W["g_pre_mix"])
    return loss, grad_x, G


WEIGHTS = ["g_pre_mix", "w_in", "b_gate", "mla_q_norm", "w_uq", "mla_kv_norm", "w_ukv", "g_mem", "w_mem_kv", "w_br_mla",
           "w_br_dil", "w_br_mem", "w_o", "g_post_mix", "g_pre_ffn", "w_ffn_up", "conv_w", "conv_b", "w_ffn_down", "g_post_ffn"]
GROUPS = [
    [("w_in", (D_MODEL, D_IN), 1)],
    [("w_uq", (Q_RANK, MLA_HEADS * MLA_QK), 1)],
    [("w_ukv", (KV_RANK, MLA_HEADS * 2 * MLA_NOPE), 1), ("w_br_mla", (MLA_HEADS * MLA_NOPE, D_MODEL), 1),
     ("w_br_dil", (DIL_W, D_MODEL), 1), ("w_br_mem", (MEM_W, D_MODEL), 1)],
    [("w_mem_kv", (D_MODEL, 2 * MEM_W), 0), ("w_o", (D_MODEL, D_MODEL), 0), ("w_ffn_down", (D_FF, D_MODEL), 0)],
    [("w_ffn_up", (D_MODEL, 2 * D_FF), 1)],
]
CONV_W = ("conv_w", (3, 2 * D_FF), 1)
REPLICATED = [("g_pre_mix", D_MODEL), ("b_gate", 3 * D_MODEL), ("mla_q_norm", Q_RANK), ("mla_kv_norm", KV_RANK), ("g_mem", D_MODEL),
              ("g_post_mix", D_MODEL), ("g_pre_ffn", D_MODEL), ("conv_b", 2 * D_FF), ("g_post_ffn", D_MODEL)]
SMALL_ROWS = 256
CONV_AT = sum(n for _, n in REPLICATED)
LOSS_AT = CONV_AT + 3 * 2 * D_FF


def _shard_shape(shape, axis):
    return tuple(d // N_CHIPS if a == axis else d for a, d in enumerate(shape))


def _group_shape(grp):
    shapes = [_shard_shape(shape, axis) for _, shape, axis in grp]
    assert len({s[1] for s in shapes}) == 1
    return sum(s[0] for s in shapes), shapes[0][1]


def _member_shards(a, axis):
    r, c = a.shape
    if axis == 0:
        return a.reshape(N_CHIPS, r // N_CHIPS, c)
    return a.reshape(r, N_CHIPS, c // N_CHIPS).transpose(1, 0, 2)


def _member_full(s, axis):
    n, r, c = s.shape
    if axis == 0:
        return s.reshape(n * r, c)
    return s.transpose(1, 0, 2).reshape(r, n * c)


def _my_weight_groups(w):
    out = []
    for grp in GROUPS:
        rows, width = _group_shape(grp)
        out.append(jnp.concatenate([w[name].astype(BF16) for name, _, _ in grp], axis=0).reshape(2, rows // 2, width))
    return out


def _full_weights(gathered, conv_all):
    out = {}
    for grp, ga in zip(GROUPS, gathered):
        ga = ga.reshape(N_CHIPS, -1, ga.shape[-1])
        off = 0
        for name, shape, axis in grp:
            rows = _shard_shape(shape, axis)[0]
            out[name] = _member_full(ga[:, off:off + rows], axis)
            off += rows
    out[CONV_W[0]] = _member_full(conv_all, CONV_W[2])
    return out


def _grad_groups(G):
    out = []
    for grp in GROUPS:
        rows, width = _group_shape(grp)
        a = jnp.concatenate([_member_shards(G[name], axis) for name, _, axis in grp], axis=1)
        out.append(a.reshape(N_CHIPS, 2, rows // 2, width).transpose(1, 0, 2, 3))
    return out


def _pack_small(vals, conv_g=None, loss=None):
    parts = [vals[name].reshape(-1) for name, _ in REPLICATED]
    if conv_g is not None:
        parts += [conv_g.reshape(-1), loss.reshape(1)]
    flat = jnp.concatenate(parts)
    return jnp.pad(flat, (0, SMALL_ROWS * LANES - flat.shape[0])).reshape(SMALL_ROWS, LANES)


def _unpack_small(packed):
    flat = packed.reshape(-1)
    out, off = {}, 0
    for name, n in REPLICATED:
        out[name] = flat[off:off + n].reshape(1, n)
        off += n
    return out


MESH = pl.DeviceIdType.MESH
HBM_SPEC = pl.BlockSpec(memory_space=pltpu.HBM)


def _place():
    x, y, c = lax.axis_index("x"), lax.axis_index("y"), lax.axis_index("c")
    chips = [(1 - x, y), (x, 1 - y), (1 - x, 1 - y)]
    return x, y, c, chips


def _remote(src, dst, send_sems, recv_sems, k, to):
    return pltpu.make_async_remote_copy(src_ref=src, dst_ref=dst, send_sem=send_sems.at[k], recv_sem=recv_sems.at[k],
                                        device_id=to, device_id_type=MESH)


def _gather_weights(groups, wholes, name, collective_id):
    n, m = len(groups), len(wholes)
    arrays = list(groups) + list(wholes)
    hbm = pltpu.MemorySpace.HBM
    srcs = [jax.new_ref(a, memory_space=hbm) for a in arrays]
    outs = [jax.empty_ref(jax.ShapeDtypeStruct((N_CHIPS,) + a.shape, a.dtype), memory_space=hbm) for a in arrays]
    n_sem = 6 * n + 3 * m

    def launch(send_sems, recv_sems):
        x, y, c, chips = _place()
        me = 2 * x + y
        sibling = (x, y, 1 - c)
        barrier = pltpu.get_barrier_semaphore()
        peers = [(px, py, c) for px, py in chips] + [sibling]
        for peer in peers:
            pl.semaphore_signal(barrier, inc=1, device_id=peer, device_id_type=MESH)
        pl.semaphore_wait(barrier, len(peers))
        first = [_remote(srcs[g].at[c], outs[g].at[me, c], send_sems, recv_sems, g * 3 + k, (px, py, c))
                 for k, (px, py) in enumerate(chips) for g in range(n)]
        first += [_remote(srcs[n + w], outs[n + w].at[me], send_sems, recv_sems, 6 * n + w * 3 + k, (px, py, c))
                  for k, (px, py) in enumerate(chips) for w in range(m)]
        for cp in first:
            cp.start()
        passed = []
        for k, (px, py) in enumerate(chips):
            for g in range(n):
                slot = outs[g].at[2 * px + py, c]
                _remote(slot, slot, send_sems, recv_sems, g * 3 + k, (px, py, c)).wait_recv()
                cp = _remote(slot, slot, send_sems, recv_sems, 3 * n + g * 3 + k, sibling)
                cp.start()
                passed.append(cp)
        for k, (px, py) in enumerate(chips):
            for w in range(m):
                slot = outs[n + w].at[2 * px + py]
                _remote(slot, slot, send_sems, recv_sems, 6 * n + w * 3 + k, (px, py, c)).wait_recv()
            for g in range(n):
                slot = outs[g].at[2 * px + py, 1 - c]
                _remote(slot, slot, send_sems, recv_sems, 3 * n + g * 3 + k, sibling).wait_recv()
        for cp in first + passed:
            cp.wait_send()

    pl.kernel(launch, mesh=plsc.ScalarSubcoreMesh(axis_name="sequencer", num_cores=1), name=name,
              scratch_types=(pltpu.SemaphoreType.DMA((n_sem,)), pltpu.SemaphoreType.DMA((n_sem,))),
              compiler_params=pltpu.CompilerParams(collective_id=collective_id))()
    me = 2 * lax.axis_index("x") + lax.axis_index("y")
    res = [lax.dynamic_update_slice_in_dim(o[...], a[None], me, axis=0) for o, a in zip(outs, arrays)]
    return res[:n], res[n:]


def _swap_halves(groups):
    n = len(groups)

    def body(*refs):
        srcs, outs, send_sems, recv_sems = refs[:n], refs[n:2 * n], refs[2 * n], refs[2 * n + 1]
        x, y, c, _ = _place()
        cps = [_remote(srcs[g].at[1 - c], outs[g], send_sems, recv_sems, g, (x, y, 1 - c)) for g in range(n)]
        for cp in cps:
            cp.start()
        for cp in cps:
            cp.wait()

    return pl.pallas_call(
        body, name="comm_swap_halves", in_specs=[HBM_SPEC] * n, out_specs=[HBM_SPEC] * n,
        out_shape=[jax.ShapeDtypeStruct(g.shape[1:], g.dtype) for g in groups],
        scratch_shapes=[pltpu.SemaphoreType.DMA((n,)), pltpu.SemaphoreType.DMA((n,))],
    )(*groups)


def _scatter_partials(groups):
    n = len(groups)

    def body(*refs):
        srcs, outs = refs[:n], refs[n:2 * n]
        send_sems, recv_sems = refs[2 * n:]
        x, y, c, chips = _place()
        sends = [_remote(srcs[g].at[2 * px + py], outs[g].at[k], send_sems, recv_sems, g * 3 + k, (px, py, c))
                 for k, (px, py) in enumerate(chips) for g in range(n)]
        for cp in sends:
            cp.start()
        for k, (px, py) in enumerate(chips):
            for g in range(n):
                slot = outs[g].at[k]
                _remote(slot, slot, send_sems, recv_sems, g * 3 + k, (px, py, c)).wait_recv()
        for cp in sends:
            cp.wait_send()

    return pl.pallas_call(
        body, name="comm_scatter_partials", in_specs=[HBM_SPEC] * n, out_specs=[HBM_SPEC] * n,
        out_shape=[jax.ShapeDtypeStruct((3,) + g.shape[1:], g.dtype) for g in groups],
        scratch_shapes=[pltpu.SemaphoreType.DMA((3 * n,)), pltpu.SemaphoreType.DMA((3 * n,))],
    )(*groups)


def _share_reduced(groups):
    n = len(groups)

    def body(*refs):
        srcs, outs = refs[:n], refs[n:2 * n]
        send_sems, recv_sems = refs[2 * n:]
        x, y, c, _ = _place()
        cps = [_remote(srcs[g], outs[g], send_sems, recv_sems, g, (x, y, 1 - c)) for g in range(n)]
        for cp in cps:
            cp.start()
        for cp in cps:
            cp.wait()

    return pl.pallas_call(
        body, name="comm_share_reduced", in_specs=[HBM_SPEC] * n, out_specs=[HBM_SPEC] * n,
        out_shape=[jax.ShapeDtypeStruct(g.shape, g.dtype) for g in groups],
        scratch_shapes=[pltpu.SemaphoreType.DMA((n,)), pltpu.SemaphoreType.DMA((n,))],
    )(*groups)


def _allreduce_small(v):
    n_dev = 2 * N_CHIPS

    def body(src, out, buf, send_sems, recv_sems):
        x, y, c, _ = _place()
        me = 4 * x + 2 * y + c
        buf[me] = src[...]
        flips = [(k >> 2 & 1, k >> 1 & 1, k & 1) for k in range(1, n_dev)]
        sends = []
        for k, (fx, fy, fc) in enumerate(flips):
            to = ((1 - x) if fx else x, (1 - y) if fy else y, (1 - c) if fc else c)
            cp = _remote(src, buf.at[me], send_sems, recv_sems, k, to)
            cp.start()
            sends.append((cp, to))
        for k, (cp, to) in enumerate(sends):
            slot = buf.at[4 * to[0] + 2 * to[1] + to[2]]
            _remote(slot, slot, send_sems, recv_sems, k, to).wait_recv()
        for cp, _ in sends:
            cp.wait_send()
        acc = buf[0]
        for d in range(1, n_dev):
            acc = acc + buf[d]
        out[...] = acc

    vm = pl.BlockSpec(memory_space=pltpu.VMEM)
    return pl.pallas_call(
        body, name="comm_allreduce_small", in_specs=[vm], out_specs=vm, out_shape=jax.ShapeDtypeStruct(v.shape, F32),
        scratch_shapes=[pltpu.VMEM((n_dev,) + v.shape, F32), pltpu.SemaphoreType.DMA((n_dev - 1,)),
                        pltpu.SemaphoreType.DMA((n_dev - 1,))],
    )(v)


def _row_tile(rows, cap=320):
    return max(t for t in range(16, cap + 1, 16) if rows % t == 0)


def _add_sibling(mine, theirs, core, name):
    _, n, R, C = mine.shape
    t = _row_tile(R)

    def body(core_ref, a_ref, b_ref, o_ref, ob_ref):
        tot = a_ref[...] + b_ref[...]
        o_ref[...] = tot
        ob_ref[...] = tot.astype(BF16)

    sp = pl.BlockSpec((None, t, C), lambda k, i, core_ref: (k, i, 0))
    grid_spec = pltpu.PrefetchScalarGridSpec(
        num_scalar_prefetch=1, grid=(n, R // t),
        in_specs=[pl.BlockSpec((None, None, t, C), lambda k, i, core_ref: (core_ref[0], k, i, 0)), sp], out_specs=[sp, sp])
    return pl.pallas_call(body, name=name, grid_spec=grid_spec,
                          out_shape=[jax.ShapeDtypeStruct((n, R, C), F32), jax.ShapeDtypeStruct((n, R, C), BF16)],
                          compiler_params=_params("parallel", "parallel"))(core, mine, theirs)


def _add_chips(received, own, chip, name):
    n, R, C = received.shape
    t = _row_tile(R)

    def body(chip_ref, r_ref, o_ref, out_ref):
        acc = o_ref[...]
        for k in range(n):
            acc = acc + r_ref[k].astype(F32)
        out_ref[...] = acc

    grid_spec = pltpu.PrefetchScalarGridSpec(
        num_scalar_prefetch=1, grid=(R // t,),
        in_specs=[pl.BlockSpec((n, t, C), lambda i, chip_ref: (0, i, 0)), pl.BlockSpec((None, t, C), lambda i, chip_ref: (chip_ref[0], i, 0))],
        out_specs=pl.BlockSpec((t, C), lambda i, chip_ref: (i, 0)))
    return pl.pallas_call(body, name=name, grid_spec=grid_spec, out_shape=jax.ShapeDtypeStruct((R, C), F32),
                          compiler_params=_params("parallel"))(chip, received, own)


def _adamw(w, g, m, v, name, g_row0=0):
    R, C = w.shape
    t = math.gcd(math.gcd(R, g_row0), 128) if R % 8 == 0 else R
    assert g_row0 % t == 0
    c1 = 1.0 - ADAM_B1 ** ADAM_STEP
    c2 = 1.0 - ADAM_B2 ** ADAM_STEP

    def body(w_ref, g_ref, m_ref, v_ref, go_ref, d_ref, nm_ref, nv_ref):
        gv = g_ref[...]
        nm = ADAM_B1 * m_ref[...] + (1.0 - ADAM_B1) * gv
        nv = ADAM_B2 * v_ref[...] + (1.0 - ADAM_B2) * (gv * gv)
        go_ref[...] = gv
        d_ref[...] = -ADAM_LR * ((nm / c1) / (jnp.sqrt(nv / c2) + ADAM_EPS) + ADAM_WD * w_ref[...])
        nm_ref[...] = nm
        nv_ref[...] = nv

    sp = pl.BlockSpec((t, C), lambda i: (i, 0))
    g_sp = pl.BlockSpec((t, C), lambda i: (i + g_row0 // t, 0))
    return pl.pallas_call(body, name=name, grid=(R // t,), in_specs=[sp, g_sp, sp, sp], out_specs=[sp] * 4,
                          out_shape=[jax.ShapeDtypeStruct((R, C), F32)] * 4, compiler_params=_params("parallel"))(w, g, m, v)


def kernel(x, mem, positions, g_pre_mix, w_in, b_gate, mla_q_norm, w_uq, mla_kv_norm, w_ukv, g_mem, w_mem_kv, w_br_mla, w_br_dil, w_br_mem, w_o, g_post_mix, g_pre_ffn, w_ffn_up, conv_w, conv_b, w_ffn_down, g_post_ffn, loss_target, m_g_pre_mix, m_w_in, m_b_gate, m_mla_q_norm, m_w_uq, m_mla_kv_norm, m_w_ukv, m_g_mem, m_w_mem_kv, m_w_br_mla, m_w_br_dil, m_w_br_mem, m_w_o, m_g_post_mix, m_g_pre_ffn, m_w_ffn_up, m_conv_w, m_conv_b, m_w_ffn_down, m_g_post_ffn, v_g_pre_mix, v_w_in, v_b_gate, v_mla_q_norm, v_w_uq, v_mla_kv_norm, v_w_ukv, v_g_mem, v_w_mem_kv, v_w_br_mla, v_w_br_dil, v_w_br_mem, v_w_o, v_g_post_mix, v_g_pre_ffn, v_w_ffn_up, v_conv_w, v_conv_b, v_w_ffn_down, v_g_post_ffn):
    w_args = (g_pre_mix, w_in, b_gate, mla_q_norm, w_uq, mla_kv_norm, w_ukv, g_mem, w_mem_kv, w_br_mla, w_br_dil, w_br_mem, w_o,
              g_post_mix, g_pre_ffn, w_ffn_up, conv_w, conv_b, w_ffn_down, g_post_ffn)
    m_args = (m_g_pre_mix, m_w_in, m_b_gate, m_mla_q_norm, m_w_uq, m_mla_kv_norm, m_w_ukv, m_g_mem, m_w_mem_kv, m_w_br_mla,
              m_w_br_dil, m_w_br_mem, m_w_o, m_g_post_mix, m_g_pre_ffn, m_w_ffn_up, m_conv_w, m_conv_b, m_w_ffn_down, m_g_post_ffn)
    v_args = (v_g_pre_mix, v_w_in, v_b_gate, v_mla_q_norm, v_w_uq, v_mla_kv_norm, v_w_ukv, v_g_mem, v_w_mem_kv, v_w_br_mla,
              v_w_br_dil, v_w_br_mem, v_w_o, v_g_post_mix, v_g_pre_ffn, v_w_ffn_up, v_conv_w, v_conv_b, v_w_ffn_down, v_g_post_ffn)
    sharded = {name for grp in GROUPS for name, _, _ in grp} | {CONV_W[0]}

    def local(a, name):
        return a[0] if name in sharded else a

    w = {n: local(a, n) for n, a in zip(WEIGHTS, w_args)}
    m = {n: local(a, n) for n, a in zip(WEIGHTS, m_args)}
    v = {n: local(a, n) for n, a in zip(WEIGHTS, v_args)}

    mine = _my_weight_groups(w)
    first, _ = _gather_weights(mine[:1], [], "comm_gather_w_in", 0)
    rest, (conv_all,) = _gather_weights(mine[1:], [w[CONV_W[0]]], "comm_gather_rest", 1)
    full = _full_weights(first + rest, conv_all)
    full.update({name: w[name] for name, _ in REPLICATED})

    loss_local, grad_x, G = _local_step(x[0], mem[0], positions, loss_target[0], full)

    core = lax.axis_index("c").astype(jnp.int32).reshape(1)
    mine = _grad_groups(G)
    theirs = _swap_halves(mine)
    chip = (2 * lax.axis_index("x") + lax.axis_index("y")).astype(jnp.int32).reshape(1)
    partial = [_add_sibling(a, b, core, f"add_sibling_{i}") for i, (a, b) in enumerate(zip(mine, theirs))]
    received = _scatter_partials([p16 for _, p16 in partial])
    reduced = [_add_chips(r, p32, chip, f"add_chips_{i}") for i, (r, (p32, _)) in enumerate(zip(received, partial))]
    first = lax.axis_index("c") == 0
    shard_groups = [jnp.concatenate([jnp.where(first, a, b), jnp.where(first, b, a)], axis=0)
                    for a, b in zip(reduced, _share_reduced(reduced))]
    small = _allreduce_small(_pack_small(G, G[CONV_W[0]], loss_local))
    flat = small.reshape(-1)
    loss = flat[LOSS_AT]
    conv_g = flat[CONV_AT:LOSS_AT].reshape(CONV_W[1])
    conv_cols = CONV_W[1][1] // N_CHIPS
    conv_g = lax.dynamic_slice_in_dim(conv_g, (2 * lax.axis_index("x") + lax.axis_index("y")) * conv_cols, conv_cols, axis=1)

    grads, deltas, new_m, new_v = {}, {}, {}, {}
    for grp, g_all in zip(GROUPS, shard_groups):
        off = 0
        for name, _, _ in grp:
            grads[name], deltas[name], new_m[name], new_v[name] = _adamw(w[name], g_all, m[name], v[name], "adamw_" + name, off)
            off += w[name].shape[0]
    name = CONV_W[0]
    grads[name], deltas[name], new_m[name], new_v[name] = _adamw(w[name], conv_g, m[name], v[name], "adamw_" + name)
    packed = _adamw(_pack_small(w), small, _pack_small(m), _pack_small(v), "adamw_small")
    for dst, packed_small in zip((grads, deltas, new_m, new_v), packed):
        dst.update(_unpack_small(packed_small))

    def out(d, name):
        return d[name][None] if name in sharded else d[name]

    return (loss, grad_x[None], *[out(grads, n) for n in WEIGHTS], *[out(deltas, n) for n in WEIGHTS],
            *[out(new_m, n) for n in WEIGHTS], *[out(new_v, n) for n in WEIGHTS])
```

```python
import functools
import math

import jax
import jax.numpy as jnp
from jax import lax
from jax.experimental import pallas as pl
from jax.experimental.pallas import tpu as pltpu
from jax.experimental.pallas import tpu_sc as plsc

F32 = jnp.float32
BF16 = jnp.bfloat16

D_MODEL = 1024
N_MEM = 256
RMS_EPS = 1e-6
NEG_INF = -1e30
MLA_HEADS = 8
MLA_NOPE = 64
MLA_ROPE = 32
MLA_QK = 96
Q_RANK = 384
KV_RANK = 256
ROPE_THETA = 10000.0
DIL_PAIRS = ((128, 1), (512, 4), (2048, 16))
DIL_GROUPS = 3
DIL_HPG = 4
DIL_HEADS = 12
DIL_W = 512
MEM_HEADS = 4
MEM_W = 512
D_FF = 2816
OFF_Q = 384
OFF_KV = 640
OFF_KR = 672
OFF_DIL = 5280
OFF_MEMQ = 5792
D_IN = 8864
ADAM_LR = 0.001
ADAM_B1 = 0.9
ADAM_B2 = 0.999
ADAM_EPS = 1e-08
ADAM_WD = 0.01
ADAM_STEP = 10

LANES = 128
VMEM_LIMIT = 56 * 1024 * 1024

N_CHIPS = 4
ROW_TILE = 256

NN = (((1,), (0,)), ((), ()))
NT = (((1,), (1,)), ((), ()))
TN = (((0,), (0,)), ((), ()))


def _params(*sem):
    return pltpu.CompilerParams(dimension_semantics=sem, vmem_limit_bytes=VMEM_LIMIT)


def _full(shape):
    return pl.BlockSpec(shape, lambda *_: (0,) * len(shape))


def _matmul(a, b, *, mode="nn", out_dtype=F32, tm=1024, tn=1024, tk=None, add=None, scale=None, name):
    if mode == "nn":
        (M, K), N = a.shape, b.shape[1]
    elif mode == "nt":
        (M, K), N = a.shape, b.shape[0]
    else:
        (K, M), N = a.shape, b.shape[1]
    tm, tn = min(tm, M), min(tn, N)
    tk = K if tk is None else min(tk, K)
    assert M % tm == 0 and N % tn == 0 and K % tk == 0, (name, M, N, K, tm, tn, tk)
    a_spec = pl.BlockSpec((tk, tm), lambda i, j, k: (k, i)) if mode == "tn" else pl.BlockSpec((tm, tk), lambda i, j, k: (i, k))
    b_spec = pl.BlockSpec((tn, tk), lambda i, j, k: (j, k)) if mode == "nt" else pl.BlockSpec((tk, tn), lambda i, j, k: (k, j))
    o_spec = pl.BlockSpec((tm, tn), lambda i, j, k: (i, j))
    return _matmul_blocks(a, b, mode=mode, grid=(M // tm, N // tn, K // tk), a_spec=a_spec, b_spec=b_spec, o_spec=o_spec,
                          out_shape=(M, N), out_dtype=out_dtype, add=add,
                          scale=None if scale is None else (lambda j: j < scale[0], scale[1]), name=name)


def _matmul_blocks(a, b, *, mode, grid, a_spec, b_spec, o_spec, out_shape, out_dtype=F32, add=None, scale=None, name):
    nk = grid[2]
    dims = {"nn": NN, "nt": NT, "tn": TN}[mode]
    tm, tn = o_spec.block_shape
    has_add = add is not None

    def body(*refs):
        a_ref, b_ref = refs[0], refs[1]
        c_ref = refs[2] if has_add else None
        o_ref = refs[3] if has_add else refs[2]
        part = lax.dot_general(a_ref[...].astype(BF16), b_ref[...].astype(BF16), dims, preferred_element_type=F32)
        if scale is not None:
            assert nk == 1 and not has_add
            part = part * jnp.where(scale[0](pl.program_id(1)), scale[1], 1.0)
        if nk == 1:
            if has_add:
                part = part + c_ref[...]
            o_ref[...] = part.astype(out_dtype)
        else:
            acc = refs[-1]
            k = pl.program_id(2)

            @pl.when(k == 0)
            def _():
                acc[...] = part

            @pl.when(k > 0)
            def _():
                acc[...] += part

            @pl.when(k == nk - 1)
            def _():
                r = acc[...]
                if has_add:
                    r = r + c_ref[...]
                o_ref[...] = r.astype(out_dtype)

    in_specs = [a_spec, b_spec] + ([o_spec] if has_add else [])
    args = (a, b) + ((add,) if has_add else ())
    return pl.pallas_call(
        body, name=name, grid=grid, in_specs=in_specs, out_specs=o_spec,
        out_shape=jax.ShapeDtypeStruct(out_shape, out_dtype),
        scratch_shapes=[pltpu.VMEM((tm, tn), F32)] if nk > 1 else [],
        compiler_params=_params("parallel", "parallel", "arbitrary"),
    )(*args)


def _proj_view(h_view, w, d, q_scale, name):
    L, K = h_view.shape[0], h_view.shape[1] // d
    N = w.shape[1]
    tn = N // 3
    tm = min(1024, L)
    return _matmul_blocks(
        h_view, w, mode="nn", grid=(L // tm, 3 * d, 1), a_spec=pl.BlockSpec((tm, K), lambda i, j, k: (i, j // 3)),
        b_spec=pl.BlockSpec((K, tn), lambda i, j, k: (0, j % 3)), o_spec=pl.BlockSpec((tm, tn), lambda i, j, k: (i, j)),
        out_shape=(L, d * N), out_dtype=BF16, scale=(lambda j: j % 3 == 0, q_scale), name=name)


def _dh_view(dp_view, w, d, add, name):
    L, N = dp_view.shape[0], dp_view.shape[1] // d
    K = w.shape[0]
    tm = min(1024, L)
    return _matmul_blocks(
        dp_view, w, mode="nt", grid=(L // tm, d, 1), a_spec=pl.BlockSpec((tm, N), lambda i, j, k: (i, j)),
        b_spec=pl.BlockSpec((K, N), lambda i, j, k: (0, 0)), o_spec=pl.BlockSpec((tm, K), lambda i, j, k: (i, j)),
        out_shape=(L, d * K), add=add, name=name)


def _dw_view(h_view, dp_view, d, name):
    L, K = h_view.shape[0], h_view.shape[1] // d
    N = dp_view.shape[1] // d
    tk = min(1024, L)
    nl = L // tk
    return _matmul_blocks(
        h_view, dp_view, mode="tn", grid=(1, 1, d * nl), a_spec=pl.BlockSpec((tk, K), lambda i, j, k: (k % nl, k // nl)),
        b_spec=pl.BlockSpec((tk, N), lambda i, j, k: (k % nl, k // nl)), o_spec=pl.BlockSpec((K, N), lambda i, j, k: (0, 0)),
        out_shape=(K, N), name=name)


def _rms_fwd_val(x, g):
    r = lax.rsqrt(jnp.mean(x * x, axis=-1, keepdims=True) + RMS_EPS)
    return (x * r) * g


def _rms_bwd_val(dy, x, g):
    r = lax.rsqrt(jnp.mean(x * x, axis=-1, keepdims=True) + RMS_EPS)
    xn = x * r
    gdy = g * dy
    dx = r * (gdy - xn * jnp.mean(gdy * xn, axis=-1, keepdims=True))
    return dx, dy * xn


def _rope_tables(pos, invf, inverse):
    ang = pos * invf
    cos, sin = jnp.cos(ang), jnp.sin(ang)
    lane = lax.broadcasted_iota(jnp.int32, ang.shape, 1)
    first = (lane >= MLA_NOPE) & (lane < MLA_NOPE + MLA_ROPE // 2)
    second = (lane >= MLA_NOPE + MLA_ROPE // 2) & (lane < MLA_QK)
    sgn = -1.0 if inverse else 1.0
    sa = jnp.where(first, -sgn * sin, 0.0)
    sb = jnp.where(second, sgn * sin, 0.0)
    return cos, sa, sb


def _rope_val(x, cos, sa, sb):
    half = MLA_ROPE // 2
    return x * cos + pltpu.roll(x, LANES - half, 1) * sa + pltpu.roll(x, half, 1) * sb


def _head_sum_bcast(v, n_heads):
    parts = []
    for h in range(n_heads):
        s = jnp.sum(v[:, h * LANES:(h + 1) * LANES], axis=1, keepdims=True)
        parts.append(jnp.broadcast_to(s, (v.shape[0], LANES)))
    return parts


def _row_spec(t, w):
    return pl.BlockSpec((t, w), lambda i: (i, 0))


def _acc_spec(w, rows=1):
    return pl.BlockSpec((rows, w), lambda i: (0, 0))


def _rmsnorm(x, g, out_dtype, name):
    S, W = x.shape
    t = min(ROW_TILE, S)

    def body(x_ref, g_ref, o_ref):
        o_ref[...] = _rms_fwd_val(x_ref[...], g_ref[...]).astype(out_dtype)

    return pl.pallas_call(body, name=name, grid=(S // t,), in_specs=[_row_spec(t, W), _acc_spec(W)],
                          out_specs=_row_spec(t, W), out_shape=jax.ShapeDtypeStruct((S, W), out_dtype),
                          compiler_params=_params("parallel"))(x, g)


def _mla_prep(proj_a, pos, invf, q_norm, kv_norm):
    S = proj_a.shape[0]
    t = ROW_TILE

    def body(a_ref, pos_ref, invf_ref, qn_ref, kvn_ref, cq_ref, ckv_ref, kpe_ref):
        a = a_ref[...]
        cq_ref[...] = _rms_fwd_val(a[:, 0:Q_RANK], qn_ref[...]).astype(BF16)
        ckv_ref[...] = _rms_fwd_val(a[:, Q_RANK + LANES:], kvn_ref[...]).astype(BF16)
        cos, sa, sb = _rope_tables(pos_ref[...], invf_ref[...], False)
        kpe_ref[...] = _rope_val(a[:, Q_RANK:Q_RANK + LANES], cos, sa, sb)

    return pl.pallas_call(
        body, name="mla_prep", grid=(S // t,),
        in_specs=[_row_spec(t, 768), _row_spec(t, 1), _acc_spec(LANES), _acc_spec(Q_RANK), _acc_spec(KV_RANK)],
        out_specs=[_row_spec(t, Q_RANK), _row_spec(t, KV_RANK), _row_spec(t, LANES)],
        out_shape=[jax.ShapeDtypeStruct((S, Q_RANK), BF16), jax.ShapeDtypeStruct((S, KV_RANK), BF16),
                   jax.ShapeDtypeStruct((S, LANES), F32)],
        compiler_params=_params("parallel"))(proj_a, pos, invf, q_norm, kv_norm)


def _qk_final(q_pre, k_pre, kpe, pos, invf):
    S, W = q_pre.shape
    t = ROW_TILE
    scale = MLA_QK ** -0.5

    def body(q_ref, k_ref, kpe_ref, pos_ref, invf_ref, qo_ref, ko_ref):
        cos, sa, sb = _rope_tables(pos_ref[...], invf_ref[...], False)
        kpe_v = kpe_ref[...]
        for h in range(MLA_HEADS):
            sl = slice(h * LANES, (h + 1) * LANES)
            qo_ref[:, sl] = (_rope_val(q_ref[:, sl], cos, sa, sb) * scale).astype(BF16)
            ko_ref[:, sl] = (k_ref[:, sl] + kpe_v).astype(BF16)

    return pl.pallas_call(
        body, name="qk_final", grid=(S // t,),
        in_specs=[_row_spec(t, W), _row_spec(t, W), _row_spec(t, LANES), _row_spec(t, 1), _acc_spec(LANES)],
        out_specs=[_row_spec(t, W), _row_spec(t, W)],
        out_shape=[jax.ShapeDtypeStruct((S, W), BF16)] * 2,
        compiler_params=_params("parallel"))(q_pre, k_pre, kpe, pos, invf)


def _mla_bwd_prep(dq, dk, pos, invf):
    S, W = dq.shape
    t = ROW_TILE
    scale = MLA_QK ** -0.5

    def body(dq_ref, dk_ref, pos_ref, invf_ref, dqp_ref, dkpe_ref):
        cos, sa, sb = _rope_tables(pos_ref[...], invf_ref[...], True)
        tot = jnp.zeros((t, LANES), F32)
        for h in range(MLA_HEADS):
            sl = slice(h * LANES, (h + 1) * LANES)
            dqp_ref[:, sl] = _rope_val(dq_ref[:, sl] * scale, cos, sa, sb).astype(BF16)
            tot = tot + dk_ref[:, sl]
        lane = lax.broadcasted_iota(jnp.int32, tot.shape, 1)
        tot = jnp.where((lane >= MLA_NOPE) & (lane < MLA_QK), tot, 0.0)
        dkpe_ref[...] = _rope_val(tot, cos, sa, sb)

    return pl.pallas_call(
        body, name="mla_bwd_prep", grid=(S // t,),
        in_specs=[_row_spec(t, W), _row_spec(t, W), _row_spec(t, 1), _acc_spec(LANES)],
        out_specs=[_row_spec(t, W), _row_spec(t, LANES)],
        out_shape=[jax.ShapeDtypeStruct((S, W), BF16), jax.ShapeDtypeStruct((S, LANES), F32)],
        compiler_params=_params("parallel"))(dq, dk, pos, invf)


def _mla_norm_bwd(dcq, dckv_a, dckv_b, dkpe, proj_a, q_norm, kv_norm):
    S = proj_a.shape[0]
    t = ROW_TILE

    def body(dcq_ref, da_ref, db_ref, dkpe_ref, a_ref, qn_ref, kvn_ref, o_ref, dqn_ref, dkvn_ref):
        i = pl.program_id(0)
        a = a_ref[...]
        dxq, gq = _rms_bwd_val(dcq_ref[...], a[:, 0:Q_RANK], qn_ref[...])
        dxkv, gkv = _rms_bwd_val(da_ref[...] + db_ref[...], a[:, Q_RANK + LANES:], kvn_ref[...])
        o_ref[:, 0:Q_RANK] = dxq.astype(BF16)
        o_ref[:, Q_RANK:Q_RANK + LANES] = dkpe_ref[...].astype(BF16)
        o_ref[:, Q_RANK + LANES:] = dxkv.astype(BF16)

        @pl.when(i == 0)
        def _():
            dqn_ref[...] = jnp.zeros_like(dqn_ref)
            dkvn_ref[...] = jnp.zeros_like(dkvn_ref)

        dqn_ref[...] += jnp.sum(gq, axis=0, keepdims=True)
        dkvn_ref[...] += jnp.sum(gkv, axis=0, keepdims=True)

    return pl.pallas_call(
        body, name="mla_norm_bwd", grid=(S // t,),
        in_specs=[_row_spec(t, Q_RANK), _row_spec(t, KV_RANK), _row_spec(t, KV_RANK), _row_spec(t, LANES),
                  _row_spec(t, 768), _acc_spec(Q_RANK), _acc_spec(KV_RANK)],
        out_specs=[_row_spec(t, 768), _acc_spec(Q_RANK), _acc_spec(KV_RANK)],
        out_shape=[jax.ShapeDtypeStruct((S, 768), BF16), jax.ShapeDtypeStruct((1, Q_RANK), F32),
                   jax.ShapeDtypeStruct((1, KV_RANK), F32)],
        compiler_params=_params("arbitrary"))(dcq, dckv_a, dckv_b, dkpe, proj_a, q_norm, kv_norm)


DILATIONS = tuple(d for _, d in DIL_PAIRS)


def _to_tokens(view, scr, d):
    if d == 1:
        return view
    n, w = view.shape[0], view.shape[1] // d
    for r in range(d):
        for c in range(w // LANES):
            scr[pl.ds(c, 1), pl.ds(r, n, stride=d), :] = view[:, r * w + c * LANES:r * w + (c + 1) * LANES][None]
    return jnp.concatenate([scr[c] for c in range(w // LANES)], axis=1)


def _from_tokens(tok, scr, d, out_ref):
    if d == 1:
        out_ref[...] = tok.astype(out_ref.dtype)
        return
    n, w = tok.shape[0] // d, tok.shape[1]
    for c in range(w // LANES):
        scr[c] = tok[:, c * LANES:(c + 1) * LANES]
    for r in range(d):
        for c in range(w // LANES):
            out_ref[:, r * w + c * LANES:r * w + (c + 1) * LANES] = scr[pl.ds(c, 1), pl.ds(r, n, stride=d), :][0].astype(out_ref.dtype)


def _token_scratch(t, w):
    return pltpu.VMEM((w // LANES, t, LANES), F32)


def _view_spec(t, d):
    return pl.BlockSpec((t // d, d * DIL_W), lambda i: (i, 0))


def _mix_weights(ls):
    m = jnp.maximum(jnp.maximum(ls[0], ls[1]), ls[2])
    es = [jnp.exp(l - m) for l in ls]
    den = es[0] + es[1] + es[2]
    return [e / den for e in es]


def _dil_mix(o_list, lse_list):
    S = o_list[0].shape[0] * DILATIONS[0]
    t = ROW_TILE
    specs = [_view_spec(t, d) for d in DILATIONS]

    def body(o0, o1, o2, l0, l1, l2, y_ref, *scr):
        os_ = [_to_tokens(r[...], scr[g], d) for g, (r, d) in enumerate(zip((o0, o1, o2), DILATIONS))]
        ws = _mix_weights([_to_tokens(r[...], scr[3 + g], d) for g, (r, d) in enumerate(zip((l0, l1, l2), DILATIONS))])
        y_ref[...] = (ws[0] * os_[0] + ws[1] * os_[1] + ws[2] * os_[2]).astype(BF16)

    return pl.pallas_call(
        body, name="dil_mix", grid=(S // t,), in_specs=specs * 2, out_specs=_row_spec(t, DIL_W),
        out_shape=jax.ShapeDtypeStruct((S, DIL_W), BF16), scratch_shapes=[_token_scratch(t, DIL_W)] * 6,
        compiler_params=_params("parallel"))(*o_list, *lse_list)


def _dil_mix_bwd(dy, o_list, lse_list):
    S = dy.shape[0]
    t = ROW_TILE
    specs = [_view_spec(t, d) for d in DILATIONS]

    def body(dy_ref, o0, o1, o2, l0, l1, l2, d0, d1, d2, e0, e1, e2, *scr):
        os_ = [_to_tokens(r[...], scr[g], d) for g, (r, d) in enumerate(zip((o0, o1, o2), DILATIONS))]
        ws = _mix_weights([_to_tokens(r[...], scr[3 + g], d) for g, (r, d) in enumerate(zip((l0, l1, l2), DILATIONS))])
        dyv = dy_ref[...]
        y = ws[0] * os_[0] + ws[1] * os_[1] + ws[2] * os_[2]
        b = jnp.concatenate(_head_sum_bcast(dyv * y, DIL_HPG), axis=1)
        for g, (w, d_ref, e_ref, d) in enumerate(zip(ws, (d0, d1, d2), (e0, e1, e2), DILATIONS)):
            _from_tokens(w * dyv, scr[6 + g], d, d_ref)
            _from_tokens(w * b, scr[9 + g], d, e_ref)

    shapes = [(S // d, d * DIL_W) for d in DILATIONS]
    return pl.pallas_call(
        body, name="dil_mix_bwd", grid=(S // t,), in_specs=[_row_spec(t, DIL_W)] + specs * 2, out_specs=specs * 2,
        out_shape=[jax.ShapeDtypeStruct(s, BF16) for s in shapes] + [jax.ShapeDtypeStruct(s, F32) for s in shapes],
        scratch_shapes=[_token_scratch(t, DIL_W)] * 12, compiler_params=_params("parallel"))(dy, *o_list, *lse_list)


def _delta(do, o, n_heads, name):
    S, W = do.shape
    t = ROW_TILE

    def body(do_ref, o_ref, d_ref):
        prod = do_ref[...].astype(F32) * o_ref[...].astype(F32)
        d_ref[...] = jnp.concatenate(_head_sum_bcast(prod, n_heads), axis=1)

    return pl.pallas_call(body, name=name, grid=(S // t,), in_specs=[_row_spec(t, W)] * 2, out_specs=_row_spec(t, W),
                          out_shape=jax.ShapeDtypeStruct((S, W), F32), compiler_params=_params("parallel"))(do, o)


def _merge(proj_g, b_gate, b_list):
    S = proj_g.shape[0]
    t = ROW_TILE

    def body(g_ref, b_ref, y0, y1, y2, o_ref):
        acc = jnp.zeros((t, D_MODEL), F32)
        for i, y in enumerate((y0, y1, y2)):
            sl = slice(i * D_MODEL, (i + 1) * D_MODEL)
            acc = acc + jax.nn.sigmoid(g_ref[:, sl] + b_ref[:, sl]) * y[...]
        o_ref[...] = acc.astype(BF16)

    return pl.pallas_call(
        body, name="merge", grid=(S // t,),
        in_specs=[_row_spec(t, 3 * D_MODEL), _acc_spec(3 * D_MODEL)] + [_row_spec(t, D_MODEL)] * 3,
        out_specs=_row_spec(t, D_MODEL), out_shape=jax.ShapeDtypeStruct((S, D_MODEL), BF16),
        compiler_params=_params("parallel"))(proj_g, b_gate, *b_list)


def _merge_bwd(dmerged, proj_g, b_gate, b_list):
    S = proj_g.shape[0]
    t = ROW_TILE

    def body(dm_ref, g_ref, b_ref, y0, y1, y2, d0, d1, d2, dz_ref, db_ref):
        i = pl.program_id(0)

        @pl.when(i == 0)
        def _():
            db_ref[...] = jnp.zeros_like(db_ref)

        dm = dm_ref[...]
        for k, (y, d_ref) in enumerate(zip((y0, y1, y2), (d0, d1, d2))):
            sl = slice(k * D_MODEL, (k + 1) * D_MODEL)
            s = jax.nn.sigmoid(g_ref[:, sl] + b_ref[:, sl])
            d_ref[...] = (s * dm).astype(BF16)
            dz = dm * y[...] * (s * (1.0 - s))
            dz_ref[:, sl] = dz.astype(BF16)
            db_ref[:, sl] += jnp.sum(dz, axis=0, keepdims=True)

    return pl.pallas_call(
        body, name="merge_bwd", grid=(S // t,),
        in_specs=[_row_spec(t, D_MODEL), _row_spec(t, 3 * D_MODEL), _acc_spec(3 * D_MODEL)] + [_row_spec(t, D_MODEL)] * 3,
        out_specs=[_row_spec(t, D_MODEL)] * 3 + [_row_spec(t, 3 * D_MODEL), _acc_spec(3 * D_MODEL)],
        out_shape=[jax.ShapeDtypeStruct((S, D_MODEL), BF16)] * 3
        + [jax.ShapeDtypeStruct((S, 3 * D_MODEL), BF16), jax.ShapeDtypeStruct((1, 3 * D_MODEL), F32)],
        compiler_params=_params("arbitrary"))(dmerged, proj_g, b_gate, *b_list)


def _norm2(o, x, g_post, g_pre):
    S = x.shape[0]
    t = ROW_TILE

    def body(o_ref, x_ref, gp_ref, gf_ref, x1_ref, h2_ref):
        x1 = x_ref[...] + _rms_fwd_val(o_ref[...], gp_ref[...])
        x1_ref[...] = x1
        h2_ref[...] = _rms_fwd_val(x1, gf_ref[...]).astype(BF16)

    return pl.pallas_call(
        body, name="norm2", grid=(S // t,),
        in_specs=[_row_spec(t, D_MODEL)] * 2 + [_acc_spec(D_MODEL)] * 2, out_specs=[_row_spec(t, D_MODEL)] * 2,
        out_shape=[jax.ShapeDtypeStruct((S, D_MODEL), F32), jax.ShapeDtypeStruct((S, D_MODEL), BF16)],
        compiler_params=_params("parallel"))(o, x, g_post, g_pre)


def _norm2_bwd(dx2, dh2, x1, o, g_pre, g_post):
    S = x1.shape[0]
    t = ROW_TILE

    def body(dx2_ref, dh2_ref, x1_ref, o_ref, gf_ref, gp_ref, dx1_ref, do_ref, dgf_ref, dgp_ref):
        i = pl.program_id(0)

        @pl.when(i == 0)
        def _():
            dgf_ref[...] = jnp.zeros_like(dgf_ref)
            dgp_ref[...] = jnp.zeros_like(dgp_ref)

        d1, gf = _rms_bwd_val(dh2_ref[...], x1_ref[...], gf_ref[...])
        dx1 = dx2_ref[...] + d1
        dx1_ref[...] = dx1
        do, gp = _rms_bwd_val(dx1, o_ref[...], gp_ref[...])
        do_ref[...] = do.astype(BF16)
        dgf_ref[...] += jnp.sum(gf, axis=0, keepdims=True)
        dgp_ref[...] += jnp.sum(gp, axis=0, keepdims=True)

    return pl.pallas_call(
        body, name="norm2_bwd", grid=(S // t,),
        in_specs=[_row_spec(t, D_MODEL)] * 4 + [_acc_spec(D_MODEL)] * 2,
        out_specs=[_row_spec(t, D_MODEL)] * 2 + [_acc_spec(D_MODEL)] * 2,
        out_shape=[jax.ShapeDtypeStruct((S, D_MODEL), F32), jax.ShapeDtypeStruct((S, D_MODEL), BF16),
                   jax.ShapeDtypeStruct((1, D_MODEL), F32), jax.ShapeDtypeStruct((1, D_MODEL), F32)],
        compiler_params=_params("arbitrary"))(dx2, dh2, x1, o, g_pre, g_post)


def _norm1_bwd(dx1, dh_list, x, g):
    S = x.shape[0]
    t = ROW_TILE
    dils = (1,) + DILATIONS
    assert len(dh_list) == len(dils)

    def body(dx1_ref, *refs):
        dh_refs, (x_ref, g_ref, dx_ref, dg_ref), scr = refs[:len(dils)], refs[len(dils):len(dils) + 4], refs[len(dils) + 4:]
        i = pl.program_id(0)

        @pl.when(i == 0)
        def _():
            dg_ref[...] = jnp.zeros_like(dg_ref)

        dh = dh_refs[0][...]
        for k in range(1, len(dils)):
            dh = dh + _to_tokens(dh_refs[k][...], scr[k - 1], dils[k])
        d, gg = _rms_bwd_val(dh, x_ref[...], g_ref[...])
        dx_ref[...] = dx1_ref[...] + d
        dg_ref[...] += jnp.sum(gg, axis=0, keepdims=True)

    dh_specs = [pl.BlockSpec((t // d, d * D_MODEL), lambda i: (i, 0)) for d in dils]
    return pl.pallas_call(
        body, name="norm1_bwd", grid=(S // t,),
        in_specs=[_row_spec(t, D_MODEL)] + dh_specs + [_row_spec(t, D_MODEL), _acc_spec(D_MODEL)],
        out_specs=[_row_spec(t, D_MODEL), _acc_spec(D_MODEL)],
        out_shape=[jax.ShapeDtypeStruct((S, D_MODEL), F32), jax.ShapeDtypeStruct((1, D_MODEL), F32)],
        scratch_shapes=[_token_scratch(t, D_MODEL)] * (len(dils) - 1),
        compiler_params=_params("arbitrary"))(dx1, *dh_list, x, g)


def _gain_grad(dy, x, name):
    R, W = x.shape

    def body(dy_ref, x_ref, dg_ref):
        xv = x_ref[...]
        r = lax.rsqrt(jnp.mean(xv * xv, axis=-1, keepdims=True) + RMS_EPS)
        dg_ref[...] = jnp.sum(dy_ref[...] * (xv * r), axis=0, keepdims=True)

    return pl.pallas_call(body, name=name, grid=(1,), in_specs=[_full((R, W))] * 2, out_specs=_full((1, W)),
                          out_shape=jax.ShapeDtypeStruct((1, W), F32), compiler_params=_params("arbitrary"))(dy, x)


def _loss_head(f, x1, tgt, g):
    S = f.shape[0]
    t = ROW_TILE

    def body(f_ref, x1_ref, t_ref, g_ref, loss_ref, dx2_ref, df_ref, dg_ref):
        i = pl.program_id(0)

        @pl.when(i == 0)
        def _():
            loss_ref[...] = jnp.zeros_like(loss_ref)
            dg_ref[...] = jnp.zeros_like(dg_ref)

        fv, gv = f_ref[...], g_ref[...]
        err = x1_ref[...] + _rms_fwd_val(fv, gv) - t_ref[...]
        part = jnp.sum(jnp.mean(err * err, axis=-1, keepdims=True), axis=0, keepdims=True)
        loss_ref[...] += jnp.broadcast_to(0.5 * part, loss_ref.shape)
        dx2 = err * (1.0 / D_MODEL)
        dx2_ref[...] = dx2
        df, gg = _rms_bwd_val(dx2, fv, gv)
        df_ref[...] = df.astype(BF16)
        dg_ref[...] += jnp.sum(gg, axis=0, keepdims=True)

    return pl.pallas_call(
        body, name="loss_head", grid=(S // t,),
        in_specs=[_row_spec(t, D_MODEL)] * 3 + [_acc_spec(D_MODEL)],
        out_specs=[_acc_spec(LANES, 8), _row_spec(t, D_MODEL), _row_spec(t, D_MODEL), _acc_spec(D_MODEL)],
        out_shape=[jax.ShapeDtypeStruct((8, LANES), F32), jax.ShapeDtypeStruct((S, D_MODEL), F32),
                   jax.ShapeDtypeStruct((S, D_MODEL), BF16), jax.ShapeDtypeStruct((1, D_MODEL), F32)],
        compiler_params=_params("arbitrary"))(f, x1, tgt, g)


CONV_TC = 1408
CONV_TT = 256
HALO = 8


def _shift_down(u, halo, first):
    row = lax.broadcasted_iota(jnp.int32, u.shape, 0)
    h6 = jnp.where(first, 0.0, halo[HALO - 2:HALO - 1, :])
    h7 = jnp.where(first, 0.0, halo[HALO - 1:HALO, :])
    s1 = jnp.where(row == 0, h7, pltpu.roll(u, 1, 0))
    s2 = jnp.where(row == 0, h6, jnp.where(row == 1, h7, pltpu.roll(u, 2, 0)))
    return s1, s2


def _conv_specs(tt, n_c, n_t, lead):
    def halo_row(i):
        return jnp.maximum(i * (tt // HALO) - 1, 0) if lead else jnp.minimum((i + 1) * (tt // HALO), n_t * (tt // HALO) - 1)
    return [
        pl.BlockSpec((tt, CONV_TC), lambda j, i: (i, j)),
        pl.BlockSpec((tt, CONV_TC), lambda j, i: (i, j + n_c)),
        pl.BlockSpec((HALO, CONV_TC), lambda j, i: (halo_row(i), j)),
        pl.BlockSpec((HALO, CONV_TC), lambda j, i: (halo_row(i), j + n_c)),
    ]


def _conv_z(ug, uv, hg, hv, w_g, w_v, b_g, b_v, first):
    g1, g2 = _shift_down(ug, hg, first)
    v1, v2 = _shift_down(uv, hv, first)
    zg = b_g + w_g[0:1, :] * g2
    zg = zg + w_g[1:2, :] * g1
    zg = zg + w_g[2:3, :] * ug
    zv = b_v + w_v[0:1, :] * v2
    zv = zv + w_v[1:2, :] * v1
    zv = zv + w_v[2:3, :] * uv
    return zg, zv, (g2, g1, ug), (v2, v1, uv)


def _conv_fwd(u, conv_w, conv_b):
    S = u.shape[0]
    tt = min(CONV_TT, S)
    n_c, n_t = D_FF // CONV_TC, S // tt
    wspec = [pl.BlockSpec((3, CONV_TC), lambda j, i: (0, j)), pl.BlockSpec((3, CONV_TC), lambda j, i: (0, j + n_c)),
             pl.BlockSpec((1, CONV_TC), lambda j, i: (0, j)), pl.BlockSpec((1, CONV_TC), lambda j, i: (0, j + n_c))]

    def body(ug_ref, uv_ref, hg_ref, hv_ref, wg_ref, wv_ref, bg_ref, bv_ref, a_ref):
        first = pl.program_id(1) == 0
        zg, zv, _, _ = _conv_z(ug_ref[...], uv_ref[...], hg_ref, hv_ref, wg_ref, wv_ref, bg_ref[...], bv_ref[...], first)
        a_ref[...] = (zg * jax.nn.sigmoid(zg) * zv).astype(BF16)

    return pl.pallas_call(
        body, name="conv_fwd", grid=(n_c, n_t), in_specs=_conv_specs(tt, n_c, n_t, True) + wspec,
        out_specs=pl.BlockSpec((tt, CONV_TC), lambda j, i: (i, j)), out_shape=jax.ShapeDtypeStruct((S, D_FF), BF16),
        compiler_params=_params("parallel", "parallel"))(u, u, u, u, conv_w, conv_w, conv_b, conv_b)


def _gate_bwd(da, zg, zv):
    sg = jax.nn.sigmoid(zg)
    return da * zv * (sg * (1.0 + zg * (1.0 - sg))), da * (zg * sg)


def _conv_bwd(da, u, conv_w, conv_b):
    S = u.shape[0]
    tt = min(CONV_TT, S)
    n_c, n_t = D_FF // CONV_TC, S // tt
    wspec = [pl.BlockSpec((3, CONV_TC), lambda j, i: (0, j)), pl.BlockSpec((3, CONV_TC), lambda j, i: (0, j + n_c)),
             pl.BlockSpec((1, CONV_TC), lambda j, i: (0, j)), pl.BlockSpec((1, CONV_TC), lambda j, i: (0, j + n_c))]
    tile = pl.BlockSpec((tt, CONV_TC), lambda j, i: (i, j))
    trail = pl.BlockSpec((HALO, CONV_TC), lambda j, i: (jnp.minimum((i + 1) * (tt // HALO), n_t * (tt // HALO) - 1), j))

    def body(da_ref, ug_ref, uv_ref, hg_ref, hv_ref, tda_ref, tg_ref, tv_ref, wg_ref, wv_ref, bg_ref, bv_ref,
             dug_ref, duv_ref, dwg_ref, dwv_ref, dbg_ref, dbv_ref):
        i = pl.program_id(1)
        last = i == n_t - 1
        bg, bv = bg_ref[...], bv_ref[...]
        zg, zv, gs, vs = _conv_z(ug_ref[...], uv_ref[...], hg_ref, hv_ref, wg_ref, wv_ref, bg, bv, i == 0)
        dzg, dzv = _gate_bwd(da_ref[...], zg, zv)
        tzg, tzv, _, _ = _conv_z(tg_ref[...], tv_ref[...], ug_ref.at[pl.ds(tt - HALO, HALO), :], uv_ref.at[pl.ds(tt - HALO, HALO), :],
                                 wg_ref, wv_ref, bg, bv, False)
        tdzg, tdzv = _gate_bwd(tda_ref[...], tzg, tzv)
        row = lax.broadcasted_iota(jnp.int32, dzg.shape, 0)
        for dz, tdz, w_ref, du_ref in ((dzg, tdzg, wg_ref, dug_ref), (dzv, tdzv, wv_ref, duv_ref)):
            h0 = jnp.where(last, 0.0, tdz[0:1, :])
            h1 = jnp.where(last, 0.0, tdz[1:2, :])
            u1 = jnp.where(row == tt - 1, h0, pltpu.roll(dz, tt - 1, 0))
            u2 = jnp.where(row == tt - 1, h1, jnp.where(row == tt - 2, h0, pltpu.roll(dz, tt - 2, 0)))
            du_ref[...] = (w_ref[2:3, :] * dz + w_ref[1:2, :] * u1 + w_ref[0:1, :] * u2).astype(BF16)

        @pl.when(i == 0)
        def _():
            for r in (dwg_ref, dwv_ref, dbg_ref, dbv_ref):
                r[...] = jnp.zeros_like(r)

        for k in range(3):
            dwg_ref[k:k + 1, :] += jnp.sum(dzg * gs[k], axis=0, keepdims=True)
            dwv_ref[k:k + 1, :] += jnp.sum(dzv * vs[k], axis=0, keepdims=True)
        dbg_ref[...] += jnp.sum(dzg, axis=0, keepdims=True)
        dbv_ref[...] += jnp.sum(dzv, axis=0, keepdims=True)

    lead = _conv_specs(tt, n_c, n_t, True)
    trail_v = pl.BlockSpec((HALO, CONV_TC), lambda j, i: (jnp.minimum((i + 1) * (tt // HALO), n_t * (tt // HALO) - 1), j + n_c))
    outs = pl.pallas_call(
        body, name="conv_bwd", grid=(n_c, n_t), in_specs=[tile] + lead + [trail, trail, trail_v] + wspec,
        out_specs=[tile, tile] + [pl.BlockSpec((3, CONV_TC), lambda j, i: (0, j))] * 2 + [pl.BlockSpec((1, CONV_TC), lambda j, i: (0, j))] * 2,
        out_shape=[jax.ShapeDtypeStruct((S, D_FF), BF16)] * 2 + [jax.ShapeDtypeStruct((3, D_FF), F32)] * 2
        + [jax.ShapeDtypeStruct((1, D_FF), F32)] * 2,
        compiler_params=_params("parallel", "arbitrary"))(da, u, u, u, u, da, u, u, conv_w, conv_w, conv_b, conv_b)
    dug, duv, dwg, dwv, dbg, dbv = outs
    return dug, duv, jnp.concatenate([dwg, dwv], axis=1), jnp.concatenate([dbg, dbv], axis=1)


BAND = 128


MEM_TQ = 512


def _mem_fwd(q, kv, *, name):
    S, W = q.shape
    M = kv.shape[0]
    nh = W // LANES
    tq = min(MEM_TQ, S)

    def body(q_ref, k_ref, v_ref, o_ref, l_ref):
        s = lax.dot_general(q_ref[...], k_ref[...].astype(BF16), NT, preferred_element_type=F32)
        m = jnp.max(s, axis=1, keepdims=True)
        p = jnp.exp(s - m)
        l = jnp.sum(p, axis=1, keepdims=True)
        o_ref[...] = (lax.dot_general(p.astype(BF16), v_ref[...].astype(BF16), NN, preferred_element_type=F32) / l).astype(BF16)
        l_ref[...] = jnp.broadcast_to(m + jnp.log(l), (tq, LANES))

    blk = pl.BlockSpec((tq, LANES), lambda hh, i: (i, hh))
    return pl.pallas_call(
        body, name=name, grid=(nh, S // tq),
        in_specs=[blk, pl.BlockSpec((M, LANES), lambda hh, i: (0, hh)), pl.BlockSpec((M, LANES), lambda hh, i: (0, hh + nh))],
        out_specs=[blk, blk], out_shape=[jax.ShapeDtypeStruct((S, W), BF16), jax.ShapeDtypeStruct((S, W), F32)],
        compiler_params=_params("parallel", "parallel"))(q, kv, kv)


def _mem_bwd(q, kv, do, lse, delta, *, scale, name):
    S, W = q.shape
    M = kv.shape[0]
    nh = W // LANES
    tq = min(MEM_TQ, S)

    def body(q_ref, k_ref, v_ref, do_ref, l_ref, d_ref, dq_ref, dk_ref, dv_ref):
        i = pl.program_id(1)

        @pl.when(i == 0)
        def _():
            dk_ref[...] = jnp.zeros_like(dk_ref)
            dv_ref[...] = jnp.zeros_like(dv_ref)

        qv = q_ref[...]
        kv_, vv = k_ref[...].astype(BF16), v_ref[...].astype(BF16)
        dov = do_ref[...].astype(BF16)
        s = lax.dot_general(qv, kv_, NT, preferred_element_type=F32)
        p = jnp.exp(s - l_ref[...][:, 0:1])
        dp = lax.dot_general(dov, vv, NT, preferred_element_type=F32)
        ds = (p * (dp - d_ref[...][:, 0:1])).astype(BF16)
        dq_ref[...] = lax.dot_general(ds, kv_, NN, preferred_element_type=F32) * scale
        dk_ref[...] += lax.dot_general(ds, qv, TN, preferred_element_type=F32)
        dv_ref[...] += lax.dot_general(p.astype(BF16), dov, TN, preferred_element_type=F32)

    blk = pl.BlockSpec((tq, LANES), lambda hh, i: (i, hh))
    kblk = pl.BlockSpec((M, LANES), lambda hh, i: (0, hh))
    vblk = pl.BlockSpec((M, LANES), lambda hh, i: (0, hh + nh))
    dq, dk, dv = pl.pallas_call(
        body, name=name, grid=(nh, S // tq), in_specs=[blk, kblk, vblk, blk, blk, blk], out_specs=[blk, kblk, kblk],
        out_shape=[jax.ShapeDtypeStruct((S, W), F32), jax.ShapeDtypeStruct((M, W), F32), jax.ShapeDtypeStruct((M, W), F32)],
        compiler_params=_params("parallel", "arbitrary"))(q, kv, kv, do, lse, delta)
    return dq, jnp.concatenate([dk, dv], axis=1)


CAUSAL_BLOCK = 512
STRIP = 32


def _causal_fwd(q, k, v, *, name):
    S, W = q.shape
    T = min(CAUSAL_BLOCK, S // 2)
    n_strips = T // STRIP

    def body(q_ref, k_ref, v_ref, o_ref, lse_ref, s0, s1, p0, p1, a0, a1, acc_scr):
        i = pl.program_id(1)
        s_scr, p_scr, a_scr = (s0, s1), (p0, p1), (a0, a1)

        def rows(j):
            return pl.ds(pl.multiple_of(j * T, T), T)

        def scores(j, slot):
            s_scr[slot][...] = lax.dot_general(q_ref[...], k_ref[rows(j), :], NT, preferred_element_type=F32)

        def softmax(slot, stats, diag):
            def strip(r):
                s = s_scr[slot][r * STRIP:(r + 1) * STRIP, :]
                if diag:
                    row = r * STRIP + lax.broadcasted_iota(jnp.int32, s.shape, 0)
                    s = jnp.where(row >= lax.broadcasted_iota(jnp.int32, s.shape, 1), s, NEG_INF)
                return s

            m_new = [jnp.maximum(m_old, jnp.max(strip(r), axis=1, keepdims=True)) for r, (m_old, _) in enumerate(stats)]
            new = []
            for r, (m_old, l_old) in enumerate(stats):
                rs = slice(r * STRIP, (r + 1) * STRIP)
                p = jnp.exp(strip(r) - m_new[r])
                alpha = jnp.exp(m_old - m_new[r])
                new.append((m_new[r], alpha * l_old + jnp.sum(p, axis=1, keepdims=True)))
                a_scr[slot][rs, :] = alpha
                p_scr[slot][rs, :] = p.astype(BF16)
            return tuple(new)

        def values(j, slot):
            acc_scr[...] = a_scr[slot][...] * acc_scr[...] + lax.dot_general(p_scr[slot][...], v_ref[rows(j), :], NN,
                                                                            preferred_element_type=F32)

        def trip(j, stats, mine, other):
            scores(j + 1, other)
            stats = softmax(mine, stats, False)
            values(jnp.maximum(j - 1, 0), other)
            return stats

        def pair(jj, stats):
            return trip(2 * jj + 1, trip(2 * jj, stats, 0, 1), 1, 0)

        def last(stats, mine, other):
            values(jnp.maximum(i - 1, 0), other)
            stats = softmax(mine, stats, True)
            values(i, mine)
            for r, (m, l) in enumerate(stats):
                rs = slice(r * STRIP, (r + 1) * STRIP)
                o_ref[rs, :] = (acc_scr[rs, :] / l).astype(BF16)
                lse_ref[rs, :] = jnp.broadcast_to(m + jnp.log(l), (STRIP, LANES))

        acc_scr[...] = jnp.zeros_like(acc_scr)
        p1[...] = jnp.zeros_like(p1)
        a1[...] = jnp.ones_like(a1)
        scores(0, 0)
        init = tuple((jnp.full((STRIP, 1), NEG_INF, F32), jnp.zeros((STRIP, 1), F32)) for _ in range(n_strips))
        stats = lax.fori_loop(0, i // 2, pair, init)

        @pl.when(i % 2 == 1)
        def _():
            last(trip(i - 1, stats, 0, 1), 1, 0)

        @pl.when(i % 2 == 0)
        def _():
            last(stats, 0, 1)

    blk = pl.BlockSpec((T, LANES), lambda hh, i: (i, hh))
    whole = pl.BlockSpec((S, LANES), lambda hh, i: (0, hh))
    return pl.pallas_call(
        body, name=name, grid=(W // LANES, S // T), in_specs=[blk, whole, whole], out_specs=[blk, blk],
        out_shape=[jax.ShapeDtypeStruct((S, W), BF16), jax.ShapeDtypeStruct((S, W), F32)],
        scratch_shapes=[pltpu.VMEM((T, T), F32)] * 2 + [pltpu.VMEM((T, T), BF16)] * 2 + [pltpu.VMEM((T, 1), F32)] * 2
        + [pltpu.VMEM((T, LANES), F32)],
        compiler_params=_params("parallel", "arbitrary"))(q, k, v)


def _causal_bwd(q, k, v, do, lse, delta, *, name):
    S, W = q.shape
    T = min(CAUSAL_BLOCK, S // 2)
    nq = S // T
    n_strips, n_col = T // STRIP, T // LANES

    def body(q_ref, k_ref, v_ref, do_ref, l_ref, d_ref, dq_ref, dk_ref, dv_ref, s0, s1, e0, e1, p0, p1, g0, g1):
        j = pl.program_id(1)
        s_scr, e_scr, p_scr, g_scr = (s0, s1), (e0, e1), (p0, p1), (g0, g1)

        @pl.when(j == 0)
        def _():
            dq_ref[...] = jnp.zeros_like(dq_ref)

        dk_ref[...] = jnp.zeros_like(dk_ref)
        dv_ref[...] = jnp.zeros_like(dv_ref)

        def rows(i):
            return pl.ds(pl.multiple_of(jnp.minimum(i, nq - 1) * T, T), T)

        def products(i, slot):
            r = rows(i)
            s_scr[slot][...] = lax.dot_general(q_ref[r, :], k_ref[...], NT, preferred_element_type=F32)
            e_scr[slot][...] = lax.dot_general(do_ref[r, :].astype(BF16), v_ref[...], NT, preferred_element_type=F32)

        def pointwise(i, slot, diag):
            base = pl.multiple_of(i * T, T)
            for r in range(n_strips):
                rs = slice(r * STRIP, (r + 1) * STRIP)
                lse_r = l_ref[pl.ds(base + r * STRIP, STRIP), :]
                del_r = d_ref[pl.ds(base + r * STRIP, STRIP), :]
                for c in range(n_col):
                    cs = slice(c * LANES, (c + 1) * LANES)
                    if diag and c * LANES > (r + 1) * STRIP - 1:
                        p_scr[slot][rs, cs] = jnp.zeros((STRIP, LANES), BF16)
                        g_scr[slot][rs, cs] = jnp.zeros((STRIP, LANES), BF16)
                        continue
                    sv = s_scr[slot][rs, cs]
                    if diag and (c + 1) * LANES - 1 > r * STRIP:
                        row = r * STRIP + lax.broadcasted_iota(jnp.int32, sv.shape, 0)
                        col = c * LANES + lax.broadcasted_iota(jnp.int32, sv.shape, 1)
                        sv = jnp.where(row >= col, sv, NEG_INF)
                    p = jnp.exp(sv - lse_r)
                    p_scr[slot][rs, cs] = p.astype(BF16)
                    g_scr[slot][rs, cs] = (p * (e_scr[slot][rs, cs] - del_r)).astype(BF16)

        def gradients(i, slot):
            r = rows(i)
            qi, doi = q_ref[r, :], do_ref[r, :].astype(BF16)
            g = g_scr[slot][...]
            dv_ref[...] += lax.dot_general(p_scr[slot][...], doi, TN, preferred_element_type=F32)
            dk_ref[...] += lax.dot_general(g, qi, TN, preferred_element_type=F32)
            dq_ref[r, :] += lax.dot_general(g, k_ref[...], NN, preferred_element_type=F32)

        products(j, 0)
        products(j + 1, 1)
        pointwise(j, 0, True)

        def trip(i, mine, other):
            products(i + 1, other)
            pointwise(i, mine, False)
            gradients(i - 1, other)

        def pair(t, _):
            trip(j + 1 + 2 * t, 1, 0)
            trip(j + 2 + 2 * t, 0, 1)
            return 0

        n_rest = nq - 1 - j
        lax.fori_loop(0, n_rest // 2, pair, 0)

        @pl.when(n_rest % 2 == 1)
        def _():
            trip(nq - 1, 1, 0)
            gradients(nq - 1, 1)

        @pl.when(n_rest % 2 == 0)
        def _():
            gradients(nq - 1, 0)

    blk = pl.BlockSpec((T, LANES), lambda hh, j: (j, hh))
    whole = pl.BlockSpec((S, LANES), lambda hh, j: (0, hh))
    return pl.pallas_call(
        body, name=name, grid=(W // LANES, S // T), in_specs=[whole, blk, blk, whole, whole, whole],
        out_specs=[whole, blk, blk], out_shape=[jax.ShapeDtypeStruct((S, W), F32)] * 3,
        scratch_shapes=[pltpu.VMEM((T, T), F32)] * 4 + [pltpu.VMEM((T, T), BF16)] * 4,
        compiler_params=_params("parallel", "arbitrary"))(q, k, v, do, lse, delta)


BAND_TQ = 512


def _band_window(i, sub, nsub, L, q_ref, k_ref, v_ref, slope):
    kw = min(2 * BAND, L)
    n = i * nsub + sub
    k0 = 0 if kw == L else pl.multiple_of(jnp.maximum(n - 1, 0) * BAND, BAND)
    win = pl.ds(k0, kw)
    qs = q_ref[sub * BAND:(sub + 1) * BAND, :]
    kwv, vwv = k_ref[win, :], v_ref[win, :]
    s = lax.dot_general(qs, kwv, NT, preferred_element_type=F32)
    dist = (n * BAND + lax.broadcasted_iota(jnp.int32, s.shape, 0)) - (k0 + lax.broadcasted_iota(jnp.int32, s.shape, 1))
    s = jnp.where((dist >= 0) & (dist <= BAND), s - slope * dist.astype(F32), NEG_INF)
    return win, qs, kwv, vwv, s


def _band_fwd(q, k, v, slopes, *, n_heads, qcol, kcol, vcol, L, slope_mul, o_shape, name):
    tq = min(BAND_TQ, L)
    nsub = tq // BAND

    def body(sl_ref, q_ref, k_ref, v_ref, o_ref, l_ref):
        hh, i = pl.program_id(0), pl.program_id(1)
        slope = sl_ref[hh % DIL_HPG] * slope_mul
        wins = [_band_window(i, sub, nsub, L, q_ref, k_ref, v_ref, slope) for sub in range(nsub)]
        ms = [jnp.max(w[4], axis=1, keepdims=True) for w in wins]
        ps = [jnp.exp(w[4] - m) for w, m in zip(wins, ms)]
        ls = [jnp.sum(p, axis=1, keepdims=True) for p in ps]
        for sub, (w, m, p, l) in enumerate(zip(wins, ms, ps, ls)):
            rows = slice(sub * BAND, (sub + 1) * BAND)
            o_ref[rows, :] = lax.dot_general(p.astype(BF16), w[3], NN, preferred_element_type=F32) / l
            l_ref[rows, :] = jnp.broadcast_to(m + jnp.log(l), (BAND, LANES))

    whole = lambda col: pl.BlockSpec((L, LANES), lambda hh, i: (0, col(hh)))
    o_spec = pl.BlockSpec((tq, LANES), lambda hh, i: (i, hh))
    return pl.pallas_call(
        body, name=name, grid=(n_heads, L // tq),
        in_specs=[pl.BlockSpec(memory_space=pltpu.SMEM), pl.BlockSpec((tq, LANES), lambda hh, i: (i, qcol(hh))), whole(kcol), whole(vcol)],
        out_specs=[o_spec, o_spec], out_shape=[jax.ShapeDtypeStruct(o_shape, F32)] * 2,
        compiler_params=_params("parallel", "arbitrary"))(slopes, q, k, v)


def _band_bwd(q, k, v, do, lse, delta, slopes, *, n_heads, qcol, kcol, vcol, L, scale, slope_mul, d_shape, name):
    tq = min(BAND_TQ, L)
    nsub = tq // BAND
    n_steps = L // tq

    def body(sl_ref, q_ref, k_ref, v_ref, do_ref, l_ref, d_ref, dq_ref, dk_ref, dv_ref, dk_acc, dv_acc):
        hh, i = pl.program_id(0), pl.program_id(1)
        slope = sl_ref[hh % DIL_HPG] * slope_mul

        @pl.when(i == 0)
        def _():
            dk_acc[...] = jnp.zeros_like(dk_acc)
            dv_acc[...] = jnp.zeros_like(dv_acc)

        blocks = range(nsub)
        rows = [slice(sub * BAND, (sub + 1) * BAND) for sub in blocks]
        wins = [_band_window(i, sub, nsub, L, q_ref, k_ref, v_ref, slope) for sub in blocks]
        dos = [do_ref[r, :] for r in rows]
        dps = [lax.dot_general(do, w[3], NT, preferred_element_type=F32) for do, w in zip(dos, wins)]
        ps = [jnp.exp(w[4] - l_ref[r, :][:, 0:1]) for w, r in zip(wins, rows)]
        dss = [(p * (dp - d_ref[r, :][:, 0:1])).astype(BF16) for p, dp, r in zip(ps, dps, rows)]
        for r, ds, w in zip(rows, dss, wins):
            dq_ref[r, :] = (lax.dot_general(ds, w[2], NN, preferred_element_type=F32) * scale).astype(BF16)
        dks = [lax.dot_general(ds, w[1], TN, preferred_element_type=F32) for ds, w in zip(dss, wins)]
        dvs = [lax.dot_general(p.astype(BF16), do, TN, preferred_element_type=F32) for p, do in zip(ps, dos)]
        for w, dk, dv in zip(wins, dks, dvs):
            dk_acc[w[0], :] += dk
            dv_acc[w[0], :] += dv

        @pl.when(i == n_steps - 1)
        def _():
            dk_ref[...] = dk_acc[...].astype(BF16)
            dv_ref[...] = dv_acc[...].astype(BF16)

    whole = lambda col: pl.BlockSpec((L, LANES), lambda hh, i: (0, col(hh)))
    blk = pl.BlockSpec((tq, LANES), lambda hh, i: (i, hh))
    ident = lambda hh: hh
    return pl.pallas_call(
        body, name=name, grid=(n_heads, n_steps),
        in_specs=[pl.BlockSpec(memory_space=pltpu.SMEM), pl.BlockSpec((tq, LANES), lambda hh, i: (i, qcol(hh))), whole(kcol), whole(vcol),
                  blk, blk, blk],
        out_specs=[blk, whole(ident), whole(ident)], out_shape=[jax.ShapeDtypeStruct(d_shape, BF16)] * 3,
        scratch_shapes=[pltpu.VMEM((L, LANES), F32)] * 2,
        compiler_params=_params("parallel", "arbitrary"))(slopes, q, k, v, do, lse, delta)


def _pad_heads(w, n_heads, width, axis):
    shp = w.shape
    new = shp[:axis] + (n_heads, width) + shp[axis + 1:]
    pad = [(0, 0)] * len(new)
    pad[axis + 1] = (0, LANES - width)
    out = jnp.pad(w.reshape(new), pad)
    return out.reshape(shp[:axis] + (n_heads * LANES,) + shp[axis + 1:])


def _unpad_heads(w, n_heads, width, axis):
    shp = w.shape
    new = shp[:axis] + (n_heads, LANES) + shp[axis + 1:]
    out = lax.slice_in_dim(w.reshape(new), 0, width, axis=axis + 1)
    return out.reshape(shp[:axis] + (n_heads * width,) + shp[axis + 1:])


def _alibi_slopes():
    s = jnp.exp2(-8.0 * jnp.arange(1, DIL_HEADS + 1, dtype=F32) / DIL_HEADS)
    return s.reshape(DIL_HPG, DIL_GROUPS).T


def _local_step(x, mem, positions, tgt, W):
    S = x.shape[0]
    pos = positions.reshape(S, 1).astype(F32)
    half = MLA_ROPE // 2
    inv_freq = ROPE_THETA ** (-jnp.arange(half, dtype=F32) / half)
    invf = jnp.zeros((1, LANES), F32).at[0, MLA_NOPE:MLA_NOPE + half].set(inv_freq).at[0, MLA_NOPE + half:MLA_QK].set(inv_freq)
    slopes = _alibi_slopes()

    w_in = W["w_in"]
    zc = lambda n: jnp.zeros((D_MODEL, n), BF16)
    w_a = jnp.concatenate([w_in[:, :OFF_Q], zc(MLA_NOPE), w_in[:, OFF_KV:OFF_KR], zc(LANES - MLA_QK), w_in[:, OFF_Q:OFF_KV]], axis=1)
    w_d, w_m, w_g = w_in[:, OFF_KR:OFF_DIL], w_in[:, OFF_DIL:OFF_MEMQ], w_in[:, OFF_MEMQ:]
    w_uq_p = _pad_heads(W["w_uq"], MLA_HEADS, MLA_QK, 1)
    ukv = W["w_ukv"].reshape(KV_RANK, MLA_HEADS, 2 * MLA_NOPE)
    w_uk_p = _pad_heads(ukv[:, :, :MLA_NOPE].reshape(KV_RANK, -1), MLA_HEADS, MLA_NOPE, 1)
    w_uv_p = _pad_heads(ukv[:, :, MLA_NOPE:].reshape(KV_RANK, -1), MLA_HEADS, MLA_NOPE, 1)
    w_br_mla_p = _pad_heads(W["w_br_mla"], MLA_HEADS, MLA_NOPE, 0)

    h = _rmsnorm(x, W["g_pre_mix"], BF16, "norm1")
    proj_a = _matmul(h, w_a, tn=768, name="proj_a")
    att_scale = LANES ** -0.5
    w_dg = [jnp.concatenate([w_d[:, (part * DIL_GROUPS + g) * DIL_W:(part * DIL_GROUPS + g + 1) * DIL_W] for part in range(3)], axis=1)
            for g in range(DIL_GROUPS)]
    proj_m = _matmul(h, w_m, tn=MEM_W, out_dtype=BF16, scale=(1, att_scale), name="proj_m")
    proj_g = _matmul(h, w_g, tn=1536, name="proj_g")

    cq_n, ckv_n, kpe = _mla_prep(proj_a, pos, invf, W["mla_q_norm"], W["mla_kv_norm"])
    q_pre = _matmul(cq_n, w_uq_p, tn=1024, name="mla_q")
    k_pre = _matmul(ckv_n, w_uk_p, tn=1024, name="mla_k")
    v_mla = _matmul(ckv_n, w_uv_p, tn=1024, out_dtype=BF16, name="mla_v")
    q_mla, k_mla = _qk_final(q_pre, k_pre, kpe, pos, invf)
    ident = lambda hh: hh
    o_mla, lse_mla = _causal_fwd(q_mla, k_mla, v_mla, name="attn_mla_fwd")

    col = lambda part: (lambda hh: (hh // DIL_HPG) * 3 * DIL_HPG + part * DIL_HPG + hh % DIL_HPG)
    h_views = [h.reshape(S // d, d * D_MODEL) for d in DILATIONS]
    proj_dg, o_dil, lse_dil = [], [], []
    for g, d in enumerate(DILATIONS):
        pv = _proj_view(h_views[g], w_dg[g], d, att_scale, f"proj_d{g}")
        o, lse = _band_fwd(pv, pv, pv, slopes[g], n_heads=d * DIL_HPG, qcol=col(0), kcol=col(1), vcol=col(2), L=S // d,
                           slope_mul=float(d), o_shape=(S // d, d * DIL_W), name=f"attn_dil{g}_fwd")
        proj_dg.append(pv)
        o_dil.append(o)
        lse_dil.append(lse)
    y_dil = _dil_mix(o_dil, lse_dil)

    mem_n = _rmsnorm(mem, W["g_mem"], BF16, "mem_norm")
    kv_mem = _matmul(mem_n, W["w_mem_kv"], name="mem_kv")
    o_mem, lse_mem = _mem_fwd(proj_m, kv_mem, name="attn_mem_fwd")

    b_mla = _matmul(o_mla, w_br_mla_p, name="br_mla")
    b_dil = _matmul(y_dil, W["w_br_dil"], name="br_dil")
    b_mem = _matmul(o_mem, W["w_br_mem"], name="br_mem")
    merged = _merge(proj_g, W["b_gate"], [b_mla, b_dil, b_mem])
    o_proj = _matmul(merged, W["w_o"], name="o_proj")
    x1, h2 = _norm2(o_proj, x, W["g_post_mix"], W["g_pre_ffn"])

    u = _matmul(h2, W["w_ffn_up"], tn=1408, name="ffn_up")
    act = _conv_fwd(u, W["conv_w"], W["conv_b"])
    f = _matmul(act, W["w_ffn_down"], name="ffn_down")
    loss8, dx2, df, dg_post_ffn = _loss_head(f, x1, tgt, W["g_post_ffn"])
    loss = loss8[0, 0]

    G = {"g_post_ffn": dg_post_ffn}
    d_act = _matmul(df, W["w_ffn_down"], mode="nt", tn=1408, name="d_act")
    G["w_ffn_down"] = _matmul(act, df, mode="tn", tm=1408, tk=1024, name="dw_ffn_down")
    du_g, du_v, G["conv_w"], G["conv_b"] = _conv_bwd(d_act, u, W["conv_w"], W["conv_b"])
    dh2 = _matmul(du_g, W["w_ffn_up"][:, :D_FF], mode="nt", name="d_h2_gate")
    dh2 = _matmul(du_v, W["w_ffn_up"][:, D_FF:], mode="nt", add=dh2, name="d_h2_value")
    G["w_ffn_up"] = jnp.concatenate([_matmul(h2, du_g, mode="tn", tn=1408, tk=1024, name="dw_ffn_up_gate"),
                                     _matmul(h2, du_v, mode="tn", tn=1408, tk=1024, name="dw_ffn_up_value")], axis=1)
    dx1, do_proj, G["g_pre_ffn"], G["g_post_mix"] = _norm2_bwd(dx2, dh2, x1, o_proj, W["g_pre_ffn"], W["g_post_mix"])
    dmerged = _matmul(do_proj, W["w_o"], mode="nt", name="d_merged")
    G["w_o"] = _matmul(merged, do_proj, mode="tn", tk=1024, name="dw_o")
    db_mla, db_dil, db_mem, dproj_g, G["b_gate"] = _merge_bwd(dmerged, proj_g, W["b_gate"], [b_mla, b_dil, b_mem])

    dy_mem = _matmul(db_mem, W["w_br_mem"], mode="nt", name="d_y_mem")
    G["w_br_mem"] = _matmul(o_mem, db_mem, mode="tn", tk=1024, name="dw_br_mem")
    delta_mem = _delta(dy_mem, o_mem, MEM_HEADS, "delta_mem")
    dq_mem, dkv_mem = _mem_bwd(proj_m, kv_mem, dy_mem, lse_mem, delta_mem, scale=att_scale, name="attn_mem_bwd")
    G["w_mem_kv"] = _matmul(mem_n, dkv_mem, mode="tn", name="dw_mem_kv")
    dmem_n = _matmul(dkv_mem, W["w_mem_kv"], mode="nt", name="d_mem_n")
    G["g_mem"] = _gain_grad(dmem_n, mem, "dg_mem")

    dy_dil = _matmul(db_dil, W["w_br_dil"], mode="nt", name="d_y_dil")
    G["w_br_dil"] = _matmul(y_dil, db_dil, mode="tn", tk=1024, name="dw_br_dil")
    mix = _dil_mix_bwd(dy_dil, o_dil, lse_dil)
    do_dil, dl_dil = mix[:3], mix[3:]
    dh_views, dw_groups = [], []
    for g, d in enumerate(DILATIONS):
        pv = proj_dg[g]
        parts = _band_bwd(pv, pv, pv, do_dil[g], lse_dil[g], dl_dil[g], slopes[g], n_heads=d * DIL_HPG, qcol=col(0),
                          kcol=col(1), vcol=col(2), L=S // d, scale=att_scale, slope_mul=float(d), d_shape=(S // d, d * DIL_W),
                          name=f"attn_dil{g}_bwd")
        dpv = jnp.concatenate([p[:, r * DIL_W:(r + 1) * DIL_W] for r in range(d) for p in parts], axis=1)
        dh_views.append(_dh_view(dpv, w_dg[g], d, None, f"d_h_d{g}"))
        dw_groups.append(_dw_view(h_views[g], dpv, d, f"dw_in_d{g}"))
    dw_d = jnp.concatenate([dw[:, part * DIL_W:(part + 1) * DIL_W] for part in range(3) for dw in dw_groups], axis=1)

    dy_mla = _matmul(db_mla, w_br_mla_p, mode="nt", name="d_y_mla")
    dw_br_mla_p = _matmul(o_mla, db_mla, mode="tn", tk=1024, name="dw_br_mla")
    G["w_br_mla"] = _unpad_heads(dw_br_mla_p, MLA_HEADS, MLA_NOPE, 0)
    delta_mla = _delta(dy_mla, o_mla, MLA_HEADS, "delta_mla")
    dq_mla, dk_mla, dv_mla = _causal_bwd(q_mla, k_mla, v_mla, dy_mla, lse_mla, delta_mla, name="attn_mla_bwd")
    dq_pre, dkpe = _mla_bwd_prep(dq_mla, dk_mla, pos, invf)
    dcq_n = _matmul(dq_pre, w_uq_p, mode="nt", tn=Q_RANK, name="d_cq")
    G["w_uq"] = _unpad_heads(_matmul(cq_n, dq_pre, mode="tn", tm=Q_RANK, tk=1024, name="dw_uq"), MLA_HEADS, MLA_QK, 1)
    dckv_a = _matmul(dk_mla, w_uk_p, mode="nt", tn=KV_RANK, name="d_ckv_k")
    dckv_b = _matmul(dv_mla, w_uv_p, mode="nt", tn=KV_RANK, name="d_ckv_v")
    dw_uk = _unpad_heads(_matmul(ckv_n, dk_mla, mode="tn", tm=KV_RANK, tk=1024, name="dw_uk"), MLA_HEADS, MLA_NOPE, 1)
    dw_uv = _unpad_heads(_matmul(ckv_n, dv_mla, mode="tn", tm=KV_RANK, tk=1024, name="dw_uv"), MLA_HEADS, MLA_NOPE, 1)
    G["w_ukv"] = jnp.concatenate([dw_uk.reshape(KV_RANK, MLA_HEADS, MLA_NOPE), dw_uv.reshape(KV_RANK, MLA_HEADS, MLA_NOPE)],
                                 axis=2).reshape(KV_RANK, -1)
    dproj_a, G["mla_q_norm"], G["mla_kv_norm"] = _mla_norm_bwd(dcq_n, dckv_a, dckv_b, dkpe, proj_a, W["mla_q_norm"],
                                                              W["mla_kv_norm"])

    dh = _matmul(dproj_a, w_a, mode="nt", name="d_h_a")
    dh = _matmul(dq_mem, w_m, mode="nt", add=dh, name="d_h_m")
    dh = _matmul(dproj_g, w_g, mode="nt", add=dh, name="d_h_g")
    dw_a = _matmul(h, dproj_a, mode="tn", tn=768, tk=1024, name="dw_in_a")
    dw_m = _matmul(h, dq_mem, mode="tn", tn=512, tk=1024, name="dw_in_m")
    dw_g = _matmul(h, dproj_g, mode="tn", tn=1536, tk=1024, name="dw_in_g")
    kr0 = Q_RANK + MLA_NOPE
    G["w_in"] = jnp.concatenate([dw_a[:, :Q_RANK], dw_a[:, Q_RANK + LANES:], dw_a[:, kr0:kr0 + MLA_ROPE], dw_d, dw_m, dw_g], axis=1)
    grad_x, G["g_pre_mix"] = _norm1_bwd(dx1, [dh] + dh_views, x, W["g_pre_mix"])
    return loss, grad_x, G


WEIGHTS = ["g_pre_mix", "w_in", "b_gate", "mla_q_norm", "w_uq", "mla_kv_norm", "w_ukv", "g_mem", "w_mem_kv", "w_br_mla",
           "w_br_dil", "w_br_mem", "w_o", "g_post_mix", "g_pre_ffn", "w_ffn_up", "conv_w", "conv_b", "w_ffn_down", "g_post_ffn"]
GROUPS = [
    [("w_in", (D_MODEL, D_IN), 1)],
    [("w_uq", (Q_RANK, MLA_HEADS * MLA_QK), 1)],
    [("w_ukv", (KV_RANK, MLA_HEADS * 2 * MLA_NOPE), 1), ("w_br_mla", (MLA_HEADS * MLA_NOPE, D_MODEL), 1),
     ("w_br_dil", (DIL_W, D_MODEL), 1), ("w_br_mem", (MEM_W, D_MODEL), 1)],
    [("w_mem_kv", (D_MODEL, 2 * MEM_W), 0), ("w_o", (D_MODEL, D_MODEL), 0), ("w_ffn_down", (D_FF, D_MODEL), 0)],
    [("w_ffn_up", (D_MODEL, 2 * D_FF), 1)],
]
CONV_W = ("conv_w", (3, 2 * D_FF), 1)
REPLICATED = [("g_pre_mix", D_MODEL), ("b_gate", 3 * D_MODEL), ("mla_q_norm", Q_RANK), ("mla_kv_norm", KV_RANK), ("g_mem", D_MODEL),
              ("g_post_mix", D_MODEL), ("g_pre_ffn", D_MODEL), ("conv_b", 2 * D_FF), ("g_post_ffn", D_MODEL)]
SMALL_ROWS = 256
CONV_AT = sum(n for _, n in REPLICATED)
LOSS_AT = CONV_AT + 3 * 2 * D_FF


def _shard_shape(shape, axis):
    return tuple(d // N_CHIPS if a == axis else d for a, d in enumerate(shape))


def _group_shape(grp):
    shapes = [_shard_shape(shape, axis) for _, shape, axis in grp]
    assert len({s[1] for s in shapes}) == 1
    return sum(s[0] for s in shapes), shapes[0][1]


def _member_shards(a, axis):
    r, c = a.shape
    if axis == 0:
        return a.reshape(N_CHIPS, r // N_CHIPS, c)
    return a.reshape(r, N_CHIPS, c // N_CHIPS).transpose(1, 0, 2)


def _member_full(s, axis):
    n, r, c = s.shape
    if axis == 0:
        return s.reshape(n * r, c)
    return s.transpose(1, 0, 2).reshape(r, n * c)


def _my_weight_groups(w):
    out = []
    for grp in GROUPS:
        rows, width = _group_shape(grp)
        out.append(jnp.concatenate([w[name].astype(BF16) for name, _, _ in grp], axis=0).reshape(2, rows // 2, width))
    return out


def _full_weights(gathered, conv_all):
    out = {}
    for grp, ga in zip(GROUPS, gathered):
        ga = ga.reshape(N_CHIPS, -1, ga.shape[-1])
        off = 0
        for name, shape, axis in grp:
            rows = _shard_shape(shape, axis)[0]
            out[name] = _member_full(ga[:, off:off + rows], axis)
            off += rows
    out[CONV_W[0]] = _member_full(conv_all, CONV_W[2])
    return out


def _grad_groups(G):
    out = []
    for grp in GROUPS:
        rows, width = _group_shape(grp)
        a = jnp.concatenate([_member_shards(G[name], axis) for name, _, axis in grp], axis=1)
        out.append(a.reshape(N_CHIPS, 2, rows // 2, width).transpose(1, 0, 2, 3))
    return out


def _pack_small(vals, conv_g=None, loss=None):
    parts = [vals[name].reshape(-1) for name, _ in REPLICATED]
    if conv_g is not None:
        parts += [conv_g.reshape(-1), loss.reshape(1)]
    flat = jnp.concatenate(parts)
    return jnp.pad(flat, (0, SMALL_ROWS * LANES - flat.shape[0])).reshape(SMALL_ROWS, LANES)


def _unpack_small(packed):
    flat = packed.reshape(-1)
    out, off = {}, 0
    for name, n in REPLICATED:
        out[name] = flat[off:off + n].reshape(1, n)
        off += n
    return out


MESH = pl.DeviceIdType.MESH
HBM_SPEC = pl.BlockSpec(memory_space=pltpu.HBM)


def _place():
    x, y, c = lax.axis_index("x"), lax.axis_index("y"), lax.axis_index("c")
    chips = [(1 - x, y), (x, 1 - y), (1 - x, 1 - y)]
    return x, y, c, chips


def _remote(src, dst, send_sems, recv_sems, k, to):
    return pltpu.make_async_remote_copy(src_ref=src, dst_ref=dst, send_sem=send_sems.at[k], recv_sem=recv_sems.at[k],
                                        device_id=to, device_id_type=MESH)


def _gather_weights(groups, wholes, name, collective_id):
    n, m = len(groups), len(wholes)
    arrays = list(groups) + list(wholes)
    hbm = pltpu.MemorySpace.HBM
    srcs = [jax.new_ref(a, memory_space=hbm) for a in arrays]
    outs = [jax.empty_ref(jax.ShapeDtypeStruct((N_CHIPS,) + a.shape, a.dtype), memory_space=hbm) for a in arrays]
    n_sem = 6 * n + 3 * m

    def launch(send_sems, recv_sems):
        x, y, c, chips = _place()
        me = 2 * x + y
        sibling = (x, y, 1 - c)
        barrier = pltpu.get_barrier_semaphore()
        peers = [(px, py, c) for px, py in chips] + [sibling]
        for peer in peers:
            pl.semaphore_signal(barrier, inc=1, device_id=peer, device_id_type=MESH)
        pl.semaphore_wait(barrier, len(peers))
        first = [_remote(srcs[g].at[c], outs[g].at[me, c], send_sems, recv_sems, g * 3 + k, (px, py, c))
                 for k, (px, py) in enumerate(chips) for g in range(n)]
        first += [_remote(srcs[n + w], outs[n + w].at[me], send_sems, recv_sems, 6 * n + w * 3 + k, (px, py, c))
                  for k, (px, py) in enumerate(chips) for w in range(m)]
        for cp in first:
            cp.start()
        passed = []
        for k, (px, py) in enumerate(chips):
            for g in range(n):
                slot = outs[g].at[2 * px + py, c]
                _remote(slot, slot, send_sems, recv_sems, g * 3 + k, (px, py, c)).wait_recv()
                cp = _remote(slot, slot, send_sems, recv_sems, 3 * n + g * 3 + k, sibling)
                cp.start()
                passed.append(cp)
        for k, (px, py) in enumerate(chips):
            for w in range(m):
                slot = outs[n + w].at[2 * px + py]
                _remote(slot, slot, send_sems, recv_sems, 6 * n + w * 3 + k, (px, py, c)).wait_recv()
            for g in range(n):
                slot = outs[g].at[2 * px + py, 1 - c]
                _remote(slot, slot, send_sems, recv_sems, 3 * n + g * 3 + k, sibling).wait_recv()
        for cp in first + passed:
            cp.wait_send()

    pl.kernel(launch, mesh=plsc.ScalarSubcoreMesh(axis_name="sequencer", num_cores=1), name=name,
              scratch_types=(pltpu.SemaphoreType.DMA((n_sem,)), pltpu.SemaphoreType.DMA((n_sem,))),
              compiler_params=pltpu.CompilerParams(collective_id=collective_id))()
    me = 2 * lax.axis_index("x") + lax.axis_index("y")
    res = [lax.dynamic_update_slice_in_dim(o[...], a[None], me, axis=0) for o, a in zip(outs, arrays)]
    return res[:n], res[n:]


def _swap_halves(groups):
    n = len(groups)

    def body(*refs):
        srcs, outs, send_sems, recv_sems = refs[:n], refs[n:2 * n], refs[2 * n], refs[2 * n + 1]
        x, y, c, _ = _place()
        cps = [_remote(srcs[g].at[1 - c], outs[g], send_sems, recv_sems, g, (x, y, 1 - c)) for g in range(n)]
        for cp in cps:
            cp.start()
        for cp in cps:
            cp.wait()

    return pl.pallas_call(
        body, name="comm_swap_halves", in_specs=[HBM_SPEC] * n, out_specs=[HBM_SPEC] * n,
        out_shape=[jax.ShapeDtypeStruct(g.shape[1:], g.dtype) for g in groups],
        scratch_shapes=[pltpu.SemaphoreType.DMA((n,)), pltpu.SemaphoreType.DMA((n,))],
    )(*groups)


def _scatter_partials(groups):
    n = len(groups)

    def body(*refs):
        srcs, outs = refs[:n], refs[n:2 * n]
        send_sems, recv_sems = refs[2 * n:]
        x, y, c, chips = _place()
        sends = [_remote(srcs[g].at[2 * px + py], outs[g].at[k], send_sems, recv_sems, g * 3 + k, (px, py, c))
                 for k, (px, py) in enumerate(chips) for g in range(n)]
        for cp in sends:
            cp.start()
        for k, (px, py) in enumerate(chips):
            for g in range(n):
                slot = outs[g].at[k]
                _remote(slot, slot, send_sems, recv_sems, g * 3 + k, (px, py, c)).wait_recv()
        for cp in sends:
            cp.wait_send()

    return pl.pallas_call(
        body, name="comm_scatter_partials", in_specs=[HBM_SPEC] * n, out_specs=[HBM_SPEC] * n,
        out_shape=[jax.ShapeDtypeStruct((3,) + g.shape[1:], g.dtype) for g in groups],
        scratch_shapes=[pltpu.SemaphoreType.DMA((3 * n,)), pltpu.SemaphoreType.DMA((3 * n,))],
    )(*groups)


def _share_reduced(groups):
    n = len(groups)

    def body(*refs):
        srcs, outs = refs[:n], refs[n:2 * n]
        send_sems, recv_sems = refs[2 * n:]
        x, y, c, _ = _place()
        cps = [_remote(srcs[g], outs[g], send_sems, recv_sems, g, (x, y, 1 - c)) for g in range(n)]
        for cp in cps:
            cp.start()
        for cp in cps:
            cp.wait()

    return pl.pallas_call(
        body, name="comm_share_reduced", in_specs=[HBM_SPEC] * n, out_specs=[HBM_SPEC] * n,
        out_shape=[jax.ShapeDtypeStruct(g.shape, g.dtype) for g in groups],
        scratch_shapes=[pltpu.SemaphoreType.DMA((n,)), pltpu.SemaphoreType.DMA((n,))],
    )(*groups)


def _allreduce_small(v):
    n_dev = 2 * N_CHIPS

    def body(src, out, buf, send_sems, recv_sems):
        x, y, c, _ = _place()
        me = 4 * x + 2 * y + c
        buf[me] = src[...]
        flips = [(k >> 2 & 1, k >> 1 & 1, k & 1) for k in range(1, n_dev)]
        sends = []
        for k, (fx, fy, fc) in enumerate(flips):
            to = ((1 - x) if fx else x, (1 - y) if fy else y, (1 - c) if fc else c)
            cp = _remote(src, buf.at[me], send_sems, recv_sems, k, to)
            cp.start()
            sends.append((cp, to))
        for k, (cp, to) in enumerate(sends):
            slot = buf.at[4 * to[0] + 2 * to[1] + to[2]]
            _remote(slot, slot, send_sems, recv_sems, k, to).wait_recv()
        for cp, _ in sends:
            cp.wait_send()
        acc = buf[0]
        for d in range(1, n_dev):
            acc = acc + buf[d]
        out[...] = acc

    vm = pl.BlockSpec(memory_space=pltpu.VMEM)
    return pl.pallas_call(
        body, name="comm_allreduce_small", in_specs=[vm], out_specs=vm, out_shape=jax.ShapeDtypeStruct(v.shape, F32),
        scratch_shapes=[pltpu.VMEM((n_dev,) + v.shape, F32), pltpu.SemaphoreType.DMA((n_dev - 1,)),
                        pltpu.SemaphoreType.DMA((n_dev - 1,))],
    )(v)


def _row_tile(rows, cap=320):
    return max(t for t in range(16, cap + 1, 16) if rows % t == 0)


def _add_sibling(mine, theirs, core, name):
    _, n, R, C = mine.shape
    t = _row_tile(R)

    def body(core_ref, a_ref, b_ref, o_ref, ob_ref):
        tot = a_ref[...] + b_ref[...]
        o_ref[...] = tot
        ob_ref[...] = tot.astype(BF16)

    sp = pl.BlockSpec((None, t, C), lambda k, i, core_ref: (k, i, 0))
    grid_spec = pltpu.PrefetchScalarGridSpec(
        num_scalar_prefetch=1, grid=(n, R // t),
        in_specs=[pl.BlockSpec((None, None, t, C), lambda k, i, core_ref: (core_ref[0], k, i, 0)), sp], out_specs=[sp, sp])
    return pl.pallas_call(body, name=name, grid_spec=grid_spec,
                          out_shape=[jax.ShapeDtypeStruct((n, R, C), F32), jax.ShapeDtypeStruct((n, R, C), BF16)],
                          compiler_params=_params("parallel", "parallel"))(core, mine, theirs)


def _add_chips(received, own, chip, name):
    n, R, C = received.shape
    t = _row_tile(R)

    def body(chip_ref, r_ref, o_ref, out_ref):
        acc = o_ref[...]
        for k in range(n):
            acc = acc + r_ref[k].astype(F32)
        out_ref[...] = acc

    grid_spec = pltpu.PrefetchScalarGridSpec(
        num_scalar_prefetch=1, grid=(R // t,),
        in_specs=[pl.BlockSpec((n, t, C), lambda i, chip_ref: (0, i, 0)), pl.BlockSpec((None, t, C), lambda i, chip_ref: (chip_ref[0], i, 0))],
        out_specs=pl.BlockSpec((t, C), lambda i, chip_ref: (i, 0)))
    return pl.pallas_call(body, name=name, grid_spec=grid_spec, out_shape=jax.ShapeDtypeStruct((R, C), F32),
                          compiler_params=_params("parallel"))(chip, received, own)


def _adamw(w, g, m, v, name, g_row0=0):
    R, C = w.shape
    t = math.gcd(math.gcd(R, g_row0), 128) if R % 8 == 0 else R
    assert g_row0 % t == 0
    c1 = 1.0 - ADAM_B1 ** ADAM_STEP
    c2 = 1.0 - ADAM_B2 ** ADAM_STEP

    def body(w_ref, g_ref, m_ref, v_ref, go_ref, d_ref, nm_ref, nv_ref):
        gv = g_ref[...]
        nm = ADAM_B1 * m_ref[...] + (1.0 - ADAM_B1) * gv
        nv = ADAM_B2 * v_ref[...] + (1.0 - ADAM_B2) * (gv * gv)
        go_ref[...] = gv
        d_ref[...] = -ADAM_LR * ((nm / c1) / (jnp.sqrt(nv / c2) + ADAM_EPS) + ADAM_WD * w_ref[...])
        nm_ref[...] = nm
        nv_ref[...] = nv

    sp = pl.BlockSpec((t, C), lambda i: (i, 0))
    g_sp = pl.BlockSpec((t, C), lambda i: (i + g_row0 // t, 0))
    return pl.pallas_call(body, name=name, grid=(R // t,), in_specs=[sp, g_sp, sp, sp], out_specs=[sp] * 4,
                          out_shape=[jax.ShapeDtypeStruct((R, C), F32)] * 4, compiler_params=_params("parallel"))(w, g, m, v)


def kernel(x, mem, positions, g_pre_mix, w_in, b_gate, mla_q_norm, w_uq, mla_kv_norm, w_ukv, g_mem, w_mem_kv, w_br_mla, w_br_dil, w_br_mem, w_o, g_post_mix, g_pre_ffn, w_ffn_up, conv_w, conv_b, w_ffn_down, g_post_ffn, loss_target, m_g_pre_mix, m_w_in, m_b_gate, m_mla_q_norm, m_w_uq, m_mla_kv_norm, m_w_ukv, m_g_mem, m_w_mem_kv, m_w_br_mla, m_w_br_dil, m_w_br_mem, m_w_o, m_g_post_mix, m_g_pre_ffn, m_w_ffn_up, m_conv_w, m_conv_b, m_w_ffn_down, m_g_post_ffn, v_g_pre_mix, v_w_in, v_b_gate, v_mla_q_norm, v_w_uq, v_mla_kv_norm, v_w_ukv, v_g_mem, v_w_mem_kv, v_w_br_mla, v_w_br_dil, v_w_br_mem, v_w_o, v_g_post_mix, v_g_pre_ffn, v_w_ffn_up, v_conv_w, v_conv_b, v_w_ffn_down, v_g_post_ffn):
    w_args = (g_pre_mix, w_in, b_gate, mla_q_norm, w_uq, mla_kv_norm, w_ukv, g_mem, w_mem_kv, w_br_mla, w_br_dil, w_br_mem, w_o,
              g_post_mix, g_pre_ffn, w_ffn_up, conv_w, conv_b, w_ffn_down, g_post_ffn)
    m_args = (m_g_pre_mix, m_w_in, m_b_gate, m_mla_q_norm, m_w_uq, m_mla_kv_norm, m_w_ukv, m_g_mem, m_w_mem_kv, m_w_br_mla,
              m_w_br_dil, m_w_br_mem, m_w_o, m_g_post_mix, m_g_pre_ffn, m_w_ffn_up, m_conv_w, m_conv_b, m_w_ffn_down, m_g_post_ffn)
    v_args = (v_g_pre_mix, v_w_in, v_b_gate, v_mla_q_norm, v_w_uq, v_mla_kv_norm, v_w_ukv, v_g_mem, v_w_mem_kv, v_w_br_mla,
              v_w_br_dil, v_w_br_mem, v_w_o, v_g_post_mix, v_g_pre_ffn, v_w_ffn_up, v_conv_w, v_conv_b, v_w_ffn_down, v_g_post_ffn)
    sharded = {name for grp in GROUPS for name, _, _ in grp} | {CONV_W[0]}

    def local(a, name):
        return a[0] if name in sharded else a

    w = {n: local(a, n) for n, a in zip(WEIGHTS, w_args)}
    m = {n: local(a, n) for n, a in zip(WEIGHTS, m_args)}
    v = {n: local(a, n) for n, a in zip(WEIGHTS, v_args)}

    mine = _my_weight_groups(w)
    first, _ = _gather_weights(mine[:1], [], "comm_gather_w_in", 0)
    rest, (conv_all,) = _gather_weights(mine[1:], [w[CONV_W[0]]], "comm_gather_rest", 1)
    full = _full_weights(first + rest, conv_all)
    full.update({name: w[name] for name, _ in REPLICATED})

    loss_local, grad_x, G = _local_step(x[0], mem[0], positions, loss_target[0], full)

    core = lax.axis_index("c").astype(jnp.int32).reshape(1)
    mine = _grad_groups(G)
    theirs = _swap_halves(mine)
    chip = (2 * lax.axis_index("x") + lax.axis_index("y")).astype(jnp.int32).reshape(1)
    partial = [_add_sibling(a, b, core, f"add_sibling_{i}") for i, (a, b) in enumerate(zip(mine, theirs))]
    received = _scatter_partials([p16 for _, p16 in partial])
    reduced = [_add_chips(r, p32, chip, f"add_chips_{i}") for i, (r, (p32, _)) in enumerate(zip(received, partial))]
    first = lax.axis_index("c") == 0
    shard_groups = [jnp.concatenate([jnp.where(first, a, b), jnp.where(first, b, a)], axis=0)
                    for a, b in zip(reduced, _share_reduced(reduced))]
    small = _allreduce_small(_pack_small(G, G[CONV_W[0]], loss_local))
    flat = small.reshape(-1)
    loss = flat[LOSS_AT]
    conv_g = flat[CONV_AT:LOSS_AT].reshape(CONV_W[1])
    conv_cols = CONV_W[1][1] // N_CHIPS
    conv_g = lax.dynamic_slice_in_dim(conv_g, (2 * lax.axis_index("x") + lax.axis_index("y")) * conv_cols, conv_cols, axis=1)

    grads, deltas, new_m, new_v = {}, {}, {}, {}
    for grp, g_all in zip(GROUPS, shard_groups):
        off = 0
        for name, _, _ in grp:
            grads[name], deltas[name], new_m[name], new_v[name] = _adamw(w[name], g_all, m[name], v[name], "adamw_" + name, off)
            off += w[name].shape[0]
    name = CONV_W[0]
    grads[name], deltas[name], new_m[name], new_v[name] = _adamw(w[name], conv_g, m[name], v[name], "adamw_" + name)
    packed = _adamw(_pack_small(w), small, _pack_small(m), _pack_small(v), "adamw_small")
    for dst, packed_small in zip((grads, deltas, new_m, new_v), packed):
        dst.update(_unpack_small(packed_small))

    def out(d, name):
        return d[name][None] if name in sharded else d[name]

    return (loss, grad_x[None], *[out(grads, n) for n in WEIGHTS], *[out(deltas, n) for n in WEIGHTS],
            *[out(new_m, n) for n in WEIGHTS], *[out(new_v, n) for n in WEIGHTS])
```

```python
import functools
import math

import jax
import jax.numpy as jnp
from jax import lax
from jax.experimental import pallas as pl
from jax.experimental.pallas import tpu as pltpu
from jax.experimental.pallas import tpu_sc as plsc

F32 = jnp.float32
BF16 = jnp.bfloat16

D_MODEL = 1024
N_MEM = 256
RMS_EPS = 1e-6
NEG_INF = -1e30
MLA_HEADS = 8
MLA_NOPE = 64
MLA_ROPE = 32
MLA_QK = 96
Q_RANK = 384
KV_RANK = 256
ROPE_THETA = 10000.0
DIL_PAIRS = ((128, 1), (512, 4), (2048, 16))
DIL_GROUPS = 3
DIL_HPG = 4
DIL_HEADS = 12
DIL_W = 512
MEM_HEADS = 4
MEM_W = 512
D_FF = 2816
OFF_Q = 384
OFF_KV = 640
OFF_KR = 672
OFF_DIL = 5280
OFF_MEMQ = 5792
D_IN = 8864
ADAM_LR = 0.001
ADAM_B1 = 0.9
ADAM_B2 = 0.999
ADAM_EPS = 1e-08
ADAM_WD = 0.01
ADAM_STEP = 10

LANES = 128
VMEM_LIMIT = 56 * 1024 * 1024

N_CHIPS = 4
ROW_TILE = 256

NN = (((1,), (0,)), ((), ()))
NT = (((1,), (1,)), ((), ()))
TN = (((0,), (0,)), ((), ()))


def _params(*sem):
    return pltpu.CompilerParams(dimension_semantics=sem, vmem_limit_bytes=VMEM_LIMIT)


def _full(shape):
    return pl.BlockSpec(shape, lambda *_: (0,) * len(shape))


def _matmul(a, b, *, mode="nn", out_dtype=F32, tm=1024, tn=1024, tk=None, add=None, scale=None, name):
    if mode == "nn":
        (M, K), N = a.shape, b.shape[1]
    elif mode == "nt":
        (M, K), N = a.shape, b.shape[0]
    else:
        (K, M), N = a.shape, b.shape[1]
    tm, tn = min(tm, M), min(tn, N)
    tk = K if tk is None else min(tk, K)
    assert M % tm == 0 and N % tn == 0 and K % tk == 0, (name, M, N, K, tm, tn, tk)
    a_spec = pl.BlockSpec((tk, tm), lambda i, j, k: (k, i)) if mode == "tn" else pl.BlockSpec((tm, tk), lambda i, j, k: (i, k))
    b_spec = pl.BlockSpec((tn, tk), lambda i, j, k: (j, k)) if mode == "nt" else pl.BlockSpec((tk, tn), lambda i, j, k: (k, j))
    o_spec = pl.BlockSpec((tm, tn), lambda i, j, k: (i, j))
    return _matmul_blocks(a, b, mode=mode, grid=(M // tm, N // tn, K // tk), a_spec=a_spec, b_spec=b_spec, o_spec=o_spec,
                          out_shape=(M, N), out_dtype=out_dtype, add=add,
                          scale=None if scale is None else (lambda j: j < scale[0], scale[1]), name=name)


def _matmul_blocks(a, b, *, mode, grid, a_spec, b_spec, o_spec, out_shape, out_dtype=F32, add=None, scale=None, name):
    nk = grid[2]
    dims = {"nn": NN, "nt": NT, "tn": TN}[mode]
    tm, tn = o_spec.block_shape
    has_add = add is not None

    def body(*refs):
        a_ref, b_ref = refs[0], refs[1]
        c_ref = refs[2] if has_add else None
        o_ref = refs[3] if has_add else refs[2]
        part = lax.dot_general(a_ref[...].astype(BF16), b_ref[...].astype(BF16), dims, preferred_element_type=F32)
        if scale is not None:
            assert nk == 1 and not has_add
            part = part * jnp.where(scale[0](pl.program_id(1)), scale[1], 1.0)
        if nk == 1:
            if has_add:
                part = part + c_ref[...]
            o_ref[...] = part.astype(out_dtype)
        else:
            acc = refs[-1]
            k = pl.program_id(2)

            @pl.when(k == 0)
            def _():
                acc[...] = part

            @pl.when(k > 0)
            def _():
                acc[...] += part

            @pl.when(k == nk - 1)
            def _():
                r = acc[...]
                if has_add:
                    r = r + c_ref[...]
                o_ref[...] = r.astype(out_dtype)

    in_specs = [a_spec, b_spec] + ([o_spec] if has_add else [])
    args = (a, b) + ((add,) if has_add else ())
    return pl.pallas_call(
        body, name=name, grid=grid, in_specs=in_specs, out_specs=o_spec,
        out_shape=jax.ShapeDtypeStruct(out_shape, out_dtype),
        scratch_shapes=[pltpu.VMEM((tm, tn), F32)] if nk > 1 else [],
        compiler_params=_params("parallel", "parallel", "arbitrary"),
    )(*args)


def _proj_view(h_view, w, d, q_scale, name):
    L, K = h_view.shape[0], h_view.shape[1] // d
    N = w.shape[1]
    tn = N // 3
    tm = min(1024, L)
    return _matmul_blocks(
        h_view, w, mode="nn", grid=(L // tm, 3 * d, 1), a_spec=pl.BlockSpec((tm, K), lambda i, j, k: (i, j // 3)),
        b_spec=pl.BlockSpec((K, tn), lambda i, j, k: (0, j % 3)), o_spec=pl.BlockSpec((tm, tn), lambda i, j, k: (i, j)),
        out_shape=(L, d * N), out_dtype=BF16, scale=(lambda j: j % 3 == 0, q_scale), name=name)


def _dh_view(dp_view, w, d, add, name):
    L, N = dp_view.shape[0], dp_view.shape[1] // d
    K = w.shape[0]
    tm = min(1024, L)
    return _matmul_blocks(
        dp_view, w, mode="nt", grid=(L // tm, d, 1), a_spec=pl.BlockSpec((tm, N), lambda i, j, k: (i, j)),
        b_spec=pl.BlockSpec((K, N), lambda i, j, k: (0, 0)), o_spec=pl.BlockSpec((tm, K), lambda i, j, k: (i, j)),
        out_shape=(L, d * K), add=add, name=name)


def _dw_view(h_view, dp_view, d, name):
    L, K = h_view.shape[0], h_view.shape[1] // d
    N = dp_view.shape[1] // d
    tk = min(1024, L)
    nl = L // tk
    return _matmul_blocks(
        h_view, dp_view, mode="tn", grid=(1, 1, d * nl), a_spec=pl.BlockSpec((tk, K), lambda i, j, k: (k % nl, k // nl)),
        b_spec=pl.BlockSpec((tk, N), lambda i, j, k: (k % nl, k // nl)), o_spec=pl.BlockSpec((K, N), lambda i, j, k: (0, 0)),
        out_shape=(K, N), name=name)


def _rms_fwd_val(x, g):
    r = lax.rsqrt(jnp.mean(x * x, axis=-1, keepdims=True) + RMS_EPS)
    return (x * r) * g


def _rms_bwd_val(dy, x, g):
    r = lax.rsqrt(jnp.mean(x * x, axis=-1, keepdims=True) + RMS_EPS)
    xn = x * r
    gdy = g * dy
    dx = r * (gdy - xn * jnp.mean(gdy * xn, axis=-1, keepdims=True))
    return dx, dy * xn


def _rope_tables(pos, invf, inverse):
    ang = pos * invf
    cos, sin = jnp.cos(ang), jnp.sin(ang)
    lane = lax.broadcasted_iota(jnp.int32, ang.shape, 1)
    first = (lane >= MLA_NOPE) & (lane < MLA_NOPE + MLA_ROPE // 2)
    second = (lane >= MLA_NOPE + MLA_ROPE // 2) & (lane < MLA_QK)
    sgn = -1.0 if inverse else 1.0
    sa = jnp.where(first, -sgn * sin, 0.0)
    sb = jnp.where(second, sgn * sin, 0.0)
    return cos, sa, sb


def _rope_val(x, cos, sa, sb):
    half = MLA_ROPE // 2
    return x * cos + pltpu.roll(x, LANES - half, 1) * sa + pltpu.roll(x, half, 1) * sb


def _head_sum_bcast(v, n_heads):
    parts = []
    for h in range(n_heads):
        s = jnp.sum(v[:, h * LANES:(h + 1) * LANES], axis=1, keepdims=True)
        parts.append(jnp.broadcast_to(s, (v.shape[0], LANES)))
    return parts


def _row_spec(t, w):
    return pl.BlockSpec((t, w), lambda i: (i, 0))


def _acc_spec(w, rows=1):
    return pl.BlockSpec((rows, w), lambda i: (0, 0))


def _rmsnorm(x, g, out_dtype, name):
    S, W = x.shape
    t = min(ROW_TILE, S)

    def body(x_ref, g_ref, o_ref):
        o_ref[...] = _rms_fwd_val(x_ref[...], g_ref[...]).astype(out_dtype)

    return pl.pallas_call(body, name=name, grid=(S // t,), in_specs=[_row_spec(t, W), _acc_spec(W)],
                          out_specs=_row_spec(t, W), out_shape=jax.ShapeDtypeStruct((S, W), out_dtype),
                          compiler_params=_params("parallel"))(x, g)


def _mla_prep(proj_a, pos, invf, q_norm, kv_norm):
    S = proj_a.shape[0]
    t = ROW_TILE

    def body(a_ref, pos_ref, invf_ref, qn_ref, kvn_ref, cq_ref, ckv_ref, kpe_ref):
        a = a_ref[...]
        cq_ref[...] = _rms_fwd_val(a[:, 0:Q_RANK], qn_ref[...]).astype(BF16)
        ckv_ref[...] = _rms_fwd_val(a[:, Q_RANK + LANES:], kvn_ref[...]).astype(BF16)
        cos, sa, sb = _rope_tables(pos_ref[...], invf_ref[...], False)
        kpe_ref[...] = _rope_val(a[:, Q_RANK:Q_RANK + LANES], cos, sa, sb)

    return pl.pallas_call(
        body, name="mla_prep", grid=(S // t,),
        in_specs=[_row_spec(t, 768), _row_spec(t, 1), _acc_spec(LANES), _acc_spec(Q_RANK), _acc_spec(KV_RANK)],
        out_specs=[_row_spec(t, Q_RANK), _row_spec(t, KV_RANK), _row_spec(t, LANES)],
        out_shape=[jax.ShapeDtypeStruct((S, Q_RANK), BF16), jax.ShapeDtypeStruct((S, KV_RANK), BF16),
                   jax.ShapeDtypeStruct((S, LANES), F32)],
        compiler_params=_params("parallel"))(proj_a, pos, invf, q_norm, kv_norm)


def _qk_final(q_pre, k_pre, kpe, pos, invf):
    S, W = q_pre.shape
    t = ROW_TILE
    scale = MLA_QK ** -0.5

    def body(q_ref, k_ref, kpe_ref, pos_ref, invf_ref, qo_ref, ko_ref):
        cos, sa, sb = _rope_tables(pos_ref[...], invf_ref[...], False)
        kpe_v = kpe_ref[...]
        for h in range(MLA_HEADS):
            sl = slice(h * LANES, (h + 1) * LANES)
            qo_ref[:, sl] = (_rope_val(q_ref[:, sl], cos, sa, sb) * scale).astype(BF16)
            ko_ref[:, sl] = (k_ref[:, sl] + kpe_v).astype(BF16)

    return pl.pallas_call(
        body, name="qk_final", grid=(S // t,),
        in_specs=[_row_spec(t, W), _row_spec(t, W), _row_spec(t, LANES), _row_spec(t, 1), _acc_spec(LANES)],
        out_specs=[_row_spec(t, W), _row_spec(t, W)],
        out_shape=[jax.ShapeDtypeStruct((S, W), BF16)] * 2,
        compiler_params=_params("parallel"))(q_pre, k_pre, kpe, pos, invf)


def _mla_bwd_prep(dq, dk, pos, invf):
    S, W = dq.shape
    t = ROW_TILE
    scale = MLA_QK ** -0.5

    def body(dq_ref, dk_ref, pos_ref, invf_ref, dqp_ref, dkpe_ref):
        cos, sa, sb = _rope_tables(pos_ref[...], invf_ref[...], True)
        tot = jnp.zeros((t, LANES), F32)
        for h in range(MLA_HEADS):
            sl = slice(h * LANES, (h + 1) * LANES)
            dqp_ref[:, sl] = _rope_val(dq_ref[:, sl] * scale, cos, sa, sb).astype(BF16)
            tot = tot + dk_ref[:, sl]
        lane = lax.broadcasted_iota(jnp.int32, tot.shape, 1)
        tot = jnp.where((lane >= MLA_NOPE) & (lane < MLA_QK), tot, 0.0)
        dkpe_ref[...] = _rope_val(tot, cos, sa, sb)

    return pl.pallas_call(
        body, name="mla_bwd_prep", grid=(S // t,),
        in_specs=[_row_spec(t, W), _row_spec(t, W), _row_spec(t, 1), _acc_spec(LANES)],
        out_specs=[_row_spec(t, W), _row_spec(t, LANES)],
        out_shape=[jax.ShapeDtypeStruct((S, W), BF16), jax.ShapeDtypeStruct((S, LANES), F32)],
        compiler_params=_params("parallel"))(dq, dk, pos, invf)


def _mla_norm_bwd(dcq, dckv_a, dckv_b, dkpe, proj_a, q_norm, kv_norm):
    S = proj_a.shape[0]
    t = ROW_TILE

    def body(dcq_ref, da_ref, db_ref, dkpe_ref, a_ref, qn_ref, kvn_ref, o_ref, dqn_ref, dkvn_ref):
        i = pl.program_id(0)
        a = a_ref[...]
        dxq, gq = _rms_bwd_val(dcq_ref[...], a[:, 0:Q_RANK], qn_ref[...])
        dxkv, gkv = _rms_bwd_val(da_ref[...] + db_ref[...], a[:, Q_RANK + LANES:], kvn_ref[...])
        o_ref[:, 0:Q_RANK] = dxq.astype(BF16)
        o_ref[:, Q_RANK:Q_RANK + LANES] = dkpe_ref[...].astype(BF16)
        o_ref[:, Q_RANK + LANES:] = dxkv.astype(BF16)

        @pl.when(i == 0)
        def _():
            dqn_ref[...] = jnp.zeros_like(dqn_ref)
            dkvn_ref[...] = jnp.zeros_like(dkvn_ref)

        dqn_ref[...] += jnp.sum(gq, axis=0, keepdims=True)
        dkvn_ref[...] += jnp.sum(gkv, axis=0, keepdims=True)

    return pl.pallas_call(
        body, name="mla_norm_bwd", grid=(S // t,),
        in_specs=[_row_spec(t, Q_RANK), _row_spec(t, KV_RANK), _row_spec(t, KV_RANK), _row_spec(t, LANES),
                  _row_spec(t, 768), _acc_spec(Q_RANK), _acc_spec(KV_RANK)],
        out_specs=[_row_spec(t, 768), _acc_spec(Q_RANK), _acc_spec(KV_RANK)],
        out_shape=[jax.ShapeDtypeStruct((S, 768), BF16), jax.ShapeDtypeStruct((1, Q_RANK), F32),
                   jax.ShapeDtypeStruct((1, KV_RANK), F32)],
        compiler_params=_params("arbitrary"))(dcq, dckv_a, dckv_b, dkpe, proj_a, q_norm, kv_norm)


DILATIONS = tuple(d for _, d in DIL_PAIRS)


def _to_tokens(view, scr, d):
    if d == 1:
        return view
    n, w = view.shape[0], view.shape[1] // d
    for r in range(d):
        for c in range(w // LANES):
            scr[pl.ds(c, 1), pl.ds(r, n, stride=d), :] = view[:, r * w + c * LANES:r * w + (c + 1) * LANES][None]
    return jnp.concatenate([scr[c] for c in range(w // LANES)], axis=1)


def _from_tokens(tok, scr, d, out_ref):
    if d == 1:
        out_ref[...] = tok.astype(out_ref.dtype)
        return
    n, w = tok.shape[0] // d, tok.shape[1]
    for c in range(w // LANES):
        scr[c] = tok[:, c * LANES:(c + 1) * LANES]
    for r in range(d):
        for c in range(w // LANES):
            out_ref[:, r * w + c * LANES:r * w + (c + 1) * LANES] = scr[pl.ds(c, 1), pl.ds(r, n, stride=d), :][0].astype(out_ref.dtype)


def _token_scratch(t, w):
    return pltpu.VMEM((w // LANES, t, LANES), F32)


def _view_spec(t, d):
    return pl.BlockSpec((t // d, d * DIL_W), lambda i: (i, 0))


def _mix_weights(ls):
    m = jnp.maximum(jnp.maximum(ls[0], ls[1]), ls[2])
    es = [jnp.exp(l - m) for l in ls]
    den = es[0] + es[1] + es[2]
    return [e / den for e in es]


def _dil_mix(o_list, lse_list):
    S = o_list[0].shape[0] * DILATIONS[0]
    t = ROW_TILE
    specs = [_view_spec(t, d) for d in DILATIONS]

    def body(o0, o1, o2, l0, l1, l2, y_ref, *scr):
        os_ = [_to_tokens(r[...], scr[g], d) for g, (r, d) in enumerate(zip((o0, o1, o2), DILATIONS))]
        ws = _mix_weights([_to_tokens(r[...], scr[3 + g], d) for g, (r, d) in enumerate(zip((l0, l1, l2), DILATIONS))])
        y_ref[...] = (ws[0] * os_[0] + ws[1] * os_[1] + ws[2] * os_[2]).astype(BF16)

    return pl.pallas_call(
        body, name="dil_mix", grid=(S // t,), in_specs=specs * 2, out_specs=_row_spec(t, DIL_W),
        out_shape=jax.ShapeDtypeStruct((S, DIL_W), BF16), scratch_shapes=[_token_scratch(t, DIL_W)] * 6,
        compiler_params=_params("parallel"))(*o_list, *lse_list)


def _dil_mix_bwd(dy, o_list, lse_list):
    S = dy.shape[0]
    t = ROW_TILE
    specs = [_view_spec(t, d) for d in DILATIONS]

    def body(dy_ref, o0, o1, o2, l0, l1, l2, d0, d1, d2, e0, e1, e2, *scr):
        os_ = [_to_tokens(r[...], scr[g], d) for g, (r, d) in enumerate(zip((o0, o1, o2), DILATIONS))]
        ws = _mix_weights([_to_tokens(r[...], scr[3 + g], d) for g, (r, d) in enumerate(zip((l0, l1, l2), DILATIONS))])
        dyv = dy_ref[...]
        y = ws[0] * os_[0] + ws[1] * os_[1] + ws[2] * os_[2]
        b = jnp.concatenate(_head_sum_bcast(dyv * y, DIL_HPG), axis=1)
        for g, (w, d_ref, e_ref, d) in enumerate(zip(ws, (d0, d1, d2), (e0, e1, e2), DILATIONS)):
            _from_tokens(w * dyv, scr[6 + g], d, d_ref)
            _from_tokens(w * b, scr[9 + g], d, e_ref)

    shapes = [(S // d, d * DIL_W) for d in DILATIONS]
    return pl.pallas_call(
        body, name="dil_mix_bwd", grid=(S // t,), in_specs=[_row_spec(t, DIL_W)] + specs * 2, out_specs=specs * 2,
        out_shape=[jax.ShapeDtypeStruct(s, BF16) for s in shapes] + [jax.ShapeDtypeStruct(s, F32) for s in shapes],
        scratch_shapes=[_token_scratch(t, DIL_W)] * 12, compiler_params=_params("parallel"))(dy, *o_list, *lse_list)


def _delta(do, o, n_heads, name):
    S, W = do.shape
    t = ROW_TILE

    def body(do_ref, o_ref, d_ref):
        prod = do_ref[...].astype(F32) * o_ref[...].astype(F32)
        d_ref[...] = jnp.concatenate(_head_sum_bcast(prod, n_heads), axis=1)

    return pl.pallas_call(body, name=name, grid=(S // t,), in_specs=[_row_spec(t, W)] * 2, out_specs=_row_spec(t, W),
                          out_shape=jax.ShapeDtypeStruct((S, W), F32), compiler_params=_params("parallel"))(do, o)


def _merge(proj_g, b_gate, b_list):
    S = proj_g.shape[0]
    t = ROW_TILE

    def body(g_ref, b_ref, y0, y1, y2, o_ref):
        acc = jnp.zeros((t, D_MODEL), F32)
        for i, y in enumerate((y0, y1, y2)):
            sl = slice(i * D_MODEL, (i + 1) * D_MODEL)
            acc = acc + jax.nn.sigmoid(g_ref[:, sl] + b_ref[:, sl]) * y[...]
        o_ref[...] = acc.astype(BF16)

    return pl.pallas_call(
        body, name="merge", grid=(S // t,),
        in_specs=[_row_spec(t, 3 * D_MODEL), _acc_spec(3 * D_MODEL)] + [_row_spec(t, D_MODEL)] * 3,
        out_specs=_row_spec(t, D_MODEL), out_shape=jax.ShapeDtypeStruct((S, D_MODEL), BF16),
        compiler_params=_params("parallel"))(proj_g, b_gate, *b_list)


def _merge_bwd(dmerged, proj_g, b_gate, b_list):
    S = proj_g.shape[0]
    t = ROW_TILE

    def body(dm_ref, g_ref, b_ref, y0, y1, y2, d0, d1, d2, dz_ref, db_ref):
        i = pl.program_id(0)

        @pl.when(i == 0)
        def _():
            db_ref[...] = jnp.zeros_like(db_ref)

        dm = dm_ref[...]
        for k, (y, d_ref) in enumerate(zip((y0, y1, y2), (d0, d1, d2))):
            sl = slice(k * D_MODEL, (k + 1) * D_MODEL)
            s = jax.nn.sigmoid(g_ref[:, sl] + b_ref[:, sl])
            d_ref[...] = (s * dm).astype(BF16)
            dz = dm * y[...] * (s * (1.0 - s))
            dz_ref[:, sl] = dz.astype(BF16)
            db_ref[:, sl] += jnp.sum(dz, axis=0, keepdims=True)

    return pl.pallas_call(
        body, name="merge_bwd", grid=(S // t,),
        in_specs=[_row_spec(t, D_MODEL), _row_spec(t, 3 * D_MODEL), _acc_spec(3 * D_MODEL)] + [_row_spec(t, D_MODEL)] * 3,
        out_specs=[_row_spec(t, D_MODEL)] * 3 + [_row_spec(t, 3 * D_MODEL), _acc_spec(3 * D_MODEL)],
        out_shape=[jax.ShapeDtypeStruct((S, D_MODEL), BF16)] * 3
        + [jax.ShapeDtypeStruct((S, 3 * D_MODEL), BF16), jax.ShapeDtypeStruct((1, 3 * D_MODEL), F32)],
        compiler_params=_params("arbitrary"))(dmerged, proj_g, b_gate, *b_list)


def _norm2(o, x, g_post, g_pre):
    S = x.shape[0]
    t = ROW_TILE

    def body(o_ref, x_ref, gp_ref, gf_ref, x1_ref, h2_ref):
        x1 = x_ref[...] + _rms_fwd_val(o_ref[...], gp_ref[...])
        x1_ref[...] = x1
        h2_ref[...] = _rms_fwd_val(x1, gf_ref[...]).astype(BF16)

    return pl.pallas_call(
        body, name="norm2", grid=(S // t,),
        in_specs=[_row_spec(t, D_MODEL)] * 2 + [_acc_spec(D_MODEL)] * 2, out_specs=[_row_spec(t, D_MODEL)] * 2,
        out_shape=[jax.ShapeDtypeStruct((S, D_MODEL), F32), jax.ShapeDtypeStruct((S, D_MODEL), BF16)],
        compiler_params=_params("parallel"))(o, x, g_post, g_pre)


def _norm2_bwd(dx2, dh2, x1, o, g_pre, g_post):
    S = x1.shape[0]
    t = ROW_TILE

    def body(dx2_ref, dh2_ref, x1_ref, o_ref, gf_ref, gp_ref, dx1_ref, do_ref, dgf_ref, dgp_ref):
        i = pl.program_id(0)

        @pl.when(i == 0)
        def _():
            dgf_ref[...] = jnp.zeros_like(dgf_ref)
            dgp_ref[...] = jnp.zeros_like(dgp_ref)

        d1, gf = _rms_bwd_val(dh2_ref[...], x1_ref[...], gf_ref[...])
        dx1 = dx2_ref[...] + d1
        dx1_ref[...] = dx1
        do, gp = _rms_bwd_val(dx1, o_ref[...], gp_ref[...])
        do_ref[...] = do.astype(BF16)
        dgf_ref[...] += jnp.sum(gf, axis=0, keepdims=True)
        dgp_ref[...] += jnp.sum(gp, axis=0, keepdims=True)

    return pl.pallas_call(
        body, name="norm2_bwd", grid=(S // t,),
        in_specs=[_row_spec(t, D_MODEL)] * 4 + [_acc_spec(D_MODEL)] * 2,
        out_specs=[_row_spec(t, D_MODEL)] * 2 + [_acc_spec(D_MODEL)] * 2,
        out_shape=[jax.ShapeDtypeStruct((S, D_MODEL), F32), jax.ShapeDtypeStruct((S, D_MODEL), BF16),
                   jax.ShapeDtypeStruct((1, D_MODEL), F32), jax.ShapeDtypeStruct((1, D_MODEL), F32)],
        compiler_params=_params("arbitrary"))(dx2, dh2, x1, o, g_pre, g_post)


def _norm1_bwd(dx1, dh_list, x, g):
    S = x.shape[0]
    t = ROW_TILE
    dils = (1,) + DILATIONS
    assert len(dh_list) == len(dils)

    def body(dx1_ref, *refs):
        dh_refs, (x_ref, g_ref, dx_ref, dg_ref), scr = refs[:len(dils)], refs[len(dils):len(dils) + 4], refs[len(dils) + 4:]
        i = pl.program_id(0)

        @pl.when(i == 0)
        def _():
            dg_ref[...] = jnp.zeros_like(dg_ref)

        dh = dh_refs[0][...]
        for k in range(1, len(dils)):
            dh = dh + _to_tokens(dh_refs[k][...], scr[k - 1], dils[k])
        d, gg = _rms_bwd_val(dh, x_ref[...], g_ref[...])
        dx_ref[...] = dx1_ref[...] + d
        dg_ref[...] += jnp.sum(gg, axis=0, keepdims=True)

    dh_specs = [pl.BlockSpec((t // d, d * D_MODEL), lambda i: (i, 0)) for d in dils]
    return pl.pallas_call(
        body, name="norm1_bwd", grid=(S // t,),
        in_specs=[_row_spec(t, D_MODEL)] + dh_specs + [_row_spec(t, D_MODEL), _acc_spec(D_MODEL)],
        out_specs=[_row_spec(t, D_MODEL), _acc_spec(D_MODEL)],
        out_shape=[jax.ShapeDtypeStruct((S, D_MODEL), F32), jax.ShapeDtypeStruct((1, D_MODEL), F32)],
        scratch_shapes=[_token_scratch(t, D_MODEL)] * (len(dils) - 1),
        compiler_params=_params("arbitrary"))(dx1, *dh_list, x, g)


def _gain_grad(dy, x, name):
    R, W = x.shape

    def body(dy_ref, x_ref, dg_ref):
        xv = x_ref[...]
        r = lax.rsqrt(jnp.mean(xv * xv, axis=-1, keepdims=True) + RMS_EPS)
        dg_ref[...] = jnp.sum(dy_ref[...] * (xv * r), axis=0, keepdims=True)

    return pl.pallas_call(body, name=name, grid=(1,), in_specs=[_full((R, W))] * 2, out_specs=_full((1, W)),
                          out_shape=jax.ShapeDtypeStruct((1, W), F32), compiler_params=_params("arbitrary"))(dy, x)


def _loss_head(f, x1, tgt, g):
    S = f.shape[0]
    t = ROW_TILE

    def body(f_ref, x1_ref, t_ref, g_ref, loss_ref, dx2_ref, df_ref, dg_ref):
        i = pl.program_id(0)

        @pl.when(i == 0)
        def _():
            loss_ref[...] = jnp.zeros_like(loss_ref)
            dg_ref[...] = jnp.zeros_like(dg_ref)

        fv, gv = f_ref[...], g_ref[...]
        err = x1_ref[...] + _rms_fwd_val(fv, gv) - t_ref[...]
        part = jnp.sum(jnp.mean(err * err, axis=-1, keepdims=True), axis=0, keepdims=True)
        loss_ref[...] += jnp.broadcast_to(0.5 * part, loss_ref.shape)
        dx2 = err * (1.0 / D_MODEL)
        dx2_ref[...] = dx2
        df, gg = _rms_bwd_val(dx2, fv, gv)
        df_ref[...] = df.astype(BF16)
        dg_ref[...] += jnp.sum(gg, axis=0, keepdims=True)

    return pl.pallas_call(
        body, name="loss_head", grid=(S // t,),
        in_specs=[_row_spec(t, D_MODEL)] * 3 + [_acc_spec(D_MODEL)],
        out_specs=[_acc_spec(LANES, 8), _row_spec(t, D_MODEL), _row_spec(t, D_MODEL), _acc_spec(D_MODEL)],
        out_shape=[jax.ShapeDtypeStruct((8, LANES), F32), jax.ShapeDtypeStruct((S, D_MODEL), F32),
                   jax.ShapeDtypeStruct((S, D_MODEL), BF16), jax.ShapeDtypeStruct((1, D_MODEL), F32)],
        compiler_params=_params("arbitrary"))(f, x1, tgt, g)


CONV_TC = 1408
CONV_TT = 256
HALO = 8


def _shift_down(u, halo, first):
    row = lax.broadcasted_iota(jnp.int32, u.shape, 0)
    h6 = jnp.where(first, 0.0, halo[HALO - 2:HALO - 1, :])
    h7 = jnp.where(first, 0.0, halo[HALO - 1:HALO, :])
    s1 = jnp.where(row == 0, h7, pltpu.roll(u, 1, 0))
    s2 = jnp.where(row == 0, h6, jnp.where(row == 1, h7, pltpu.roll(u, 2, 0)))
    return s1, s2


def _conv_specs(tt, n_c, n_t, lead):
    def halo_row(i):
        return jnp.maximum(i * (tt // HALO) - 1, 0) if lead else jnp.minimum((i + 1) * (tt // HALO), n_t * (tt // HALO) - 1)
    return [
        pl.BlockSpec((tt, CONV_TC), lambda j, i: (i, j)),
        pl.BlockSpec((tt, CONV_TC), lambda j, i: (i, j + n_c)),
        pl.BlockSpec((HALO, CONV_TC), lambda j, i: (halo_row(i), j)),
        pl.BlockSpec((HALO, CONV_TC), lambda j, i: (halo_row(i), j + n_c)),
    ]


def _conv_z(ug, uv, hg, hv, w_g, w_v, b_g, b_v, first):
    g1, g2 = _shift_down(ug, hg, first)
    v1, v2 = _shift_down(uv, hv, first)
    zg = b_g + w_g[0:1, :] * g2
    zg = zg + w_g[1:2, :] * g1
    zg = zg + w_g[2:3, :] * ug
    zv = b_v + w_v[0:1, :] * v2
    zv = zv + w_v[1:2, :] * v1
    zv = zv + w_v[2:3, :] * uv
    return zg, zv, (g2, g1, ug), (v2, v1, uv)


def _conv_fwd(u, conv_w, conv_b):
    S = u.shape[0]
    tt = min(CONV_TT, S)
    n_c, n_t = D_FF // CONV_TC, S // tt
    wspec = [pl.BlockSpec((3, CONV_TC), lambda j, i: (0, j)), pl.BlockSpec((3, CONV_TC), lambda j, i: (0, j + n_c)),
             pl.BlockSpec((1, CONV_TC), lambda j, i: (0, j)), pl.BlockSpec((1, CONV_TC), lambda j, i: (0, j + n_c))]

    def body(ug_ref, uv_ref, hg_ref, hv_ref, wg_ref, wv_ref, bg_ref, bv_ref, a_ref):
        first = pl.program_id(1) == 0
        zg, zv, _, _ = _conv_z(ug_ref[...], uv_ref[...], hg_ref, hv_ref, wg_ref, wv_ref, bg_ref[...], bv_ref[...], first)
        a_ref[...] = (zg * jax.nn.sigmoid(zg) * zv).astype(BF16)

    return pl.pallas_call(
        body, name="conv_fwd", grid=(n_c, n_t), in_specs=_conv_specs(tt, n_c, n_t, True) + wspec,
        out_specs=pl.BlockSpec((tt, CONV_TC), lambda j, i: (i, j)), out_shape=jax.ShapeDtypeStruct((S, D_FF), BF16),
        compiler_params=_params("parallel", "parallel"))(u, u, u, u, conv_w, conv_w, conv_b, conv_b)


def _gate_bwd(da, zg, zv):
    sg = jax.nn.sigmoid(zg)
    return da * zv * (sg * (1.0 + zg * (1.0 - sg))), da * (zg * sg)


def _conv_bwd(da, u, conv_w, conv_b):
    S = u.shape[0]
    tt = min(CONV_TT, S)
    n_c, n_t = D_FF // CONV_TC, S // tt
    wspec = [pl.BlockSpec((3, CONV_TC), lambda j, i: (0, j)), pl.BlockSpec((3, CONV_TC), lambda j, i: (0, j + n_c)),
             pl.BlockSpec((1, CONV_TC), lambda j, i: (0, j)), pl.BlockSpec((1, CONV_TC), lambda j, i: (0, j + n_c))]
    tile = pl.BlockSpec((tt, CONV_TC), lambda j, i: (i, j))
    trail = pl.BlockSpec((HALO, CONV_TC), lambda j, i: (jnp.minimum((i + 1) * (tt // HALO), n_t * (tt // HALO) - 1), j))

    def body(da_ref, ug_ref, uv_ref, hg_ref, hv_ref, tda_ref, tg_ref, tv_ref, wg_ref, wv_ref, bg_ref, bv_ref,
             dug_ref, duv_ref, dwg_ref, dwv_ref, dbg_ref, dbv_ref):
        i = pl.program_id(1)
        last = i == n_t - 1
        bg, bv = bg_ref[...], bv_ref[...]
        zg, zv, gs, vs = _conv_z(ug_ref[...], uv_ref[...], hg_ref, hv_ref, wg_ref, wv_ref, bg, bv, i == 0)
        dzg, dzv = _gate_bwd(da_ref[...], zg, zv)
        tzg, tzv, _, _ = _conv_z(tg_ref[...], tv_ref[...], ug_ref.at[pl.ds(tt - HALO, HALO), :], uv_ref.at[pl.ds(tt - HALO, HALO), :],
                                 wg_ref, wv_ref, bg, bv, False)
        tdzg, tdzv = _gate_bwd(tda_ref[...], tzg, tzv)
        row = lax.broadcasted_iota(jnp.int32, dzg.shape, 0)
        for dz, tdz, w_ref, du_ref in ((dzg, tdzg, wg_ref, dug_ref), (dzv, tdzv, wv_ref, duv_ref)):
            h0 = jnp.where(last, 0.0, tdz[0:1, :])
            h1 = jnp.where(last, 0.0, tdz[1:2, :])
            u1 = jnp.where(row == tt - 1, h0, pltpu.roll(dz, tt - 1, 0))
            u2 = jnp.where(row == tt - 1, h1, jnp.where(row == tt - 2, h0, pltpu.roll(dz, tt - 2, 0)))
            du_ref[...] = (w_ref[2:3, :] * dz + w_ref[1:2, :] * u1 + w_ref[0:1, :] * u2).astype(BF16)

        @pl.when(i == 0)
        def _():
            for r in (dwg_ref, dwv_ref, dbg_ref, dbv_ref):
                r[...] = jnp.zeros_like(r)

        for k in range(3):
            dwg_ref[k:k + 1, :] += jnp.sum(dzg * gs[k], axis=0, keepdims=True)
            dwv_ref[k:k + 1, :] += jnp.sum(dzv * vs[k], axis=0, keepdims=True)
        dbg_ref[...] += jnp.sum(dzg, axis=0, keepdims=True)
        dbv_ref[...] += jnp.sum(dzv, axis=0, keepdims=True)

    lead = _conv_specs(tt, n_c, n_t, True)
    trail_v = pl.BlockSpec((HALO, CONV_TC), lambda j, i: (jnp.minimum((i + 1) * (tt // HALO), n_t * (tt // HALO) - 1), j + n_c))
    outs = pl.pallas_call(
        body, name="conv_bwd", grid=(n_c, n_t), in_specs=[tile] + lead + [trail, trail, trail_v] + wspec,
        out_specs=[tile, tile] + [pl.BlockSpec((3, CONV_TC), lambda j, i: (0, j))] * 2 + [pl.BlockSpec((1, CONV_TC), lambda j, i: (0, j))] * 2,
        out_shape=[jax.ShapeDtypeStruct((S, D_FF), BF16)] * 2 + [jax.ShapeDtypeStruct((3, D_FF), F32)] * 2
        + [jax.ShapeDtypeStruct((1, D_FF), F32)] * 2,
        compiler_params=_params("parallel", "arbitrary"))(da, u, u, u, u, da, u, u, conv_w, conv_w, conv_b, conv_b)
    dug, duv, dwg, dwv, dbg, dbv = outs
    return dug, duv, jnp.concatenate([dwg, dwv], axis=1), jnp.concatenate([dbg, dbv], axis=1)


BAND = 128


MEM_TQ = 512


def _mem_fwd(q, kv, *, name):
    S, W = q.shape
    M = kv.shape[0]
    nh = W // LANES
    tq = min(MEM_TQ, S)

    def body(q_ref, k_ref, v_ref, o_ref, l_ref):
        s = lax.dot_general(q_ref[...], k_ref[...].astype(BF16), NT, preferred_element_type=F32)
        m = jnp.max(s, axis=1, keepdims=True)
        p = jnp.exp(s - m)
        l = jnp.sum(p, axis=1, keepdims=True)
        o_ref[...] = (lax.dot_general(p.astype(BF16), v_ref[...].astype(BF16), NN, preferred_element_type=F32) / l).astype(BF16)
        l_ref[...] = jnp.broadcast_to(m + jnp.log(l), (tq, LANES))

    blk = pl.BlockSpec((tq, LANES), lambda hh, i: (i, hh))
    return pl.pallas_call(
        body, name=name, grid=(nh, S // tq),
        in_specs=[blk, pl.BlockSpec((M, LANES), lambda hh, i: (0, hh)), pl.BlockSpec((M, LANES), lambda hh, i: (0, hh + nh))],
        out_specs=[blk, blk], out_shape=[jax.ShapeDtypeStruct((S, W), BF16), jax.ShapeDtypeStruct((S, W), F32)],
        compiler_params=_params("parallel", "parallel"))(q, kv, kv)


def _mem_bwd(q, kv, do, lse, delta, *, scale, name):
    S, W = q.shape
    M = kv.shape[0]
    nh = W // LANES
    tq = min(MEM_TQ, S)

    def body(q_ref, k_ref, v_ref, do_ref, l_ref, d_ref, dq_ref, dk_ref, dv_ref):
        i = pl.program_id(1)

        @pl.when(i == 0)
        def _():
            dk_ref[...] = jnp.zeros_like(dk_ref)
            dv_ref[...] = jnp.zeros_like(dv_ref)

        qv = q_ref[...]
        kv_, vv = k_ref[...].astype(BF16), v_ref[...].astype(BF16)
        dov = do_ref[...].astype(BF16)
        s = lax.dot_general(qv, kv_, NT, preferred_element_type=F32)
        p = jnp.exp(s - l_ref[...][:, 0:1])
        dp = lax.dot_general(dov, vv, NT, preferred_element_type=F32)
        ds = (p * (dp - d_ref[...][:, 0:1])).astype(BF16)
        dq_ref[...] = lax.dot_general(ds, kv_, NN, preferred_element_type=F32) * scale
        dk_ref[...] += lax.dot_general(ds, qv, TN, preferred_element_type=F32)
        dv_ref[...] += lax.dot_general(p.astype(BF16), dov, TN, preferred_element_type=F32)

    blk = pl.BlockSpec((tq, LANES), lambda hh, i: (i, hh))
    kblk = pl.BlockSpec((M, LANES), lambda hh, i: (0, hh))
    vblk = pl.BlockSpec((M, LANES), lambda hh, i: (0, hh + nh))
    dq, dk, dv = pl.pallas_call(
        body, name=name, grid=(nh, S // tq), in_specs=[blk, kblk, vblk, blk, blk, blk], out_specs=[blk, kblk, kblk],
        out_shape=[jax.ShapeDtypeStruct((S, W), F32), jax.ShapeDtypeStruct((M, W), F32), jax.ShapeDtypeStruct((M, W), F32)],
        compiler_params=_params("parallel", "arbitrary"))(q, kv, kv, do, lse, delta)
    return dq, jnp.concatenate([dk, dv], axis=1)


CAUSAL_BLOCK = 512
STRIP = 32


def _causal_fwd(q, k, v, *, name):
    S, W = q.shape
    T = min(CAUSAL_BLOCK, S // 2)
    n_strips = T // STRIP

    def body(q_ref, k_ref, v_ref, o_ref, lse_ref, s0, s1, p0, p1, a0, a1, acc_scr):
        i = pl.program_id(1)
        s_scr, p_scr, a_scr = (s0, s1), (p0, p1), (a0, a1)

        def rows(j):
            return pl.ds(pl.multiple_of(j * T, T), T)

        def scores(j, slot):
            s_scr[slot][...] = lax.dot_general(q_ref[...], k_ref[rows(j), :], NT, preferred_element_type=F32)

        def softmax(slot, stats, diag):
            def strip(r):
                s = s_scr[slot][r * STRIP:(r + 1) * STRIP, :]
                if diag:
                    row = r * STRIP + lax.broadcasted_iota(jnp.int32, s.shape, 0)
                    s = jnp.where(row >= lax.broadcasted_iota(jnp.int32, s.shape, 1), s, NEG_INF)
                return s

            m_new = [jnp.maximum(m_old, jnp.max(strip(r), axis=1, keepdims=True)) for r, (m_old, _) in enumerate(stats)]
            new = []
            for r, (m_old, l_old) in enumerate(stats):
                rs = slice(r * STRIP, (r + 1) * STRIP)
                p = jnp.exp(strip(r) - m_new[r])
                alpha = jnp.exp(m_old - m_new[r])
                new.append((m_new[r], alpha * l_old + jnp.sum(p, axis=1, keepdims=True)))
                a_scr[slot][rs, :] = alpha
                p_scr[slot][rs, :] = p.astype(BF16)
            return tuple(new)

        def values(j, slot):
            acc_scr[...] = a_scr[slot][...] * acc_scr[...] + lax.dot_general(p_scr[slot][...], v_ref[rows(j), :], NN,
                                                                            preferred_element_type=F32)

        def trip(j, stats, mine, other):
            scores(j + 1, other)
            stats = softmax(mine, stats, False)
            values(jnp.maximum(j - 1, 0), other)
            return stats

        def pair(jj, stats):
            return trip(2 * jj + 1, trip(2 * jj, stats, 0, 1), 1, 0)

        def last(stats, mine, other):
            values(jnp.maximum(i - 1, 0), other)
            stats = softmax(mine, stats, True)
            values(i, mine)
            for r, (m, l) in enumerate(stats):
                rs = slice(r * STRIP, (r + 1) * STRIP)
                o_ref[rs, :] = (acc_scr[rs, :] / l).astype(BF16)
                lse_ref[rs, :] = jnp.broadcast_to(m + jnp.log(l), (STRIP, LANES))

        acc_scr[...] = jnp.zeros_like(acc_scr)
        p1[...] = jnp.zeros_like(p1)
        a1[...] = jnp.ones_like(a1)
        scores(0, 0)
        init = tuple((jnp.full((STRIP, 1), NEG_INF, F32), jnp.zeros((STRIP, 1), F32)) for _ in range(n_strips))
        stats = lax.fori_loop(0, i // 2, pair, init)

        @pl.when(i % 2 == 1)
        def _():
            last(trip(i - 1, stats, 0, 1), 1, 0)

        @pl.when(i % 2 == 0)
        def _():
            last(stats, 0, 1)

    blk = pl.BlockSpec((T, LANES), lambda hh, i: (i, hh))
    whole = pl.BlockSpec((S, LANES), lambda hh, i: (0, hh))
    return pl.pallas_call(
        body, name=name, grid=(W // LANES, S // T), in_specs=[blk, whole, whole], out_specs=[blk, blk],
        out_shape=[jax.ShapeDtypeStruct((S, W), BF16), jax.ShapeDtypeStruct((S, W), F32)],
        scratch_shapes=[pltpu.VMEM((T, T), F32)] * 2 + [pltpu.VMEM((T, T), BF16)] * 2 + [pltpu.VMEM((T, 1), F32)] * 2
        + [pltpu.VMEM((T, LANES), F32)],
        compiler_params=_params("parallel", "arbitrary"))(q, k, v)


def _causal_bwd(q, k, v, do, lse, delta, *, name):
    S, W = q.shape
    T = min(CAUSAL_BLOCK, S // 2)
    nq = S // T
    n_strips, n_col = T // STRIP, T // LANES

    def body(q_ref, k_ref, v_ref, do_ref, l_ref, d_ref, dq_ref, dk_ref, dv_ref, s0, s1, e0, e1, p0, p1, g0, g1):
        j = pl.program_id(1)
        s_scr, e_scr, p_scr, g_scr = (s0, s1), (e0, e1), (p0, p1), (g0, g1)

        @pl.when(j == 0)
        def _():
            dq_ref[...] = jnp.zeros_like(dq_ref)

        dk_ref[...] = jnp.zeros_like(dk_ref)
        dv_ref[...] = jnp.zeros_like(dv_ref)

        def rows(i):
            return pl.ds(pl.multiple_of(jnp.minimum(i, nq - 1) * T, T), T)

        def products(i, slot):
            r = rows(i)
            s_scr[slot][...] = lax.dot_general(q_ref[r, :], k_ref[...], NT, preferred_element_type=F32)
            e_scr[slot][...] = lax.dot_general(do_ref[r, :].astype(BF16), v_ref[...], NT, preferred_element_type=F32)

        def pointwise(i, slot, diag):
            base = pl.multiple_of(i * T, T)
            for r in range(n_strips):
                rs = slice(r * STRIP, (r + 1) * STRIP)
                lse_r = l_ref[pl.ds(base + r * STRIP, STRIP), :]
                del_r = d_ref[pl.ds(base + r * STRIP, STRIP), :]
                for c in range(n_col):
                    cs = slice(c * LANES, (c + 1) * LANES)
                    if diag and c * LANES > (r + 1) * STRIP - 1:
                        p_scr[slot][rs, cs] = jnp.zeros((STRIP, LANES), BF16)
                        g_scr[slot][rs, cs] = jnp.zeros((STRIP, LANES), BF16)
                        continue
                    sv = s_scr[slot][rs, cs]
                    if diag and (c + 1) * LANES - 1 > r * STRIP:
                        row = r * STRIP + lax.broadcasted_iota(jnp.int32, sv.shape, 0)
                        col = c * LANES + lax.broadcasted_iota(jnp.int32, sv.shape, 1)
                        sv = jnp.where(row >= col, sv, NEG_INF)
                    p = jnp.exp(sv - lse_r)
                    p_scr[slot][rs, cs] = p.astype(BF16)
                    g_scr[slot][rs, cs] = (p * (e_scr[slot][rs, cs] - del_r)).astype(BF16)

        def gradients(i, slot):
            r = rows(i)
            qi, doi = q_ref[r, :], do_ref[r, :].astype(BF16)
            g = g_scr[slot][...]
            dv_ref[...] += lax.dot_general(p_scr[slot][...], doi, TN, preferred_element_type=F32)
            dk_ref[...] += lax.dot_general(g, qi, TN, preferred_element_type=F32)
            dq_ref[r, :] += lax.dot_general(g, k_ref[...], NN, preferred_element_type=F32)

        products(j, 0)
        products(j + 1, 1)
        pointwise(j, 0, True)

        def trip(i, mine, other):
            products(i + 1, other)
            pointwise(i, mine, False)
            gradients(i - 1, other)

        def pair(t, _):
            trip(j + 1 + 2 * t, 1, 0)
            trip(j + 2 + 2 * t, 0, 1)
            return 0

        n_rest = nq - 1 - j
        lax.fori_loop(0, n_rest // 2, pair, 0)

        @pl.when(n_rest % 2 == 1)
        def _():
            trip(nq - 1, 1, 0)
            gradients(nq - 1, 1)

        @pl.when(n_rest % 2 == 0)
        def _():
            gradients(nq - 1, 0)

    blk = pl.BlockSpec((T, LANES), lambda hh, j: (j, hh))
    whole = pl.BlockSpec((S, LANES), lambda hh, j: (0, hh))
    return pl.pallas_call(
        body, name=name, grid=(W // LANES, S // T), in_specs=[whole, blk, blk, whole, whole, whole],
        out_specs=[whole, blk, blk], out_shape=[jax.ShapeDtypeStruct((S, W), F32)] * 3,
        scratch_shapes=[pltpu.VMEM((T, T), F32)] * 4 + [pltpu.VMEM((T, T), BF16)] * 4,
        compiler_params=_params("parallel", "arbitrary"))(q, k, v, do, lse, delta)


BAND_TQ = 512


def _band_window(i, sub, nsub, L, q_ref, k_ref, v_ref, slope):
    kw = min(2 * BAND, L)
    n = i * nsub + sub
    k0 = 0 if kw == L else pl.multiple_of(jnp.maximum(n - 1, 0) * BAND, BAND)
    win = pl.ds(k0, kw)
    qs = q_ref[sub * BAND:(sub + 1) * BAND, :]
    kwv, vwv = k_ref[win, :], v_ref[win, :]
    s = lax.dot_general(qs, kwv, NT, preferred_element_type=F32)
    dist = (n * BAND + lax.broadcasted_iota(jnp.int32, s.shape, 0)) - (k0 + lax.broadcasted_iota(jnp.int32, s.shape, 1))
    s = jnp.where((dist >= 0) & (dist <= BAND), s - slope * dist.astype(F32), NEG_INF)
    return win, qs, kwv, vwv, s


def _band_fwd(q, k, v, slopes, *, n_heads, qcol, kcol, vcol, L, slope_mul, o_shape, name):
    tq = min(BAND_TQ, L)
    nsub = tq // BAND

    def body(sl_ref, q_ref, k_ref, v_ref, o_ref, l_ref):
        hh, i = pl.program_id(0), pl.program_id(1)
        slope = sl_ref[hh % DIL_HPG] * slope_mul
        wins = [_band_window(i, sub, nsub, L, q_ref, k_ref, v_ref, slope) for sub in range(nsub)]
        ms = [jnp.max(w[4], axis=1, keepdims=True) for w in wins]
        ps = [jnp.exp(w[4] - m) for w, m in zip(wins, ms)]
        ls = [jnp.sum(p, axis=1, keepdims=True) for p in ps]
        for sub, (w, m, p, l) in enumerate(zip(wins, ms, ps, ls)):
            rows = slice(sub * BAND, (sub + 1) * BAND)
            o_ref[rows, :] = lax.dot_general(p.astype(BF16), w[3], NN, preferred_element_type=F32) / l
            l_ref[rows, :] = jnp.broadcast_to(m + jnp.log(l), (BAND, LANES))

    whole = lambda col: pl.BlockSpec((L, LANES), lambda hh, i: (0, col(hh)))
    o_spec = pl.BlockSpec((tq, LANES), lambda hh, i: (i, hh))
    return pl.pallas_call(
        body, name=name, grid=(n_heads, L // tq),
        in_specs=[pl.BlockSpec(memory_space=pltpu.SMEM), pl.BlockSpec((tq, LANES), lambda hh, i: (i, qcol(hh))), whole(kcol), whole(vcol)],
        out_specs=[o_spec, o_spec], out_shape=[jax.ShapeDtypeStruct(o_shape, F32)] * 2,
        compiler_params=_params("parallel", "arbitrary"))(slopes, q, k, v)


def _band_bwd(q, k, v, do, lse, delta, slopes, *, n_heads, qcol, kcol, vcol, L, scale, slope_mul, d_shape, name):
    tq = min(BAND_TQ, L)
    nsub = tq // BAND
    n_steps = L // tq

    def body(sl_ref, q_ref, k_ref, v_ref, do_ref, l_ref, d_ref, dq_ref, dk_ref, dv_ref, dk_acc, dv_acc):
        hh, i = pl.program_id(0), pl.program_id(1)
        slope = sl_ref[hh % DIL_HPG] * slope_mul

        @pl.when(i == 0)
        def _():
            dk_acc[...] = jnp.zeros_like(dk_acc)
            dv_acc[...] = jnp.zeros_like(dv_acc)

        blocks = range(nsub)
        rows = [slice(sub * BAND, (sub + 1) * BAND) for sub in blocks]
        wins = [_band_window(i, sub, nsub, L, q_ref, k_ref, v_ref, slope) for sub in blocks]
        dos = [do_ref[r, :] for r in rows]
        dps = [lax.dot_general(do, w[3], NT, preferred_element_type=F32) for do, w in zip(dos, wins)]
        ps = [jnp.exp(w[4] - l_ref[r, :][:, 0:1]) for w, r in zip(wins, rows)]
        dss = [(p * (dp - d_ref[r, :][:, 0:1])).astype(BF16) for p, dp, r in zip(ps, dps, rows)]
        for r, ds, w in zip(rows, dss, wins):
            dq_ref[r, :] = (lax.dot_general(ds, w[2], NN, preferred_element_type=F32) * scale).astype(BF16)
        dks = [lax.dot_general(ds, w[1], TN, preferred_element_type=F32) for ds, w in zip(dss, wins)]
        dvs = [lax.dot_general(p.astype(BF16), do, TN, preferred_element_type=F32) for p, do in zip(ps, dos)]
        for w, dk, dv in zip(wins, dks, dvs):
            dk_acc[w[0], :] += dk
            dv_acc[w[0], :] += dv

        @pl.when(i == n_steps - 1)
        def _():
            dk_ref[...] = dk_acc[...].astype(BF16)
            dv_ref[...] = dv_acc[...].astype(BF16)

    whole = lambda col: pl.BlockSpec((L, LANES), lambda hh, i: (0, col(hh)))
    blk = pl.BlockSpec((tq, LANES), lambda hh, i: (i, hh))
    ident = lambda hh: hh
    return pl.pallas_call(
        body, name=name, grid=(n_heads, n_steps),
        in_specs=[pl.BlockSpec(memory_space=pltpu.SMEM), pl.BlockSpec((tq, LANES), lambda hh, i: (i, qcol(hh))), whole(kcol), whole(vcol),
                  blk, blk, blk],
        out_specs=[blk, whole(ident), whole(ident)], out_shape=[jax.ShapeDtypeStruct(d_shape, BF16)] * 3,
        scratch_shapes=[pltpu.VMEM((L, LANES), F32)] * 2,
        compiler_params=_params("parallel", "arbitrary"))(slopes, q, k, v, do, lse, delta)


def _pad_heads(w, n_heads, width, axis):
    shp = w.shape
    new = shp[:axis] + (n_heads, width) + shp[axis + 1:]
    pad = [(0, 0)] * len(new)
    pad[axis + 1] = (0, LANES - width)
    out = jnp.pad(w.reshape(new), pad)
    return out.reshape(shp[:axis] + (n_heads * LANES,) + shp[axis + 1:])


def _unpad_heads(w, n_heads, width, axis):
    shp = w.shape
    new = shp[:axis] + (n_heads, LANES) + shp[axis + 1:]
    out = lax.slice_in_dim(w.reshape(new), 0, width, axis=axis + 1)
    return out.reshape(shp[:axis] + (n_heads * width,) + shp[axis + 1:])


def _alibi_slopes():
    s = jnp.exp2(-8.0 * jnp.arange(1, DIL_HEADS + 1, dtype=F32) / DIL_HEADS)
    return s.reshape(DIL_HPG, DIL_GROUPS).T


def _local_step(x, mem, positions, tgt, W):
    S = x.shape[0]
    pos = positions.reshape(S, 1).astype(F32)
    half = MLA_ROPE // 2
    inv_freq = ROPE_THETA ** (-jnp.arange(half, dtype=F32) / half)
    invf = jnp.zeros((1, LANES), F32).at[0, MLA_NOPE:MLA_NOPE + half].set(inv_freq).at[0, MLA_NOPE + half:MLA_QK].set(inv_freq)
    slopes = _alibi_slopes()

    w_in = W["w_in"]
    zc = lambda n: jnp.zeros((D_MODEL, n), BF16)
    w_a = jnp.concatenate([w_in[:, :OFF_Q], zc(MLA_NOPE), w_in[:, OFF_KV:OFF_KR], zc(LANES - MLA_QK), w_in[:, OFF_Q:OFF_KV]], axis=1)
    w_d, w_m, w_g = w_in[:, OFF_KR:OFF_DIL], w_in[:, OFF_DIL:OFF_MEMQ], w_in[:, OFF_MEMQ:]
    w_uq_p = _pad_heads(W["w_uq"], MLA_HEADS, MLA_QK, 1)
    ukv = W["w_ukv"].reshape(KV_RANK, MLA_HEADS, 2 * MLA_NOPE)
    w_uk_p = _pad_heads(ukv[:, :, :MLA_NOPE].reshape(KV_RANK, -1), MLA_HEADS, MLA_NOPE, 1)
    w_uv_p = _pad_heads(ukv[:, :, MLA_NOPE:].reshape(KV_RANK, -1), MLA_HEADS, MLA_NOPE, 1)
    w_br_mla_p = _pad_heads(W["w_br_mla"], MLA_HEADS, MLA_NOPE, 0)

    h = _rmsnorm(x, W["g_pre_mix"], BF16, "norm1")
    proj_a = _matmul(h, w_a, tn=768, name="proj_a")
    att_scale = LANES ** -0.5
    w_dg = [jnp.concatenate([w_d[:, (part * DIL_GROUPS + g) * DIL_W:(part * DIL_GROUPS + g + 1) * DIL_W] for part in range(3)], axis=1)
            for g in range(DIL_GROUPS)]
    proj_m = _matmul(h, w_m, tn=MEM_W, out_dtype=BF16, scale=(1, att_scale), name="proj_m")
    proj_g = _matmul(h, w_g, tn=1536, name="proj_g")

    cq_n, ckv_n, kpe = _mla_prep(proj_a, pos, invf, W["mla_q_norm"], W["mla_kv_norm"])
    q_pre = _matmul(cq_n, w_uq_p, tn=1024, name="mla_q")
    k_pre = _matmul(ckv_n, w_uk_p, tn=1024, name="mla_k")
    v_mla = _matmul(ckv_n, w_uv_p, tn=1024, out_dtype=BF16, name="mla_v")
    q_mla, k_mla = _qk_final(q_pre, k_pre, kpe, pos, invf)
    ident = lambda hh: hh
    o_mla, lse_mla = _causal_fwd(q_mla, k_mla, v_mla, name="attn_mla_fwd")

    col = lambda part: (lambda hh: (hh // DIL_HPG) * 3 * DIL_HPG + part * DIL_HPG + hh % DIL_HPG)
    h_views = [h.reshape(S // d, d * D_MODEL) for d in DILATIONS]
    proj_dg, o_dil, lse_dil = [], [], []
    for g, d in enumerate(DILATIONS):
        pv = _proj_view(h_views[g], w_dg[g], d, att_scale, f"proj_d{g}")
        o, lse = _band_fwd(pv, pv, pv, slopes[g], n_heads=d * DIL_HPG, qcol=col(0), kcol=col(1), vcol=col(2), L=S // d,
                           slope_mul=float(d), o_shape=(S // d, d * DIL_W), name=f"attn_dil{g}_fwd")
        proj_dg.append(pv)
        o_dil.append(o)
        lse_dil.append(lse)
    y_dil = _dil_mix(o_dil, lse_dil)

    mem_n = _rmsnorm(mem, W["g_mem"], BF16, "mem_norm")
    kv_mem = _matmul(mem_n, W["w_mem_kv"], name="mem_kv")
    o_mem, lse_mem = _mem_fwd(proj_m, kv_mem, name="attn_mem_fwd")

    b_mla = _matmul(o_mla, w_br_mla_p, name="br_mla")
    b_dil = _matmul(y_dil, W["w_br_dil"], name="br_dil")
    b_mem = _matmul(o_mem, W["w_br_mem"], name="br_mem")
    merged = _merge(proj_g, W["b_gate"], [b_mla, b_dil, b_mem])
    o_proj = _matmul(merged, W["w_o"], name="o_proj")
    x1, h2 = _norm2(o_proj, x, W["g_post_mix"], W["g_pre_ffn"])

    u = _matmul(h2, W["w_ffn_up"], tn=1408, name="ffn_up")
    act = _conv_fwd(u, W["conv_w"], W["conv_b"])
    f = _matmul(act, W["w_ffn_down"], name="ffn_down")
    loss8, dx2, df, dg_post_ffn = _loss_head(f, x1, tgt, W["g_post_ffn"])
    loss = loss8[0, 0]

    G = {"g_post_ffn": dg_post_ffn}
    d_act = _matmul(df, W["w_ffn_down"], mode="nt", tn=1408, name="d_act")
    G["w_ffn_down"] = _matmul(act, df, mode="tn", tm=1408, tk=1024, name="dw_ffn_down")
    du_g, du_v, G["conv_w"], G["conv_b"] = _conv_bwd(d_act, u, W["conv_w"], W["conv_b"])
    dh2 = _matmul(du_g, W["w_ffn_up"][:, :D_FF], mode="nt", name="d_h2_gate")
    dh2 = _matmul(du_v, W["w_ffn_up"][:, D_FF:], mode="nt", add=dh2, name="d_h2_value")
    G["w_ffn_up"] = jnp.concatenate([_matmul(h2, du_g, mode="tn", tn=1408, tk=1024, name="dw_ffn_up_gate"),
                                     _matmul(h2, du_v, mode="tn", tn=1408, tk=1024, name="dw_ffn_up_value")], axis=1)
    dx1, do_proj, G["g_pre_ffn"], G["g_post_mix"] = _norm2_bwd(dx2, dh2, x1, o_proj, W["g_pre_ffn"], W["g_post_mix"])
    dmerged = _matmul(do_proj, W["w_o"], mode="nt", name="d_merged")
    G["w_o"] = _matmul(merged, do_proj, mode="tn", tk=1024, name="dw_o")
    db_mla, db_dil, db_mem, dproj_g, G["b_gate"] = _merge_bwd(dmerged, proj_g, W["b_gate"], [b_mla, b_dil, b_mem])

    dy_mem = _matmul(db_mem, W["w_br_mem"], mode="nt", name="d_y_mem")
    G["w_br_mem"] = _matmul(o_mem, db_mem, mode="tn", tk=1024, name="dw_br_mem")
    delta_mem = _delta(dy_mem, o_mem, MEM_HEADS, "delta_mem")
    dq_mem, dkv_mem = _mem_bwd(proj_m, kv_mem, dy_mem, lse_mem, delta_mem, scale=att_scale, name="attn_mem_bwd")
    G["w_mem_kv"] = _matmul(mem_n, dkv_mem, mode="tn", name="dw_mem_kv")
    dmem_n = _matmul(dkv_mem, W["w_mem_kv"], mode="nt", name="d_mem_n")
    G["g_mem"] = _gain_grad(dmem_n, mem, "dg_mem")

    dy_dil = _matmul(db_dil, W["w_br_dil"], mode="nt", name="d_y_dil")
    G["w_br_dil"] = _matmul(y_dil, db_dil, mode="tn", tk=1024, name="dw_br_dil")
    mix = _dil_mix_bwd(dy_dil, o_dil, lse_dil)
    do_dil, dl_dil = mix[:3], mix[3:]
    dh_views, dw_groups = [], []
    for g, d in enumerate(DILATIONS):
        pv = proj_dg[g]
        parts = _band_bwd(pv, pv, pv, do_dil[g], lse_dil[g], dl_dil[g], slopes[g], n_heads=d * DIL_HPG, qcol=col(0),
                          kcol=col(1), vcol=col(2), L=S // d, scale=att_scale, slope_mul=float(d), d_shape=(S // d, d * DIL_W),
                          name=f"attn_dil{g}_bwd")
        dpv = jnp.concatenate([p[:, r * DIL_W:(r + 1) * DIL_W] for r in range(d) for p in parts], axis=1)
        dh_views.append(_dh_view(dpv, w_dg[g], d, None, f"d_h_d{g}"))
        dw_groups.append(_dw_view(h_views[g], dpv, d, f"dw_in_d{g}"))
    dw_d = jnp.concatenate([dw[:, part * DIL_W:(part + 1) * DIL_W] for part in range(3) for dw in dw_groups], axis=1)

    dy_mla = _matmul(db_mla, w_br_mla_p, mode="nt", name="d_y_mla")
    dw_br_mla_p = _matmul(o_mla, db_mla, mode="tn", tk=1024, name="dw_br_mla")
    G["w_br_mla"] = _unpad_heads(dw_br_mla_p, MLA_HEADS, MLA_NOPE, 0)
    delta_mla = _delta(dy_mla, o_mla, MLA_HEADS, "delta_mla")
    dq_mla, dk_mla, dv_mla = _causal_bwd(q_mla, k_mla, v_mla, dy_mla, lse_mla, delta_mla, name="attn_mla_bwd")
    dq_pre, dkpe = _mla_bwd_prep(dq_mla, dk_mla, pos, invf)
    dcq_n = _matmul(dq_pre, w_uq_p, mode="nt", tn=Q_RANK, name="d_cq")
    G["w_uq"] = _unpad_heads(_matmul(cq_n, dq_pre, mode="tn", tm=Q_RANK, tk=1024, name="dw_uq"), MLA_HEADS, MLA_QK, 1)
    dckv_a = _matmul(dk_mla, w_uk_p, mode="nt", tn=KV_RANK, name="d_ckv_k")
    dckv_b = _matmul(dv_mla, w_uv_p, mode="nt", tn=KV_RANK, name="d_ckv_v")
    dw_uk = _unpad_heads(_matmul(ckv_n, dk_mla, mode="tn", tm=KV_RANK, tk=1024, name="dw_uk"), MLA_HEADS, MLA_NOPE, 1)
    dw_uv = _unpad_heads(_matmul(ckv_n, dv_mla, mode="tn", tm=KV_RANK, tk=1024, name="dw_uv"), MLA_HEADS, MLA_NOPE, 1)
    G["w_ukv"] = jnp.concatenate([dw_uk.reshape(KV_RANK, MLA_HEADS, MLA_NOPE), dw_uv.reshape(KV_RANK, MLA_HEADS, MLA_NOPE)],
                                 axis=2).reshape(KV_RANK, -1)
    dproj_a, G["mla_q_norm"], G["mla_kv_norm"] = _mla_norm_bwd(dcq_n, dckv_a, dckv_b, dkpe, proj_a, W["mla_q_norm"],
                                                              W["mla_kv_norm"])

    dh = _matmul(dproj_a, w_a, mode="nt", name="d_h_a")
    dh = _matmul(dq_mem, w_m, mode="nt", add=dh, name="d_h_m")
    dh = _matmul(dproj_g, w_g, mode="nt", add=dh, name="d_h_g")
    dw_a = _matmul(h, dproj_a, mode="tn", tn=768, tk=1024, name="dw_in_a")
    dw_m = _matmul(h, dq_mem, mode="tn", tn=512, tk=1024, name="dw_in_m")
    dw_g = _matmul(h, dproj_g, mode="tn", tn=1536, tk=1024, name="dw_in_g")
    kr0 = Q_RANK + MLA_NOPE
    G["w_in"] = jnp.concatenate([dw_a[:, :Q_RANK], dw_a[:, Q_RANK + LANES:], dw_a[:, kr0:kr0 + MLA_ROPE], dw_d, dw_m, dw_g], axis=1)
    grad_x, G["g_pre_mix"] = _norm1_bwd(dx1, [dh] + dh_views, x, W["g_pre_mix"])
    return loss, grad_x, G


WEIGHTS = ["g_pre_mix", "w_in", "b_gate", "mla_q_norm", "w_uq", "mla_kv_norm", "w_ukv", "g_mem", "w_mem_kv", "w_br_mla",
           "w_br_dil", "w_br_mem", "w_o", "g_post_mix", "g_pre_ffn", "w_ffn_up", "conv_w", "conv_b", "w_ffn_down", "g_post_ffn"]
GROUPS = [
    [("w_in", (D_MODEL, D_IN), 1)],
    [("w_uq", (Q_RANK, MLA_HEADS * MLA_QK), 1)],
    [("w_ukv", (KV_RANK, MLA_HEADS * 2 * MLA_NOPE), 1)],
    [("w_br_mla", (MLA_HEADS * MLA_NOPE, D_MODEL), 1), ("w_br_dil", (DIL_W, D_MODEL), 1), ("w_br_mem", (MEM_W, D_MODEL), 1)],
    [("w_mem_kv", (D_MODEL, 2 * MEM_W), 0), ("w_o", (D_MODEL, D_MODEL), 0), ("w_ffn_down", (D_FF, D_MODEL), 0)],
    [("w_ffn_up", (D_MODEL, 2 * D_FF), 1)],
]
EARLY_GROUPS = (3, 4, 5)
LATE_GROUPS = (0, 1, 2)
CONV_W = ("conv_w", (3, 2 * D_FF), 1)
REPLICATED = [("g_pre_mix", D_MODEL), ("b_gate", 3 * D_MODEL), ("mla_q_norm", Q_RANK), ("mla_kv_norm", KV_RANK), ("g_mem", D_MODEL),
              ("g_post_mix", D_MODEL), ("g_pre_ffn", D_MODEL), ("conv_b", 2 * D_FF), ("g_post_ffn", D_MODEL)]
SMALL_ROWS = 256
CONV_AT = sum(n for _, n in REPLICATED)
LOSS_AT = CONV_AT + 3 * 2 * D_FF


def _shard_shape(shape, axis):
    return tuple(d // N_CHIPS if a == axis else d for a, d in enumerate(shape))


def _group_shape(grp):
    shapes = [_shard_shape(shape, axis) for _, shape, axis in grp]
    assert len({s[1] for s in shapes}) == 1
    return sum(s[0] for s in shapes), shapes[0][1]


def _member_shards(a, axis):
    r, c = a.shape
    if axis == 0:
        return a.reshape(N_CHIPS, r // N_CHIPS, c)
    return a.reshape(r, N_CHIPS, c // N_CHIPS).transpose(1, 0, 2)


def _member_full(s, axis):
    n, r, c = s.shape
    if axis == 0:
        return s.reshape(n * r, c)
    return s.transpose(1, 0, 2).reshape(r, n * c)


def _my_weight_groups(w):
    out = []
    for grp in GROUPS:
        rows, width = _group_shape(grp)
        out.append(jnp.concatenate([w[name].astype(BF16) for name, _, _ in grp], axis=0).reshape(2, rows // 2, width))
    return out


def _full_weights(gathered, conv_all):
    out = {}
    for grp, ga in zip(GROUPS, gathered):
        ga = ga.reshape(N_CHIPS, -1, ga.shape[-1])
        off = 0
        for name, shape, axis in grp:
            rows = _shard_shape(shape, axis)[0]
            out[name] = _member_full(ga[:, off:off + rows], axis)
            off += rows
    out[CONV_W[0]] = _member_full(conv_all, CONV_W[2])
    return out


def _grad_groups(G):
    out = []
    for grp in GROUPS:
        rows, width = _group_shape(grp)
        a = jnp.concatenate([_member_shards(G[name], axis) for name, _, axis in grp], axis=1)
        out.append(a.reshape(N_CHIPS, 2, rows // 2, width).transpose(1, 0, 2, 3))
    return out


def _pack_small(vals, conv_g=None, loss=None):
    parts = [vals[name].reshape(-1) for name, _ in REPLICATED]
    if conv_g is not None:
        parts += [conv_g.reshape(-1), loss.reshape(1)]
    flat = jnp.concatenate(parts)
    return jnp.pad(flat, (0, SMALL_ROWS * LANES - flat.shape[0])).reshape(SMALL_ROWS, LANES)


def _unpack_small(packed):
    flat = packed.reshape(-1)
    out, off = {}, 0
    for name, n in REPLICATED:
        out[name] = flat[off:off + n].reshape(1, n)
        off += n
    return out


MESH = pl.DeviceIdType.MESH
HBM_SPEC = pl.BlockSpec(memory_space=pltpu.HBM)


def _place():
    x, y, c = lax.axis_index("x"), lax.axis_index("y"), lax.axis_index("c")
    chips = [(1 - x, y), (x, 1 - y), (1 - x, 1 - y)]
    return x, y, c, chips


def _remote(src, dst, send_sems, recv_sems, k, to):
    return pltpu.make_async_remote_copy(src_ref=src, dst_ref=dst, send_sem=send_sems.at[k], recv_sem=recv_sems.at[k],
                                        device_id=to, device_id_type=MESH)


def _gather_weights(groups, wholes, name, collective_id):
    n, m = len(groups), len(wholes)
    arrays = list(groups) + list(wholes)
    hbm = pltpu.MemorySpace.HBM
    srcs = [jax.new_ref(a, memory_space=hbm) for a in arrays]
    outs = [jax.empty_ref(jax.ShapeDtypeStruct((N_CHIPS,) + a.shape, a.dtype), memory_space=hbm) for a in arrays]
    n_sem = 6 * n + 3 * m

    def launch(send_sems, recv_sems):
        x, y, c, chips = _place()
        me = 2 * x + y
        sibling = (x, y, 1 - c)
        barrier = pltpu.get_barrier_semaphore()
        peers = [(px, py, c) for px, py in chips] + [sibling]
        for peer in peers:
            pl.semaphore_signal(barrier, inc=1, device_id=peer, device_id_type=MESH)
        pl.semaphore_wait(barrier, len(peers))
        first = [_remote(srcs[g].at[c], outs[g].at[me, c], send_sems, recv_sems, g * 3 + k, (px, py, c))
                 for k, (px, py) in enumerate(chips) for g in range(n)]
        first += [_remote(srcs[n + w], outs[n + w].at[me], send_sems, recv_sems, 6 * n + w * 3 + k, (px, py, c))
                  for k, (px, py) in enumerate(chips) for w in range(m)]
        for cp in first:
            cp.start()
        passed = []
        for k, (px, py) in enumerate(chips):
            for g in range(n):
                slot = outs[g].at[2 * px + py, c]
                _remote(slot, slot, send_sems, recv_sems, g * 3 + k, (px, py, c)).wait_recv()
                cp = _remote(slot, slot, send_sems, recv_sems, 3 * n + g * 3 + k, sibling)
                cp.start()
                passed.append(cp)
        for k, (px, py) in enumerate(chips):
            for w in range(m):
                slot = outs[n + w].at[2 * px + py]
                _remote(slot, slot, send_sems, recv_sems, 6 * n + w * 3 + k, (px, py, c)).wait_recv()
            for g in range(n):
                slot = outs[g].at[2 * px + py, 1 - c]
                _remote(slot, slot, send_sems, recv_sems, 3 * n + g * 3 + k, sibling).wait_recv()
        for cp in first + passed:
            cp.wait_send()

    pl.kernel(launch, mesh=plsc.ScalarSubcoreMesh(axis_name="sequencer", num_cores=1), name=name,
              scratch_types=(pltpu.SemaphoreType.DMA((n_sem,)), pltpu.SemaphoreType.DMA((n_sem,))),
              compiler_params=pltpu.CompilerParams(collective_id=collective_id))()
    me = 2 * lax.axis_index("x") + lax.axis_index("y")
    res = [lax.dynamic_update_slice_in_dim(o[...], a[None], me, axis=0) for o, a in zip(outs, arrays)]
    return res[:n], res[n:]


def _on_sequencer(exchange, arrays, out_structs, n_sem, peers, name, collective_id):
    hbm = pltpu.MemorySpace.HBM
    srcs = [jax.new_ref(a, memory_space=hbm) for a in arrays]
    outs = [jax.empty_ref(s, memory_space=hbm) for s in out_structs]

    def launch(send_sems, recv_sems):
        x, y, c, chips = _place()
        barrier = pltpu.get_barrier_semaphore()
        them = peers(x, y, c, chips)
        for peer in them:
            pl.semaphore_signal(barrier, inc=1, device_id=peer, device_id_type=MESH)
        pl.semaphore_wait(barrier, len(them))
        exchange(srcs, outs, send_sems, recv_sems)

    pl.kernel(launch, mesh=plsc.ScalarSubcoreMesh(axis_name="sequencer", num_cores=1), name=name,
              scratch_types=(pltpu.SemaphoreType.DMA((n_sem,)), pltpu.SemaphoreType.DMA((n_sem,))),
              compiler_params=pltpu.CompilerParams(collective_id=collective_id))()
    return [o[...] for o in outs]


def _sibling_only(x, y, c, chips):
    return [(x, y, 1 - c)]


def _swap_halves(groups, name, collective_id):
    n = len(groups)

    def exchange(srcs, outs, send_sems, recv_sems):
        x, y, c, _ = _place()
        cps = [_remote(srcs[g].at[1 - c], outs[g], send_sems, recv_sems, g, (x, y, 1 - c)) for g in range(n)]
        for cp in cps:
            cp.start()
        for cp in cps:
            cp.wait()

    return _on_sequencer(exchange, groups, [jax.ShapeDtypeStruct(g.shape[1:], g.dtype) for g in groups], n, _sibling_only,
                         name, collective_id)


def _scatter_partials(groups, name, collective_id):
    n = len(groups)

    def exchange(srcs, outs, send_sems, recv_sems):
        x, y, c, chips = _place()
        sends = [_remote(srcs[g].at[2 * px + py], outs[g].at[k], send_sems, recv_sems, g * 3 + k, (px, py, c))
                 for k, (px, py) in enumerate(chips) for g in range(n)]
        for cp in sends:
            cp.start()
        for k, (px, py) in enumerate(chips):
            for g in range(n):
                slot = outs[g].at[k]
                _remote(slot, slot, send_sems, recv_sems, g * 3 + k, (px, py, c)).wait_recv()
        for cp in sends:
            cp.wait_send()

    return _on_sequencer(exchange, groups, [jax.ShapeDtypeStruct((3,) + g.shape[1:], g.dtype) for g in groups], 3 * n,
                         lambda x, y, c, chips: [(px, py, c) for px, py in chips], name, collective_id)


def _share_reduced(groups, name, collective_id):
    n = len(groups)

    def exchange(srcs, outs, send_sems, recv_sems):
        x, y, c, _ = _place()
        cps = [_remote(srcs[g], outs[g], send_sems, recv_sems, g, (x, y, 1 - c)) for g in range(n)]
        for cp in cps:
            cp.start()
        for cp in cps:
            cp.wait()

    return _on_sequencer(exchange, groups, [jax.ShapeDtypeStruct(g.shape, g.dtype) for g in groups], n, _sibling_only,
                         name, collective_id)


def _allreduce_small(v):
    n_dev = 2 * N_CHIPS

    def body(src, out, buf, send_sems, recv_sems):
        x, y, c, _ = _place()
        me = 4 * x + 2 * y + c
        buf[me] = src[...]
        flips = [(k >> 2 & 1, k >> 1 & 1, k & 1) for k in range(1, n_dev)]
        sends = []
        for k, (fx, fy, fc) in enumerate(flips):
            to = ((1 - x) if fx else x, (1 - y) if fy else y, (1 - c) if fc else c)
            cp = _remote(src, buf.at[me], send_sems, recv_sems, k, to)
            cp.start()
            sends.append((cp, to))
        for k, (cp, to) in enumerate(sends):
            slot = buf.at[4 * to[0] + 2 * to[1] + to[2]]
            _remote(slot, slot, send_sems, recv_sems, k, to).wait_recv()
        for cp, _ in sends:
            cp.wait_send()
        acc = buf[0]
        for d in range(1, n_dev):
            acc = acc + buf[d]
        out[...] = acc

    vm = pl.BlockSpec(memory_space=pltpu.VMEM)
    return pl.pallas_call(
        body, name="comm_allreduce_small", in_specs=[vm], out_specs=vm, out_shape=jax.ShapeDtypeStruct(v.shape, F32),
        scratch_shapes=[pltpu.VMEM((n_dev,) + v.shape, F32), pltpu.SemaphoreType.DMA((n_dev - 1,)),
                        pltpu.SemaphoreType.DMA((n_dev - 1,))],
    )(v)


def _row_tile(rows, cap=320):
    return max(t for t in range(16, cap + 1, 16) if rows % t == 0)


def _add_sibling(mine, theirs, core, name):
    _, n, R, C = mine.shape
    t = _row_tile(R)

    def body(core_ref, a_ref, b_ref, o_ref, ob_ref):
        tot = a_ref[...] + b_ref[...]
        o_ref[...] = tot
        ob_ref[...] = tot.astype(BF16)

    sp = pl.BlockSpec((None, t, C), lambda k, i, core_ref: (k, i, 0))
    grid_spec = pltpu.PrefetchScalarGridSpec(
        num_scalar_prefetch=1, grid=(n, R // t),
        in_specs=[pl.BlockSpec((None, None, t, C), lambda k, i, core_ref: (core_ref[0], k, i, 0)), sp], out_specs=[sp, sp])
    return pl.pallas_call(body, name=name, grid_spec=grid_spec,
                          out_shape=[jax.ShapeDtypeStruct((n, R, C), F32), jax.ShapeDtypeStruct((n, R, C), BF16)],
                          compiler_params=_params("parallel", "parallel"))(core, mine, theirs)


def _add_chips(received, own, chip, name):
    n, R, C = received.shape
    t = _row_tile(R)

    def body(chip_ref, r_ref, o_ref, out_ref):
        acc = o_ref[...]
        for k in range(n):
            acc = acc + r_ref[k].astype(F32)
        out_ref[...] = acc

    grid_spec = pltpu.PrefetchScalarGridSpec(
        num_scalar_prefetch=1, grid=(R // t,),
        in_specs=[pl.BlockSpec((n, t, C), lambda i, chip_ref: (0, i, 0)), pl.BlockSpec((None, t, C), lambda i, chip_ref: (chip_ref[0], i, 0))],
        out_specs=pl.BlockSpec((t, C), lambda i, chip_ref: (i, 0)))
    return pl.pallas_call(body, name=name, grid_spec=grid_spec, out_shape=jax.ShapeDtypeStruct((R, C), F32),
                          compiler_params=_params("parallel"))(chip, received, own)


def _adamw(w, g, m, v, name, g_row0=0):
    R, C = w.shape
    t = math.gcd(math.gcd(R, g_row0), 128) if R % 8 == 0 else R
    assert g_row0 % t == 0
    c1 = 1.0 - ADAM_B1 ** ADAM_STEP
    c2 = 1.0 - ADAM_B2 ** ADAM_STEP

    def body(w_ref, g_ref, m_ref, v_ref, go_ref, d_ref, nm_ref, nv_ref):
        gv = g_ref[...]
        nm = ADAM_B1 * m_ref[...] + (1.0 - ADAM_B1) * gv
        nv = ADAM_B2 * v_ref[...] + (1.0 - ADAM_B2) * (gv * gv)
        go_ref[...] = gv
        d_ref[...] = -ADAM_LR * ((nm / c1) / (jnp.sqrt(nv / c2) + ADAM_EPS) + ADAM_WD * w_ref[...])
        nm_ref[...] = nm
        nv_ref[...] = nv

    sp = pl.BlockSpec((t, C), lambda i: (i, 0))
    g_sp = pl.BlockSpec((t, C), lambda i: (i + g_row0 // t, 0))
    return pl.pallas_call(body, name=name, grid=(R // t,), in_specs=[sp, g_sp, sp, sp], out_specs=[sp] * 4,
                          out_shape=[jax.ShapeDtypeStruct((R, C), F32)] * 4, compiler_params=_params("parallel"))(w, g, m, v)


def kernel(x, mem, positions, g_pre_mix, w_in, b_gate, mla_q_norm, w_uq, mla_kv_norm, w_ukv, g_mem, w_mem_kv, w_br_mla, w_br_dil, w_br_mem, w_o, g_post_mix, g_pre_ffn, w_ffn_up, conv_w, conv_b, w_ffn_down, g_post_ffn, loss_target, m_g_pre_mix, m_w_in, m_b_gate, m_mla_q_norm, m_w_uq, m_mla_kv_norm, m_w_ukv, m_g_mem, m_w_mem_kv, m_w_br_mla, m_w_br_dil, m_w_br_mem, m_w_o, m_g_post_mix, m_g_pre_ffn, m_w_ffn_up, m_conv_w, m_conv_b, m_w_ffn_down, m_g_post_ffn, v_g_pre_mix, v_w_in, v_b_gate, v_mla_q_norm, v_w_uq, v_mla_kv_norm, v_w_ukv, v_g_mem, v_w_mem_kv, v_w_br_mla, v_w_br_dil, v_w_br_mem, v_w_o, v_g_post_mix, v_g_pre_ffn, v_w_ffn_up, v_conv_w, v_conv_b, v_w_ffn_down, v_g_post_ffn):
    w_args = (g_pre_mix, w_in, b_gate, mla_q_norm, w_uq, mla_kv_norm, w_ukv, g_mem, w_mem_kv, w_br_mla, w_br_dil, w_br_mem, w_o,
              g_post_mix, g_pre_ffn, w_ffn_up, conv_w, conv_b, w_ffn_down, g_post_ffn)
    m_args = (m_g_pre_mix, m_w_in, m_b_gate, m_mla_q_norm, m_w_uq, m_mla_kv_norm, m_w_ukv, m_g_mem, m_w_mem_kv, m_w_br_mla,
              m_w_br_dil, m_w_br_mem, m_w_o, m_g_post_mix, m_g_pre_ffn, m_w_ffn_up, m_conv_w, m_conv_b, m_w_ffn_down, m_g_post_ffn)
    v_args = (v_g_pre_mix, v_w_in, v_b_gate, v_mla_q_norm, v_w_uq, v_mla_kv_norm, v_w_ukv, v_g_mem, v_w_mem_kv, v_w_br_mla,
              v_w_br_dil, v_w_br_mem, v_w_o, v_g_post_mix, v_g_pre_ffn, v_w_ffn_up, v_conv_w, v_conv_b, v_w_ffn_down, v_g_post_ffn)
    sharded = {name for grp in GROUPS for name, _, _ in grp} | {CONV_W[0]}

    def local(a, name):
        return a[0] if name in sharded else a

    w = {n: local(a, n) for n, a in zip(WEIGHTS, w_args)}
    m = {n: local(a, n) for n, a in zip(WEIGHTS, m_args)}
    v = {n: local(a, n) for n, a in zip(WEIGHTS, v_args)}

    mine = _my_weight_groups(w)
    head, _ = _gather_weights(mine[:1], [], "comm_gather_w_in", 0)
    rest, (conv_all,) = _gather_weights(mine[1:], [w[CONV_W[0]]], "comm_gather_rest", 1)
    full = _full_weights(head + rest, conv_all)
    full.update({name: w[name] for name, _ in REPLICATED})

    loss_local, grad_x, G = _local_step(x[0], mem[0], positions, loss_target[0], full)

    core = lax.axis_index("c").astype(jnp.int32).reshape(1)
    chip = (2 * lax.axis_index("x") + lax.axis_index("y")).astype(jnp.int32).reshape(1)
    first = lax.axis_index("c") == 0
    all_mine = _grad_groups(G)
    shard_groups = [None] * len(GROUPS)
    for tag, which, cid in (("early", EARLY_GROUPS, 2), ("late", LATE_GROUPS, 5)):
        mine = [all_mine[i] for i in which]
        theirs = _swap_halves(mine, f"comm_swap_{tag}", cid)
        partial = [_add_sibling(a, b, core, f"add_sibling_{i}") for i, a, b in zip(which, mine, theirs)]
        received = _scatter_partials([p16 for _, p16 in partial], f"comm_scatter_{tag}", cid + 1)
        reduced = [_add_chips(r, p32, chip, f"add_chips_{i}") for i, r, (p32, _) in zip(which, received, partial)]
        for i, a, b in zip(which, reduced, _share_reduced(reduced, f"comm_share_{tag}", cid + 2)):
            shard_groups[i] = jnp.concatenate([jnp.where(first, a, b), jnp.where(first, b, a)], axis=0)
    small = _allreduce_small(_pack_small(G, G[CONV_W[0]], loss_local))
    flat = small.reshape(-1)
    loss = flat[LOSS_AT]
    conv_g = flat[CONV_AT:LOSS_AT].reshape(CONV_W[1])
    conv_cols = CONV_W[1][1] // N_CHIPS
    conv_g = lax.dynamic_slice_in_dim(conv_g, (2 * lax.axis_index("x") + lax.axis_index("y")) * conv_cols, conv_cols, axis=1)

    grads, deltas, new_m, new_v = {}, {}, {}, {}
    for grp, g_all in zip(GROUPS, shard_groups):
        off = 0
        for name, _, _ in grp:
            grads[name], deltas[name], new_m[name], new_v[name] = _adamw(w[name], g_all, m[name], v[name], "adamw_" + name, off)
            off += w[name].shape[0]
    name = CONV_W[0]
    grads[name], deltas[name], new_m[name], new_v[name] = _adamw(w[name], conv_g, m[name], v[name], "adamw_" + name)
    packed = _adamw(_pack_small(w), small, _pack_small(m), _pack_small(v), "adamw_small")
    for dst, packed_small in zip((grads, deltas, new_m, new_v), packed):
        dst.update(_unpack_small(packed_small))

    def out(d, name):
        return d[name][None] if name in sharded else d[name]

    return (loss, grad_x[None], *[out(grads, n) for n in WEIGHTS], *[out(deltas, n) for n in WEIGHTS],
            *[out(new_m, n) for n in WEIGHTS], *[out(new_v, n) for n in WEIGHTS])
```

```python
import functools
import math

import jax
import jax.numpy as jnp
from jax import lax
from jax.experimental import pallas as pl
from jax.experimental.pallas import tpu as pltpu
from jax.experimental.pallas import tpu_sc as plsc

F32 = jnp.float32
BF16 = jnp.bfloat16

D_MODEL = 1024
N_MEM = 256
RMS_EPS = 1e-6
NEG_INF = -1e30
MLA_HEADS = 8
MLA_NOPE = 64
MLA_ROPE = 32
MLA_QK = 96
Q_RANK = 384
KV_RANK = 256
ROPE_THETA = 10000.0
DIL_PAIRS = ((128, 1), (512, 4), (2048, 16))
DIL_GROUPS = 3
DIL_HPG = 4
DIL_HEADS = 12
DIL_W = 512
MEM_HEADS = 4
MEM_W = 512
D_FF = 2816
OFF_Q = 384
OFF_KV = 640
OFF_KR = 672
OFF_DIL = 5280
OFF_MEMQ = 5792
D_IN = 8864
ADAM_LR = 0.001
ADAM_B1 = 0.9
ADAM_B2 = 0.999
ADAM_EPS = 1e-08
ADAM_WD = 0.01
ADAM_STEP = 10

LANES = 128
VMEM_LIMIT = 56 * 1024 * 1024

N_CHIPS = 4
ROW_TILE = 256

NN = (((1,), (0,)), ((), ()))
NT = (((1,), (1,)), ((), ()))
TN = (((0,), (0,)), ((), ()))


def _params(*sem):
    return pltpu.CompilerParams(dimension_semantics=sem, vmem_limit_bytes=VMEM_LIMIT)


def _full(shape):
    return pl.BlockSpec(shape, lambda *_: (0,) * len(shape))


def _matmul(a, b, *, mode="nn", out_dtype=F32, tm=1024, tn=1024, tk=None, add=None, scale=None, name):
    if mode == "nn":
        (M, K), N = a.shape, b.shape[1]
    elif mode == "nt":
        (M, K), N = a.shape, b.shape[0]
    else:
        (K, M), N = a.shape, b.shape[1]
    tm, tn = min(tm, M), min(tn, N)
    tk = K if tk is None else min(tk, K)
    assert M % tm == 0 and N % tn == 0 and K % tk == 0, (name, M, N, K, tm, tn, tk)
    a_spec = pl.BlockSpec((tk, tm), lambda i, j, k: (k, i)) if mode == "tn" else pl.BlockSpec((tm, tk), lambda i, j, k: (i, k))
    b_spec = pl.BlockSpec((tn, tk), lambda i, j, k: (j, k)) if mode == "nt" else pl.BlockSpec((tk, tn), lambda i, j, k: (k, j))
    o_spec = pl.BlockSpec((tm, tn), lambda i, j, k: (i, j))
    return _matmul_blocks(a, b, mode=mode, grid=(M // tm, N // tn, K // tk), a_spec=a_spec, b_spec=b_spec, o_spec=o_spec,
                          out_shape=(M, N), out_dtype=out_dtype, add=add,
                          scale=None if scale is None else (lambda j: j < scale[0], scale[1]), name=name)


def _matmul_blocks(a, b, *, mode, grid, a_spec, b_spec, o_spec, out_shape, out_dtype=F32, add=None, scale=None, name):
    nk = grid[2]
    dims = {"nn": NN, "nt": NT, "tn": TN}[mode]
    tm, tn = o_spec.block_shape
    has_add = add is not None

    def body(*refs):
        a_ref, b_ref = refs[0], refs[1]
        c_ref = refs[2] if has_add else None
        o_ref = refs[3] if has_add else refs[2]
        part = lax.dot_general(a_ref[...].astype(BF16), b_ref[...].astype(BF16), dims, preferred_element_type=F32)
        if scale is not None:
            assert nk == 1 and not has_add
            part = part * jnp.where(scale[0](pl.program_id(1)), scale[1], 1.0)
        if nk == 1:
            if has_add:
                part = part + c_ref[...]
            o_ref[...] = part.astype(out_dtype)
        else:
            acc = refs[-1]
            k = pl.program_id(2)

            @pl.when(k == 0)
            def _():
                acc[...] = part

            @pl.when(k > 0)
            def _():
                acc[...] += part

            @pl.when(k == nk - 1)
            def _():
                r = acc[...]
                if has_add:
                    r = r + c_ref[...]
                o_ref[...] = r.astype(out_dtype)

    in_specs = [a_spec, b_spec] + ([o_spec] if has_add else [])
    args = (a, b) + ((add,) if has_add else ())
    return pl.pallas_call(
        body, name=name, grid=grid, in_specs=in_specs, out_specs=o_spec,
        out_shape=jax.ShapeDtypeStruct(out_shape, out_dtype),
        scratch_shapes=[pltpu.VMEM((tm, tn), F32)] if nk > 1 else [],
        compiler_params=_params("parallel", "parallel", "arbitrary"),
    )(*args)


def _proj_view(h_view, w, d, q_scale, name):
    L, K = h_view.shape[0], h_view.shape[1] // d
    N = w.shape[1]
    tn = N // 3
    tm = min(1024, L)
    return _matmul_blocks(
        h_view, w, mode="nn", grid=(L // tm, 3 * d, 1), a_spec=pl.BlockSpec((tm, K), lambda i, j, k: (i, j // 3)),
        b_spec=pl.BlockSpec((K, tn), lambda i, j, k: (0, j % 3)), o_spec=pl.BlockSpec((tm, tn), lambda i, j, k: (i, j)),
        out_shape=(L, d * N), out_dtype=BF16, scale=(lambda j: j % 3 == 0, q_scale), name=name)


def _dh_view(dp_view, w, d, add, name):
    L, N = dp_view.shape[0], dp_view.shape[1] // d
    K = w.shape[0]
    tm = min(1024, L)
    return _matmul_blocks(
        dp_view, w, mode="nt", grid=(L // tm, d, 1), a_spec=pl.BlockSpec((tm, N), lambda i, j, k: (i, j)),
        b_spec=pl.BlockSpec((K, N), lambda i, j, k: (0, 0)), o_spec=pl.BlockSpec((tm, K), lambda i, j, k: (i, j)),
        out_shape=(L, d * K), add=add, name=name)


def _dw_view(h_view, dp_view, d, name):
    L, K = h_view.shape[0], h_view.shape[1] // d
    N = dp_view.shape[1] // d
    tk = min(1024, L)
    nl = L // tk
    return _matmul_blocks(
        h_view, dp_view, mode="tn", grid=(1, 1, d * nl), a_spec=pl.BlockSpec((tk, K), lambda i, j, k: (k % nl, k // nl)),
        b_spec=pl.BlockSpec((tk, N), lambda i, j, k: (k % nl, k // nl)), o_spec=pl.BlockSpec((K, N), lambda i, j, k: (0, 0)),
        out_shape=(K, N), name=name)


def _rms_fwd_val(x, g):
    r = lax.rsqrt(jnp.mean(x * x, axis=-1, keepdims=True) + RMS_EPS)
    return (x * r) * g


def _rms_bwd_val(dy, x, g):
    r = lax.rsqrt(jnp.mean(x * x, axis=-1, keepdims=True) + RMS_EPS)
    xn = x * r
    gdy = g * dy
    dx = r * (gdy - xn * jnp.mean(gdy * xn, axis=-1, keepdims=True))
    return dx, dy * xn


def _rope_tables(pos, invf, inverse):
    ang = pos * invf
    cos, sin = jnp.cos(ang), jnp.sin(ang)
    lane = lax.broadcasted_iota(jnp.int32, ang.shape, 1)
    first = (lane >= MLA_NOPE) & (lane < MLA_NOPE + MLA_ROPE // 2)
    second = (lane >= MLA_NOPE + MLA_ROPE // 2) & (lane < MLA_QK)
    sgn = -1.0 if inverse else 1.0
    sa = jnp.where(first, -sgn * sin, 0.0)
    sb = jnp.where(second, sgn * sin, 0.0)
    return cos, sa, sb


def _rope_val(x, cos, sa, sb):
    half = MLA_ROPE // 2
    return x * cos + pltpu.roll(x, LANES - half, 1) * sa + pltpu.roll(x, half, 1) * sb


def _head_sum_bcast(v, n_heads):
    parts = []
    for h in range(n_heads):
        s = jnp.sum(v[:, h * LANES:(h + 1) * LANES], axis=1, keepdims=True)
        parts.append(jnp.broadcast_to(s, (v.shape[0], LANES)))
    return parts


def _row_spec(t, w):
    return pl.BlockSpec((t, w), lambda i: (i, 0))


def _acc_spec(w, rows=1):
    return pl.BlockSpec((rows, w), lambda i: (0, 0))


def _rmsnorm(x, g, out_dtype, name):
    S, W = x.shape
    t = min(ROW_TILE, S)

    def body(x_ref, g_ref, o_ref):
        o_ref[...] = _rms_fwd_val(x_ref[...], g_ref[...]).astype(out_dtype)

    return pl.pallas_call(body, name=name, grid=(S // t,), in_specs=[_row_spec(t, W), _acc_spec(W)],
                          out_specs=_row_spec(t, W), out_shape=jax.ShapeDtypeStruct((S, W), out_dtype),
                          compiler_params=_params("parallel"))(x, g)


def _mla_prep(proj_a, pos, invf, q_norm, kv_norm):
    S = proj_a.shape[0]
    t = ROW_TILE

    def body(a_ref, pos_ref, invf_ref, qn_ref, kvn_ref, cq_ref, ckv_ref, kpe_ref):
        a = a_ref[...]
        cq_ref[...] = _rms_fwd_val(a[:, 0:Q_RANK], qn_ref[...]).astype(BF16)
        ckv_ref[...] = _rms_fwd_val(a[:, Q_RANK + LANES:], kvn_ref[...]).astype(BF16)
        cos, sa, sb = _rope_tables(pos_ref[...], invf_ref[...], False)
        kpe_ref[...] = _rope_val(a[:, Q_RANK:Q_RANK + LANES], cos, sa, sb)

    return pl.pallas_call(
        body, name="mla_prep", grid=(S // t,),
        in_specs=[_row_spec(t, 768), _row_spec(t, 1), _acc_spec(LANES), _acc_spec(Q_RANK), _acc_spec(KV_RANK)],
        out_specs=[_row_spec(t, Q_RANK), _row_spec(t, KV_RANK), _row_spec(t, LANES)],
        out_shape=[jax.ShapeDtypeStruct((S, Q_RANK), BF16), jax.ShapeDtypeStruct((S, KV_RANK), BF16),
                   jax.ShapeDtypeStruct((S, LANES), F32)],
        compiler_params=_params("parallel"))(proj_a, pos, invf, q_norm, kv_norm)


def _qk_final(q_pre, k_pre, kpe, pos, invf):
    S, W = q_pre.shape
    t = ROW_TILE
    scale = MLA_QK ** -0.5

    def body(q_ref, k_ref, kpe_ref, pos_ref, invf_ref, qo_ref, ko_ref):
        cos, sa, sb = _rope_tables(pos_ref[...], invf_ref[...], False)
        kpe_v = kpe_ref[...]
        for h in range(MLA_HEADS):
            sl = slice(h * LANES, (h + 1) * LANES)
            qo_ref[:, sl] = (_rope_val(q_ref[:, sl], cos, sa, sb) * scale).astype(BF16)
            ko_ref[:, sl] = (k_ref[:, sl] + kpe_v).astype(BF16)

    return pl.pallas_call(
        body, name="qk_final", grid=(S // t,),
        in_specs=[_row_spec(t, W), _row_spec(t, W), _row_spec(t, LANES), _row_spec(t, 1), _acc_spec(LANES)],
        out_specs=[_row_spec(t, W), _row_spec(t, W)],
        out_shape=[jax.ShapeDtypeStruct((S, W), BF16)] * 2,
        compiler_params=_params("parallel"))(q_pre, k_pre, kpe, pos, invf)


def _mla_bwd_prep(dq, dk, pos, invf):
    S, W = dq.shape
    t = ROW_TILE
    scale = MLA_QK ** -0.5

    def body(dq_ref, dk_ref, pos_ref, invf_ref, dqp_ref, dkpe_ref):
        cos, sa, sb = _rope_tables(pos_ref[...], invf_ref[...], True)
        tot = jnp.zeros((t, LANES), F32)
        for h in range(MLA_HEADS):
            sl = slice(h * LANES, (h + 1) * LANES)
            dqp_ref[:, sl] = _rope_val(dq_ref[:, sl] * scale, cos, sa, sb).astype(BF16)
            tot = tot + dk_ref[:, sl]
        lane = lax.broadcasted_iota(jnp.int32, tot.shape, 1)
        tot = jnp.where((lane >= MLA_NOPE) & (lane < MLA_QK), tot, 0.0)
        dkpe_ref[...] = _rope_val(tot, cos, sa, sb)

    return pl.pallas_call(
        body, name="mla_bwd_prep", grid=(S // t,),
        in_specs=[_row_spec(t, W), _row_spec(t, W), _row_spec(t, 1), _acc_spec(LANES)],
        out_specs=[_row_spec(t, W), _row_spec(t, LANES)],
        out_shape=[jax.ShapeDtypeStruct((S, W), BF16), jax.ShapeDtypeStruct((S, LANES), F32)],
        compiler_params=_params("parallel"))(dq, dk, pos, invf)


def _mla_norm_bwd(dcq, dckv_a, dckv_b, dkpe, proj_a, q_norm, kv_norm):
    S = proj_a.shape[0]
    t = ROW_TILE

    def body(dcq_ref, da_ref, db_ref, dkpe_ref, a_ref, qn_ref, kvn_ref, o_ref, dqn_ref, dkvn_ref):
        i = pl.program_id(0)
        a = a_ref[...]
        dxq, gq = _rms_bwd_val(dcq_ref[...], a[:, 0:Q_RANK], qn_ref[...])
        dxkv, gkv = _rms_bwd_val(da_ref[...] + db_ref[...], a[:, Q_RANK + LANES:], kvn_ref[...])
        o_ref[:, 0:Q_RANK] = dxq.astype(BF16)
        o_ref[:, Q_RANK:Q_RANK + LANES] = dkpe_ref[...].astype(BF16)
        o_ref[:, Q_RANK + LANES:] = dxkv.astype(BF16)

        @pl.when(i == 0)
        def _():
            dqn_ref[...] = jnp.zeros_like(dqn_ref)
            dkvn_ref[...] = jnp.zeros_like(dkvn_ref)

        dqn_ref[...] += jnp.sum(gq, axis=0, keepdims=True)
        dkvn_ref[...] += jnp.sum(gkv, axis=0, keepdims=True)

    return pl.pallas_call(
        body, name="mla_norm_bwd", grid=(S // t,),
        in_specs=[_row_spec(t, Q_RANK), _row_spec(t, KV_RANK), _row_spec(t, KV_RANK), _row_spec(t, LANES),
                  _row_spec(t, 768), _acc_spec(Q_RANK), _acc_spec(KV_RANK)],
        out_specs=[_row_spec(t, 768), _acc_spec(Q_RANK), _acc_spec(KV_RANK)],
        out_shape=[jax.ShapeDtypeStruct((S, 768), BF16), jax.ShapeDtypeStruct((1, Q_RANK), F32),
                   jax.ShapeDtypeStruct((1, KV_RANK), F32)],
        compiler_params=_params("arbitrary"))(dcq, dckv_a, dckv_b, dkpe, proj_a, q_norm, kv_norm)


DILATIONS = tuple(d for _, d in DIL_PAIRS)


def _to_tokens(view, scr, d):
    if d == 1:
        return view
    n, w = view.shape[0], view.shape[1] // d
    for r in range(d):
        for c in range(w // LANES):
            scr[pl.ds(c, 1), pl.ds(r, n, stride=d), :] = view[:, r * w + c * LANES:r * w + (c + 1) * LANES][None]
    return jnp.concatenate([scr[c] for c in range(w // LANES)], axis=1)


def _from_tokens(tok, scr, d, out_ref):
    if d == 1:
        out_ref[...] = tok.astype(out_ref.dtype)
        return
    n, w = tok.shape[0] // d, tok.shape[1]
    for c in range(w // LANES):
        scr[c] = tok[:, c * LANES:(c + 1) * LANES]
    for r in range(d):
        for c in range(w // LANES):
            out_ref[:, r * w + c * LANES:r * w + (c + 1) * LANES] = scr[pl.ds(c, 1), pl.ds(r, n, stride=d), :][0].astype(out_ref.dtype)


def _token_scratch(t, w):
    return pltpu.VMEM((w // LANES, t, LANES), F32)


def _view_spec(t, d):
    return pl.BlockSpec((t // d, d * DIL_W), lambda i: (i, 0))


def _mix_weights(ls):
    m = jnp.maximum(jnp.maximum(ls[0], ls[1]), ls[2])
    es = [jnp.exp(l - m) for l in ls]
    den = es[0] + es[1] + es[2]
    return [e / den for e in es]


def _dil_mix(o_list, lse_list):
    S = o_list[0].shape[0] * DILATIONS[0]
    t = ROW_TILE
    specs = [_view_spec(t, d) for d in DILATIONS]

    def body(o0, o1, o2, l0, l1, l2, y_ref, *scr):
        os_ = [_to_tokens(r[...], scr[g], d) for g, (r, d) in enumerate(zip((o0, o1, o2), DILATIONS))]
        ws = _mix_weights([_to_tokens(r[...], scr[3 + g], d) for g, (r, d) in enumerate(zip((l0, l1, l2), DILATIONS))])
        y_ref[...] = (ws[0] * os_[0] + ws[1] * os_[1] + ws[2] * os_[2]).astype(BF16)

    return pl.pallas_call(
        body, name="dil_mix", grid=(S // t,), in_specs=specs * 2, out_specs=_row_spec(t, DIL_W),
        out_shape=jax.ShapeDtypeStruct((S, DIL_W), BF16), scratch_shapes=[_token_scratch(t, DIL_W)] * 6,
        compiler_params=_params("parallel"))(*o_list, *lse_list)


def _dil_mix_bwd(dy, o_list, lse_list):
    S = dy.shape[0]
    t = ROW_TILE
    specs = [_view_spec(t, d) for d in DILATIONS]

    def body(dy_ref, o0, o1, o2, l0, l1, l2, d0, d1, d2, e0, e1, e2, *scr):
        os_ = [_to_tokens(r[...], scr[g], d) for g, (r, d) in enumerate(zip((o0, o1, o2), DILATIONS))]
        ws = _mix_weights([_to_tokens(r[...], scr[3 + g], d) for g, (r, d) in enumerate(zip((l0, l1, l2), DILATIONS))])
        dyv = dy_ref[...]
        y = ws[0] * os_[0] + ws[1] * os_[1] + ws[2] * os_[2]
        b = jnp.concatenate(_head_sum_bcast(dyv * y, DIL_HPG), axis=1)
        for g, (w, d_ref, e_ref, d) in enumerate(zip(ws, (d0, d1, d2), (e0, e1, e2), DILATIONS)):
            _from_tokens(w * dyv, scr[6 + g], d, d_ref)
            _from_tokens(w * b, scr[9 + g], d, e_ref)

    shapes = [(S // d, d * DIL_W) for d in DILATIONS]
    return pl.pallas_call(
        body, name="dil_mix_bwd", grid=(S // t,), in_specs=[_row_spec(t, DIL_W)] + specs * 2, out_specs=specs * 2,
        out_shape=[jax.ShapeDtypeStruct(s, BF16) for s in shapes] + [jax.ShapeDtypeStruct(s, F32) for s in shapes],
        scratch_shapes=[_token_scratch(t, DIL_W)] * 12, compiler_params=_params("parallel"))(dy, *o_list, *lse_list)


def _delta(do, o, n_heads, name):
    S, W = do.shape
    t = ROW_TILE

    def body(do_ref, o_ref, d_ref):
        prod = do_ref[...].astype(F32) * o_ref[...].astype(F32)
        d_ref[...] = jnp.concatenate(_head_sum_bcast(prod, n_heads), axis=1)

    return pl.pallas_call(body, name=name, grid=(S // t,), in_specs=[_row_spec(t, W)] * 2, out_specs=_row_spec(t, W),
                          out_shape=jax.ShapeDtypeStruct((S, W), F32), compiler_params=_params("parallel"))(do, o)


def _merge(proj_g, b_gate, b_list):
    S = proj_g.shape[0]
    t = ROW_TILE

    def body(g_ref, b_ref, y0, y1, y2, o_ref):
        acc = jnp.zeros((t, D_MODEL), F32)
        for i, y in enumerate((y0, y1, y2)):
            sl = slice(i * D_MODEL, (i + 1) * D_MODEL)
            acc = acc + jax.nn.sigmoid(g_ref[:, sl] + b_ref[:, sl]) * y[...]
        o_ref[...] = acc.astype(BF16)

    return pl.pallas_call(
        body, name="merge", grid=(S // t,),
        in_specs=[_row_spec(t, 3 * D_MODEL), _acc_spec(3 * D_MODEL)] + [_row_spec(t, D_MODEL)] * 3,
        out_specs=_row_spec(t, D_MODEL), out_shape=jax.ShapeDtypeStruct((S, D_MODEL), BF16),
        compiler_params=_params("parallel"))(proj_g, b_gate, *b_list)


def _merge_bwd(dmerged, proj_g, b_gate, b_list):
    S = proj_g.shape[0]
    t = ROW_TILE

    def body(dm_ref, g_ref, b_ref, y0, y1, y2, d0, d1, d2, dz_ref, db_ref):
        i = pl.program_id(0)

        @pl.when(i == 0)
        def _():
            db_ref[...] = jnp.zeros_like(db_ref)

        dm = dm_ref[...]
        for k, (y, d_ref) in enumerate(zip((y0, y1, y2), (d0, d1, d2))):
            sl = slice(k * D_MODEL, (k + 1) * D_MODEL)
            s = jax.nn.sigmoid(g_ref[:, sl] + b_ref[:, sl])
            d_ref[...] = (s * dm).astype(BF16)
            dz = dm * y[...] * (s * (1.0 - s))
            dz_ref[:, sl] = dz.astype(BF16)
            db_ref[:, sl] += jnp.sum(dz, axis=0, keepdims=True)

    return pl.pallas_call(
        body, name="merge_bwd", grid=(S // t,),
        in_specs=[_row_spec(t, D_MODEL), _row_spec(t, 3 * D_MODEL), _acc_spec(3 * D_MODEL)] + [_row_spec(t, D_MODEL)] * 3,
        out_specs=[_row_spec(t, D_MODEL)] * 3 + [_row_spec(t, 3 * D_MODEL), _acc_spec(3 * D_MODEL)],
        out_shape=[jax.ShapeDtypeStruct((S, D_MODEL), BF16)] * 3
        + [jax.ShapeDtypeStruct((S, 3 * D_MODEL), BF16), jax.ShapeDtypeStruct((1, 3 * D_MODEL), F32)],
        compiler_params=_params("arbitrary"))(dmerged, proj_g, b_gate, *b_list)


def _norm2(o, x, g_post, g_pre):
    S = x.shape[0]
    t = ROW_TILE

    def body(o_ref, x_ref, gp_ref, gf_ref, x1_ref, h2_ref):
        x1 = x_ref[...] + _rms_fwd_val(o_ref[...], gp_ref[...])
        x1_ref[...] = x1
        h2_ref[...] = _rms_fwd_val(x1, gf_ref[...]).astype(BF16)

    return pl.pallas_call(
        body, name="norm2", grid=(S // t,),
        in_specs=[_row_spec(t, D_MODEL)] * 2 + [_acc_spec(D_MODEL)] * 2, out_specs=[_row_spec(t, D_MODEL)] * 2,
        out_shape=[jax.ShapeDtypeStruct((S, D_MODEL), F32), jax.ShapeDtypeStruct((S, D_MODEL), BF16)],
        compiler_params=_params("parallel"))(o, x, g_post, g_pre)


def _norm2_bwd(dx2, dh2, x1, o, g_pre, g_post):
    S = x1.shape[0]
    t = ROW_TILE

    def body(dx2_ref, dh2_ref, x1_ref, o_ref, gf_ref, gp_ref, dx1_ref, do_ref, dgf_ref, dgp_ref):
        i = pl.program_id(0)

        @pl.when(i == 0)
        def _():
            dgf_ref[...] = jnp.zeros_like(dgf_ref)
            dgp_ref[...] = jnp.zeros_like(dgp_ref)

        d1, gf = _rms_bwd_val(dh2_ref[...], x1_ref[...], gf_ref[...])
        dx1 = dx2_ref[...] + d1
        dx1_ref[...] = dx1
        do, gp = _rms_bwd_val(dx1, o_ref[...], gp_ref[...])
        do_ref[...] = do.astype(BF16)
        dgf_ref[...] += jnp.sum(gf, axis=0, keepdims=True)
        dgp_ref[...] += jnp.sum(gp, axis=0, keepdims=True)

    return pl.pallas_call(
        body, name="norm2_bwd", grid=(S // t,),
        in_specs=[_row_spec(t, D_MODEL)] * 4 + [_acc_spec(D_MODEL)] * 2,
        out_specs=[_row_spec(t, D_MODEL)] * 2 + [_acc_spec(D_MODEL)] * 2,
        out_shape=[jax.ShapeDtypeStruct((S, D_MODEL), F32), jax.ShapeDtypeStruct((S, D_MODEL), BF16),
                   jax.ShapeDtypeStruct((1, D_MODEL), F32), jax.ShapeDtypeStruct((1, D_MODEL), F32)],
        compiler_params=_params("arbitrary"))(dx2, dh2, x1, o, g_pre, g_post)


def _norm1_bwd(dx1, dh_list, x, g):
    S = x.shape[0]
    t = ROW_TILE
    dils = (1,) + DILATIONS
    assert len(dh_list) == len(dils)

    def body(dx1_ref, *refs):
        dh_refs, (x_ref, g_ref, dx_ref, dg_ref), scr = refs[:len(dils)], refs[len(dils):len(dils) + 4], refs[len(dils) + 4:]
        i = pl.program_id(0)

        @pl.when(i == 0)
        def _():
            dg_ref[...] = jnp.zeros_like(dg_ref)

        dh = dh_refs[0][...]
        for k in range(1, len(dils)):
            dh = dh + _to_tokens(dh_refs[k][...], scr[k - 1], dils[k])
        d, gg = _rms_bwd_val(dh, x_ref[...], g_ref[...])
        dx_ref[...] = dx1_ref[...] + d
        dg_ref[...] += jnp.sum(gg, axis=0, keepdims=True)

    dh_specs = [pl.BlockSpec((t // d, d * D_MODEL), lambda i: (i, 0)) for d in dils]
    return pl.pallas_call(
        body, name="norm1_bwd", grid=(S // t,),
        in_specs=[_row_spec(t, D_MODEL)] + dh_specs + [_row_spec(t, D_MODEL), _acc_spec(D_MODEL)],
        out_specs=[_row_spec(t, D_MODEL), _acc_spec(D_MODEL)],
        out_shape=[jax.ShapeDtypeStruct((S, D_MODEL), F32), jax.ShapeDtypeStruct((1, D_MODEL), F32)],
        scratch_shapes=[_token_scratch(t, D_MODEL)] * (len(dils) - 1),
        compiler_params=_params("arbitrary"))(dx1, *dh_list, x, g)


def _gain_grad(dy, x, name):
    R, W = x.shape

    def body(dy_ref, x_ref, dg_ref):
        xv = x_ref[...]
        r = lax.rsqrt(jnp.mean(xv * xv, axis=-1, keepdims=True) + RMS_EPS)
        dg_ref[...] = jnp.sum(dy_ref[...] * (xv * r), axis=0, keepdims=True)

    return pl.pallas_call(body, name=name, grid=(1,), in_specs=[_full((R, W))] * 2, out_specs=_full((1, W)),
                          out_shape=jax.ShapeDtypeStruct((1, W), F32), compiler_params=_params("arbitrary"))(dy, x)


def _loss_head(f, x1, tgt, g):
    S = f.shape[0]
    t = ROW_TILE

    def body(f_ref, x1_ref, t_ref, g_ref, loss_ref, dx2_ref, df_ref, dg_ref):
        i = pl.program_id(0)

        @pl.when(i == 0)
        def _():
            loss_ref[...] = jnp.zeros_like(loss_ref)
            dg_ref[...] = jnp.zeros_like(dg_ref)

        fv, gv = f_ref[...], g_ref[...]
        err = x1_ref[...] + _rms_fwd_val(fv, gv) - t_ref[...]
        part = jnp.sum(jnp.mean(err * err, axis=-1, keepdims=True), axis=0, keepdims=True)
        loss_ref[...] += jnp.broadcast_to(0.5 * part, loss_ref.shape)
        dx2 = err * (1.0 / D_MODEL)
        dx2_ref[...] = dx2
        df, gg = _rms_bwd_val(dx2, fv, gv)
        df_ref[...] = df.astype(BF16)
        dg_ref[...] += jnp.sum(gg, axis=0, keepdims=True)

    return pl.pallas_call(
        body, name="loss_head", grid=(S // t,),
        in_specs=[_row_spec(t, D_MODEL)] * 3 + [_acc_spec(D_MODEL)],
        out_specs=[_acc_spec(LANES, 8), _row_spec(t, D_MODEL), _row_spec(t, D_MODEL), _acc_spec(D_MODEL)],
        out_shape=[jax.ShapeDtypeStruct((8, LANES), F32), jax.ShapeDtypeStruct((S, D_MODEL), F32),
                   jax.ShapeDtypeStruct((S, D_MODEL), BF16), jax.ShapeDtypeStruct((1, D_MODEL), F32)],
        compiler_params=_params("arbitrary"))(f, x1, tgt, g)


CONV_TC = 1408
CONV_TT = 256
HALO = 8


def _shift_down(u, halo, first):
    row = lax.broadcasted_iota(jnp.int32, u.shape, 0)
    h6 = jnp.where(first, 0.0, halo[HALO - 2:HALO - 1, :])
    h7 = jnp.where(first, 0.0, halo[HALO - 1:HALO, :])
    s1 = jnp.where(row == 0, h7, pltpu.roll(u, 1, 0))
    s2 = jnp.where(row == 0, h6, jnp.where(row == 1, h7, pltpu.roll(u, 2, 0)))
    return s1, s2


def _conv_specs(tt, n_c, n_t, lead):
    def halo_row(i):
        return jnp.maximum(i * (tt // HALO) - 1, 0) if lead else jnp.minimum((i + 1) * (tt // HALO), n_t * (tt // HALO) - 1)
    return [
        pl.BlockSpec((tt, CONV_TC), lambda j, i: (i, j)),
        pl.BlockSpec((tt, CONV_TC), lambda j, i: (i, j + n_c)),
        pl.BlockSpec((HALO, CONV_TC), lambda j, i: (halo_row(i), j)),
        pl.BlockSpec((HALO, CONV_TC), lambda j, i: (halo_row(i), j + n_c)),
    ]


def _conv_z(ug, uv, hg, hv, w_g, w_v, b_g, b_v, first):
    g1, g2 = _shift_down(ug, hg, first)
    v1, v2 = _shift_down(uv, hv, first)
    zg = b_g + w_g[0:1, :] * g2
    zg = zg + w_g[1:2, :] * g1
    zg = zg + w_g[2:3, :] * ug
    zv = b_v + w_v[0:1, :] * v2
    zv = zv + w_v[1:2, :] * v1
    zv = zv + w_v[2:3, :] * uv
    return zg, zv, (g2, g1, ug), (v2, v1, uv)


def _conv_fwd(u, conv_w, conv_b):
    S = u.shape[0]
    tt = min(CONV_TT, S)
    n_c, n_t = D_FF // CONV_TC, S // tt
    wspec = [pl.BlockSpec((3, CONV_TC), lambda j, i: (0, j)), pl.BlockSpec((3, CONV_TC), lambda j, i: (0, j + n_c)),
             pl.BlockSpec((1, CONV_TC), lambda j, i: (0, j)), pl.BlockSpec((1, CONV_TC), lambda j, i: (0, j + n_c))]

    def body(ug_ref, uv_ref, hg_ref, hv_ref, wg_ref, wv_ref, bg_ref, bv_ref, a_ref):
        first = pl.program_id(1) == 0
        zg, zv, _, _ = _conv_z(ug_ref[...], uv_ref[...], hg_ref, hv_ref, wg_ref, wv_ref, bg_ref[...], bv_ref[...], first)
        a_ref[...] = (zg * jax.nn.sigmoid(zg) * zv).astype(BF16)

    return pl.pallas_call(
        body, name="conv_fwd", grid=(n_c, n_t), in_specs=_conv_specs(tt, n_c, n_t, True) + wspec,
        out_specs=pl.BlockSpec((tt, CONV_TC), lambda j, i: (i, j)), out_shape=jax.ShapeDtypeStruct((S, D_FF), BF16),
        compiler_params=_params("parallel", "parallel"))(u, u, u, u, conv_w, conv_w, conv_b, conv_b)


def _gate_bwd(da, zg, zv):
    sg = jax.nn.sigmoid(zg)
    return da * zv * (sg * (1.0 + zg * (1.0 - sg))), da * (zg * sg)


def _conv_bwd(da, u, conv_w, conv_b):
    S = u.shape[0]
    tt = min(CONV_TT, S)
    n_c, n_t = D_FF // CONV_TC, S // tt
    wspec = [pl.BlockSpec((3, CONV_TC), lambda j, i: (0, j)), pl.BlockSpec((3, CONV_TC), lambda j, i: (0, j + n_c)),
             pl.BlockSpec((1, CONV_TC), lambda j, i: (0, j)), pl.BlockSpec((1, CONV_TC), lambda j, i: (0, j + n_c))]
    tile = pl.BlockSpec((tt, CONV_TC), lambda j, i: (i, j))
    trail = pl.BlockSpec((HALO, CONV_TC), lambda j, i: (jnp.minimum((i + 1) * (tt // HALO), n_t * (tt // HALO) - 1), j))

    def body(da_ref, ug_ref, uv_ref, hg_ref, hv_ref, tda_ref, tg_ref, tv_ref, wg_ref, wv_ref, bg_ref, bv_ref,
             dug_ref, duv_ref, dwg_ref, dwv_ref, dbg_ref, dbv_ref):
        i = pl.program_id(1)
        last = i == n_t - 1
        bg, bv = bg_ref[...], bv_ref[...]
        zg, zv, gs, vs = _conv_z(ug_ref[...], uv_ref[...], hg_ref, hv_ref, wg_ref, wv_ref, bg, bv, i == 0)
        dzg, dzv = _gate_bwd(da_ref[...], zg, zv)
        tzg, tzv, _, _ = _conv_z(tg_ref[...], tv_ref[...], ug_ref.at[pl.ds(tt - HALO, HALO), :], uv_ref.at[pl.ds(tt - HALO, HALO), :],
                                 wg_ref, wv_ref, bg, bv, False)
        tdzg, tdzv = _gate_bwd(tda_ref[...], tzg, tzv)
        row = lax.broadcasted_iota(jnp.int32, dzg.shape, 0)
        for dz, tdz, w_ref, du_ref in ((dzg, tdzg, wg_ref, dug_ref), (dzv, tdzv, wv_ref, duv_ref)):
            h0 = jnp.where(last, 0.0, tdz[0:1, :])
            h1 = jnp.where(last, 0.0, tdz[1:2, :])
            u1 = jnp.where(row == tt - 1, h0, pltpu.roll(dz, tt - 1, 0))
            u2 = jnp.where(row == tt - 1, h1, jnp.where(row == tt - 2, h0, pltpu.roll(dz, tt - 2, 0)))
            du_ref[...] = (w_ref[2:3, :] * dz + w_ref[1:2, :] * u1 + w_ref[0:1, :] * u2).astype(BF16)

        @pl.when(i == 0)
        def _():
            for r in (dwg_ref, dwv_ref, dbg_ref, dbv_ref):
                r[...] = jnp.zeros_like(r)

        for k in range(3):
            dwg_ref[k:k + 1, :] += jnp.sum(dzg * gs[k], axis=0, keepdims=True)
            dwv_ref[k:k + 1, :] += jnp.sum(dzv * vs[k], axis=0, keepdims=True)
        dbg_ref[...] += jnp.sum(dzg, axis=0, keepdims=True)
        dbv_ref[...] += jnp.sum(dzv, axis=0, keepdims=True)

    lead = _conv_specs(tt, n_c, n_t, True)
    trail_v = pl.BlockSpec((HALO, CONV_TC), lambda j, i: (jnp.minimum((i + 1) * (tt // HALO), n_t * (tt // HALO) - 1), j + n_c))
    outs = pl.pallas_call(
        body, name="conv_bwd", grid=(n_c, n_t), in_specs=[tile] + lead + [trail, trail, trail_v] + wspec,
        out_specs=[tile, tile] + [pl.BlockSpec((3, CONV_TC), lambda j, i: (0, j))] * 2 + [pl.BlockSpec((1, CONV_TC), lambda j, i: (0, j))] * 2,
        out_shape=[jax.ShapeDtypeStruct((S, D_FF), BF16)] * 2 + [jax.ShapeDtypeStruct((3, D_FF), F32)] * 2
        + [jax.ShapeDtypeStruct((1, D_FF), F32)] * 2,
        compiler_params=_params("parallel", "arbitrary"))(da, u, u, u, u, da, u, u, conv_w, conv_w, conv_b, conv_b)
    dug, duv, dwg, dwv, dbg, dbv = outs
    return dug, duv, jnp.concatenate([dwg, dwv], axis=1), jnp.concatenate([dbg, dbv], axis=1)


BAND = 128


MEM_TQ = 512


def _mem_fwd(q, kv, *, name):
    S, W = q.shape
    M = kv.shape[0]
    nh = W // LANES
    tq = min(MEM_TQ, S)

    def body(q_ref, k_ref, v_ref, o_ref, l_ref):
        s = lax.dot_general(q_ref[...], k_ref[...].astype(BF16), NT, preferred_element_type=F32)
        m = jnp.max(s, axis=1, keepdims=True)
        p = jnp.exp(s - m)
        l = jnp.sum(p, axis=1, keepdims=True)
        o_ref[...] = (lax.dot_general(p.astype(BF16), v_ref[...].astype(BF16), NN, preferred_element_type=F32) / l).astype(BF16)
        l_ref[...] = jnp.broadcast_to(m + jnp.log(l), (tq, LANES))

    blk = pl.BlockSpec((tq, LANES), lambda hh, i: (i, hh))
    return pl.pallas_call(
        body, name=name, grid=(nh, S // tq),
        in_specs=[blk, pl.BlockSpec((M, LANES), lambda hh, i: (0, hh)), pl.BlockSpec((M, LANES), lambda hh, i: (0, hh + nh))],
        out_specs=[blk, blk], out_shape=[jax.ShapeDtypeStruct((S, W), BF16), jax.ShapeDtypeStruct((S, W), F32)],
        compiler_params=_params("parallel", "parallel"))(q, kv, kv)


def _mem_bwd(q, kv, do, lse, delta, *, scale, name):
    S, W = q.shape
    M = kv.shape[0]
    nh = W // LANES
    tq = min(MEM_TQ, S)

    def body(q_ref, k_ref, v_ref, do_ref, l_ref, d_ref, dq_ref, dk_ref, dv_ref):
        i = pl.program_id(1)

        @pl.when(i == 0)
        def _():
            dk_ref[...] = jnp.zeros_like(dk_ref)
            dv_ref[...] = jnp.zeros_like(dv_ref)

        qv = q_ref[...]
        kv_, vv = k_ref[...].astype(BF16), v_ref[...].astype(BF16)
        dov = do_ref[...].astype(BF16)
        s = lax.dot_general(qv, kv_, NT, preferred_element_type=F32)
        p = jnp.exp(s - l_ref[...][:, 0:1])
        dp = lax.dot_general(dov, vv, NT, preferred_element_type=F32)
        ds = (p * (dp - d_ref[...][:, 0:1])).astype(BF16)
        dq_ref[...] = lax.dot_general(ds, kv_, NN, preferred_element_type=F32) * scale
        dk_ref[...] += lax.dot_general(ds, qv, TN, preferred_element_type=F32)
        dv_ref[...] += lax.dot_general(p.astype(BF16), dov, TN, preferred_element_type=F32)

    blk = pl.BlockSpec((tq, LANES), lambda hh, i: (i, hh))
    kblk = pl.BlockSpec((M, LANES), lambda hh, i: (0, hh))
    vblk = pl.BlockSpec((M, LANES), lambda hh, i: (0, hh + nh))
    dq, dk, dv = pl.pallas_call(
        body, name=name, grid=(nh, S // tq), in_specs=[blk, kblk, vblk, blk, blk, blk], out_specs=[blk, kblk, kblk],
        out_shape=[jax.ShapeDtypeStruct((S, W), F32), jax.ShapeDtypeStruct((M, W), F32), jax.ShapeDtypeStruct((M, W), F32)],
        compiler_params=_params("parallel", "arbitrary"))(q, kv, kv, do, lse, delta)
    return dq, jnp.concatenate([dk, dv], axis=1)


CAUSAL_BLOCK = 512
STRIP = 32


def _causal_fwd(q, k, v, *, name):
    S, W = q.shape
    T = min(CAUSAL_BLOCK, S // 2)
    n_strips = T // STRIP

    def body(q_ref, k_ref, v_ref, o_ref, lse_ref, s0, s1, p0, p1, a0, a1, acc_scr):
        i = pl.program_id(1)
        s_scr, p_scr, a_scr = (s0, s1), (p0, p1), (a0, a1)

        def rows(j):
            return pl.ds(pl.multiple_of(j * T, T), T)

        def scores(j, slot):
            s_scr[slot][...] = lax.dot_general(q_ref[...], k_ref[rows(j), :], NT, preferred_element_type=F32)

        def softmax(slot, stats, diag):
            def strip(r):
                s = s_scr[slot][r * STRIP:(r + 1) * STRIP, :]
                if diag:
                    row = r * STRIP + lax.broadcasted_iota(jnp.int32, s.shape, 0)
                    s = jnp.where(row >= lax.broadcasted_iota(jnp.int32, s.shape, 1), s, NEG_INF)
                return s

            m_new = [jnp.maximum(m_old, jnp.max(strip(r), axis=1, keepdims=True)) for r, (m_old, _) in enumerate(stats)]
            new = []
            for r, (m_old, l_old) in enumerate(stats):
                rs = slice(r * STRIP, (r + 1) * STRIP)
                p = jnp.exp(strip(r) - m_new[r])
                alpha = jnp.exp(m_old - m_new[r])
                new.append((m_new[r], alpha * l_old + jnp.sum(p, axis=1, keepdims=True)))
                a_scr[slot][rs, :] = alpha
                p_scr[slot][rs, :] = p.astype(BF16)
            return tuple(new)

        def values(j, slot):
            acc_scr[...] = a_scr[slot][...] * acc_scr[...] + lax.dot_general(p_scr[slot][...], v_ref[rows(j), :], NN,
                                                                            preferred_element_type=F32)

        def trip(j, stats, mine, other):
            scores(j + 1, other)
            stats = softmax(mine, stats, False)
            values(jnp.maximum(j - 1, 0), other)
            return stats

        def pair(jj, stats):
            return trip(2 * jj + 1, trip(2 * jj, stats, 0, 1), 1, 0)

        def last(stats, mine, other):
            values(jnp.maximum(i - 1, 0), other)
            stats = softmax(mine, stats, True)
            values(i, mine)
            for r, (m, l) in enumerate(stats):
                rs = slice(r * STRIP, (r + 1) * STRIP)
                o_ref[rs, :] = (acc_scr[rs, :] / l).astype(BF16)
                lse_ref[rs, :] = jnp.broadcast_to(m + jnp.log(l), (STRIP, LANES))

        acc_scr[...] = jnp.zeros_like(acc_scr)
        p1[...] = jnp.zeros_like(p1)
        a1[...] = jnp.ones_like(a1)
        scores(0, 0)
        init = tuple((jnp.full((STRIP, 1), NEG_INF, F32), jnp.zeros((STRIP, 1), F32)) for _ in range(n_strips))
        stats = lax.fori_loop(0, i // 2, pair, init)

        @pl.when(i % 2 == 1)
        def _():
            last(trip(i - 1, stats, 0, 1), 1, 0)

        @pl.when(i % 2 == 0)
        def _():
            last(stats, 0, 1)

    blk = pl.BlockSpec((T, LANES), lambda hh, i: (i, hh))
    whole = pl.BlockSpec((S, LANES), lambda hh, i: (0, hh))
    return pl.pallas_call(
        body, name=name, grid=(W // LANES, S // T), in_specs=[blk, whole, whole], out_specs=[blk, blk],
        out_shape=[jax.ShapeDtypeStruct((S, W), BF16), jax.ShapeDtypeStruct((S, W), F32)],
        scratch_shapes=[pltpu.VMEM((T, T), F32)] * 2 + [pltpu.VMEM((T, T), BF16)] * 2 + [pltpu.VMEM((T, 1), F32)] * 2
        + [pltpu.VMEM((T, LANES), F32)],
        compiler_params=_params("parallel", "arbitrary"))(q, k, v)


def _causal_bwd(q, k, v, do, lse, delta, *, name):
    S, W = q.shape
    T = min(CAUSAL_BLOCK, S // 2)
    nq = S // T
    n_strips, n_col = T // STRIP, T // LANES

    def body(q_ref, k_ref, v_ref, do_ref, l_ref, d_ref, dq_ref, dk_ref, dv_ref, s0, s1, e0, e1, p0, p1, g0, g1):
        j = pl.program_id(1)
        s_scr, e_scr, p_scr, g_scr = (s0, s1), (e0, e1), (p0, p1), (g0, g1)

        @pl.when(j == 0)
        def _():
            dq_ref[...] = jnp.zeros_like(dq_ref)

        dk_ref[...] = jnp.zeros_like(dk_ref)
        dv_ref[...] = jnp.zeros_like(dv_ref)

        def rows(i):
            return pl.ds(pl.multiple_of(jnp.minimum(i, nq - 1) * T, T), T)

        def products(i, slot):
            r = rows(i)
            s_scr[slot][...] = lax.dot_general(q_ref[r, :], k_ref[...], NT, preferred_element_type=F32)
            e_scr[slot][...] = lax.dot_general(do_ref[r, :].astype(BF16), v_ref[...], NT, preferred_element_type=F32)

        def pointwise(i, slot, diag):
            base = pl.multiple_of(i * T, T)
            for r in range(n_strips):
                rs = slice(r * STRIP, (r + 1) * STRIP)
                lse_r = l_ref[pl.ds(base + r * STRIP, STRIP), :]
                del_r = d_ref[pl.ds(base + r * STRIP, STRIP), :]
                for c in range(n_col):
                    cs = slice(c * LANES, (c + 1) * LANES)
                    if diag and c * LANES > (r + 1) * STRIP - 1:
                        p_scr[slot][rs, cs] = jnp.zeros((STRIP, LANES), BF16)
                        g_scr[slot][rs, cs] = jnp.zeros((STRIP, LANES), BF16)
                        continue
                    sv = s_scr[slot][rs, cs]
                    if diag and (c + 1) * LANES - 1 > r * STRIP:
                        row = r * STRIP + lax.broadcasted_iota(jnp.int32, sv.shape, 0)
                        col = c * LANES + lax.broadcasted_iota(jnp.int32, sv.shape, 1)
                        sv = jnp.where(row >= col, sv, NEG_INF)
                    p = jnp.exp(sv - lse_r)
                    p_scr[slot][rs, cs] = p.astype(BF16)
                    g_scr[slot][rs, cs] = (p * (e_scr[slot][rs, cs] - del_r)).astype(BF16)

        def gradients(i, slot):
            r = rows(i)
            qi, doi = q_ref[r, :], do_ref[r, :].astype(BF16)
            g = g_scr[slot][...]
            dv_ref[...] += lax.dot_general(p_scr[slot][...], doi, TN, preferred_element_type=F32)
            dk_ref[...] += lax.dot_general(g, qi, TN, preferred_element_type=F32)
            dq_ref[r, :] += lax.dot_general(g, k_ref[...], NN, preferred_element_type=F32)

        products(j, 0)
        products(j + 1, 1)
        pointwise(j, 0, True)

        def trip(i, mine, other):
            products(i + 1, other)
            pointwise(i, mine, False)
            gradients(i - 1, other)

        def pair(t, _):
            trip(j + 1 + 2 * t, 1, 0)
            trip(j + 2 + 2 * t, 0, 1)
            return 0

        n_rest = nq - 1 - j
        lax.fori_loop(0, n_rest // 2, pair, 0)

        @pl.when(n_rest % 2 == 1)
        def _():
            trip(nq - 1, 1, 0)
            gradients(nq - 1, 1)

        @pl.when(n_rest % 2 == 0)
        def _():
            gradients(nq - 1, 0)

    blk = pl.BlockSpec((T, LANES), lambda hh, j: (j, hh))
    whole = pl.BlockSpec((S, LANES), lambda hh, j: (0, hh))
    return pl.pallas_call(
        body, name=name, grid=(W // LANES, S // T), in_specs=[whole, blk, blk, whole, whole, whole],
        out_specs=[whole, blk, blk], out_shape=[jax.ShapeDtypeStruct((S, W), F32)] * 3,
        scratch_shapes=[pltpu.VMEM((T, T), F32)] * 4 + [pltpu.VMEM((T, T), BF16)] * 4,
        compiler_params=_params("parallel", "arbitrary"))(q, k, v, do, lse, delta)


BAND_TQ = 512


def _band_window(i, sub, nsub, L, q_ref, k_ref, v_ref, slope):
    kw = min(2 * BAND, L)
    n = i * nsub + sub
    k0 = 0 if kw == L else pl.multiple_of(jnp.maximum(n - 1, 0) * BAND, BAND)
    win = pl.ds(k0, kw)
    qs = q_ref[sub * BAND:(sub + 1) * BAND, :]
    kwv, vwv = k_ref[win, :], v_ref[win, :]
    s = lax.dot_general(qs, kwv, NT, preferred_element_type=F32)
    dist = (n * BAND + lax.broadcasted_iota(jnp.int32, s.shape, 0)) - (k0 + lax.broadcasted_iota(jnp.int32, s.shape, 1))
    s = jnp.where((dist >= 0) & (dist <= BAND), s - slope * dist.astype(F32), NEG_INF)
    return win, qs, kwv, vwv, s


def _band_fwd(q, k, v, slopes, *, n_heads, qcol, kcol, vcol, L, slope_mul, o_shape, name):
    tq = min(BAND_TQ, L)
    nsub = tq // BAND

    def body(sl_ref, q_ref, k_ref, v_ref, o_ref, l_ref):
        hh, i = pl.program_id(0), pl.program_id(1)
        slope = sl_ref[hh % DIL_HPG] * slope_mul
        wins = [_band_window(i, sub, nsub, L, q_ref, k_ref, v_ref, slope) for sub in range(nsub)]
        ms = [jnp.max(w[4], axis=1, keepdims=True) for w in wins]
        ps = [jnp.exp(w[4] - m) for w, m in zip(wins, ms)]
        ls = [jnp.sum(p, axis=1, keepdims=True) for p in ps]
        for sub, (w, m, p, l) in enumerate(zip(wins, ms, ps, ls)):
            rows = slice(sub * BAND, (sub + 1) * BAND)
            o_ref[rows, :] = lax.dot_general(p.astype(BF16), w[3], NN, preferred_element_type=F32) / l
            l_ref[rows, :] = jnp.broadcast_to(m + jnp.log(l), (BAND, LANES))

    whole = lambda col: pl.BlockSpec((L, LANES), lambda hh, i: (0, col(hh)))
    o_spec = pl.BlockSpec((tq, LANES), lambda hh, i: (i, hh))
    return pl.pallas_call(
        body, name=name, grid=(n_heads, L // tq),
        in_specs=[pl.BlockSpec(memory_space=pltpu.SMEM), pl.BlockSpec((tq, LANES), lambda hh, i: (i, qcol(hh))), whole(kcol), whole(vcol)],
        out_specs=[o_spec, o_spec], out_shape=[jax.ShapeDtypeStruct(o_shape, F32)] * 2,
        compiler_params=_params("parallel", "arbitrary"))(slopes, q, k, v)


def _band_bwd(q, k, v, do, lse, delta, slopes, *, n_heads, qcol, kcol, vcol, L, scale, slope_mul, d_shape, name):
    tq = min(BAND_TQ, L)
    nsub = tq // BAND
    n_steps = L // tq

    def body(sl_ref, q_ref, k_ref, v_ref, do_ref, l_ref, d_ref, dq_ref, dk_ref, dv_ref, dk_acc, dv_acc):
        hh, i = pl.program_id(0), pl.program_id(1)
        slope = sl_ref[hh % DIL_HPG] * slope_mul

        @pl.when(i == 0)
        def _():
            dk_acc[...] = jnp.zeros_like(dk_acc)
            dv_acc[...] = jnp.zeros_like(dv_acc)

        blocks = range(nsub)
        rows = [slice(sub * BAND, (sub + 1) * BAND) for sub in blocks]
        wins = [_band_window(i, sub, nsub, L, q_ref, k_ref, v_ref, slope) for sub in blocks]
        dos = [do_ref[r, :] for r in rows]
        dps = [lax.dot_general(do, w[3], NT, preferred_element_type=F32) for do, w in zip(dos, wins)]
        ps = [jnp.exp(w[4] - l_ref[r, :][:, 0:1]) for w, r in zip(wins, rows)]
        dss = [(p * (dp - d_ref[r, :][:, 0:1])).astype(BF16) for p, dp, r in zip(ps, dps, rows)]
        for r, ds, w in zip(rows, dss, wins):
            dq_ref[r, :] = (lax.dot_general(ds, w[2], NN, preferred_element_type=F32) * scale).astype(BF16)
        dks = [lax.dot_general(ds, w[1], TN, preferred_element_type=F32) for ds, w in zip(dss, wins)]
        dvs = [lax.dot_general(p.astype(BF16), do, TN, preferred_element_type=F32) for p, do in zip(ps, dos)]
        for w, dk, dv in zip(wins, dks, dvs):
            dk_acc[w[0], :] += dk
            dv_acc[w[0], :] += dv

        @pl.when(i == n_steps - 1)
        def _():
            dk_ref[...] = dk_acc[...].astype(BF16)
            dv_ref[...] = dv_acc[...].astype(BF16)

    whole = lambda col: pl.BlockSpec((L, LANES), lambda hh, i: (0, col(hh)))
    blk = pl.BlockSpec((tq, LANES), lambda hh, i: (i, hh))
    ident = lambda hh: hh
    return pl.pallas_call(
        body, name=name, grid=(n_heads, n_steps),
        in_specs=[pl.BlockSpec(memory_space=pltpu.SMEM), pl.BlockSpec((tq, LANES), lambda hh, i: (i, qcol(hh))), whole(kcol), whole(vcol),
                  blk, blk, blk],
        out_specs=[blk, whole(ident), whole(ident)], out_shape=[jax.ShapeDtypeStruct(d_shape, BF16)] * 3,
        scratch_shapes=[pltpu.VMEM((L, LANES), F32)] * 2,
        compiler_params=_params("parallel", "arbitrary"))(slopes, q, k, v, do, lse, delta)


def _pad_heads(w, n_heads, width, axis):
    shp = w.shape
    new = shp[:axis] + (n_heads, width) + shp[axis + 1:]
    pad = [(0, 0)] * len(new)
    pad[axis + 1] = (0, LANES - width)
    out = jnp.pad(w.reshape(new), pad)
    return out.reshape(shp[:axis] + (n_heads * LANES,) + shp[axis + 1:])


def _unpad_heads(w, n_heads, width, axis):
    shp = w.shape
    new = shp[:axis] + (n_heads, LANES) + shp[axis + 1:]
    out = lax.slice_in_dim(w.reshape(new), 0, width, axis=axis + 1)
    return out.reshape(shp[:axis] + (n_heads * width,) + shp[axis + 1:])


def _alibi_slopes():
    s = jnp.exp2(-8.0 * jnp.arange(1, DIL_HEADS + 1, dtype=F32) / DIL_HEADS)
    return s.reshape(DIL_HPG, DIL_GROUPS).T


def _local_step(x, mem, positions, tgt, W):
    S = x.shape[0]
    pos = positions.reshape(S, 1).astype(F32)
    half = MLA_ROPE // 2
    inv_freq = ROPE_THETA ** (-jnp.arange(half, dtype=F32) / half)
    invf = jnp.zeros((1, LANES), F32).at[0, MLA_NOPE:MLA_NOPE + half].set(inv_freq).at[0, MLA_NOPE + half:MLA_QK].set(inv_freq)
    slopes = _alibi_slopes()

    w_in = W["w_in"]
    zc = lambda n: jnp.zeros((D_MODEL, n), BF16)
    w_a = jnp.concatenate([w_in[:, :OFF_Q], zc(MLA_NOPE), w_in[:, OFF_KV:OFF_KR], zc(LANES - MLA_QK), w_in[:, OFF_Q:OFF_KV]], axis=1)
    w_d, w_m, w_g = w_in[:, OFF_KR:OFF_DIL], w_in[:, OFF_DIL:OFF_MEMQ], w_in[:, OFF_MEMQ:]
    w_uq_p = _pad_heads(W["w_uq"], MLA_HEADS, MLA_QK, 1)
    ukv = W["w_ukv"].reshape(KV_RANK, MLA_HEADS, 2 * MLA_NOPE)
    w_uk_p = _pad_heads(ukv[:, :, :MLA_NOPE].reshape(KV_RANK, -1), MLA_HEADS, MLA_NOPE, 1)
    w_uv_p = _pad_heads(ukv[:, :, MLA_NOPE:].reshape(KV_RANK, -1), MLA_HEADS, MLA_NOPE, 1)
    w_br_mla_p = _pad_heads(W["w_br_mla"], MLA_HEADS, MLA_NOPE, 0)

    h = _rmsnorm(x, W["g_pre_mix"], BF16, "norm1")
    proj_a = _matmul(h, w_a, tn=768, name="proj_a")
    att_scale = LANES ** -0.5
    w_dg = [jnp.concatenate([w_d[:, (part * DIL_GROUPS + g) * DIL_W:(part * DIL_GROUPS + g + 1) * DIL_W] for part in range(3)], axis=1)
            for g in range(DIL_GROUPS)]
    proj_m = _matmul(h, w_m, tn=MEM_W, out_dtype=BF16, scale=(1, att_scale), name="proj_m")
    proj_g = _matmul(h, w_g, tn=1536, name="proj_g")

    cq_n, ckv_n, kpe = _mla_prep(proj_a, pos, invf, W["mla_q_norm"], W["mla_kv_norm"])
    q_pre = _matmul(cq_n, w_uq_p, tn=1024, name="mla_q")
    k_pre = _matmul(ckv_n, w_uk_p, tn=1024, name="mla_k")
    v_mla = _matmul(ckv_n, w_uv_p, tn=1024, out_dtype=BF16, name="mla_v")
    q_mla, k_mla = _qk_final(q_pre, k_pre, kpe, pos, invf)
    ident = lambda hh: hh
    o_mla, lse_mla = _causal_fwd(q_mla, k_mla, v_mla, name="attn_mla_fwd")

    col = lambda part: (lambda hh: (hh // DIL_HPG) * 3 * DIL_HPG + part * DIL_HPG + hh % DIL_HPG)
    h_views = [h.reshape(S // d, d * D_MODEL) for d in DILATIONS]
    proj_dg, o_dil, lse_dil = [], [], []
    for g, d in enumerate(DILATIONS):
        pv = _proj_view(h_views[g], w_dg[g], d, att_scale, f"proj_d{g}")
        o, lse = _band_fwd(pv, pv, pv, slopes[g], n_heads=d * DIL_HPG, qcol=col(0), kcol=col(1), vcol=col(2), L=S // d,
                           slope_mul=float(d), o_shape=(S // d, d * DIL_W), name=f"attn_dil{g}_fwd")
        proj_dg.append(pv)
        o_dil.append(o)
        lse_dil.append(lse)
    y_dil = _dil_mix(o_dil, lse_dil)

    mem_n = _rmsnorm(mem, W["g_mem"], BF16, "mem_norm")
    kv_mem = _matmul(mem_n, W["w_mem_kv"], name="mem_kv")
    o_mem, lse_mem = _mem_fwd(proj_m, kv_mem, name="attn_mem_fwd")

    b_mla = _matmul(o_mla, w_br_mla_p, name="br_mla")
    b_dil = _matmul(y_dil, W["w_br_dil"], name="br_dil")
    b_mem = _matmul(o_mem, W["w_br_mem"], name="br_mem")
    merged = _merge(proj_g, W["b_gate"], [b_mla, b_dil, b_mem])
    o_proj = _matmul(merged, W["w_o"], name="o_proj")
    x1, h2 = _norm2(o_proj, x, W["g_post_mix"], W["g_pre_ffn"])

    u = _matmul(h2, W["w_ffn_up"], tn=1408, name="ffn_up")
    act = _conv_fwd(u, W["conv_w"], W["conv_b"])
    f = _matmul(act, W["w_ffn_down"], name="ffn_down")
    loss8, dx2, df, dg_post_ffn = _loss_head(f, x1, tgt, W["g_post_ffn"])
    loss = loss8[0, 0]

    G = {"g_post_ffn": dg_post_ffn}
    d_act = _matmul(df, W["w_ffn_down"], mode="nt", tn=1408, name="d_act")
    G["w_ffn_down"] = _matmul(act, df, mode="tn", tm=1408, tk=1024, name="dw_ffn_down")
    du_g, du_v, G["conv_w"], G["conv_b"] = _conv_bwd(d_act, u, W["conv_w"], W["conv_b"])
    dh2 = _matmul(du_g, W["w_ffn_up"][:, :D_FF], mode="nt", name="d_h2_gate")
    dh2 = _matmul(du_v, W["w_ffn_up"][:, D_FF:], mode="nt", add=dh2, name="d_h2_value")
    G["w_ffn_up"] = jnp.concatenate([_matmul(h2, du_g, mode="tn", tn=1408, tk=1024, name="dw_ffn_up_gate"),
                                     _matmul(h2, du_v, mode="tn", tn=1408, tk=1024, name="dw_ffn_up_value")], axis=1)
    dx1, do_proj, G["g_pre_ffn"], G["g_post_mix"] = _norm2_bwd(dx2, dh2, x1, o_proj, W["g_pre_ffn"], W["g_post_mix"])
    dmerged = _matmul(do_proj, W["w_o"], mode="nt", name="d_merged")
    G["w_o"] = _matmul(merged, do_proj, mode="tn", tk=1024, name="dw_o")
    db_mla, db_dil, db_mem, dproj_g, G["b_gate"] = _merge_bwd(dmerged, proj_g, W["b_gate"], [b_mla, b_dil, b_mem])

    dy_mem = _matmul(db_mem, W["w_br_mem"], mode="nt", name="d_y_mem")
    G["w_br_mem"] = _matmul(o_mem, db_mem, mode="tn", tk=1024, name="dw_br_mem")
    delta_mem = _delta(dy_mem, o_mem, MEM_HEADS, "delta_mem")
    dq_mem, dkv_mem = _mem_bwd(proj_m, kv_mem, dy_mem, lse_mem, delta_mem, scale=att_scale, name="attn_mem_bwd")
    G["w_mem_kv"] = _matmul(mem_n, dkv_mem, mode="tn", name="dw_mem_kv")
    dmem_n = _matmul(dkv_mem, W["w_mem_kv"], mode="nt", name="d_mem_n")
    G["g_mem"] = _gain_grad(dmem_n, mem, "dg_mem")

    dy_dil = _matmul(db_dil, W["w_br_dil"], mode="nt", name="d_y_dil")
    G["w_br_dil"] = _matmul(y_dil, db_dil, mode="tn", tk=1024, name="dw_br_dil")
    mix = _dil_mix_bwd(dy_dil, o_dil, lse_dil)
    do_dil, dl_dil = mix[:3], mix[3:]
    dh_views, dw_groups = [], []
    for g, d in enumerate(DILATIONS):
        pv = proj_dg[g]
        parts = _band_bwd(pv, pv, pv, do_dil[g], lse_dil[g], dl_dil[g], slopes[g], n_heads=d * DIL_HPG, qcol=col(0),
                          kcol=col(1), vcol=col(2), L=S // d, scale=att_scale, slope_mul=float(d), d_shape=(S // d, d * DIL_W),
                          name=f"attn_dil{g}_bwd")
        dpv = jnp.concatenate([p[:, r * DIL_W:(r + 1) * DIL_W] for r in range(d) for p in parts], axis=1)
        dh_views.append(_dh_view(dpv, w_dg[g], d, None, f"d_h_d{g}"))
        dw_groups.append(_dw_view(h_views[g], dpv, d, f"dw_in_d{g}"))
    dw_d = jnp.concatenate([dw[:, part * DIL_W:(part + 1) * DIL_W] for part in range(3) for dw in dw_groups], axis=1)

    dy_mla = _matmul(db_mla, w_br_mla_p, mode="nt", name="d_y_mla")
    dw_br_mla_p = _matmul(o_mla, db_mla, mode="tn", tk=1024, name="dw_br_mla")
    G["w_br_mla"] = _unpad_heads(dw_br_mla_p, MLA_HEADS, MLA_NOPE, 0)
    delta_mla = _delta(dy_mla, o_mla, MLA_HEADS, "delta_mla")
    dq_mla, dk_mla, dv_mla = _causal_bwd(q_mla, k_mla, v_mla, dy_mla, lse_mla, delta_mla, name="attn_mla_bwd")
    dq_pre, dkpe = _mla_bwd_prep(dq_mla, dk_mla, pos, invf)
    dcq_n = _matmul(dq_pre, w_uq_p, mode="nt", tn=Q_RANK, name="d_cq")
    G["w_uq"] = _unpad_heads(_matmul(cq_n, dq_pre, mode="tn", tm=Q_RANK, tk=1024, name="dw_uq"), MLA_HEADS, MLA_QK, 1)
    dckv_a = _matmul(dk_mla, w_uk_p, mode="nt", tn=KV_RANK, name="d_ckv_k")
    dckv_b = _matmul(dv_mla, w_uv_p, mode="nt", tn=KV_RANK, name="d_ckv_v")
    dw_uk = _unpad_heads(_matmul(ckv_n, dk_mla, mode="tn", tm=KV_RANK, tk=1024, name="dw_uk"), MLA_HEADS, MLA_NOPE, 1)
    dw_uv = _unpad_heads(_matmul(ckv_n, dv_mla, mode="tn", tm=KV_RANK, tk=1024, name="dw_uv"), MLA_HEADS, MLA_NOPE, 1)
    G["w_ukv"] = jnp.concatenate([dw_uk.reshape(KV_RANK, MLA_HEADS, MLA_NOPE), dw_uv.reshape(KV_RANK, MLA_HEADS, MLA_NOPE)],
                                 axis=2).reshape(KV_RANK, -1)
    dproj_a, G["mla_q_norm"], G["mla_kv_norm"] = _mla_norm_bwd(dcq_n, dckv_a, dckv_b, dkpe, proj_a, W["mla_q_norm"],
                                                              W["mla_kv_norm"])

    dh = _matmul(dproj_a, w_a, mode="nt", name="d_h_a")
    dh = _matmul(dq_mem, w_m, mode="nt", add=dh, name="d_h_m")
    dh = _matmul(dproj_g, w_g, mode="nt", add=dh, name="d_h_g")
    dw_a = _matmul(h, dproj_a, mode="tn", tn=768, tk=1024, name="dw_in_a")
    dw_m = _matmul(h, dq_mem, mode="tn", tn=512, tk=1024, name="dw_in_m")
    dw_g = _matmul(h, dproj_g, mode="tn", tn=1536, tk=1024, name="dw_in_g")
    kr0 = Q_RANK + MLA_NOPE
    G["w_in"] = jnp.concatenate([dw_a[:, :Q_RANK], dw_a[:, Q_RANK + LANES:], dw_a[:, kr0:kr0 + MLA_ROPE], dw_d, dw_m, dw_g], axis=1)
    grad_x, G["g_pre_mix"] = _norm1_bwd(dx1, [dh] + dh_views, x, W["g_pre_mix"])
    return loss, grad_x, G


WEIGHTS = ["g_pre_mix", "w_in", "b_gate", "mla_q_norm", "w_uq", "mla_kv_norm", "w_ukv", "g_mem", "w_mem_kv", "w_br_mla",
           "w_br_dil", "w_br_mem", "w_o", "g_post_mix", "g_pre_ffn", "w_ffn_up", "conv_w", "conv_b", "w_ffn_down", "g_post_ffn"]
GROUPS = [
    [("w_in", (D_MODEL, D_IN), 1)],
    [("w_uq", (Q_RANK, MLA_HEADS * MLA_QK), 1)],
    [("w_ukv", (KV_RANK, MLA_HEADS * 2 * MLA_NOPE), 1)],
    [("w_br_mla", (MLA_HEADS * MLA_NOPE, D_MODEL), 1), ("w_br_dil", (DIL_W, D_MODEL), 1), ("w_br_mem", (MEM_W, D_MODEL), 1)],
    [("w_mem_kv", (D_MODEL, 2 * MEM_W), 0), ("w_o", (D_MODEL, D_MODEL), 0), ("w_ffn_down", (D_FF, D_MODEL), 0)],
    [("w_ffn_up", (D_MODEL, 2 * D_FF), 1)],
]
EARLY_GROUPS = (3, 4, 5)
LATE_GROUPS = (0, 1, 2)
CONV_W = ("conv_w", (3, 2 * D_FF), 1)
REPLICATED = [("g_pre_mix", D_MODEL), ("b_gate", 3 * D_MODEL), ("mla_q_norm", Q_RANK), ("mla_kv_norm", KV_RANK), ("g_mem", D_MODEL),
              ("g_post_mix", D_MODEL), ("g_pre_ffn", D_MODEL), ("conv_b", 2 * D_FF), ("g_post_ffn", D_MODEL)]
SMALL_ROWS = 256
CONV_AT = sum(n for _, n in REPLICATED)
LOSS_AT = CONV_AT + 3 * 2 * D_FF


def _shard_shape(shape, axis):
    return tuple(d // N_CHIPS if a == axis else d for a, d in enumerate(shape))


def _group_shape(grp):
    shapes = [_shard_shape(shape, axis) for _, shape, axis in grp]
    assert len({s[1] for s in shapes}) == 1
    return sum(s[0] for s in shapes), shapes[0][1]


def _member_shards(a, axis):
    r, c = a.shape
    if axis == 0:
        return a.reshape(N_CHIPS, r // N_CHIPS, c)
    return a.reshape(r, N_CHIPS, c // N_CHIPS).transpose(1, 0, 2)


def _member_full(s, axis):
    n, r, c = s.shape
    if axis == 0:
        return s.reshape(n * r, c)
    return s.transpose(1, 0, 2).reshape(r, n * c)


def _my_weight_groups(w):
    out = []
    for grp in GROUPS:
        rows, width = _group_shape(grp)
        out.append(jnp.concatenate([w[name].astype(BF16) for name, _, _ in grp], axis=0).reshape(2, rows // 2, width))
    return out


def _full_weights(gathered, conv_all):
    out = {}
    for grp, ga in zip(GROUPS, gathered):
        ga = ga.reshape(N_CHIPS, -1, ga.shape[-1])
        off = 0
        for name, shape, axis in grp:
            rows = _shard_shape(shape, axis)[0]
            out[name] = _member_full(ga[:, off:off + rows], axis)
            off += rows
    out[CONV_W[0]] = _member_full(conv_all, CONV_W[2])
    return out


def _grad_groups(G):
    out = []
    for grp in GROUPS:
        rows, width = _group_shape(grp)
        a = jnp.concatenate([_member_shards(G[name], axis) for name, _, axis in grp], axis=1)
        out.append(a.reshape(N_CHIPS, 2, rows // 2, width).transpose(1, 0, 2, 3))
    return out


def _pack_small(vals, conv_g=None, loss=None):
    parts = [vals[name].reshape(-1) for name, _ in REPLICATED]
    if conv_g is not None:
        parts += [conv_g.reshape(-1), loss.reshape(1)]
    flat = jnp.concatenate(parts)
    return jnp.pad(flat, (0, SMALL_ROWS * LANES - flat.shape[0])).reshape(SMALL_ROWS, LANES)


def _unpack_small(packed):
    flat = packed.reshape(-1)
    out, off = {}, 0
    for name, n in REPLICATED:
        out[name] = flat[off:off + n].reshape(1, n)
        off += n
    return out


MESH = pl.DeviceIdType.MESH
HBM_SPEC = pl.BlockSpec(memory_space=pltpu.HBM)


def _place():
    x, y, c = lax.axis_index("x"), lax.axis_index("y"), lax.axis_index("c")
    chips = [(1 - x, y), (x, 1 - y), (1 - x, 1 - y)]
    return x, y, c, chips


def _remote(src, dst, send_sems, recv_sems, k, to):
    return pltpu.make_async_remote_copy(src_ref=src, dst_ref=dst, send_sem=send_sems.at[k], recv_sem=recv_sems.at[k],
                                        device_id=to, device_id_type=MESH)


def _gather_weights(groups, wholes, name, collective_id):
    n, m = len(groups), len(wholes)
    arrays = list(groups) + list(wholes)
    hbm = pltpu.MemorySpace.HBM
    srcs = [jax.new_ref(a, memory_space=hbm) for a in arrays]
    outs = [jax.empty_ref(jax.ShapeDtypeStruct((N_CHIPS,) + a.shape, a.dtype), memory_space=hbm) for a in arrays]
    n_sem = 6 * n + 3 * m

    def launch(send_sems, recv_sems):
        x, y, c, chips = _place()
        me = 2 * x + y
        sibling = (x, y, 1 - c)
        barrier = pltpu.get_barrier_semaphore()
        peers = [(px, py, c) for px, py in chips] + [sibling]
        for peer in peers:
            pl.semaphore_signal(barrier, inc=1, device_id=peer, device_id_type=MESH)
        pl.semaphore_wait(barrier, len(peers))
        first = [_remote(srcs[g].at[c], outs[g].at[me, c], send_sems, recv_sems, g * 3 + k, (px, py, c))
                 for k, (px, py) in enumerate(chips) for g in range(n)]
        first += [_remote(srcs[n + w], outs[n + w].at[me], send_sems, recv_sems, 6 * n + w * 3 + k, (px, py, c))
                  for k, (px, py) in enumerate(chips) for w in range(m)]
        for cp in first:
            cp.start()
        passed = []
        for k, (px, py) in enumerate(chips):
            for g in range(n):
                slot = outs[g].at[2 * px + py, c]
                _remote(slot, slot, send_sems, recv_sems, g * 3 + k, (px, py, c)).wait_recv()
                cp = _remote(slot, slot, send_sems, recv_sems, 3 * n + g * 3 + k, sibling)
                cp.start()
                passed.append(cp)
        for k, (px, py) in enumerate(chips):
            for w in range(m):
                slot = outs[n + w].at[2 * px + py]
                _remote(slot, slot, send_sems, recv_sems, 6 * n + w * 3 + k, (px, py, c)).wait_recv()
            for g in range(n):
                slot = outs[g].at[2 * px + py, 1 - c]
                _remote(slot, slot, send_sems, recv_sems, 3 * n + g * 3 + k, sibling).wait_recv()
        for cp in first + passed:
            cp.wait_send()

    pl.kernel(launch, mesh=plsc.ScalarSubcoreMesh(axis_name="sequencer", num_cores=1), name=name,
              scratch_types=(pltpu.SemaphoreType.DMA((n_sem,)), pltpu.SemaphoreType.DMA((n_sem,))),
              compiler_params=pltpu.CompilerParams(collective_id=collective_id))()
    me = 2 * lax.axis_index("x") + lax.axis_index("y")
    res = [lax.dynamic_update_slice_in_dim(o[...], a[None], me, axis=0) for o, a in zip(outs, arrays)]
    return res[:n], res[n:]


def _on_sequencer(exchange, arrays, out_structs, n_sem, peers, name, collective_id):
    hbm = pltpu.MemorySpace.HBM
    srcs = [jax.new_ref(a, memory_space=hbm) for a in arrays]
    outs = [jax.empty_ref(s, memory_space=hbm) for s in out_structs]

    def launch(send_sems, recv_sems):
        x, y, c, chips = _place()
        barrier = pltpu.get_barrier_semaphore()
        them = peers(x, y, c, chips)
        for peer in them:
            pl.semaphore_signal(barrier, inc=1, device_id=peer, device_id_type=MESH)
        pl.semaphore_wait(barrier, len(them))
        exchange(srcs, outs, send_sems, recv_sems)

    pl.kernel(launch, mesh=plsc.ScalarSubcoreMesh(axis_name="sequencer", num_cores=1), name=name,
              scratch_types=(pltpu.SemaphoreType.DMA((n_sem,)), pltpu.SemaphoreType.DMA((n_sem,))),
              compiler_params=pltpu.CompilerParams(collective_id=collective_id))()
    return [o[...] for o in outs]


def _sibling_only(x, y, c, chips):
    return [(x, y, 1 - c)]


def _swap_halves(groups, name, collective_id):
    n = len(groups)

    def exchange(srcs, outs, send_sems, recv_sems):
        x, y, c, _ = _place()
        cps = [_remote(srcs[g].at[1 - c], outs[g], send_sems, recv_sems, g, (x, y, 1 - c)) for g in range(n)]
        for cp in cps:
            cp.start()
        for cp in cps:
            cp.wait()

    return _on_sequencer(exchange, groups, [jax.ShapeDtypeStruct(g.shape[1:], g.dtype) for g in groups], n, _sibling_only,
                         name, collective_id)


def _scatter_partials(groups, name, collective_id):
    n = len(groups)

    def exchange(srcs, outs, send_sems, recv_sems):
        x, y, c, chips = _place()
        sends = [_remote(srcs[g].at[2 * px + py], outs[g].at[k], send_sems, recv_sems, g * 3 + k, (px, py, c))
                 for k, (px, py) in enumerate(chips) for g in range(n)]
        for cp in sends:
            cp.start()
        for k, (px, py) in enumerate(chips):
            for g in range(n):
                slot = outs[g].at[k]
                _remote(slot, slot, send_sems, recv_sems, g * 3 + k, (px, py, c)).wait_recv()
        for cp in sends:
            cp.wait_send()

    return _on_sequencer(exchange, groups, [jax.ShapeDtypeStruct((3,) + g.shape[1:], g.dtype) for g in groups], 3 * n,
                         lambda x, y, c, chips: [(px, py, c) for px, py in chips], name, collective_id)


def _share_reduced(groups, name, collective_id):
    n = len(groups)

    def exchange(srcs, outs, send_sems, recv_sems):
        x, y, c, _ = _place()
        cps = [_remote(srcs[g], outs[g], send_sems, recv_sems, g, (x, y, 1 - c)) for g in range(n)]
        for cp in cps:
            cp.start()
        for cp in cps:
            cp.wait()

    return _on_sequencer(exchange, groups, [jax.ShapeDtypeStruct(g.shape, g.dtype) for g in groups], n, _sibling_only,
                         name, collective_id)


def _allreduce_small(v):
    n_dev = 2 * N_CHIPS

    def body(src, out, buf, send_sems, recv_sems):
        x, y, c, _ = _place()
        me = 4 * x + 2 * y + c
        buf[me] = src[...]
        flips = [(k >> 2 & 1, k >> 1 & 1, k & 1) for k in range(1, n_dev)]
        sends = []
        for k, (fx, fy, fc) in enumerate(flips):
            to = ((1 - x) if fx else x, (1 - y) if fy else y, (1 - c) if fc else c)
            cp = _remote(src, buf.at[me], send_sems, recv_sems, k, to)
            cp.start()
            sends.append((cp, to))
        for k, (cp, to) in enumerate(sends):
            slot = buf.at[4 * to[0] + 2 * to[1] + to[2]]
            _remote(slot, slot, send_sems, recv_sems, k, to).wait_recv()
        for cp, _ in sends:
            cp.wait_send()
        acc = buf[0]
        for d in range(1, n_dev):
            acc = acc + buf[d]
        out[...] = acc

    vm = pl.BlockSpec(memory_space=pltpu.VMEM)
    return pl.pallas_call(
        body, name="comm_allreduce_small", in_specs=[vm], out_specs=vm, out_shape=jax.ShapeDtypeStruct(v.shape, F32),
        scratch_shapes=[pltpu.VMEM((n_dev,) + v.shape, F32), pltpu.SemaphoreType.DMA((n_dev - 1,)),
                        pltpu.SemaphoreType.DMA((n_dev - 1,))],
    )(v)


def _row_tile(rows, cap=320):
    return max(t for t in range(16, cap + 1, 16) if rows % t == 0)


def _add_sibling(mine, theirs, core, name):
    _, n, R, C = mine.shape
    t = _row_tile(R)

    def body(core_ref, a_ref, b_ref, o_ref, ob_ref):
        tot = a_ref[...] + b_ref[...]
        o_ref[...] = tot
        ob_ref[...] = tot.astype(BF16)

    sp = pl.BlockSpec((None, t, C), lambda k, i, core_ref: (k, i, 0))
    grid_spec = pltpu.PrefetchScalarGridSpec(
        num_scalar_prefetch=1, grid=(n, R // t),
        in_specs=[pl.BlockSpec((None, None, t, C), lambda k, i, core_ref: (core_ref[0], k, i, 0)), sp], out_specs=[sp, sp])
    return pl.pallas_call(body, name=name, grid_spec=grid_spec,
                          out_shape=[jax.ShapeDtypeStruct((n, R, C), F32), jax.ShapeDtypeStruct((n, R, C), BF16)],
                          compiler_params=_params("parallel", "parallel"))(core, mine, theirs)


def _add_chips(received, own, chip, name, after=None):
    n, R, C = received.shape
    t = _row_tile(R)

    def body(chip_ref, r_ref, o_ref, *rest):
        out_ref = rest[-1]
        acc = o_ref[...]
        for k in range(n):
            acc = acc + r_ref[k].astype(F32)
        out_ref[...] = acc

    in_specs = [pl.BlockSpec((n, t, C), lambda i, chip_ref: (0, i, 0)), pl.BlockSpec((None, t, C), lambda i, chip_ref: (chip_ref[0], i, 0))]
    args = (chip, received, own)
    if after is not None:
        in_specs.append(pl.BlockSpec(memory_space=pl.ANY))
        args += (after,)
    grid_spec = pltpu.PrefetchScalarGridSpec(num_scalar_prefetch=1, grid=(R // t,), in_specs=in_specs,
                                             out_specs=pl.BlockSpec((t, C), lambda i, chip_ref: (i, 0)))
    return pl.pallas_call(body, name=name, grid_spec=grid_spec, out_shape=jax.ShapeDtypeStruct((R, C), F32),
                          compiler_params=_params("parallel"))(*args)


def _adamw(w, g, m, v, name, g_row0=0):
    R, C = w.shape
    t = math.gcd(math.gcd(R, g_row0), 128) if R % 8 == 0 else R
    assert g_row0 % t == 0
    c1 = 1.0 - ADAM_B1 ** ADAM_STEP
    c2 = 1.0 - ADAM_B2 ** ADAM_STEP

    def body(w_ref, g_ref, m_ref, v_ref, go_ref, d_ref, nm_ref, nv_ref):
        gv = g_ref[...]
        nm = ADAM_B1 * m_ref[...] + (1.0 - ADAM_B1) * gv
        nv = ADAM_B2 * v_ref[...] + (1.0 - ADAM_B2) * (gv * gv)
        go_ref[...] = gv
        d_ref[...] = -ADAM_LR * ((nm / c1) / (jnp.sqrt(nv / c2) + ADAM_EPS) + ADAM_WD * w_ref[...])
        nm_ref[...] = nm
        nv_ref[...] = nv

    sp = pl.BlockSpec((t, C), lambda i: (i, 0))
    g_sp = pl.BlockSpec((t, C), lambda i: (i + g_row0 // t, 0))
    return pl.pallas_call(body, name=name, grid=(R // t,), in_specs=[sp, g_sp, sp, sp], out_specs=[sp] * 4,
                          out_shape=[jax.ShapeDtypeStruct((R, C), F32)] * 4, compiler_params=_params("parallel"))(w, g, m, v)


def kernel(x, mem, positions, g_pre_mix, w_in, b_gate, mla_q_norm, w_uq, mla_kv_norm, w_ukv, g_mem, w_mem_kv, w_br_mla, w_br_dil, w_br_mem, w_o, g_post_mix, g_pre_ffn, w_ffn_up, conv_w, conv_b, w_ffn_down, g_post_ffn, loss_target, m_g_pre_mix, m_w_in, m_b_gate, m_mla_q_norm, m_w_uq, m_mla_kv_norm, m_w_ukv, m_g_mem, m_w_mem_kv, m_w_br_mla, m_w_br_dil, m_w_br_mem, m_w_o, m_g_post_mix, m_g_pre_ffn, m_w_ffn_up, m_conv_w, m_conv_b, m_w_ffn_down, m_g_post_ffn, v_g_pre_mix, v_w_in, v_b_gate, v_mla_q_norm, v_w_uq, v_mla_kv_norm, v_w_ukv, v_g_mem, v_w_mem_kv, v_w_br_mla, v_w_br_dil, v_w_br_mem, v_w_o, v_g_post_mix, v_g_pre_ffn, v_w_ffn_up, v_conv_w, v_conv_b, v_w_ffn_down, v_g_post_ffn):
    w_args = (g_pre_mix, w_in, b_gate, mla_q_norm, w_uq, mla_kv_norm, w_ukv, g_mem, w_mem_kv, w_br_mla, w_br_dil, w_br_mem, w_o,
              g_post_mix, g_pre_ffn, w_ffn_up, conv_w, conv_b, w_ffn_down, g_post_ffn)
    m_args = (m_g_pre_mix, m_w_in, m_b_gate, m_mla_q_norm, m_w_uq, m_mla_kv_norm, m_w_ukv, m_g_mem, m_w_mem_kv, m_w_br_mla,
              m_w_br_dil, m_w_br_mem, m_w_o, m_g_post_mix, m_g_pre_ffn, m_w_ffn_up, m_conv_w, m_conv_b, m_w_ffn_down, m_g_post_ffn)
    v_args = (v_g_pre_mix, v_w_in, v_b_gate, v_mla_q_norm, v_w_uq, v_mla_kv_norm, v_w_ukv, v_g_mem, v_w_mem_kv, v_w_br_mla,
              v_w_br_dil, v_w_br_mem, v_w_o, v_g_post_mix, v_g_pre_ffn, v_w_ffn_up, v_conv_w, v_conv_b, v_w_ffn_down, v_g_post_ffn)
    sharded = {name for grp in GROUPS for name, _, _ in grp} | {CONV_W[0]}

    def local(a, name):
        return a[0] if name in sharded else a

    w = {n: local(a, n) for n, a in zip(WEIGHTS, w_args)}
    m = {n: local(a, n) for n, a in zip(WEIGHTS, m_args)}
    v = {n: local(a, n) for n, a in zip(WEIGHTS, v_args)}

    mine = _my_weight_groups(w)
    head, _ = _gather_weights(mine[:1], [], "comm_gather_w_in", 0)
    rest, (conv_all,) = _gather_weights(mine[1:], [w[CONV_W[0]]], "comm_gather_rest", 1)
    full = _full_weights(head + rest, conv_all)
    full.update({name: w[name] for name, _ in REPLICATED})

    loss_local, grad_x, G = _local_step(x[0], mem[0], positions, loss_target[0], full)

    core = lax.axis_index("c").astype(jnp.int32).reshape(1)
    chip = (2 * lax.axis_index("x") + lax.axis_index("y")).astype(jnp.int32).reshape(1)
    first = lax.axis_index("c") == 0
    all_mine = _grad_groups(G)
    shard_groups = [None] * len(GROUPS)
    for tag, which, cid in (("early", EARLY_GROUPS, 2), ("late", LATE_GROUPS, 5)):
        mine = [all_mine[i] for i in which]
        theirs = _swap_halves(mine, f"comm_swap_{tag}", cid)
        partial = [_add_sibling(a, b, core, f"add_sibling_{i}") for i, a, b in zip(which, mine, theirs)]
        received = _scatter_partials([p16 for _, p16 in partial], f"comm_scatter_{tag}", cid + 1)
        after = G["w_uq"] if tag == "early" else None
        reduced = [_add_chips(r, p32, chip, f"add_chips_{i}", after) for i, r, (p32, _) in zip(which, received, partial)]
        for i, a, b in zip(which, reduced, _share_reduced(reduced, f"comm_share_{tag}", cid + 2)):
            shard_groups[i] = jnp.concatenate([jnp.where(first, a, b), jnp.where(first, b, a)], axis=0)
    small = _allreduce_small(_pack_small(G, G[CONV_W[0]], loss_local))
    flat = small.reshape(-1)
    loss = flat[LOSS_AT]
    conv_g = flat[CONV_AT:LOSS_AT].reshape(CONV_W[1])
    conv_cols = CONV_W[1][1] // N_CHIPS
    conv_g = lax.dynamic_slice_in_dim(conv_g, (2 * lax.axis_index("x") + lax.axis_index("y")) * conv_cols, conv_cols, axis=1)

    grads, deltas, new_m, new_v = {}, {}, {}, {}
    for grp, g_all in zip(GROUPS, shard_groups):
        off = 0
        for name, _, _ in grp:
            grads[name], deltas[name], new_m[name], new_v[name] = _adamw(w[name], g_all, m[name], v[name], "adamw_" + name, off)
            off += w[name].shape[0]
    name = CONV_W[0]
    grads[name], deltas[name], new_m[name], new_v[name] = _adamw(w[name], conv_g, m[name], v[name], "adamw_" + name)
    packed = _adamw(_pack_small(w), small, _pack_small(m), _pack_small(v), "adamw_small")
    for dst, packed_small in zip((grads, deltas, new_m, new_v), packed):
        dst.update(_unpack_small(packed_small))

    def out(d, name):
        return d[name][None] if name in sharded else d[name]

    return (loss, grad_x[None], *[out(grads, n) for n in WEIGHTS], *[out(deltas, n) for n in WEIGHTS],
            *[out(new_m, n) for n in WEIGHTS], *[out(new_v, n) for n in WEIGHTS])
```

```python
import functools
import math

import jax
import jax.numpy as jnp
from jax import lax
from jax.experimental import pallas as pl
from jax.experimental.pallas import tpu as pltpu
from jax.experimental.pallas import tpu_sc as plsc

F32 = jnp.float32
BF16 = jnp.bfloat16

D_MODEL = 1024
N_MEM = 256
RMS_EPS = 1e-6
NEG_INF = -1e30
MLA_HEADS = 8
MLA_NOPE = 64
MLA_ROPE = 32
MLA_QK = 96
Q_RANK = 384
KV_RANK = 256
ROPE_THETA = 10000.0
DIL_PAIRS = ((128, 1), (512, 4), (2048, 16))
DIL_GROUPS = 3
DIL_HPG = 4
DIL_HEADS = 12
DIL_W = 512
MEM_HEADS = 4
MEM_W = 512
D_FF = 2816
OFF_Q = 384
OFF_KV = 640
OFF_KR = 672
OFF_DIL = 5280
OFF_MEMQ = 5792
D_IN = 8864
ADAM_LR = 0.001
ADAM_B1 = 0.9
ADAM_B2 = 0.999
ADAM_EPS = 1e-08
ADAM_WD = 0.01
ADAM_STEP = 10

LANES = 128
VMEM_LIMIT = 56 * 1024 * 1024

N_CHIPS = 4
ROW_TILE = 512

NN = (((1,), (0,)), ((), ()))
NT = (((1,), (1,)), ((), ()))
TN = (((0,), (0,)), ((), ()))


def _params(*sem):
    return pltpu.CompilerParams(dimension_semantics=sem, vmem_limit_bytes=VMEM_LIMIT)


def _full(shape):
    return pl.BlockSpec(shape, lambda *_: (0,) * len(shape))


def _matmul(a, b, *, mode="nn", out_dtype=F32, tm=1024, tn=1024, tk=None, add=None, scale=None, name):
    if mode == "nn":
        (M, K), N = a.shape, b.shape[1]
    elif mode == "nt":
        (M, K), N = a.shape, b.shape[0]
    else:
        (K, M), N = a.shape, b.shape[1]
    tm, tn = min(tm, M), min(tn, N)
    tk = K if tk is None else min(tk, K)
    assert M % tm == 0 and N % tn == 0 and K % tk == 0, (name, M, N, K, tm, tn, tk)
    a_spec = pl.BlockSpec((tk, tm), lambda i, j, k: (k, i)) if mode == "tn" else pl.BlockSpec((tm, tk), lambda i, j, k: (i, k))
    b_spec = pl.BlockSpec((tn, tk), lambda i, j, k: (j, k)) if mode == "nt" else pl.BlockSpec((tk, tn), lambda i, j, k: (k, j))
    o_spec = pl.BlockSpec((tm, tn), lambda i, j, k: (i, j))
    return _matmul_blocks(a, b, mode=mode, grid=(M // tm, N // tn, K // tk), a_spec=a_spec, b_spec=b_spec, o_spec=o_spec,
                          out_shape=(M, N), out_dtype=out_dtype, add=add,
                          scale=None if scale is None else (lambda j: j < scale[0], scale[1]), name=name)


def _matmul_blocks(a, b, *, mode, grid, a_spec, b_spec, o_spec, out_shape, out_dtype=F32, add=None, scale=None, name):
    nk = grid[2]
    dims = {"nn": NN, "nt": NT, "tn": TN}[mode]
    tm, tn = o_spec.block_shape
    has_add = add is not None

    def body(*refs):
        a_ref, b_ref = refs[0], refs[1]
        c_ref = refs[2] if has_add else None
        o_ref = refs[3] if has_add else refs[2]
        part = lax.dot_general(a_ref[...].astype(BF16), b_ref[...].astype(BF16), dims, preferred_element_type=F32)
        if scale is not None:
            assert nk == 1 and not has_add
            part = part * jnp.where(scale[0](pl.program_id(1)), scale[1], 1.0)
        if nk == 1:
            if has_add:
                part = part + c_ref[...]
            o_ref[...] = part.astype(out_dtype)
        else:
            acc = refs[-1]
            k = pl.program_id(2)

            @pl.when(k == 0)
            def _():
                acc[...] = part

            @pl.when(k > 0)
            def _():
                acc[...] += part

            @pl.when(k == nk - 1)
            def _():
                r = acc[...]
                if has_add:
                    r = r + c_ref[...]
                o_ref[...] = r.astype(out_dtype)

    in_specs = [a_spec, b_spec] + ([o_spec] if has_add else [])
    args = (a, b) + ((add,) if has_add else ())
    return pl.pallas_call(
        body, name=name, grid=grid, in_specs=in_specs, out_specs=o_spec,
        out_shape=jax.ShapeDtypeStruct(out_shape, out_dtype),
        scratch_shapes=[pltpu.VMEM((tm, tn), F32)] if nk > 1 else [],
        compiler_params=_params("parallel", "parallel", "arbitrary"),
    )(*args)


def _proj_view(h_view, w, d, q_scale, name):
    L, K = h_view.shape[0], h_view.shape[1] // d
    N = w.shape[1]
    tn = N // 3
    tm = min(1024, L)
    return _matmul_blocks(
        h_view, w, mode="nn", grid=(L // tm, 3 * d, 1), a_spec=pl.BlockSpec((tm, K), lambda i, j, k: (i, j // 3)),
        b_spec=pl.BlockSpec((K, tn), lambda i, j, k: (0, j % 3)), o_spec=pl.BlockSpec((tm, tn), lambda i, j, k: (i, j)),
        out_shape=(L, d * N), out_dtype=BF16, scale=(lambda j: j % 3 == 0, q_scale), name=name)


def _dh_view(dp_view, w, d, add, name):
    L, N = dp_view.shape[0], dp_view.shape[1] // d
    K = w.shape[0]
    tm = min(1024, L)
    return _matmul_blocks(
        dp_view, w, mode="nt", grid=(L // tm, d, 1), a_spec=pl.BlockSpec((tm, N), lambda i, j, k: (i, j)),
        b_spec=pl.BlockSpec((K, N), lambda i, j, k: (0, 0)), o_spec=pl.BlockSpec((tm, K), lambda i, j, k: (i, j)),
        out_shape=(L, d * K), add=add, name=name)


def _dw_view(h_view, dp_view, d, name):
    L, K = h_view.shape[0], h_view.shape[1] // d
    N = dp_view.shape[1] // d
    tk = min(1024, L)
    nl = L // tk
    return _matmul_blocks(
        h_view, dp_view, mode="tn", grid=(1, 1, d * nl), a_spec=pl.BlockSpec((tk, K), lambda i, j, k: (k % nl, k // nl)),
        b_spec=pl.BlockSpec((tk, N), lambda i, j, k: (k % nl, k // nl)), o_spec=pl.BlockSpec((K, N), lambda i, j, k: (0, 0)),
        out_shape=(K, N), name=name)


def _rms_fwd_val(x, g):
    r = lax.rsqrt(jnp.mean(x * x, axis=-1, keepdims=True) + RMS_EPS)
    return (x * r) * g


def _rms_bwd_val(dy, x, g):
    r = lax.rsqrt(jnp.mean(x * x, axis=-1, keepdims=True) + RMS_EPS)
    xn = x * r
    gdy = g * dy
    dx = r * (gdy - xn * jnp.mean(gdy * xn, axis=-1, keepdims=True))
    return dx, dy * xn


def _rope_tables(pos, invf, inverse):
    ang = pos * invf
    cos, sin = jnp.cos(ang), jnp.sin(ang)
    lane = lax.broadcasted_iota(jnp.int32, ang.shape, 1)
    first = (lane >= MLA_NOPE) & (lane < MLA_NOPE + MLA_ROPE // 2)
    second = (lane >= MLA_NOPE + MLA_ROPE // 2) & (lane < MLA_QK)
    sgn = -1.0 if inverse else 1.0
    sa = jnp.where(first, -sgn * sin, 0.0)
    sb = jnp.where(second, sgn * sin, 0.0)
    return cos, sa, sb


def _rope_val(x, cos, sa, sb):
    half = MLA_ROPE // 2
    return x * cos + pltpu.roll(x, LANES - half, 1) * sa + pltpu.roll(x, half, 1) * sb


def _head_sum_bcast(v, n_heads):
    parts = []
    for h in range(n_heads):
        s = jnp.sum(v[:, h * LANES:(h + 1) * LANES], axis=1, keepdims=True)
        parts.append(jnp.broadcast_to(s, (v.shape[0], LANES)))
    return parts


def _row_spec(t, w):
    return pl.BlockSpec((t, w), lambda i: (i, 0))


def _acc_spec(w, rows=1):
    return pl.BlockSpec((rows, w), lambda i: (0, 0))


def _rmsnorm(x, g, out_dtype, name):
    S, W = x.shape
    t = min(ROW_TILE, S)

    def body(x_ref, g_ref, o_ref):
        o_ref[...] = _rms_fwd_val(x_ref[...], g_ref[...]).astype(out_dtype)

    return pl.pallas_call(body, name=name, grid=(S // t,), in_specs=[_row_spec(t, W), _acc_spec(W)],
                          out_specs=_row_spec(t, W), out_shape=jax.ShapeDtypeStruct((S, W), out_dtype),
                          compiler_params=_params("parallel"))(x, g)


def _mla_prep(proj_a, pos, invf, q_norm, kv_norm):
    S = proj_a.shape[0]
    t = ROW_TILE

    def body(a_ref, pos_ref, invf_ref, qn_ref, kvn_ref, cq_ref, ckv_ref, kpe_ref):
        a = a_ref[...]
        cq_ref[...] = _rms_fwd_val(a[:, 0:Q_RANK], qn_ref[...]).astype(BF16)
        ckv_ref[...] = _rms_fwd_val(a[:, Q_RANK + LANES:], kvn_ref[...]).astype(BF16)
        cos, sa, sb = _rope_tables(pos_ref[...], invf_ref[...], False)
        kpe_ref[...] = _rope_val(a[:, Q_RANK:Q_RANK + LANES], cos, sa, sb)

    return pl.pallas_call(
        body, name="mla_prep", grid=(S // t,),
        in_specs=[_row_spec(t, 768), _row_spec(t, 1), _acc_spec(LANES), _acc_spec(Q_RANK), _acc_spec(KV_RANK)],
        out_specs=[_row_spec(t, Q_RANK), _row_spec(t, KV_RANK), _row_spec(t, LANES)],
        out_shape=[jax.ShapeDtypeStruct((S, Q_RANK), BF16), jax.ShapeDtypeStruct((S, KV_RANK), BF16),
                   jax.ShapeDtypeStruct((S, LANES), F32)],
        compiler_params=_params("parallel"))(proj_a, pos, invf, q_norm, kv_norm)


def _qk_final(q_pre, k_pre, kpe, pos, invf):
    S, W = q_pre.shape
    t = ROW_TILE
    scale = MLA_QK ** -0.5

    def body(q_ref, k_ref, kpe_ref, pos_ref, invf_ref, qo_ref, ko_ref):
        cos, sa, sb = _rope_tables(pos_ref[...], invf_ref[...], False)
        kpe_v = kpe_ref[...]
        for h in range(MLA_HEADS):
            sl = slice(h * LANES, (h + 1) * LANES)
            qo_ref[:, sl] = (_rope_val(q_ref[:, sl], cos, sa, sb) * scale).astype(BF16)
            ko_ref[:, sl] = (k_ref[:, sl] + kpe_v).astype(BF16)

    return pl.pallas_call(
        body, name="qk_final", grid=(S // t,),
        in_specs=[_row_spec(t, W), _row_spec(t, W), _row_spec(t, LANES), _row_spec(t, 1), _acc_spec(LANES)],
        out_specs=[_row_spec(t, W), _row_spec(t, W)],
        out_shape=[jax.ShapeDtypeStruct((S, W), BF16)] * 2,
        compiler_params=_params("parallel"))(q_pre, k_pre, kpe, pos, invf)


def _mla_bwd_prep(dq, dk, pos, invf):
    S, W = dq.shape
    t = ROW_TILE
    scale = MLA_QK ** -0.5

    def body(dq_ref, dk_ref, pos_ref, invf_ref, dqp_ref, dkpe_ref):
        cos, sa, sb = _rope_tables(pos_ref[...], invf_ref[...], True)
        tot = jnp.zeros((t, LANES), F32)
        for h in range(MLA_HEADS):
            sl = slice(h * LANES, (h + 1) * LANES)
            dqp_ref[:, sl] = _rope_val(dq_ref[:, sl] * scale, cos, sa, sb).astype(BF16)
            tot = tot + dk_ref[:, sl]
        lane = lax.broadcasted_iota(jnp.int32, tot.shape, 1)
        tot = jnp.where((lane >= MLA_NOPE) & (lane < MLA_QK), tot, 0.0)
        dkpe_ref[...] = _rope_val(tot, cos, sa, sb)

    return pl.pallas_call(
        body, name="mla_bwd_prep", grid=(S // t,),
        in_specs=[_row_spec(t, W), _row_spec(t, W), _row_spec(t, 1), _acc_spec(LANES)],
        out_specs=[_row_spec(t, W), _row_spec(t, LANES)],
        out_shape=[jax.ShapeDtypeStruct((S, W), BF16), jax.ShapeDtypeStruct((S, LANES), F32)],
        compiler_params=_params("parallel"))(dq, dk, pos, invf)


def _mla_norm_bwd(dcq, dckv_a, dckv_b, dkpe, proj_a, q_norm, kv_norm):
    S = proj_a.shape[0]
    t = ROW_TILE

    def body(dcq_ref, da_ref, db_ref, dkpe_ref, a_ref, qn_ref, kvn_ref, o_ref, dqn_ref, dkvn_ref):
        i = pl.program_id(0)
        a = a_ref[...]
        dxq, gq = _rms_bwd_val(dcq_ref[...], a[:, 0:Q_RANK], qn_ref[...])
        dxkv, gkv = _rms_bwd_val(da_ref[...] + db_ref[...], a[:, Q_RANK + LANES:], kvn_ref[...])
        o_ref[:, 0:Q_RANK] = dxq.astype(BF16)
        o_ref[:, Q_RANK:Q_RANK + LANES] = dkpe_ref[...].astype(BF16)
        o_ref[:, Q_RANK + LANES:] = dxkv.astype(BF16)

        @pl.when(i == 0)
        def _():
            dqn_ref[...] = jnp.zeros_like(dqn_ref)
            dkvn_ref[...] = jnp.zeros_like(dkvn_ref)

        dqn_ref[...] += jnp.sum(gq, axis=0, keepdims=True)
        dkvn_ref[...] += jnp.sum(gkv, axis=0, keepdims=True)

    return pl.pallas_call(
        body, name="mla_norm_bwd", grid=(S // t,),
        in_specs=[_row_spec(t, Q_RANK), _row_spec(t, KV_RANK), _row_spec(t, KV_RANK), _row_spec(t, LANES),
                  _row_spec(t, 768), _acc_spec(Q_RANK), _acc_spec(KV_RANK)],
        out_specs=[_row_spec(t, 768), _acc_spec(Q_RANK), _acc_spec(KV_RANK)],
        out_shape=[jax.ShapeDtypeStruct((S, 768), BF16), jax.ShapeDtypeStruct((1, Q_RANK), F32),
                   jax.ShapeDtypeStruct((1, KV_RANK), F32)],
        compiler_params=_params("arbitrary"))(dcq, dckv_a, dckv_b, dkpe, proj_a, q_norm, kv_norm)


DILATIONS = tuple(d for _, d in DIL_PAIRS)


def _to_tokens(view, scr, d):
    if d == 1:
        return view
    n, w = view.shape[0], view.shape[1] // d
    for r in range(d):
        for c in range(w // LANES):
            scr[pl.ds(c, 1), pl.ds(r, n, stride=d), :] = view[:, r * w + c * LANES:r * w + (c + 1) * LANES][None]
    return jnp.concatenate([scr[c] for c in range(w // LANES)], axis=1)


def _from_tokens(tok, scr, d, out_ref):
    if d == 1:
        out_ref[...] = tok.astype(out_ref.dtype)
        return
    n, w = tok.shape[0] // d, tok.shape[1]
    for c in range(w // LANES):
        scr[c] = tok[:, c * LANES:(c + 1) * LANES]
    for r in range(d):
        for c in range(w // LANES):
            out_ref[:, r * w + c * LANES:r * w + (c + 1) * LANES] = scr[pl.ds(c, 1), pl.ds(r, n, stride=d), :][0].astype(out_ref.dtype)


def _token_scratch(t, w):
    return pltpu.VMEM((w // LANES, t, LANES), F32)


def _view_spec(t, d):
    return pl.BlockSpec((t // d, d * DIL_W), lambda i: (i, 0))


def _mix_weights(ls):
    m = jnp.maximum(jnp.maximum(ls[0], ls[1]), ls[2])
    es = [jnp.exp(l - m) for l in ls]
    den = es[0] + es[1] + es[2]
    return [e / den for e in es]


def _dil_mix(o_list, lse_list):
    S = o_list[0].shape[0] * DILATIONS[0]
    t = ROW_TILE
    specs = [_view_spec(t, d) for d in DILATIONS]

    def body(o0, o1, o2, l0, l1, l2, y_ref, *scr):
        os_ = [_to_tokens(r[...], scr[g], d) for g, (r, d) in enumerate(zip((o0, o1, o2), DILATIONS))]
        ws = _mix_weights([_to_tokens(r[...], scr[3 + g], d) for g, (r, d) in enumerate(zip((l0, l1, l2), DILATIONS))])
        y_ref[...] = (ws[0] * os_[0] + ws[1] * os_[1] + ws[2] * os_[2]).astype(BF16)

    return pl.pallas_call(
        body, name="dil_mix", grid=(S // t,), in_specs=specs * 2, out_specs=_row_spec(t, DIL_W),
        out_shape=jax.ShapeDtypeStruct((S, DIL_W), BF16), scratch_shapes=[_token_scratch(t, DIL_W)] * 6,
        compiler_params=_params("parallel"))(*o_list, *lse_list)


def _dil_mix_bwd(dy, o_list, lse_list):
    S = dy.shape[0]
    t = ROW_TILE
    specs = [_view_spec(t, d) for d in DILATIONS]

    def body(dy_ref, o0, o1, o2, l0, l1, l2, d0, d1, d2, e0, e1, e2, *scr):
        os_ = [_to_tokens(r[...], scr[g], d) for g, (r, d) in enumerate(zip((o0, o1, o2), DILATIONS))]
        ws = _mix_weights([_to_tokens(r[...], scr[3 + g], d) for g, (r, d) in enumerate(zip((l0, l1, l2), DILATIONS))])
        dyv = dy_ref[...]
        y = ws[0] * os_[0] + ws[1] * os_[1] + ws[2] * os_[2]
        b = jnp.concatenate(_head_sum_bcast(dyv * y, DIL_HPG), axis=1)
        for g, (w, d_ref, e_ref, d) in enumerate(zip(ws, (d0, d1, d2), (e0, e1, e2), DILATIONS)):
            _from_tokens(w * dyv, scr[6 + g], d, d_ref)
            _from_tokens(w * b, scr[9 + g], d, e_ref)

    shapes = [(S // d, d * DIL_W) for d in DILATIONS]
    return pl.pallas_call(
        body, name="dil_mix_bwd", grid=(S // t,), in_specs=[_row_spec(t, DIL_W)] + specs * 2, out_specs=specs * 2,
        out_shape=[jax.ShapeDtypeStruct(s, BF16) for s in shapes] + [jax.ShapeDtypeStruct(s, F32) for s in shapes],
        scratch_shapes=[_token_scratch(t, DIL_W)] * 12, compiler_params=_params("parallel"))(dy, *o_list, *lse_list)


def _delta(do, o, n_heads, name):
    S, W = do.shape
    t = ROW_TILE

    def body(do_ref, o_ref, d_ref):
        prod = do_ref[...].astype(F32) * o_ref[...].astype(F32)
        d_ref[...] = jnp.concatenate(_head_sum_bcast(prod, n_heads), axis=1)

    return pl.pallas_call(body, name=name, grid=(S // t,), in_specs=[_row_spec(t, W)] * 2, out_specs=_row_spec(t, W),
                          out_shape=jax.ShapeDtypeStruct((S, W), F32), compiler_params=_params("parallel"))(do, o)


def _merge(proj_g, b_gate, b_list):
    S = proj_g.shape[0]
    t = ROW_TILE

    def body(g_ref, b_ref, y0, y1, y2, o_ref):
        acc = jnp.zeros((t, D_MODEL), F32)
        for i, y in enumerate((y0, y1, y2)):
            sl = slice(i * D_MODEL, (i + 1) * D_MODEL)
            acc = acc + jax.nn.sigmoid(g_ref[:, sl] + b_ref[:, sl]) * y[...]
        o_ref[...] = acc.astype(BF16)

    return pl.pallas_call(
        body, name="merge", grid=(S // t,),
        in_specs=[_row_spec(t, 3 * D_MODEL), _acc_spec(3 * D_MODEL)] + [_row_spec(t, D_MODEL)] * 3,
        out_specs=_row_spec(t, D_MODEL), out_shape=jax.ShapeDtypeStruct((S, D_MODEL), BF16),
        compiler_params=_params("parallel"))(proj_g, b_gate, *b_list)


def _merge_bwd(dmerged, proj_g, b_gate, b_list):
    S = proj_g.shape[0]
    t = ROW_TILE

    def body(dm_ref, g_ref, b_ref, y0, y1, y2, d0, d1, d2, dz_ref, db_ref):
        i = pl.program_id(0)

        @pl.when(i == 0)
        def _():
            db_ref[...] = jnp.zeros_like(db_ref)

        dm = dm_ref[...]
        for k, (y, d_ref) in enumerate(zip((y0, y1, y2), (d0, d1, d2))):
            sl = slice(k * D_MODEL, (k + 1) * D_MODEL)
            s = jax.nn.sigmoid(g_ref[:, sl] + b_ref[:, sl])
            d_ref[...] = (s * dm).astype(BF16)
            dz = dm * y[...] * (s * (1.0 - s))
            dz_ref[:, sl] = dz.astype(BF16)
            db_ref[:, sl] += jnp.sum(dz, axis=0, keepdims=True)

    return pl.pallas_call(
        body, name="merge_bwd", grid=(S // t,),
        in_specs=[_row_spec(t, D_MODEL), _row_spec(t, 3 * D_MODEL), _acc_spec(3 * D_MODEL)] + [_row_spec(t, D_MODEL)] * 3,
        out_specs=[_row_spec(t, D_MODEL)] * 3 + [_row_spec(t, 3 * D_MODEL), _acc_spec(3 * D_MODEL)],
        out_shape=[jax.ShapeDtypeStruct((S, D_MODEL), BF16)] * 3
        + [jax.ShapeDtypeStruct((S, 3 * D_MODEL), BF16), jax.ShapeDtypeStruct((1, 3 * D_MODEL), F32)],
        compiler_params=_params("arbitrary"))(dmerged, proj_g, b_gate, *b_list)


def _norm2(o, x, g_post, g_pre):
    S = x.shape[0]
    t = ROW_TILE

    def body(o_ref, x_ref, gp_ref, gf_ref, x1_ref, h2_ref):
        x1 = x_ref[...] + _rms_fwd_val(o_ref[...], gp_ref[...])
        x1_ref[...] = x1
        h2_ref[...] = _rms_fwd_val(x1, gf_ref[...]).astype(BF16)

    return pl.pallas_call(
        body, name="norm2", grid=(S // t,),
        in_specs=[_row_spec(t, D_MODEL)] * 2 + [_acc_spec(D_MODEL)] * 2, out_specs=[_row_spec(t, D_MODEL)] * 2,
        out_shape=[jax.ShapeDtypeStruct((S, D_MODEL), F32), jax.ShapeDtypeStruct((S, D_MODEL), BF16)],
        compiler_params=_params("parallel"))(o, x, g_post, g_pre)


def _norm2_bwd(dx2, dh2, x1, o, g_pre, g_post):
    S = x1.shape[0]
    t = ROW_TILE

    def body(dx2_ref, dh2_ref, x1_ref, o_ref, gf_ref, gp_ref, dx1_ref, do_ref, dgf_ref, dgp_ref):
        i = pl.program_id(0)

        @pl.when(i == 0)
        def _():
            dgf_ref[...] = jnp.zeros_like(dgf_ref)
            dgp_ref[...] = jnp.zeros_like(dgp_ref)

        d1, gf = _rms_bwd_val(dh2_ref[...], x1_ref[...], gf_ref[...])
        dx1 = dx2_ref[...] + d1
        dx1_ref[...] = dx1
        do, gp = _rms_bwd_val(dx1, o_ref[...], gp_ref[...])
        do_ref[...] = do.astype(BF16)
        dgf_ref[...] += jnp.sum(gf, axis=0, keepdims=True)
        dgp_ref[...] += jnp.sum(gp, axis=0, keepdims=True)

    return pl.pallas_call(
        body, name="norm2_bwd", grid=(S // t,),
        in_specs=[_row_spec(t, D_MODEL)] * 4 + [_acc_spec(D_MODEL)] * 2,
        out_specs=[_row_spec(t, D_MODEL)] * 2 + [_acc_spec(D_MODEL)] * 2,
        out_shape=[jax.ShapeDtypeStruct((S, D_MODEL), F32), jax.ShapeDtypeStruct((S, D_MODEL), BF16),
                   jax.ShapeDtypeStruct((1, D_MODEL), F32), jax.ShapeDtypeStruct((1, D_MODEL), F32)],
        compiler_params=_params("arbitrary"))(dx2, dh2, x1, o, g_pre, g_post)


def _norm1_bwd(dx1, dh_list, x, g):
    S = x.shape[0]
    t = ROW_TILE
    dils = (1,) + DILATIONS
    assert len(dh_list) == len(dils)

    def body(dx1_ref, *refs):
        dh_refs, (x_ref, g_ref, dx_ref, dg_ref), scr = refs[:len(dils)], refs[len(dils):len(dils) + 4], refs[len(dils) + 4:]
        i = pl.program_id(0)

        @pl.when(i == 0)
        def _():
            dg_ref[...] = jnp.zeros_like(dg_ref)

        dh = dh_refs[0][...]
        for k in range(1, len(dils)):
            dh = dh + _to_tokens(dh_refs[k][...], scr[k - 1], dils[k])
        d, gg = _rms_bwd_val(dh, x_ref[...], g_ref[...])
        dx_ref[...] = dx1_ref[...] + d
        dg_ref[...] += jnp.sum(gg, axis=0, keepdims=True)

    dh_specs = [pl.BlockSpec((t // d, d * D_MODEL), lambda i: (i, 0)) for d in dils]
    return pl.pallas_call(
        body, name="norm1_bwd", grid=(S // t,),
        in_specs=[_row_spec(t, D_MODEL)] + dh_specs + [_row_spec(t, D_MODEL), _acc_spec(D_MODEL)],
        out_specs=[_row_spec(t, D_MODEL), _acc_spec(D_MODEL)],
        out_shape=[jax.ShapeDtypeStruct((S, D_MODEL), F32), jax.ShapeDtypeStruct((1, D_MODEL), F32)],
        scratch_shapes=[_token_scratch(t, D_MODEL)] * (len(dils) - 1),
        compiler_params=_params("arbitrary"))(dx1, *dh_list, x, g)


def _gain_grad(dy, x, name):
    R, W = x.shape

    def body(dy_ref, x_ref, dg_ref):
        xv = x_ref[...]
        r = lax.rsqrt(jnp.mean(xv * xv, axis=-1, keepdims=True) + RMS_EPS)
        dg_ref[...] = jnp.sum(dy_ref[...] * (xv * r), axis=0, keepdims=True)

    return pl.pallas_call(body, name=name, grid=(1,), in_specs=[_full((R, W))] * 2, out_specs=_full((1, W)),
                          out_shape=jax.ShapeDtypeStruct((1, W), F32), compiler_params=_params("arbitrary"))(dy, x)


def _loss_head(f, x1, tgt, g):
    S = f.shape[0]
    t = ROW_TILE

    def body(f_ref, x1_ref, t_ref, g_ref, loss_ref, dx2_ref, df_ref, dg_ref):
        i = pl.program_id(0)

        @pl.when(i == 0)
        def _():
            loss_ref[...] = jnp.zeros_like(loss_ref)
            dg_ref[...] = jnp.zeros_like(dg_ref)

        fv, gv = f_ref[...], g_ref[...]
        err = x1_ref[...] + _rms_fwd_val(fv, gv) - t_ref[...]
        part = jnp.sum(jnp.mean(err * err, axis=-1, keepdims=True), axis=0, keepdims=True)
        loss_ref[...] += jnp.broadcast_to(0.5 * part, loss_ref.shape)
        dx2 = err * (1.0 / D_MODEL)
        dx2_ref[...] = dx2
        df, gg = _rms_bwd_val(dx2, fv, gv)
        df_ref[...] = df.astype(BF16)
        dg_ref[...] += jnp.sum(gg, axis=0, keepdims=True)

    return pl.pallas_call(
        body, name="loss_head", grid=(S // t,),
        in_specs=[_row_spec(t, D_MODEL)] * 3 + [_acc_spec(D_MODEL)],
        out_specs=[_acc_spec(LANES, 8), _row_spec(t, D_MODEL), _row_spec(t, D_MODEL), _acc_spec(D_MODEL)],
        out_shape=[jax.ShapeDtypeStruct((8, LANES), F32), jax.ShapeDtypeStruct((S, D_MODEL), F32),
                   jax.ShapeDtypeStruct((S, D_MODEL), BF16), jax.ShapeDtypeStruct((1, D_MODEL), F32)],
        compiler_params=_params("arbitrary"))(f, x1, tgt, g)


CONV_TC = 1408
CONV_TT = 256
HALO = 8


def _shift_down(u, halo, first):
    row = lax.broadcasted_iota(jnp.int32, u.shape, 0)
    h6 = jnp.where(first, 0.0, halo[HALO - 2:HALO - 1, :])
    h7 = jnp.where(first, 0.0, halo[HALO - 1:HALO, :])
    s1 = jnp.where(row == 0, h7, pltpu.roll(u, 1, 0))
    s2 = jnp.where(row == 0, h6, jnp.where(row == 1, h7, pltpu.roll(u, 2, 0)))
    return s1, s2


def _conv_specs(tt, n_c, n_t, lead):
    def halo_row(i):
        return jnp.maximum(i * (tt // HALO) - 1, 0) if lead else jnp.minimum((i + 1) * (tt // HALO), n_t * (tt // HALO) - 1)
    return [
        pl.BlockSpec((tt, CONV_TC), lambda j, i: (i, j)),
        pl.BlockSpec((tt, CONV_TC), lambda j, i: (i, j + n_c)),
        pl.BlockSpec((HALO, CONV_TC), lambda j, i: (halo_row(i), j)),
        pl.BlockSpec((HALO, CONV_TC), lambda j, i: (halo_row(i), j + n_c)),
    ]


def _conv_z(ug, uv, hg, hv, w_g, w_v, b_g, b_v, first):
    g1, g2 = _shift_down(ug, hg, first)
    v1, v2 = _shift_down(uv, hv, first)
    zg = b_g + w_g[0:1, :] * g2
    zg = zg + w_g[1:2, :] * g1
    zg = zg + w_g[2:3, :] * ug
    zv = b_v + w_v[0:1, :] * v2
    zv = zv + w_v[1:2, :] * v1
    zv = zv + w_v[2:3, :] * uv
    return zg, zv, (g2, g1, ug), (v2, v1, uv)


def _conv_fwd(u, conv_w, conv_b):
    S = u.shape[0]
    tt = min(CONV_TT, S)
    n_c, n_t = D_FF // CONV_TC, S // tt
    wspec = [pl.BlockSpec((3, CONV_TC), lambda j, i: (0, j)), pl.BlockSpec((3, CONV_TC), lambda j, i: (0, j + n_c)),
             pl.BlockSpec((1, CONV_TC), lambda j, i: (0, j)), pl.BlockSpec((1, CONV_TC), lambda j, i: (0, j + n_c))]

    def body(ug_ref, uv_ref, hg_ref, hv_ref, wg_ref, wv_ref, bg_ref, bv_ref, a_ref):
        first = pl.program_id(1) == 0
        zg, zv, _, _ = _conv_z(ug_ref[...], uv_ref[...], hg_ref, hv_ref, wg_ref, wv_ref, bg_ref[...], bv_ref[...], first)
        a_ref[...] = (zg * jax.nn.sigmoid(zg) * zv).astype(BF16)

    return pl.pallas_call(
        body, name="conv_fwd", grid=(n_c, n_t), in_specs=_conv_specs(tt, n_c, n_t, True) + wspec,
        out_specs=pl.BlockSpec((tt, CONV_TC), lambda j, i: (i, j)), out_shape=jax.ShapeDtypeStruct((S, D_FF), BF16),
        compiler_params=_params("parallel", "parallel"))(u, u, u, u, conv_w, conv_w, conv_b, conv_b)


def _gate_bwd(da, zg, zv):
    sg = jax.nn.sigmoid(zg)
    return da * zv * (sg * (1.0 + zg * (1.0 - sg))), da * (zg * sg)


def _conv_bwd(da, u, conv_w, conv_b):
    S = u.shape[0]
    tt = min(CONV_TT, S)
    n_c, n_t = D_FF // CONV_TC, S // tt
    wspec = [pl.BlockSpec((3, CONV_TC), lambda j, i: (0, j)), pl.BlockSpec((3, CONV_TC), lambda j, i: (0, j + n_c)),
             pl.BlockSpec((1, CONV_TC), lambda j, i: (0, j)), pl.BlockSpec((1, CONV_TC), lambda j, i: (0, j + n_c))]
    tile = pl.BlockSpec((tt, CONV_TC), lambda j, i: (i, j))
    trail = pl.BlockSpec((HALO, CONV_TC), lambda j, i: (jnp.minimum((i + 1) * (tt // HALO), n_t * (tt // HALO) - 1), j))

    def body(da_ref, ug_ref, uv_ref, hg_ref, hv_ref, tda_ref, tg_ref, tv_ref, wg_ref, wv_ref, bg_ref, bv_ref,
             dug_ref, duv_ref, dwg_ref, dwv_ref, dbg_ref, dbv_ref):
        i = pl.program_id(1)
        last = i == n_t - 1
        bg, bv = bg_ref[...], bv_ref[...]
        zg, zv, gs, vs = _conv_z(ug_ref[...], uv_ref[...], hg_ref, hv_ref, wg_ref, wv_ref, bg, bv, i == 0)
        dzg, dzv = _gate_bwd(da_ref[...], zg, zv)
        tzg, tzv, _, _ = _conv_z(tg_ref[...], tv_ref[...], ug_ref.at[pl.ds(tt - HALO, HALO), :], uv_ref.at[pl.ds(tt - HALO, HALO), :],
                                 wg_ref, wv_ref, bg, bv, False)
        tdzg, tdzv = _gate_bwd(tda_ref[...], tzg, tzv)
        row = lax.broadcasted_iota(jnp.int32, dzg.shape, 0)
        for dz, tdz, w_ref, du_ref in ((dzg, tdzg, wg_ref, dug_ref), (dzv, tdzv, wv_ref, duv_ref)):
            h0 = jnp.where(last, 0.0, tdz[0:1, :])
            h1 = jnp.where(last, 0.0, tdz[1:2, :])
            u1 = jnp.where(row == tt - 1, h0, pltpu.roll(dz, tt - 1, 0))
            u2 = jnp.where(row == tt - 1, h1, jnp.where(row == tt - 2, h0, pltpu.roll(dz, tt - 2, 0)))
            du_ref[...] = (w_ref[2:3, :] * dz + w_ref[1:2, :] * u1 + w_ref[0:1, :] * u2).astype(BF16)

        @pl.when(i == 0)
        def _():
            for r in (dwg_ref, dwv_ref, dbg_ref, dbv_ref):
                r[...] = jnp.zeros_like(r)

        for k in range(3):
            dwg_ref[k:k + 1, :] += jnp.sum(dzg * gs[k], axis=0, keepdims=True)
            dwv_ref[k:k + 1, :] += jnp.sum(dzv * vs[k], axis=0, keepdims=True)
        dbg_ref[...] += jnp.sum(dzg, axis=0, keepdims=True)
        dbv_ref[...] += jnp.sum(dzv, axis=0, keepdims=True)

    lead = _conv_specs(tt, n_c, n_t, True)
    trail_v = pl.BlockSpec((HALO, CONV_TC), lambda j, i: (jnp.minimum((i + 1) * (tt // HALO), n_t * (tt // HALO) - 1), j + n_c))
    outs = pl.pallas_call(
        body, name="conv_bwd", grid=(n_c, n_t), in_specs=[tile] + lead + [trail, trail, trail_v] + wspec,
        out_specs=[tile, tile] + [pl.BlockSpec((3, CONV_TC), lambda j, i: (0, j))] * 2 + [pl.BlockSpec((1, CONV_TC), lambda j, i: (0, j))] * 2,
        out_shape=[jax.ShapeDtypeStruct((S, D_FF), BF16)] * 2 + [jax.ShapeDtypeStruct((3, D_FF), F32)] * 2
        + [jax.ShapeDtypeStruct((1, D_FF), F32)] * 2,
        compiler_params=_params("parallel", "arbitrary"))(da, u, u, u, u, da, u, u, conv_w, conv_w, conv_b, conv_b)
    dug, duv, dwg, dwv, dbg, dbv = outs
    return dug, duv, jnp.concatenate([dwg, dwv], axis=1), jnp.concatenate([dbg, dbv], axis=1)


BAND = 128


MEM_TQ = 512


def _mem_fwd(q, kv, *, name):
    S, W = q.shape
    M = kv.shape[0]
    nh = W // LANES
    tq = min(MEM_TQ, S)

    def body(q_ref, k_ref, v_ref, o_ref, l_ref):
        s = lax.dot_general(q_ref[...], k_ref[...].astype(BF16), NT, preferred_element_type=F32)
        m = jnp.max(s, axis=1, keepdims=True)
        p = jnp.exp(s - m)
        l = jnp.sum(p, axis=1, keepdims=True)
        o_ref[...] = (lax.dot_general(p.astype(BF16), v_ref[...].astype(BF16), NN, preferred_element_type=F32) / l).astype(BF16)
        l_ref[...] = jnp.broadcast_to(m + jnp.log(l), (tq, LANES))

    blk = pl.BlockSpec((tq, LANES), lambda hh, i: (i, hh))
    return pl.pallas_call(
        body, name=name, grid=(nh, S // tq),
        in_specs=[blk, pl.BlockSpec((M, LANES), lambda hh, i: (0, hh)), pl.BlockSpec((M, LANES), lambda hh, i: (0, hh + nh))],
        out_specs=[blk, blk], out_shape=[jax.ShapeDtypeStruct((S, W), BF16), jax.ShapeDtypeStruct((S, W), F32)],
        compiler_params=_params("parallel", "parallel"))(q, kv, kv)


def _mem_bwd(q, kv, do, lse, delta, *, scale, name):
    S, W = q.shape
    M = kv.shape[0]
    nh = W // LANES
    tq = min(MEM_TQ, S)

    def body(q_ref, k_ref, v_ref, do_ref, l_ref, d_ref, dq_ref, dk_ref, dv_ref):
        i = pl.program_id(1)

        @pl.when(i == 0)
        def _():
            dk_ref[...] = jnp.zeros_like(dk_ref)
            dv_ref[...] = jnp.zeros_like(dv_ref)

        qv = q_ref[...]
        kv_, vv = k_ref[...].astype(BF16), v_ref[...].astype(BF16)
        dov = do_ref[...].astype(BF16)
        s = lax.dot_general(qv, kv_, NT, preferred_element_type=F32)
        p = jnp.exp(s - l_ref[...][:, 0:1])
        dp = lax.dot_general(dov, vv, NT, preferred_element_type=F32)
        ds = (p * (dp - d_ref[...][:, 0:1])).astype(BF16)
        dq_ref[...] = lax.dot_general(ds, kv_, NN, preferred_element_type=F32) * scale
        dk_ref[...] += lax.dot_general(ds, qv, TN, preferred_element_type=F32)
        dv_ref[...] += lax.dot_general(p.astype(BF16), dov, TN, preferred_element_type=F32)

    blk = pl.BlockSpec((tq, LANES), lambda hh, i: (i, hh))
    kblk = pl.BlockSpec((M, LANES), lambda hh, i: (0, hh))
    vblk = pl.BlockSpec((M, LANES), lambda hh, i: (0, hh + nh))
    dq, dk, dv = pl.pallas_call(
        body, name=name, grid=(nh, S // tq), in_specs=[blk, kblk, vblk, blk, blk, blk], out_specs=[blk, kblk, kblk],
        out_shape=[jax.ShapeDtypeStruct((S, W), F32), jax.ShapeDtypeStruct((M, W), F32), jax.ShapeDtypeStruct((M, W), F32)],
        compiler_params=_params("parallel", "arbitrary"))(q, kv, kv, do, lse, delta)
    return dq, jnp.concatenate([dk, dv], axis=1)


CAUSAL_BLOCK = 512
STRIP = 32


def _causal_fwd(q, k, v, *, name):
    S, W = q.shape
    T = min(CAUSAL_BLOCK, S // 2)
    n_strips = T // STRIP

    def body(q_ref, k_ref, v_ref, o_ref, lse_ref, s0, s1, p0, p1, a0, a1, acc_scr):
        i = pl.program_id(1)
        s_scr, p_scr, a_scr = (s0, s1), (p0, p1), (a0, a1)

        def rows(j):
            return pl.ds(pl.multiple_of(j * T, T), T)

        def scores(j, slot):
            s_scr[slot][...] = lax.dot_general(q_ref[...], k_ref[rows(j), :], NT, preferred_element_type=F32)

        def softmax(slot, stats, diag):
            def strip(r):
                s = s_scr[slot][r * STRIP:(r + 1) * STRIP, :]
                if diag:
                    row = r * STRIP + lax.broadcasted_iota(jnp.int32, s.shape, 0)
                    s = jnp.where(row >= lax.broadcasted_iota(jnp.int32, s.shape, 1), s, NEG_INF)
                return s

            m_new = [jnp.maximum(m_old, jnp.max(strip(r), axis=1, keepdims=True)) for r, (m_old, _) in enumerate(stats)]
            new = []
            for r, (m_old, l_old) in enumerate(stats):
                rs = slice(r * STRIP, (r + 1) * STRIP)
                p = jnp.exp(strip(r) - m_new[r])
                alpha = jnp.exp(m_old - m_new[r])
                new.append((m_new[r], alpha * l_old + jnp.sum(p, axis=1, keepdims=True)))
                a_scr[slot][rs, :] = alpha
                p_scr[slot][rs, :] = p.astype(BF16)
            return tuple(new)

        def values(j, slot):
            acc_scr[...] = a_scr[slot][...] * acc_scr[...] + lax.dot_general(p_scr[slot][...], v_ref[rows(j), :], NN,
                                                                            preferred_element_type=F32)

        def trip(j, stats, mine, other):
            scores(j + 1, other)
            stats = softmax(mine, stats, False)
            values(jnp.maximum(j - 1, 0), other)
            return stats

        def pair(jj, stats):
            return trip(2 * jj + 1, trip(2 * jj, stats, 0, 1), 1, 0)

        def last(stats, mine, other):
            values(jnp.maximum(i - 1, 0), other)
            stats = softmax(mine, stats, True)
            values(i, mine)
            for r, (m, l) in enumerate(stats):
                rs = slice(r * STRIP, (r + 1) * STRIP)
                o_ref[rs, :] = (acc_scr[rs, :] / l).astype(BF16)
                lse_ref[rs, :] = jnp.broadcast_to(m + jnp.log(l), (STRIP, LANES))

        acc_scr[...] = jnp.zeros_like(acc_scr)
        p1[...] = jnp.zeros_like(p1)
        a1[...] = jnp.ones_like(a1)
        scores(0, 0)
        init = tuple((jnp.full((STRIP, 1), NEG_INF, F32), jnp.zeros((STRIP, 1), F32)) for _ in range(n_strips))
        stats = lax.fori_loop(0, i // 2, pair, init)

        @pl.when(i % 2 == 1)
        def _():
            last(trip(i - 1, stats, 0, 1), 1, 0)

        @pl.when(i % 2 == 0)
        def _():
            last(stats, 0, 1)

    blk = pl.BlockSpec((T, LANES), lambda hh, i: (i, hh))
    whole = pl.BlockSpec((S, LANES), lambda hh, i: (0, hh))
    return pl.pallas_call(
        body, name=name, grid=(W // LANES, S // T), in_specs=[blk, whole, whole], out_specs=[blk, blk],
        out_shape=[jax.ShapeDtypeStruct((S, W), BF16), jax.ShapeDtypeStruct((S, W), F32)],
        scratch_shapes=[pltpu.VMEM((T, T), F32)] * 2 + [pltpu.VMEM((T, T), BF16)] * 2 + [pltpu.VMEM((T, 1), F32)] * 2
        + [pltpu.VMEM((T, LANES), F32)],
        compiler_params=_params("parallel", "arbitrary"))(q, k, v)


def _causal_bwd(q, k, v, do, lse, delta, *, name):
    S, W = q.shape
    T = min(CAUSAL_BLOCK, S // 2)
    nq = S // T
    n_strips, n_col = T // STRIP, T // LANES

    def body(q_ref, k_ref, v_ref, do_ref, l_ref, d_ref, dq_ref, dk_ref, dv_ref, s0, s1, e0, e1, p0, p1, g0, g1):
        j = pl.program_id(1)
        s_scr, e_scr, p_scr, g_scr = (s0, s1), (e0, e1), (p0, p1), (g0, g1)

        @pl.when(j == 0)
        def _():
            dq_ref[...] = jnp.zeros_like(dq_ref)

        dk_ref[...] = jnp.zeros_like(dk_ref)
        dv_ref[...] = jnp.zeros_like(dv_ref)

        def rows(i):
            return pl.ds(pl.multiple_of(jnp.minimum(i, nq - 1) * T, T), T)

        def products(i, slot):
            r = rows(i)
            s_scr[slot][...] = lax.dot_general(q_ref[r, :], k_ref[...], NT, preferred_element_type=F32)
            e_scr[slot][...] = lax.dot_general(do_ref[r, :].astype(BF16), v_ref[...], NT, preferred_element_type=F32)

        def pointwise(i, slot, diag):
            base = pl.multiple_of(i * T, T)
            for r in range(n_strips):
                rs = slice(r * STRIP, (r + 1) * STRIP)
                lse_r = l_ref[pl.ds(base + r * STRIP, STRIP), :]
                del_r = d_ref[pl.ds(base + r * STRIP, STRIP), :]
                for c in range(n_col):
                    cs = slice(c * LANES, (c + 1) * LANES)
                    if diag and c * LANES > (r + 1) * STRIP - 1:
                        p_scr[slot][rs, cs] = jnp.zeros((STRIP, LANES), BF16)
                        g_scr[slot][rs, cs] = jnp.zeros((STRIP, LANES), BF16)
                        continue
                    sv = s_scr[slot][rs, cs]
                    if diag and (c + 1) * LANES - 1 > r * STRIP:
                        row = r * STRIP + lax.broadcasted_iota(jnp.int32, sv.shape, 0)
                        col = c * LANES + lax.broadcasted_iota(jnp.int32, sv.shape, 1)
                        sv = jnp.where(row >= col, sv, NEG_INF)
                    p = jnp.exp(sv - lse_r)
                    p_scr[slot][rs, cs] = p.astype(BF16)
                    g_scr[slot][rs, cs] = (p * (e_scr[slot][rs, cs] - del_r)).astype(BF16)

        def gradients(i, slot):
            r = rows(i)
            qi, doi = q_ref[r, :], do_ref[r, :].astype(BF16)
            g = g_scr[slot][...]
            dv_ref[...] += lax.dot_general(p_scr[slot][...], doi, TN, preferred_element_type=F32)
            dk_ref[...] += lax.dot_general(g, qi, TN, preferred_element_type=F32)
            dq_ref[r, :] += lax.dot_general(g, k_ref[...], NN, preferred_element_type=F32)

        products(j, 0)
        products(j + 1, 1)
        pointwise(j, 0, True)

        def trip(i, mine, other):
            products(i + 1, other)
            pointwise(i, mine, False)
            gradients(i - 1, other)

        def pair(t, _):
            trip(j + 1 + 2 * t, 1, 0)
            trip(j + 2 + 2 * t, 0, 1)
            return 0

        n_rest = nq - 1 - j
        lax.fori_loop(0, n_rest // 2, pair, 0)

        @pl.when(n_rest % 2 == 1)
        def _():
            trip(nq - 1, 1, 0)
            gradients(nq - 1, 1)

        @pl.when(n_rest % 2 == 0)
        def _():
            gradients(nq - 1, 0)

    blk = pl.BlockSpec((T, LANES), lambda hh, j: (j, hh))
    whole = pl.BlockSpec((S, LANES), lambda hh, j: (0, hh))
    return pl.pallas_call(
        body, name=name, grid=(W // LANES, S // T), in_specs=[whole, blk, blk, whole, whole, whole],
        out_specs=[whole, blk, blk], out_shape=[jax.ShapeDtypeStruct((S, W), F32)] * 3,
        scratch_shapes=[pltpu.VMEM((T, T), F32)] * 4 + [pltpu.VMEM((T, T), BF16)] * 4,
        compiler_params=_params("parallel", "arbitrary"))(q, k, v, do, lse, delta)


BAND_TQ = 512


def _band_window(i, sub, nsub, L, q_ref, k_ref, v_ref, slope):
    kw = min(2 * BAND, L)
    n = i * nsub + sub
    k0 = 0 if kw == L else pl.multiple_of(jnp.maximum(n - 1, 0) * BAND, BAND)
    win = pl.ds(k0, kw)
    qs = q_ref[sub * BAND:(sub + 1) * BAND, :]
    kwv, vwv = k_ref[win, :], v_ref[win, :]
    s = lax.dot_general(qs, kwv, NT, preferred_element_type=F32)
    dist = (n * BAND + lax.broadcasted_iota(jnp.int32, s.shape, 0)) - (k0 + lax.broadcasted_iota(jnp.int32, s.shape, 1))
    s = jnp.where((dist >= 0) & (dist <= BAND), s - slope * dist.astype(F32), NEG_INF)
    return win, qs, kwv, vwv, s


def _band_fwd(q, k, v, slopes, *, n_heads, qcol, kcol, vcol, L, slope_mul, o_shape, name):
    tq = min(BAND_TQ, L)
    nsub = tq // BAND

    def body(sl_ref, q_ref, k_ref, v_ref, o_ref, l_ref):
        hh, i = pl.program_id(0), pl.program_id(1)
        slope = sl_ref[hh % DIL_HPG] * slope_mul
        wins = [_band_window(i, sub, nsub, L, q_ref, k_ref, v_ref, slope) for sub in range(nsub)]
        ms = [jnp.max(w[4], axis=1, keepdims=True) for w in wins]
        ps = [jnp.exp(w[4] - m) for w, m in zip(wins, ms)]
        ls = [jnp.sum(p, axis=1, keepdims=True) for p in ps]
        for sub, (w, m, p, l) in enumerate(zip(wins, ms, ps, ls)):
            rows = slice(sub * BAND, (sub + 1) * BAND)
            o_ref[rows, :] = lax.dot_general(p.astype(BF16), w[3], NN, preferred_element_type=F32) / l
            l_ref[rows, :] = jnp.broadcast_to(m + jnp.log(l), (BAND, LANES))

    whole = lambda col: pl.BlockSpec((L, LANES), lambda hh, i: (0, col(hh)))
    o_spec = pl.BlockSpec((tq, LANES), lambda hh, i: (i, hh))
    return pl.pallas_call(
        body, name=name, grid=(n_heads, L // tq),
        in_specs=[pl.BlockSpec(memory_space=pltpu.SMEM), pl.BlockSpec((tq, LANES), lambda hh, i: (i, qcol(hh))), whole(kcol), whole(vcol)],
        out_specs=[o_spec, o_spec], out_shape=[jax.ShapeDtypeStruct(o_shape, F32)] * 2,
        compiler_params=_params("parallel", "arbitrary"))(slopes, q, k, v)


def _band_bwd(q, k, v, do, lse, delta, slopes, *, n_heads, qcol, kcol, vcol, L, scale, slope_mul, d_shape, name):
    tq = min(BAND_TQ, L)
    nsub = tq // BAND
    n_steps = L // tq

    def body(sl_ref, q_ref, k_ref, v_ref, do_ref, l_ref, d_ref, dq_ref, dk_ref, dv_ref, dk_acc, dv_acc):
        hh, i = pl.program_id(0), pl.program_id(1)
        slope = sl_ref[hh % DIL_HPG] * slope_mul

        @pl.when(i == 0)
        def _():
            dk_acc[...] = jnp.zeros_like(dk_acc)
            dv_acc[...] = jnp.zeros_like(dv_acc)

        blocks = range(nsub)
        rows = [slice(sub * BAND, (sub + 1) * BAND) for sub in blocks]
        wins = [_band_window(i, sub, nsub, L, q_ref, k_ref, v_ref, slope) for sub in blocks]
        dos = [do_ref[r, :] for r in rows]
        dps = [lax.dot_general(do, w[3], NT, preferred_element_type=F32) for do, w in zip(dos, wins)]
        ps = [jnp.exp(w[4] - l_ref[r, :][:, 0:1]) for w, r in zip(wins, rows)]
        dss = [(p * (dp - d_ref[r, :][:, 0:1])).astype(BF16) for p, dp, r in zip(ps, dps, rows)]
        for r, ds, w in zip(rows, dss, wins):
            dq_ref[r, :] = (lax.dot_general(ds, w[2], NN, preferred_element_type=F32) * scale).astype(BF16)
        dks = [lax.dot_general(ds, w[1], TN, preferred_element_type=F32) for ds, w in zip(dss, wins)]
        dvs = [lax.dot_general(p.astype(BF16), do, TN, preferred_element_type=F32) for p, do in zip(ps, dos)]
        for w, dk, dv in zip(wins, dks, dvs):
            dk_acc[w[0], :] += dk
            dv_acc[w[0], :] += dv

        @pl.when(i == n_steps - 1)
        def _():
            dk_ref[...] = dk_acc[...].astype(BF16)
            dv_ref[...] = dv_acc[...].astype(BF16)

    whole = lambda col: pl.BlockSpec((L, LANES), lambda hh, i: (0, col(hh)))
    blk = pl.BlockSpec((tq, LANES), lambda hh, i: (i, hh))
    ident = lambda hh: hh
    return pl.pallas_call(
        body, name=name, grid=(n_heads, n_steps),
        in_specs=[pl.BlockSpec(memory_space=pltpu.SMEM), pl.BlockSpec((tq, LANES), lambda hh, i: (i, qcol(hh))), whole(kcol), whole(vcol),
                  blk, blk, blk],
        out_specs=[blk, whole(ident), whole(ident)], out_shape=[jax.ShapeDtypeStruct(d_shape, BF16)] * 3,
        scratch_shapes=[pltpu.VMEM((L, LANES), F32)] * 2,
        compiler_params=_params("parallel", "arbitrary"))(slopes, q, k, v, do, lse, delta)


def _pad_heads(w, n_heads, width, axis):
    shp = w.shape
    new = shp[:axis] + (n_heads, width) + shp[axis + 1:]
    pad = [(0, 0)] * len(new)
    pad[axis + 1] = (0, LANES - width)
    out = jnp.pad(w.reshape(new), pad)
    return out.reshape(shp[:axis] + (n_heads * LANES,) + shp[axis + 1:])


def _unpad_heads(w, n_heads, width, axis):
    shp = w.shape
    new = shp[:axis] + (n_heads, LANES) + shp[axis + 1:]
    out = lax.slice_in_dim(w.reshape(new), 0, width, axis=axis + 1)
    return out.reshape(shp[:axis] + (n_heads * width,) + shp[axis + 1:])


def _alibi_slopes():
    s = jnp.exp2(-8.0 * jnp.arange(1, DIL_HEADS + 1, dtype=F32) / DIL_HEADS)
    return s.reshape(DIL_HPG, DIL_GROUPS).T


def _local_step(x, mem, positions, tgt, W):
    S = x.shape[0]
    pos = positions.reshape(S, 1).astype(F32)
    half = MLA_ROPE // 2
    inv_freq = ROPE_THETA ** (-jnp.arange(half, dtype=F32) / half)
    invf = jnp.zeros((1, LANES), F32).at[0, MLA_NOPE:MLA_NOPE + half].set(inv_freq).at[0, MLA_NOPE + half:MLA_QK].set(inv_freq)
    slopes = _alibi_slopes()

    w_in = W["w_in"]
    zc = lambda n: jnp.zeros((D_MODEL, n), BF16)
    w_a = jnp.concatenate([w_in[:, :OFF_Q], zc(MLA_NOPE), w_in[:, OFF_KV:OFF_KR], zc(LANES - MLA_QK), w_in[:, OFF_Q:OFF_KV]], axis=1)
    w_d, w_m, w_g = w_in[:, OFF_KR:OFF_DIL], w_in[:, OFF_DIL:OFF_MEMQ], w_in[:, OFF_MEMQ:]
    w_uq_p = _pad_heads(W["w_uq"], MLA_HEADS, MLA_QK, 1)
    ukv = W["w_ukv"].reshape(KV_RANK, MLA_HEADS, 2 * MLA_NOPE)
    w_uk_p = _pad_heads(ukv[:, :, :MLA_NOPE].reshape(KV_RANK, -1), MLA_HEADS, MLA_NOPE, 1)
    w_uv_p = _pad_heads(ukv[:, :, MLA_NOPE:].reshape(KV_RANK, -1), MLA_HEADS, MLA_NOPE, 1)
    w_br_mla_p = _pad_heads(W["w_br_mla"], MLA_HEADS, MLA_NOPE, 0)

    h = _rmsnorm(x, W["g_pre_mix"], BF16, "norm1")
    proj_a = _matmul(h, w_a, tn=768, name="proj_a")
    att_scale = LANES ** -0.5
    w_dg = [jnp.concatenate([w_d[:, (part * DIL_GROUPS + g) * DIL_W:(part * DIL_GROUPS + g + 1) * DIL_W] for part in range(3)], axis=1)
            for g in range(DIL_GROUPS)]
    proj_m = _matmul(h, w_m, tn=MEM_W, out_dtype=BF16, scale=(1, att_scale), name="proj_m")
    proj_g = _matmul(h, w_g, tn=1536, name="proj_g")

    cq_n, ckv_n, kpe = _mla_prep(proj_a, pos, invf, W["mla_q_norm"], W["mla_kv_norm"])
    q_pre = _matmul(cq_n, w_uq_p, tn=1024, name="mla_q")
    k_pre = _matmul(ckv_n, w_uk_p, tn=1024, name="mla_k")
    v_mla = _matmul(ckv_n, w_uv_p, tn=1024, out_dtype=BF16, name="mla_v")
    q_mla, k_mla = _qk_final(q_pre, k_pre, kpe, pos, invf)
    ident = lambda hh: hh
    o_mla, lse_mla = _causal_fwd(q_mla, k_mla, v_mla, name="attn_mla_fwd")

    col = lambda part: (lambda hh: (hh // DIL_HPG) * 3 * DIL_HPG + part * DIL_HPG + hh % DIL_HPG)
    h_views = [h.reshape(S // d, d * D_MODEL) for d in DILATIONS]
    proj_dg, o_dil, lse_dil = [], [], []
    for g, d in enumerate(DILATIONS):
        pv = _proj_view(h_views[g], w_dg[g], d, att_scale, f"proj_d{g}")
        o, lse = _band_fwd(pv, pv, pv, slopes[g], n_heads=d * DIL_HPG, qcol=col(0), kcol=col(1), vcol=col(2), L=S // d,
                           slope_mul=float(d), o_shape=(S // d, d * DIL_W), name=f"attn_dil{g}_fwd")
        proj_dg.append(pv)
        o_dil.append(o)
        lse_dil.append(lse)
    y_dil = _dil_mix(o_dil, lse_dil)

    mem_n = _rmsnorm(mem, W["g_mem"], BF16, "mem_norm")
    kv_mem = _matmul(mem_n, W["w_mem_kv"], name="mem_kv")
    o_mem, lse_mem = _mem_fwd(proj_m, kv_mem, name="attn_mem_fwd")

    b_mla = _matmul(o_mla, w_br_mla_p, name="br_mla")
    b_dil = _matmul(y_dil, W["w_br_dil"], name="br_dil")
    b_mem = _matmul(o_mem, W["w_br_mem"], name="br_mem")
    merged = _merge(proj_g, W["b_gate"], [b_mla, b_dil, b_mem])
    o_proj = _matmul(merged, W["w_o"], name="o_proj")
    x1, h2 = _norm2(o_proj, x, W["g_post_mix"], W["g_pre_ffn"])

    u = _matmul(h2, W["w_ffn_up"], tn=1408, name="ffn_up")
    act = _conv_fwd(u, W["conv_w"], W["conv_b"])
    f = _matmul(act, W["w_ffn_down"], name="ffn_down")
    loss8, dx2, df, dg_post_ffn = _loss_head(f, x1, tgt, W["g_post_ffn"])
    loss = loss8[0, 0]

    G = {"g_post_ffn": dg_post_ffn}
    d_act = _matmul(df, W["w_ffn_down"], mode="nt", tn=1408, name="d_act")
    G["w_ffn_down"] = _matmul(act, df, mode="tn", tm=1408, tk=1024, name="dw_ffn_down")
    du_g, du_v, G["conv_w"], G["conv_b"] = _conv_bwd(d_act, u, W["conv_w"], W["conv_b"])
    dh2 = _matmul(du_g, W["w_ffn_up"][:, :D_FF], mode="nt", name="d_h2_gate")
    dh2 = _matmul(du_v, W["w_ffn_up"][:, D_FF:], mode="nt", add=dh2, name="d_h2_value")
    G["w_ffn_up"] = jnp.concatenate([_matmul(h2, du_g, mode="tn", tn=1408, tk=1024, name="dw_ffn_up_gate"),
                                     _matmul(h2, du_v, mode="tn", tn=1408, tk=1024, name="dw_ffn_up_value")], axis=1)
    dx1, do_proj, G["g_pre_ffn"], G["g_post_mix"] = _norm2_bwd(dx2, dh2, x1, o_proj, W["g_pre_ffn"], W["g_post_mix"])
    dmerged = _matmul(do_proj, W["w_o"], mode="nt", name="d_merged")
    G["w_o"] = _matmul(merged, do_proj, mode="tn", tk=1024, name="dw_o")
    db_mla, db_dil, db_mem, dproj_g, G["b_gate"] = _merge_bwd(dmerged, proj_g, W["b_gate"], [b_mla, b_dil, b_mem])

    dy_mem = _matmul(db_mem, W["w_br_mem"], mode="nt", name="d_y_mem")
    G["w_br_mem"] = _matmul(o_mem, db_mem, mode="tn", tk=1024, name="dw_br_mem")
    delta_mem = _delta(dy_mem, o_mem, MEM_HEADS, "delta_mem")
    dq_mem, dkv_mem = _mem_bwd(proj_m, kv_mem, dy_mem, lse_mem, delta_mem, scale=att_scale, name="attn_mem_bwd")
    G["w_mem_kv"] = _matmul(mem_n, dkv_mem, mode="tn", name="dw_mem_kv")
    dmem_n = _matmul(dkv_mem, W["w_mem_kv"], mode="nt", name="d_mem_n")
    G["g_mem"] = _gain_grad(dmem_n, mem, "dg_mem")

    dy_dil = _matmul(db_dil, W["w_br_dil"], mode="nt", name="d_y_dil")
    G["w_br_dil"] = _matmul(y_dil, db_dil, mode="tn", tk=1024, name="dw_br_dil")
    mix = _dil_mix_bwd(dy_dil, o_dil, lse_dil)
    do_dil, dl_dil = mix[:3], mix[3:]
    dh_views, dw_groups = [], []
    for g, d in enumerate(DILATIONS):
        pv = proj_dg[g]
        parts = _band_bwd(pv, pv, pv, do_dil[g], lse_dil[g], dl_dil[g], slopes[g], n_heads=d * DIL_HPG, qcol=col(0),
                          kcol=col(1), vcol=col(2), L=S // d, scale=att_scale, slope_mul=float(d), d_shape=(S // d, d * DIL_W),
                          name=f"attn_dil{g}_bwd")
        dpv = jnp.concatenate([p[:, r * DIL_W:(r + 1) * DIL_W] for r in range(d) for p in parts], axis=1)
        dh_views.append(_dh_view(dpv, w_dg[g], d, None, f"d_h_d{g}"))
        dw_groups.append(_dw_view(h_views[g], dpv, d, f"dw_in_d{g}"))
    dw_d = jnp.concatenate([dw[:, part * DIL_W:(part + 1) * DIL_W] for part in range(3) for dw in dw_groups], axis=1)

    dy_mla = _matmul(db_mla, w_br_mla_p, mode="nt", name="d_y_mla")
    dw_br_mla_p = _matmul(o_mla, db_mla, mode="tn", tk=1024, name="dw_br_mla")
    G["w_br_mla"] = _unpad_heads(dw_br_mla_p, MLA_HEADS, MLA_NOPE, 0)
    delta_mla = _delta(dy_mla, o_mla, MLA_HEADS, "delta_mla")
    dq_mla, dk_mla, dv_mla = _causal_bwd(q_mla, k_mla, v_mla, dy_mla, lse_mla, delta_mla, name="attn_mla_bwd")
    dq_pre, dkpe = _mla_bwd_prep(dq_mla, dk_mla, pos, invf)
    dcq_n = _matmul(dq_pre, w_uq_p, mode="nt", tn=Q_RANK, name="d_cq")
    G["w_uq"] = _unpad_heads(_matmul(cq_n, dq_pre, mode="tn", tm=Q_RANK, tk=1024, name="dw_uq"), MLA_HEADS, MLA_QK, 1)
    dckv_a = _matmul(dk_mla, w_uk_p, mode="nt", tn=KV_RANK, name="d_ckv_k")
    dckv_b = _matmul(dv_mla, w_uv_p, mode="nt", tn=KV_RANK, name="d_ckv_v")
    dw_uk = _unpad_heads(_matmul(ckv_n, dk_mla, mode="tn", tm=KV_RANK, tk=1024, name="dw_uk"), MLA_HEADS, MLA_NOPE, 1)
    dw_uv = _unpad_heads(_matmul(ckv_n, dv_mla, mode="tn", tm=KV_RANK, tk=1024, name="dw_uv"), MLA_HEADS, MLA_NOPE, 1)
    G["w_ukv"] = jnp.concatenate([dw_uk.reshape(KV_RANK, MLA_HEADS, MLA_NOPE), dw_uv.reshape(KV_RANK, MLA_HEADS, MLA_NOPE)],
                                 axis=2).reshape(KV_RANK, -1)
    dproj_a, G["mla_q_norm"], G["mla_kv_norm"] = _mla_norm_bwd(dcq_n, dckv_a, dckv_b, dkpe, proj_a, W["mla_q_norm"],
                                                              W["mla_kv_norm"])

    dh = _matmul(dproj_a, w_a, mode="nt", name="d_h_a")
    dh = _matmul(dq_mem, w_m, mode="nt", add=dh, name="d_h_m")
    dh = _matmul(dproj_g, w_g, mode="nt", add=dh, name="d_h_g")
    dw_a = _matmul(h, dproj_a, mode="tn", tn=768, tk=1024, name="dw_in_a")
    dw_m = _matmul(h, dq_mem, mode="tn", tn=512, tk=1024, name="dw_in_m")
    dw_g = _matmul(h, dproj_g, mode="tn", tn=1536, tk=1024, name="dw_in_g")
    kr0 = Q_RANK + MLA_NOPE
    G["w_in"] = jnp.concatenate([dw_a[:, :Q_RANK], dw_a[:, Q_RANK + LANES:], dw_a[:, kr0:kr0 + MLA_ROPE], dw_d, dw_m, dw_g], axis=1)
    grad_x, G["g_pre_mix"] = _norm1_bwd(dx1, [dh] + dh_views, x, W["g_pre_mix"])
    return loss, grad_x, G


WEIGHTS = ["g_pre_mix", "w_in", "b_gate", "mla_q_norm", "w_uq", "mla_kv_norm", "w_ukv", "g_mem", "w_mem_kv", "w_br_mla",
           "w_br_dil", "w_br_mem", "w_o", "g_post_mix", "g_pre_ffn", "w_ffn_up", "conv_w", "conv_b", "w_ffn_down", "g_post_ffn"]
GROUPS = [
    [("w_in", (D_MODEL, D_IN), 1)],
    [("w_uq", (Q_RANK, MLA_HEADS * MLA_QK), 1)],
    [("w_ukv", (KV_RANK, MLA_HEADS * 2 * MLA_NOPE), 1)],
    [("w_br_mla", (MLA_HEADS * MLA_NOPE, D_MODEL), 1), ("w_br_dil", (DIL_W, D_MODEL), 1), ("w_br_mem", (MEM_W, D_MODEL), 1)],
    [("w_mem_kv", (D_MODEL, 2 * MEM_W), 0), ("w_o", (D_MODEL, D_MODEL), 0), ("w_ffn_down", (D_FF, D_MODEL), 0)],
    [("w_ffn_up", (D_MODEL, 2 * D_FF), 1)],
]
EARLY_GROUPS = (3, 4, 5)
LATE_GROUPS = (0, 1, 2)
CONV_W = ("conv_w", (3, 2 * D_FF), 1)
REPLICATED = [("g_pre_mix", D_MODEL), ("b_gate", 3 * D_MODEL), ("mla_q_norm", Q_RANK), ("mla_kv_norm", KV_RANK), ("g_mem", D_MODEL),
              ("g_post_mix", D_MODEL), ("g_pre_ffn", D_MODEL), ("conv_b", 2 * D_FF), ("g_post_ffn", D_MODEL)]
SMALL_ROWS = 256
CONV_AT = sum(n for _, n in REPLICATED)
LOSS_AT = CONV_AT + 3 * 2 * D_FF


def _shard_shape(shape, axis):
    return tuple(d // N_CHIPS if a == axis else d for a, d in enumerate(shape))


def _group_shape(grp):
    shapes = [_shard_shape(shape, axis) for _, shape, axis in grp]
    assert len({s[1] for s in shapes}) == 1
    return sum(s[0] for s in shapes), shapes[0][1]


def _member_shards(a, axis):
    r, c = a.shape
    if axis == 0:
        return a.reshape(N_CHIPS, r // N_CHIPS, c)
    return a.reshape(r, N_CHIPS, c // N_CHIPS).transpose(1, 0, 2)


def _member_full(s, axis):
    n, r, c = s.shape
    if axis == 0:
        return s.reshape(n * r, c)
    return s.transpose(1, 0, 2).reshape(r, n * c)


def _my_weight_groups(w):
    out = []
    for grp in GROUPS:
        rows, width = _group_shape(grp)
        out.append(jnp.concatenate([w[name].astype(BF16) for name, _, _ in grp], axis=0).reshape(2, rows // 2, width))
    return out


def _full_weights(gathered, conv_all):
    out = {}
    for grp, ga in zip(GROUPS, gathered):
        ga = ga.reshape(N_CHIPS, -1, ga.shape[-1])
        off = 0
        for name, shape, axis in grp:
            rows = _shard_shape(shape, axis)[0]
            out[name] = _member_full(ga[:, off:off + rows], axis)
            off += rows
    out[CONV_W[0]] = _member_full(conv_all, CONV_W[2])
    return out


def _grad_groups(G):
    out = []
    for grp in GROUPS:
        rows, width = _group_shape(grp)
        a = jnp.concatenate([_member_shards(G[name], axis) for name, _, axis in grp], axis=1)
        out.append(a.reshape(N_CHIPS, 2, rows // 2, width).transpose(1, 0, 2, 3))
    return out


def _pack_small(vals, conv_g=None, loss=None):
    parts = [vals[name].reshape(-1) for name, _ in REPLICATED]
    if conv_g is not None:
        parts += [conv_g.reshape(-1), loss.reshape(1)]
    flat = jnp.concatenate(parts)
    return jnp.pad(flat, (0, SMALL_ROWS * LANES - flat.shape[0])).reshape(SMALL_ROWS, LANES)


def _unpack_small(packed):
    flat = packed.reshape(-1)
    out, off = {}, 0
    for name, n in REPLICATED:
        out[name] = flat[off:off + n].reshape(1, n)
        off += n
    return out


MESH = pl.DeviceIdType.MESH
HBM_SPEC = pl.BlockSpec(memory_space=pltpu.HBM)


def _place():
    x, y, c = lax.axis_index("x"), lax.axis_index("y"), lax.axis_index("c")
    chips = [(1 - x, y), (x, 1 - y), (1 - x, 1 - y)]
    return x, y, c, chips


def _remote(src, dst, send_sems, recv_sems, k, to):
    return pltpu.make_async_remote_copy(src_ref=src, dst_ref=dst, send_sem=send_sems.at[k], recv_sem=recv_sems.at[k],
                                        device_id=to, device_id_type=MESH)


def _gather_weights(groups, wholes, name, collective_id):
    n, m = len(groups), len(wholes)
    arrays = list(groups) + list(wholes)
    hbm = pltpu.MemorySpace.HBM
    srcs = [jax.new_ref(a, memory_space=hbm) for a in arrays]
    outs = [jax.empty_ref(jax.ShapeDtypeStruct((N_CHIPS,) + a.shape, a.dtype), memory_space=hbm) for a in arrays]
    n_sem = 6 * n + 3 * m

    def launch(send_sems, recv_sems):
        x, y, c, chips = _place()
        me = 2 * x + y
        sibling = (x, y, 1 - c)
        barrier = pltpu.get_barrier_semaphore()
        peers = [(px, py, c) for px, py in chips] + [sibling]
        for peer in peers:
            pl.semaphore_signal(barrier, inc=1, device_id=peer, device_id_type=MESH)
        pl.semaphore_wait(barrier, len(peers))
        first = [_remote(srcs[g].at[c], outs[g].at[me, c], send_sems, recv_sems, g * 3 + k, (px, py, c))
                 for k, (px, py) in enumerate(chips) for g in range(n)]
        first += [_remote(srcs[n + w], outs[n + w].at[me], send_sems, recv_sems, 6 * n + w * 3 + k, (px, py, c))
                  for k, (px, py) in enumerate(chips) for w in range(m)]
        for cp in first:
            cp.start()
        passed = []
        for k, (px, py) in enumerate(chips):
            for g in range(n):
                slot = outs[g].at[2 * px + py, c]
                _remote(slot, slot, send_sems, recv_sems, g * 3 + k, (px, py, c)).wait_recv()
                cp = _remote(slot, slot, send_sems, recv_sems, 3 * n + g * 3 + k, sibling)
                cp.start()
                passed.append(cp)
        for k, (px, py) in enumerate(chips):
            for w in range(m):
                slot = outs[n + w].at[2 * px + py]
                _remote(slot, slot, send_sems, recv_sems, 6 * n + w * 3 + k, (px, py, c)).wait_recv()
            for g in range(n):
                slot = outs[g].at[2 * px + py, 1 - c]
                _remote(slot, slot, send_sems, recv_sems, 3 * n + g * 3 + k, sibling).wait_recv()
        for cp in first + passed:
            cp.wait_send()

    pl.kernel(launch, mesh=plsc.ScalarSubcoreMesh(axis_name="sequencer", num_cores=1), name=name,
              scratch_types=(pltpu.SemaphoreType.DMA((n_sem,)), pltpu.SemaphoreType.DMA((n_sem,))),
              compiler_params=pltpu.CompilerParams(collective_id=collective_id))()
    me = 2 * lax.axis_index("x") + lax.axis_index("y")
    res = [lax.dynamic_update_slice_in_dim(o[...], a[None], me, axis=0) for o, a in zip(outs, arrays)]
    return res[:n], res[n:]


def _on_sequencer(exchange, arrays, out_structs, n_sem, peers, name, collective_id):
    hbm = pltpu.MemorySpace.HBM
    srcs = [jax.new_ref(a, memory_space=hbm) for a in arrays]
    outs = [jax.empty_ref(s, memory_space=hbm) for s in out_structs]

    def launch(send_sems, recv_sems):
        x, y, c, chips = _place()
        barrier = pltpu.get_barrier_semaphore()
        them = peers(x, y, c, chips)
        for peer in them:
            pl.semaphore_signal(barrier, inc=1, device_id=peer, device_id_type=MESH)
        pl.semaphore_wait(barrier, len(them))
        exchange(srcs, outs, send_sems, recv_sems)

    pl.kernel(launch, mesh=plsc.ScalarSubcoreMesh(axis_name="sequencer", num_cores=1), name=name,
              scratch_types=(pltpu.SemaphoreType.DMA((n_sem,)), pltpu.SemaphoreType.DMA((n_sem,))),
              compiler_params=pltpu.CompilerParams(collective_id=collective_id))()
    return [o[...] for o in outs]


def _sibling_only(x, y, c, chips):
    return [(x, y, 1 - c)]


def _swap_halves(groups, name, collective_id):
    n = len(groups)

    def exchange(srcs, outs, send_sems, recv_sems):
        x, y, c, _ = _place()
        cps = [_remote(srcs[g].at[1 - c], outs[g], send_sems, recv_sems, g, (x, y, 1 - c)) for g in range(n)]
        for cp in cps:
            cp.start()
        for cp in cps:
            cp.wait()

    return _on_sequencer(exchange, groups, [jax.ShapeDtypeStruct(g.shape[1:], g.dtype) for g in groups], n, _sibling_only,
                         name, collective_id)


def _scatter_partials(groups, name, collective_id):
    n = len(groups)

    def exchange(srcs, outs, send_sems, recv_sems):
        x, y, c, chips = _place()
        sends = [_remote(srcs[g].at[2 * px + py], outs[g].at[k], send_sems, recv_sems, g * 3 + k, (px, py, c))
                 for k, (px, py) in enumerate(chips) for g in range(n)]
        for cp in sends:
            cp.start()
        for k, (px, py) in enumerate(chips):
            for g in range(n):
                slot = outs[g].at[k]
                _remote(slot, slot, send_sems, recv_sems, g * 3 + k, (px, py, c)).wait_recv()
        for cp in sends:
            cp.wait_send()

    return _on_sequencer(exchange, groups, [jax.ShapeDtypeStruct((3,) + g.shape[1:], g.dtype) for g in groups], 3 * n,
                         lambda x, y, c, chips: [(px, py, c) for px, py in chips], name, collective_id)


def _share_reduced(groups, name, collective_id):
    n = len(groups)

    def exchange(srcs, outs, send_sems, recv_sems):
        x, y, c, _ = _place()
        cps = [_remote(srcs[g], outs[g], send_sems, recv_sems, g, (x, y, 1 - c)) for g in range(n)]
        for cp in cps:
            cp.start()
        for cp in cps:
            cp.wait()

    return _on_sequencer(exchange, groups, [jax.ShapeDtypeStruct(g.shape, g.dtype) for g in groups], n, _sibling_only,
                         name, collective_id)


def _allreduce_small(v):
    n_dev = 2 * N_CHIPS

    def body(src, out, buf, send_sems, recv_sems):
        x, y, c, _ = _place()
        me = 4 * x + 2 * y + c
        buf[me] = src[...]
        flips = [(k >> 2 & 1, k >> 1 & 1, k & 1) for k in range(1, n_dev)]
        sends = []
        for k, (fx, fy, fc) in enumerate(flips):
            to = ((1 - x) if fx else x, (1 - y) if fy else y, (1 - c) if fc else c)
            cp = _remote(src, buf.at[me], send_sems, recv_sems, k, to)
            cp.start()
            sends.append((cp, to))
        for k, (cp, to) in enumerate(sends):
            slot = buf.at[4 * to[0] + 2 * to[1] + to[2]]
            _remote(slot, slot, send_sems, recv_sems, k, to).wait_recv()
        for cp, _ in sends:
            cp.wait_send()
        acc = buf[0]
        for d in range(1, n_dev):
            acc = acc + buf[d]
        out[...] = acc

    vm = pl.BlockSpec(memory_space=pltpu.VMEM)
    return pl.pallas_call(
        body, name="comm_allreduce_small", in_specs=[vm], out_specs=vm, out_shape=jax.ShapeDtypeStruct(v.shape, F32),
        scratch_shapes=[pltpu.VMEM((n_dev,) + v.shape, F32), pltpu.SemaphoreType.DMA((n_dev - 1,)),
                        pltpu.SemaphoreType.DMA((n_dev - 1,))],
    )(v)


def _row_tile(rows, cap=320):
    return max(t for t in range(16, cap + 1, 16) if rows % t == 0)


def _add_sibling(mine, theirs, core, name):
    _, n, R, C = mine.shape
    t = _row_tile(R)

    def body(core_ref, a_ref, b_ref, o_ref, ob_ref):
        tot = a_ref[...] + b_ref[...]
        o_ref[...] = tot
        ob_ref[...] = tot.astype(BF16)

    sp = pl.BlockSpec((None, t, C), lambda k, i, core_ref: (k, i, 0))
    grid_spec = pltpu.PrefetchScalarGridSpec(
        num_scalar_prefetch=1, grid=(n, R // t),
        in_specs=[pl.BlockSpec((None, None, t, C), lambda k, i, core_ref: (core_ref[0], k, i, 0)), sp], out_specs=[sp, sp])
    return pl.pallas_call(body, name=name, grid_spec=grid_spec,
                          out_shape=[jax.ShapeDtypeStruct((n, R, C), F32), jax.ShapeDtypeStruct((n, R, C), BF16)],
                          compiler_params=_params("parallel", "parallel"))(core, mine, theirs)


def _add_chips(received, own, chip, name):
    n, R, C = received.shape
    t = _row_tile(R)

    def body(chip_ref, r_ref, o_ref, out_ref):
        acc = o_ref[...]
        for k in range(n):
            acc = acc + r_ref[k].astype(F32)
        out_ref[...] = acc

    grid_spec = pltpu.PrefetchScalarGridSpec(
        num_scalar_prefetch=1, grid=(R // t,),
        in_specs=[pl.BlockSpec((n, t, C), lambda i, chip_ref: (0, i, 0)), pl.BlockSpec((None, t, C), lambda i, chip_ref: (chip_ref[0], i, 0))],
        out_specs=pl.BlockSpec((t, C), lambda i, chip_ref: (i, 0)))
    return pl.pallas_call(body, name=name, grid_spec=grid_spec, out_shape=jax.ShapeDtypeStruct((R, C), F32),
                          compiler_params=_params("parallel"))(chip, received, own)


def _adamw(w, g, m, v, name, g_row0=0):
    R, C = w.shape
    t = math.gcd(math.gcd(R, g_row0), 128) if R % 8 == 0 else R
    assert g_row0 % t == 0
    c1 = 1.0 - ADAM_B1 ** ADAM_STEP
    c2 = 1.0 - ADAM_B2 ** ADAM_STEP

    def body(w_ref, g_ref, m_ref, v_ref, go_ref, d_ref, nm_ref, nv_ref):
        gv = g_ref[...]
        nm = ADAM_B1 * m_ref[...] + (1.0 - ADAM_B1) * gv
        nv = ADAM_B2 * v_ref[...] + (1.0 - ADAM_B2) * (gv * gv)
        go_ref[...] = gv
        d_ref[...] = -ADAM_LR * ((nm / c1) / (jnp.sqrt(nv / c2) + ADAM_EPS) + ADAM_WD * w_ref[...])
        nm_ref[...] = nm
        nv_ref[...] = nv

    sp = pl.BlockSpec((t, C), lambda i: (i, 0))
    g_sp = pl.BlockSpec((t, C), lambda i: (i + g_row0 // t, 0))
    return pl.pallas_call(body, name=name, grid=(R // t,), in_specs=[sp, g_sp, sp, sp], out_specs=[sp] * 4,
                          out_shape=[jax.ShapeDtypeStruct((R, C), F32)] * 4, compiler_params=_params("parallel"))(w, g, m, v)


def kernel(x, mem, positions, g_pre_mix, w_in, b_gate, mla_q_norm, w_uq, mla_kv_norm, w_ukv, g_mem, w_mem_kv, w_br_mla, w_br_dil, w_br_mem, w_o, g_post_mix, g_pre_ffn, w_ffn_up, conv_w, conv_b, w_ffn_down, g_post_ffn, loss_target, m_g_pre_mix, m_w_in, m_b_gate, m_mla_q_norm, m_w_uq, m_mla_kv_norm, m_w_ukv, m_g_mem, m_w_mem_kv, m_w_br_mla, m_w_br_dil, m_w_br_mem, m_w_o, m_g_post_mix, m_g_pre_ffn, m_w_ffn_up, m_conv_w, m_conv_b, m_w_ffn_down, m_g_post_ffn, v_g_pre_mix, v_w_in, v_b_gate, v_mla_q_norm, v_w_uq, v_mla_kv_norm, v_w_ukv, v_g_mem, v_w_mem_kv, v_w_br_mla, v_w_br_dil, v_w_br_mem, v_w_o, v_g_post_mix, v_g_pre_ffn, v_w_ffn_up, v_conv_w, v_conv_b, v_w_ffn_down, v_g_post_ffn):
    w_args = (g_pre_mix, w_in, b_gate, mla_q_norm, w_uq, mla_kv_norm, w_ukv, g_mem, w_mem_kv, w_br_mla, w_br_dil, w_br_mem, w_o,
              g_post_mix, g_pre_ffn, w_ffn_up, conv_w, conv_b, w_ffn_down, g_post_ffn)
    m_args = (m_g_pre_mix, m_w_in, m_b_gate, m_mla_q_norm, m_w_uq, m_mla_kv_norm, m_w_ukv, m_g_mem, m_w_mem_kv, m_w_br_mla,
              m_w_br_dil, m_w_br_mem, m_w_o, m_g_post_mix, m_g_pre_ffn, m_w_ffn_up, m_conv_w, m_conv_b, m_w_ffn_down, m_g_post_ffn)
    v_args = (v_g_pre_mix, v_w_in, v_b_gate, v_mla_q_norm, v_w_uq, v_mla_kv_norm, v_w_ukv, v_g_mem, v_w_mem_kv, v_w_br_mla,
              v_w_br_dil, v_w_br_mem, v_w_o, v_g_post_mix, v_g_pre_ffn, v_w_ffn_up, v_conv_w, v_conv_b, v_w_ffn_down, v_g_post_ffn)
    sharded = {name for grp in GROUPS for name, _, _ in grp} | {CONV_W[0]}

    def local(a, name):
        return a[0] if name in sharded else a

    w = {n: local(a, n) for n, a in zip(WEIGHTS, w_args)}
    m = {n: local(a, n) for n, a in zip(WEIGHTS, m_args)}
    v = {n: local(a, n) for n, a in zip(WEIGHTS, v_args)}

    mine = _my_weight_groups(w)
    head, _ = _gather_weights(mine[:1], [], "comm_gather_w_in", 0)
    rest, (conv_all,) = _gather_weights(mine[1:], [w[CONV_W[0]]], "comm_gather_rest", 1)
    full = _full_weights(head + rest, conv_all)
    full.update({name: w[name] for name, _ in REPLICATED})

    loss_local, grad_x, G = _local_step(x[0], mem[0], positions, loss_target[0], full)

    core = lax.axis_index("c").astype(jnp.int32).reshape(1)
    chip = (2 * lax.axis_index("x") + lax.axis_index("y")).astype(jnp.int32).reshape(1)
    first = lax.axis_index("c") == 0
    all_mine = _grad_groups(G)
    shard_groups = [None] * len(GROUPS)
    for tag, which, cid in (("early", EARLY_GROUPS, 2), ("late", LATE_GROUPS, 5)):
        mine = [all_mine[i] for i in which]
        theirs = _swap_halves(mine, f"comm_swap_{tag}", cid)
        partial = [_add_sibling(a, b, core, f"add_sibling_{i}") for i, a, b in zip(which, mine, theirs)]
        received = _scatter_partials([p16 for _, p16 in partial], f"comm_scatter_{tag}", cid + 1)
        reduced = [_add_chips(r, p32, chip, f"add_chips_{i}") for i, r, (p32, _) in zip(which, received, partial)]
        for i, a, b in zip(which, reduced, _share_reduced(reduced, f"comm_share_{tag}", cid + 2)):
            shard_groups[i] = jnp.concatenate([jnp.where(first, a, b), jnp.where(first, b, a)], axis=0)
    small = _allreduce_small(_pack_small(G, G[CONV_W[0]], loss_local))
    flat = small.reshape(-1)
    loss = flat[LOSS_AT]
    conv_g = flat[CONV_AT:LOSS_AT].reshape(CONV_W[1])
    conv_cols = CONV_W[1][1] // N_CHIPS
    conv_g = lax.dynamic_slice_in_dim(conv_g, (2 * lax.axis_index("x") + lax.axis_index("y")) * conv_cols, conv_cols, axis=1)

    grads, deltas, new_m, new_v = {}, {}, {}, {}
    for grp, g_all in zip(GROUPS, shard_groups):
        off = 0
        for name, _, _ in grp:
            grads[name], deltas[name], new_m[name], new_v[name] = _adamw(w[name], g_all, m[name], v[name], "adamw_" + name, off)
            off += w[name].shape[0]
    name = CONV_W[0]
    grads[name], deltas[name], new_m[name], new_v[name] = _adamw(w[name], conv_g, m[name], v[name], "adamw_" + name)
    packed = _adamw(_pack_small(w), small, _pack_small(m), _pack_small(v), "adamw_small")
    for dst, packed_small in zip((grads, deltas, new_m, new_v), packed):
        dst.update(_unpack_small(packed_small))

    def out(d, name):
        return d[name][None] if name in sharded else d[name]

    return (loss, grad_x[None], *[out(grads, n) for n in WEIGHTS], *[out(deltas, n) for n in WEIGHTS],
            *[out(new_m, n) for n in WEIGHTS], *[out(new_v, n) for n in WEIGHTS])
```

```python
import functools
import math

import jax
import jax.numpy as jnp
from jax import lax
from jax.experimental import pallas as pl
from jax.experimental.pallas import tpu as pltpu
from jax.experimental.pallas import tpu_sc as plsc

F32 = jnp.float32
BF16 = jnp.bfloat16

D_MODEL = 1024
N_MEM = 256
RMS_EPS = 1e-6
NEG_INF = -1e30
MLA_HEADS = 8
MLA_NOPE = 64
MLA_ROPE = 32
MLA_QK = 96
Q_RANK = 384
KV_RANK = 256
ROPE_THETA = 10000.0
DIL_PAIRS = ((128, 1), (512, 4), (2048, 16))
DIL_GROUPS = 3
DIL_HPG = 4
DIL_HEADS = 12
DIL_W = 512
MEM_HEADS = 4
MEM_W = 512
D_FF = 2816
OFF_Q = 384
OFF_KV = 640
OFF_KR = 672
OFF_DIL = 5280
OFF_MEMQ = 5792
D_IN = 8864
ADAM_LR = 0.001
ADAM_B1 = 0.9
ADAM_B2 = 0.999
ADAM_EPS = 1e-08
ADAM_WD = 0.01
ADAM_STEP = 10

LANES = 128
VMEM_LIMIT = 56 * 1024 * 1024

N_CHIPS = 4
ROW_TILE = 512

NN = (((1,), (0,)), ((), ()))
NT = (((1,), (1,)), ((), ()))
TN = (((0,), (0,)), ((), ()))


def _params(*sem):
    return pltpu.CompilerParams(dimension_semantics=sem, vmem_limit_bytes=VMEM_LIMIT)


def _full(shape):
    return pl.BlockSpec(shape, lambda *_: (0,) * len(shape))


def _matmul(a, b, *, mode="nn", out_dtype=F32, tm=1024, tn=1024, tk=None, add=None, scale=None, name):
    if mode == "nn":
        (M, K), N = a.shape, b.shape[1]
    elif mode == "nt":
        (M, K), N = a.shape, b.shape[0]
    else:
        (K, M), N = a.shape, b.shape[1]
    tm, tn = min(tm, M), min(tn, N)
    tk = K if tk is None else min(tk, K)
    assert M % tm == 0 and N % tn == 0 and K % tk == 0, (name, M, N, K, tm, tn, tk)
    a_spec = pl.BlockSpec((tk, tm), lambda i, j, k: (k, i)) if mode == "tn" else pl.BlockSpec((tm, tk), lambda i, j, k: (i, k))
    b_spec = pl.BlockSpec((tn, tk), lambda i, j, k: (j, k)) if mode == "nt" else pl.BlockSpec((tk, tn), lambda i, j, k: (k, j))
    o_spec = pl.BlockSpec((tm, tn), lambda i, j, k: (i, j))
    return _matmul_blocks(a, b, mode=mode, grid=(M // tm, N // tn, K // tk), a_spec=a_spec, b_spec=b_spec, o_spec=o_spec,
                          out_shape=(M, N), out_dtype=out_dtype, add=add,
                          scale=None if scale is None else (lambda j: j < scale[0], scale[1]), name=name)


def _matmul_blocks(a, b, *, mode, grid, a_spec, b_spec, o_spec, out_shape, out_dtype=F32, add=None, scale=None, name):
    nk = grid[2]
    dims = {"nn": NN, "nt": NT, "tn": TN}[mode]
    tm, tn = o_spec.block_shape
    has_add = add is not None

    def body(*refs):
        a_ref, b_ref = refs[0], refs[1]
        c_ref = refs[2] if has_add else None
        o_ref = refs[3] if has_add else refs[2]
        part = lax.dot_general(a_ref[...].astype(BF16), b_ref[...].astype(BF16), dims, preferred_element_type=F32)
        if scale is not None:
            assert nk == 1 and not has_add
            part = part * jnp.where(scale[0](pl.program_id(1)), scale[1], 1.0)
        if nk == 1:
            if has_add:
                part = part + c_ref[...]
            o_ref[...] = part.astype(out_dtype)
        else:
            acc = refs[-1]
            k = pl.program_id(2)

            @pl.when(k == 0)
            def _():
                acc[...] = part

            @pl.when(k > 0)
            def _():
                acc[...] += part

            @pl.when(k == nk - 1)
            def _():
                r = acc[...]
                if has_add:
                    r = r + c_ref[...]
                o_ref[...] = r.astype(out_dtype)

    in_specs = [a_spec, b_spec] + ([o_spec] if has_add else [])
    args = (a, b) + ((add,) if has_add else ())
    return pl.pallas_call(
        body, name=name, grid=grid, in_specs=in_specs, out_specs=o_spec,
        out_shape=jax.ShapeDtypeStruct(out_shape, out_dtype),
        scratch_shapes=[pltpu.VMEM((tm, tn), F32)] if nk > 1 else [],
        compiler_params=_params("parallel", "parallel", "arbitrary"),
    )(*args)


def _proj_view(h_view, w, d, q_scale, name):
    L, K = h_view.shape[0], h_view.shape[1] // d
    N = w.shape[1]
    tn = N // 3
    tm = min(1024, L)
    return _matmul_blocks(
        h_view, w, mode="nn", grid=(L // tm, 3 * d, 1), a_spec=pl.BlockSpec((tm, K), lambda i, j, k: (i, j // 3)),
        b_spec=pl.BlockSpec((K, tn), lambda i, j, k: (0, j % 3)), o_spec=pl.BlockSpec((tm, tn), lambda i, j, k: (i, j)),
        out_shape=(L, d * N), out_dtype=BF16, scale=(lambda j: j % 3 == 0, q_scale), name=name)


def _dh_view(dp_view, w, d, add, name):
    L, N = dp_view.shape[0], dp_view.shape[1] // d
    K = w.shape[0]
    tm = min(1024, L)
    return _matmul_blocks(
        dp_view, w, mode="nt", grid=(L // tm, d, 1), a_spec=pl.BlockSpec((tm, N), lambda i, j, k: (i, j)),
        b_spec=pl.BlockSpec((K, N), lambda i, j, k: (0, 0)), o_spec=pl.BlockSpec((tm, K), lambda i, j, k: (i, j)),
        out_shape=(L, d * K), add=add, name=name)


def _dw_view(h_view, dp_view, d, name):
    L, K = h_view.shape[0], h_view.shape[1] // d
    N = dp_view.shape[1] // d
    tk = min(1024, L)
    nl = L // tk
    return _matmul_blocks(
        h_view, dp_view, mode="tn", grid=(1, 1, d * nl), a_spec=pl.BlockSpec((tk, K), lambda i, j, k: (k % nl, k // nl)),
        b_spec=pl.BlockSpec((tk, N), lambda i, j, k: (k % nl, k // nl)), o_spec=pl.BlockSpec((K, N), lambda i, j, k: (0, 0)),
        out_shape=(K, N), name=name)


def _rms_fwd_val(x, g):
    r = lax.rsqrt(jnp.mean(x * x, axis=-1, keepdims=True) + RMS_EPS)
    return (x * r) * g


def _rms_bwd_val(dy, x, g):
    r = lax.rsqrt(jnp.mean(x * x, axis=-1, keepdims=True) + RMS_EPS)
    xn = x * r
    gdy = g * dy
    dx = r * (gdy - xn * jnp.mean(gdy * xn, axis=-1, keepdims=True))
    return dx, dy * xn


def _rope_tables(pos, invf, inverse):
    ang = pos * invf
    cos, sin = jnp.cos(ang), jnp.sin(ang)
    lane = lax.broadcasted_iota(jnp.int32, ang.shape, 1)
    first = (lane >= MLA_NOPE) & (lane < MLA_NOPE + MLA_ROPE // 2)
    second = (lane >= MLA_NOPE + MLA_ROPE // 2) & (lane < MLA_QK)
    sgn = -1.0 if inverse else 1.0
    sa = jnp.where(first, -sgn * sin, 0.0)
    sb = jnp.where(second, sgn * sin, 0.0)
    return cos, sa, sb


def _rope_val(x, cos, sa, sb):
    half = MLA_ROPE // 2
    return x * cos + pltpu.roll(x, LANES - half, 1) * sa + pltpu.roll(x, half, 1) * sb


def _head_sum_bcast(v, n_heads):
    parts = []
    for h in range(n_heads):
        s = jnp.sum(v[:, h * LANES:(h + 1) * LANES], axis=1, keepdims=True)
        parts.append(jnp.broadcast_to(s, (v.shape[0], LANES)))
    return parts


def _row_spec(t, w):
    return pl.BlockSpec((t, w), lambda i: (i, 0))


def _acc_spec(w, rows=1):
    return pl.BlockSpec((rows, w), lambda i: (0, 0))


def _rmsnorm(x, g, out_dtype, name):
    S, W = x.shape
    t = min(ROW_TILE, S)

    def body(x_ref, g_ref, o_ref):
        o_ref[...] = _rms_fwd_val(x_ref[...], g_ref[...]).astype(out_dtype)

    return pl.pallas_call(body, name=name, grid=(S // t,), in_specs=[_row_spec(t, W), _acc_spec(W)],
                          out_specs=_row_spec(t, W), out_shape=jax.ShapeDtypeStruct((S, W), out_dtype),
                          compiler_params=_params("parallel"))(x, g)


def _mla_prep(proj_a, pos, invf, q_norm, kv_norm):
    S = proj_a.shape[0]
    t = ROW_TILE

    def body(a_ref, pos_ref, invf_ref, qn_ref, kvn_ref, cq_ref, ckv_ref, kpe_ref):
        a = a_ref[...]
        cq_ref[...] = _rms_fwd_val(a[:, 0:Q_RANK], qn_ref[...]).astype(BF16)
        ckv_ref[...] = _rms_fwd_val(a[:, Q_RANK + LANES:], kvn_ref[...]).astype(BF16)
        cos, sa, sb = _rope_tables(pos_ref[...], invf_ref[...], False)
        kpe_ref[...] = _rope_val(a[:, Q_RANK:Q_RANK + LANES], cos, sa, sb)

    return pl.pallas_call(
        body, name="mla_prep", grid=(S // t,),
        in_specs=[_row_spec(t, 768), _row_spec(t, 1), _acc_spec(LANES), _acc_spec(Q_RANK), _acc_spec(KV_RANK)],
        out_specs=[_row_spec(t, Q_RANK), _row_spec(t, KV_RANK), _row_spec(t, LANES)],
        out_shape=[jax.ShapeDtypeStruct((S, Q_RANK), BF16), jax.ShapeDtypeStruct((S, KV_RANK), BF16),
                   jax.ShapeDtypeStruct((S, LANES), F32)],
        compiler_params=_params("parallel"))(proj_a, pos, invf, q_norm, kv_norm)


def _qk_final(q_pre, k_pre, kpe, pos, invf):
    S, W = q_pre.shape
    t = ROW_TILE
    scale = MLA_QK ** -0.5

    def body(q_ref, k_ref, kpe_ref, pos_ref, invf_ref, qo_ref, ko_ref):
        cos, sa, sb = _rope_tables(pos_ref[...], invf_ref[...], False)
        kpe_v = kpe_ref[...]
        for h in range(MLA_HEADS):
            sl = slice(h * LANES, (h + 1) * LANES)
            qo_ref[:, sl] = (_rope_val(q_ref[:, sl], cos, sa, sb) * scale).astype(BF16)
            ko_ref[:, sl] = (k_ref[:, sl] + kpe_v).astype(BF16)

    return pl.pallas_call(
        body, name="qk_final", grid=(S // t,),
        in_specs=[_row_spec(t, W), _row_spec(t, W), _row_spec(t, LANES), _row_spec(t, 1), _acc_spec(LANES)],
        out_specs=[_row_spec(t, W), _row_spec(t, W)],
        out_shape=[jax.ShapeDtypeStruct((S, W), BF16)] * 2,
        compiler_params=_params("parallel"))(q_pre, k_pre, kpe, pos, invf)


def _mla_bwd_prep(dq, dk, pos, invf):
    S, W = dq.shape
    t = ROW_TILE
    scale = MLA_QK ** -0.5

    def body(dq_ref, dk_ref, pos_ref, invf_ref, dqp_ref, dkpe_ref):
        cos, sa, sb = _rope_tables(pos_ref[...], invf_ref[...], True)
        tot = jnp.zeros((t, LANES), F32)
        for h in range(MLA_HEADS):
            sl = slice(h * LANES, (h + 1) * LANES)
            dqp_ref[:, sl] = _rope_val(dq_ref[:, sl] * scale, cos, sa, sb).astype(BF16)
            tot = tot + dk_ref[:, sl]
        lane = lax.broadcasted_iota(jnp.int32, tot.shape, 1)
        tot = jnp.where((lane >= MLA_NOPE) & (lane < MLA_QK), tot, 0.0)
        dkpe_ref[...] = _rope_val(tot, cos, sa, sb)

    return pl.pallas_call(
        body, name="mla_bwd_prep", grid=(S // t,),
        in_specs=[_row_spec(t, W), _row_spec(t, W), _row_spec(t, 1), _acc_spec(LANES)],
        out_specs=[_row_spec(t, W), _row_spec(t, LANES)],
        out_shape=[jax.ShapeDtypeStruct((S, W), BF16), jax.ShapeDtypeStruct((S, LANES), F32)],
        compiler_params=_params("parallel"))(dq, dk, pos, invf)


def _mla_norm_bwd(dcq, dckv_a, dckv_b, dkpe, proj_a, q_norm, kv_norm):
    S = proj_a.shape[0]
    t = ROW_TILE

    def body(dcq_ref, da_ref, db_ref, dkpe_ref, a_ref, qn_ref, kvn_ref, o_ref, dqn_ref, dkvn_ref):
        i = pl.program_id(0)
        a = a_ref[...]
        dxq, gq = _rms_bwd_val(dcq_ref[...], a[:, 0:Q_RANK], qn_ref[...])
        dxkv, gkv = _rms_bwd_val(da_ref[...] + db_ref[...], a[:, Q_RANK + LANES:], kvn_ref[...])
        o_ref[:, 0:Q_RANK] = dxq.astype(BF16)
        o_ref[:, Q_RANK:Q_RANK + LANES] = dkpe_ref[...].astype(BF16)
        o_ref[:, Q_RANK + LANES:] = dxkv.astype(BF16)

        @pl.when(i == 0)
        def _():
            dqn_ref[...] = jnp.zeros_like(dqn_ref)
            dkvn_ref[...] = jnp.zeros_like(dkvn_ref)

        dqn_ref[...] += jnp.sum(gq, axis=0, keepdims=True)
        dkvn_ref[...] += jnp.sum(gkv, axis=0, keepdims=True)

    return pl.pallas_call(
        body, name="mla_norm_bwd", grid=(S // t,),
        in_specs=[_row_spec(t, Q_RANK), _row_spec(t, KV_RANK), _row_spec(t, KV_RANK), _row_spec(t, LANES),
                  _row_spec(t, 768), _acc_spec(Q_RANK), _acc_spec(KV_RANK)],
        out_specs=[_row_spec(t, 768), _acc_spec(Q_RANK), _acc_spec(KV_RANK)],
        out_shape=[jax.ShapeDtypeStruct((S, 768), BF16), jax.ShapeDtypeStruct((1, Q_RANK), F32),
                   jax.ShapeDtypeStruct((1, KV_RANK), F32)],
        compiler_params=_params("arbitrary"))(dcq, dckv_a, dckv_b, dkpe, proj_a, q_norm, kv_norm)


DILATIONS = tuple(d for _, d in DIL_PAIRS)


def _to_tokens(view, scr, d):
    if d == 1:
        return view
    n, w = view.shape[0], view.shape[1] // d
    for r in range(d):
        for c in range(w // LANES):
            scr[pl.ds(c, 1), pl.ds(r, n, stride=d), :] = view[:, r * w + c * LANES:r * w + (c + 1) * LANES][None]
    return jnp.concatenate([scr[c] for c in range(w // LANES)], axis=1)


def _from_tokens(tok, scr, d, out_ref):
    if d == 1:
        out_ref[...] = tok.astype(out_ref.dtype)
        return
    n, w = tok.shape[0] // d, tok.shape[1]
    for c in range(w // LANES):
        scr[c] = tok[:, c * LANES:(c + 1) * LANES]
    for r in range(d):
        for c in range(w // LANES):
            out_ref[:, r * w + c * LANES:r * w + (c + 1) * LANES] = scr[pl.ds(c, 1), pl.ds(r, n, stride=d), :][0].astype(out_ref.dtype)


def _token_scratch(t, w):
    return pltpu.VMEM((w // LANES, t, LANES), F32)


def _view_spec(t, d):
    return pl.BlockSpec((t // d, d * DIL_W), lambda i: (i, 0))


def _mix_weights(ls):
    m = jnp.maximum(jnp.maximum(ls[0], ls[1]), ls[2])
    es = [jnp.exp(l - m) for l in ls]
    den = es[0] + es[1] + es[2]
    return [e / den for e in es]


def _dil_mix(o_list, lse_list):
    S = o_list[0].shape[0] * DILATIONS[0]
    t = ROW_TILE
    specs = [_view_spec(t, d) for d in DILATIONS]

    def body(o0, o1, o2, l0, l1, l2, y_ref, *scr):
        os_ = [_to_tokens(r[...], scr[g], d) for g, (r, d) in enumerate(zip((o0, o1, o2), DILATIONS))]
        ws = _mix_weights([_to_tokens(r[...], scr[3 + g], d) for g, (r, d) in enumerate(zip((l0, l1, l2), DILATIONS))])
        y_ref[...] = (ws[0] * os_[0] + ws[1] * os_[1] + ws[2] * os_[2]).astype(BF16)

    return pl.pallas_call(
        body, name="dil_mix", grid=(S // t,), in_specs=specs * 2, out_specs=_row_spec(t, DIL_W),
        out_shape=jax.ShapeDtypeStruct((S, DIL_W), BF16), scratch_shapes=[_token_scratch(t, DIL_W)] * 6,
        compiler_params=_params("parallel"))(*o_list, *lse_list)


def _dil_mix_bwd(dy, o_list, lse_list):
    S = dy.shape[0]
    t = ROW_TILE
    specs = [_view_spec(t, d) for d in DILATIONS]

    def body(dy_ref, o0, o1, o2, l0, l1, l2, d0, d1, d2, e0, e1, e2, *scr):
        os_ = [_to_tokens(r[...], scr[g], d) for g, (r, d) in enumerate(zip((o0, o1, o2), DILATIONS))]
        ws = _mix_weights([_to_tokens(r[...], scr[3 + g], d) for g, (r, d) in enumerate(zip((l0, l1, l2), DILATIONS))])
        dyv = dy_ref[...]
        y = ws[0] * os_[0] + ws[1] * os_[1] + ws[2] * os_[2]
        b = jnp.concatenate(_head_sum_bcast(dyv * y, DIL_HPG), axis=1)
        for g, (w, d_ref, e_ref, d) in enumerate(zip(ws, (d0, d1, d2), (e0, e1, e2), DILATIONS)):
            _from_tokens(w * dyv, scr[6 + g], d, d_ref)
            _from_tokens(w * b, scr[9 + g], d, e_ref)

    shapes = [(S // d, d * DIL_W) for d in DILATIONS]
    return pl.pallas_call(
        body, name="dil_mix_bwd", grid=(S // t,), in_specs=[_row_spec(t, DIL_W)] + specs * 2, out_specs=specs * 2,
        out_shape=[jax.ShapeDtypeStruct(s, BF16) for s in shapes] + [jax.ShapeDtypeStruct(s, F32) for s in shapes],
        scratch_shapes=[_token_scratch(t, DIL_W)] * 12, compiler_params=_params("parallel"))(dy, *o_list, *lse_list)


def _delta(do, o, n_heads, name):
    S, W = do.shape
    t = ROW_TILE

    def body(do_ref, o_ref, d_ref):
        prod = do_ref[...].astype(F32) * o_ref[...].astype(F32)
        d_ref[...] = jnp.concatenate(_head_sum_bcast(prod, n_heads), axis=1)

    return pl.pallas_call(body, name=name, grid=(S // t,), in_specs=[_row_spec(t, W)] * 2, out_specs=_row_spec(t, W),
                          out_shape=jax.ShapeDtypeStruct((S, W), F32), compiler_params=_params("parallel"))(do, o)


def _merge(proj_g, b_gate, b_list):
    S = proj_g.shape[0]
    t = ROW_TILE

    def body(g_ref, b_ref, y0, y1, y2, o_ref):
        acc = jnp.zeros((t, D_MODEL), F32)
        for i, y in enumerate((y0, y1, y2)):
            sl = slice(i * D_MODEL, (i + 1) * D_MODEL)
            acc = acc + jax.nn.sigmoid(g_ref[:, sl] + b_ref[:, sl]) * y[...]
        o_ref[...] = acc.astype(BF16)

    return pl.pallas_call(
        body, name="merge", grid=(S // t,),
        in_specs=[_row_spec(t, 3 * D_MODEL), _acc_spec(3 * D_MODEL)] + [_row_spec(t, D_MODEL)] * 3,
        out_specs=_row_spec(t, D_MODEL), out_shape=jax.ShapeDtypeStruct((S, D_MODEL), BF16),
        compiler_params=_params("parallel"))(proj_g, b_gate, *b_list)


def _merge_bwd(dmerged, proj_g, b_gate, b_list):
    S = proj_g.shape[0]
    t = ROW_TILE

    def body(dm_ref, g_ref, b_ref, y0, y1, y2, d0, d1, d2, dz_ref, db_ref):
        i = pl.program_id(0)

        @pl.when(i == 0)
        def _():
            db_ref[...] = jnp.zeros_like(db_ref)

        dm = dm_ref[...]
        for k, (y, d_ref) in enumerate(zip((y0, y1, y2), (d0, d1, d2))):
            sl = slice(k * D_MODEL, (k + 1) * D_MODEL)
            s = jax.nn.sigmoid(g_ref[:, sl] + b_ref[:, sl])
            d_ref[...] = (s * dm).astype(BF16)
            dz = dm * y[...] * (s * (1.0 - s))
            dz_ref[:, sl] = dz.astype(BF16)
            db_ref[:, sl] += jnp.sum(dz, axis=0, keepdims=True)

    return pl.pallas_call(
        body, name="merge_bwd", grid=(S // t,),
        in_specs=[_row_spec(t, D_MODEL), _row_spec(t, 3 * D_MODEL), _acc_spec(3 * D_MODEL)] + [_row_spec(t, D_MODEL)] * 3,
        out_specs=[_row_spec(t, D_MODEL)] * 3 + [_row_spec(t, 3 * D_MODEL), _acc_spec(3 * D_MODEL)],
        out_shape=[jax.ShapeDtypeStruct((S, D_MODEL), BF16)] * 3
        + [jax.ShapeDtypeStruct((S, 3 * D_MODEL), BF16), jax.ShapeDtypeStruct((1, 3 * D_MODEL), F32)],
        compiler_params=_params("arbitrary"))(dmerged, proj_g, b_gate, *b_list)


def _norm2(o, x, g_post, g_pre):
    S = x.shape[0]
    t = ROW_TILE

    def body(o_ref, x_ref, gp_ref, gf_ref, x1_ref, h2_ref):
        x1 = x_ref[...] + _rms_fwd_val(o_ref[...], gp_ref[...])
        x1_ref[...] = x1
        h2_ref[...] = _rms_fwd_val(x1, gf_ref[...]).astype(BF16)

    return pl.pallas_call(
        body, name="norm2", grid=(S // t,),
        in_specs=[_row_spec(t, D_MODEL)] * 2 + [_acc_spec(D_MODEL)] * 2, out_specs=[_row_spec(t, D_MODEL)] * 2,
        out_shape=[jax.ShapeDtypeStruct((S, D_MODEL), F32), jax.ShapeDtypeStruct((S, D_MODEL), BF16)],
        compiler_params=_params("parallel"))(o, x, g_post, g_pre)


def _norm2_bwd(dx2, dh2, x1, o, g_pre, g_post):
    S = x1.shape[0]
    t = ROW_TILE

    def body(dx2_ref, dh2_ref, x1_ref, o_ref, gf_ref, gp_ref, dx1_ref, do_ref, dgf_ref, dgp_ref):
        i = pl.program_id(0)

        @pl.when(i == 0)
        def _():
            dgf_ref[...] = jnp.zeros_like(dgf_ref)
            dgp_ref[...] = jnp.zeros_like(dgp_ref)

        d1, gf = _rms_bwd_val(dh2_ref[...], x1_ref[...], gf_ref[...])
        dx1 = dx2_ref[...] + d1
        dx1_ref[...] = dx1
        do, gp = _rms_bwd_val(dx1, o_ref[...], gp_ref[...])
        do_ref[...] = do.astype(BF16)
        dgf_ref[...] += jnp.sum(gf, axis=0, keepdims=True)
        dgp_ref[...] += jnp.sum(gp, axis=0, keepdims=True)

    return pl.pallas_call(
        body, name="norm2_bwd", grid=(S // t,),
        in_specs=[_row_spec(t, D_MODEL)] * 4 + [_acc_spec(D_MODEL)] * 2,
        out_specs=[_row_spec(t, D_MODEL)] * 2 + [_acc_spec(D_MODEL)] * 2,
        out_shape=[jax.ShapeDtypeStruct((S, D_MODEL), F32), jax.ShapeDtypeStruct((S, D_MODEL), BF16),
                   jax.ShapeDtypeStruct((1, D_MODEL), F32), jax.ShapeDtypeStruct((1, D_MODEL), F32)],
        compiler_params=_params("arbitrary"))(dx2, dh2, x1, o, g_pre, g_post)


def _norm1_bwd(dx1, dh_list, x, g):
    S = x.shape[0]
    t = ROW_TILE
    dils = (1,) + DILATIONS
    assert len(dh_list) == len(dils)

    def body(dx1_ref, *refs):
        dh_refs, (x_ref, g_ref, dx_ref, dg_ref), scr = refs[:len(dils)], refs[len(dils):len(dils) + 4], refs[len(dils) + 4:]
        i = pl.program_id(0)

        @pl.when(i == 0)
        def _():
            dg_ref[...] = jnp.zeros_like(dg_ref)

        dh = dh_refs[0][...]
        for k in range(1, len(dils)):
            dh = dh + _to_tokens(dh_refs[k][...], scr[k - 1], dils[k])
        d, gg = _rms_bwd_val(dh, x_ref[...], g_ref[...])
        dx_ref[...] = dx1_ref[...] + d
        dg_ref[...] += jnp.sum(gg, axis=0, keepdims=True)

    dh_specs = [pl.BlockSpec((t // d, d * D_MODEL), lambda i: (i, 0)) for d in dils]
    return pl.pallas_call(
        body, name="norm1_bwd", grid=(S // t,),
        in_specs=[_row_spec(t, D_MODEL)] + dh_specs + [_row_spec(t, D_MODEL), _acc_spec(D_MODEL)],
        out_specs=[_row_spec(t, D_MODEL), _acc_spec(D_MODEL)],
        out_shape=[jax.ShapeDtypeStruct((S, D_MODEL), F32), jax.ShapeDtypeStruct((1, D_MODEL), F32)],
        scratch_shapes=[_token_scratch(t, D_MODEL)] * (len(dils) - 1),
        compiler_params=_params("arbitrary"))(dx1, *dh_list, x, g)


def _gain_grad(dy, x, name):
    R, W = x.shape

    def body(dy_ref, x_ref, dg_ref):
        xv = x_ref[...]
        r = lax.rsqrt(jnp.mean(xv * xv, axis=-1, keepdims=True) + RMS_EPS)
        dg_ref[...] = jnp.sum(dy_ref[...] * (xv * r), axis=0, keepdims=True)

    return pl.pallas_call(body, name=name, grid=(1,), in_specs=[_full((R, W))] * 2, out_specs=_full((1, W)),
                          out_shape=jax.ShapeDtypeStruct((1, W), F32), compiler_params=_params("arbitrary"))(dy, x)


def _loss_head(f, x1, tgt, g):
    S = f.shape[0]
    t = ROW_TILE

    def body(f_ref, x1_ref, t_ref, g_ref, loss_ref, dx2_ref, df_ref, dg_ref):
        i = pl.program_id(0)

        @pl.when(i == 0)
        def _():
            loss_ref[...] = jnp.zeros_like(loss_ref)
            dg_ref[...] = jnp.zeros_like(dg_ref)

        fv, gv = f_ref[...], g_ref[...]
        err = x1_ref[...] + _rms_fwd_val(fv, gv) - t_ref[...]
        part = jnp.sum(jnp.mean(err * err, axis=-1, keepdims=True), axis=0, keepdims=True)
        loss_ref[...] += jnp.broadcast_to(0.5 * part, loss_ref.shape)
        dx2 = err * (1.0 / D_MODEL)
        dx2_ref[...] = dx2
        df, gg = _rms_bwd_val(dx2, fv, gv)
        df_ref[...] = df.astype(BF16)
        dg_ref[...] += jnp.sum(gg, axis=0, keepdims=True)

    return pl.pallas_call(
        body, name="loss_head", grid=(S // t,),
        in_specs=[_row_spec(t, D_MODEL)] * 3 + [_acc_spec(D_MODEL)],
        out_specs=[_acc_spec(LANES, 8), _row_spec(t, D_MODEL), _row_spec(t, D_MODEL), _acc_spec(D_MODEL)],
        out_shape=[jax.ShapeDtypeStruct((8, LANES), F32), jax.ShapeDtypeStruct((S, D_MODEL), F32),
                   jax.ShapeDtypeStruct((S, D_MODEL), BF16), jax.ShapeDtypeStruct((1, D_MODEL), F32)],
        compiler_params=_params("arbitrary"))(f, x1, tgt, g)


CONV_TC = 1408
CONV_TT = 512
HALO = 8


def _shift_down(u, halo, first):
    row = lax.broadcasted_iota(jnp.int32, u.shape, 0)
    h6 = jnp.where(first, 0.0, halo[HALO - 2:HALO - 1, :])
    h7 = jnp.where(first, 0.0, halo[HALO - 1:HALO, :])
    s1 = jnp.where(row == 0, h7, pltpu.roll(u, 1, 0))
    s2 = jnp.where(row == 0, h6, jnp.where(row == 1, h7, pltpu.roll(u, 2, 0)))
    return s1, s2


def _conv_specs(tt, n_c, n_t, lead):
    def halo_row(i):
        return jnp.maximum(i * (tt // HALO) - 1, 0) if lead else jnp.minimum((i + 1) * (tt // HALO), n_t * (tt // HALO) - 1)
    return [
        pl.BlockSpec((tt, CONV_TC), lambda j, i: (i, j)),
        pl.BlockSpec((tt, CONV_TC), lambda j, i: (i, j + n_c)),
        pl.BlockSpec((HALO, CONV_TC), lambda j, i: (halo_row(i), j)),
        pl.BlockSpec((HALO, CONV_TC), lambda j, i: (halo_row(i), j + n_c)),
    ]


def _conv_z(ug, uv, hg, hv, w_g, w_v, b_g, b_v, first):
    g1, g2 = _shift_down(ug, hg, first)
    v1, v2 = _shift_down(uv, hv, first)
    zg = b_g + w_g[0:1, :] * g2
    zg = zg + w_g[1:2, :] * g1
    zg = zg + w_g[2:3, :] * ug
    zv = b_v + w_v[0:1, :] * v2
    zv = zv + w_v[1:2, :] * v1
    zv = zv + w_v[2:3, :] * uv
    return zg, zv, (g2, g1, ug), (v2, v1, uv)


def _conv_fwd(u, conv_w, conv_b):
    S = u.shape[0]
    tt = min(CONV_TT, S)
    n_c, n_t = D_FF // CONV_TC, S // tt
    wspec = [pl.BlockSpec((3, CONV_TC), lambda j, i: (0, j)), pl.BlockSpec((3, CONV_TC), lambda j, i: (0, j + n_c)),
             pl.BlockSpec((1, CONV_TC), lambda j, i: (0, j)), pl.BlockSpec((1, CONV_TC), lambda j, i: (0, j + n_c))]

    def body(ug_ref, uv_ref, hg_ref, hv_ref, wg_ref, wv_ref, bg_ref, bv_ref, a_ref):
        first = pl.program_id(1) == 0
        zg, zv, _, _ = _conv_z(ug_ref[...], uv_ref[...], hg_ref, hv_ref, wg_ref, wv_ref, bg_ref[...], bv_ref[...], first)
        a_ref[...] = (zg * jax.nn.sigmoid(zg) * zv).astype(BF16)

    return pl.pallas_call(
        body, name="conv_fwd", grid=(n_c, n_t), in_specs=_conv_specs(tt, n_c, n_t, True) + wspec,
        out_specs=pl.BlockSpec((tt, CONV_TC), lambda j, i: (i, j)), out_shape=jax.ShapeDtypeStruct((S, D_FF), BF16),
        compiler_params=_params("parallel", "parallel"))(u, u, u, u, conv_w, conv_w, conv_b, conv_b)


def _gate_bwd(da, zg, zv):
    sg = jax.nn.sigmoid(zg)
    return da * zv * (sg * (1.0 + zg * (1.0 - sg))), da * (zg * sg)


def _conv_bwd(da, u, conv_w, conv_b):
    S = u.shape[0]
    tt = min(CONV_TT, S)
    n_c, n_t = D_FF // CONV_TC, S // tt
    wspec = [pl.BlockSpec((3, CONV_TC), lambda j, i: (0, j)), pl.BlockSpec((3, CONV_TC), lambda j, i: (0, j + n_c)),
             pl.BlockSpec((1, CONV_TC), lambda j, i: (0, j)), pl.BlockSpec((1, CONV_TC), lambda j, i: (0, j + n_c))]
    tile = pl.BlockSpec((tt, CONV_TC), lambda j, i: (i, j))
    trail = pl.BlockSpec((HALO, CONV_TC), lambda j, i: (jnp.minimum((i + 1) * (tt // HALO), n_t * (tt // HALO) - 1), j))

    def body(da_ref, ug_ref, uv_ref, hg_ref, hv_ref, tda_ref, tg_ref, tv_ref, wg_ref, wv_ref, bg_ref, bv_ref,
             dug_ref, duv_ref, dwg_ref, dwv_ref, dbg_ref, dbv_ref):
        i = pl.program_id(1)
        last = i == n_t - 1
        bg, bv = bg_ref[...], bv_ref[...]
        zg, zv, gs, vs = _conv_z(ug_ref[...], uv_ref[...], hg_ref, hv_ref, wg_ref, wv_ref, bg, bv, i == 0)
        dzg, dzv = _gate_bwd(da_ref[...], zg, zv)
        tzg, tzv, _, _ = _conv_z(tg_ref[...], tv_ref[...], ug_ref.at[pl.ds(tt - HALO, HALO), :], uv_ref.at[pl.ds(tt - HALO, HALO), :],
                                 wg_ref, wv_ref, bg, bv, False)
        tdzg, tdzv = _gate_bwd(tda_ref[...], tzg, tzv)
        row = lax.broadcasted_iota(jnp.int32, dzg.shape, 0)
        for dz, tdz, w_ref, du_ref in ((dzg, tdzg, wg_ref, dug_ref), (dzv, tdzv, wv_ref, duv_ref)):
            h0 = jnp.where(last, 0.0, tdz[0:1, :])
            h1 = jnp.where(last, 0.0, tdz[1:2, :])
            u1 = jnp.where(row == tt - 1, h0, pltpu.roll(dz, tt - 1, 0))
            u2 = jnp.where(row == tt - 1, h1, jnp.where(row == tt - 2, h0, pltpu.roll(dz, tt - 2, 0)))
            du_ref[...] = (w_ref[2:3, :] * dz + w_ref[1:2, :] * u1 + w_ref[0:1, :] * u2).astype(BF16)

        @pl.when(i == 0)
        def _():
            for r in (dwg_ref, dwv_ref, dbg_ref, dbv_ref):
                r[...] = jnp.zeros_like(r)

        for k in range(3):
            dwg_ref[k:k + 1, :] += jnp.sum(dzg * gs[k], axis=0, keepdims=True)
            dwv_ref[k:k + 1, :] += jnp.sum(dzv * vs[k], axis=0, keepdims=True)
        dbg_ref[...] += jnp.sum(dzg, axis=0, keepdims=True)
        dbv_ref[...] += jnp.sum(dzv, axis=0, keepdims=True)

    lead = _conv_specs(tt, n_c, n_t, True)
    trail_v = pl.BlockSpec((HALO, CONV_TC), lambda j, i: (jnp.minimum((i + 1) * (tt // HALO), n_t * (tt // HALO) - 1), j + n_c))
    outs = pl.pallas_call(
        body, name="conv_bwd", grid=(n_c, n_t), in_specs=[tile] + lead + [trail, trail, trail_v] + wspec,
        out_specs=[tile, tile] + [pl.BlockSpec((3, CONV_TC), lambda j, i: (0, j))] * 2 + [pl.BlockSpec((1, CONV_TC), lambda j, i: (0, j))] * 2,
        out_shape=[jax.ShapeDtypeStruct((S, D_FF), BF16)] * 2 + [jax.ShapeDtypeStruct((3, D_FF), F32)] * 2
        + [jax.ShapeDtypeStruct((1, D_FF), F32)] * 2,
        compiler_params=_params("parallel", "arbitrary"))(da, u, u, u, u, da, u, u, conv_w, conv_w, conv_b, conv_b)
    dug, duv, dwg, dwv, dbg, dbv = outs
    return dug, duv, jnp.concatenate([dwg, dwv], axis=1), jnp.concatenate([dbg, dbv], axis=1)


BAND = 128


MEM_TQ = 1024


def _mem_fwd(q, kv, *, name):
    S, W = q.shape
    M = kv.shape[0]
    nh = W // LANES
    tq = min(MEM_TQ, S)

    def body(q_ref, k_ref, v_ref, o_ref, l_ref):
        s = lax.dot_general(q_ref[...], k_ref[...].astype(BF16), NT, preferred_element_type=F32)
        m = jnp.max(s, axis=1, keepdims=True)
        p = jnp.exp(s - m)
        l = jnp.sum(p, axis=1, keepdims=True)
        o_ref[...] = (lax.dot_general(p.astype(BF16), v_ref[...].astype(BF16), NN, preferred_element_type=F32) / l).astype(BF16)
        l_ref[...] = jnp.broadcast_to(m + jnp.log(l), (tq, LANES))

    blk = pl.BlockSpec((tq, LANES), lambda hh, i: (i, hh))
    return pl.pallas_call(
        body, name=name, grid=(nh, S // tq),
        in_specs=[blk, pl.BlockSpec((M, LANES), lambda hh, i: (0, hh)), pl.BlockSpec((M, LANES), lambda hh, i: (0, hh + nh))],
        out_specs=[blk, blk], out_shape=[jax.ShapeDtypeStruct((S, W), BF16), jax.ShapeDtypeStruct((S, W), F32)],
        compiler_params=_params("parallel", "parallel"))(q, kv, kv)


def _mem_bwd(q, kv, do, lse, delta, *, scale, name):
    S, W = q.shape
    M = kv.shape[0]
    nh = W // LANES
    tq = min(MEM_TQ, S)

    def body(q_ref, k_ref, v_ref, do_ref, l_ref, d_ref, dq_ref, dk_ref, dv_ref):
        i = pl.program_id(1)

        @pl.when(i == 0)
        def _():
            dk_ref[...] = jnp.zeros_like(dk_ref)
            dv_ref[...] = jnp.zeros_like(dv_ref)

        qv = q_ref[...]
        kv_, vv = k_ref[...].astype(BF16), v_ref[...].astype(BF16)
        dov = do_ref[...].astype(BF16)
        s = lax.dot_general(qv, kv_, NT, preferred_element_type=F32)
        p = jnp.exp(s - l_ref[...][:, 0:1])
        dp = lax.dot_general(dov, vv, NT, preferred_element_type=F32)
        ds = (p * (dp - d_ref[...][:, 0:1])).astype(BF16)
        dq_ref[...] = lax.dot_general(ds, kv_, NN, preferred_element_type=F32) * scale
        dk_ref[...] += lax.dot_general(ds, qv, TN, preferred_element_type=F32)
        dv_ref[...] += lax.dot_general(p.astype(BF16), dov, TN, preferred_element_type=F32)

    blk = pl.BlockSpec((tq, LANES), lambda hh, i: (i, hh))
    kblk = pl.BlockSpec((M, LANES), lambda hh, i: (0, hh))
    vblk = pl.BlockSpec((M, LANES), lambda hh, i: (0, hh + nh))
    dq, dk, dv = pl.pallas_call(
        body, name=name, grid=(nh, S // tq), in_specs=[blk, kblk, vblk, blk, blk, blk], out_specs=[blk, kblk, kblk],
        out_shape=[jax.ShapeDtypeStruct((S, W), F32), jax.ShapeDtypeStruct((M, W), F32), jax.ShapeDtypeStruct((M, W), F32)],
        compiler_params=_params("parallel", "arbitrary"))(q, kv, kv, do, lse, delta)
    return dq, jnp.concatenate([dk, dv], axis=1)


CAUSAL_BLOCK = 512
STRIP = 32


def _causal_fwd(q, k, v, *, name):
    S, W = q.shape
    T = min(CAUSAL_BLOCK, S // 2)
    n_strips = T // STRIP

    def body(q_ref, k_ref, v_ref, o_ref, lse_ref, s0, s1, p0, p1, a0, a1, acc_scr):
        i = pl.program_id(1)
        s_scr, p_scr, a_scr = (s0, s1), (p0, p1), (a0, a1)

        def rows(j):
            return pl.ds(pl.multiple_of(j * T, T), T)

        def scores(j, slot):
            s_scr[slot][...] = lax.dot_general(q_ref[...], k_ref[rows(j), :], NT, preferred_element_type=F32)

        def softmax(slot, stats, diag):
            def strip(r):
                s = s_scr[slot][r * STRIP:(r + 1) * STRIP, :]
                if diag:
                    row = r * STRIP + lax.broadcasted_iota(jnp.int32, s.shape, 0)
                    s = jnp.where(row >= lax.broadcasted_iota(jnp.int32, s.shape, 1), s, NEG_INF)
                return s

            m_new = [jnp.maximum(m_old, jnp.max(strip(r), axis=1, keepdims=True)) for r, (m_old, _) in enumerate(stats)]
            new = []
            for r, (m_old, l_old) in enumerate(stats):
                rs = slice(r * STRIP, (r + 1) * STRIP)
                p = jnp.exp(strip(r) - m_new[r])
                alpha = jnp.exp(m_old - m_new[r])
                new.append((m_new[r], alpha * l_old + jnp.sum(p, axis=1, keepdims=True)))
                a_scr[slot][rs, :] = alpha
                p_scr[slot][rs, :] = p.astype(BF16)
            return tuple(new)

        def values(j, slot):
            acc_scr[...] = a_scr[slot][...] * acc_scr[...] + lax.dot_general(p_scr[slot][...], v_ref[rows(j), :], NN,
                                                                            preferred_element_type=F32)

        def trip(j, stats, mine, other):
            scores(j + 1, other)
            stats = softmax(mine, stats, False)
            values(jnp.maximum(j - 1, 0), other)
            return stats

        def pair(jj, stats):
            return trip(2 * jj + 1, trip(2 * jj, stats, 0, 1), 1, 0)

        def last(stats, mine, other):
            values(jnp.maximum(i - 1, 0), other)
            stats = softmax(mine, stats, True)
            values(i, mine)
            for r, (m, l) in enumerate(stats):
                rs = slice(r * STRIP, (r + 1) * STRIP)
                o_ref[rs, :] = (acc_scr[rs, :] / l).astype(BF16)
                lse_ref[rs, :] = jnp.broadcast_to(m + jnp.log(l), (STRIP, LANES))

        acc_scr[...] = jnp.zeros_like(acc_scr)
        p1[...] = jnp.zeros_like(p1)
        a1[...] = jnp.ones_like(a1)
        scores(0, 0)
        init = tuple((jnp.full((STRIP, 1), NEG_INF, F32), jnp.zeros((STRIP, 1), F32)) for _ in range(n_strips))
        stats = lax.fori_loop(0, i // 2, pair, init)

        @pl.when(i % 2 == 1)
        def _():
            last(trip(i - 1, stats, 0, 1), 1, 0)

        @pl.when(i % 2 == 0)
        def _():
            last(stats, 0, 1)

    blk = pl.BlockSpec((T, LANES), lambda hh, i: (i, hh))
    whole = pl.BlockSpec((S, LANES), lambda hh, i: (0, hh))
    return pl.pallas_call(
        body, name=name, grid=(W // LANES, S // T), in_specs=[blk, whole, whole], out_specs=[blk, blk],
        out_shape=[jax.ShapeDtypeStruct((S, W), BF16), jax.ShapeDtypeStruct((S, W), F32)],
        scratch_shapes=[pltpu.VMEM((T, T), F32)] * 2 + [pltpu.VMEM((T, T), BF16)] * 2 + [pltpu.VMEM((T, 1), F32)] * 2
        + [pltpu.VMEM((T, LANES), F32)],
        compiler_params=_params("parallel", "arbitrary"))(q, k, v)


def _causal_bwd(q, k, v, do, lse, delta, *, name):
    S, W = q.shape
    T = min(CAUSAL_BLOCK, S // 2)
    nq = S // T
    n_strips, n_col = T // STRIP, T // LANES

    def body(q_ref, k_ref, v_ref, do_ref, l_ref, d_ref, dq_ref, dk_ref, dv_ref, s0, s1, e0, e1, p0, p1, g0, g1):
        j = pl.program_id(1)
        s_scr, e_scr, p_scr, g_scr = (s0, s1), (e0, e1), (p0, p1), (g0, g1)

        @pl.when(j == 0)
        def _():
            dq_ref[...] = jnp.zeros_like(dq_ref)

        dk_ref[...] = jnp.zeros_like(dk_ref)
        dv_ref[...] = jnp.zeros_like(dv_ref)

        def rows(i):
            return pl.ds(pl.multiple_of(jnp.minimum(i, nq - 1) * T, T), T)

        def products(i, slot):
            r = rows(i)
            s_scr[slot][...] = lax.dot_general(q_ref[r, :], k_ref[...], NT, preferred_element_type=F32)
            e_scr[slot][...] = lax.dot_general(do_ref[r, :].astype(BF16), v_ref[...], NT, preferred_element_type=F32)

        def pointwise(i, slot, diag):
            base = pl.multiple_of(i * T, T)
            for r in range(n_strips):
                rs = slice(r * STRIP, (r + 1) * STRIP)
                lse_r = l_ref[pl.ds(base + r * STRIP, STRIP), :]
                del_r = d_ref[pl.ds(base + r * STRIP, STRIP), :]
                for c in range(n_col):
                    cs = slice(c * LANES, (c + 1) * LANES)
                    if diag and c * LANES > (r + 1) * STRIP - 1:
                        p_scr[slot][rs, cs] = jnp.zeros((STRIP, LANES), BF16)
                        g_scr[slot][rs, cs] = jnp.zeros((STRIP, LANES), BF16)
                        continue
                    sv = s_scr[slot][rs, cs]
                    if diag and (c + 1) * LANES - 1 > r * STRIP:
                        row = r * STRIP + lax.broadcasted_iota(jnp.int32, sv.shape, 0)
                        col = c * LANES + lax.broadcasted_iota(jnp.int32, sv.shape, 1)
                        sv = jnp.where(row >= col, sv, NEG_INF)
                    p = jnp.exp(sv - lse_r)
                    p_scr[slot][rs, cs] = p.astype(BF16)
                    g_scr[slot][rs, cs] = (p * (e_scr[slot][rs, cs] - del_r)).astype(BF16)

        def gradients(i, slot):
            r = rows(i)
            qi, doi = q_ref[r, :], do_ref[r, :].astype(BF16)
            g = g_scr[slot][...]
            dv_ref[...] += lax.dot_general(p_scr[slot][...], doi, TN, preferred_element_type=F32)
            dk_ref[...] += lax.dot_general(g, qi, TN, preferred_element_type=F32)
            dq_ref[r, :] += lax.dot_general(g, k_ref[...], NN, preferred_element_type=F32)

        products(j, 0)
        products(j + 1, 1)
        pointwise(j, 0, True)

        def trip(i, mine, other):
            products(i + 1, other)
            pointwise(i, mine, False)
            gradients(i - 1, other)

        def pair(t, _):
            trip(j + 1 + 2 * t, 1, 0)
            trip(j + 2 + 2 * t, 0, 1)
            return 0

        n_rest = nq - 1 - j
        lax.fori_loop(0, n_rest // 2, pair, 0)

        @pl.when(n_rest % 2 == 1)
        def _():
            trip(nq - 1, 1, 0)
            gradients(nq - 1, 1)

        @pl.when(n_rest % 2 == 0)
        def _():
            gradients(nq - 1, 0)

    blk = pl.BlockSpec((T, LANES), lambda hh, j: (j, hh))
    whole = pl.BlockSpec((S, LANES), lambda hh, j: (0, hh))
    return pl.pallas_call(
        body, name=name, grid=(W // LANES, S // T), in_specs=[whole, blk, blk, whole, whole, whole],
        out_specs=[whole, blk, blk], out_shape=[jax.ShapeDtypeStruct((S, W), F32)] * 3,
        scratch_shapes=[pltpu.VMEM((T, T), F32)] * 4 + [pltpu.VMEM((T, T), BF16)] * 4,
        compiler_params=_params("parallel", "arbitrary"))(q, k, v, do, lse, delta)


BAND_TQ = 512


def _band_window(i, sub, nsub, L, q_ref, k_ref, v_ref, slope):
    kw = min(2 * BAND, L)
    n = i * nsub + sub
    k0 = 0 if kw == L else pl.multiple_of(jnp.maximum(n - 1, 0) * BAND, BAND)
    win = pl.ds(k0, kw)
    qs = q_ref[sub * BAND:(sub + 1) * BAND, :]
    kwv, vwv = k_ref[win, :], v_ref[win, :]
    s = lax.dot_general(qs, kwv, NT, preferred_element_type=F32)
    dist = (n * BAND + lax.broadcasted_iota(jnp.int32, s.shape, 0)) - (k0 + lax.broadcasted_iota(jnp.int32, s.shape, 1))
    s = jnp.where((dist >= 0) & (dist <= BAND), s - slope * dist.astype(F32), NEG_INF)
    return win, qs, kwv, vwv, s


def _band_fwd(q, k, v, slopes, *, n_heads, qcol, kcol, vcol, L, slope_mul, o_shape, name):
    tq = min(BAND_TQ, L)
    nsub = tq // BAND

    def body(sl_ref, q_ref, k_ref, v_ref, o_ref, l_ref):
        hh, i = pl.program_id(0), pl.program_id(1)
        slope = sl_ref[hh % DIL_HPG] * slope_mul
        wins = [_band_window(i, sub, nsub, L, q_ref, k_ref, v_ref, slope) for sub in range(nsub)]
        ms = [jnp.max(w[4], axis=1, keepdims=True) for w in wins]
        ps = [jnp.exp(w[4] - m) for w, m in zip(wins, ms)]
        ls = [jnp.sum(p, axis=1, keepdims=True) for p in ps]
        for sub, (w, m, p, l) in enumerate(zip(wins, ms, ps, ls)):
            rows = slice(sub * BAND, (sub + 1) * BAND)
            o_ref[rows, :] = lax.dot_general(p.astype(BF16), w[3], NN, preferred_element_type=F32) / l
            l_ref[rows, :] = jnp.broadcast_to(m + jnp.log(l), (BAND, LANES))

    whole = lambda col: pl.BlockSpec((L, LANES), lambda hh, i: (0, col(hh)))
    o_spec = pl.BlockSpec((tq, LANES), lambda hh, i: (i, hh))
    return pl.pallas_call(
        body, name=name, grid=(n_heads, L // tq),
        in_specs=[pl.BlockSpec(memory_space=pltpu.SMEM), pl.BlockSpec((tq, LANES), lambda hh, i: (i, qcol(hh))), whole(kcol), whole(vcol)],
        out_specs=[o_spec, o_spec], out_shape=[jax.ShapeDtypeStruct(o_shape, F32)] * 2,
        compiler_params=_params("parallel", "arbitrary"))(slopes, q, k, v)


def _band_bwd(q, k, v, do, lse, delta, slopes, *, n_heads, qcol, kcol, vcol, L, scale, slope_mul, d_shape, name):
    tq = min(BAND_TQ, L)
    nsub = tq // BAND
    n_steps = L // tq

    def body(sl_ref, q_ref, k_ref, v_ref, do_ref, l_ref, d_ref, dq_ref, dk_ref, dv_ref, dk_acc, dv_acc):
        hh, i = pl.program_id(0), pl.program_id(1)
        slope = sl_ref[hh % DIL_HPG] * slope_mul

        @pl.when(i == 0)
        def _():
            dk_acc[...] = jnp.zeros_like(dk_acc)
            dv_acc[...] = jnp.zeros_like(dv_acc)

        blocks = range(nsub)
        rows = [slice(sub * BAND, (sub + 1) * BAND) for sub in blocks]
        wins = [_band_window(i, sub, nsub, L, q_ref, k_ref, v_ref, slope) for sub in blocks]
        dos = [do_ref[r, :] for r in rows]
        dps = [lax.dot_general(do, w[3], NT, preferred_element_type=F32) for do, w in zip(dos, wins)]
        ps = [jnp.exp(w[4] - l_ref[r, :][:, 0:1]) for w, r in zip(wins, rows)]
        dss = [(p * (dp - d_ref[r, :][:, 0:1])).astype(BF16) for p, dp, r in zip(ps, dps, rows)]
        for r, ds, w in zip(rows, dss, wins):
            dq_ref[r, :] = (lax.dot_general(ds, w[2], NN, preferred_element_type=F32) * scale).astype(BF16)
        dks = [lax.dot_general(ds, w[1], TN, preferred_element_type=F32) for ds, w in zip(dss, wins)]
        dvs = [lax.dot_general(p.astype(BF16), do, TN, preferred_element_type=F32) for p, do in zip(ps, dos)]
        for w, dk, dv in zip(wins, dks, dvs):
            dk_acc[w[0], :] += dk
            dv_acc[w[0], :] += dv

        @pl.when(i == n_steps - 1)
        def _():
            dk_ref[...] = dk_acc[...].astype(BF16)
            dv_ref[...] = dv_acc[...].astype(BF16)

    whole = lambda col: pl.BlockSpec((L, LANES), lambda hh, i: (0, col(hh)))
    blk = pl.BlockSpec((tq, LANES), lambda hh, i: (i, hh))
    ident = lambda hh: hh
    return pl.pallas_call(
        body, name=name, grid=(n_heads, n_steps),
        in_specs=[pl.BlockSpec(memory_space=pltpu.SMEM), pl.BlockSpec((tq, LANES), lambda hh, i: (i, qcol(hh))), whole(kcol), whole(vcol),
                  blk, blk, blk],
        out_specs=[blk, whole(ident), whole(ident)], out_shape=[jax.ShapeDtypeStruct(d_shape, BF16)] * 3,
        scratch_shapes=[pltpu.VMEM((L, LANES), F32)] * 2,
        compiler_params=_params("parallel", "arbitrary"))(slopes, q, k, v, do, lse, delta)


def _pad_heads(w, n_heads, width, axis):
    shp = w.shape
    new = shp[:axis] + (n_heads, width) + shp[axis + 1:]
    pad = [(0, 0)] * len(new)
    pad[axis + 1] = (0, LANES - width)
    out = jnp.pad(w.reshape(new), pad)
    return out.reshape(shp[:axis] + (n_heads * LANES,) + shp[axis + 1:])


def _unpad_heads(w, n_heads, width, axis):
    shp = w.shape
    new = shp[:axis] + (n_heads, LANES) + shp[axis + 1:]
    out = lax.slice_in_dim(w.reshape(new), 0, width, axis=axis + 1)
    return out.reshape(shp[:axis] + (n_heads * width,) + shp[axis + 1:])


def _alibi_slopes():
    s = jnp.exp2(-8.0 * jnp.arange(1, DIL_HEADS + 1, dtype=F32) / DIL_HEADS)
    return s.reshape(DIL_HPG, DIL_GROUPS).T


def _local_step(x, mem, positions, tgt, W):
    S = x.shape[0]
    pos = positions.reshape(S, 1).astype(F32)
    half = MLA_ROPE // 2
    inv_freq = ROPE_THETA ** (-jnp.arange(half, dtype=F32) / half)
    invf = jnp.zeros((1, LANES), F32).at[0, MLA_NOPE:MLA_NOPE + half].set(inv_freq).at[0, MLA_NOPE + half:MLA_QK].set(inv_freq)
    slopes = _alibi_slopes()

    w_in = W["w_in"]
    zc = lambda n: jnp.zeros((D_MODEL, n), BF16)
    w_a = jnp.concatenate([w_in[:, :OFF_Q], zc(MLA_NOPE), w_in[:, OFF_KV:OFF_KR], zc(LANES - MLA_QK), w_in[:, OFF_Q:OFF_KV]], axis=1)
    w_d, w_m, w_g = w_in[:, OFF_KR:OFF_DIL], w_in[:, OFF_DIL:OFF_MEMQ], w_in[:, OFF_MEMQ:]
    w_uq_p = _pad_heads(W["w_uq"], MLA_HEADS, MLA_QK, 1)
    ukv = W["w_ukv"].reshape(KV_RANK, MLA_HEADS, 2 * MLA_NOPE)
    w_uk_p = _pad_heads(ukv[:, :, :MLA_NOPE].reshape(KV_RANK, -1), MLA_HEADS, MLA_NOPE, 1)
    w_uv_p = _pad_heads(ukv[:, :, MLA_NOPE:].reshape(KV_RANK, -1), MLA_HEADS, MLA_NOPE, 1)
    w_br_mla_p = _pad_heads(W["w_br_mla"], MLA_HEADS, MLA_NOPE, 0)

    h = _rmsnorm(x, W["g_pre_mix"], BF16, "norm1")
    proj_a = _matmul(h, w_a, tn=768, name="proj_a")
    att_scale = LANES ** -0.5
    w_dg = [jnp.concatenate([w_d[:, (part * DIL_GROUPS + g) * DIL_W:(part * DIL_GROUPS + g + 1) * DIL_W] for part in range(3)], axis=1)
            for g in range(DIL_GROUPS)]
    proj_m = _matmul(h, w_m, tn=MEM_W, out_dtype=BF16, scale=(1, att_scale), name="proj_m")
    proj_g = _matmul(h, w_g, tn=1536, name="proj_g")

    cq_n, ckv_n, kpe = _mla_prep(proj_a, pos, invf, W["mla_q_norm"], W["mla_kv_norm"])
    q_pre = _matmul(cq_n, w_uq_p, tn=1024, name="mla_q")
    k_pre = _matmul(ckv_n, w_uk_p, tn=1024, name="mla_k")
    v_mla = _matmul(ckv_n, w_uv_p, tn=1024, out_dtype=BF16, name="mla_v")
    q_mla, k_mla = _qk_final(q_pre, k_pre, kpe, pos, invf)
    ident = lambda hh: hh
    o_mla, lse_mla = _causal_fwd(q_mla, k_mla, v_mla, name="attn_mla_fwd")

    col = lambda part: (lambda hh: (hh // DIL_HPG) * 3 * DIL_HPG + part * DIL_HPG + hh % DIL_HPG)
    h_views = [h.reshape(S // d, d * D_MODEL) for d in DILATIONS]
    proj_dg, o_dil, lse_dil = [], [], []
    for g, d in enumerate(DILATIONS):
        pv = _proj_view(h_views[g], w_dg[g], d, att_scale, f"proj_d{g}")
        o, lse = _band_fwd(pv, pv, pv, slopes[g], n_heads=d * DIL_HPG, qcol=col(0), kcol=col(1), vcol=col(2), L=S // d,
                           slope_mul=float(d), o_shape=(S // d, d * DIL_W), name=f"attn_dil{g}_fwd")
        proj_dg.append(pv)
        o_dil.append(o)
        lse_dil.append(lse)
    y_dil = _dil_mix(o_dil, lse_dil)

    mem_n = _rmsnorm(mem, W["g_mem"], BF16, "mem_norm")
    kv_mem = _matmul(mem_n, W["w_mem_kv"], name="mem_kv")
    o_mem, lse_mem = _mem_fwd(proj_m, kv_mem, name="attn_mem_fwd")

    b_mla = _matmul(o_mla, w_br_mla_p, name="br_mla")
    b_dil = _matmul(y_dil, W["w_br_dil"], name="br_dil")
    b_mem = _matmul(o_mem, W["w_br_mem"], name="br_mem")
    merged = _merge(proj_g, W["b_gate"], [b_mla, b_dil, b_mem])
    o_proj = _matmul(merged, W["w_o"], name="o_proj")
    x1, h2 = _norm2(o_proj, x, W["g_post_mix"], W["g_pre_ffn"])

    u = _matmul(h2, W["w_ffn_up"], tn=1408, name="ffn_up")
    act = _conv_fwd(u, W["conv_w"], W["conv_b"])
    f = _matmul(act, W["w_ffn_down"], name="ffn_down")
    loss8, dx2, df, dg_post_ffn = _loss_head(f, x1, tgt, W["g_post_ffn"])
    loss = loss8[0, 0]

    G = {"g_post_ffn": dg_post_ffn}
    d_act = _matmul(df, W["w_ffn_down"], mode="nt", tn=1408, name="d_act")
    G["w_ffn_down"] = _matmul(act, df, mode="tn", tm=1408, tk=1024, name="dw_ffn_down")
    du_g, du_v, G["conv_w"], G["conv_b"] = _conv_bwd(d_act, u, W["conv_w"], W["conv_b"])
    dh2 = _matmul(du_g, W["w_ffn_up"][:, :D_FF], mode="nt", name="d_h2_gate")
    dh2 = _matmul(du_v, W["w_ffn_up"][:, D_FF:], mode="nt", add=dh2, name="d_h2_value")
    G["w_ffn_up"] = jnp.concatenate([_matmul(h2, du_g, mode="tn", tn=1408, tk=1024, name="dw_ffn_up_gate"),
                                     _matmul(h2, du_v, mode="tn", tn=1408, tk=1024, name="dw_ffn_up_value")], axis=1)
    dx1, do_proj, G["g_pre_ffn"], G["g_post_mix"] = _norm2_bwd(dx2, dh2, x1, o_proj, W["g_pre_ffn"], W["g_post_mix"])
    dmerged = _matmul(do_proj, W["w_o"], mode="nt", name="d_merged")
    G["w_o"] = _matmul(merged, do_proj, mode="tn", tk=1024, name="dw_o")
    db_mla, db_dil, db_mem, dproj_g, G["b_gate"] = _merge_bwd(dmerged, proj_g, W["b_gate"], [b_mla, b_dil, b_mem])

    dy_mem = _matmul(db_mem, W["w_br_mem"], mode="nt", name="d_y_mem")
    G["w_br_mem"] = _matmul(o_mem, db_mem, mode="tn", tk=1024, name="dw_br_mem")
    delta_mem = _delta(dy_mem, o_mem, MEM_HEADS, "delta_mem")
    dq_mem, dkv_mem = _mem_bwd(proj_m, kv_mem, dy_mem, lse_mem, delta_mem, scale=att_scale, name="attn_mem_bwd")
    G["w_mem_kv"] = _matmul(mem_n, dkv_mem, mode="tn", name="dw_mem_kv")
    dmem_n = _matmul(dkv_mem, W["w_mem_kv"], mode="nt", name="d_mem_n")
    G["g_mem"] = _gain_grad(dmem_n, mem, "dg_mem")

    dy_dil = _matmul(db_dil, W["w_br_dil"], mode="nt", name="d_y_dil")
    G["w_br_dil"] = _matmul(y_dil, db_dil, mode="tn", tk=1024, name="dw_br_dil")
    mix = _dil_mix_bwd(dy_dil, o_dil, lse_dil)
    do_dil, dl_dil = mix[:3], mix[3:]
    dh_views, dw_groups = [], []
    for g, d in enumerate(DILATIONS):
        pv = proj_dg[g]
        parts = _band_bwd(pv, pv, pv, do_dil[g], lse_dil[g], dl_dil[g], slopes[g], n_heads=d * DIL_HPG, qcol=col(0),
                          kcol=col(1), vcol=col(2), L=S // d, scale=att_scale, slope_mul=float(d), d_shape=(S // d, d * DIL_W),
                          name=f"attn_dil{g}_bwd")
        dpv = jnp.concatenate([p[:, r * DIL_W:(r + 1) * DIL_W] for r in range(d) for p in parts], axis=1)
        dh_views.append(_dh_view(dpv, w_dg[g], d, None, f"d_h_d{g}"))
        dw_groups.append(_dw_view(h_views[g], dpv, d, f"dw_in_d{g}"))
    dw_d = jnp.concatenate([dw[:, part * DIL_W:(part + 1) * DIL_W] for part in range(3) for dw in dw_groups], axis=1)

    dy_mla = _matmul(db_mla, w_br_mla_p, mode="nt", name="d_y_mla")
    dw_br_mla_p = _matmul(o_mla, db_mla, mode="tn", tk=1024, name="dw_br_mla")
    G["w_br_mla"] = _unpad_heads(dw_br_mla_p, MLA_HEADS, MLA_NOPE, 0)
    delta_mla = _delta(dy_mla, o_mla, MLA_HEADS, "delta_mla")
    dq_mla, dk_mla, dv_mla = _causal_bwd(q_mla, k_mla, v_mla, dy_mla, lse_mla, delta_mla, name="attn_mla_bwd")
    dq_pre, dkpe = _mla_bwd_prep(dq_mla, dk_mla, pos, invf)
    dcq_n = _matmul(dq_pre, w_uq_p, mode="nt", tn=Q_RANK, name="d_cq")
    G["w_uq"] = _unpad_heads(_matmul(cq_n, dq_pre, mode="tn", tm=Q_RANK, tk=1024, name="dw_uq"), MLA_HEADS, MLA_QK, 1)
    dckv_a = _matmul(dk_mla, w_uk_p, mode="nt", tn=KV_RANK, name="d_ckv_k")
    dckv_b = _matmul(dv_mla, w_uv_p, mode="nt", tn=KV_RANK, name="d_ckv_v")
    dw_uk = _unpad_heads(_matmul(ckv_n, dk_mla, mode="tn", tm=KV_RANK, tk=1024, name="dw_uk"), MLA_HEADS, MLA_NOPE, 1)
    dw_uv = _unpad_heads(_matmul(ckv_n, dv_mla, mode="tn", tm=KV_RANK, tk=1024, name="dw_uv"), MLA_HEADS, MLA_NOPE, 1)
    G["w_ukv"] = jnp.concatenate([dw_uk.reshape(KV_RANK, MLA_HEADS, MLA_NOPE), dw_uv.reshape(KV_RANK, MLA_HEADS, MLA_NOPE)],
                                 axis=2).reshape(KV_RANK, -1)
    dproj_a, G["mla_q_norm"], G["mla_kv_norm"] = _mla_norm_bwd(dcq_n, dckv_a, dckv_b, dkpe, proj_a, W["mla_q_norm"],
                                                              W["mla_kv_norm"])

    dh = _matmul(dproj_a, w_a, mode="nt", name="d_h_a")
    dh = _matmul(dq_mem, w_m, mode="nt", add=dh, name="d_h_m")
    dh = _matmul(dproj_g, w_g, mode="nt", add=dh, name="d_h_g")
    dw_a = _matmul(h, dproj_a, mode="tn", tn=768, tk=1024, name="dw_in_a")
    dw_m = _matmul(h, dq_mem, mode="tn", tn=512, tk=1024, name="dw_in_m")
    dw_g = _matmul(h, dproj_g, mode="tn", tn=1536, tk=1024, name="dw_in_g")
    kr0 = Q_RANK + MLA_NOPE
    G["w_in"] = jnp.concatenate([dw_a[:, :Q_RANK], dw_a[:, Q_RANK + LANES:], dw_a[:, kr0:kr0 + MLA_ROPE], dw_d, dw_m, dw_g], axis=1)
    grad_x, G["g_pre_mix"] = _norm1_bwd(dx1, [dh] + dh_views, x, W["g_pre_mix"])
    return loss, grad_x, G


WEIGHTS = ["g_pre_mix", "w_in", "b_gate", "mla_q_norm", "w_uq", "mla_kv_norm", "w_ukv", "g_mem", "w_mem_kv", "w_br_mla",
           "w_br_dil", "w_br_mem", "w_o", "g_post_mix", "g_pre_ffn", "w_ffn_up", "conv_w", "conv_b", "w_ffn_down", "g_post_ffn"]
GROUPS = [
    [("w_in", (D_MODEL, D_IN), 1)],
    [("w_uq", (Q_RANK, MLA_HEADS * MLA_QK), 1)],
    [("w_ukv", (KV_RANK, MLA_HEADS * 2 * MLA_NOPE), 1)],
    [("w_br_mla", (MLA_HEADS * MLA_NOPE, D_MODEL), 1), ("w_br_dil", (DIL_W, D_MODEL), 1), ("w_br_mem", (MEM_W, D_MODEL), 1)],
    [("w_mem_kv", (D_MODEL, 2 * MEM_W), 0), ("w_o", (D_MODEL, D_MODEL), 0), ("w_ffn_down", (D_FF, D_MODEL), 0)],
    [("w_ffn_up", (D_MODEL, 2 * D_FF), 1)],
]
EARLY_GROUPS = (3, 4, 5)
LATE_GROUPS = (0, 1, 2)
CONV_W = ("conv_w", (3, 2 * D_FF), 1)
REPLICATED = [("g_pre_mix", D_MODEL), ("b_gate", 3 * D_MODEL), ("mla_q_norm", Q_RANK), ("mla_kv_norm", KV_RANK), ("g_mem", D_MODEL),
              ("g_post_mix", D_MODEL), ("g_pre_ffn", D_MODEL), ("conv_b", 2 * D_FF), ("g_post_ffn", D_MODEL)]
SMALL_ROWS = 256
CONV_AT = sum(n for _, n in REPLICATED)
LOSS_AT = CONV_AT + 3 * 2 * D_FF


def _shard_shape(shape, axis):
    return tuple(d // N_CHIPS if a == axis else d for a, d in enumerate(shape))


def _group_shape(grp):
    shapes = [_shard_shape(shape, axis) for _, shape, axis in grp]
    assert len({s[1] for s in shapes}) == 1
    return sum(s[0] for s in shapes), shapes[0][1]


def _member_shards(a, axis):
    r, c = a.shape
    if axis == 0:
        return a.reshape(N_CHIPS, r // N_CHIPS, c)
    return a.reshape(r, N_CHIPS, c // N_CHIPS).transpose(1, 0, 2)


def _member_full(s, axis):
    n, r, c = s.shape
    if axis == 0:
        return s.reshape(n * r, c)
    return s.transpose(1, 0, 2).reshape(r, n * c)


def _my_weight_groups(w):
    out = []
    for grp in GROUPS:
        rows, width = _group_shape(grp)
        out.append(jnp.concatenate([w[name].astype(BF16) for name, _, _ in grp], axis=0).reshape(2, rows // 2, width))
    return out


def _full_weights(gathered, conv_all):
    out = {}
    for grp, ga in zip(GROUPS, gathered):
        ga = ga.reshape(N_CHIPS, -1, ga.shape[-1])
        off = 0
        for name, shape, axis in grp:
            rows = _shard_shape(shape, axis)[0]
            out[name] = _member_full(ga[:, off:off + rows], axis)
            off += rows
    out[CONV_W[0]] = _member_full(conv_all, CONV_W[2])
    return out


def _grad_groups(G):
    out = []
    for grp in GROUPS:
        rows, width = _group_shape(grp)
        a = jnp.concatenate([_member_shards(G[name], axis) for name, _, axis in grp], axis=1)
        out.append(a.reshape(N_CHIPS, 2, rows // 2, width).transpose(1, 0, 2, 3))
    return out


def _pack_small(vals, conv_g=None, loss=None):
    parts = [vals[name].reshape(-1) for name, _ in REPLICATED]
    if conv_g is not None:
        parts += [conv_g.reshape(-1), loss.reshape(1)]
    flat = jnp.concatenate(parts)
    return jnp.pad(flat, (0, SMALL_ROWS * LANES - flat.shape[0])).reshape(SMALL_ROWS, LANES)


def _unpack_small(packed):
    flat = packed.reshape(-1)
    out, off = {}, 0
    for name, n in REPLICATED:
        out[name] = flat[off:off + n].reshape(1, n)
        off += n
    return out


MESH = pl.DeviceIdType.MESH
HBM_SPEC = pl.BlockSpec(memory_space=pltpu.HBM)


def _place():
    x, y, c = lax.axis_index("x"), lax.axis_index("y"), lax.axis_index("c")
    chips = [(1 - x, y), (x, 1 - y), (1 - x, 1 - y)]
    return x, y, c, chips


def _remote(src, dst, send_sems, recv_sems, k, to):
    return pltpu.make_async_remote_copy(src_ref=src, dst_ref=dst, send_sem=send_sems.at[k], recv_sem=recv_sems.at[k],
                                        device_id=to, device_id_type=MESH)


def _gather_weights(groups, wholes, name, collective_id):
    n, m = len(groups), len(wholes)
    arrays = list(groups) + list(wholes)
    hbm = pltpu.MemorySpace.HBM
    srcs = [jax.new_ref(a, memory_space=hbm) for a in arrays]
    outs = [jax.empty_ref(jax.ShapeDtypeStruct((N_CHIPS,) + a.shape, a.dtype), memory_space=hbm) for a in arrays]
    n_sem = 6 * n + 3 * m

    def launch(send_sems, recv_sems):
        x, y, c, chips = _place()
        me = 2 * x + y
        sibling = (x, y, 1 - c)
        barrier = pltpu.get_barrier_semaphore()
        peers = [(px, py, c) for px, py in chips] + [sibling]
        for peer in peers:
            pl.semaphore_signal(barrier, inc=1, device_id=peer, device_id_type=MESH)
        pl.semaphore_wait(barrier, len(peers))
        first = [_remote(srcs[g].at[c], outs[g].at[me, c], send_sems, recv_sems, g * 3 + k, (px, py, c))
                 for k, (px, py) in enumerate(chips) for g in range(n)]
        first += [_remote(srcs[n + w], outs[n + w].at[me], send_sems, recv_sems, 6 * n + w * 3 + k, (px, py, c))
                  for k, (px, py) in enumerate(chips) for w in range(m)]
        for cp in first:
            cp.start()
        passed = []
        for k, (px, py) in enumerate(chips):
            for g in range(n):
                slot = outs[g].at[2 * px + py, c]
                _remote(slot, slot, send_sems, recv_sems, g * 3 + k, (px, py, c)).wait_recv()
                cp = _remote(slot, slot, send_sems, recv_sems, 3 * n + g * 3 + k, sibling)
                cp.start()
                passed.append(cp)
        for k, (px, py) in enumerate(chips):
            for w in range(m):
                slot = outs[n + w].at[2 * px + py]
                _remote(slot, slot, send_sems, recv_sems, 6 * n + w * 3 + k, (px, py, c)).wait_recv()
            for g in range(n):
                slot = outs[g].at[2 * px + py, 1 - c]
                _remote(slot, slot, send_sems, recv_sems, 3 * n + g * 3 + k, sibling).wait_recv()
        for cp in first + passed:
            cp.wait_send()

    pl.kernel(launch, mesh=plsc.ScalarSubcoreMesh(axis_name="sequencer", num_cores=1), name=name,
              scratch_types=(pltpu.SemaphoreType.DMA((n_sem,)), pltpu.SemaphoreType.DMA((n_sem,))),
              compiler_params=pltpu.CompilerParams(collective_id=collective_id))()
    me = 2 * lax.axis_index("x") + lax.axis_index("y")
    res = [lax.dynamic_update_slice_in_dim(o[...], a[None], me, axis=0) for o, a in zip(outs, arrays)]
    return res[:n], res[n:]


def _on_sequencer(exchange, arrays, out_structs, n_sem, peers, name, collective_id):
    hbm = pltpu.MemorySpace.HBM
    srcs = [jax.new_ref(a, memory_space=hbm) for a in arrays]
    outs = [jax.empty_ref(s, memory_space=hbm) for s in out_structs]

    def launch(send_sems, recv_sems):
        x, y, c, chips = _place()
        barrier = pltpu.get_barrier_semaphore()
        them = peers(x, y, c, chips)
        for peer in them:
            pl.semaphore_signal(barrier, inc=1, device_id=peer, device_id_type=MESH)
        pl.semaphore_wait(barrier, len(them))
        exchange(srcs, outs, send_sems, recv_sems)

    pl.kernel(launch, mesh=plsc.ScalarSubcoreMesh(axis_name="sequencer", num_cores=1), name=name,
              scratch_types=(pltpu.SemaphoreType.DMA((n_sem,)), pltpu.SemaphoreType.DMA((n_sem,))),
              compiler_params=pltpu.CompilerParams(collective_id=collective_id))()
    return [o[...] for o in outs]


def _sibling_only(x, y, c, chips):
    return [(x, y, 1 - c)]


def _swap_halves(groups, name, collective_id):
    n = len(groups)

    def exchange(srcs, outs, send_sems, recv_sems):
        x, y, c, _ = _place()
        cps = [_remote(srcs[g].at[1 - c], outs[g], send_sems, recv_sems, g, (x, y, 1 - c)) for g in range(n)]
        for cp in cps:
            cp.start()
        for cp in cps:
            cp.wait()

    return _on_sequencer(exchange, groups, [jax.ShapeDtypeStruct(g.shape[1:], g.dtype) for g in groups], n, _sibling_only,
                         name, collective_id)


def _scatter_partials(groups, name, collective_id):
    n = len(groups)

    def exchange(srcs, outs, send_sems, recv_sems):
        x, y, c, chips = _place()
        sends = [_remote(srcs[g].at[2 * px + py], outs[g].at[k], send_sems, recv_sems, g * 3 + k, (px, py, c))
                 for k, (px, py) in enumerate(chips) for g in range(n)]
        for cp in sends:
            cp.start()
        for k, (px, py) in enumerate(chips):
            for g in range(n):
                slot = outs[g].at[k]
                _remote(slot, slot, send_sems, recv_sems, g * 3 + k, (px, py, c)).wait_recv()
        for cp in sends:
            cp.wait_send()

    return _on_sequencer(exchange, groups, [jax.ShapeDtypeStruct((3,) + g.shape[1:], g.dtype) for g in groups], 3 * n,
                         lambda x, y, c, chips: [(px, py, c) for px, py in chips], name, collective_id)


def _share_reduced(groups, name, collective_id):
    n = len(groups)

    def exchange(srcs, outs, send_sems, recv_sems):
        x, y, c, _ = _place()
        cps = [_remote(srcs[g], outs[g], send_sems, recv_sems, g, (x, y, 1 - c)) for g in range(n)]
        for cp in cps:
            cp.start()
        for cp in cps:
            cp.wait()

    return _on_sequencer(exchange, groups, [jax.ShapeDtypeStruct(g.shape, g.dtype) for g in groups], n, _sibling_only,
                         name, collective_id)


def _allreduce_small(v):
    n_dev = 2 * N_CHIPS

    def body(src, out, buf, send_sems, recv_sems):
        x, y, c, _ = _place()
        me = 4 * x + 2 * y + c
        buf[me] = src[...]
        flips = [(k >> 2 & 1, k >> 1 & 1, k & 1) for k in range(1, n_dev)]
        sends = []
        for k, (fx, fy, fc) in enumerate(flips):
            to = ((1 - x) if fx else x, (1 - y) if fy else y, (1 - c) if fc else c)
            cp = _remote(src, buf.at[me], send_sems, recv_sems, k, to)
            cp.start()
            sends.append((cp, to))
        for k, (cp, to) in enumerate(sends):
            slot = buf.at[4 * to[0] + 2 * to[1] + to[2]]
            _remote(slot, slot, send_sems, recv_sems, k, to).wait_recv()
        for cp, _ in sends:
            cp.wait_send()
        acc = buf[0]
        for d in range(1, n_dev):
            acc = acc + buf[d]
        out[...] = acc

    vm = pl.BlockSpec(memory_space=pltpu.VMEM)
    return pl.pallas_call(
        body, name="comm_allreduce_small", in_specs=[vm], out_specs=vm, out_shape=jax.ShapeDtypeStruct(v.shape, F32),
        scratch_shapes=[pltpu.VMEM((n_dev,) + v.shape, F32), pltpu.SemaphoreType.DMA((n_dev - 1,)),
                        pltpu.SemaphoreType.DMA((n_dev - 1,))],
    )(v)


def _row_tile(rows, cap=320):
    return max(t for t in range(16, cap + 1, 16) if rows % t == 0)


def _add_sibling(mine, theirs, core, name):
    _, n, R, C = mine.shape
    t = _row_tile(R)

    def body(core_ref, a_ref, b_ref, o_ref, ob_ref):
        tot = a_ref[...] + b_ref[...]
        o_ref[...] = tot
        ob_ref[...] = tot.astype(BF16)

    sp = pl.BlockSpec((None, t, C), lambda k, i, core_ref: (k, i, 0))
    grid_spec = pltpu.PrefetchScalarGridSpec(
        num_scalar_prefetch=1, grid=(n, R // t),
        in_specs=[pl.BlockSpec((None, None, t, C), lambda k, i, core_ref: (core_ref[0], k, i, 0)), sp], out_specs=[sp, sp])
    return pl.pallas_call(body, name=name, grid_spec=grid_spec,
                          out_shape=[jax.ShapeDtypeStruct((n, R, C), F32), jax.ShapeDtypeStruct((n, R, C), BF16)],
                          compiler_params=_params("parallel", "parallel"))(core, mine, theirs)


def _add_chips(received, own, chip, name):
    n, R, C = received.shape
    t = _row_tile(R)

    def body(chip_ref, r_ref, o_ref, out_ref):
        acc = o_ref[...]
        for k in range(n):
            acc = acc + r_ref[k].astype(F32)
        out_ref[...] = acc

    grid_spec = pltpu.PrefetchScalarGridSpec(
        num_scalar_prefetch=1, grid=(R // t,),
        in_specs=[pl.BlockSpec((n, t, C), lambda i, chip_ref: (0, i, 0)), pl.BlockSpec((None, t, C), lambda i, chip_ref: (chip_ref[0], i, 0))],
        out_specs=pl.BlockSpec((t, C), lambda i, chip_ref: (i, 0)))
    return pl.pallas_call(body, name=name, grid_spec=grid_spec, out_shape=jax.ShapeDtypeStruct((R, C), F32),
                          compiler_params=_params("parallel"))(chip, received, own)


def _adamw(w, g, m, v, name, g_row0=0):
    R, C = w.shape
    t = math.gcd(math.gcd(R, g_row0), 128) if R % 8 == 0 else R
    assert g_row0 % t == 0
    c1 = 1.0 - ADAM_B1 ** ADAM_STEP
    c2 = 1.0 - ADAM_B2 ** ADAM_STEP

    def body(w_ref, g_ref, m_ref, v_ref, go_ref, d_ref, nm_ref, nv_ref):
        gv = g_ref[...]
        nm = ADAM_B1 * m_ref[...] + (1.0 - ADAM_B1) * gv
        nv = ADAM_B2 * v_ref[...] + (1.0 - ADAM_B2) * (gv * gv)
        go_ref[...] = gv
        d_ref[...] = -ADAM_LR * ((nm / c1) / (jnp.sqrt(nv / c2) + ADAM_EPS) + ADAM_WD * w_ref[...])
        nm_ref[...] = nm
        nv_ref[...] = nv

    sp = pl.BlockSpec((t, C), lambda i: (i, 0))
    g_sp = pl.BlockSpec((t, C), lambda i: (i + g_row0 // t, 0))
    return pl.pallas_call(body, name=name, grid=(R // t,), in_specs=[sp, g_sp, sp, sp], out_specs=[sp] * 4,
                          out_shape=[jax.ShapeDtypeStruct((R, C), F32)] * 4, compiler_params=_params("parallel"))(w, g, m, v)


def kernel(x, mem, positions, g_pre_mix, w_in, b_gate, mla_q_norm, w_uq, mla_kv_norm, w_ukv, g_mem, w_mem_kv, w_br_mla, w_br_dil, w_br_mem, w_o, g_post_mix, g_pre_ffn, w_ffn_up, conv_w, conv_b, w_ffn_down, g_post_ffn, loss_target, m_g_pre_mix, m_w_in, m_b_gate, m_mla_q_norm, m_w_uq, m_mla_kv_norm, m_w_ukv, m_g_mem, m_w_mem_kv, m_w_br_mla, m_w_br_dil, m_w_br_mem, m_w_o, m_g_post_mix, m_g_pre_ffn, m_w_ffn_up, m_conv_w, m_conv_b, m_w_ffn_down, m_g_post_ffn, v_g_pre_mix, v_w_in, v_b_gate, v_mla_q_norm, v_w_uq, v_mla_kv_norm, v_w_ukv, v_g_mem, v_w_mem_kv, v_w_br_mla, v_w_br_dil, v_w_br_mem, v_w_o, v_g_post_mix, v_g_pre_ffn, v_w_ffn_up, v_conv_w, v_conv_b, v_w_ffn_down, v_g_post_ffn):
    w_args = (g_pre_mix, w_in, b_gate, mla_q_norm, w_uq, mla_kv_norm, w_ukv, g_mem, w_mem_kv, w_br_mla, w_br_dil, w_br_mem, w_o,
              g_post_mix, g_pre_ffn, w_ffn_up, conv_w, conv_b, w_ffn_down, g_post_ffn)
    m_args = (m_g_pre_mix, m_w_in, m_b_gate, m_mla_q_norm, m_w_uq, m_mla_kv_norm, m_w_ukv, m_g_mem, m_w_mem_kv, m_w_br_mla,
              m_w_br_dil, m_w_br_mem, m_w_o, m_g_post_mix, m_g_pre_ffn, m_w_ffn_up, m_conv_w, m_conv_b, m_w_ffn_down, m_g_post_ffn)
    v_args = (v_g_pre_mix, v_w_in, v_b_gate, v_mla_q_norm, v_w_uq, v_mla_kv_norm, v_w_ukv, v_g_mem, v_w_mem_kv, v_w_br_mla,
              v_w_br_dil, v_w_br_mem, v_w_o, v_g_post_mix, v_g_pre_ffn, v_w_ffn_up, v_conv_w, v_conv_b, v_w_ffn_down, v_g_post_ffn)
    sharded = {name for grp in GROUPS for name, _, _ in grp} | {CONV_W[0]}

    def local(a, name):
        return a[0] if name in sharded else a

    w = {n: local(a, n) for n, a in zip(WEIGHTS, w_args)}
    m = {n: local(a, n) for n, a in zip(WEIGHTS, m_args)}
    v = {n: local(a, n) for n, a in zip(WEIGHTS, v_args)}

    mine = _my_weight_groups(w)
    head, _ = _gather_weights(mine[:1], [], "comm_gather_w_in", 0)
    rest, (conv_all,) = _gather_weights(mine[1:], [w[CONV_W[0]]], "comm_gather_rest", 1)
    full = _full_weights(head + rest, conv_all)
    full.update({name: w[name] for name, _ in REPLICATED})

    loss_local, grad_x, G = _local_step(x[0], mem[0], positions, loss_target[0], full)

    core = lax.axis_index("c").astype(jnp.int32).reshape(1)
    chip = (2 * lax.axis_index("x") + lax.axis_index("y")).astype(jnp.int32).reshape(1)
    first = lax.axis_index("c") == 0
    all_mine = _grad_groups(G)
    shard_groups = [None] * len(GROUPS)
    for tag, which, cid in (("early", EARLY_GROUPS, 2), ("late", LATE_GROUPS, 5)):
        mine = [all_mine[i] for i in which]
        theirs = _swap_halves(mine, f"comm_swap_{tag}", cid)
        partial = [_add_sibling(a, b, core, f"add_sibling_{i}") for i, a, b in zip(which, mine, theirs)]
        received = _scatter_partials([p16 for _, p16 in partial], f"comm_scatter_{tag}", cid + 1)
        reduced = [_add_chips(r, p32, chip, f"add_chips_{i}") for i, r, (p32, _) in zip(which, received, partial)]
        for i, a, b in zip(which, reduced, _share_reduced(reduced, f"comm_share_{tag}", cid + 2)):
            shard_groups[i] = jnp.concatenate([jnp.where(first, a, b), jnp.where(first, b, a)], axis=0)
    small = _allreduce_small(_pack_small(G, G[CONV_W[0]], loss_local))
    flat = small.reshape(-1)
    loss = flat[LOSS_AT]
    conv_g = flat[CONV_AT:LOSS_AT].reshape(CONV_W[1])
    conv_cols = CONV_W[1][1] // N_CHIPS
    conv_g = lax.dynamic_slice_in_dim(conv_g, (2 * lax.axis_index("x") + lax.axis_index("y")) * conv_cols, conv_cols, axis=1)

    grads, deltas, new_m, new_v = {}, {}, {}, {}
    for grp, g_all in zip(GROUPS, shard_groups):
        off = 0
        for name, _, _ in grp:
            grads[name], deltas[name], new_m[name], new_v[name] = _adamw(w[name], g_all, m[name], v[name], "adamw_" + name, off)
            off += w[name].shape[0]
    name = CONV_W[0]
    grads[name], deltas[name], new_m[name], new_v[name] = _adamw(w[name], conv_g, m[name], v[name], "adamw_" + name)
    packed = _adamw(_pack_small(w), small, _pack_small(m), _pack_small(v), "adamw_small")
    for dst, packed_small in zip((grads, deltas, new_m, new_v), packed):
        dst.update(_unpack_small(packed_small))

    def out(d, name):
        return d[name][None] if name in sharded else d[name]

    return (loss, grad_x[None], *[out(grads, n) for n in WEIGHTS], *[out(deltas, n) for n in WEIGHTS],
            *[out(new_m, n) for n in WEIGHTS], *[out(new_v, n) for n in WEIGHTS])
```

```python
import functools
import math

import jax
import jax.numpy as jnp
from jax import lax
from jax.experimental import pallas as pl
from jax.experimental.pallas import tpu as pltpu
from jax.experimental.pallas import tpu_sc as plsc

F32 = jnp.float32
BF16 = jnp.bfloat16

D_MODEL = 1024
N_MEM = 256
RMS_EPS = 1e-6
NEG_INF = -1e30
MLA_HEADS = 8
MLA_NOPE = 64
MLA_ROPE = 32
MLA_QK = 96
Q_RANK = 384
KV_RANK = 256
ROPE_THETA = 10000.0
DIL_PAIRS = ((128, 1), (512, 4), (2048, 16))
DIL_GROUPS = 3
DIL_HPG = 4
DIL_HEADS = 12
DIL_W = 512
MEM_HEADS = 4
MEM_W = 512
D_FF = 2816
OFF_Q = 384
OFF_KV = 640
OFF_KR = 672
OFF_DIL = 5280
OFF_MEMQ = 5792
D_IN = 8864
ADAM_LR = 0.001
ADAM_B1 = 0.9
ADAM_B2 = 0.999
ADAM_EPS = 1e-08
ADAM_WD = 0.01
ADAM_STEP = 10

LANES = 128
VMEM_LIMIT = 56 * 1024 * 1024

N_CHIPS = 4
ROW_TILE = 512

NN = (((1,), (0,)), ((), ()))
NT = (((1,), (1,)), ((), ()))
TN = (((0,), (0,)), ((), ()))


def _params(*sem):
    return pltpu.CompilerParams(dimension_semantics=sem, vmem_limit_bytes=VMEM_LIMIT)


def _full(shape):
    return pl.BlockSpec(shape, lambda *_: (0,) * len(shape))


def _matmul(a, b, *, mode="nn", out_dtype=F32, tm=1024, tn=1024, tk=None, add=None, scale=None, name):
    if mode == "nn":
        (M, K), N = a.shape, b.shape[1]
    elif mode == "nt":
        (M, K), N = a.shape, b.shape[0]
    else:
        (K, M), N = a.shape, b.shape[1]
    tm, tn = min(tm, M), min(tn, N)
    tk = K if tk is None else min(tk, K)
    assert M % tm == 0 and N % tn == 0 and K % tk == 0, (name, M, N, K, tm, tn, tk)
    a_spec = pl.BlockSpec((tk, tm), lambda i, j, k: (k, i)) if mode == "tn" else pl.BlockSpec((tm, tk), lambda i, j, k: (i, k))
    b_spec = pl.BlockSpec((tn, tk), lambda i, j, k: (j, k)) if mode == "nt" else pl.BlockSpec((tk, tn), lambda i, j, k: (k, j))
    o_spec = pl.BlockSpec((tm, tn), lambda i, j, k: (i, j))
    return _matmul_blocks(a, b, mode=mode, grid=(M // tm, N // tn, K // tk), a_spec=a_spec, b_spec=b_spec, o_spec=o_spec,
                          out_shape=(M, N), out_dtype=out_dtype, add=add,
                          scale=None if scale is None else (lambda j: j < scale[0], scale[1]), name=name)


def _matmul_blocks(a, b, *, mode, grid, a_spec, b_spec, o_spec, out_shape, out_dtype=F32, add=None, scale=None, name):
    nk = grid[2]
    dims = {"nn": NN, "nt": NT, "tn": TN}[mode]
    tm, tn = o_spec.block_shape
    has_add = add is not None

    def body(*refs):
        a_ref, b_ref = refs[0], refs[1]
        c_ref = refs[2] if has_add else None
        o_ref = refs[3] if has_add else refs[2]
        part = lax.dot_general(a_ref[...].astype(BF16), b_ref[...].astype(BF16), dims, preferred_element_type=F32)
        if scale is not None:
            assert nk == 1 and not has_add
            part = part * jnp.where(scale[0](pl.program_id(1)), scale[1], 1.0)
        if nk == 1:
            if has_add:
                part = part + c_ref[...]
            o_ref[...] = part.astype(out_dtype)
        else:
            acc = refs[-1]
            k = pl.program_id(2)

            @pl.when(k == 0)
            def _():
                acc[...] = part

            @pl.when(k > 0)
            def _():
                acc[...] += part

            @pl.when(k == nk - 1)
            def _():
                r = acc[...]
                if has_add:
                    r = r + c_ref[...]
                o_ref[...] = r.astype(out_dtype)

    in_specs = [a_spec, b_spec] + ([o_spec] if has_add else [])
    args = (a, b) + ((add,) if has_add else ())
    return pl.pallas_call(
        body, name=name, grid=grid, in_specs=in_specs, out_specs=o_spec,
        out_shape=jax.ShapeDtypeStruct(out_shape, out_dtype),
        scratch_shapes=[pltpu.VMEM((tm, tn), F32)] if nk > 1 else [],
        compiler_params=_params("parallel", "parallel", "arbitrary"),
    )(*args)


def _proj_view(h_view, w, d, q_scale, name):
    L, K = h_view.shape[0], h_view.shape[1] // d
    N = w.shape[1]
    tn = N // 3
    tm = min(1024, L)
    return _matmul_blocks(
        h_view, w, mode="nn", grid=(L // tm, 3 * d, 1), a_spec=pl.BlockSpec((tm, K), lambda i, j, k: (i, j // 3)),
        b_spec=pl.BlockSpec((K, tn), lambda i, j, k: (0, j % 3)), o_spec=pl.BlockSpec((tm, tn), lambda i, j, k: (i, j)),
        out_shape=(L, d * N), out_dtype=BF16, scale=(lambda j: j % 3 == 0, q_scale), name=name)


def _dh_view(dp_view, w, d, add, name):
    L, N = dp_view.shape[0], dp_view.shape[1] // d
    K = w.shape[0]
    tm = min(1024, L)
    return _matmul_blocks(
        dp_view, w, mode="nt", grid=(L // tm, d, 1), a_spec=pl.BlockSpec((tm, N), lambda i, j, k: (i, j)),
        b_spec=pl.BlockSpec((K, N), lambda i, j, k: (0, 0)), o_spec=pl.BlockSpec((tm, K), lambda i, j, k: (i, j)),
        out_shape=(L, d * K), add=add, name=name)


def _dw_view(h_view, dp_view, d, name):
    L, K = h_view.shape[0], h_view.shape[1] // d
    N = dp_view.shape[1] // d
    tk = min(1024, L)
    nl = L // tk
    return _matmul_blocks(
        h_view, dp_view, mode="tn", grid=(1, 1, d * nl), a_spec=pl.BlockSpec((tk, K), lambda i, j, k: (k % nl, k // nl)),
        b_spec=pl.BlockSpec((tk, N), lambda i, j, k: (k % nl, k // nl)), o_spec=pl.BlockSpec((K, N), lambda i, j, k: (0, 0)),
        out_shape=(K, N), name=name)


def _rms_fwd_val(x, g):
    r = lax.rsqrt(jnp.mean(x * x, axis=-1, keepdims=True) + RMS_EPS)
    return (x * r) * g


def _rms_bwd_val(dy, x, g):
    r = lax.rsqrt(jnp.mean(x * x, axis=-1, keepdims=True) + RMS_EPS)
    xn = x * r
    gdy = g * dy
    dx = r * (gdy - xn * jnp.mean(gdy * xn, axis=-1, keepdims=True))
    return dx, dy * xn


def _rope_tables(pos, invf, inverse):
    ang = pos * invf
    cos, sin = jnp.cos(ang), jnp.sin(ang)
    lane = lax.broadcasted_iota(jnp.int32, ang.shape, 1)
    first = (lane >= MLA_NOPE) & (lane < MLA_NOPE + MLA_ROPE // 2)
    second = (lane >= MLA_NOPE + MLA_ROPE // 2) & (lane < MLA_QK)
    sgn = -1.0 if inverse else 1.0
    sa = jnp.where(first, -sgn * sin, 0.0)
    sb = jnp.where(second, sgn * sin, 0.0)
    return cos, sa, sb


def _rope_val(x, cos, sa, sb):
    half = MLA_ROPE // 2
    return x * cos + pltpu.roll(x, LANES - half, 1) * sa + pltpu.roll(x, half, 1) * sb


def _head_sum_bcast(v, n_heads):
    parts = []
    for h in range(n_heads):
        s = jnp.sum(v[:, h * LANES:(h + 1) * LANES], axis=1, keepdims=True)
        parts.append(jnp.broadcast_to(s, (v.shape[0], LANES)))
    return parts


def _row_spec(t, w):
    return pl.BlockSpec((t, w), lambda i: (i, 0))


def _acc_spec(w, rows=1):
    return pl.BlockSpec((rows, w), lambda i: (0, 0))


def _rmsnorm(x, g, out_dtype, name):
    S, W = x.shape
    t = min(ROW_TILE, S)

    def body(x_ref, g_ref, o_ref):
        o_ref[...] = _rms_fwd_val(x_ref[...], g_ref[...]).astype(out_dtype)

    return pl.pallas_call(body, name=name, grid=(S // t,), in_specs=[_row_spec(t, W), _acc_spec(W)],
                          out_specs=_row_spec(t, W), out_shape=jax.ShapeDtypeStruct((S, W), out_dtype),
                          compiler_params=_params("parallel"))(x, g)


def _mla_prep(proj_a, pos, invf, q_norm, kv_norm):
    S = proj_a.shape[0]
    t = ROW_TILE

    def body(a_ref, pos_ref, invf_ref, qn_ref, kvn_ref, cq_ref, ckv_ref, kpe_ref):
        a = a_ref[...]
        cq_ref[...] = _rms_fwd_val(a[:, 0:Q_RANK], qn_ref[...]).astype(BF16)
        ckv_ref[...] = _rms_fwd_val(a[:, Q_RANK + LANES:], kvn_ref[...]).astype(BF16)
        cos, sa, sb = _rope_tables(pos_ref[...], invf_ref[...], False)
        kpe_ref[...] = _rope_val(a[:, Q_RANK:Q_RANK + LANES], cos, sa, sb)

    return pl.pallas_call(
        body, name="mla_prep", grid=(S // t,),
        in_specs=[_row_spec(t, 768), _row_spec(t, 1), _acc_spec(LANES), _acc_spec(Q_RANK), _acc_spec(KV_RANK)],
        out_specs=[_row_spec(t, Q_RANK), _row_spec(t, KV_RANK), _row_spec(t, LANES)],
        out_shape=[jax.ShapeDtypeStruct((S, Q_RANK), BF16), jax.ShapeDtypeStruct((S, KV_RANK), BF16),
                   jax.ShapeDtypeStruct((S, LANES), F32)],
        compiler_params=_params("parallel"))(proj_a, pos, invf, q_norm, kv_norm)


def _qk_final(q_pre, k_pre, kpe, pos, invf):
    S, W = q_pre.shape
    t = ROW_TILE
    scale = MLA_QK ** -0.5

    def body(q_ref, k_ref, kpe_ref, pos_ref, invf_ref, qo_ref, ko_ref):
        cos, sa, sb = _rope_tables(pos_ref[...], invf_ref[...], False)
        kpe_v = kpe_ref[...]
        for h in range(MLA_HEADS):
            sl = slice(h * LANES, (h + 1) * LANES)
            qo_ref[:, sl] = (_rope_val(q_ref[:, sl], cos, sa, sb) * scale).astype(BF16)
            ko_ref[:, sl] = (k_ref[:, sl] + kpe_v).astype(BF16)

    return pl.pallas_call(
        body, name="qk_final", grid=(S // t,),
        in_specs=[_row_spec(t, W), _row_spec(t, W), _row_spec(t, LANES), _row_spec(t, 1), _acc_spec(LANES)],
        out_specs=[_row_spec(t, W), _row_spec(t, W)],
        out_shape=[jax.ShapeDtypeStruct((S, W), BF16)] * 2,
        compiler_params=_params("parallel"))(q_pre, k_pre, kpe, pos, invf)


def _mla_bwd_prep(dq, dk, pos, invf):
    S, W = dq.shape
    t = ROW_TILE
    scale = MLA_QK ** -0.5

    def body(dq_ref, dk_ref, pos_ref, invf_ref, dqp_ref, dkpe_ref):
        cos, sa, sb = _rope_tables(pos_ref[...], invf_ref[...], True)
        tot = jnp.zeros((t, LANES), F32)
        for h in range(MLA_HEADS):
            sl = slice(h * LANES, (h + 1) * LANES)
            dqp_ref[:, sl] = _rope_val(dq_ref[:, sl] * scale, cos, sa, sb).astype(BF16)
            tot = tot + dk_ref[:, sl]
        lane = lax.broadcasted_iota(jnp.int32, tot.shape, 1)
        tot = jnp.where((lane >= MLA_NOPE) & (lane < MLA_QK), tot, 0.0)
        dkpe_ref[...] = _rope_val(tot, cos, sa, sb)

    return pl.pallas_call(
        body, name="mla_bwd_prep", grid=(S // t,),
        in_specs=[_row_spec(t, W), _row_spec(t, W), _row_spec(t, 1), _acc_spec(LANES)],
        out_specs=[_row_spec(t, W), _row_spec(t, LANES)],
        out_shape=[jax.ShapeDtypeStruct((S, W), BF16), jax.ShapeDtypeStruct((S, LANES), F32)],
        compiler_params=_params("parallel"))(dq, dk, pos, invf)


def _mla_norm_bwd(dcq, dckv_a, dckv_b, dkpe, proj_a, q_norm, kv_norm):
    S = proj_a.shape[0]
    t = ROW_TILE

    def body(dcq_ref, da_ref, db_ref, dkpe_ref, a_ref, qn_ref, kvn_ref, o_ref, dqn_ref, dkvn_ref):
        i = pl.program_id(0)
        a = a_ref[...]
        dxq, gq = _rms_bwd_val(dcq_ref[...], a[:, 0:Q_RANK], qn_ref[...])
        dxkv, gkv = _rms_bwd_val(da_ref[...] + db_ref[...], a[:, Q_RANK + LANES:], kvn_ref[...])
        o_ref[:, 0:Q_RANK] = dxq.astype(BF16)
        o_ref[:, Q_RANK:Q_RANK + LANES] = dkpe_ref[...].astype(BF16)
        o_ref[:, Q_RANK + LANES:] = dxkv.astype(BF16)

        @pl.when(i == 0)
        def _():
            dqn_ref[...] = jnp.zeros_like(dqn_ref)
            dkvn_ref[...] = jnp.zeros_like(dkvn_ref)

        dqn_ref[...] += jnp.sum(gq, axis=0, keepdims=True)
        dkvn_ref[...] += jnp.sum(gkv, axis=0, keepdims=True)

    return pl.pallas_call(
        body, name="mla_norm_bwd", grid=(S // t,),
        in_specs=[_row_spec(t, Q_RANK), _row_spec(t, KV_RANK), _row_spec(t, KV_RANK), _row_spec(t, LANES),
                  _row_spec(t, 768), _acc_spec(Q_RANK), _acc_spec(KV_RANK)],
        out_specs=[_row_spec(t, 768), _acc_spec(Q_RANK), _acc_spec(KV_RANK)],
        out_shape=[jax.ShapeDtypeStruct((S, 768), BF16), jax.ShapeDtypeStruct((1, Q_RANK), F32),
                   jax.ShapeDtypeStruct((1, KV_RANK), F32)],
        compiler_params=_params("arbitrary"))(dcq, dckv_a, dckv_b, dkpe, proj_a, q_norm, kv_norm)


DILATIONS = tuple(d for _, d in DIL_PAIRS)


def _to_tokens(view, scr, d):
    if d == 1:
        return view
    n, w = view.shape[0], view.shape[1] // d
    for r in range(d):
        for c in range(w // LANES):
            scr[pl.ds(c, 1), pl.ds(r, n, stride=d), :] = view[:, r * w + c * LANES:r * w + (c + 1) * LANES][None]
    return jnp.concatenate([scr[c] for c in range(w // LANES)], axis=1)


def _from_tokens(tok, scr, d, out_ref):
    if d == 1:
        out_ref[...] = tok.astype(out_ref.dtype)
        return
    n, w = tok.shape[0] // d, tok.shape[1]
    for c in range(w // LANES):
        scr[c] = tok[:, c * LANES:(c + 1) * LANES]
    for r in range(d):
        for c in range(w // LANES):
            out_ref[:, r * w + c * LANES:r * w + (c + 1) * LANES] = scr[pl.ds(c, 1), pl.ds(r, n, stride=d), :][0].astype(out_ref.dtype)


def _token_scratch(t, w):
    return pltpu.VMEM((w // LANES, t, LANES), F32)


def _view_spec(t, d):
    return pl.BlockSpec((t // d, d * DIL_W), lambda i: (i, 0))


def _mix_weights(ls):
    m = jnp.maximum(jnp.maximum(ls[0], ls[1]), ls[2])
    es = [jnp.exp(l - m) for l in ls]
    den = es[0] + es[1] + es[2]
    return [e / den for e in es]


def _dil_mix(o_list, lse_list):
    S = o_list[0].shape[0] * DILATIONS[0]
    t = ROW_TILE
    specs = [_view_spec(t, d) for d in DILATIONS]

    def body(o0, o1, o2, l0, l1, l2, y_ref, *scr):
        os_ = [_to_tokens(r[...], scr[g], d) for g, (r, d) in enumerate(zip((o0, o1, o2), DILATIONS))]
        ws = _mix_weights([_to_tokens(r[...], scr[3 + g], d) for g, (r, d) in enumerate(zip((l0, l1, l2), DILATIONS))])
        y_ref[...] = (ws[0] * os_[0] + ws[1] * os_[1] + ws[2] * os_[2]).astype(BF16)

    return pl.pallas_call(
        body, name="dil_mix", grid=(S // t,), in_specs=specs * 2, out_specs=_row_spec(t, DIL_W),
        out_shape=jax.ShapeDtypeStruct((S, DIL_W), BF16), scratch_shapes=[_token_scratch(t, DIL_W)] * 6,
        compiler_params=_params("parallel"))(*o_list, *lse_list)


def _dil_mix_bwd(dy, o_list, lse_list):
    S = dy.shape[0]
    t = ROW_TILE
    specs = [_view_spec(t, d) for d in DILATIONS]

    def body(dy_ref, o0, o1, o2, l0, l1, l2, d0, d1, d2, e0, e1, e2, *scr):
        os_ = [_to_tokens(r[...], scr[g], d) for g, (r, d) in enumerate(zip((o0, o1, o2), DILATIONS))]
        ws = _mix_weights([_to_tokens(r[...], scr[3 + g], d) for g, (r, d) in enumerate(zip((l0, l1, l2), DILATIONS))])
        dyv = dy_ref[...]
        y = ws[0] * os_[0] + ws[1] * os_[1] + ws[2] * os_[2]
        b = jnp.concatenate(_head_sum_bcast(dyv * y, DIL_HPG), axis=1)
        for g, (w, d_ref, e_ref, d) in enumerate(zip(ws, (d0, d1, d2), (e0, e1, e2), DILATIONS)):
            _from_tokens(w * dyv, scr[6 + g], d, d_ref)
            _from_tokens(w * b, scr[9 + g], d, e_ref)

    shapes = [(S // d, d * DIL_W) for d in DILATIONS]
    return pl.pallas_call(
        body, name="dil_mix_bwd", grid=(S // t,), in_specs=[_row_spec(t, DIL_W)] + specs * 2, out_specs=specs * 2,
        out_shape=[jax.ShapeDtypeStruct(s, BF16) for s in shapes] + [jax.ShapeDtypeStruct(s, F32) for s in shapes],
        scratch_shapes=[_token_scratch(t, DIL_W)] * 12, compiler_params=_params("parallel"))(dy, *o_list, *lse_list)


def _delta(do, o, n_heads, name):
    S, W = do.shape
    t = ROW_TILE

    def body(do_ref, o_ref, d_ref):
        prod = do_ref[...].astype(F32) * o_ref[...].astype(F32)
        d_ref[...] = jnp.concatenate(_head_sum_bcast(prod, n_heads), axis=1)

    return pl.pallas_call(body, name=name, grid=(S // t,), in_specs=[_row_spec(t, W)] * 2, out_specs=_row_spec(t, W),
                          out_shape=jax.ShapeDtypeStruct((S, W), F32), compiler_params=_params("parallel"))(do, o)


def _merge(proj_g, b_gate, b_list):
    S = proj_g.shape[0]
    t = ROW_TILE

    def body(g_ref, b_ref, y0, y1, y2, o_ref):
        acc = jnp.zeros((t, D_MODEL), F32)
        for i, y in enumerate((y0, y1, y2)):
            sl = slice(i * D_MODEL, (i + 1) * D_MODEL)
            acc = acc + jax.nn.sigmoid(g_ref[:, sl] + b_ref[:, sl]) * y[...]
        o_ref[...] = acc.astype(BF16)

    return pl.pallas_call(
        body, name="merge", grid=(S // t,),
        in_specs=[_row_spec(t, 3 * D_MODEL), _acc_spec(3 * D_MODEL)] + [_row_spec(t, D_MODEL)] * 3,
        out_specs=_row_spec(t, D_MODEL), out_shape=jax.ShapeDtypeStruct((S, D_MODEL), BF16),
        compiler_params=_params("parallel"))(proj_g, b_gate, *b_list)


def _merge_bwd(dmerged, proj_g, b_gate, b_list):
    S = proj_g.shape[0]
    t = ROW_TILE

    def body(dm_ref, g_ref, b_ref, y0, y1, y2, d0, d1, d2, dz_ref, db_ref):
        i = pl.program_id(0)

        @pl.when(i == 0)
        def _():
            db_ref[...] = jnp.zeros_like(db_ref)

        dm = dm_ref[...]
        for k, (y, d_ref) in enumerate(zip((y0, y1, y2), (d0, d1, d2))):
            sl = slice(k * D_MODEL, (k + 1) * D_MODEL)
            s = jax.nn.sigmoid(g_ref[:, sl] + b_ref[:, sl])
            d_ref[...] = (s * dm).astype(BF16)
            dz = dm * y[...] * (s * (1.0 - s))
            dz_ref[:, sl] = dz.astype(BF16)
            db_ref[:, sl] += jnp.sum(dz, axis=0, keepdims=True)

    return pl.pallas_call(
        body, name="merge_bwd", grid=(S // t,),
        in_specs=[_row_spec(t, D_MODEL), _row_spec(t, 3 * D_MODEL), _acc_spec(3 * D_MODEL)] + [_row_spec(t, D_MODEL)] * 3,
        out_specs=[_row_spec(t, D_MODEL)] * 3 + [_row_spec(t, 3 * D_MODEL), _acc_spec(3 * D_MODEL)],
        out_shape=[jax.ShapeDtypeStruct((S, D_MODEL), BF16)] * 3
        + [jax.ShapeDtypeStruct((S, 3 * D_MODEL), BF16), jax.ShapeDtypeStruct((1, 3 * D_MODEL), F32)],
        compiler_params=_params("arbitrary"))(dmerged, proj_g, b_gate, *b_list)


def _norm2(o, x, g_post, g_pre):
    S = x.shape[0]
    t = ROW_TILE

    def body(o_ref, x_ref, gp_ref, gf_ref, x1_ref, h2_ref):
        x1 = x_ref[...] + _rms_fwd_val(o_ref[...], gp_ref[...])
        x1_ref[...] = x1
        h2_ref[...] = _rms_fwd_val(x1, gf_ref[...]).astype(BF16)

    return pl.pallas_call(
        body, name="norm2", grid=(S // t,),
        in_specs=[_row_spec(t, D_MODEL)] * 2 + [_acc_spec(D_MODEL)] * 2, out_specs=[_row_spec(t, D_MODEL)] * 2,
        out_shape=[jax.ShapeDtypeStruct((S, D_MODEL), F32), jax.ShapeDtypeStruct((S, D_MODEL), BF16)],
        compiler_params=_params("parallel"))(o, x, g_post, g_pre)


def _norm2_bwd(dx2, dh2, x1, o, g_pre, g_post):
    S = x1.shape[0]
    t = ROW_TILE

    def body(dx2_ref, dh2_ref, x1_ref, o_ref, gf_ref, gp_ref, dx1_ref, do_ref, dgf_ref, dgp_ref):
        i = pl.program_id(0)

        @pl.when(i == 0)
        def _():
            dgf_ref[...] = jnp.zeros_like(dgf_ref)
            dgp_ref[...] = jnp.zeros_like(dgp_ref)

        d1, gf = _rms_bwd_val(dh2_ref[...], x1_ref[...], gf_ref[...])
        dx1 = dx2_ref[...] + d1
        dx1_ref[...] = dx1
        do, gp = _rms_bwd_val(dx1, o_ref[...], gp_ref[...])
        do_ref[...] = do.astype(BF16)
        dgf_ref[...] += jnp.sum(gf, axis=0, keepdims=True)
        dgp_ref[...] += jnp.sum(gp, axis=0, keepdims=True)

    return pl.pallas_call(
        body, name="norm2_bwd", grid=(S // t,),
        in_specs=[_row_spec(t, D_MODEL)] * 4 + [_acc_spec(D_MODEL)] * 2,
        out_specs=[_row_spec(t, D_MODEL)] * 2 + [_acc_spec(D_MODEL)] * 2,
        out_shape=[jax.ShapeDtypeStruct((S, D_MODEL), F32), jax.ShapeDtypeStruct((S, D_MODEL), BF16),
                   jax.ShapeDtypeStruct((1, D_MODEL), F32), jax.ShapeDtypeStruct((1, D_MODEL), F32)],
        compiler_params=_params("arbitrary"))(dx2, dh2, x1, o, g_pre, g_post)


def _norm1_bwd(dx1, dh_list, x, g):
    S = x.shape[0]
    t = ROW_TILE
    dils = (1,) + DILATIONS
    assert len(dh_list) == len(dils)

    def body(dx1_ref, *refs):
        dh_refs, (x_ref, g_ref, dx_ref, dg_ref), scr = refs[:len(dils)], refs[len(dils):len(dils) + 4], refs[len(dils) + 4:]
        i = pl.program_id(0)

        @pl.when(i == 0)
        def _():
            dg_ref[...] = jnp.zeros_like(dg_ref)

        dh = dh_refs[0][...]
        for k in range(1, len(dils)):
            dh = dh + _to_tokens(dh_refs[k][...], scr[k - 1], dils[k])
        d, gg = _rms_bwd_val(dh, x_ref[...], g_ref[...])
        dx_ref[...] = dx1_ref[...] + d
        dg_ref[...] += jnp.sum(gg, axis=0, keepdims=True)

    dh_specs = [pl.BlockSpec((t // d, d * D_MODEL), lambda i: (i, 0)) for d in dils]
    return pl.pallas_call(
        body, name="norm1_bwd", grid=(S // t,),
        in_specs=[_row_spec(t, D_MODEL)] + dh_specs + [_row_spec(t, D_MODEL), _acc_spec(D_MODEL)],
        out_specs=[_row_spec(t, D_MODEL), _acc_spec(D_MODEL)],
        out_shape=[jax.ShapeDtypeStruct((S, D_MODEL), F32), jax.ShapeDtypeStruct((1, D_MODEL), F32)],
        scratch_shapes=[_token_scratch(t, D_MODEL)] * (len(dils) - 1),
        compiler_params=_params("arbitrary"))(dx1, *dh_list, x, g)


def _gain_grad(dy, x, name):
    R, W = x.shape

    def body(dy_ref, x_ref, dg_ref):
        xv = x_ref[...]
        r = lax.rsqrt(jnp.mean(xv * xv, axis=-1, keepdims=True) + RMS_EPS)
        dg_ref[...] = jnp.sum(dy_ref[...] * (xv * r), axis=0, keepdims=True)

    return pl.pallas_call(body, name=name, grid=(1,), in_specs=[_full((R, W))] * 2, out_specs=_full((1, W)),
                          out_shape=jax.ShapeDtypeStruct((1, W), F32), compiler_params=_params("arbitrary"))(dy, x)


def _loss_head(f, x1, tgt, g):
    S = f.shape[0]
    t = ROW_TILE

    def body(f_ref, x1_ref, t_ref, g_ref, loss_ref, dx2_ref, df_ref, dg_ref):
        i = pl.program_id(0)

        @pl.when(i == 0)
        def _():
            loss_ref[...] = jnp.zeros_like(loss_ref)
            dg_ref[...] = jnp.zeros_like(dg_ref)

        fv, gv = f_ref[...], g_ref[...]
        err = x1_ref[...] + _rms_fwd_val(fv, gv) - t_ref[...]
        part = jnp.sum(jnp.mean(err * err, axis=-1, keepdims=True), axis=0, keepdims=True)
        loss_ref[...] += jnp.broadcast_to(0.5 * part, loss_ref.shape)
        dx2 = err * (1.0 / D_MODEL)
        dx2_ref[...] = dx2
        df, gg = _rms_bwd_val(dx2, fv, gv)
        df_ref[...] = df.astype(BF16)
        dg_ref[...] += jnp.sum(gg, axis=0, keepdims=True)

    return pl.pallas_call(
        body, name="loss_head", grid=(S // t,),
        in_specs=[_row_spec(t, D_MODEL)] * 3 + [_acc_spec(D_MODEL)],
        out_specs=[_acc_spec(LANES, 8), _row_spec(t, D_MODEL), _row_spec(t, D_MODEL), _acc_spec(D_MODEL)],
        out_shape=[jax.ShapeDtypeStruct((8, LANES), F32), jax.ShapeDtypeStruct((S, D_MODEL), F32),
                   jax.ShapeDtypeStruct((S, D_MODEL), BF16), jax.ShapeDtypeStruct((1, D_MODEL), F32)],
        compiler_params=_params("arbitrary"))(f, x1, tgt, g)


CONV_TC = 1408
CONV_TT = 512
HALO = 8


def _shift_down(u, halo, first):
    row = lax.broadcasted_iota(jnp.int32, u.shape, 0)
    h6 = jnp.where(first, 0.0, halo[HALO - 2:HALO - 1, :])
    h7 = jnp.where(first, 0.0, halo[HALO - 1:HALO, :])
    s1 = jnp.where(row == 0, h7, pltpu.roll(u, 1, 0))
    s2 = jnp.where(row == 0, h6, jnp.where(row == 1, h7, pltpu.roll(u, 2, 0)))
    return s1, s2


def _conv_specs(tt, n_c, n_t, lead):
    def halo_row(i):
        return jnp.maximum(i * (tt // HALO) - 1, 0) if lead else jnp.minimum((i + 1) * (tt // HALO), n_t * (tt // HALO) - 1)
    return [
        pl.BlockSpec((tt, CONV_TC), lambda j, i: (i, j)),
        pl.BlockSpec((tt, CONV_TC), lambda j, i: (i, j + n_c)),
        pl.BlockSpec((HALO, CONV_TC), lambda j, i: (halo_row(i), j)),
        pl.BlockSpec((HALO, CONV_TC), lambda j, i: (halo_row(i), j + n_c)),
    ]


def _conv_z(ug, uv, hg, hv, w_g, w_v, b_g, b_v, first):
    g1, g2 = _shift_down(ug, hg, first)
    v1, v2 = _shift_down(uv, hv, first)
    zg = b_g + w_g[0:1, :] * g2
    zg = zg + w_g[1:2, :] * g1
    zg = zg + w_g[2:3, :] * ug
    zv = b_v + w_v[0:1, :] * v2
    zv = zv + w_v[1:2, :] * v1
    zv = zv + w_v[2:3, :] * uv
    return zg, zv, (g2, g1, ug), (v2, v1, uv)


def _conv_fwd(u, conv_w, conv_b):
    S = u.shape[0]
    tt = min(CONV_TT, S)
    n_c, n_t = D_FF // CONV_TC, S // tt
    wspec = [pl.BlockSpec((3, CONV_TC), lambda j, i: (0, j)), pl.BlockSpec((3, CONV_TC), lambda j, i: (0, j + n_c)),
             pl.BlockSpec((1, CONV_TC), lambda j, i: (0, j)), pl.BlockSpec((1, CONV_TC), lambda j, i: (0, j + n_c))]

    def body(ug_ref, uv_ref, hg_ref, hv_ref, wg_ref, wv_ref, bg_ref, bv_ref, a_ref):
        first = pl.program_id(1) == 0
        zg, zv, _, _ = _conv_z(ug_ref[...], uv_ref[...], hg_ref, hv_ref, wg_ref, wv_ref, bg_ref[...], bv_ref[...], first)
        a_ref[...] = (zg * jax.nn.sigmoid(zg) * zv).astype(BF16)

    return pl.pallas_call(
        body, name="conv_fwd", grid=(n_c, n_t), in_specs=_conv_specs(tt, n_c, n_t, True) + wspec,
        out_specs=pl.BlockSpec((tt, CONV_TC), lambda j, i: (i, j)), out_shape=jax.ShapeDtypeStruct((S, D_FF), BF16),
        compiler_params=_params("parallel", "parallel"))(u, u, u, u, conv_w, conv_w, conv_b, conv_b)


def _gate_bwd(da, zg, zv):
    sg = jax.nn.sigmoid(zg)
    return da * zv * (sg * (1.0 + zg * (1.0 - sg))), da * (zg * sg)


def _conv_bwd(da, u, conv_w, conv_b):
    S = u.shape[0]
    tt = min(CONV_TT, S)
    n_c, n_t = D_FF // CONV_TC, S // tt
    wspec = [pl.BlockSpec((3, CONV_TC), lambda j, i: (0, j)), pl.BlockSpec((3, CONV_TC), lambda j, i: (0, j + n_c)),
             pl.BlockSpec((1, CONV_TC), lambda j, i: (0, j)), pl.BlockSpec((1, CONV_TC), lambda j, i: (0, j + n_c))]
    tile = pl.BlockSpec((tt, CONV_TC), lambda j, i: (i, j))
    trail = pl.BlockSpec((HALO, CONV_TC), lambda j, i: (jnp.minimum((i + 1) * (tt // HALO), n_t * (tt // HALO) - 1), j))

    def body(da_ref, ug_ref, uv_ref, hg_ref, hv_ref, tda_ref, tg_ref, tv_ref, wg_ref, wv_ref, bg_ref, bv_ref,
             dug_ref, duv_ref, dwg_ref, dwv_ref, dbg_ref, dbv_ref):
        i = pl.program_id(1)
        last = i == n_t - 1
        bg, bv = bg_ref[...], bv_ref[...]
        zg, zv, gs, vs = _conv_z(ug_ref[...], uv_ref[...], hg_ref, hv_ref, wg_ref, wv_ref, bg, bv, i == 0)
        dzg, dzv = _gate_bwd(da_ref[...], zg, zv)
        tzg, tzv, _, _ = _conv_z(tg_ref[...], tv_ref[...], ug_ref.at[pl.ds(tt - HALO, HALO), :], uv_ref.at[pl.ds(tt - HALO, HALO), :],
                                 wg_ref, wv_ref, bg, bv, False)
        tdzg, tdzv = _gate_bwd(tda_ref[...], tzg, tzv)
        row = lax.broadcasted_iota(jnp.int32, dzg.shape, 0)
        for dz, tdz, w_ref, du_ref in ((dzg, tdzg, wg_ref, dug_ref), (dzv, tdzv, wv_ref, duv_ref)):
            h0 = jnp.where(last, 0.0, tdz[0:1, :])
            h1 = jnp.where(last, 0.0, tdz[1:2, :])
            u1 = jnp.where(row == tt - 1, h0, pltpu.roll(dz, tt - 1, 0))
            u2 = jnp.where(row == tt - 1, h1, jnp.where(row == tt - 2, h0, pltpu.roll(dz, tt - 2, 0)))
            du_ref[...] = (w_ref[2:3, :] * dz + w_ref[1:2, :] * u1 + w_ref[0:1, :] * u2).astype(BF16)

        @pl.when(i == 0)
        def _():
            for r in (dwg_ref, dwv_ref, dbg_ref, dbv_ref):
                r[...] = jnp.zeros_like(r)

        for k in range(3):
            dwg_ref[k:k + 1, :] += jnp.sum(dzg * gs[k], axis=0, keepdims=True)
            dwv_ref[k:k + 1, :] += jnp.sum(dzv * vs[k], axis=0, keepdims=True)
        dbg_ref[...] += jnp.sum(dzg, axis=0, keepdims=True)
        dbv_ref[...] += jnp.sum(dzv, axis=0, keepdims=True)

    lead = _conv_specs(tt, n_c, n_t, True)
    trail_v = pl.BlockSpec((HALO, CONV_TC), lambda j, i: (jnp.minimum((i + 1) * (tt // HALO), n_t * (tt // HALO) - 1), j + n_c))
    outs = pl.pallas_call(
        body, name="conv_bwd", grid=(n_c, n_t), in_specs=[tile] + lead + [trail, trail, trail_v] + wspec,
        out_specs=[tile, tile] + [pl.BlockSpec((3, CONV_TC), lambda j, i: (0, j))] * 2 + [pl.BlockSpec((1, CONV_TC), lambda j, i: (0, j))] * 2,
        out_shape=[jax.ShapeDtypeStruct((S, D_FF), BF16)] * 2 + [jax.ShapeDtypeStruct((3, D_FF), F32)] * 2
        + [jax.ShapeDtypeStruct((1, D_FF), F32)] * 2,
        compiler_params=_params("parallel", "arbitrary"))(da, u, u, u, u, da, u, u, conv_w, conv_w, conv_b, conv_b)
    dug, duv, dwg, dwv, dbg, dbv = outs
    return dug, duv, jnp.concatenate([dwg, dwv], axis=1), jnp.concatenate([dbg, dbv], axis=1)


BAND = 128


MEM_TQ = 2048


def _mem_fwd(q, kv, *, name):
    S, W = q.shape
    M = kv.shape[0]
    nh = W // LANES
    tq = min(MEM_TQ, S)

    def body(q_ref, k_ref, v_ref, o_ref, l_ref):
        s = lax.dot_general(q_ref[...], k_ref[...].astype(BF16), NT, preferred_element_type=F32)
        m = jnp.max(s, axis=1, keepdims=True)
        p = jnp.exp(s - m)
        l = jnp.sum(p, axis=1, keepdims=True)
        o_ref[...] = (lax.dot_general(p.astype(BF16), v_ref[...].astype(BF16), NN, preferred_element_type=F32) / l).astype(BF16)
        l_ref[...] = jnp.broadcast_to(m + jnp.log(l), (tq, LANES))

    blk = pl.BlockSpec((tq, LANES), lambda hh, i: (i, hh))
    return pl.pallas_call(
        body, name=name, grid=(nh, S // tq),
        in_specs=[blk, pl.BlockSpec((M, LANES), lambda hh, i: (0, hh)), pl.BlockSpec((M, LANES), lambda hh, i: (0, hh + nh))],
        out_specs=[blk, blk], out_shape=[jax.ShapeDtypeStruct((S, W), BF16), jax.ShapeDtypeStruct((S, W), F32)],
        compiler_params=_params("parallel", "parallel"))(q, kv, kv)


def _mem_bwd(q, kv, do, lse, delta, *, scale, name):
    S, W = q.shape
    M = kv.shape[0]
    nh = W // LANES
    tq = min(MEM_TQ, S)

    def body(q_ref, k_ref, v_ref, do_ref, l_ref, d_ref, dq_ref, dk_ref, dv_ref):
        i = pl.program_id(1)

        @pl.when(i == 0)
        def _():
            dk_ref[...] = jnp.zeros_like(dk_ref)
            dv_ref[...] = jnp.zeros_like(dv_ref)

        qv = q_ref[...]
        kv_, vv = k_ref[...].astype(BF16), v_ref[...].astype(BF16)
        dov = do_ref[...].astype(BF16)
        s = lax.dot_general(qv, kv_, NT, preferred_element_type=F32)
        p = jnp.exp(s - l_ref[...][:, 0:1])
        dp = lax.dot_general(dov, vv, NT, preferred_element_type=F32)
        ds = (p * (dp - d_ref[...][:, 0:1])).astype(BF16)
        dq_ref[...] = lax.dot_general(ds, kv_, NN, preferred_element_type=F32) * scale
        dk_ref[...] += lax.dot_general(ds, qv, TN, preferred_element_type=F32)
        dv_ref[...] += lax.dot_general(p.astype(BF16), dov, TN, preferred_element_type=F32)

    blk = pl.BlockSpec((tq, LANES), lambda hh, i: (i, hh))
    kblk = pl.BlockSpec((M, LANES), lambda hh, i: (0, hh))
    vblk = pl.BlockSpec((M, LANES), lambda hh, i: (0, hh + nh))
    dq, dk, dv = pl.pallas_call(
        body, name=name, grid=(nh, S // tq), in_specs=[blk, kblk, vblk, blk, blk, blk], out_specs=[blk, kblk, kblk],
        out_shape=[jax.ShapeDtypeStruct((S, W), F32), jax.ShapeDtypeStruct((M, W), F32), jax.ShapeDtypeStruct((M, W), F32)],
        compiler_params=_params("parallel", "arbitrary"))(q, kv, kv, do, lse, delta)
    return dq, jnp.concatenate([dk, dv], axis=1)


CAUSAL_BLOCK = 512
STRIP = 32


def _causal_fwd(q, k, v, *, name):
    S, W = q.shape
    T = min(CAUSAL_BLOCK, S // 2)
    n_strips = T // STRIP

    def body(q_ref, k_ref, v_ref, o_ref, lse_ref, s0, s1, p0, p1, a0, a1, acc_scr):
        i = pl.program_id(1)
        s_scr, p_scr, a_scr = (s0, s1), (p0, p1), (a0, a1)

        def rows(j):
            return pl.ds(pl.multiple_of(j * T, T), T)

        def scores(j, slot):
            s_scr[slot][...] = lax.dot_general(q_ref[...], k_ref[rows(j), :], NT, preferred_element_type=F32)

        def softmax(slot, stats, diag):
            def strip(r):
                s = s_scr[slot][r * STRIP:(r + 1) * STRIP, :]
                if diag:
                    row = r * STRIP + lax.broadcasted_iota(jnp.int32, s.shape, 0)
                    s = jnp.where(row >= lax.broadcasted_iota(jnp.int32, s.shape, 1), s, NEG_INF)
                return s

            m_new = [jnp.maximum(m_old, jnp.max(strip(r), axis=1, keepdims=True)) for r, (m_old, _) in enumerate(stats)]
            new = []
            for r, (m_old, l_old) in enumerate(stats):
                rs = slice(r * STRIP, (r + 1) * STRIP)
                p = jnp.exp(strip(r) - m_new[r])
                alpha = jnp.exp(m_old - m_new[r])
                new.append((m_new[r], alpha * l_old + jnp.sum(p, axis=1, keepdims=True)))
                a_scr[slot][rs, :] = alpha
                p_scr[slot][rs, :] = p.astype(BF16)
            return tuple(new)

        def values(j, slot):
            acc_scr[...] = a_scr[slot][...] * acc_scr[...] + lax.dot_general(p_scr[slot][...], v_ref[rows(j), :], NN,
                                                                            preferred_element_type=F32)

        def trip(j, stats, mine, other):
            scores(j + 1, other)
            stats = softmax(mine, stats, False)
            values(jnp.maximum(j - 1, 0), other)
            return stats

        def pair(jj, stats):
            return trip(2 * jj + 1, trip(2 * jj, stats, 0, 1), 1, 0)

        def last(stats, mine, other):
            values(jnp.maximum(i - 1, 0), other)
            stats = softmax(mine, stats, True)
            values(i, mine)
            for r, (m, l) in enumerate(stats):
                rs = slice(r * STRIP, (r + 1) * STRIP)
                o_ref[rs, :] = (acc_scr[rs, :] / l).astype(BF16)
                lse_ref[rs, :] = jnp.broadcast_to(m + jnp.log(l), (STRIP, LANES))

        acc_scr[...] = jnp.zeros_like(acc_scr)
        p1[...] = jnp.zeros_like(p1)
        a1[...] = jnp.ones_like(a1)
        scores(0, 0)
        init = tuple((jnp.full((STRIP, 1), NEG_INF, F32), jnp.zeros((STRIP, 1), F32)) for _ in range(n_strips))
        stats = lax.fori_loop(0, i // 2, pair, init)

        @pl.when(i % 2 == 1)
        def _():
            last(trip(i - 1, stats, 0, 1), 1, 0)

        @pl.when(i % 2 == 0)
        def _():
            last(stats, 0, 1)

    blk = pl.BlockSpec((T, LANES), lambda hh, i: (i, hh))
    whole = pl.BlockSpec((S, LANES), lambda hh, i: (0, hh))
    return pl.pallas_call(
        body, name=name, grid=(W // LANES, S // T), in_specs=[blk, whole, whole], out_specs=[blk, blk],
        out_shape=[jax.ShapeDtypeStruct((S, W), BF16), jax.ShapeDtypeStruct((S, W), F32)],
        scratch_shapes=[pltpu.VMEM((T, T), F32)] * 2 + [pltpu.VMEM((T, T), BF16)] * 2 + [pltpu.VMEM((T, 1), F32)] * 2
        + [pltpu.VMEM((T, LANES), F32)],
        compiler_params=_params("parallel", "arbitrary"))(q, k, v)


def _causal_bwd(q, k, v, do, lse, delta, *, name):
    S, W = q.shape
    T = min(CAUSAL_BLOCK, S // 2)
    nq = S // T
    n_strips, n_col = T // STRIP, T // LANES

    def body(q_ref, k_ref, v_ref, do_ref, l_ref, d_ref, dq_ref, dk_ref, dv_ref, s0, s1, e0, e1, p0, p1, g0, g1):
        j = pl.program_id(1)
        s_scr, e_scr, p_scr, g_scr = (s0, s1), (e0, e1), (p0, p1), (g0, g1)

        @pl.when(j == 0)
        def _():
            dq_ref[...] = jnp.zeros_like(dq_ref)

        dk_ref[...] = jnp.zeros_like(dk_ref)
        dv_ref[...] = jnp.zeros_like(dv_ref)

        def rows(i):
            return pl.ds(pl.multiple_of(jnp.minimum(i, nq - 1) * T, T), T)

        def products(i, slot):
            r = rows(i)
            s_scr[slot][...] = lax.dot_general(q_ref[r, :], k_ref[...], NT, preferred_element_type=F32)
            e_scr[slot][...] = lax.dot_general(do_ref[r, :].astype(BF16), v_ref[...], NT, preferred_element_type=F32)

        def pointwise(i, slot, diag):
            base = pl.multiple_of(i * T, T)
            for r in range(n_strips):
                rs = slice(r * STRIP, (r + 1) * STRIP)
                lse_r = l_ref[pl.ds(base + r * STRIP, STRIP), :]
                del_r = d_ref[pl.ds(base + r * STRIP, STRIP), :]
                for c in range(n_col):
                    cs = slice(c * LANES, (c + 1) * LANES)
                    if diag and c * LANES > (r + 1) * STRIP - 1:
                        p_scr[slot][rs, cs] = jnp.zeros((STRIP, LANES), BF16)
                        g_scr[slot][rs, cs] = jnp.zeros((STRIP, LANES), BF16)
                        continue
                    sv = s_scr[slot][rs, cs]
                    if diag and (c + 1) * LANES - 1 > r * STRIP:
                        row = r * STRIP + lax.broadcasted_iota(jnp.int32, sv.shape, 0)
                        col = c * LANES + lax.broadcasted_iota(jnp.int32, sv.shape, 1)
                        sv = jnp.where(row >= col, sv, NEG_INF)
                    p = jnp.exp(sv - lse_r)
                    p_scr[slot][rs, cs] = p.astype(BF16)
                    g_scr[slot][rs, cs] = (p * (e_scr[slot][rs, cs] - del_r)).astype(BF16)

        def gradients(i, slot):
            r = rows(i)
            qi, doi = q_ref[r, :], do_ref[r, :].astype(BF16)
            g = g_scr[slot][...]
            dv_ref[...] += lax.dot_general(p_scr[slot][...], doi, TN, preferred_element_type=F32)
            dk_ref[...] += lax.dot_general(g, qi, TN, preferred_element_type=F32)
            dq_ref[r, :] += lax.dot_general(g, k_ref[...], NN, preferred_element_type=F32)

        products(j, 0)
        products(j + 1, 1)
        pointwise(j, 0, True)

        def trip(i, mine, other):
            products(i + 1, other)
            pointwise(i, mine, False)
            gradients(i - 1, other)

        def pair(t, _):
            trip(j + 1 + 2 * t, 1, 0)
            trip(j + 2 + 2 * t, 0, 1)
            return 0

        n_rest = nq - 1 - j
        lax.fori_loop(0, n_rest // 2, pair, 0)

        @pl.when(n_rest % 2 == 1)
        def _():
            trip(nq - 1, 1, 0)
            gradients(nq - 1, 1)

        @pl.when(n_rest % 2 == 0)
        def _():
            gradients(nq - 1, 0)

    blk = pl.BlockSpec((T, LANES), lambda hh, j: (j, hh))
    whole = pl.BlockSpec((S, LANES), lambda hh, j: (0, hh))
    return pl.pallas_call(
        body, name=name, grid=(W // LANES, S // T), in_specs=[whole, blk, blk, whole, whole, whole],
        out_specs=[whole, blk, blk], out_shape=[jax.ShapeDtypeStruct((S, W), F32)] * 3,
        scratch_shapes=[pltpu.VMEM((T, T), F32)] * 4 + [pltpu.VMEM((T, T), BF16)] * 4,
        compiler_params=_params("parallel", "arbitrary"))(q, k, v, do, lse, delta)


BAND_TQ = 512


def _band_window(i, sub, nsub, L, q_ref, k_ref, v_ref, slope):
    kw = min(2 * BAND, L)
    n = i * nsub + sub
    k0 = 0 if kw == L else pl.multiple_of(jnp.maximum(n - 1, 0) * BAND, BAND)
    win = pl.ds(k0, kw)
    qs = q_ref[sub * BAND:(sub + 1) * BAND, :]
    kwv, vwv = k_ref[win, :], v_ref[win, :]
    s = lax.dot_general(qs, kwv, NT, preferred_element_type=F32)
    dist = (n * BAND + lax.broadcasted_iota(jnp.int32, s.shape, 0)) - (k0 + lax.broadcasted_iota(jnp.int32, s.shape, 1))
    s = jnp.where((dist >= 0) & (dist <= BAND), s - slope * dist.astype(F32), NEG_INF)
    return win, qs, kwv, vwv, s


def _band_fwd(q, k, v, slopes, *, n_heads, qcol, kcol, vcol, L, slope_mul, o_shape, name):
    tq = min(BAND_TQ, L)
    nsub = tq // BAND

    def body(sl_ref, q_ref, k_ref, v_ref, o_ref, l_ref):
        hh, i = pl.program_id(0), pl.program_id(1)
        slope = sl_ref[hh % DIL_HPG] * slope_mul
        wins = [_band_window(i, sub, nsub, L, q_ref, k_ref, v_ref, slope) for sub in range(nsub)]
        ms = [jnp.max(w[4], axis=1, keepdims=True) for w in wins]
        ps = [jnp.exp(w[4] - m) for w, m in zip(wins, ms)]
        ls = [jnp.sum(p, axis=1, keepdims=True) for p in ps]
        for sub, (w, m, p, l) in enumerate(zip(wins, ms, ps, ls)):
            rows = slice(sub * BAND, (sub + 1) * BAND)
            o_ref[rows, :] = lax.dot_general(p.astype(BF16), w[3], NN, preferred_element_type=F32) / l
            l_ref[rows, :] = jnp.broadcast_to(m + jnp.log(l), (BAND, LANES))

    whole = lambda col: pl.BlockSpec((L, LANES), lambda hh, i: (0, col(hh)))
    o_spec = pl.BlockSpec((tq, LANES), lambda hh, i: (i, hh))
    return pl.pallas_call(
        body, name=name, grid=(n_heads, L // tq),
        in_specs=[pl.BlockSpec(memory_space=pltpu.SMEM), pl.BlockSpec((tq, LANES), lambda hh, i: (i, qcol(hh))), whole(kcol), whole(vcol)],
        out_specs=[o_spec, o_spec], out_shape=[jax.ShapeDtypeStruct(o_shape, F32)] * 2,
        compiler_params=_params("parallel", "arbitrary"))(slopes, q, k, v)


def _band_bwd(q, k, v, do, lse, delta, slopes, *, n_heads, qcol, kcol, vcol, L, scale, slope_mul, d_shape, name):
    tq = min(BAND_TQ, L)
    nsub = tq // BAND
    n_steps = L // tq

    def body(sl_ref, q_ref, k_ref, v_ref, do_ref, l_ref, d_ref, dq_ref, dk_ref, dv_ref, dk_acc, dv_acc):
        hh, i = pl.program_id(0), pl.program_id(1)
        slope = sl_ref[hh % DIL_HPG] * slope_mul

        @pl.when(i == 0)
        def _():
            dk_acc[...] = jnp.zeros_like(dk_acc)
            dv_acc[...] = jnp.zeros_like(dv_acc)

        blocks = range(nsub)
        rows = [slice(sub * BAND, (sub + 1) * BAND) for sub in blocks]
        wins = [_band_window(i, sub, nsub, L, q_ref, k_ref, v_ref, slope) for sub in blocks]
        dos = [do_ref[r, :] for r in rows]
        dps = [lax.dot_general(do, w[3], NT, preferred_element_type=F32) for do, w in zip(dos, wins)]
        ps = [jnp.exp(w[4] - l_ref[r, :][:, 0:1]) for w, r in zip(wins, rows)]
        dss = [(p * (dp - d_ref[r, :][:, 0:1])).astype(BF16) for p, dp, r in zip(ps, dps, rows)]
        for r, ds, w in zip(rows, dss, wins):
            dq_ref[r, :] = (lax.dot_general(ds, w[2], NN, preferred_element_type=F32) * scale).astype(BF16)
        dks = [lax.dot_general(ds, w[1], TN, preferred_element_type=F32) for ds, w in zip(dss, wins)]
        dvs = [lax.dot_general(p.astype(BF16), do, TN, preferred_element_type=F32) for p, do in zip(ps, dos)]
        for w, dk, dv in zip(wins, dks, dvs):
            dk_acc[w[0], :] += dk
            dv_acc[w[0], :] += dv

        @pl.when(i == n_steps - 1)
        def _():
            dk_ref[...] = dk_acc[...].astype(BF16)
            dv_ref[...] = dv_acc[...].astype(BF16)

    whole = lambda col: pl.BlockSpec((L, LANES), lambda hh, i: (0, col(hh)))
    blk = pl.BlockSpec((tq, LANES), lambda hh, i: (i, hh))
    ident = lambda hh: hh
    return pl.pallas_call(
        body, name=name, grid=(n_heads, n_steps),
        in_specs=[pl.BlockSpec(memory_space=pltpu.SMEM), pl.BlockSpec((tq, LANES), lambda hh, i: (i, qcol(hh))), whole(kcol), whole(vcol),
                  blk, blk, blk],
        out_specs=[blk, whole(ident), whole(ident)], out_shape=[jax.ShapeDtypeStruct(d_shape, BF16)] * 3,
        scratch_shapes=[pltpu.VMEM((L, LANES), F32)] * 2,
        compiler_params=_params("parallel", "arbitrary"))(slopes, q, k, v, do, lse, delta)


def _pad_heads(w, n_heads, width, axis):
    shp = w.shape
    new = shp[:axis] + (n_heads, width) + shp[axis + 1:]
    pad = [(0, 0)] * len(new)
    pad[axis + 1] = (0, LANES - width)
    out = jnp.pad(w.reshape(new), pad)
    return out.reshape(shp[:axis] + (n_heads * LANES,) + shp[axis + 1:])


def _unpad_heads(w, n_heads, width, axis):
    shp = w.shape
    new = shp[:axis] + (n_heads, LANES) + shp[axis + 1:]
    out = lax.slice_in_dim(w.reshape(new), 0, width, axis=axis + 1)
    return out.reshape(shp[:axis] + (n_heads * width,) + shp[axis + 1:])


def _alibi_slopes():
    s = jnp.exp2(-8.0 * jnp.arange(1, DIL_HEADS + 1, dtype=F32) / DIL_HEADS)
    return s.reshape(DIL_HPG, DIL_GROUPS).T


def _local_step(x, mem, positions, tgt, W):
    S = x.shape[0]
    pos = positions.reshape(S, 1).astype(F32)
    half = MLA_ROPE // 2
    inv_freq = ROPE_THETA ** (-jnp.arange(half, dtype=F32) / half)
    invf = jnp.zeros((1, LANES), F32).at[0, MLA_NOPE:MLA_NOPE + half].set(inv_freq).at[0, MLA_NOPE + half:MLA_QK].set(inv_freq)
    slopes = _alibi_slopes()

    w_in = W["w_in"]
    zc = lambda n: jnp.zeros((D_MODEL, n), BF16)
    w_a = jnp.concatenate([w_in[:, :OFF_Q], zc(MLA_NOPE), w_in[:, OFF_KV:OFF_KR], zc(LANES - MLA_QK), w_in[:, OFF_Q:OFF_KV]], axis=1)
    w_d, w_m, w_g = w_in[:, OFF_KR:OFF_DIL], w_in[:, OFF_DIL:OFF_MEMQ], w_in[:, OFF_MEMQ:]
    w_uq_p = _pad_heads(W["w_uq"], MLA_HEADS, MLA_QK, 1)
    ukv = W["w_ukv"].reshape(KV_RANK, MLA_HEADS, 2 * MLA_NOPE)
    w_uk_p = _pad_heads(ukv[:, :, :MLA_NOPE].reshape(KV_RANK, -1), MLA_HEADS, MLA_NOPE, 1)
    w_uv_p = _pad_heads(ukv[:, :, MLA_NOPE:].reshape(KV_RANK, -1), MLA_HEADS, MLA_NOPE, 1)
    w_br_mla_p = _pad_heads(W["w_br_mla"], MLA_HEADS, MLA_NOPE, 0)

    h = _rmsnorm(x, W["g_pre_mix"], BF16, "norm1")
    proj_a = _matmul(h, w_a, tn=768, name="proj_a")
    att_scale = LANES ** -0.5
    w_dg = [jnp.concatenate([w_d[:, (part * DIL_GROUPS + g) * DIL_W:(part * DIL_GROUPS + g + 1) * DIL_W] for part in range(3)], axis=1)
            for g in range(DIL_GROUPS)]
    proj_m = _matmul(h, w_m, tn=MEM_W, out_dtype=BF16, scale=(1, att_scale), name="proj_m")
    proj_g = _matmul(h, w_g, tn=1536, name="proj_g")

    cq_n, ckv_n, kpe = _mla_prep(proj_a, pos, invf, W["mla_q_norm"], W["mla_kv_norm"])
    q_pre = _matmul(cq_n, w_uq_p, tn=1024, name="mla_q")
    k_pre = _matmul(ckv_n, w_uk_p, tn=1024, name="mla_k")
    v_mla = _matmul(ckv_n, w_uv_p, tn=1024, out_dtype=BF16, name="mla_v")
    q_mla, k_mla = _qk_final(q_pre, k_pre, kpe, pos, invf)
    ident = lambda hh: hh
    o_mla, lse_mla = _causal_fwd(q_mla, k_mla, v_mla, name="attn_mla_fwd")

    col = lambda part: (lambda hh: (hh // DIL_HPG) * 3 * DIL_HPG + part * DIL_HPG + hh % DIL_HPG)
    h_views = [h.reshape(S // d, d * D_MODEL) for d in DILATIONS]
    proj_dg, o_dil, lse_dil = [], [], []
    for g, d in enumerate(DILATIONS):
        pv = _proj_view(h_views[g], w_dg[g], d, att_scale, f"proj_d{g}")
        o, lse = _band_fwd(pv, pv, pv, slopes[g], n_heads=d * DIL_HPG, qcol=col(0), kcol=col(1), vcol=col(2), L=S // d,
                           slope_mul=float(d), o_shape=(S // d, d * DIL_W), name=f"attn_dil{g}_fwd")
        proj_dg.append(pv)
        o_dil.append(o)
        lse_dil.append(lse)
    y_dil = _dil_mix(o_dil, lse_dil)

    mem_n = _rmsnorm(mem, W["g_mem"], BF16, "mem_norm")
    kv_mem = _matmul(mem_n, W["w_mem_kv"], name="mem_kv")
    o_mem, lse_mem = _mem_fwd(proj_m, kv_mem, name="attn_mem_fwd")

    b_mla = _matmul(o_mla, w_br_mla_p, name="br_mla")
    b_dil = _matmul(y_dil, W["w_br_dil"], name="br_dil")
    b_mem = _matmul(o_mem, W["w_br_mem"], name="br_mem")
    merged = _merge(proj_g, W["b_gate"], [b_mla, b_dil, b_mem])
    o_proj = _matmul(merged, W["w_o"], name="o_proj")
    x1, h2 = _norm2(o_proj, x, W["g_post_mix"], W["g_pre_ffn"])

    u = _matmul(h2, W["w_ffn_up"], tn=1408, name="ffn_up")
    act = _conv_fwd(u, W["conv_w"], W["conv_b"])
    f = _matmul(act, W["w_ffn_down"], name="ffn_down")
    loss8, dx2, df, dg_post_ffn = _loss_head(f, x1, tgt, W["g_post_ffn"])
    loss = loss8[0, 0]

    G = {"g_post_ffn": dg_post_ffn}
    d_act = _matmul(df, W["w_ffn_down"], mode="nt", tn=1408, name="d_act")
    G["w_ffn_down"] = _matmul(act, df, mode="tn", tm=1408, tk=1024, name="dw_ffn_down")
    du_g, du_v, G["conv_w"], G["conv_b"] = _conv_bwd(d_act, u, W["conv_w"], W["conv_b"])
    dh2 = _matmul(du_g, W["w_ffn_up"][:, :D_FF], mode="nt", name="d_h2_gate")
    dh2 = _matmul(du_v, W["w_ffn_up"][:, D_FF:], mode="nt", add=dh2, name="d_h2_value")
    G["w_ffn_up"] = jnp.concatenate([_matmul(h2, du_g, mode="tn", tn=1408, tk=1024, name="dw_ffn_up_gate"),
                                     _matmul(h2, du_v, mode="tn", tn=1408, tk=1024, name="dw_ffn_up_value")], axis=1)
    dx1, do_proj, G["g_pre_ffn"], G["g_post_mix"] = _norm2_bwd(dx2, dh2, x1, o_proj, W["g_pre_ffn"], W["g_post_mix"])
    dmerged = _matmul(do_proj, W["w_o"], mode="nt", name="d_merged")
    G["w_o"] = _matmul(merged, do_proj, mode="tn", tk=1024, name="dw_o")
    db_mla, db_dil, db_mem, dproj_g, G["b_gate"] = _merge_bwd(dmerged, proj_g, W["b_gate"], [b_mla, b_dil, b_mem])

    dy_mem = _matmul(db_mem, W["w_br_mem"], mode="nt", name="d_y_mem")
    G["w_br_mem"] = _matmul(o_mem, db_mem, mode="tn", tk=1024, name="dw_br_mem")
    delta_mem = _delta(dy_mem, o_mem, MEM_HEADS, "delta_mem")
    dq_mem, dkv_mem = _mem_bwd(proj_m, kv_mem, dy_mem, lse_mem, delta_mem, scale=att_scale, name="attn_mem_bwd")
    G["w_mem_kv"] = _matmul(mem_n, dkv_mem, mode="tn", name="dw_mem_kv")
    dmem_n = _matmul(dkv_mem, W["w_mem_kv"], mode="nt", name="d_mem_n")
    G["g_mem"] = _gain_grad(dmem_n, mem, "dg_mem")

    dy_dil = _matmul(db_dil, W["w_br_dil"], mode="nt", name="d_y_dil")
    G["w_br_dil"] = _matmul(y_dil, db_dil, mode="tn", tk=1024, name="dw_br_dil")
    mix = _dil_mix_bwd(dy_dil, o_dil, lse_dil)
    do_dil, dl_dil = mix[:3], mix[3:]
    dh_views, dw_groups = [], []
    for g, d in enumerate(DILATIONS):
        pv = proj_dg[g]
        parts = _band_bwd(pv, pv, pv, do_dil[g], lse_dil[g], dl_dil[g], slopes[g], n_heads=d * DIL_HPG, qcol=col(0),
                          kcol=col(1), vcol=col(2), L=S // d, scale=att_scale, slope_mul=float(d), d_shape=(S // d, d * DIL_W),
                          name=f"attn_dil{g}_bwd")
        dpv = jnp.concatenate([p[:, r * DIL_W:(r + 1) * DIL_W] for r in range(d) for p in parts], axis=1)
        dh_views.append(_dh_view(dpv, w_dg[g], d, None, f"d_h_d{g}"))
        dw_groups.append(_dw_view(h_views[g], dpv, d, f"dw_in_d{g}"))
    dw_d = jnp.concatenate([dw[:, part * DIL_W:(part + 1) * DIL_W] for part in range(3) for dw in dw_groups], axis=1)

    dy_mla = _matmul(db_mla, w_br_mla_p, mode="nt", name="d_y_mla")
    dw_br_mla_p = _matmul(o_mla, db_mla, mode="tn", tk=1024, name="dw_br_mla")
    G["w_br_mla"] = _unpad_heads(dw_br_mla_p, MLA_HEADS, MLA_NOPE, 0)
    delta_mla = _delta(dy_mla, o_mla, MLA_HEADS, "delta_mla")
    dq_mla, dk_mla, dv_mla = _causal_bwd(q_mla, k_mla, v_mla, dy_mla, lse_mla, delta_mla, name="attn_mla_bwd")
    dq_pre, dkpe = _mla_bwd_prep(dq_mla, dk_mla, pos, invf)
    dcq_n = _matmul(dq_pre, w_uq_p, mode="nt", tn=Q_RANK, name="d_cq")
    G["w_uq"] = _unpad_heads(_matmul(cq_n, dq_pre, mode="tn", tm=Q_RANK, tk=1024, name="dw_uq"), MLA_HEADS, MLA_QK, 1)
    dckv_a = _matmul(dk_mla, w_uk_p, mode="nt", tn=KV_RANK, name="d_ckv_k")
    dckv_b = _matmul(dv_mla, w_uv_p, mode="nt", tn=KV_RANK, name="d_ckv_v")
    dw_uk = _unpad_heads(_matmul(ckv_n, dk_mla, mode="tn", tm=KV_RANK, tk=1024, name="dw_uk"), MLA_HEADS, MLA_NOPE, 1)
    dw_uv = _unpad_heads(_matmul(ckv_n, dv_mla, mode="tn", tm=KV_RANK, tk=1024, name="dw_uv"), MLA_HEADS, MLA_NOPE, 1)
    G["w_ukv"] = jnp.concatenate([dw_uk.reshape(KV_RANK, MLA_HEADS, MLA_NOPE), dw_uv.reshape(KV_RANK, MLA_HEADS, MLA_NOPE)],
                                 axis=2).reshape(KV_RANK, -1)
    dproj_a, G["mla_q_norm"], G["mla_kv_norm"] = _mla_norm_bwd(dcq_n, dckv_a, dckv_b, dkpe, proj_a, W["mla_q_norm"],
                                                              W["mla_kv_norm"])

    dh = _matmul(dproj_a, w_a, mode="nt", name="d_h_a")
    dh = _matmul(dq_mem, w_m, mode="nt", add=dh, name="d_h_m")
    dh = _matmul(dproj_g, w_g, mode="nt", add=dh, name="d_h_g")
    dw_a = _matmul(h, dproj_a, mode="tn", tn=768, tk=1024, name="dw_in_a")
    dw_m = _matmul(h, dq_mem, mode="tn", tn=512, tk=1024, name="dw_in_m")
    dw_g = _matmul(h, dproj_g, mode="tn", tn=1536, tk=1024, name="dw_in_g")
    kr0 = Q_RANK + MLA_NOPE
    G["w_in"] = jnp.concatenate([dw_a[:, :Q_RANK], dw_a[:, Q_RANK + LANES:], dw_a[:, kr0:kr0 + MLA_ROPE], dw_d, dw_m, dw_g], axis=1)
    grad_x, G["g_pre_mix"] = _norm1_bwd(dx1, [dh] + dh_views, x, W["g_pre_mix"])
    return loss, grad_x, G


WEIGHTS = ["g_pre_mix", "w_in", "b_gate", "mla_q_norm", "w_uq", "mla_kv_norm", "w_ukv", "g_mem", "w_mem_kv", "w_br_mla",
           "w_br_dil", "w_br_mem", "w_o", "g_post_mix", "g_pre_ffn", "w_ffn_up", "conv_w", "conv_b", "w_ffn_down", "g_post_ffn"]
GROUPS = [
    [("w_in", (D_MODEL, D_IN), 1)],
    [("w_uq", (Q_RANK, MLA_HEADS * MLA_QK), 1)],
    [("w_ukv", (KV_RANK, MLA_HEADS * 2 * MLA_NOPE), 1)],
    [("w_br_mla", (MLA_HEADS * MLA_NOPE, D_MODEL), 1), ("w_br_dil", (DIL_W, D_MODEL), 1), ("w_br_mem", (MEM_W, D_MODEL), 1)],
    [("w_mem_kv", (D_MODEL, 2 * MEM_W), 0), ("w_o", (D_MODEL, D_MODEL), 0), ("w_ffn_down", (D_FF, D_MODEL), 0)],
    [("w_ffn_up", (D_MODEL, 2 * D_FF), 1)],
]
EARLY_GROUPS = (3, 4, 5)
LATE_GROUPS = (0, 1, 2)
CONV_W = ("conv_w", (3, 2 * D_FF), 1)
REPLICATED = [("g_pre_mix", D_MODEL), ("b_gate", 3 * D_MODEL), ("mla_q_norm", Q_RANK), ("mla_kv_norm", KV_RANK), ("g_mem", D_MODEL),
              ("g_post_mix", D_MODEL), ("g_pre_ffn", D_MODEL), ("conv_b", 2 * D_FF), ("g_post_ffn", D_MODEL)]
SMALL_ROWS = 256
CONV_AT = sum(n for _, n in REPLICATED)
LOSS_AT = CONV_AT + 3 * 2 * D_FF


def _shard_shape(shape, axis):
    return tuple(d // N_CHIPS if a == axis else d for a, d in enumerate(shape))


def _group_shape(grp):
    shapes = [_shard_shape(shape, axis) for _, shape, axis in grp]
    assert len({s[1] for s in shapes}) == 1
    return sum(s[0] for s in shapes), shapes[0][1]


def _member_shards(a, axis):
    r, c = a.shape
    if axis == 0:
        return a.reshape(N_CHIPS, r // N_CHIPS, c)
    return a.reshape(r, N_CHIPS, c // N_CHIPS).transpose(1, 0, 2)


def _member_full(s, axis):
    n, r, c = s.shape
    if axis == 0:
        return s.reshape(n * r, c)
    return s.transpose(1, 0, 2).reshape(r, n * c)


def _my_weight_groups(w):
    out = []
    for grp in GROUPS:
        rows, width = _group_shape(grp)
        out.append(jnp.concatenate([w[name].astype(BF16) for name, _, _ in grp], axis=0).reshape(2, rows // 2, width))
    return out


def _full_weights(gathered, conv_all):
    out = {}
    for grp, ga in zip(GROUPS, gathered):
        ga = ga.reshape(N_CHIPS, -1, ga.shape[-1])
        off = 0
        for name, shape, axis in grp:
            rows = _shard_shape(shape, axis)[0]
            out[name] = _member_full(ga[:, off:off + rows], axis)
            off += rows
    out[CONV_W[0]] = _member_full(conv_all, CONV_W[2])
    return out


def _grad_groups(G):
    out = []
    for grp in GROUPS:
        rows, width = _group_shape(grp)
        a = jnp.concatenate([_member_shards(G[name], axis) for name, _, axis in grp], axis=1)
        out.append(a.reshape(N_CHIPS, 2, rows // 2, width).transpose(1, 0, 2, 3))
    return out


def _pack_small(vals, conv_g=None, loss=None):
    parts = [vals[name].reshape(-1) for name, _ in REPLICATED]
    if conv_g is not None:
        parts += [conv_g.reshape(-1), loss.reshape(1)]
    flat = jnp.concatenate(parts)
    return jnp.pad(flat, (0, SMALL_ROWS * LANES - flat.shape[0])).reshape(SMALL_ROWS, LANES)


def _unpack_small(packed):
    flat = packed.reshape(-1)
    out, off = {}, 0
    for name, n in REPLICATED:
        out[name] = flat[off:off + n].reshape(1, n)
        off += n
    return out


MESH = pl.DeviceIdType.MESH
HBM_SPEC = pl.BlockSpec(memory_space=pltpu.HBM)


def _place():
    x, y, c = lax.axis_index("x"), lax.axis_index("y"), lax.axis_index("c")
    chips = [(1 - x, y), (x, 1 - y), (1 - x, 1 - y)]
    return x, y, c, chips


def _remote(src, dst, send_sems, recv_sems, k, to):
    return pltpu.make_async_remote_copy(src_ref=src, dst_ref=dst, send_sem=send_sems.at[k], recv_sem=recv_sems.at[k],
                                        device_id=to, device_id_type=MESH)


def _gather_weights(groups, wholes, name, collective_id):
    n, m = len(groups), len(wholes)
    arrays = list(groups) + list(wholes)
    hbm = pltpu.MemorySpace.HBM
    srcs = [jax.new_ref(a, memory_space=hbm) for a in arrays]
    outs = [jax.empty_ref(jax.ShapeDtypeStruct((N_CHIPS,) + a.shape, a.dtype), memory_space=hbm) for a in arrays]
    n_sem = 6 * n + 3 * m

    def launch(send_sems, recv_sems):
        x, y, c, chips = _place()
        me = 2 * x + y
        sibling = (x, y, 1 - c)
        barrier = pltpu.get_barrier_semaphore()
        peers = [(px, py, c) for px, py in chips] + [sibling]
        for peer in peers:
            pl.semaphore_signal(barrier, inc=1, device_id=peer, device_id_type=MESH)
        pl.semaphore_wait(barrier, len(peers))
        first = [_remote(srcs[g].at[c], outs[g].at[me, c], send_sems, recv_sems, g * 3 + k, (px, py, c))
                 for k, (px, py) in enumerate(chips) for g in range(n)]
        first += [_remote(srcs[n + w], outs[n + w].at[me], send_sems, recv_sems, 6 * n + w * 3 + k, (px, py, c))
                  for k, (px, py) in enumerate(chips) for w in range(m)]
        for cp in first:
            cp.start()
        passed = []
        for k, (px, py) in enumerate(chips):
            for g in range(n):
                slot = outs[g].at[2 * px + py, c]
                _remote(slot, slot, send_sems, recv_sems, g * 3 + k, (px, py, c)).wait_recv()
                cp = _remote(slot, slot, send_sems, recv_sems, 3 * n + g * 3 + k, sibling)
                cp.start()
                passed.append(cp)
        for k, (px, py) in enumerate(chips):
            for w in range(m):
                slot = outs[n + w].at[2 * px + py]
                _remote(slot, slot, send_sems, recv_sems, 6 * n + w * 3 + k, (px, py, c)).wait_recv()
            for g in range(n):
                slot = outs[g].at[2 * px + py, 1 - c]
                _remote(slot, slot, send_sems, recv_sems, 3 * n + g * 3 + k, sibling).wait_recv()
        for cp in first + passed:
            cp.wait_send()

    pl.kernel(launch, mesh=plsc.ScalarSubcoreMesh(axis_name="sequencer", num_cores=1), name=name,
              scratch_types=(pltpu.SemaphoreType.DMA((n_sem,)), pltpu.SemaphoreType.DMA((n_sem,))),
              compiler_params=pltpu.CompilerParams(collective_id=collective_id))()
    me = 2 * lax.axis_index("x") + lax.axis_index("y")
    res = [lax.dynamic_update_slice_in_dim(o[...], a[None], me, axis=0) for o, a in zip(outs, arrays)]
    return res[:n], res[n:]


def _on_sequencer(exchange, arrays, out_structs, n_sem, peers, name, collective_id):
    hbm = pltpu.MemorySpace.HBM
    srcs = [jax.new_ref(a, memory_space=hbm) for a in arrays]
    outs = [jax.empty_ref(s, memory_space=hbm) for s in out_structs]

    def launch(send_sems, recv_sems):
        x, y, c, chips = _place()
        barrier = pltpu.get_barrier_semaphore()
        them = peers(x, y, c, chips)
        for peer in them:
            pl.semaphore_signal(barrier, inc=1, device_id=peer, device_id_type=MESH)
        pl.semaphore_wait(barrier, len(them))
        exchange(srcs, outs, send_sems, recv_sems)

    pl.kernel(launch, mesh=plsc.ScalarSubcoreMesh(axis_name="sequencer", num_cores=1), name=name,
              scratch_types=(pltpu.SemaphoreType.DMA((n_sem,)), pltpu.SemaphoreType.DMA((n_sem,))),
              compiler_params=pltpu.CompilerParams(collective_id=collective_id))()
    return [o[...] for o in outs]


def _sibling_only(x, y, c, chips):
    return [(x, y, 1 - c)]


def _swap_halves(groups, name, collective_id):
    n = len(groups)

    def exchange(srcs, outs, send_sems, recv_sems):
        x, y, c, _ = _place()
        cps = [_remote(srcs[g].at[1 - c], outs[g], send_sems, recv_sems, g, (x, y, 1 - c)) for g in range(n)]
        for cp in cps:
            cp.start()
        for cp in cps:
            cp.wait()

    return _on_sequencer(exchange, groups, [jax.ShapeDtypeStruct(g.shape[1:], g.dtype) for g in groups], n, _sibling_only,
                         name, collective_id)


def _scatter_partials(groups, name, collective_id):
    n = len(groups)

    def exchange(srcs, outs, send_sems, recv_sems):
        x, y, c, chips = _place()
        sends = [_remote(srcs[g].at[2 * px + py], outs[g].at[k], send_sems, recv_sems, g * 3 + k, (px, py, c))
                 for k, (px, py) in enumerate(chips) for g in range(n)]
        for cp in sends:
            cp.start()
        for k, (px, py) in enumerate(chips):
            for g in range(n):
                slot = outs[g].at[k]
                _remote(slot, slot, send_sems, recv_sems, g * 3 + k, (px, py, c)).wait_recv()
        for cp in sends:
            cp.wait_send()

    return _on_sequencer(exchange, groups, [jax.ShapeDtypeStruct((3,) + g.shape[1:], g.dtype) for g in groups], 3 * n,
                         lambda x, y, c, chips: [(px, py, c) for px, py in chips], name, collective_id)


def _share_reduced(groups, name, collective_id):
    n = len(groups)

    def exchange(srcs, outs, send_sems, recv_sems):
        x, y, c, _ = _place()
        cps = [_remote(srcs[g], outs[g], send_sems, recv_sems, g, (x, y, 1 - c)) for g in range(n)]
        for cp in cps:
            cp.start()
        for cp in cps:
            cp.wait()

    return _on_sequencer(exchange, groups, [jax.ShapeDtypeStruct(g.shape, g.dtype) for g in groups], n, _sibling_only,
                         name, collective_id)


def _allreduce_small(v):
    n_dev = 2 * N_CHIPS

    def body(src, out, buf, send_sems, recv_sems):
        x, y, c, _ = _place()
        me = 4 * x + 2 * y + c
        buf[me] = src[...]
        flips = [(k >> 2 & 1, k >> 1 & 1, k & 1) for k in range(1, n_dev)]
        sends = []
        for k, (fx, fy, fc) in enumerate(flips):
            to = ((1 - x) if fx else x, (1 - y) if fy else y, (1 - c) if fc else c)
            cp = _remote(src, buf.at[me], send_sems, recv_sems, k, to)
            cp.start()
            sends.append((cp, to))
        for k, (cp, to) in enumerate(sends):
            slot = buf.at[4 * to[0] + 2 * to[1] + to[2]]
            _remote(slot, slot, send_sems, recv_sems, k, to).wait_recv()
        for cp, _ in sends:
            cp.wait_send()
        acc = buf[0]
        for d in range(1, n_dev):
            acc = acc + buf[d]
        out[...] = acc

    vm = pl.BlockSpec(memory_space=pltpu.VMEM)
    return pl.pallas_call(
        body, name="comm_allreduce_small", in_specs=[vm], out_specs=vm, out_shape=jax.ShapeDtypeStruct(v.shape, F32),
        scratch_shapes=[pltpu.VMEM((n_dev,) + v.shape, F32), pltpu.SemaphoreType.DMA((n_dev - 1,)),
                        pltpu.SemaphoreType.DMA((n_dev - 1,))],
    )(v)


def _row_tile(rows, cap=320):
    return max(t for t in range(16, cap + 1, 16) if rows % t == 0)


def _add_sibling(mine, theirs, core, name):
    _, n, R, C = mine.shape
    t = _row_tile(R)

    def body(core_ref, a_ref, b_ref, o_ref, ob_ref):
        tot = a_ref[...] + b_ref[...]
        o_ref[...] = tot
        ob_ref[...] = tot.astype(BF16)

    sp = pl.BlockSpec((None, t, C), lambda k, i, core_ref: (k, i, 0))
    grid_spec = pltpu.PrefetchScalarGridSpec(
        num_scalar_prefetch=1, grid=(n, R // t),
        in_specs=[pl.BlockSpec((None, None, t, C), lambda k, i, core_ref: (core_ref[0], k, i, 0)), sp], out_specs=[sp, sp])
    return pl.pallas_call(body, name=name, grid_spec=grid_spec,
                          out_shape=[jax.ShapeDtypeStruct((n, R, C), F32), jax.ShapeDtypeStruct((n, R, C), BF16)],
                          compiler_params=_params("parallel", "parallel"))(core, mine, theirs)


def _add_chips(received, own, chip, name):
    n, R, C = received.shape
    t = _row_tile(R)

    def body(chip_ref, r_ref, o_ref, out_ref):
        acc = o_ref[...]
        for k in range(n):
            acc = acc + r_ref[k].astype(F32)
        out_ref[...] = acc

    grid_spec = pltpu.PrefetchScalarGridSpec(
        num_scalar_prefetch=1, grid=(R // t,),
        in_specs=[pl.BlockSpec((n, t, C), lambda i, chip_ref: (0, i, 0)), pl.BlockSpec((None, t, C), lambda i, chip_ref: (chip_ref[0], i, 0))],
        out_specs=pl.BlockSpec((t, C), lambda i, chip_ref: (i, 0)))
    return pl.pallas_call(body, name=name, grid_spec=grid_spec, out_shape=jax.ShapeDtypeStruct((R, C), F32),
                          compiler_params=_params("parallel"))(chip, received, own)


def _adamw(w, g, m, v, name, g_row0=0):
    R, C = w.shape
    t = math.gcd(math.gcd(R, g_row0), 128) if R % 8 == 0 else R
    assert g_row0 % t == 0
    c1 = 1.0 - ADAM_B1 ** ADAM_STEP
    c2 = 1.0 - ADAM_B2 ** ADAM_STEP

    def body(w_ref, g_ref, m_ref, v_ref, go_ref, d_ref, nm_ref, nv_ref):
        gv = g_ref[...]
        nm = ADAM_B1 * m_ref[...] + (1.0 - ADAM_B1) * gv
        nv = ADAM_B2 * v_ref[...] + (1.0 - ADAM_B2) * (gv * gv)
        go_ref[...] = gv
        d_ref[...] = -ADAM_LR * ((nm / c1) / (jnp.sqrt(nv / c2) + ADAM_EPS) + ADAM_WD * w_ref[...])
        nm_ref[...] = nm
        nv_ref[...] = nv

    sp = pl.BlockSpec((t, C), lambda i: (i, 0))
    g_sp = pl.BlockSpec((t, C), lambda i: (i + g_row0 // t, 0))
    return pl.pallas_call(body, name=name, grid=(R // t,), in_specs=[sp, g_sp, sp, sp], out_specs=[sp] * 4,
                          out_shape=[jax.ShapeDtypeStruct((R, C), F32)] * 4, compiler_params=_params("parallel"))(w, g, m, v)


def kernel(x, mem, positions, g_pre_mix, w_in, b_gate, mla_q_norm, w_uq, mla_kv_norm, w_ukv, g_mem, w_mem_kv, w_br_mla, w_br_dil, w_br_mem, w_o, g_post_mix, g_pre_ffn, w_ffn_up, conv_w, conv_b, w_ffn_down, g_post_ffn, loss_target, m_g_pre_mix, m_w_in, m_b_gate, m_mla_q_norm, m_w_uq, m_mla_kv_norm, m_w_ukv, m_g_mem, m_w_mem_kv, m_w_br_mla, m_w_br_dil, m_w_br_mem, m_w_o, m_g_post_mix, m_g_pre_ffn, m_w_ffn_up, m_conv_w, m_conv_b, m_w_ffn_down, m_g_post_ffn, v_g_pre_mix, v_w_in, v_b_gate, v_mla_q_norm, v_w_uq, v_mla_kv_norm, v_w_ukv, v_g_mem, v_w_mem_kv, v_w_br_mla, v_w_br_dil, v_w_br_mem, v_w_o, v_g_post_mix, v_g_pre_ffn, v_w_ffn_up, v_conv_w, v_conv_b, v_w_ffn_down, v_g_post_ffn):
    w_args = (g_pre_mix, w_in, b_gate, mla_q_norm, w_uq, mla_kv_norm, w_ukv, g_mem, w_mem_kv, w_br_mla, w_br_dil, w_br_mem, w_o,
              g_post_mix, g_pre_ffn, w_ffn_up, conv_w, conv_b, w_ffn_down, g_post_ffn)
    m_args = (m_g_pre_mix, m_w_in, m_b_gate, m_mla_q_norm, m_w_uq, m_mla_kv_norm, m_w_ukv, m_g_mem, m_w_mem_kv, m_w_br_mla,
              m_w_br_dil, m_w_br_mem, m_w_o, m_g_post_mix, m_g_pre_ffn, m_w_ffn_up, m_conv_w, m_conv_b, m_w_ffn_down, m_g_post_ffn)
    v_args = (v_g_pre_mix, v_w_in, v_b_gate, v_mla_q_norm, v_w_uq, v_mla_kv_norm, v_w_ukv, v_g_mem, v_w_mem_kv, v_w_br_mla,
              v_w_br_dil, v_w_br_mem, v_w_o, v_g_post_mix, v_g_pre_ffn, v_w_ffn_up, v_conv_w, v_conv_b, v_w_ffn_down, v_g_post_ffn)
    sharded = {name for grp in GROUPS for name, _, _ in grp} | {CONV_W[0]}

    def local(a, name):
        return a[0] if name in sharded else a

    w = {n: local(a, n) for n, a in zip(WEIGHTS, w_args)}
    m = {n: local(a, n) for n, a in zip(WEIGHTS, m_args)}
    v = {n: local(a, n) for n, a in zip(WEIGHTS, v_args)}

    mine = _my_weight_groups(w)
    head, _ = _gather_weights(mine[:1], [], "comm_gather_w_in", 0)
    rest, (conv_all,) = _gather_weights(mine[1:], [w[CONV_W[0]]], "comm_gather_rest", 1)
    full = _full_weights(head + rest, conv_all)
    full.update({name: w[name] for name, _ in REPLICATED})

    loss_local, grad_x, G = _local_step(x[0], mem[0], positions, loss_target[0], full)

    core = lax.axis_index("c").astype(jnp.int32).reshape(1)
    chip = (2 * lax.axis_index("x") + lax.axis_index("y")).astype(jnp.int32).reshape(1)
    first = lax.axis_index("c") == 0
    all_mine = _grad_groups(G)
    shard_groups = [None] * len(GROUPS)
    for tag, which, cid in (("early", EARLY_GROUPS, 2), ("late", LATE_GROUPS, 5)):
        mine = [all_mine[i] for i in which]
        theirs = _swap_halves(mine, f"comm_swap_{tag}", cid)
        partial = [_add_sibling(a, b, core, f"add_sibling_{i}") for i, a, b in zip(which, mine, theirs)]
        received = _scatter_partials([p16 for _, p16 in partial], f"comm_scatter_{tag}", cid + 1)
        reduced = [_add_chips(r, p32, chip, f"add_chips_{i}") for i, r, (p32, _) in zip(which, received, partial)]
        for i, a, b in zip(which, reduced, _share_reduced(reduced, f"comm_share_{tag}", cid + 2)):
            shard_groups[i] = jnp.concatenate([jnp.where(first, a, b), jnp.where(first, b, a)], axis=0)
    small = _allreduce_small(_pack_small(G, G[CONV_W[0]], loss_local))
    flat = small.reshape(-1)
    loss = flat[LOSS_AT]
    conv_g = flat[CONV_AT:LOSS_AT].reshape(CONV_W[1])
    conv_cols = CONV_W[1][1] // N_CHIPS
    conv_g = lax.dynamic_slice_in_dim(conv_g, (2 * lax.axis_index("x") + lax.axis_index("y")) * conv_cols, conv_cols, axis=1)

    grads, deltas, new_m, new_v = {}, {}, {}, {}
    for grp, g_all in zip(GROUPS, shard_groups):
        off = 0
        for name, _, _ in grp:
            grads[name], deltas[name], new_m[name], new_v[name] = _adamw(w[name], g_all, m[name], v[name], "adamw_" + name, off)
            off += w[name].shape[0]
    name = CONV_W[0]
    grads[name], deltas[name], new_m[name], new_v[name] = _adamw(w[name], conv_g, m[name], v[name], "adamw_" + name)
    packed = _adamw(_pack_small(w), small, _pack_small(m), _pack_small(v), "adamw_small")
    for dst, packed_small in zip((grads, deltas, new_m, new_v), packed):
        dst.update(_unpack_small(packed_small))

    def out(d, name):
        return d[name][None] if name in sharded else d[name]

    return (loss, grad_x[None], *[out(grads, n) for n in WEIGHTS], *[out(deltas, n) for n in WEIGHTS],
            *[out(new_m, n) for n in WEIGHTS], *[out(new_v, n) for n in WEIGHTS])
```
